```python
import math
import jax, jax.numpy as jnp
from jax import lax
import numpy as np

D_MODEL = 1024
BATCH = 8
SEQ = 8192
DEPTH = 2

N_META = 16
D_FF = 2816
NORM_EPS = 1e-6

RET_HEADS = 4
RET_DK = 128
RET_DV = 128
RET_CHUNK = 128
GDN_HEADS = 4
GDN_DK = 128
GDN_DV = 128
GDN_CONV = 4
GDN_CHUNK = 64
GDN_QKV = 2 * GDN_HEADS * GDN_DK + GDN_HEADS * GDN_DV
SWA_HEADS = 8
SWA_KV_HEADS = 2
SWA_DH = 64
SWA_WINDOW = 128
SB_HEADS = 8
SB_DH = 64
ATT_BLOCK = 128

AB_WIDTHS = (RET_HEADS * RET_DK, RET_HEADS * RET_DK, RET_HEADS * RET_DV, RET_HEADS * RET_DV,
             GDN_QKV, GDN_HEADS * GDN_DV, GDN_HEADS, GDN_HEADS)
AB_IN = 2 * RET_HEADS * RET_DK + 2 * RET_HEADS * RET_DV + GDN_QKV + GDN_HEADS * GDN_DV + 2 * GDN_HEADS
AB_OUT = RET_HEADS * RET_DV + GDN_HEADS * GDN_DV
CD_WIDTHS = (SWA_HEADS * SWA_DH, SWA_KV_HEADS * SWA_DH, SWA_KV_HEADS * SWA_DH,
             SB_HEADS * SB_DH, SB_HEADS * SB_DH, SB_HEADS * SB_DH)
CD_IN = SWA_HEADS * SWA_DH + 2 * SWA_KV_HEADS * SWA_DH + 3 * SB_HEADS * SB_DH
CD_OUT = SWA_HEADS * SWA_DH + SB_HEADS * SB_DH

kernel_name = "hybrid_retention_gdn_swa_stickbreak_macaron"


def rms_norm(x, g):
    xf = x.astype(jnp.float32)
    y = xf * lax.rsqrt(jnp.mean(xf * xf, axis=-1, keepdims=True) + NORM_EPS)
    return (y * g.astype(jnp.float32)).astype(x.dtype)


def l2_normalize(x):
    return x * lax.rsqrt(jnp.sum(x * x, axis=-1, keepdims=True) + NORM_EPS)


def swiglu(x, w_gate, w_up, w_down):
    return (jax.nn.silu(x @ w_gate) * (x @ w_up)) @ w_down


def split_cols(t, widths):
    return jnp.split(t, np.cumsum(widths)[:-1].tolist(), axis=-1)


def pad_front(t, n):
    return jnp.pad(t, [(0, 0), (n, 0)] + [(0, 0)] * (t.ndim - 2))


def rotate_pairs(x, pos):
    d = x.shape[-1]
    inv_freq = 1.0 / (10000.0 ** jnp.linspace(0.0, 1.0, d // 2, dtype=jnp.float32))
    ang = pos[:, None] * inv_freq[None, :]
    cos = jnp.cos(ang)[None, :, None, :]
    sin = jnp.sin(ang)[None, :, None, :]
    x1, x2 = x[..., 0::2], x[..., 1::2]
    return jnp.stack([x1 * cos - x2 * sin, x2 * cos + x1 * sin], axis=-1).reshape(x.shape)


def causal_depthwise_conv(x, w):
    K, ch = w.shape
    return lax.conv_general_dilated(x, w[:, None, :], window_strides=(1,), padding=[(K - 1, 0)],
                                    dimension_numbers=('NWC', 'WIO', 'NWC'), feature_group_count=ch)


def retention_chunked(q, k, v):
    B, Lp, H, dk = q.shape
    dv = v.shape[-1]
    C = RET_CHUNK
    N = Lp // C
    q = q.reshape(B, N, C, H, dk)
    k = k.reshape(B, N, C, H, dk)
    v = v.reshape(B, N, C, H, dv)
    log_gamma = jnp.log1p(-jnp.exp2(-5.0 - jnp.arange(H, dtype=jnp.float32)))
    idx = jnp.arange(C, dtype=jnp.float32)
    diff = idx[:, None] - idx[None, :]
    intra_decay = jnp.where(diff >= 0, jnp.exp(jnp.maximum(diff, 0.0) * log_gamma[:, None, None]), 0.0)
    scores = jnp.einsum('bnihd,bnjhd->bnhij', q, k) * intra_decay
    intra = jnp.einsum('bnhij,bnjhe->bnihe', scores, v)
    zeta = jnp.exp((C - 1.0 - idx)[:, None] * log_gamma[None, :])
    xi = jnp.exp((idx + 1.0)[:, None] * log_gamma[None, :])
    kv = jnp.einsum('bnjhd,bnjhe->bnhde', k * zeta[..., None], v)
    chunk_decay = jnp.exp(C * log_gamma)[:, None, None]

    def step(S, kv_n):
        return S * chunk_decay + kv_n, S

    _, s_prev = lax.scan(step, jnp.zeros_like(kv[:, 0]), jnp.moveaxis(kv, 1, 0))
    s_prev = jnp.moveaxis(s_prev, 0, 1)
    cross = jnp.einsum('bnihd,bnhde->bnihe', q * xi[..., None], s_prev)
    return (intra + cross).reshape(B, Lp, H, dv)


def gated_delta_chunked(q, k, v, g, beta):
    B, Lp, H, dk = q.shape
    dv = v.shape[-1]
    C = GDN_CHUNK
    N = Lp // C

    def chunks(t):
        return jnp.moveaxis(t.reshape((B, N, C, H) + t.shape[3:]), 2, 3)

    q = chunks(q) * dk ** -0.5
    k = chunks(k)
    v = chunks(v)
    g = chunks(g)
    beta = chunks(beta)
    g_cum = jnp.cumsum(g, axis=-1)
    idx = jnp.arange(C)
    incl = idx[:, None] >= idx[None, :]
    strict = idx[:, None] > idx[None, :]
    gdiff = g_cum[..., :, None] - g_cum[..., None, :]
    decay = jnp.where(incl, jnp.exp(jnp.where(incl, gdiff, 0.0)), 0.0)
    k_beta = k * beta[..., None]
    a_mat = jnp.where(strict, jnp.einsum('bnhid,bnhjd->bnhij', k_beta, k) * decay, 0.0)
    system = a_mat + jnp.eye(C, dtype=a_mat.dtype)
    rhs = jnp.concatenate([v * beta[..., None], k_beta * jnp.exp(g_cum)[..., None]], axis=-1)
    sol = lax.linalg.triangular_solve(system, rhs, left_side=True, lower=True, unit_diagonal=True)
    u_vec, w_vec = sol[..., :dv], sol[..., dv:]
    qk = jnp.where(incl, jnp.einsum('bnhid,bnhjd->bnhij', q, k) * decay, 0.0)
    q_dec = q * jnp.exp(g_cum)[..., None]
    g_last = g_cum[..., -1]
    k_tail = k * jnp.exp(g_last[..., None] - g_cum)[..., None]

    def step(S, xs):
        u_n, w_n, qk_n, qd_n, kt_n, gl_n = xs
        v_new = u_n - jnp.einsum('bhcd,bhde->bhce', w_n, S)
        o_n = jnp.einsum('bhcd,bhde->bhce', qd_n, S) + jnp.einsum('bhij,bhje->bhie', qk_n, v_new)
        S = S * jnp.exp(gl_n)[..., None, None] + jnp.einsum('bhcd,bhce->bhde', kt_n, v_new)
        return S, o_n

    xs = tuple(jnp.moveaxis(t, 1, 0) for t in (u_vec, w_vec, qk, q_dec, k_tail, g_last))
    _, o = lax.scan(step, jnp.zeros((B, H, dk, dv), q.dtype), xs)
    return jnp.moveaxis(o, 0, 1).transpose(0, 1, 3, 2, 4).reshape(B, Lp, H, dv)


def swa_sink_attention(q, k, v, sinks):
    B, Lp, HQ, dh = q.shape
    HKV = k.shape[2]
    G = HQ // HKV
    BLK = ATT_BLOCK
    NB = Lp // BLK
    pad = BLK - N_META
    qb = q.reshape(B, NB, BLK, HKV, G, dh) * dh ** -0.5
    kb = k.reshape(B, NB, BLK, HKV, dh)
    vb = v.reshape(B, NB, BLK, HKV, dh)
    shift = ((0, 0), (1, 0), (0, 0), (0, 0), (0, 0))
    k_prev = jnp.pad(kb, shift)[:, :-1]
    v_prev = jnp.pad(vb, shift)[:, :-1]
    meta_k = jnp.broadcast_to(k[:, None, pad:BLK], (B, NB, N_META, HKV, dh))
    meta_v = jnp.broadcast_to(v[:, None, pad:BLK], (B, NB, N_META, HKV, dh))
    keys = jnp.concatenate([meta_k, k_prev, kb], axis=2)
    vals = jnp.concatenate([meta_v, v_prev, vb], axis=2)
    blocks = jnp.arange(NB)
    qpos = blocks[:, None] * BLK + jnp.arange(BLK)[None, :]
    kpos = (blocks[:, None] - 1) * BLK + jnp.arange(2 * BLK)[None, :]
    dpos = qpos[:, :, None] - kpos[:, None, :]
    band = (dpos >= 0) & (dpos < SWA_WINDOW) & (kpos[:, None, :] >= BLK)
    meta_vis = (pad + jnp.arange(N_META))[None, None, :] <= qpos[:, :, None]
    mask = jnp.concatenate([jnp.broadcast_to(meta_vis, (NB, BLK, N_META)), band], axis=-1)
    s = jnp.einsum('bnqhgd,bnkhd->bnhgqk', qb, keys)
    s = jnp.where(mask[None, :, None, None], s, -jnp.inf)
    sink = jnp.broadcast_to(sinks.astype(s.dtype).reshape(1, 1, HKV, G, 1, 1), s.shape[:-1] + (1,))
    p = jax.nn.softmax(jnp.concatenate([s, sink], axis=-1), axis=-1)[..., :-1]
    o = jnp.einsum('bnhgqk,bnkhd->bnqhgd', p, vals)
    return o.reshape(B, Lp, HQ, dh)


def stick_breaking_attention(q, k, v):
    B, Lp, H, dh = q.shape
    BLK = ATT_BLOCK
    NB = Lp // BLK
    pad = BLK - N_META
    kpos = jnp.arange(Lp)
    qb = jnp.moveaxis(q.reshape(B, NB, BLK, H, dh), 1, 0)
    scale = dh ** -0.5

    def block(args):
        q_blk, n = args
        qpos = n * BLK + jnp.arange(BLK)
        valid = (kpos[None, :] < qpos[:, None]) & (kpos[None, :] >= pad)
        z = jnp.einsum('bqhd,bkhd->bhqk', q_blk, k) * scale
        log_beta = jax.nn.log_sigmoid(z)
        log_1m_beta = jnp.where(valid, jax.nn.log_sigmoid(-z), 0.0)
        log_stick = lax.cumsum(log_1m_beta, axis=3, reverse=True) - log_1m_beta
        a = jnp.where(valid, jnp.exp(log_beta + log_stick), 0.0)
        return jnp.einsum('bhqk,bkhd->bqhd', a, v)

    o = lax.map(block, (qb, jnp.arange(NB)))
    return jnp.moveaxis(o, 0, 1).reshape(B, Lp, H, dh)


def mixer_ab(u, w_in, conv_w, a_log, dt_bias, out_norm, w_out):
    B, L, _ = u.shape
    f32 = jnp.float32
    rq, rk, rv, rg, gqkv, gz, gb, ga = split_cols(u @ w_in, AB_WIDTHS)
    pos = jnp.arange(L, dtype=f32)
    rq = rotate_pairs(rq.astype(f32).reshape(B, L, RET_HEADS, RET_DK), pos)
    rk = rotate_pairs(rk.astype(f32).reshape(B, L, RET_HEADS, RET_DK), pos) * RET_DK ** -0.5
    rv = rv.astype(f32).reshape(B, L, RET_HEADS, RET_DV)
    rpad = RET_CHUNK - N_META
    ret = retention_chunked(pad_front(rq, rpad), pad_front(rk, rpad), pad_front(rv, rpad))[:, rpad:]
    mu = jnp.mean(ret, axis=-1, keepdims=True)
    var = jnp.mean(jnp.square(ret - mu), axis=-1, keepdims=True)
    ret = ((ret - mu) * lax.rsqrt(var + NORM_EPS)).reshape(B, L, -1) * jax.nn.silu(rg.astype(f32))
    qkv = jax.nn.silu(causal_depthwise_conv(gqkv.astype(f32), conv_w.astype(f32)))
    gq, gk, gv = split_cols(qkv, (GDN_HEADS * GDN_DK, GDN_HEADS * GDN_DK, GDN_HEADS * GDN_DV))
    gq = l2_normalize(gq.reshape(B, L, GDN_HEADS, GDN_DK))
    gk = l2_normalize(gk.reshape(B, L, GDN_HEADS, GDN_DK))
    gv = gv.reshape(B, L, GDN_HEADS, GDN_DV)
    beta = jax.nn.sigmoid(gb.astype(f32))
    g = -jnp.exp(a_log.astype(f32)) * jax.nn.softplus(ga.astype(f32) + dt_bias.astype(f32))
    gpad = GDN_CHUNK - N_META
    o = gated_delta_chunked(*(pad_front(t, gpad) for t in (gq, gk, gv, g, beta)))[:, gpad:]
    o = rms_norm(o, out_norm) * jax.nn.silu(gz.astype(f32).reshape(B, L, GDN_HEADS, GDN_DV))
    mixed = jnp.concatenate([ret, o.reshape(B, L, -1)], axis=-1).astype(u.dtype)
    return mixed @ w_out


def mixer_cd(u, w_in, sinks, w_out):
    B, L, _ = u.shape
    pad = ATT_BLOCK - N_META
    Lp = L + pad
    cq, ck, cv, sq, sk, sv = [pad_front(t.astype(jnp.float32), pad) for t in split_cols(u @ w_in, CD_WIDTHS)]
    swa = swa_sink_attention(cq.reshape(B, Lp, SWA_HEADS, SWA_DH), ck.reshape(B, Lp, SWA_KV_HEADS, SWA_DH),
                             cv.reshape(B, Lp, SWA_KV_HEADS, SWA_DH), sinks)
    sb = stick_breaking_attention(sq.reshape(B, Lp, SB_HEADS, SB_DH), sk.reshape(B, Lp, SB_HEADS, SB_DH),
                                  sv.reshape(B, Lp, SB_HEADS, SB_DH))
    mixed = jnp.concatenate([swa.reshape(B, Lp, -1), sb.reshape(B, Lp, -1)], axis=-1)[:, pad:]
    return mixed.astype(u.dtype) @ w_out


def _fwd_setup_inputs(seed: int = 0) -> dict:
    key = jax.random.key(seed)
    ks = jax.random.split(key, 16)
    f32 = jnp.float32
    n_even = (DEPTH + 1) // 2
    n_odd = DEPTH // 2

    def dense(k, shape, fan_in):
        return jax.random.normal(k, shape, f32) * fan_in ** -0.5

    x = jax.random.normal(ks[0], (BATCH, SEQ, D_MODEL), f32)
    meta_tokens = jax.random.normal(ks[1], (N_META, D_MODEL), f32)
    norm_gains = 1.0 + 0.01 * jax.random.normal(ks[2], (DEPTH, 6, D_MODEL), f32)
    ffn_w_gate = dense(ks[3], (DEPTH, 2, D_MODEL, D_FF), D_MODEL)
    ffn_w_up = dense(ks[4], (DEPTH, 2, D_MODEL, D_FF), D_MODEL)
    ffn_w_down = dense(ks[5], (DEPTH, 2, D_FF, D_MODEL), D_FF)
    ab_w_in = dense(ks[6], (n_even, D_MODEL, AB_IN), D_MODEL)
    ab_conv_w = dense(ks[7], (n_even, GDN_CONV, GDN_QKV), GDN_CONV)
    ab_a_log = jnp.log(jax.random.uniform(ks[8], (n_even, GDN_HEADS), f32, 1.0, 16.0))
    dt = jnp.exp(jax.random.uniform(ks[9], (n_even, GDN_HEADS), f32, math.log(1e-3), math.log(1e-1)))
    ab_dt_bias = dt + jnp.log(-jnp.expm1(-dt))
    ab_out_norm = 1.0 + 0.01 * jax.random.normal(ks[10], (n_even, GDN_DV), f32)
    ab_w_out = dense(ks[11], (n_even, AB_OUT, D_MODEL), AB_OUT)
    cd_w_in = dense(ks[12], (n_odd, D_MODEL, CD_IN), D_MODEL)
    cd_sinks = jax.random.normal(ks[13], (n_odd, SWA_HEADS), f32)
    cd_w_out = dense(ks[14], (n_odd, CD_OUT, D_MODEL), CD_OUT)
    return {"x": x, "meta_tokens": meta_tokens, "norm_gains": norm_gains,
            "ffn_w_gate": ffn_w_gate, "ffn_w_up": ffn_w_up, "ffn_w_down": ffn_w_down,
            "ab_w_in": ab_w_in, "ab_conv_w": ab_conv_w, "ab_a_log": ab_a_log, "ab_dt_bias": ab_dt_bias,
            "ab_out_norm": ab_out_norm, "ab_w_out": ab_w_out,
            "cd_w_in": cd_w_in, "cd_sinks": cd_sinks, "cd_w_out": cd_w_out}


def _fwd_reference(x, meta_tokens, norm_gains, ffn_w_gate, ffn_w_up, ffn_w_down,
              ab_w_in, ab_conv_w, ab_a_log, ab_dt_bias, ab_out_norm, ab_w_out,
              cd_w_in, cd_sinks, cd_w_out):
    B = x.shape[0]
    meta = jnp.broadcast_to(meta_tokens[None].astype(x.dtype), (B, N_META, D_MODEL))
    h = jnp.concatenate([meta, x], axis=1)
    for i in range(DEPTH):
        g = norm_gains[i]
        j = i // 2
        y = swiglu(rms_norm(h, g[0]), ffn_w_gate[i, 0], ffn_w_up[i, 0], ffn_w_down[i, 0])
        h = h + 0.5 * rms_norm(y, g[1])
        u = rms_norm(h, g[2])
        if i % 2 == 0:
            y = mixer_ab(u, ab_w_in[j], ab_conv_w[j], ab_a_log[j], ab_dt_bias[j], ab_out_norm[j], ab_w_out[j])
        else:
            y = mixer_cd(u, cd_w_in[j], cd_sinks[j], cd_w_out[j])
        h = h + rms_norm(y, g[3])
        y = swiglu(rms_norm(h, g[4]), ffn_w_gate[i, 1], ffn_w_up[i, 1], ffn_w_down[i, 1])
        h = h + 0.5 * rms_norm(y, g[5])
    return h[:, N_META:]


import jax as _jax
import jax.numpy as _jnp

TWIN_FORMAT = 'train_step'
FWD_PARAMS = ['x', 'meta_tokens', 'norm_gains', 'ffn_w_gate', 'ffn_w_up', 'ffn_w_down', 'ab_w_in', 'ab_conv_w', 'ab_a_log', 'ab_dt_bias', 'ab_out_norm', 'ab_w_out', 'cd_w_in', 'cd_sinks', 'cd_w_out']
TWIN_WEIGHTS = ['meta_tokens', 'norm_gains', 'ffn_w_gate', 'ffn_w_up', 'ffn_w_down', 'ab_w_in', 'ab_conv_w', 'ab_a_log', 'ab_dt_bias', 'ab_out_norm', 'ab_w_out', 'cd_w_in', 'cd_sinks', 'cd_w_out']
TWIN_DIFF_INPUT = 'x'
TWIN_INPUTS = ['x', 'meta_tokens', 'norm_gains', 'ffn_w_gate', 'ffn_w_up', 'ffn_w_down', 'ab_w_in', 'ab_conv_w', 'ab_a_log', 'ab_dt_bias', 'ab_out_norm', 'ab_w_out', 'cd_w_in', 'cd_sinks', 'cd_w_out', 'loss_target', 'm_meta_tokens', 'm_norm_gains', 'm_ffn_w_gate', 'm_ffn_w_up', 'm_ffn_w_down', 'm_ab_w_in', 'm_ab_conv_w', 'm_ab_a_log', 'm_ab_dt_bias', 'm_ab_out_norm', 'm_ab_w_out', 'm_cd_w_in', 'm_cd_sinks', 'm_cd_w_out', 'v_meta_tokens', 'v_norm_gains', 'v_ffn_w_gate', 'v_ffn_w_up', 'v_ffn_w_down', 'v_ab_w_in', 'v_ab_conv_w', 'v_ab_a_log', 'v_ab_dt_bias', 'v_ab_out_norm', 'v_ab_w_out', 'v_cd_w_in', 'v_cd_sinks', 'v_cd_w_out']
TWIN_OUTPUTS = ['loss', 'grad_x', 'grad_meta_tokens', 'grad_norm_gains', 'grad_ffn_w_gate', 'grad_ffn_w_up', 'grad_ffn_w_down', 'grad_ab_w_in', 'grad_ab_conv_w', 'grad_ab_a_log', 'grad_ab_dt_bias', 'grad_ab_out_norm', 'grad_ab_w_out', 'grad_cd_w_in', 'grad_cd_sinks', 'grad_cd_w_out', 'delta_meta_tokens', 'delta_norm_gains', 'delta_ffn_w_gate', 'delta_ffn_w_up', 'delta_ffn_w_down', 'delta_ab_w_in', 'delta_ab_conv_w', 'delta_ab_a_log', 'delta_ab_dt_bias', 'delta_ab_out_norm', 'delta_ab_w_out', 'delta_cd_w_in', 'delta_cd_sinks', 'delta_cd_w_out', 'new_m_meta_tokens', 'new_m_norm_gains', 'new_m_ffn_w_gate', 'new_m_ffn_w_up', 'new_m_ffn_w_down', 'new_m_ab_w_in', 'new_m_ab_conv_w', 'new_m_ab_a_log', 'new_m_ab_dt_bias', 'new_m_ab_out_norm', 'new_m_ab_w_out', 'new_m_cd_w_in', 'new_m_cd_sinks', 'new_m_cd_w_out', 'new_v_meta_tokens', 'new_v_norm_gains', 'new_v_ffn_w_gate', 'new_v_ffn_w_up', 'new_v_ffn_w_down', 'new_v_ab_w_in', 'new_v_ab_conv_w', 'new_v_ab_a_log', 'new_v_ab_dt_bias', 'new_v_ab_out_norm', 'new_v_ab_w_out', 'new_v_cd_w_in', 'new_v_cd_sinks', 'new_v_cd_w_out']
TWIN_LEAF_KINDS = {'loss': 'loss', 'grad_x': 'grad_x', 'grad_meta_tokens': 'grad_w', 'grad_norm_gains': 'grad_w', 'grad_ffn_w_gate': 'grad_w', 'grad_ffn_w_up': 'grad_w', 'grad_ffn_w_down': 'grad_w', 'grad_ab_w_in': 'grad_w', 'grad_ab_conv_w': 'grad_w', 'grad_ab_a_log': 'grad_w', 'grad_ab_dt_bias': 'grad_w', 'grad_ab_out_norm': 'grad_w', 'grad_ab_w_out': 'grad_w', 'grad_cd_w_in': 'grad_w', 'grad_cd_sinks': 'grad_w', 'grad_cd_w_out': 'grad_w', 'delta_meta_tokens': 'delta_w', 'delta_norm_gains': 'delta_w', 'delta_ffn_w_gate': 'delta_w', 'delta_ffn_w_up': 'delta_w', 'delta_ffn_w_down': 'delta_w', 'delta_ab_w_in': 'delta_w', 'delta_ab_conv_w': 'delta_w', 'delta_ab_a_log': 'delta_w', 'delta_ab_dt_bias': 'delta_w', 'delta_ab_out_norm': 'delta_w', 'delta_ab_w_out': 'delta_w', 'delta_cd_w_in': 'delta_w', 'delta_cd_sinks': 'delta_w', 'delta_cd_w_out': 'delta_w', 'new_m_meta_tokens': 'new_m', 'new_m_norm_gains': 'new_m', 'new_m_ffn_w_gate': 'new_m', 'new_m_ffn_w_up': 'new_m', 'new_m_ffn_w_down': 'new_m', 'new_m_ab_w_in': 'new_m', 'new_m_ab_conv_w': 'new_m', 'new_m_ab_a_log': 'new_m', 'new_m_ab_dt_bias': 'new_m', 'new_m_ab_out_norm': 'new_m', 'new_m_ab_w_out': 'new_m', 'new_m_cd_w_in': 'new_m', 'new_m_cd_sinks': 'new_m', 'new_m_cd_w_out': 'new_m', 'new_v_meta_tokens': 'new_v', 'new_v_norm_gains': 'new_v', 'new_v_ffn_w_gate': 'new_v', 'new_v_ffn_w_up': 'new_v', 'new_v_ffn_w_down': 'new_v', 'new_v_ab_w_in': 'new_v', 'new_v_ab_conv_w': 'new_v', 'new_v_ab_a_log': 'new_v', 'new_v_ab_dt_bias': 'new_v', 'new_v_ab_out_norm': 'new_v', 'new_v_ab_w_out': 'new_v', 'new_v_cd_w_in': 'new_v', 'new_v_cd_sinks': 'new_v', 'new_v_cd_w_out': 'new_v'}


def _forward(args):
    return _fwd_reference(*[args[k] for k in FWD_PARAMS])


def _output_shape():
    def fwd():
        inp = _fwd_setup_inputs(0)
        return _fwd_reference(*[inp[k] for k in FWD_PARAMS])
    out = _jax.eval_shape(fwd)
    return out.shape, out.dtype

N_MICROBATCH = 1
ADAM_LR = 0.001
ADAM_B1 = 0.9
ADAM_B2 = 0.999
ADAM_EPS = 1e-08
ADAM_WD = 0.01
ADAM_STEP = 10
PER_EXAMPLE_BATCH_AXIS = {'x': 0, 'loss_target': 0}
SHARED_INPUTS = []
_WEIGHT_DTYPES = {'meta_tokens': _jnp.float32, 'norm_gains': _jnp.float32, 'ffn_w_gate': _jnp.float32, 'ffn_w_up': _jnp.float32, 'ffn_w_down': _jnp.float32, 'ab_w_in': _jnp.float32, 'ab_conv_w': _jnp.float32, 'ab_a_log': _jnp.float32, 'ab_dt_bias': _jnp.float32, 'ab_out_norm': _jnp.float32, 'ab_w_out': _jnp.float32, 'cd_w_in': _jnp.float32, 'cd_sinks': _jnp.float32, 'cd_w_out': _jnp.float32}
MOMENT_SCALE = {'meta_tokens': 1.206357e-01, 'norm_gains': 2.755201e+01, 'ffn_w_gate': 3.489586e-01, 'ffn_w_up': 4.216887e-01, 'ffn_w_down': 7.015839e-01, 'ab_w_in': 8.506990e-01, 'ab_conv_w': 1.553124e+00, 'ab_a_log': 4.295194e+00, 'ab_dt_bias': 4.219982e+00, 'ab_out_norm': 7.440227e+00, 'ab_w_out': 2.623510e+00, 'cd_w_in': 1.193406e+00, 'cd_sinks': 1.305894e-01, 'cd_w_out': 1.811937e+00}


def _to_microbatches(a, axis):
    t = _jnp.moveaxis(a, axis, 0)
    t = t.reshape((N_MICROBATCH, t.shape[0] // N_MICROBATCH) + t.shape[1:])
    return _jnp.moveaxis(t, 1, axis + 1)


def setup_inputs(seed: int = 0) -> dict:
    inp = _fwd_setup_inputs(seed)
    key = _jax.random.fold_in(_jax.random.key(seed), 7919)
    shape, _ = _output_shape()
    out = dict(inp)
    out["loss_target"] = _jax.random.normal(_jax.random.fold_in(key, 0), shape, _jnp.float32)
    for i, name in enumerate(TWIN_WEIGHTS):
        w = inp[name].astype(_jnp.float32)
        if MOMENT_SCALE is None:
            s = _jnp.sqrt(_jnp.mean(_jnp.square(w)) + 1e-30)
        else:
            s = MOMENT_SCALE[name]
        km, kv = _jax.random.split(_jax.random.fold_in(key, i + 1))
        out[name] = w
        out["m_" + name] = s * _jax.random.normal(km, w.shape, _jnp.float32)
        out["v_" + name] = (s * s) * _jax.random.uniform(kv, w.shape, _jnp.float32, 0.5, 1.5)
    if N_MICROBATCH > 1:
        for name, axis in PER_EXAMPLE_BATCH_AXIS.items():
            out[name] = _to_microbatches(out[name], axis)
    return {'x': out['x'], 'meta_tokens': out['meta_tokens'], 'norm_gains': out['norm_gains'], 'ffn_w_gate': out['ffn_w_gate'], 'ffn_w_up': out['ffn_w_up'], 'ffn_w_down': out['ffn_w_down'], 'ab_w_in': out['ab_w_in'], 'ab_conv_w': out['ab_conv_w'], 'ab_a_log': out['ab_a_log'], 'ab_dt_bias': out['ab_dt_bias'], 'ab_out_norm': out['ab_out_norm'], 'ab_w_out': out['ab_w_out'], 'cd_w_in': out['cd_w_in'], 'cd_sinks': out['cd_sinks'], 'cd_w_out': out['cd_w_out'], 'loss_target': out['loss_target'], 'm_meta_tokens': out['m_meta_tokens'], 'm_norm_gains': out['m_norm_gains'], 'm_ffn_w_gate': out['m_ffn_w_gate'], 'm_ffn_w_up': out['m_ffn_w_up'], 'm_ffn_w_down': out['m_ffn_w_down'], 'm_ab_w_in': out['m_ab_w_in'], 'm_ab_conv_w': out['m_ab_conv_w'], 'm_ab_a_log': out['m_ab_a_log'], 'm_ab_dt_bias': out['m_ab_dt_bias'], 'm_ab_out_norm': out['m_ab_out_norm'], 'm_ab_w_out': out['m_ab_w_out'], 'm_cd_w_in': out['m_cd_w_in'], 'm_cd_sinks': out['m_cd_sinks'], 'm_cd_w_out': out['m_cd_w_out'], 'v_meta_tokens': out['v_meta_tokens'], 'v_norm_gains': out['v_norm_gains'], 'v_ffn_w_gate': out['v_ffn_w_gate'], 'v_ffn_w_up': out['v_ffn_w_up'], 'v_ffn_w_down': out['v_ffn_w_down'], 'v_ab_w_in': out['v_ab_w_in'], 'v_ab_conv_w': out['v_ab_conv_w'], 'v_ab_a_log': out['v_ab_a_log'], 'v_ab_dt_bias': out['v_ab_dt_bias'], 'v_ab_out_norm': out['v_ab_out_norm'], 'v_ab_w_out': out['v_ab_w_out'], 'v_cd_w_in': out['v_cd_w_in'], 'v_cd_sinks': out['v_cd_sinks'], 'v_cd_w_out': out['v_cd_w_out']}


def _loss(weights, diff, rest, loss_target):
    with _jax.named_scope("forward"):
        args = {**rest, TWIN_DIFF_INPUT: diff, **{k: w.astype(_WEIGHT_DTYPES[k]) for k, w in weights.items()}}
        y = _forward(args)
    with _jax.named_scope("loss_head"):
        err = _jnp.square(y.astype(_jnp.float32) - loss_target)
        return 0.5 * _jnp.sum(_jnp.mean(err, axis=-1)) if err.ndim else 0.5 * err


def _adamw(w, g, m, v):
    m = ADAM_B1 * m + (1.0 - ADAM_B1) * g
    v = ADAM_B2 * v + (1.0 - ADAM_B2) * _jnp.square(g)
    m_hat = m / (1.0 - ADAM_B1 ** ADAM_STEP)
    v_hat = v / (1.0 - ADAM_B2 ** ADAM_STEP)
    delta = -ADAM_LR * (m_hat / (_jnp.sqrt(v_hat) + ADAM_EPS) + ADAM_WD * w)
    return delta, m, v


def reference(x, meta_tokens, norm_gains, ffn_w_gate, ffn_w_up, ffn_w_down, ab_w_in, ab_conv_w, ab_a_log, ab_dt_bias, ab_out_norm, ab_w_out, cd_w_in, cd_sinks, cd_w_out, loss_target, m_meta_tokens, m_norm_gains, m_ffn_w_gate, m_ffn_w_up, m_ffn_w_down, m_ab_w_in, m_ab_conv_w, m_ab_a_log, m_ab_dt_bias, m_ab_out_norm, m_ab_w_out, m_cd_w_in, m_cd_sinks, m_cd_w_out, v_meta_tokens, v_norm_gains, v_ffn_w_gate, v_ffn_w_up, v_ffn_w_down, v_ab_w_in, v_ab_conv_w, v_ab_a_log, v_ab_dt_bias, v_ab_out_norm, v_ab_w_out, v_cd_w_in, v_cd_sinks, v_cd_w_out):
    given = dict(x=x, meta_tokens=meta_tokens, norm_gains=norm_gains, ffn_w_gate=ffn_w_gate, ffn_w_up=ffn_w_up, ffn_w_down=ffn_w_down, ab_w_in=ab_w_in, ab_conv_w=ab_conv_w, ab_a_log=ab_a_log, ab_dt_bias=ab_dt_bias, ab_out_norm=ab_out_norm, ab_w_out=ab_w_out, cd_w_in=cd_w_in, cd_sinks=cd_sinks, cd_w_out=cd_w_out, loss_target=loss_target, m_meta_tokens=m_meta_tokens, m_norm_gains=m_norm_gains, m_ffn_w_gate=m_ffn_w_gate, m_ffn_w_up=m_ffn_w_up, m_ffn_w_down=m_ffn_w_down, m_ab_w_in=m_ab_w_in, m_ab_conv_w=m_ab_conv_w, m_ab_a_log=m_ab_a_log, m_ab_dt_bias=m_ab_dt_bias, m_ab_out_norm=m_ab_out_norm, m_ab_w_out=m_ab_w_out, m_cd_w_in=m_cd_w_in, m_cd_sinks=m_cd_sinks, m_cd_w_out=m_cd_w_out, v_meta_tokens=v_meta_tokens, v_norm_gains=v_norm_gains, v_ffn_w_gate=v_ffn_w_gate, v_ffn_w_up=v_ffn_w_up, v_ffn_w_down=v_ffn_w_down, v_ab_w_in=v_ab_w_in, v_ab_conv_w=v_ab_conv_w, v_ab_a_log=v_ab_a_log, v_ab_dt_bias=v_ab_dt_bias, v_ab_out_norm=v_ab_out_norm, v_ab_w_out=v_ab_w_out, v_cd_w_in=v_cd_w_in, v_cd_sinks=v_cd_sinks, v_cd_w_out=v_cd_w_out)
    weights = {n: given[n] for n in TWIN_WEIGHTS}
    shared = {n: given[n] for n in SHARED_INPUTS}
    per_example = {n: given[n] for n in ['x']}
    grad_fn = _jax.value_and_grad(_loss, argnums=(0, 1))

    def one_microbatch(ex, loss_target):
        ex = dict(ex)
        diff = ex.pop(TWIN_DIFF_INPUT)
        return grad_fn(weights, diff, {**shared, **ex}, loss_target)

    if N_MICROBATCH == 1:
        loss, (grad_w, grad_x) = one_microbatch(per_example, given["loss_target"])
    else:
        def body(carry, xs):
            loss_sum, grad_sum = carry
            l_k, (gw_k, gx_k) = one_microbatch(xs[0], xs[1])
            with _jax.named_scope("update"):
                return (loss_sum + l_k, _jax.tree.map(_jnp.add, grad_sum, gw_k)), gx_k

        init = (_jnp.zeros((), _jnp.float32), _jax.tree.map(_jnp.zeros_like, weights))
        (loss, grad_w), grad_x = _jax.lax.scan(body, init, (per_example, given["loss_target"]))
    with _jax.named_scope("update"):
        delta_w, new_m, new_v = {}, {}, {}
        for n in TWIN_WEIGHTS:
            delta_w[n], new_m[n], new_v[n] = _adamw(weights[n], grad_w[n], given["m_" + n], given["v_" + n])
    return (loss, grad_x, *[grad_w[n] for n in TWIN_WEIGHTS], *[delta_w[n] for n in TWIN_WEIGHTS],
            *[new_m[n] for n in TWIN_WEIGHTS], *[new_v[n] for n in TWIN_WEIGHTS])
```

```python
import functools
import math

import numpy as np
import jax
import jax.numpy as jnp
from jax import lax
from jax.experimental import pallas as pl
from jax.experimental.pallas import tpu as pltpu

F32, BF16 = jnp.float32, jnp.bfloat16
EPS = 1e-6
D = 1024
NMETA = 16
BLK = 128
PAD = BLK - NMETA
DFF = 2816
LANES = 128
NDEV = 8
AB_IN, AB_INP = 4104, 4224
ADAM_LR, ADAM_B1, ADAM_B2, ADAM_EPS, ADAM_WD, ADAM_STEP = 0.001, 0.9, 0.999, 1e-08, 0.01, 10
VMEM_LIMIT = 56 * 1024 * 1024
MESH = pl.DeviceIdType.MESH
HIGHEST = lax.Precision.HIGHEST


def _params(sem):
    return pltpu.CompilerParams(dimension_semantics=sem, vmem_limit_bytes=VMEM_LIMIT)


def _row_tile(T, streamed, resident):
    for tm in (640, 320, 128):
        if T % tm == 0 and 2 * (tm * streamed + resident) <= VMEM_LIMIT - 14 * 1024 * 1024:
            return tm
    return _tile(T, 128)


def _tile(n, cap):
    if n <= cap:
        return n
    best = None
    for t in range(LANES, cap + 1, LANES):
        if n % t == 0:
            best = t
    assert best is not None, (n, cap)
    return best


def _rms_fwd(x, g):
    return x * lax.rsqrt(jnp.mean(x * x, axis=-1, keepdims=True) + EPS) * g


def _rms_bwd(x, g, dz):
    r = lax.rsqrt(jnp.mean(x * x, axis=-1, keepdims=True) + EPS)
    xh = x * r
    dg = jnp.sum(dz * xh, axis=0, keepdims=True)
    t = dz * g
    return r * (t - xh * jnp.mean(t * xh, axis=-1, keepdims=True)), dg


def _sigmoid(x):
    return 1.0 / (1.0 + jnp.exp(-x))


def _silu(x):
    return x * _sigmoid(x)


def _nn(a, b, precision=None):
    return lax.dot_general(a, b, (((1,), (0,)), ((), ())), preferred_element_type=F32, precision=precision)


def _nt(a, b):
    return lax.dot_general(a, b, (((1,), (1,)), ((), ())), preferred_element_type=F32)


def _tn(a, b):
    return lax.dot_general(a, b, (((0,), (0,)), ((), ())), preferred_element_type=F32)


@jax.custom_vjp
def bdot(a, b):
    return _nn(a.astype(BF16), b.astype(BF16))


def _bdot_fwd(a, b):
    return bdot(a, b), (a, b)


def _bdot_bwd(res, g):
    a, b = res
    return bdot(g, b.T), bdot(a.T, g)


bdot.defvjp(_bdot_fwd, _bdot_bwd)


@jax.custom_vjp
def hdot(a, b):
    return _nn(a, b, HIGHEST)


def _hdot_fwd(a, b):
    return hdot(a, b), (a, b)


def _hdot_bwd(res, g):
    a, b = res
    return hdot(g, b.T), hdot(a.T, g)


hdot.defvjp(_hdot_fwd, _hdot_bwd)


def _iota2(shape, axis):
    return lax.broadcasted_iota(jnp.int32, shape, axis)


def _lane_pick(row, lane):
    return jnp.sum(jnp.where(_iota2(row.shape, 1) == lane, row, 0.0), axis=1, keepdims=True)


def norm_mm(h, gain, ws, *, swiglu, name):
    T, Dm = h.shape
    N = ws[0].shape[1]
    tm, tn = _tile(T, 640), _tile(N, 1408)
    nw = len(ws)

    def body(h_ref, g_ref, *refs):
        w_refs, u_ref, o_refs = refs[:nw], refs[nw], refs[nw + 1:]

        @pl.when(pl.program_id(1) == 0)
        def _():
            u_ref[...] = _rms_fwd(h_ref[...], g_ref[...]).astype(BF16)

        u = u_ref[...]
        acc = [_nn(u, w[...]) for w in w_refs]
        if swiglu:
            o_refs[0][...] = acc[0].astype(BF16)
            o_refs[1][...] = acc[1].astype(BF16)
            o_refs[2][...] = (_silu(acc[0]) * acc[1]).astype(BF16)
        else:
            o_refs[0][...] = acc[0]

    row = pl.BlockSpec((tm, Dm), lambda i, j: (i, 0))
    tile = pl.BlockSpec((tm, tn), lambda i, j: (i, j))
    if swiglu:
        out_shape = [jax.ShapeDtypeStruct((T, Dm), BF16)] + [jax.ShapeDtypeStruct((T, N), BF16)] * 3
        out_specs = [row, tile, tile, tile]
    else:
        out_shape = [jax.ShapeDtypeStruct((T, Dm), BF16), jax.ShapeDtypeStruct((T, N), F32)]
        out_specs = [row, tile]
    return pl.pallas_call(
        body, name=name, grid=(T // tm, N // tn),
        in_specs=[row, pl.BlockSpec((1, Dm), lambda i, j: (0, 0))]
        + [pl.BlockSpec((Dm, tn), lambda i, j: (0, j))] * nw,
        out_specs=out_specs, out_shape=out_shape,
        compiler_params=_params(("arbitrary", "arbitrary")),
    )(h, gain, *ws)


def mm_norm_res(As, Ws, h, gain, scale, *, name):
    T, Dm = h.shape
    n = len(As)
    tm = _row_tile(T, sum(a.shape[1] * a.dtype.itemsize for a in As) + 3 * Dm * 4,
                   sum(w.size * w.dtype.itemsize for w in Ws))

    def body(*refs):
        a_refs, w_refs = refs[:n], refs[n:2 * n]
        h_ref, g_ref, y_ref, hn_ref = refs[2 * n:]
        y = _nn(a_refs[0][...].astype(BF16), w_refs[0][...])
        for a, w in zip(a_refs[1:], w_refs[1:]):
            y = y + _nn(a[...].astype(BF16), w[...])
        y_ref[...] = y
        hn_ref[...] = h_ref[...] + scale * _rms_fwd(y, g_ref[...])

    row = pl.BlockSpec((tm, Dm), lambda i: (i, 0))
    return pl.pallas_call(
        body, name=name, grid=(T // tm,),
        in_specs=[pl.BlockSpec((tm, a.shape[1]), lambda i: (i, 0)) for a in As]
        + [pl.BlockSpec(w.shape, lambda i: (0, 0)) for w in Ws]
        + [row, pl.BlockSpec((1, Dm), lambda i: (0, 0))],
        out_specs=[row, row], out_shape=[jax.ShapeDtypeStruct((T, Dm), F32)] * 2,
        compiler_params=_params(("arbitrary",)),
    )(*As, *Ws, h, gain)


def normbwd_mm_nt(dh, y, gain, w, scale, gu=None, *, name):
    T, Dm = dh.shape
    N = w.shape[0]
    tm, tn = _tile(T, 640), _tile(N, 1408)
    swiglu = gu is not None

    def body(dh_ref, y_ref, g_ref, w_ref, *refs):
        if swiglu:
            gate_ref, up_ref, dy_ref, dg_ref, dgate_ref, dup_ref, a_ref = refs
        else:
            dy_ref, dg_ref, da_ref = refs
        i, j = pl.program_id(0), pl.program_id(1)

        @pl.when(j == 0)
        def _():
            dy, dg = _rms_bwd(y_ref[...], g_ref[...], scale * dh_ref[...])
            dy_ref[...] = dy.astype(BF16)

            @pl.when(i == 0)
            def _():
                dg_ref[...] = jnp.zeros_like(dg_ref)

            dg_ref[...] += dg

        da = _nt(dy_ref[...], w_ref[...])
        if swiglu:
            gate, up = gate_ref[...].astype(F32), up_ref[...].astype(F32)
            s = _sigmoid(gate)
            dgate_ref[...] = (da * up * s * (1.0 + gate * (1.0 - s))).astype(BF16)
            dup_ref[...] = (da * gate * s).astype(BF16)
            a_ref[...] = (gate * s * up).astype(BF16)
        else:
            da_ref[...] = da

    row = pl.BlockSpec((tm, Dm), lambda i, j: (i, 0))
    vec = pl.BlockSpec((1, Dm), lambda i, j: (0, 0))
    tile = pl.BlockSpec((tm, tn), lambda i, j: (i, j))
    in_specs = [row, row, vec, pl.BlockSpec((tn, Dm), lambda i, j: (j, 0))]
    out_shape = [jax.ShapeDtypeStruct((T, Dm), BF16), jax.ShapeDtypeStruct((1, Dm), F32)]
    if swiglu:
        in_specs += [tile, tile]
        out_shape += [jax.ShapeDtypeStruct((T, N), BF16)] * 3
        out_specs = [row, vec, tile, tile, tile]
        args = (dh, y, gain, w, *gu)
    else:
        out_shape += [jax.ShapeDtypeStruct((T, N), F32)]
        out_specs = [row, vec, tile]
        args = (dh, y, gain, w)
    return pl.pallas_call(
        body, name=name, grid=(T // tm, N // tn), in_specs=in_specs, out_specs=out_specs,
        out_shape=out_shape, compiler_params=_params(("arbitrary", "arbitrary")),
    )(*args)


def mm_nt_normbwd(dPs, Ws, h, gain, dh_in, *, name):
    T, Dm = h.shape
    n = len(dPs)
    tm = _row_tile(T, sum(p.shape[1] * p.dtype.itemsize for p in dPs) + 3 * Dm * 4,
                   sum(w.size * w.dtype.itemsize for w in Ws))

    def body(*refs):
        p_refs, w_refs = refs[:n], refs[n:2 * n]
        h_ref, g_ref, dhin_ref, dh_ref, dg_ref = refs[2 * n:]
        du = _nt(p_refs[0][...].astype(BF16), w_refs[0][...])
        for p, w in zip(p_refs[1:], w_refs[1:]):
            du = du + _nt(p[...].astype(BF16), w[...])
        dx, dg = _rms_bwd(h_ref[...], g_ref[...], du)
        dh_ref[...] = dhin_ref[...] + dx

        @pl.when(pl.program_id(0) == 0)
        def _():
            dg_ref[...] = jnp.zeros_like(dg_ref)

        dg_ref[...] += dg

    row = pl.BlockSpec((tm, Dm), lambda i: (i, 0))
    vec = pl.BlockSpec((1, Dm), lambda i: (0, 0))
    return pl.pallas_call(
        body, name=name, grid=(T // tm,),
        in_specs=[pl.BlockSpec((tm, p.shape[1]), lambda i: (i, 0)) for p in dPs]
        + [pl.BlockSpec(w.shape, lambda i: (0, 0)) for w in Ws] + [row, vec, row],
        out_specs=[row, vec],
        out_shape=[jax.ShapeDtypeStruct((T, Dm), F32), jax.ShapeDtypeStruct((1, Dm), F32)],
        compiler_params=_params(("arbitrary",)),
    )(*dPs, *Ws, h, gain, dh_in)


def mm_tn(a, b, *, name):
    T, M = a.shape
    N = b.shape[1]
    tm, tn, tk = _tile(M, 1024), _tile(N, 1408), _tile(T, 640)

    def body(a_ref, b_ref, o_ref):
        @pl.when(pl.program_id(2) == 0)
        def _():
            o_ref[...] = jnp.zeros_like(o_ref)

        o_ref[...] += _tn(a_ref[...].astype(BF16), b_ref[...].astype(BF16))

    return pl.pallas_call(
        body, name=name, grid=(M // tm, N // tn, T // tk),
        in_specs=[pl.BlockSpec((tk, tm), lambda i, j, k: (k, i)), pl.BlockSpec((tk, tn), lambda i, j, k: (k, j))],
        out_specs=pl.BlockSpec((tm, tn), lambda i, j, k: (i, j)),
        out_shape=jax.ShapeDtypeStruct((M, N), F32),
        compiler_params=_params(("arbitrary", "arbitrary", "arbitrary")),
    )(a, b)


def loss_and_grad(h, target, *, name):
    T, Dm = h.shape

    def body(h_ref, t_ref, loss_ref, dh_ref):
        b = pl.program_id(0)

        @pl.when(b == 0)
        def _():
            loss_ref[...] = jnp.zeros_like(loss_ref)
            dh_ref[...] = jnp.zeros_like(dh_ref)

        @pl.when(b > 0)
        def _():
            e = h_ref[...] - t_ref[...]
            dh_ref[...] = e * (1.0 / Dm)
            loss_ref[...] += jnp.sum(e * e) * (0.5 / Dm)

    return pl.pallas_call(
        body, name=name, grid=(T // BLK,),
        in_specs=[pl.BlockSpec((BLK, Dm), lambda b: (b, 0)),
                  pl.BlockSpec((BLK, Dm), lambda b: (jnp.maximum(b - 1, 0), 0))],
        out_specs=[pl.BlockSpec((8, LANES), lambda b: (0, 0)), pl.BlockSpec((BLK, Dm), lambda b: (b, 0))],
        out_shape=[jax.ShapeDtypeStruct((8, LANES), F32), jax.ShapeDtypeStruct((T, Dm), F32)],
        compiler_params=_params(("arbitrary",)),
    )(h, target)


def adamw(w, g, m, v, *, name):
    R, C = w.shape
    tr = R
    for t in (512, 352, 256):
        if R > t and R % t == 0:
            tr = t
            break

    def body(w_ref, g_ref, m_ref, v_ref, d_ref, nm_ref, nv_ref):
        g_ = g_ref[...]
        m_ = ADAM_B1 * m_ref[...] + (1.0 - ADAM_B1) * g_
        v_ = ADAM_B2 * v_ref[...] + (1.0 - ADAM_B2) * (g_ * g_)
        m_hat = m_ / (1.0 - ADAM_B1 ** ADAM_STEP)
        v_hat = v_ / (1.0 - ADAM_B2 ** ADAM_STEP)
        d_ref[...] = -ADAM_LR * (m_hat / (jnp.sqrt(v_hat) + ADAM_EPS) + ADAM_WD * w_ref[...])
        nm_ref[...] = m_
        nv_ref[...] = v_

    spec = pl.BlockSpec((tr, C), lambda i: (i, 0))
    return pl.pallas_call(
        body, name=name, grid=(R // tr,), in_specs=[spec] * 4, out_specs=[spec] * 3,
        out_shape=[jax.ShapeDtypeStruct((R, C), F32)] * 3, compiler_params=_params(("arbitrary",)),
    )(w, g, m, v)


def _me():
    return lax.axis_index("x"), lax.axis_index("y"), lax.axis_index("c")


def _flip(pos, rel):
    return tuple(1 - p if r else p for p, r in zip(pos, rel))


def _slot(pos):
    return 4 * pos[0] + 2 * pos[1] + pos[2]


HBM_SPEC = pl.BlockSpec(memory_space=pltpu.HBM)
CHIP_RELS = ((1, 0), (0, 1), (1, 1))


def all_gather_big(x, *, name):
    R, C = x.shape

    def body(x_ref, out_ref, send_sems, recv_sems, local_sem):
        me = _me()
        sibling = _flip(me, (0, 0, 1))
        chips = [_flip(me, rel + (0,)) for rel in CHIP_RELS]

        def copy(k, block, to, src=None):
            dst = out_ref.at[_slot(block)]
            return pltpu.make_async_remote_copy(
                src_ref=dst if src is None else src, dst_ref=dst, send_sem=send_sems.at[k],
                recv_sem=recv_sems.at[k], device_id=to, device_id_type=MESH)

        mine = pltpu.make_async_copy(x_ref, out_ref.at[_slot(me)], local_sem)
        mine.start()
        first = [copy(0, me, sibling, src=x_ref)] + [copy(1 + j, me, chip, src=x_ref) for j, chip in enumerate(chips)]
        for cp in first:
            cp.start()
        passed = [copy(4 + j, chip, sibling) for j, chip in enumerate(chips)]
        for j, chip in enumerate(chips):
            copy(1 + j, chip, me).wait_recv()
            passed[j].start()
        copy(0, sibling, me).wait_recv()
        for j, chip in enumerate(chips):
            copy(4 + j, _flip(chip, (0, 0, 1)), me).wait_recv()
        for cp in first + passed:
            cp.wait_send()
        mine.wait()

    return pl.pallas_call(
        body, name=name, in_specs=[HBM_SPEC], out_specs=HBM_SPEC,
        out_shape=jax.ShapeDtypeStruct((NDEV, R, C), x.dtype),
        scratch_shapes=[pltpu.SemaphoreType.DMA((7,)), pltpu.SemaphoreType.DMA((7,)), pltpu.SemaphoreType.DMA],
    )(x)


def all_to_all_small(src, *, name):
    _, r, C = src.shape

    def body(src_ref, out_ref, send_sems, recv_sems):
        me = _me()
        my = _slot(me)
        out_ref[my] = src_ref[my]
        copies = []
        for k in range(1, NDEV):
            peer = _flip(me, ((k >> 2) & 1, (k >> 1) & 1, k & 1))
            cp = pltpu.make_async_remote_copy(
                src_ref=src_ref.at[_slot(peer)], dst_ref=out_ref.at[my], send_sem=send_sems.at[k - 1],
                recv_sem=recv_sems.at[k - 1], device_id=peer, device_id_type=MESH)
            cp.start()
            copies.append((cp, peer))
        for k, (cp, peer) in enumerate(copies):
            pltpu.make_async_remote_copy(
                src_ref=src_ref.at[my], dst_ref=out_ref.at[_slot(peer)], send_sem=send_sems.at[k],
                recv_sem=recv_sems.at[k], device_id=peer, device_id_type=MESH).wait_recv()
        for cp, _ in copies:
            cp.wait_send()

    vm = pl.BlockSpec(memory_space=pltpu.VMEM)
    return pl.pallas_call(
        body, name=name, in_specs=[vm], out_specs=vm, out_shape=jax.ShapeDtypeStruct(src.shape, src.dtype),
        scratch_shapes=[pltpu.SemaphoreType.DMA((7,)), pltpu.SemaphoreType.DMA((7,))],
    )(src)


def sum_slots(a, *, name):
    n, r, C = a.shape

    def body(a_ref, o_ref):
        s = a_ref[0]
        for k in range(1, n):
            s = s + a_ref[k]
        o_ref[...] = s

    vm = pl.BlockSpec(memory_space=pltpu.VMEM)
    return pl.pallas_call(body, name=name, in_specs=[vm], out_specs=vm,
                          out_shape=jax.ShapeDtypeStruct((r, C), F32))(a)


def rs_exchange_sibling(g, *, name):
    _, R, C = g.shape

    def body(g_ref, out_ref, send_sems, recv_sems):
        me = _me()
        sibling = _flip(me, (0, 0, 1))
        copies = []
        for chip in range(4):
            cp = pltpu.make_async_remote_copy(
                src_ref=g_ref.at[2 * chip + sibling[2]], dst_ref=out_ref.at[chip], send_sem=send_sems.at[chip],
                recv_sem=recv_sems.at[chip], device_id=sibling, device_id_type=MESH)
            cp.start()
            copies.append(cp)
        for cp in copies:
            cp.wait()

    return pl.pallas_call(
        body, name=name, in_specs=[HBM_SPEC], out_specs=HBM_SPEC,
        out_shape=jax.ShapeDtypeStruct((4, R, C), g.dtype),
        scratch_shapes=[pltpu.SemaphoreType.DMA((4,)), pltpu.SemaphoreType.DMA((4,))],
    )(g)


def rs_chip_partials(g, got, *, name):
    _, R, C = g.shape
    tr = _tile(R, 768)

    def body(c_ref, g_ref, got_ref, o_ref):
        o_ref[...] = (g_ref[...].astype(F32) + got_ref[...].astype(F32)).astype(o_ref.dtype)

    c = jnp.reshape(lax.axis_index("c"), (1,)).astype(jnp.int32)
    return pl.pallas_call(
        body, name=name,
        grid_spec=pltpu.PrefetchScalarGridSpec(
            num_scalar_prefetch=1, grid=(4, R // tr),
            in_specs=[pl.BlockSpec((None, tr, C), lambda k, i, c_ref: (2 * k + c_ref[0], i, 0)),
                      pl.BlockSpec((None, tr, C), lambda k, i, c_ref: (k, i, 0))],
            out_specs=pl.BlockSpec((None, tr, C), lambda k, i, c_ref: (k, i, 0))),
        out_shape=jax.ShapeDtypeStruct((4, R, C), g.dtype), compiler_params=_params(("arbitrary", "arbitrary")),
    )(c, g, got)


def rs_exchange_chips(p, *, name):
    _, R, C = p.shape

    def body(p_ref, out_ref, send_sems, recv_sems):
        me = _me()
        copies = []
        for j, rel in enumerate(CHIP_RELS):
            peer = _flip(me, rel + (0,))
            cp = pltpu.make_async_remote_copy(
                src_ref=p_ref.at[2 * peer[0] + peer[1]], dst_ref=out_ref.at[j], send_sem=send_sems.at[j],
                recv_sem=recv_sems.at[j], device_id=peer, device_id_type=MESH)
            cp.start()
            copies.append(cp)
        for cp in copies:
            cp.wait()

    return pl.pallas_call(
        body, name=name, in_specs=[HBM_SPEC], out_specs=HBM_SPEC,
        out_shape=jax.ShapeDtypeStruct((3, R, C), p.dtype),
        scratch_shapes=[pltpu.SemaphoreType.DMA((3,)), pltpu.SemaphoreType.DMA((3,))],
    )(p)


def rs_final_sum(p, got, *, name):
    _, R, C = p.shape
    tr = _tile(R, 768)

    def body(chip_ref, p_ref, got_ref, o_ref):
        s = p_ref[...].astype(F32)
        for j in range(3):
            s = s + got_ref[j].astype(F32)
        o_ref[...] = s

    mychip = jnp.reshape(2 * lax.axis_index("x") + lax.axis_index("y"), (1,)).astype(jnp.int32)
    return pl.pallas_call(
        body, name=name,
        grid_spec=pltpu.PrefetchScalarGridSpec(
            num_scalar_prefetch=1, grid=(R // tr,),
            in_specs=[pl.BlockSpec((None, tr, C), lambda i, chip_ref: (chip_ref[0], i, 0)),
                      pl.BlockSpec((3, tr, C), lambda i, chip_ref: (0, i, 0))],
            out_specs=pl.BlockSpec((tr, C), lambda i, chip_ref: (i, 0))),
        out_shape=jax.ShapeDtypeStruct((R, C), F32), compiler_params=_params(("arbitrary",)),
    )(mychip, p, got)


def reduce_scatter_big(g):
    got = rs_exchange_sibling(g, name="rs_sibling")
    part = rs_chip_partials(g, got, name="rs_chip_partials")
    got2 = rs_exchange_chips(part, name="rs_chips")
    return rs_final_sum(part, got2, name="rs_final_sum")


def _blk(off):
    return pl.BlockSpec((BLK, LANES), lambda h, n: (n, off + h))


def _const_spec(shape):
    return pl.BlockSpec(shape, lambda *_: (0,) * len(shape))


def retention_tables(T):
    pos = jnp.arange(T, dtype=F32) - float(PAD)
    inv_freq = 1.0 / (10000.0 ** jnp.linspace(0.0, 1.0, 64, dtype=F32))
    ang = pos[:, None] * inv_freq[None, :]
    cos = jnp.repeat(jnp.cos(ang), 2, axis=1)
    sin = jnp.repeat(jnp.sin(ang), 2, axis=1) * jnp.tile(jnp.array([-1.0, 1.0], F32), 64)[None, :]
    lane = np.arange(LANES)
    perm = jnp.asarray((lane[:, None] == (lane[None, :] ^ 1)).astype(np.float32))
    log_gamma = jnp.log1p(-jnp.exp2(-5.0 - jnp.arange(4, dtype=F32)))
    idx = jnp.arange(BLK, dtype=F32)
    diff = idx[:, None] - idx[None, :]
    intra = jnp.where(diff >= 0, jnp.exp(jnp.maximum(diff, 0.0) * log_gamma[:, None, None]), 0.0)
    zeta = jnp.exp((BLK - 1.0 - idx)[None, :] * log_gamma[:, None])
    xi = jnp.exp((idx + 1.0)[None, :] * log_gamma[:, None])
    bc = lambda t: jnp.broadcast_to(t[:, :, None], (4, BLK, LANES))
    return cos, sin, perm, jnp.stack([intra, bc(zeta), bc(xi)], axis=1)


def _ret_chunk(rq, rk, rv, rg, S, cos, sin, tab, perm):
    intra, zeta, xi = tab[0], tab[1], tab[2]
    q = rq * cos + hdot(rq, perm) * sin
    k = (rk * cos + hdot(rk, perm) * sin) * (128.0 ** -0.5)
    ret = bdot(bdot(q, k.T) * intra, rv) + bdot(q * xi, S)
    S_new = S * xi[BLK - 1:BLK, :] + bdot((k * zeta).T, rv)
    c = ret - jnp.mean(ret, axis=-1, keepdims=True)
    out = c * lax.rsqrt(jnp.mean(c * c, axis=-1, keepdims=True) + EPS) * _silu(rg)
    return out, S_new


def retention_fwd(p, tables, *, name):
    T = p.shape[0]
    N = T // BLK
    cos, sin, perm, tab = tables

    def body(rq, rk, rv, rg, cos_ref, sin_ref, tab_ref, perm_ref, out_ref, sall_ref, s_scr):
        @pl.when(pl.program_id(1) == 0)
        def _():
            s_scr[...] = jnp.zeros_like(s_scr)

        S = s_scr[...]
        sall_ref[...] = S
        out, S_new = _ret_chunk(rq[...], rk[...], rv[...], rg[...], S, cos_ref[...], sin_ref[...], tab_ref[...],
                                perm_ref[...])
        out_ref[...] = out.astype(BF16)
        s_scr[...] = S_new

    rowtab = pl.BlockSpec((BLK, LANES), lambda h, n: (n, 0))
    return pl.pallas_call(
        body, name=name, grid=(4, N),
        in_specs=[_blk(0), _blk(4), _blk(8), _blk(12), rowtab, rowtab,
                  pl.BlockSpec((None, 3, BLK, LANES), lambda h, n: (h, 0, 0, 0)), _const_spec((LANES, LANES))],
        out_specs=[_blk(0), pl.BlockSpec((None, None, LANES, LANES), lambda h, n: (h, n, 0, 0))],
        out_shape=[jax.ShapeDtypeStruct((T, 512), BF16), jax.ShapeDtypeStruct((4, N, LANES, LANES), F32)],
        scratch_shapes=[pltpu.VMEM((LANES, LANES), F32)],
        compiler_params=_params(("arbitrary", "arbitrary")),
    )(p, p, p, p, cos, sin, tab, perm)


def _row_mask(n):
    return (n * BLK + _iota2((BLK, 1), 0) >= PAD).astype(F32)


def retention_bwd(p, sall, dmixed, tables, *, name):
    T = p.shape[0]
    N = T // BLK
    cos, sin, perm, tab = tables

    def body(rq, rk, rv, rg, cos_ref, sin_ref, tab_ref, perm_ref, sall_ref, do_ref, drq, drk, drv, drg, ds_scr):
        n = N - 1 - pl.program_id(1)

        @pl.when(pl.program_id(1) == 0)
        def _():
            ds_scr[...] = jnp.zeros_like(ds_scr)

        f = lambda a, b, c, d, s: _ret_chunk(a, b, c, d, s, cos_ref[...], sin_ref[...], tab_ref[...], perm_ref[...])
        _, vjp = jax.vjp(f, rq[...], rk[...], rv[...], rg[...], sall_ref[...])
        g = vjp((do_ref[...], ds_scr[...]))
        mask = _row_mask(n)
        for ref, val in zip((drq, drk, drv, drg), g[:4]):
            ref[...] = val * mask
        ds_scr[...] = g[4]

    def rblk(off):
        return pl.BlockSpec((BLK, LANES), lambda h, n: (N - 1 - n, off + h))

    rowtab = pl.BlockSpec((BLK, LANES), lambda h, n: (N - 1 - n, 0))
    return pl.pallas_call(
        body, name=name, grid=(4, N),
        in_specs=[rblk(0), rblk(4), rblk(8), rblk(12), rowtab, rowtab,
                  pl.BlockSpec((None, 3, BLK, LANES), lambda h, n: (h, 0, 0, 0)), _const_spec((LANES, LANES)),
                  pl.BlockSpec((None, None, LANES, LANES), lambda h, n: (h, N - 1 - n, 0, 0)), rblk(0)],
        out_specs=[rblk(0)] * 4, out_shape=[jax.ShapeDtypeStruct((T, 512), F32)] * 4,
        scratch_shapes=[pltpu.VMEM((LANES, LANES), F32)],
        compiler_params=_params(("arbitrary", "arbitrary")),
    )(p, p, p, p, cos, sin, tab, perm, sall, dmixed)


def conv_silu_fwd(p, w, *, name):
    T = p.shape[0]
    N = T // BLK

    def body(x_ref, xp_ref, w_ref, o_ref):
        n = pl.program_id(0)
        cur = x_ref[...]
        cat = jnp.concatenate([jnp.where(n > 0, xp_ref[...], 0.0), cur], axis=0)
        y = w_ref[3:4, :] * cur
        for s in (1, 2, 3):
            y = y + w_ref[3 - s:4 - s, :] * pltpu.roll(cat, s, 0)[BLK:]
        o_ref[...] = _silu(y)

    return pl.pallas_call(
        body, name=name, grid=(N, 12),
        in_specs=[pl.BlockSpec((BLK, LANES), lambda n, c: (n, 16 + c)),
                  pl.BlockSpec((BLK, LANES), lambda n, c: (jnp.maximum(n - 1, 0), 16 + c)),
                  pl.BlockSpec((4, LANES), lambda n, c: (0, c))],
        out_specs=pl.BlockSpec((BLK, LANES), lambda n, c: (n, c)),
        out_shape=jax.ShapeDtypeStruct((T, 1536), F32), compiler_params=_params(("arbitrary", "arbitrary")),
    )(p, p, w)


def conv_silu_bwd(p, w, dact, part, *, name):
    T = p.shape[0]
    N = T // BLK
    off = 16 + 4 * part

    def body(xp_ref, x_ref, xn_ref, w_ref, da_ref, dan_ref, dx_ref, dw_ref):
        n = pl.program_id(1)
        last = n == N - 1
        cat = jnp.concatenate([jnp.where(n > 0, xp_ref[...], 0.0), x_ref[...], jnp.where(last, 0.0, xn_ref[...])], axis=0)
        shifted = [cat] + [pltpu.roll(cat, s, 0) for s in (1, 2, 3)]
        y = w_ref[3:4, :] * shifted[0]
        for s in (1, 2, 3):
            y = y + w_ref[3 - s:4 - s, :] * shifted[s]
        y = y[BLK:]
        da = jnp.concatenate([da_ref[...], jnp.where(last, 0.0, dan_ref[...])], axis=0)
        sg = _sigmoid(y)
        dy = da * sg * (1.0 + y * (1.0 - sg))
        dx = w_ref[3:4, :] * dy[:BLK]
        for s in (1, 2, 3):
            dx = dx + w_ref[3 - s:4 - s, :] * pltpu.roll(dy, 2 * BLK - s, 0)[:BLK]
        dx_ref[...] = dx * _row_mask(n)

        @pl.when(n == 0)
        def _():
            dw_ref[...] = jnp.zeros_like(dw_ref)

        for s in (0, 1, 2, 3):
            dw_ref[3 - s:4 - s, :] += jnp.sum(dy[:BLK] * shifted[s][BLK:2 * BLK], axis=0, keepdims=True)

    def xs(d):
        return pl.BlockSpec((BLK, LANES), lambda c, n: (jnp.clip(n + d, 0, N - 1), off + c))

    return pl.pallas_call(
        body, name=name, grid=(4, N),
        in_specs=[xs(-1), xs(0), xs(1), pl.BlockSpec((4, LANES), lambda c, n: (0, 4 * part + c)),
                  pl.BlockSpec((BLK, LANES), lambda c, n: (n, c)),
                  pl.BlockSpec((BLK, LANES), lambda c, n: (jnp.minimum(n + 1, N - 1), c))],
        out_specs=[pl.BlockSpec((BLK, LANES), lambda c, n: (n, c)), pl.BlockSpec((4, LANES), lambda c, n: (0, c))],
        out_shape=[jax.ShapeDtypeStruct((T, 512), F32), jax.ShapeDtypeStruct((4, 512), F32)],
        compiler_params=_params(("arbitrary", "arbitrary")),
    )(p, p, p, w, dact, dact)


def _softplus(x):
    return jnp.maximum(x, 0.0) + jnp.log1p(jnp.exp(-jnp.abs(x)))


def _gdn_chunk(qa, ka, va, z, ba, S, alog, dtb, onorm, head, rowmask, lincl):
    r, c = _iota2((BLK, BLK), 0), _iota2((BLK, BLK), 1)
    incl, strict = r >= c, r > c
    eye = (r == c).astype(F32)
    q = qa * lax.rsqrt(jnp.sum(qa * qa, axis=-1, keepdims=True) + EPS) * (128.0 ** -0.5)
    k = ka * lax.rsqrt(jnp.sum(ka * ka, axis=-1, keepdims=True) + EPS)
    beta = _sigmoid(_lane_pick(ba, head)) * rowmask
    a = -jnp.exp(_lane_pick(alog, head))
    g = a * _softplus(_lane_pick(ba, 4 + head) + _lane_pick(dtb, head)) * rowmask
    gc = hdot(lincl, jnp.broadcast_to(g, (BLK, LANES)))
    decay = jnp.where(incl, jnp.exp(jnp.where(incl, gc - gc.T, 0.0)), 0.0)
    kb = k * beta
    amat = jnp.where(strict, bdot(kb, k.T) * decay, 0.0)
    m = -amat
    inv = eye + m
    pw = hdot(m, m)
    for t in range(6):
        inv = inv + hdot(inv, pw)
        if t < 5:
            pw = hdot(pw, pw)
    egc = jnp.exp(gc)
    u = hdot(inv, va * beta)
    w = hdot(inv, kb * egc)
    qk = jnp.where(incl, bdot(q, k.T) * decay, 0.0)
    glast = gc[BLK - 1:BLK, :]
    vnew = u - bdot(w, S)
    o = bdot(q * egc, S) + bdot(qk, vnew)
    S_new = S * jnp.exp(glast) + bdot((k * jnp.exp(glast - gc)).T, vnew)
    out = o * lax.rsqrt(jnp.mean(o * o, axis=-1, keepdims=True) + EPS) * onorm * _silu(z)
    return out, S_new


def _lincl():
    i = np.arange(BLK)
    return jnp.asarray((i[:, None] >= i[None, :]).astype(np.float32))


def gdn_fwd(act, p, alog, dtb, onorm, *, name):
    T = p.shape[0]
    N = T // BLK

    def body(qa, ka, va, z, ba, alog_ref, dtb_ref, on_ref, l_ref, out_ref, sall_ref, s_scr):
        n, h = pl.program_id(0), pl.program_id(1)

        @pl.when(n == 0)
        def _():
            s_scr[h] = jnp.zeros((LANES, LANES), F32)

        S = s_scr[h]
        sall_ref[...] = S
        out, S_new = _gdn_chunk(qa[...], ka[...], va[...], z[...], ba[...], S, alog_ref[...], dtb_ref[...],
                                on_ref[...], h, _row_mask(n), l_ref[...])
        out_ref[...] = out.astype(BF16)
        s_scr[h] = S_new

    def blk(off):
        return pl.BlockSpec((BLK, LANES), lambda n, h: (n, off + h))

    vec = _const_spec((1, LANES))
    return pl.pallas_call(
        body, name=name, grid=(N, 4),
        in_specs=[blk(0), blk(4), blk(8), blk(28), pl.BlockSpec((BLK, LANES), lambda n, h: (n, 32)), vec, vec, vec,
                  _const_spec((BLK, BLK))],
        out_specs=[blk(0), pl.BlockSpec((None, None, LANES, LANES), lambda n, h: (n, h, 0, 0))],
        out_shape=[jax.ShapeDtypeStruct((T, 512), BF16), jax.ShapeDtypeStruct((N, 4, LANES, LANES), F32)],
        scratch_shapes=[pltpu.VMEM((4, LANES, LANES), F32)],
        compiler_params=_params(("arbitrary", "arbitrary")),
    )(act, act, act, p, p, alog, dtb, onorm, _lincl())


def gdn_bwd(act, p, alog, dtb, onorm, sall, dmixed, *, name):
    T = p.shape[0]
    N = T // BLK

    def body(qa, ka, va, z, ba, alog_ref, dtb_ref, on_ref, l_ref, sall_ref, do_ref,
             dq_ref, dk_ref, dv_ref, dz_ref, dba_ref, dal_ref, ddt_ref, don_ref, ds_scr):
        step, h = pl.program_id(0), pl.program_id(1)
        n = N - 1 - step

        @pl.when(step == 0)
        def _():
            ds_scr[h] = jnp.zeros((LANES, LANES), F32)

        @pl.when((step == 0) & (h == 0))
        def _():
            dal_ref[...] = jnp.zeros_like(dal_ref)
            ddt_ref[...] = jnp.zeros_like(ddt_ref)
            don_ref[...] = jnp.zeros_like(don_ref)

        @pl.when(h == 0)
        def _():
            dba_ref[...] = jnp.zeros_like(dba_ref)

        rowmask, lincl = _row_mask(n), l_ref[...]
        f = lambda *a: _gdn_chunk(*a, h, rowmask, lincl)
        _, vjp = jax.vjp(f, qa[...], ka[...], va[...], z[...], ba[...], sall_ref[...], alog_ref[...], dtb_ref[...],
                         on_ref[...])
        g = vjp((do_ref[...], ds_scr[h]))
        dq_ref[...] = g[0] * rowmask
        dk_ref[...] = g[1] * rowmask
        dv_ref[...] = g[2] * rowmask
        dz_ref[...] = g[3] * rowmask
        dba_ref[...] += g[4] * rowmask
        ds_scr[h] = g[5]
        dal_ref[...] += g[6]
        ddt_ref[...] += g[7]
        don_ref[...] += g[8]

    def blk(off):
        return pl.BlockSpec((BLK, LANES), lambda s, h: (N - 1 - s, off + h))

    vec = _const_spec((1, LANES))
    col = pl.BlockSpec((BLK, LANES), lambda s, h: (N - 1 - s, 0))
    return pl.pallas_call(
        body, name=name, grid=(N, 4),
        in_specs=[blk(0), blk(4), blk(8), blk(28), pl.BlockSpec((BLK, LANES), lambda s, h: (N - 1 - s, 32)), vec, vec,
                  vec, _const_spec((BLK, BLK)),
                  pl.BlockSpec((None, None, LANES, LANES), lambda s, h: (N - 1 - s, h, 0, 0)), blk(4)],
        out_specs=[blk(0)] * 4 + [col, vec, vec, vec],
        out_shape=[jax.ShapeDtypeStruct((T, 512), F32)] * 4 + [jax.ShapeDtypeStruct((T, LANES), F32)]
        + [jax.ShapeDtypeStruct((1, LANES), F32)] * 3,
        scratch_shapes=[pltpu.VMEM((4, LANES, LANES), F32)],
        compiler_params=_params(("arbitrary", "arbitrary")),
    )(act, act, act, p, p, alog, dtb, onorm, _lincl(), sall, dmixed)


NEG = -1e30


def _swa_block(q, k0, kp, kc, v0, vp, vc, sinkrow, head, n):
    r, c = _iota2((BLK, BLK), 0), _iota2((BLK, BLK), 1)
    m0 = (c >= PAD) & (c <= n * BLK + r)
    mp = (n >= 2) & (c > r)
    mc = (n >= 1) & (r >= c)
    qs = q * (64.0 ** -0.5)
    s0 = jnp.where(m0, bdot(qs, k0.T), NEG)
    sp = jnp.where(mp, bdot(qs, kp.T), NEG)
    sc = jnp.where(mc, bdot(qs, kc.T), NEG)
    sink = _lane_pick(sinkrow, head)
    mx = jnp.maximum(jnp.max(jnp.maximum(jnp.maximum(s0, sp), sc), axis=-1, keepdims=True), sink)
    mx = lax.stop_gradient(mx)
    p0, pp, pc = jnp.exp(s0 - mx), jnp.exp(sp - mx), jnp.exp(sc - mx)
    den = (jnp.sum(p0, axis=-1, keepdims=True) + jnp.sum(pp, axis=-1, keepdims=True)
           + jnp.sum(pc, axis=-1, keepdims=True) + jnp.exp(sink - mx))
    return (bdot(p0, v0) + bdot(pp, vp) + bdot(pc, vc)) / den


def swa_fwd(p2, sinkrow, *, name):
    T = p2.shape[0]
    N = T // BLK

    def body(q, k0, kp, kc, v0, vp, vc, sink_ref, o_ref):
        h, n = pl.program_id(0), pl.program_id(1)
        o_ref[...] = _swa_block(q[...], k0[...], kp[...], kc[...], v0[...], vp[...], vc[...], sink_ref[...], h,
                                n).astype(BF16)

    def kv(off, which):
        row = {"meta": lambda n: 0, "prev": lambda n: jnp.maximum(n - 1, 0), "cur": lambda n: n}[which]
        return pl.BlockSpec((BLK, LANES), lambda h, n: (row(n), off + h // 4))

    return pl.pallas_call(
        body, name=name, grid=(8, N),
        in_specs=[_blk(0), kv(8, "meta"), kv(8, "prev"), kv(8, "cur"), kv(10, "meta"), kv(10, "prev"), kv(10, "cur"),
                  _const_spec((1, LANES))],
        out_specs=_blk(0), out_shape=jax.ShapeDtypeStruct((T, 1024), BF16),
        compiler_params=_params(("arbitrary", "arbitrary")),
    )(p2, p2, p2, p2, p2, p2, p2, sinkrow)


def swa_bwd(p2, sinkrow, dmixed, *, name):
    T = p2.shape[0]
    N = T // BLK

    def body(q, k0, kp, kc, v0, vp, vc, sink_ref, do_ref, dq_ref, dk_ref, dv_ref, dsink_ref):
        kvh, g, n = pl.program_id(0), pl.program_id(1), pl.program_id(2)
        h = 4 * kvh + g

        @pl.when((g == 0) & (n == 0))
        def _():
            dk_ref[...] = jnp.zeros_like(dk_ref)
            dv_ref[...] = jnp.zeros_like(dv_ref)

        @pl.when((kvh == 0) & (g == 0) & (n == 0))
        def _():
            dsink_ref[...] = jnp.zeros_like(dsink_ref)

        f = lambda *a: _swa_block(*a, h, n)
        _, vjp = jax.vjp(f, q[...], k0[...], kp[...], kc[...], v0[...], vp[...], vc[...], sink_ref[...])
        dq, dk0, dkp, dkc, dv0, dvp, dvc, dsink = vjp(do_ref[...])
        dq_ref[...] = dq
        prev = pl.ds(pl.multiple_of(jnp.maximum(n - 1, 0) * BLK, BLK), BLK)
        cur = pl.ds(pl.multiple_of(n * BLK, BLK), BLK)
        for ref, d0, dp, dc in ((dk_ref, dk0, dkp, dkc), (dv_ref, dv0, dvp, dvc)):
            ref[0:BLK, :] += d0
            ref[prev, :] += dp
            ref[cur, :] += dc
        dsink_ref[...] += dsink

    def kv(off, which):
        row = {"meta": lambda n: 0, "prev": lambda n: jnp.maximum(n - 1, 0), "cur": lambda n: n}[which]
        return pl.BlockSpec((BLK, LANES), lambda kvh, g, n: (row(n), off + kvh))

    qspec = pl.BlockSpec((BLK, LANES), lambda kvh, g, n: (n, 4 * kvh + g))
    slab = pl.BlockSpec((T, LANES), lambda kvh, g, n: (0, kvh))
    return pl.pallas_call(
        body, name=name, grid=(2, 4, N),
        in_specs=[qspec, kv(8, "meta"), kv(8, "prev"), kv(8, "cur"), kv(10, "meta"), kv(10, "prev"), kv(10, "cur"),
                  _const_spec((1, LANES)), qspec],
        out_specs=[qspec, slab, slab, _const_spec((1, LANES))],
        out_shape=[jax.ShapeDtypeStruct((T, 1024), F32), jax.ShapeDtypeStruct((T, 256), F32),
                   jax.ShapeDtypeStruct((T, 256), F32), jax.ShapeDtypeStruct((1, LANES), F32)],
        compiler_params=_params(("arbitrary", "arbitrary", "arbitrary")),
    )(p2, p2, p2, p2, p2, p2, p2, sinkrow, dmixed)


def _split_dot(x, m):
    hi = x.astype(BF16)
    lo = (x - hi.astype(F32)).astype(BF16)
    return _nn(hi, m) + _nn(lo, m)


def _tri(strict):
    i = np.arange(BLK)
    m = (i[:, None] > i[None, :]) if strict else (i[:, None] >= i[None, :])
    return jnp.asarray(m.astype(np.float32), dtype=BF16)


def _sb_weights(qb, kj, j, n, carry, after):
    r, c = _iota2((BLK, BLK), 0), _iota2((BLK, BLK), 1)
    z = _nt(qb, kj)
    kpos = j * BLK + c
    valid = (kpos < n * BLK + r) & (kpos >= PAD)
    lb = jnp.minimum(z, 0.0) - jnp.log1p(jnp.exp(-jnp.abs(z)))
    lm = jnp.where(valid, lb - z, 0.0)
    a = jnp.where(valid, jnp.exp(lb + carry + _split_dot(lm, after)), 0.0)
    return valid, lb, lm, a


def sb_fwd(p2, *, name):
    T = p2.shape[0]
    N = T // BLK

    def body(q_ref, k_ref, v_ref, after_ref, o_ref, of_ref):
        n = pl.program_id(1)
        qb = (q_ref[...] * (64.0 ** -0.5)).astype(BF16)
        after = after_ref[...]

        def step(t, c):
            acc, carry = c
            j = n - t
            rows = pl.ds(pl.multiple_of(j * BLK, BLK), BLK)
            _, _, lm, a = _sb_weights(qb, k_ref[rows, :].astype(BF16), j, n, carry, after)
            acc = acc + _split_dot(a, v_ref[rows, :].astype(BF16))
            return acc, carry + jnp.sum(lm, axis=1, keepdims=True)

        acc, _ = lax.fori_loop(0, n + 1, step, (jnp.zeros((BLK, LANES), F32), jnp.zeros((BLK, 1), F32)))
        o_ref[...] = acc.astype(BF16)
        of_ref[...] = acc

    def slab(off):
        return pl.BlockSpec((T, LANES), lambda h, n: (0, off + h))

    return pl.pallas_call(
        body, name=name, grid=(8, N),
        in_specs=[_blk(12), slab(20), slab(28), _const_spec((BLK, BLK))],
        out_specs=[_blk(0), _blk(0)],
        out_shape=[jax.ShapeDtypeStruct((T, 1024), BF16), jax.ShapeDtypeStruct((T, 1024), F32)],
        compiler_params=_params(("arbitrary", "arbitrary")),
    )(p2, p2, p2, _tri(True))


def sb_bwd(p2, o, dmixed, *, name):
    T = p2.shape[0]
    N = T // BLK

    def body(q_ref, k_ref, v_ref, after_ref, from_ref, o_ref, do_ref, dq_ref, dk_ref, dv_ref):
        n = pl.program_id(1)

        @pl.when(n == 0)
        def _():
            dk_ref[...] = jnp.zeros_like(dk_ref)
            dv_ref[...] = jnp.zeros_like(dv_ref)

        q = q_ref[...]
        qb = (q * (64.0 ** -0.5)).astype(BF16)
        qraw = q.astype(BF16)
        do = do_ref[...]
        dob = do.astype(BF16)
        total = jnp.sum(dob.astype(F32) * o_ref[...], axis=1, keepdims=True)
        after, frm = after_ref[...], from_ref[...]

        def step(t, c):
            dq, carry, gcarry = c
            j = n - t
            rows = pl.ds(pl.multiple_of(j * BLK, BLK), BLK)
            kj = k_ref[rows, :].astype(BF16)
            valid, lb, lm, a = _sb_weights(qb, kj, j, n, carry, after)
            g = _nt(dob, v_ref[rows, :].astype(BF16)) * a
            before = total - (gcarry + _split_dot(g, frm))
            beta = jnp.exp(lb)
            dz = (jnp.where(valid, g * (1.0 - beta) - beta * before, 0.0) * (64.0 ** -0.5)).astype(BF16)
            dk_ref[rows, :] += _tn(dz, qraw)
            dv_ref[rows, :] += _tn(a.astype(BF16), dob)
            return (dq + _nn(dz, kj), carry + jnp.sum(lm, axis=1, keepdims=True),
                    gcarry + jnp.sum(g, axis=1, keepdims=True))

        zero = jnp.zeros((BLK, 1), F32)
        dq, _, _ = lax.fori_loop(0, n + 1, step, (jnp.zeros((BLK, LANES), F32), zero, zero))
        dq_ref[...] = dq

    def slab(off):
        return pl.BlockSpec((T, LANES), lambda h, n: (0, off + h))

    return pl.pallas_call(
        body, name=name, grid=(8, N),
        in_specs=[_blk(12), slab(20), slab(28), _const_spec((BLK, BLK)), _const_spec((BLK, BLK)), _blk(0), _blk(8)],
        out_specs=[_blk(0), slab(0), slab(0)],
        out_shape=[jax.ShapeDtypeStruct((T, 1024), F32)] * 3,
        compiler_params=_params(("arbitrary", "arbitrary")),
    )(p2, p2, p2, _tri(True), _tri(False), o, dmixed)


def ffn_fwd(h, g_pre, g_post, wg, wu, wd, tag):
    u, gate, up, act = norm_mm(h, g_pre, (wg, wu), swiglu=True, name=f"ffn_up_{tag}")
    y, h_new = mm_norm_res([act], [wd], h, g_post, 0.5, name=f"ffn_down_{tag}")
    return h_new, (h, u, gate, up, y)


def ffn_bwd(saved, dh, g_pre, g_post, wg, wu, wd, tag):
    h, u, gate, up, y = saved
    dy, dg_post, dgate, dup, act = normbwd_mm_nt(dh, y, g_post, wd, 0.5, (gate, up), name=f"ffn_bwd_down_{tag}")
    dwd = mm_tn(act, dy, name=f"ffn_dwd_{tag}")
    dwg = mm_tn(u, dgate, name=f"ffn_dwg_{tag}")
    dwu = mm_tn(u, dup, name=f"ffn_dwu_{tag}")
    dh_in, dg_pre = mm_nt_normbwd([dgate, dup], [wg, wu], h, g_pre, dh, name=f"ffn_bwd_up_{tag}")
    return dh_in, (dg_pre, dg_post), (dwg, dwu, dwd)


def _lane_row(v):
    v = v.reshape(1, -1)
    return jnp.pad(v, ((0, 0), (0, LANES - v.shape[1])))


AB_WIDTHS = (512,) * 8 + (LANES,)


def mixer_ab_fwd(h, g_pre, g_post, w_in, conv_w, a_log, dt_bias, out_norm, w_out, tables):
    u, p = norm_mm(h, g_pre, (w_in,), swiglu=False, name="ab_in")
    ret, sall_r = retention_fwd(p, tables, name="retention_fwd")
    act = conv_silu_fwd(p, conv_w, name="conv_fwd")
    gdn, sall_g = gdn_fwd(act, p, _lane_row(a_log), _lane_row(dt_bias), out_norm.reshape(1, LANES), name="gdn_fwd")
    y, h_new = mm_norm_res([ret, gdn], [w_out[:512], w_out[512:]], h, g_post, 1.0, name="ab_out")
    return h_new, (h, u, p, ret, sall_r, act, gdn, sall_g, y)


def mixer_ab_bwd(saved, dh, g_pre, g_post, w_in, conv_w, a_log, dt_bias, out_norm, w_out, tables):
    h, u, p, ret, sall_r, act, gdn, sall_g, y = saved
    dy, dg_post, dmixed = normbwd_mm_nt(dh, y, g_post, w_out, 1.0, name="ab_bwd_out")
    dw_out = jnp.concatenate([mm_tn(ret, dy, name="ab_dwout_ret"), mm_tn(gdn, dy, name="ab_dwout_gdn")], axis=0)
    pieces = list(retention_bwd(p, sall_r, dmixed, tables, name="retention_bwd"))
    dqa, dka, dva, dz, dba, dalog, ddtb, donorm = gdn_bwd(
        act, p, _lane_row(a_log), _lane_row(dt_bias), out_norm.reshape(1, LANES), sall_g, dmixed, name="gdn_bwd")
    dconv = []
    for part, dact in enumerate((dqa, dka, dva)):
        dx, dw = conv_silu_bwd(p, conv_w, dact, part, name=f"conv_bwd_{part}")
        pieces.append(dx)
        dconv.append(dw)
    pieces += [dz, dba]
    offs = np.cumsum((0,) + AB_WIDTHS)
    w_parts = [w_in[:, a:b] for a, b in zip(offs[:-1], offs[1:])]
    dh_in, dg_pre = mm_nt_normbwd(pieces, w_parts, h, g_pre, dh, name="ab_bwd_in")
    dw_in = jnp.concatenate([mm_tn(u, pc, name=f"ab_dwin_{i}") for i, pc in enumerate(pieces)], axis=1)
    small = (jnp.concatenate(dconv, axis=1), dalog[:, :4], ddtb[:, :4], donorm)
    return dh_in, (dg_pre, dg_post), (dw_in, dw_out), small


CD_WIDTHS = (1024, 256, 256, 1024, 1024, 1024)


def mixer_cd_fwd(h, g_pre, g_post, w_in, sinks, w_out):
    u, p2 = norm_mm(h, g_pre, (w_in,), swiglu=False, name="cd_in")
    swa = swa_fwd(p2, _lane_row(sinks), name="swa_fwd")
    sb, sb_f32 = sb_fwd(p2, name="sb_fwd")
    y, h_new = mm_norm_res([swa, sb], [w_out[:1024], w_out[1024:]], h, g_post, 1.0, name="cd_out")
    return h_new, (h, u, p2, swa, sb, sb_f32, y)


def mixer_cd_bwd(saved, dh, g_pre, g_post, w_in, sinks, w_out):
    h, u, p2, swa, sb, sb_f32, y = saved
    dy, dg_post, dmixed = normbwd_mm_nt(dh, y, g_post, w_out, 1.0, name="cd_bwd_out")
    dw_out = jnp.concatenate([mm_tn(swa, dy, name="cd_dwout_swa"), mm_tn(sb, dy, name="cd_dwout_sb")], axis=0)
    dq_c, dk_c, dv_c, dsink = swa_bwd(p2, _lane_row(sinks), dmixed, name="swa_bwd")
    pieces = [dq_c, dk_c, dv_c] + list(sb_bwd(p2, sb_f32, dmixed, name="sb_bwd"))
    offs = np.cumsum((0,) + CD_WIDTHS)
    w_parts = [w_in[:, a:b] for a, b in zip(offs[:-1], offs[1:])]
    dh_in, dg_pre = mm_nt_normbwd(pieces, w_parts, h, g_pre, dh, name="cd_bwd_in")
    dw_in = jnp.concatenate([mm_tn(u, pc, name=f"cd_dwin_{i}") for i, pc in enumerate(pieces)], axis=1)
    return dh_in, (dg_pre, dg_post), (dw_in, dw_out), dsink[:, :8]


def _pad_heads(w, axis):
    shape = w.shape
    w = w.reshape(shape[:axis] + (shape[axis] // 64, 64) + shape[axis + 1:])
    pad = [(0, 0)] * w.ndim
    pad[axis + 1] = (0, 64)
    return jnp.pad(w, pad).reshape(shape[:axis] + (2 * shape[axis],) + shape[axis + 1:])


def _unpad_heads(w, axis):
    shape = w.shape
    w = w.reshape(shape[:axis] + (shape[axis] // 128, 128) + shape[axis + 1:])
    w = lax.slice_in_dim(w, 0, 64, axis=axis + 1)
    return w.reshape(shape[:axis] + (shape[axis] // 2,) + shape[axis + 1:])


BIG = (("ffn_w_gate", (2, 2, D, DFF // NDEV), 3), ("ffn_w_up", (2, 2, D, DFF // NDEV), 3),
       ("ffn_w_down", (2, 2, DFF // NDEV, D), 2), ("ab_w_in", (1, D, AB_IN // NDEV), 2),
       ("ab_w_out", (1, D // NDEV, D), 1), ("cd_w_in", (1, D, 288), 2), ("cd_w_out", (1, D // NDEV, D), 1))
SMALL_SHARDED = (("meta_tokens", (NMETA, LANES), 1), ("norm_gains", (2, 6, LANES), 2), ("ab_conv_w", (1, 4, 192), 2))
SMALL_REPL = (("ab_a_log", (1, 4)), ("ab_dt_bias", (1, 4)), ("ab_out_norm", (1, LANES)), ("cd_sinks", (1, 8)))
BIG_ROWS = 5376


def _pack_rows(arrays, width, rows, dtype):
    cat = jnp.concatenate([a.reshape(a.shape[0], -1).astype(dtype) for a in arrays], axis=1)
    cat = jnp.pad(cat, ((0, 0), (0, rows * width - cat.shape[1])))
    return cat.reshape(cat.shape[0], rows, width)


def _unpack_rows(packed, shapes):
    flat = packed.reshape(packed.shape[0], -1)
    out, at = [], 0
    for shape in shapes:
        size = int(np.prod(shape))
        out.append(flat[:, at:at + size].reshape((packed.shape[0],) + tuple(shape)))
        at += size
    return out


def _stack_shards(g, axis):
    full = jnp.moveaxis(g, 0, axis)
    shape = full.shape
    return full.reshape(shape[:axis] + (shape[axis] * shape[axis + 1],) + shape[axis + 2:])


def _split_shards(full, axis):
    shape = full.shape
    g = full.reshape(shape[:axis] + (NDEV, shape[axis] // NDEV) + shape[axis + 1:])
    return jnp.moveaxis(g, axis, 0)


def _pad_rows8(a):
    rows = []
    for x in a:
        flat = x.reshape(x.shape[0], -1)
        n = -(-flat.shape[1] // LANES) * LANES
        rows.append(jnp.pad(flat, ((0, 0), (0, n - flat.shape[1]))).reshape(x.shape[0], n // LANES, LANES))
    cat = jnp.concatenate(rows, axis=1)
    return jnp.pad(cat, ((0, 0), (0, -cat.shape[1] % 8), (0, 0)))


def _unpad_rows8(packed, shapes):
    out, at = [], 0
    for shape in shapes:
        size = int(np.prod(shape))
        nrow = -(-size // LANES)
        blk = packed[:, at:at + nrow].reshape(packed.shape[0], -1)[:, :size]
        out.append(blk.reshape((packed.shape[0],) + tuple(shape)))
        at += nrow
    return out


def kernel(x, meta_tokens, norm_gains, ffn_w_gate, ffn_w_up, ffn_w_down, ab_w_in, ab_conv_w, ab_a_log, ab_dt_bias, ab_out_norm, ab_w_out, cd_w_in, cd_sinks, cd_w_out, loss_target, m_meta_tokens, m_norm_gains, m_ffn_w_gate, m_ffn_w_up, m_ffn_w_down, m_ab_w_in, m_ab_conv_w, m_ab_a_log, m_ab_dt_bias, m_ab_out_norm, m_ab_w_out, m_cd_w_in, m_cd_sinks, m_cd_w_out, v_meta_tokens, v_norm_gains, v_ffn_w_gate, v_ffn_w_up, v_ffn_w_down, v_ab_w_in, v_ab_conv_w, v_ab_a_log, v_ab_dt_bias, v_ab_out_norm, v_ab_w_out, v_cd_w_in, v_cd_sinks, v_cd_w_out):
    w = dict(meta_tokens=meta_tokens, norm_gains=norm_gains, ffn_w_gate=ffn_w_gate, ffn_w_up=ffn_w_up,
             ffn_w_down=ffn_w_down, ab_w_in=ab_w_in, ab_conv_w=ab_conv_w, ab_a_log=ab_a_log, ab_dt_bias=ab_dt_bias,
             ab_out_norm=ab_out_norm, ab_w_out=ab_w_out, cd_w_in=cd_w_in, cd_sinks=cd_sinks, cd_w_out=cd_w_out)
    m = dict(meta_tokens=m_meta_tokens, norm_gains=m_norm_gains, ffn_w_gate=m_ffn_w_gate, ffn_w_up=m_ffn_w_up,
             ffn_w_down=m_ffn_w_down, ab_w_in=m_ab_w_in, ab_conv_w=m_ab_conv_w, ab_a_log=m_ab_a_log,
             ab_dt_bias=m_ab_dt_bias, ab_out_norm=m_ab_out_norm, ab_w_out=m_ab_w_out, cd_w_in=m_cd_w_in,
             cd_sinks=m_cd_sinks, cd_w_out=m_cd_w_out)
    v = dict(meta_tokens=v_meta_tokens, norm_gains=v_norm_gains, ffn_w_gate=v_ffn_w_gate, ffn_w_up=v_ffn_w_up,
             ffn_w_down=v_ffn_w_down, ab_w_in=v_ab_w_in, ab_conv_w=v_ab_conv_w, ab_a_log=v_ab_a_log,
             ab_dt_bias=v_ab_dt_bias, ab_out_norm=v_ab_out_norm, ab_w_out=v_ab_w_out, cd_w_in=v_cd_w_in,
             cd_sinks=v_cd_sinks, cd_w_out=v_cd_w_out)
    order = list(w)
    S = x.shape[1]
    T = S + BLK

    packed = _pack_rows([w[n][None] for n, _, _ in BIG], D, BIG_ROWS, BF16)[0]
    gathered = _unpack_rows(all_gather_big(packed, name="gather_weights"), [s for _, s, _ in BIG])
    full = {n: _stack_shards(g, ax) for (n, _, ax), g in zip(BIG, gathered)}
    small_src = jnp.broadcast_to(_pad_rows8([w[n][None] for n, _, _ in SMALL_SHARDED]), (NDEV, 40, LANES))
    small_all = _unpad_rows8(all_to_all_small(small_src, name="gather_small"), [s for _, s, _ in SMALL_SHARDED])
    full.update({n: _stack_shards(g, ax) for (n, _, ax), g in zip(SMALL_SHARDED, small_all)})

    wg, wu, wd = full["ffn_w_gate"], full["ffn_w_up"], full["ffn_w_down"]
    ab_in = jnp.pad(full["ab_w_in"][0], ((0, 0), (0, AB_INP - AB_IN)))
    ab_out = full["ab_w_out"][0]
    cd_in = _pad_heads(full["cd_w_in"][0], 1)
    cd_out = _pad_heads(full["cd_w_out"][0], 0)
    conv_w = full["ab_conv_w"][0]
    gains = full["norm_gains"].reshape(2, 6, 1, D)
    tables = retention_tables(T)

    h = jnp.concatenate([jnp.zeros((PAD, D), F32), full["meta_tokens"], x[0]], axis=0)
    h, s00 = ffn_fwd(h, gains[0, 0], gains[0, 1], wg[0, 0], wu[0, 0], wd[0, 0], "00")
    h, sab = mixer_ab_fwd(h, gains[0, 2], gains[0, 3], ab_in, conv_w, ab_a_log, ab_dt_bias, ab_out_norm, ab_out, tables)
    h, s01 = ffn_fwd(h, gains[0, 4], gains[0, 5], wg[0, 1], wu[0, 1], wd[0, 1], "01")
    h, s10 = ffn_fwd(h, gains[1, 0], gains[1, 1], wg[1, 0], wu[1, 0], wd[1, 0], "10")
    h, scd = mixer_cd_fwd(h, gains[1, 2], gains[1, 3], cd_in, cd_sinks, cd_out)
    h, s11 = ffn_fwd(h, gains[1, 4], gains[1, 5], wg[1, 1], wu[1, 1], wd[1, 1], "11")
    loss_tile, dh = loss_and_grad(h, loss_target[0], name="loss")
    loss = lax.psum(loss_tile[0, 0], ("x", "y", "c"))

    dgain = [[None] * 6, [None] * 6]
    dffn = {}
    dh, (dgain[1][4], dgain[1][5]), dffn[1, 1] = ffn_bwd(s11, dh, gains[1, 4], gains[1, 5], wg[1, 1], wu[1, 1], wd[1, 1], "11")
    dh, (dgain[1][2], dgain[1][3]), (dcd_in, dcd_out), dsinks = mixer_cd_bwd(scd, dh, gains[1, 2], gains[1, 3], cd_in, cd_sinks, cd_out)
    dh, (dgain[1][0], dgain[1][1]), dffn[1, 0] = ffn_bwd(s10, dh, gains[1, 0], gains[1, 1], wg[1, 0], wu[1, 0], wd[1, 0], "10")
    dh, (dgain[0][4], dgain[0][5]), dffn[0, 1] = ffn_bwd(s01, dh, gains[0, 4], gains[0, 5], wg[0, 1], wu[0, 1], wd[0, 1], "01")
    dh, (dgain[0][2], dgain[0][3]), (dab_in, dab_out), (dconv, dalog, ddtb, donorm) = mixer_ab_bwd(
        sab, dh, gains[0, 2], gains[0, 3], ab_in, conv_w, ab_a_log, ab_dt_bias, ab_out_norm, ab_out, tables)
    dh, (dgain[0][0], dgain[0][1]), dffn[0, 0] = ffn_bwd(s00, dh, gains[0, 0], gains[0, 1], wg[0, 0], wu[0, 0], wd[0, 0], "00")
    grad_x = dh[BLK:][None]

    stack22 = lambda k: jnp.stack([jnp.stack([dffn[i, j][k] for j in range(2)]) for i in range(2)])
    gfull = dict(ffn_w_gate=stack22(0), ffn_w_up=stack22(1), ffn_w_down=stack22(2), ab_w_in=dab_in[None, :, :AB_IN],
                 ab_w_out=dab_out[None], cd_w_in=_unpad_heads(dcd_in, 1)[None], cd_w_out=_unpad_heads(dcd_out, 0)[None],
                 meta_tokens=dh[PAD:BLK], norm_gains=jnp.stack([jnp.concatenate(r, axis=0) for r in dgain]),
                 ab_conv_w=dconv[None])
    gsend = _pack_rows([_split_shards(gfull[n], ax) for n, _, ax in BIG], D, BIG_ROWS, BF16)
    gbig = _unpack_rows(reduce_scatter_big(gsend)[None], [s for _, s, _ in BIG])
    grads = {n: g[0] for (n, _, _), g in zip(BIG, gbig)}
    repl = [jnp.broadcast_to(t[None], (NDEV,) + t.shape) for t in (dalog, ddtb, donorm, dsinks)]
    ssend = _pad_rows8([_split_shards(gfull[n], ax) for n, _, ax in SMALL_SHARDED] + repl)
    ssum = sum_slots(all_to_all_small(ssend, name="exchange_small_grads"), name="sum_small_grads")[None]
    small = _unpad_rows8(ssum, [s for _, s, _ in SMALL_SHARDED] + [s for _, s in SMALL_REPL])
    grads.update({n: g[0] for n, g in zip([n for n, _, _ in SMALL_SHARDED] + [n for n, _ in SMALL_REPL], small)})

    delta, new_m, new_v = {}, {}, {}
    for n in order:
        shape = w[n].shape
        view = (-1, shape[-1])
        d_, m_, v_ = adamw(w[n].reshape(view), grads[n].reshape(view), m[n].reshape(view), v[n].reshape(view),
                           name=f"adamw_{n}")
        delta[n], new_m[n], new_v[n] = d_.reshape(shape), m_.reshape(shape), v_.reshape(shape)
    return (loss, grad_x, *[grads[n] for n in order], *[delta[n] for n in order], *[new_m[n] for n in order],
            *[new_v[n] for n in order])
```

```python
import functools
import math

import numpy as np
import jax
import jax.numpy as jnp
from jax import lax
from jax.experimental import pallas as pl
from jax.experimental.pallas import tpu as pltpu

F32, BF16 = jnp.float32, jnp.bfloat16
EPS = 1e-6
D = 1024
NMETA = 16
BLK = 128
PAD = BLK - NMETA
DFF = 2816
LANES = 128
NDEV = 8
AB_IN, AB_INP = 4104, 4224
ADAM_LR, ADAM_B1, ADAM_B2, ADAM_EPS, ADAM_WD, ADAM_STEP = 0.001, 0.9, 0.999, 1e-08, 0.01, 10
VMEM_LIMIT = 56 * 1024 * 1024
MESH = pl.DeviceIdType.MESH
HIGHEST = lax.Precision.HIGHEST


def _params(sem):
    return pltpu.CompilerParams(dimension_semantics=sem, vmem_limit_bytes=VMEM_LIMIT)


def _row_tile(T, streamed, resident):
    for tm in (640, 320, 128):
        if T % tm == 0 and 2 * (tm * streamed + resident) <= VMEM_LIMIT - 14 * 1024 * 1024:
            return tm
    return _tile(T, 128)


def _tile(n, cap):
    if n <= cap:
        return n
    best = None
    for t in range(LANES, cap + 1, LANES):
        if n % t == 0:
            best = t
    assert best is not None, (n, cap)
    return best


def _rms_fwd(x, g):
    return x * lax.rsqrt(jnp.mean(x * x, axis=-1, keepdims=True) + EPS) * g


def _rms_bwd(x, g, dz):
    r = lax.rsqrt(jnp.mean(x * x, axis=-1, keepdims=True) + EPS)
    xh = x * r
    dg = jnp.sum(dz * xh, axis=0, keepdims=True)
    t = dz * g
    return r * (t - xh * jnp.mean(t * xh, axis=-1, keepdims=True)), dg


def _sigmoid(x):
    return 1.0 / (1.0 + jnp.exp(-x))


def _silu(x):
    return x * _sigmoid(x)


def _nn(a, b, precision=None):
    return lax.dot_general(a, b, (((1,), (0,)), ((), ())), preferred_element_type=F32, precision=precision)


def _nt(a, b):
    return lax.dot_general(a, b, (((1,), (1,)), ((), ())), preferred_element_type=F32)


def _tn(a, b):
    return lax.dot_general(a, b, (((0,), (0,)), ((), ())), preferred_element_type=F32)


@jax.custom_vjp
def bdot(a, b):
    return _nn(a.astype(BF16), b.astype(BF16))


def _bdot_fwd(a, b):
    return bdot(a, b), (a, b)


def _bdot_bwd(res, g):
    a, b = res
    return bdot(g, b.T), bdot(a.T, g)


bdot.defvjp(_bdot_fwd, _bdot_bwd)


@jax.custom_vjp
def hdot(a, b):
    return _nn(a, b, HIGHEST)


def _hdot_fwd(a, b):
    return hdot(a, b), (a, b)


def _hdot_bwd(res, g):
    a, b = res
    return hdot(g, b.T), hdot(a.T, g)


hdot.defvjp(_hdot_fwd, _hdot_bwd)


def _iota2(shape, axis):
    return lax.broadcasted_iota(jnp.int32, shape, axis)


def _lane_pick(row, lane):
    return jnp.sum(jnp.where(_iota2(row.shape, 1) == lane, row, 0.0), axis=1, keepdims=True)


def norm_mm(h, gain, ws, *, swiglu, name, wt=False):
    T, Dm = h.shape
    N = ws[0].shape[0 if wt else 1]
    tm, tn = _tile(T, 640), _tile(N, 1408)
    nw = len(ws)
    mm = _nt if wt else _nn

    def body(h_ref, g_ref, *refs):
        w_refs, u_ref, o_refs = refs[:nw], refs[nw], refs[nw + 1:]

        @pl.when(pl.program_id(1) == 0)
        def _():
            u_ref[...] = _rms_fwd(h_ref[...], g_ref[...]).astype(BF16)

        u = u_ref[...]
        acc = [mm(u, w[...]) for w in w_refs]
        if swiglu:
            o_refs[0][...] = acc[0].astype(BF16)
            o_refs[1][...] = acc[1].astype(BF16)
            o_refs[2][...] = (_silu(acc[0]) * acc[1]).astype(BF16)
        else:
            o_refs[0][...] = acc[0]

    row = pl.BlockSpec((tm, Dm), lambda i, j: (i, 0))
    tile = pl.BlockSpec((tm, tn), lambda i, j: (i, j))
    if swiglu:
        out_shape = [jax.ShapeDtypeStruct((T, Dm), BF16)] + [jax.ShapeDtypeStruct((T, N), BF16)] * 3
        out_specs = [row, tile, tile, tile]
    else:
        out_shape = [jax.ShapeDtypeStruct((T, Dm), BF16), jax.ShapeDtypeStruct((T, N), F32)]
        out_specs = [row, tile]
    return pl.pallas_call(
        body, name=name, grid=(T // tm, N // tn),
        in_specs=[row, pl.BlockSpec((1, Dm), lambda i, j: (0, 0))]
        + [pl.BlockSpec((tn, Dm), lambda i, j: (j, 0)) if wt else pl.BlockSpec((Dm, tn), lambda i, j: (0, j))] * nw,
        out_specs=out_specs, out_shape=out_shape,
        compiler_params=_params(("arbitrary", "arbitrary")),
    )(h, gain, *ws)


def mm_norm_res(As, Ws, h, gain, scale, *, name):
    T, Dm = h.shape
    n = len(As)
    tm = _row_tile(T, sum(a.shape[1] * a.dtype.itemsize for a in As) + 3 * Dm * 4,
                   sum(w.size * w.dtype.itemsize for w in Ws))

    def body(*refs):
        a_refs, w_refs = refs[:n], refs[n:2 * n]
        h_ref, g_ref, y_ref, hn_ref = refs[2 * n:]
        y = _nn(a_refs[0][...].astype(BF16), w_refs[0][...])
        for a, w in zip(a_refs[1:], w_refs[1:]):
            y = y + _nn(a[...].astype(BF16), w[...])
        y_ref[...] = y
        hn_ref[...] = h_ref[...] + scale * _rms_fwd(y, g_ref[...])

    row = pl.BlockSpec((tm, Dm), lambda i: (i, 0))
    return pl.pallas_call(
        body, name=name, grid=(T // tm,),
        in_specs=[pl.BlockSpec((tm, a.shape[1]), lambda i: (i, 0)) for a in As]
        + [pl.BlockSpec(w.shape, lambda i: (0, 0)) for w in Ws]
        + [row, pl.BlockSpec((1, Dm), lambda i: (0, 0))],
        out_specs=[row, row], out_shape=[jax.ShapeDtypeStruct((T, Dm), F32)] * 2,
        compiler_params=_params(("arbitrary",)),
    )(*As, *Ws, h, gain)


def normbwd_mm_nt(dh, y, gain, w, scale, gu=None, *, name):
    T, Dm = dh.shape
    N = w.shape[0]
    tm, tn = _tile(T, 640), _tile(N, 1408)
    swiglu = gu is not None

    def body(dh_ref, y_ref, g_ref, w_ref, *refs):
        if swiglu:
            gate_ref, up_ref, dy_ref, dg_ref, dgate_ref, dup_ref, a_ref = refs
        else:
            dy_ref, dg_ref, da_ref = refs
        i, j = pl.program_id(0), pl.program_id(1)

        @pl.when(j == 0)
        def _():
            dy, dg = _rms_bwd(y_ref[...], g_ref[...], scale * dh_ref[...])
            dy_ref[...] = dy.astype(BF16)

            @pl.when(i == 0)
            def _():
                dg_ref[...] = jnp.zeros_like(dg_ref)

            dg_ref[...] += dg

        da = _nt(dy_ref[...], w_ref[...])
        if swiglu:
            gate, up = gate_ref[...].astype(F32), up_ref[...].astype(F32)
            s = _sigmoid(gate)
            dgate_ref[...] = (da * up * s * (1.0 + gate * (1.0 - s))).astype(BF16)
            dup_ref[...] = (da * gate * s).astype(BF16)
            a_ref[...] = (gate * s * up).astype(BF16)
        else:
            da_ref[...] = da

    row = pl.BlockSpec((tm, Dm), lambda i, j: (i, 0))
    vec = pl.BlockSpec((1, Dm), lambda i, j: (0, 0))
    tile = pl.BlockSpec((tm, tn), lambda i, j: (i, j))
    in_specs = [row, row, vec, pl.BlockSpec((tn, Dm), lambda i, j: (j, 0))]
    out_shape = [jax.ShapeDtypeStruct((T, Dm), BF16), jax.ShapeDtypeStruct((1, Dm), F32)]
    if swiglu:
        in_specs += [tile, tile]
        out_shape += [jax.ShapeDtypeStruct((T, N), BF16)] * 3
        out_specs = [row, vec, tile, tile, tile]
        args = (dh, y, gain, w, *gu)
    else:
        out_shape += [jax.ShapeDtypeStruct((T, N), F32)]
        out_specs = [row, vec, tile]
        args = (dh, y, gain, w)
    return pl.pallas_call(
        body, name=name, grid=(T // tm, N // tn), in_specs=in_specs, out_specs=out_specs,
        out_shape=out_shape, compiler_params=_params(("arbitrary", "arbitrary")),
    )(*args)


def mm_nt_normbwd(dPs, Ws, h, gain, dh_in, *, name, wt=False):
    T, Dm = h.shape
    n = len(dPs)
    tm = _row_tile(T, sum(p.shape[1] * p.dtype.itemsize for p in dPs) + 3 * Dm * 4,
                   sum(w.size * w.dtype.itemsize for w in Ws))
    mm = _nn if wt else _nt

    def body(*refs):
        p_refs, w_refs = refs[:n], refs[n:2 * n]
        h_ref, g_ref, dhin_ref, dh_ref, dg_ref = refs[2 * n:]
        du = mm(p_refs[0][...].astype(BF16), w_refs[0][...])
        for p, w in zip(p_refs[1:], w_refs[1:]):
            du = du + mm(p[...].astype(BF16), w[...])
        dx, dg = _rms_bwd(h_ref[...], g_ref[...], du)
        dh_ref[...] = dhin_ref[...] + dx

        @pl.when(pl.program_id(0) == 0)
        def _():
            dg_ref[...] = jnp.zeros_like(dg_ref)

        dg_ref[...] += dg

    row = pl.BlockSpec((tm, Dm), lambda i: (i, 0))
    vec = pl.BlockSpec((1, Dm), lambda i: (0, 0))
    return pl.pallas_call(
        body, name=name, grid=(T // tm,),
        in_specs=[pl.BlockSpec((tm, p.shape[1]), lambda i: (i, 0)) for p in dPs]
        + [pl.BlockSpec(w.shape, lambda i: (0, 0)) for w in Ws] + [row, vec, row],
        out_specs=[row, vec],
        out_shape=[jax.ShapeDtypeStruct((T, Dm), F32), jax.ShapeDtypeStruct((1, Dm), F32)],
        compiler_params=_params(("arbitrary",)),
    )(*dPs, *Ws, h, gain, dh_in)


def mm_tn(a, b, *, name):
    T, M = a.shape
    N = b.shape[1]
    tm, tn, tk = _tile(M, 1408), _tile(N, 1408), _tile(T, 640)

    def body(a_ref, b_ref, o_ref):
        @pl.when(pl.program_id(2) == 0)
        def _():
            o_ref[...] = jnp.zeros_like(o_ref)

        o_ref[...] += _tn(a_ref[...].astype(BF16), b_ref[...].astype(BF16))

    return pl.pallas_call(
        body, name=name, grid=(M // tm, N // tn, T // tk),
        in_specs=[pl.BlockSpec((tk, tm), lambda i, j, k: (k, i)), pl.BlockSpec((tk, tn), lambda i, j, k: (k, j))],
        out_specs=pl.BlockSpec((tm, tn), lambda i, j, k: (i, j)),
        out_shape=jax.ShapeDtypeStruct((M, N), F32),
        compiler_params=_params(("arbitrary", "arbitrary", "arbitrary")),
    )(a, b)


def loss_and_grad(h, target, *, name):
    T, Dm = h.shape

    def body(h_ref, t_ref, loss_ref, dh_ref):
        b = pl.program_id(0)

        @pl.when(b == 0)
        def _():
            loss_ref[...] = jnp.zeros_like(loss_ref)
            dh_ref[...] = jnp.zeros_like(dh_ref)

        @pl.when(b > 0)
        def _():
            e = h_ref[...] - t_ref[...]
            dh_ref[...] = e * (1.0 / Dm)
            loss_ref[...] += jnp.sum(e * e) * (0.5 / Dm)

    return pl.pallas_call(
        body, name=name, grid=(T // BLK,),
        in_specs=[pl.BlockSpec((BLK, Dm), lambda b: (b, 0)),
                  pl.BlockSpec((BLK, Dm), lambda b: (jnp.maximum(b - 1, 0), 0))],
        out_specs=[pl.BlockSpec((8, LANES), lambda b: (0, 0)), pl.BlockSpec((BLK, Dm), lambda b: (b, 0))],
        out_shape=[jax.ShapeDtypeStruct((8, LANES), F32), jax.ShapeDtypeStruct((T, Dm), F32)],
        compiler_params=_params(("arbitrary",)),
    )(h, target)


def adamw(w, g, m, v, *, name):
    R, C = w.shape
    tr = R
    for t in (512, 352, 256):
        if R > t and R % t == 0:
            tr = t
            break

    def body(w_ref, g_ref, m_ref, v_ref, d_ref, nm_ref, nv_ref):
        g_ = g_ref[...]
        m_ = ADAM_B1 * m_ref[...] + (1.0 - ADAM_B1) * g_
        v_ = ADAM_B2 * v_ref[...] + (1.0 - ADAM_B2) * (g_ * g_)
        m_hat = m_ / (1.0 - ADAM_B1 ** ADAM_STEP)
        v_hat = v_ / (1.0 - ADAM_B2 ** ADAM_STEP)
        d_ref[...] = -ADAM_LR * (m_hat / (jnp.sqrt(v_hat) + ADAM_EPS) + ADAM_WD * w_ref[...])
        nm_ref[...] = m_
        nv_ref[...] = v_

    spec = pl.BlockSpec((tr, C), lambda i: (i, 0))
    return pl.pallas_call(
        body, name=name, grid=(R // tr,), in_specs=[spec] * 4, out_specs=[spec] * 3,
        out_shape=[jax.ShapeDtypeStruct((R, C), F32)] * 3, compiler_params=_params(("arbitrary",)),
    )(w, g, m, v)


def _me():
    return lax.axis_index("x"), lax.axis_index("y"), lax.axis_index("c")


def _flip(pos, rel):
    return tuple(1 - p if r else p for p, r in zip(pos, rel))


def _slot(pos):
    return 4 * pos[0] + 2 * pos[1] + pos[2]


HBM_SPEC = pl.BlockSpec(memory_space=pltpu.HBM)
CHIP_RELS = ((1, 0), (0, 1), (1, 1))


def all_gather_big(xs, *, name):
    n = len(xs)

    def body(*refs):
        x_refs, out_refs = refs[:n], refs[n:2 * n]
        send_sems, recv_sems, local_sems = refs[2 * n:]
        me = _me()
        sibling = _flip(me, (0, 0, 1))
        chips = [_flip(me, rel + (0,)) for rel in CHIP_RELS]

        def copy(i, k, block, to, src=None):
            dst = out_refs[i].at[_slot(block)]
            return pltpu.make_async_remote_copy(
                src_ref=dst if src is None else src, dst_ref=dst, send_sem=send_sems.at[i, k],
                recv_sem=recv_sems.at[i, k], device_id=to, device_id_type=MESH)

        sent, local = [], []
        for i in range(n):
            mine = pltpu.make_async_copy(x_refs[i], out_refs[i].at[_slot(me)], local_sems.at[i])
            mine.start()
            local.append(mine)
            sent += [copy(i, 0, me, sibling, src=x_refs[i])]
            sent += [copy(i, 1 + j, me, chip, src=x_refs[i]) for j, chip in enumerate(chips)]
        for cp in sent:
            cp.start()
        for i in range(n):
            for j, chip in enumerate(chips):
                copy(i, 1 + j, chip, me).wait_recv()
                passed = copy(i, 4 + j, chip, sibling)
                passed.start()
                sent.append(passed)
        for i in range(n):
            copy(i, 0, sibling, me).wait_recv()
            for j, chip in enumerate(chips):
                copy(i, 4 + j, _flip(chip, (0, 0, 1)), me).wait_recv()
        for cp in sent:
            cp.wait_send()
        for mine in local:
            mine.wait()

    return pl.pallas_call(
        body, name=name, in_specs=[HBM_SPEC] * n, out_specs=[HBM_SPEC] * n,
        out_shape=[jax.ShapeDtypeStruct((NDEV,) + x.shape, x.dtype) for x in xs],
        scratch_shapes=[pltpu.SemaphoreType.DMA((n, 7)), pltpu.SemaphoreType.DMA((n, 7)), pltpu.SemaphoreType.DMA((n,))],
    )(*xs)


def all_to_all_small(src, *, name):
    _, r, C = src.shape

    def body(src_ref, out_ref, send_sems, recv_sems):
        me = _me()
        my = _slot(me)
        out_ref[my] = src_ref[my]
        copies = []
        for k in range(1, NDEV):
            peer = _flip(me, ((k >> 2) & 1, (k >> 1) & 1, k & 1))
            cp = pltpu.make_async_remote_copy(
                src_ref=src_ref.at[_slot(peer)], dst_ref=out_ref.at[my], send_sem=send_sems.at[k - 1],
                recv_sem=recv_sems.at[k - 1], device_id=peer, device_id_type=MESH)
            cp.start()
            copies.append((cp, peer))
        for k, (cp, peer) in enumerate(copies):
            pltpu.make_async_remote_copy(
                src_ref=src_ref.at[my], dst_ref=out_ref.at[_slot(peer)], send_sem=send_sems.at[k],
                recv_sem=recv_sems.at[k], device_id=peer, device_id_type=MESH).wait_recv()
        for cp, _ in copies:
            cp.wait_send()

    vm = pl.BlockSpec(memory_space=pltpu.VMEM)
    return pl.pallas_call(
        body, name=name, in_specs=[vm], out_specs=vm, out_shape=jax.ShapeDtypeStruct(src.shape, src.dtype),
        scratch_shapes=[pltpu.SemaphoreType.DMA((7,)), pltpu.SemaphoreType.DMA((7,))],
    )(src)


def sum_slots(a, *, name):
    n, r, C = a.shape

    def body(a_ref, o_ref):
        s = a_ref[0]
        for k in range(1, n):
            s = s + a_ref[k]
        o_ref[...] = s

    vm = pl.BlockSpec(memory_space=pltpu.VMEM)
    return pl.pallas_call(body, name=name, in_specs=[vm], out_specs=vm,
                          out_shape=jax.ShapeDtypeStruct((r, C), F32))(a)


def rs_exchange_sibling(gs, *, name):
    n = len(gs)

    def body(*refs):
        g_refs, out_refs, send_sems, recv_sems = refs[:n], refs[n:2 * n], refs[2 * n], refs[2 * n + 1]
        sibling = _flip(_me(), (0, 0, 1))
        copies = []
        for i in range(n):
            for chip in range(4):
                cp = pltpu.make_async_remote_copy(
                    src_ref=g_refs[i].at[2 * chip + sibling[2]], dst_ref=out_refs[i].at[chip],
                    send_sem=send_sems.at[i, chip], recv_sem=recv_sems.at[i, chip], device_id=sibling,
                    device_id_type=MESH)
                cp.start()
                copies.append(cp)
        for cp in copies:
            cp.wait()

    return pl.pallas_call(
        body, name=name, in_specs=[HBM_SPEC] * n, out_specs=[HBM_SPEC] * n,
        out_shape=[jax.ShapeDtypeStruct((4,) + g.shape[1:], g.dtype) for g in gs],
        scratch_shapes=[pltpu.SemaphoreType.DMA((n, 4)), pltpu.SemaphoreType.DMA((n, 4))],
    )(*gs)


def rs_chip_partials(g, got, *, name):
    _, R, C = g.shape
    tr = _tile(R, 768)

    def body(c_ref, g_ref, got_ref, o_ref):
        o_ref[...] = (g_ref[...].astype(F32) + got_ref[...].astype(F32)).astype(o_ref.dtype)

    c = jnp.reshape(lax.axis_index("c"), (1,)).astype(jnp.int32)
    return pl.pallas_call(
        body, name=name,
        grid_spec=pltpu.PrefetchScalarGridSpec(
            num_scalar_prefetch=1, grid=(4, R // tr),
            in_specs=[pl.BlockSpec((None, tr, C), lambda k, i, c_ref: (2 * k + c_ref[0], i, 0)),
                      pl.BlockSpec((None, tr, C), lambda k, i, c_ref: (k, i, 0))],
            out_specs=pl.BlockSpec((None, tr, C), lambda k, i, c_ref: (k, i, 0))),
        out_shape=jax.ShapeDtypeStruct((4, R, C), g.dtype), compiler_params=_params(("arbitrary", "arbitrary")),
    )(c, g, got)


def rs_exchange_chips(ps, *, name):
    n = len(ps)

    def body(*refs):
        p_refs, out_refs, send_sems, recv_sems = refs[:n], refs[n:2 * n], refs[2 * n], refs[2 * n + 1]
        me = _me()
        copies = []
        for i in range(n):
            for j, rel in enumerate(CHIP_RELS):
                peer = _flip(me, rel + (0,))
                cp = pltpu.make_async_remote_copy(
                    src_ref=p_refs[i].at[2 * peer[0] + peer[1]], dst_ref=out_refs[i].at[j], send_sem=send_sems.at[i, j],
                    recv_sem=recv_sems.at[i, j], device_id=peer, device_id_type=MESH)
                cp.start()
                copies.append(cp)
        for cp in copies:
            cp.wait()

    return pl.pallas_call(
        body, name=name, in_specs=[HBM_SPEC] * n, out_specs=[HBM_SPEC] * n,
        out_shape=[jax.ShapeDtypeStruct((3,) + p.shape[1:], p.dtype) for p in ps],
        scratch_shapes=[pltpu.SemaphoreType.DMA((n, 3)), pltpu.SemaphoreType.DMA((n, 3))],
    )(*ps)


def rs_final_sum(p, got, *, name):
    _, R, C = p.shape
    tr = _tile(R, 768)

    def body(chip_ref, p_ref, got_ref, o_ref):
        s = p_ref[...].astype(F32)
        for j in range(3):
            s = s + got_ref[j].astype(F32)
        o_ref[...] = s

    mychip = jnp.reshape(2 * lax.axis_index("x") + lax.axis_index("y"), (1,)).astype(jnp.int32)
    return pl.pallas_call(
        body, name=name,
        grid_spec=pltpu.PrefetchScalarGridSpec(
            num_scalar_prefetch=1, grid=(R // tr,),
            in_specs=[pl.BlockSpec((None, tr, C), lambda i, chip_ref: (chip_ref[0], i, 0)),
                      pl.BlockSpec((3, tr, C), lambda i, chip_ref: (0, i, 0))],
            out_specs=pl.BlockSpec((tr, C), lambda i, chip_ref: (i, 0))),
        out_shape=jax.ShapeDtypeStruct((R, C), F32), compiler_params=_params(("arbitrary",)),
    )(mychip, p, got)


def reduce_scatter_big(gs):
    got = rs_exchange_sibling(gs, name="rs_sibling")
    parts = [rs_chip_partials(g, t, name=f"rs_chip_partials_{i}") for i, (g, t) in enumerate(zip(gs, got))]
    got2 = rs_exchange_chips(parts, name="rs_chips")
    return [rs_final_sum(p, t, name=f"rs_final_sum_{i}") for i, (p, t) in enumerate(zip(parts, got2))]


def _blk(off):
    return pl.BlockSpec((BLK, LANES), lambda h, n: (n, off + h))


def _const_spec(shape):
    return pl.BlockSpec(shape, lambda *_: (0,) * len(shape))


def retention_tables(T):
    pos = jnp.arange(T, dtype=F32) - float(PAD)
    inv_freq = 1.0 / (10000.0 ** jnp.linspace(0.0, 1.0, 64, dtype=F32))
    ang = pos[:, None] * inv_freq[None, :]
    cos = jnp.repeat(jnp.cos(ang), 2, axis=1)
    sin = jnp.repeat(jnp.sin(ang), 2, axis=1) * jnp.tile(jnp.array([-1.0, 1.0], F32), 64)[None, :]
    lane = np.arange(LANES)
    perm = jnp.asarray((lane[:, None] == (lane[None, :] ^ 1)).astype(np.float32))
    log_gamma = jnp.log1p(-jnp.exp2(-5.0 - jnp.arange(4, dtype=F32)))
    idx = jnp.arange(BLK, dtype=F32)
    diff = idx[:, None] - idx[None, :]
    intra = jnp.where(diff >= 0, jnp.exp(jnp.maximum(diff, 0.0) * log_gamma[:, None, None]), 0.0)
    zeta = jnp.exp((BLK - 1.0 - idx)[None, :] * log_gamma[:, None])
    xi = jnp.exp((idx + 1.0)[None, :] * log_gamma[:, None])
    bc = lambda t: jnp.broadcast_to(t[:, :, None], (4, BLK, LANES))
    return cos, sin, perm, jnp.stack([intra, bc(zeta), bc(xi)], axis=1)


def _ret_chunk(rq, rk, rv, rg, S, cos, sin, tab, perm):
    intra, zeta, xi = tab[0], tab[1], tab[2]
    q = rq * cos + hdot(rq, perm) * sin
    k = (rk * cos + hdot(rk, perm) * sin) * (128.0 ** -0.5)
    ret = bdot(bdot(q, k.T) * intra, rv) + bdot(q * xi, S)
    S_new = S * xi[BLK - 1:BLK, :] + bdot((k * zeta).T, rv)
    c = ret - jnp.mean(ret, axis=-1, keepdims=True)
    out = c * lax.rsqrt(jnp.mean(c * c, axis=-1, keepdims=True) + EPS) * _silu(rg)
    return out, S_new


def retention_fwd(p, tables, *, name):
    T = p.shape[0]
    N = T // BLK
    cos, sin, perm, tab = tables

    def body(rq, rk, rv, rg, cos_ref, sin_ref, tab_ref, perm_ref, out_ref, sall_ref, s_scr):
        @pl.when(pl.program_id(1) == 0)
        def _():
            s_scr[...] = jnp.zeros_like(s_scr)

        S = s_scr[...]
        sall_ref[...] = S
        out, S_new = _ret_chunk(rq[...], rk[...], rv[...], rg[...], S, cos_ref[...], sin_ref[...], tab_ref[...],
                                perm_ref[...])
        out_ref[...] = out.astype(BF16)
        s_scr[...] = S_new

    rowtab = pl.BlockSpec((BLK, LANES), lambda h, n: (n, 0))
    return pl.pallas_call(
        body, name=name, grid=(4, N),
        in_specs=[_blk(0), _blk(4), _blk(8), _blk(12), rowtab, rowtab,
                  pl.BlockSpec((None, 3, BLK, LANES), lambda h, n: (h, 0, 0, 0)), _const_spec((LANES, LANES))],
        out_specs=[_blk(0), pl.BlockSpec((None, None, LANES, LANES), lambda h, n: (h, n, 0, 0))],
        out_shape=[jax.ShapeDtypeStruct((T, 512), BF16), jax.ShapeDtypeStruct((4, N, LANES, LANES), F32)],
        scratch_shapes=[pltpu.VMEM((LANES, LANES), F32)],
        compiler_params=_params(("arbitrary", "arbitrary")),
    )(p, p, p, p, cos, sin, tab, perm)


def _row_mask(n):
    return (n * BLK + _iota2((BLK, 1), 0) >= PAD).astype(F32)


def retention_bwd(p, sall, dmixed, tables, *, name):
    T = p.shape[0]
    N = T // BLK
    cos, sin, perm, tab = tables

    def body(rq, rk, rv, rg, cos_ref, sin_ref, tab_ref, perm_ref, sall_ref, do_ref, drq, drk, drv, drg, ds_scr):
        n = N - 1 - pl.program_id(1)

        @pl.when(pl.program_id(1) == 0)
        def _():
            ds_scr[...] = jnp.zeros_like(ds_scr)

        f = lambda a, b, c, d, s: _ret_chunk(a, b, c, d, s, cos_ref[...], sin_ref[...], tab_ref[...], perm_ref[...])
        _, vjp = jax.vjp(f, rq[...], rk[...], rv[...], rg[...], sall_ref[...])
        g = vjp((do_ref[...], ds_scr[...]))
        mask = _row_mask(n)
        for ref, val in zip((drq, drk, drv, drg), g[:4]):
            ref[...] = val * mask
        ds_scr[...] = g[4]

    def rblk(off):
        return pl.BlockSpec((BLK, LANES), lambda h, n: (N - 1 - n, off + h))

    rowtab = pl.BlockSpec((BLK, LANES), lambda h, n: (N - 1 - n, 0))
    return pl.pallas_call(
        body, name=name, grid=(4, N),
        in_specs=[rblk(0), rblk(4), rblk(8), rblk(12), rowtab, rowtab,
                  pl.BlockSpec((None, 3, BLK, LANES), lambda h, n: (h, 0, 0, 0)), _const_spec((LANES, LANES)),
                  pl.BlockSpec((None, None, LANES, LANES), lambda h, n: (h, N - 1 - n, 0, 0)), rblk(0)],
        out_specs=[rblk(0)] * 4, out_shape=[jax.ShapeDtypeStruct((T, 512), F32)] * 4,
        scratch_shapes=[pltpu.VMEM((LANES, LANES), F32)],
        compiler_params=_params(("arbitrary", "arbitrary")),
    )(p, p, p, p, cos, sin, tab, perm, sall, dmixed)


def conv_silu_fwd(p, w, *, name):
    T = p.shape[0]
    N = T // BLK

    def body(x_ref, xp_ref, w_ref, o_ref):
        n = pl.program_id(0)
        cur = x_ref[...]
        cat = jnp.concatenate([jnp.where(n > 0, xp_ref[...], 0.0), cur], axis=0)
        y = w_ref[3:4, :] * cur
        for s in (1, 2, 3):
            y = y + w_ref[3 - s:4 - s, :] * pltpu.roll(cat, s, 0)[BLK:]
        o_ref[...] = _silu(y)

    return pl.pallas_call(
        body, name=name, grid=(N, 12),
        in_specs=[pl.BlockSpec((BLK, LANES), lambda n, c: (n, 16 + c)),
                  pl.BlockSpec((BLK, LANES), lambda n, c: (jnp.maximum(n - 1, 0), 16 + c)),
                  pl.BlockSpec((4, LANES), lambda n, c: (0, c))],
        out_specs=pl.BlockSpec((BLK, LANES), lambda n, c: (n, c)),
        out_shape=jax.ShapeDtypeStruct((T, 1536), F32), compiler_params=_params(("arbitrary", "arbitrary")),
    )(p, p, w)


def conv_silu_bwd(p, w, dact, part, *, name):
    T = p.shape[0]
    N = T // BLK
    off = 16 + 4 * part

    def body(xp_ref, x_ref, xn_ref, w_ref, da_ref, dan_ref, dx_ref, dw_ref):
        n = pl.program_id(1)
        last = n == N - 1
        cat = jnp.concatenate([jnp.where(n > 0, xp_ref[...], 0.0), x_ref[...], jnp.where(last, 0.0, xn_ref[...])], axis=0)
        shifted = [cat] + [pltpu.roll(cat, s, 0) for s in (1, 2, 3)]
        y = w_ref[3:4, :] * shifted[0]
        for s in (1, 2, 3):
            y = y + w_ref[3 - s:4 - s, :] * shifted[s]
        y = y[BLK:]
        da = jnp.concatenate([da_ref[...], jnp.where(last, 0.0, dan_ref[...])], axis=0)
        sg = _sigmoid(y)
        dy = da * sg * (1.0 + y * (1.0 - sg))
        dx = w_ref[3:4, :] * dy[:BLK]
        for s in (1, 2, 3):
            dx = dx + w_ref[3 - s:4 - s, :] * pltpu.roll(dy, 2 * BLK - s, 0)[:BLK]
        dx_ref[...] = dx * _row_mask(n)

        @pl.when(n == 0)
        def _():
            dw_ref[...] = jnp.zeros_like(dw_ref)

        for s in (0, 1, 2, 3):
            dw_ref[3 - s:4 - s, :] += jnp.sum(dy[:BLK] * shifted[s][BLK:2 * BLK], axis=0, keepdims=True)

    def xs(d):
        return pl.BlockSpec((BLK, LANES), lambda c, n: (jnp.clip(n + d, 0, N - 1), off + c))

    return pl.pallas_call(
        body, name=name, grid=(4, N),
        in_specs=[xs(-1), xs(0), xs(1), pl.BlockSpec((4, LANES), lambda c, n: (0, 4 * part + c)),
                  pl.BlockSpec((BLK, LANES), lambda c, n: (n, c)),
                  pl.BlockSpec((BLK, LANES), lambda c, n: (jnp.minimum(n + 1, N - 1), c))],
        out_specs=[pl.BlockSpec((BLK, LANES), lambda c, n: (n, c)), pl.BlockSpec((4, LANES), lambda c, n: (0, c))],
        out_shape=[jax.ShapeDtypeStruct((T, 512), F32), jax.ShapeDtypeStruct((4, 512), F32)],
        compiler_params=_params(("arbitrary", "arbitrary")),
    )(p, p, p, w, dact, dact)


def _softplus(x):
    return jnp.maximum(x, 0.0) + jnp.log1p(jnp.exp(-jnp.abs(x)))


def _gdn_chunk(qa, ka, va, z, ba, S, alog, dtb, onorm, head, rowmask, lincl):
    r, c = _iota2((BLK, BLK), 0), _iota2((BLK, BLK), 1)
    incl, strict = r >= c, r > c
    eye = (r == c).astype(F32)
    q = qa * lax.rsqrt(jnp.sum(qa * qa, axis=-1, keepdims=True) + EPS) * (128.0 ** -0.5)
    k = ka * lax.rsqrt(jnp.sum(ka * ka, axis=-1, keepdims=True) + EPS)
    beta = _sigmoid(_lane_pick(ba, head)) * rowmask
    a = -jnp.exp(_lane_pick(alog, head))
    g = a * _softplus(_lane_pick(ba, 4 + head) + _lane_pick(dtb, head)) * rowmask
    gc = hdot(lincl, jnp.broadcast_to(g, (BLK, LANES)))
    decay = jnp.where(incl, jnp.exp(jnp.where(incl, gc - gc.T, 0.0)), 0.0)
    kb = k * beta
    amat = jnp.where(strict, bdot(kb, k.T) * decay, 0.0)
    m = -amat
    inv = eye + m
    pw = hdot(m, m)
    for t in range(6):
        inv = inv + hdot(inv, pw)
        if t < 5:
            pw = hdot(pw, pw)
    egc = jnp.exp(gc)
    u = hdot(inv, va * beta)
    w = hdot(inv, kb * egc)
    qk = jnp.where(incl, bdot(q, k.T) * decay, 0.0)
    glast = gc[BLK - 1:BLK, :]
    vnew = u - bdot(w, S)
    o = bdot(q * egc, S) + bdot(qk, vnew)
    S_new = S * jnp.exp(glast) + bdot((k * jnp.exp(glast - gc)).T, vnew)
    out = o * lax.rsqrt(jnp.mean(o * o, axis=-1, keepdims=True) + EPS) * onorm * _silu(z)
    return out, S_new


def _lincl():
    i = np.arange(BLK)
    return jnp.asarray((i[:, None] >= i[None, :]).astype(np.float32))


def gdn_fwd(act, p, alog, dtb, onorm, *, name):
    T = p.shape[0]
    N = T // BLK

    def body(qa, ka, va, z, ba, alog_ref, dtb_ref, on_ref, l_ref, out_ref, sall_ref, s_scr):
        n, h = pl.program_id(0), pl.program_id(1)

        @pl.when(n == 0)
        def _():
            s_scr[h] = jnp.zeros((LANES, LANES), F32)

        S = s_scr[h]
        sall_ref[...] = S
        out, S_new = _gdn_chunk(qa[...], ka[...], va[...], z[...], ba[...], S, alog_ref[...], dtb_ref[...],
                                on_ref[...], h, _row_mask(n), l_ref[...])
        out_ref[...] = out.astype(BF16)
        s_scr[h] = S_new

    def blk(off):
        return pl.BlockSpec((BLK, LANES), lambda n, h: (n, off + h))

    vec = _const_spec((1, LANES))
    return pl.pallas_call(
        body, name=name, grid=(N, 4),
        in_specs=[blk(0), blk(4), blk(8), blk(28), pl.BlockSpec((BLK, LANES), lambda n, h: (n, 32)), vec, vec, vec,
                  _const_spec((BLK, BLK))],
        out_specs=[blk(0), pl.BlockSpec((None, None, LANES, LANES), lambda n, h: (n, h, 0, 0))],
        out_shape=[jax.ShapeDtypeStruct((T, 512), BF16), jax.ShapeDtypeStruct((N, 4, LANES, LANES), F32)],
        scratch_shapes=[pltpu.VMEM((4, LANES, LANES), F32)],
        compiler_params=_params(("arbitrary", "arbitrary")),
    )(act, act, act, p, p, alog, dtb, onorm, _lincl())


def gdn_bwd(act, p, alog, dtb, onorm, sall, dmixed, *, name):
    T = p.shape[0]
    N = T // BLK

    def body(qa, ka, va, z, ba, alog_ref, dtb_ref, on_ref, l_ref, sall_ref, do_ref,
             dq_ref, dk_ref, dv_ref, dz_ref, dba_ref, dal_ref, ddt_ref, don_ref, ds_scr):
        step, h = pl.program_id(0), pl.program_id(1)
        n = N - 1 - step

        @pl.when(step == 0)
        def _():
            ds_scr[h] = jnp.zeros((LANES, LANES), F32)

        @pl.when((step == 0) & (h == 0))
        def _():
            dal_ref[...] = jnp.zeros_like(dal_ref)
            ddt_ref[...] = jnp.zeros_like(ddt_ref)
            don_ref[...] = jnp.zeros_like(don_ref)

        @pl.when(h == 0)
        def _():
            dba_ref[...] = jnp.zeros_like(dba_ref)

        rowmask, lincl = _row_mask(n), l_ref[...]
        f = lambda *a: _gdn_chunk(*a, h, rowmask, lincl)
        _, vjp = jax.vjp(f, qa[...], ka[...], va[...], z[...], ba[...], sall_ref[...], alog_ref[...], dtb_ref[...],
                         on_ref[...])
        g = vjp((do_ref[...], ds_scr[h]))
        dq_ref[...] = g[0] * rowmask
        dk_ref[...] = g[1] * rowmask
        dv_ref[...] = g[2] * rowmask
        dz_ref[...] = g[3] * rowmask
        dba_ref[...] += g[4] * rowmask
        ds_scr[h] = g[5]
        dal_ref[...] += g[6]
        ddt_ref[...] += g[7]
        don_ref[...] += g[8]

    def blk(off):
        return pl.BlockSpec((BLK, LANES), lambda s, h: (N - 1 - s, off + h))

    vec = _const_spec((1, LANES))
    col = pl.BlockSpec((BLK, LANES), lambda s, h: (N - 1 - s, 0))
    return pl.pallas_call(
        body, name=name, grid=(N, 4),
        in_specs=[blk(0), blk(4), blk(8), blk(28), pl.BlockSpec((BLK, LANES), lambda s, h: (N - 1 - s, 32)), vec, vec,
                  vec, _const_spec((BLK, BLK)),
                  pl.BlockSpec((None, None, LANES, LANES), lambda s, h: (N - 1 - s, h, 0, 0)), blk(4)],
        out_specs=[blk(0)] * 4 + [col, vec, vec, vec],
        out_shape=[jax.ShapeDtypeStruct((T, 512), F32)] * 4 + [jax.ShapeDtypeStruct((T, LANES), F32)]
        + [jax.ShapeDtypeStruct((1, LANES), F32)] * 3,
        scratch_shapes=[pltpu.VMEM((4, LANES, LANES), F32)],
        compiler_params=_params(("arbitrary", "arbitrary")),
    )(act, act, act, p, p, alog, dtb, onorm, _lincl(), sall, dmixed)


NEG = -1e30


def _swa_block(q, k0, kp, kc, v0, vp, vc, sinkrow, head, n):
    r, c = _iota2((BLK, BLK), 0), _iota2((BLK, BLK), 1)
    m0 = (c >= PAD) & (c <= n * BLK + r)
    mp = (n >= 2) & (c > r)
    mc = (n >= 1) & (r >= c)
    qs = q * (64.0 ** -0.5)
    s0 = jnp.where(m0, bdot(qs, k0.T), NEG)
    sp = jnp.where(mp, bdot(qs, kp.T), NEG)
    sc = jnp.where(mc, bdot(qs, kc.T), NEG)
    sink = _lane_pick(sinkrow, head)
    mx = jnp.maximum(jnp.max(jnp.maximum(jnp.maximum(s0, sp), sc), axis=-1, keepdims=True), sink)
    mx = lax.stop_gradient(mx)
    p0, pp, pc = jnp.exp(s0 - mx), jnp.exp(sp - mx), jnp.exp(sc - mx)
    den = (jnp.sum(p0, axis=-1, keepdims=True) + jnp.sum(pp, axis=-1, keepdims=True)
           + jnp.sum(pc, axis=-1, keepdims=True) + jnp.exp(sink - mx))
    return (bdot(p0, v0) + bdot(pp, vp) + bdot(pc, vc)) / den


def swa_fwd(p2, sinkrow, *, name):
    T = p2.shape[0]
    N = T // BLK

    def body(q, k0, kp, kc, v0, vp, vc, sink_ref, o_ref):
        h, n = pl.program_id(0), pl.program_id(1)
        o_ref[...] = _swa_block(q[...], k0[...], kp[...], kc[...], v0[...], vp[...], vc[...], sink_ref[...], h,
                                n).astype(BF16)

    def kv(off, which):
        row = {"meta": lambda n: 0, "prev": lambda n: jnp.maximum(n - 1, 0), "cur": lambda n: n}[which]
        return pl.BlockSpec((BLK, LANES), lambda h, n: (row(n), off + h // 4))

    return pl.pallas_call(
        body, name=name, grid=(8, N),
        in_specs=[_blk(0), kv(8, "meta"), kv(8, "prev"), kv(8, "cur"), kv(10, "meta"), kv(10, "prev"), kv(10, "cur"),
                  _const_spec((1, LANES))],
        out_specs=_blk(0), out_shape=jax.ShapeDtypeStruct((T, 1024), BF16),
        compiler_params=_params(("arbitrary", "arbitrary")),
    )(p2, p2, p2, p2, p2, p2, p2, sinkrow)


def swa_bwd(p2, sinkrow, dmixed, *, name):
    T = p2.shape[0]
    N = T // BLK

    def body(q, k0, kp, kc, v0, vp, vc, sink_ref, do_ref, dq_ref, dk_ref, dv_ref, dsink_ref):
        kvh, g, n = pl.program_id(0), pl.program_id(1), pl.program_id(2)
        h = 4 * kvh + g

        @pl.when((g == 0) & (n == 0))
        def _():
            dk_ref[...] = jnp.zeros_like(dk_ref)
            dv_ref[...] = jnp.zeros_like(dv_ref)

        @pl.when((kvh == 0) & (g == 0) & (n == 0))
        def _():
            dsink_ref[...] = jnp.zeros_like(dsink_ref)

        f = lambda *a: _swa_block(*a, h, n)
        _, vjp = jax.vjp(f, q[...], k0[...], kp[...], kc[...], v0[...], vp[...], vc[...], sink_ref[...])
        dq, dk0, dkp, dkc, dv0, dvp, dvc, dsink = vjp(do_ref[...])
        dq_ref[...] = dq
        prev = pl.ds(pl.multiple_of(jnp.maximum(n - 1, 0) * BLK, BLK), BLK)
        cur = pl.ds(pl.multiple_of(n * BLK, BLK), BLK)
        for ref, d0, dp, dc in ((dk_ref, dk0, dkp, dkc), (dv_ref, dv0, dvp, dvc)):
            ref[0:BLK, :] += d0
            ref[prev, :] += dp
            ref[cur, :] += dc
        dsink_ref[...] += dsink

    def kv(off, which):
        row = {"meta": lambda n: 0, "prev": lambda n: jnp.maximum(n - 1, 0), "cur": lambda n: n}[which]
        return pl.BlockSpec((BLK, LANES), lambda kvh, g, n: (row(n), off + kvh))

    qspec = pl.BlockSpec((BLK, LANES), lambda kvh, g, n: (n, 4 * kvh + g))
    slab = pl.BlockSpec((T, LANES), lambda kvh, g, n: (0, kvh))
    return pl.pallas_call(
        body, name=name, grid=(2, 4, N),
        in_specs=[qspec, kv(8, "meta"), kv(8, "prev"), kv(8, "cur"), kv(10, "meta"), kv(10, "prev"), kv(10, "cur"),
                  _const_spec((1, LANES)), qspec],
        out_specs=[qspec, slab, slab, _const_spec((1, LANES))],
        out_shape=[jax.ShapeDtypeStruct((T, 1024), F32), jax.ShapeDtypeStruct((T, 256), F32),
                   jax.ShapeDtypeStruct((T, 256), F32), jax.ShapeDtypeStruct((1, LANES), F32)],
        compiler_params=_params(("arbitrary", "arbitrary", "arbitrary")),
    )(p2, p2, p2, p2, p2, p2, p2, sinkrow, dmixed)


def _split_dot(x, m):
    hi = x.astype(BF16)
    lo = (x - hi.astype(F32)).astype(BF16)
    return _nn(hi, m) + _nn(lo, m)


def _tri(strict):
    i = np.arange(BLK)
    m = (i[:, None] > i[None, :]) if strict else (i[:, None] >= i[None, :])
    return jnp.asarray(m.astype(np.float32), dtype=BF16)


def _sb_weights(qb, kj, j, n, carry, after):
    r, c = _iota2((BLK, BLK), 0), _iota2((BLK, BLK), 1)
    z = _nt(qb, kj)
    kpos = j * BLK + c
    valid = (kpos < n * BLK + r) & (kpos >= PAD)
    lb = jnp.minimum(z, 0.0) - jnp.log1p(jnp.exp(-jnp.abs(z)))
    lm = jnp.where(valid, lb - z, 0.0)
    a = jnp.where(valid, jnp.exp(lb + carry + _split_dot(lm, after)), 0.0)
    return valid, lb, lm, a


def _key_blocks(n_blocks):
    return next(k for k in (5, 3, 1) if n_blocks % k == 0)


def sb_fwd(p2, *, name):
    T = p2.shape[0]
    N = T // BLK
    kb = _key_blocks(N)

    def body(q_ref, k_ref, v_ref, after_ref, o_ref, of_ref):
        n = pl.program_id(1)
        qb = (q_ref[...] * (64.0 ** -0.5)).astype(BF16)
        after = after_ref[...]

        nsup = n // kb + 1

        def step(t, c):
            acc, carry = c
            base = (nsup - 1 - t) * kb
            for sub in reversed(range(kb)):
                rows = pl.ds(pl.multiple_of((base + sub) * BLK, BLK), BLK)
                _, _, lm, a = _sb_weights(qb, k_ref[rows, :].astype(BF16), base + sub, n, carry, after)
                acc = acc + _split_dot(a, v_ref[rows, :].astype(BF16))
                carry = carry + jnp.sum(lm, axis=1, keepdims=True)
            return acc, carry

        acc, _ = lax.fori_loop(0, nsup, step, (jnp.zeros((BLK, LANES), F32), jnp.zeros((BLK, 1), F32)))
        o_ref[...] = acc.astype(BF16)
        of_ref[...] = acc

    def slab(off):
        return pl.BlockSpec((T, LANES), lambda h, n: (0, off + h))

    return pl.pallas_call(
        body, name=name, grid=(8, N),
        in_specs=[_blk(12), slab(20), slab(28), _const_spec((BLK, BLK))],
        out_specs=[_blk(0), _blk(0)],
        out_shape=[jax.ShapeDtypeStruct((T, 1024), BF16), jax.ShapeDtypeStruct((T, 1024), F32)],
        compiler_params=_params(("arbitrary", "arbitrary")),
    )(p2, p2, p2, _tri(True))


def sb_bwd(p2, o, dmixed, *, name):
    T = p2.shape[0]
    N = T // BLK
    kb = _key_blocks(N)

    def body(q_ref, k_ref, v_ref, after_ref, from_ref, o_ref, do_ref, dq_ref, dk_ref, dv_ref):
        n = pl.program_id(1)

        @pl.when(n == 0)
        def _():
            dk_ref[...] = jnp.zeros_like(dk_ref)
            dv_ref[...] = jnp.zeros_like(dv_ref)

        q = q_ref[...]
        qb = (q * (64.0 ** -0.5)).astype(BF16)
        qraw = q.astype(BF16)
        do = do_ref[...]
        dob = do.astype(BF16)
        total = jnp.sum(dob.astype(F32) * o_ref[...], axis=1, keepdims=True)
        after, frm = after_ref[...], from_ref[...]

        nsup = n // kb + 1

        def step(t, c):
            dq, carry, gcarry = c
            base = (nsup - 1 - t) * kb
            for sub in reversed(range(kb)):
                rows = pl.ds(pl.multiple_of((base + sub) * BLK, BLK), BLK)
                kj = k_ref[rows, :].astype(BF16)
                valid, lb, lm, a = _sb_weights(qb, kj, base + sub, n, carry, after)
                g = _nt(dob, v_ref[rows, :].astype(BF16)) * a
                before = total - (gcarry + _split_dot(g, frm))
                beta = jnp.exp(lb)
                dz = (jnp.where(valid, g * (1.0 - beta) - beta * before, 0.0) * (64.0 ** -0.5)).astype(BF16)
                dk_ref[rows, :] += _tn(dz, qraw)
                dv_ref[rows, :] += _tn(a.astype(BF16), dob)
                dq = dq + _nn(dz, kj)
                carry = carry + jnp.sum(lm, axis=1, keepdims=True)
                gcarry = gcarry + jnp.sum(g, axis=1, keepdims=True)
            return dq, carry, gcarry

        zero = jnp.zeros((BLK, 1), F32)
        dq, _, _ = lax.fori_loop(0, nsup, step, (jnp.zeros((BLK, LANES), F32), zero, zero))
        dq_ref[...] = dq

    def slab(off):
        return pl.BlockSpec((T, LANES), lambda h, n: (0, off + h))

    return pl.pallas_call(
        body, name=name, grid=(8, N),
        in_specs=[_blk(12), slab(20), slab(28), _const_spec((BLK, BLK)), _const_spec((BLK, BLK)), _blk(0), _blk(8)],
        out_specs=[_blk(0), slab(0), slab(0)],
        out_shape=[jax.ShapeDtypeStruct((T, 1024), F32)] * 3,
        compiler_params=_params(("arbitrary", "arbitrary")),
    )(p2, p2, p2, _tri(True), _tri(False), o, dmixed)


def ffn_fwd(h, g_pre, g_post, wg, wu, wd, tag):
    u, gate, up, act = norm_mm(h, g_pre, (wg, wu), swiglu=True, wt=True, name=f"ffn_up_{tag}")
    y, h_new = mm_norm_res([act], [wd], h, g_post, 0.5, name=f"ffn_down_{tag}")
    return h_new, (h, u, gate, up, y)


def ffn_bwd(saved, dh, g_pre, g_post, wg, wu, wd, tag):
    h, u, gate, up, y = saved
    dy, dg_post, dgate, dup, act = normbwd_mm_nt(dh, y, g_post, wd, 0.5, (gate, up), name=f"ffn_bwd_down_{tag}")
    dwd = mm_tn(act, dy, name=f"ffn_dwd_{tag}")
    dwg = mm_tn(dgate, u, name=f"ffn_dwg_{tag}")
    dwu = mm_tn(dup, u, name=f"ffn_dwu_{tag}")
    dh_in, dg_pre = mm_nt_normbwd([dgate, dup], [wg, wu], h, g_pre, dh, wt=True, name=f"ffn_bwd_up_{tag}")
    return dh_in, (dg_pre, dg_post), (dwg, dwu, dwd)


def _lane_row(v):
    v = v.reshape(1, -1)
    return jnp.pad(v, ((0, 0), (0, LANES - v.shape[1])))


AB_WIDTHS = (512,) * 8 + (LANES,)


def mixer_ab_fwd(h, g_pre, g_post, w_in, conv_w, a_log, dt_bias, out_norm, w_out, tables):
    u, p = norm_mm(h, g_pre, (w_in,), swiglu=False, name="ab_in")
    ret, sall_r = retention_fwd(p, tables, name="retention_fwd")
    act = conv_silu_fwd(p, conv_w, name="conv_fwd")
    gdn, sall_g = gdn_fwd(act, p, _lane_row(a_log), _lane_row(dt_bias), out_norm.reshape(1, LANES), name="gdn_fwd")
    y, h_new = mm_norm_res([ret, gdn], [w_out[:512], w_out[512:]], h, g_post, 1.0, name="ab_out")
    return h_new, (h, u, p, ret, sall_r, act, gdn, sall_g, y)


def mixer_ab_bwd(saved, dh, g_pre, g_post, w_in, conv_w, a_log, dt_bias, out_norm, w_out, tables):
    h, u, p, ret, sall_r, act, gdn, sall_g, y = saved
    dy, dg_post, dmixed = normbwd_mm_nt(dh, y, g_post, w_out, 1.0, name="ab_bwd_out")
    dw_out = jnp.concatenate([mm_tn(ret, dy, name="ab_dwout_ret"), mm_tn(gdn, dy, name="ab_dwout_gdn")], axis=0)
    pieces = list(retention_bwd(p, sall_r, dmixed, tables, name="retention_bwd"))
    dqa, dka, dva, dz, dba, dalog, ddtb, donorm = gdn_bwd(
        act, p, _lane_row(a_log), _lane_row(dt_bias), out_norm.reshape(1, LANES), sall_g, dmixed, name="gdn_bwd")
    dconv = []
    for part, dact in enumerate((dqa, dka, dva)):
        dx, dw = conv_silu_bwd(p, conv_w, dact, part, name=f"conv_bwd_{part}")
        pieces.append(dx)
        dconv.append(dw)
    pieces += [dz, dba]
    offs = np.cumsum((0,) + AB_WIDTHS)
    w_parts = [w_in[:, a:b] for a, b in zip(offs[:-1], offs[1:])]
    dh_in, dg_pre = mm_nt_normbwd(pieces, w_parts, h, g_pre, dh, name="ab_bwd_in")
    dw_in = jnp.concatenate([mm_tn(u, pc, name=f"ab_dwin_{i}") for i, pc in enumerate(pieces)], axis=1)
    small = (jnp.concatenate(dconv, axis=1), dalog[:, :4], ddtb[:, :4], donorm)
    return dh_in, (dg_pre, dg_post), (dw_in, dw_out), small


CD_WIDTHS = (1024, 256, 256, 1024, 1024, 1024)


def mixer_cd_fwd(h, g_pre, g_post, w_in, sinks, w_out):
    u, p2 = norm_mm(h, g_pre, (w_in,), swiglu=False, name="cd_in")
    swa = swa_fwd(p2, _lane_row(sinks), name="swa_fwd")
    sb, sb_f32 = sb_fwd(p2, name="sb_fwd")
    y, h_new = mm_norm_res([swa, sb], [w_out[:1024], w_out[1024:]], h, g_post, 1.0, name="cd_out")
    return h_new, (h, u, p2, swa, sb, sb_f32, y)


def mixer_cd_bwd(saved, dh, g_pre, g_post, w_in, sinks, w_out):
    h, u, p2, swa, sb, sb_f32, y = saved
    dy, dg_post, dmixed = normbwd_mm_nt(dh, y, g_post, w_out, 1.0, name="cd_bwd_out")
    dw_out = jnp.concatenate([mm_tn(swa, dy, name="cd_dwout_swa"), mm_tn(sb, dy, name="cd_dwout_sb")], axis=0)
    dq_c, dk_c, dv_c, dsink = swa_bwd(p2, _lane_row(sinks), dmixed, name="swa_bwd")
    pieces = [dq_c, dk_c, dv_c] + list(sb_bwd(p2, sb_f32, dmixed, name="sb_bwd"))
    offs = np.cumsum((0,) + CD_WIDTHS)
    w_parts = [w_in[:, a:b] for a, b in zip(offs[:-1], offs[1:])]
    dh_in, dg_pre = mm_nt_normbwd(pieces, w_parts, h, g_pre, dh, name="cd_bwd_in")
    dw_in = jnp.concatenate([mm_tn(u, pc, name=f"cd_dwin_{i}") for i, pc in enumerate(pieces)], axis=1)
    return dh_in, (dg_pre, dg_post), (dw_in, dw_out), dsink[:, :8]


def _pad_heads(w, axis):
    shape = w.shape
    w = w.reshape(shape[:axis] + (shape[axis] // 64, 64) + shape[axis + 1:])
    pad = [(0, 0)] * w.ndim
    pad[axis + 1] = (0, 64)
    return jnp.pad(w, pad).reshape(shape[:axis] + (2 * shape[axis],) + shape[axis + 1:])


def _unpad_heads(w, axis):
    shape = w.shape
    w = w.reshape(shape[:axis] + (shape[axis] // 128, 128) + shape[axis + 1:])
    w = lax.slice_in_dim(w, 0, 64, axis=axis + 1)
    return w.reshape(shape[:axis] + (shape[axis] // 2,) + shape[axis + 1:])


SMALL_SHARDED = (("meta_tokens", (NMETA, LANES), 1), ("norm_gains", (2, 6, LANES), 2), ("ab_conv_w", (1, 4, 192), 2))
SMALL_REPL = (("ab_a_log", (1, 4)), ("ab_dt_bias", (1, 4)), ("ab_out_norm", (1, LANES)), ("cd_sinks", (1, 8)))


def _stack_shards(g, axis):
    full = jnp.moveaxis(g, 0, axis)
    shape = full.shape
    return full.reshape(shape[:axis] + (shape[axis] * shape[axis + 1],) + shape[axis + 2:])


def _split_shards(full, axis):
    shape = full.shape
    g = full.reshape(shape[:axis] + (NDEV, shape[axis] // NDEV) + shape[axis + 1:])
    return jnp.moveaxis(g, axis, 0)


def _pad_rows8(a):
    rows = []
    for x in a:
        flat = x.reshape(x.shape[0], -1)
        n = -(-flat.shape[1] // LANES) * LANES
        rows.append(jnp.pad(flat, ((0, 0), (0, n - flat.shape[1]))).reshape(x.shape[0], n // LANES, LANES))
    cat = jnp.concatenate(rows, axis=1)
    return jnp.pad(cat, ((0, 0), (0, -cat.shape[1] % 8), (0, 0)))


def _unpad_rows8(packed, shapes):
    out, at = [], 0
    for shape in shapes:
        size = int(np.prod(shape))
        nrow = -(-size // LANES)
        blk = packed[:, at:at + nrow].reshape(packed.shape[0], -1)[:, :size]
        out.append(blk.reshape((packed.shape[0],) + tuple(shape)))
        at += nrow
    return out


def kernel(x, meta_tokens, norm_gains, ffn_w_gate, ffn_w_up, ffn_w_down, ab_w_in, ab_conv_w, ab_a_log, ab_dt_bias, ab_out_norm, ab_w_out, cd_w_in, cd_sinks, cd_w_out, loss_target, m_meta_tokens, m_norm_gains, m_ffn_w_gate, m_ffn_w_up, m_ffn_w_down, m_ab_w_in, m_ab_conv_w, m_ab_a_log, m_ab_dt_bias, m_ab_out_norm, m_ab_w_out, m_cd_w_in, m_cd_sinks, m_cd_w_out, v_meta_tokens, v_norm_gains, v_ffn_w_gate, v_ffn_w_up, v_ffn_w_down, v_ab_w_in, v_ab_conv_w, v_ab_a_log, v_ab_dt_bias, v_ab_out_norm, v_ab_w_out, v_cd_w_in, v_cd_sinks, v_cd_w_out):
    w = dict(meta_tokens=meta_tokens, norm_gains=norm_gains, ffn_w_gate=ffn_w_gate, ffn_w_up=ffn_w_up,
             ffn_w_down=ffn_w_down, ab_w_in=ab_w_in, ab_conv_w=ab_conv_w, ab_a_log=ab_a_log, ab_dt_bias=ab_dt_bias,
             ab_out_norm=ab_out_norm, ab_w_out=ab_w_out, cd_w_in=cd_w_in, cd_sinks=cd_sinks, cd_w_out=cd_w_out)
    m = dict(meta_tokens=m_meta_tokens, norm_gains=m_norm_gains, ffn_w_gate=m_ffn_w_gate, ffn_w_up=m_ffn_w_up,
             ffn_w_down=m_ffn_w_down, ab_w_in=m_ab_w_in, ab_conv_w=m_ab_conv_w, ab_a_log=m_ab_a_log,
             ab_dt_bias=m_ab_dt_bias, ab_out_norm=m_ab_out_norm, ab_w_out=m_ab_w_out, cd_w_in=m_cd_w_in,
             cd_sinks=m_cd_sinks, cd_w_out=m_cd_w_out)
    v = dict(meta_tokens=v_meta_tokens, norm_gains=v_norm_gains, ffn_w_gate=v_ffn_w_gate, ffn_w_up=v_ffn_w_up,
             ffn_w_down=v_ffn_w_down, ab_w_in=v_ab_w_in, ab_conv_w=v_ab_conv_w, ab_a_log=v_ab_a_log,
             ab_dt_bias=v_ab_dt_bias, ab_out_norm=v_ab_out_norm, ab_w_out=v_ab_w_out, cd_w_in=v_cd_w_in,
             cd_sinks=v_cd_sinks, cd_w_out=v_cd_w_out)
    order = list(w)
    S = x.shape[1]
    T = S + BLK

    fs = DFF // NDEV
    ffn_local = jnp.concatenate([jnp.swapaxes(ffn_w_gate, 2, 3).reshape(4 * fs, D),
                                 jnp.swapaxes(ffn_w_up, 2, 3).reshape(4 * fs, D), ffn_w_down.reshape(4 * fs, D)],
                                axis=0).astype(BF16)
    outs_local = jnp.concatenate([ab_w_out[0], cd_w_out[0]], axis=0).astype(BF16)
    ffn_all, abin_all, outs_all, cdin_all = all_gather_big(
        [ffn_local, ab_w_in[0].astype(BF16), outs_local, cd_w_in[0].astype(BF16)], name="gather_weights")
    ffn_mat = lambda k: ffn_all[:, k * fs:(k + 1) * fs].reshape(DFF, D)
    layers = [(i, j) for i in range(2) for j in range(2)]
    wg = {ij: ffn_mat(k) for k, ij in enumerate(layers)}
    wu = {ij: ffn_mat(4 + k) for k, ij in enumerate(layers)}
    wd = {ij: ffn_mat(8 + k) for k, ij in enumerate(layers)}
    ab_in = jnp.pad(_stack_shards(abin_all, 1), ((0, 0), (0, AB_INP - AB_IN)))
    ab_out = outs_all[:, :D // NDEV].reshape(D, D)
    cd_in = _pad_heads(_stack_shards(cdin_all, 1), 1)
    cd_out = _pad_heads(outs_all[:, D // NDEV:].reshape(D, D), 0)
    small_src = jnp.broadcast_to(_pad_rows8([w[n][None] for n, _, _ in SMALL_SHARDED]), (NDEV, 40, LANES))
    small_all = _unpad_rows8(all_to_all_small(small_src, name="gather_small"), [s for _, s, _ in SMALL_SHARDED])
    full = {n: _stack_shards(g, ax) for (n, _, ax), g in zip(SMALL_SHARDED, small_all)}
    conv_w = full["ab_conv_w"][0]
    gains = full["norm_gains"].reshape(2, 6, 1, D)
    tables = retention_tables(T)

    h = jnp.concatenate([jnp.zeros((PAD, D), F32), full["meta_tokens"], x[0]], axis=0)
    h, s00 = ffn_fwd(h, gains[0, 0], gains[0, 1], wg[0, 0], wu[0, 0], wd[0, 0], "00")
    h, sab = mixer_ab_fwd(h, gains[0, 2], gains[0, 3], ab_in, conv_w, ab_a_log, ab_dt_bias, ab_out_norm, ab_out, tables)
    h, s01 = ffn_fwd(h, gains[0, 4], gains[0, 5], wg[0, 1], wu[0, 1], wd[0, 1], "01")
    h, s10 = ffn_fwd(h, gains[1, 0], gains[1, 1], wg[1, 0], wu[1, 0], wd[1, 0], "10")
    h, scd = mixer_cd_fwd(h, gains[1, 2], gains[1, 3], cd_in, cd_sinks, cd_out)
    h, s11 = ffn_fwd(h, gains[1, 4], gains[1, 5], wg[1, 1], wu[1, 1], wd[1, 1], "11")
    loss_tile, dh = loss_and_grad(h, loss_target[0], name="loss")
    loss = lax.psum(loss_tile[0, 0], ("x", "y", "c"))

    dgain = [[None] * 6, [None] * 6]
    dffn = {}
    dh, (dgain[1][4], dgain[1][5]), dffn[1, 1] = ffn_bwd(s11, dh, gains[1, 4], gains[1, 5], wg[1, 1], wu[1, 1], wd[1, 1], "11")
    dh, (dgain[1][2], dgain[1][3]), (dcd_in, dcd_out), dsinks = mixer_cd_bwd(scd, dh, gains[1, 2], gains[1, 3], cd_in, cd_sinks, cd_out)
    dh, (dgain[1][0], dgain[1][1]), dffn[1, 0] = ffn_bwd(s10, dh, gains[1, 0], gains[1, 1], wg[1, 0], wu[1, 0], wd[1, 0], "10")
    dh, (dgain[0][4], dgain[0][5]), dffn[0, 1] = ffn_bwd(s01, dh, gains[0, 4], gains[0, 5], wg[0, 1], wu[0, 1], wd[0, 1], "01")
    dh, (dgain[0][2], dgain[0][3]), (dab_in, dab_out), (dconv, dalog, ddtb, donorm) = mixer_ab_bwd(
        sab, dh, gains[0, 2], gains[0, 3], ab_in, conv_w, ab_a_log, ab_dt_bias, ab_out_norm, ab_out, tables)
    dh, (dgain[0][0], dgain[0][1]), dffn[0, 0] = ffn_bwd(s00, dh, gains[0, 0], gains[0, 1], wg[0, 0], wu[0, 0], wd[0, 0], "00")
    grad_x = dh[BLK:][None]

    gfull = dict(meta_tokens=dh[PAD:BLK], norm_gains=jnp.stack([jnp.concatenate(r, axis=0) for r in dgain]),
                 ab_conv_w=dconv[None])
    ffn_send = jnp.concatenate([dffn[ij][k].astype(BF16).reshape(NDEV, fs, D) for k in range(3) for ij in layers], axis=1)
    outs_send = jnp.concatenate([dab_out.astype(BF16).reshape(NDEV, D // NDEV, D),
                                 _unpad_heads(dcd_out, 0).astype(BF16).reshape(NDEV, D // NDEV, D)], axis=1)
    abin_send = _split_shards(dab_in[:, :AB_IN].astype(BF16), 1)
    cdin_send = _split_shards(_unpad_heads(dcd_in, 1).astype(BF16), 1)
    ffn_g, abin_g, outs_g, cdin_g = reduce_scatter_big([ffn_send, abin_send, outs_send, cdin_send])
    ffn_g = ffn_g.reshape(3, 2, 2, fs, D)
    grads = dict(ffn_w_gate=jnp.swapaxes(ffn_g[0], 2, 3), ffn_w_up=jnp.swapaxes(ffn_g[1], 2, 3), ffn_w_down=ffn_g[2],
                 ab_w_in=abin_g[None], ab_w_out=outs_g[None, :D // NDEV], cd_w_in=cdin_g[None],
                 cd_w_out=outs_g[None, D // NDEV:])
    repl = [jnp.broadcast_to(t[None], (NDEV,) + t.shape) for t in (dalog, ddtb, donorm, dsinks)]
    ssend = _pad_rows8([_split_shards(gfull[n], ax) for n, _, ax in SMALL_SHARDED] + repl)
    ssum = sum_slots(all_to_all_small(ssend, name="exchange_small_grads"), name="sum_small_grads")[None]
    small = _unpad_rows8(ssum, [s for _, s, _ in SMALL_SHARDED] + [s for _, s in SMALL_REPL])
    grads.update({n: g[0] for n, g in zip([n for n, _, _ in SMALL_SHARDED] + [n for n, _ in SMALL_REPL], small)})

    delta, new_m, new_v = {}, {}, {}
    for n in order:
        shape = w[n].shape
        view = (-1, shape[-1])
        d_, m_, v_ = adamw(w[n].reshape(view), grads[n].reshape(view), m[n].reshape(view), v[n].reshape(view),
                           name=f"adamw_{n}")
        delta[n], new_m[n], new_v[n] = d_.reshape(shape), m_.reshape(shape), v_.reshape(shape)
    return (loss, grad_x, *[grads[n] for n in order], *[delta[n] for n in order], *[new_m[n] for n in order],
            *[new_v[n] for n in order])
```

```python
import functools
import math

import numpy as np
import jax
import jax.numpy as jnp
from jax import lax
from jax.experimental import pallas as pl
from jax.experimental.pallas import tpu as pltpu

F32, BF16 = jnp.float32, jnp.bfloat16
EPS = 1e-6
D = 1024
NMETA = 16
BLK = 128
PAD = BLK - NMETA
DFF = 2816
LANES = 128
NDEV = 8
AB_IN, AB_INP = 4104, 4224
ADAM_LR, ADAM_B1, ADAM_B2, ADAM_EPS, ADAM_WD, ADAM_STEP = 0.001, 0.9, 0.999, 1e-08, 0.01, 10
VMEM_LIMIT = 56 * 1024 * 1024
MESH = pl.DeviceIdType.MESH
HIGHEST = lax.Precision.HIGHEST


def _params(sem):
    return pltpu.CompilerParams(dimension_semantics=sem, vmem_limit_bytes=VMEM_LIMIT)


def _row_tile(T, streamed, resident):
    for tm in (640, 320, 128):
        if T % tm == 0 and 2 * (tm * streamed + resident) <= VMEM_LIMIT - 14 * 1024 * 1024:
            return tm
    return _tile(T, 128)


def _tile(n, cap):
    if n <= cap:
        return n
    best = None
    for t in range(LANES, cap + 1, LANES):
        if n % t == 0:
            best = t
    assert best is not None, (n, cap)
    return best


def _rms_fwd(x, g):
    return x * lax.rsqrt(jnp.mean(x * x, axis=-1, keepdims=True) + EPS) * g


def _rms_bwd(x, g, dz):
    r = lax.rsqrt(jnp.mean(x * x, axis=-1, keepdims=True) + EPS)
    xh = x * r
    dg = jnp.sum(dz * xh, axis=0, keepdims=True)
    t = dz * g
    return r * (t - xh * jnp.mean(t * xh, axis=-1, keepdims=True)), dg


def _sigmoid(x):
    return 1.0 / (1.0 + jnp.exp(-x))


def _silu(x):
    return x * _sigmoid(x)


def _nn(a, b, precision=None):
    return lax.dot_general(a, b, (((1,), (0,)), ((), ())), preferred_element_type=F32, precision=precision)


def _nt(a, b):
    return lax.dot_general(a, b, (((1,), (1,)), ((), ())), preferred_element_type=F32)


def _tn(a, b):
    return lax.dot_general(a, b, (((0,), (0,)), ((), ())), preferred_element_type=F32)


@jax.custom_vjp
def bdot(a, b):
    return _nn(a.astype(BF16), b.astype(BF16))


def _bdot_fwd(a, b):
    return bdot(a, b), (a, b)


def _bdot_bwd(res, g):
    a, b = res
    return bdot(g, b.T), bdot(a.T, g)


bdot.defvjp(_bdot_fwd, _bdot_bwd)


@jax.custom_vjp
def hdot(a, b):
    return _nn(a, b, HIGHEST)


def _hdot_fwd(a, b):
    return hdot(a, b), (a, b)


def _hdot_bwd(res, g):
    a, b = res
    return hdot(g, b.T), hdot(a.T, g)


hdot.defvjp(_hdot_fwd, _hdot_bwd)


def _iota2(shape, axis):
    return lax.broadcasted_iota(jnp.int32, shape, axis)


def _lane_pick(row, lane):
    return jnp.sum(jnp.where(_iota2(row.shape, 1) == lane, row, 0.0), axis=1, keepdims=True)


def norm_mm(h, gain, ws, *, swiglu, name, wt=False):
    T, Dm = h.shape
    N = ws[0].shape[0 if wt else 1]
    tm, tn = _tile(T, 640), _tile(N, 1408)
    nw = len(ws)
    mm = _nt if wt else _nn

    def body(h_ref, g_ref, *refs):
        w_refs, u_ref, o_refs = refs[:nw], refs[nw], refs[nw + 1:]

        @pl.when(pl.program_id(1) == 0)
        def _():
            u_ref[...] = _rms_fwd(h_ref[...], g_ref[...]).astype(BF16)

        u = u_ref[...]
        acc = [mm(u, w[...]) for w in w_refs]
        if swiglu:
            o_refs[0][...] = acc[0].astype(BF16)
            o_refs[1][...] = acc[1].astype(BF16)
            o_refs[2][...] = (_silu(acc[0]) * acc[1]).astype(BF16)
        else:
            o_refs[0][...] = acc[0]

    row = pl.BlockSpec((tm, Dm), lambda i, j: (i, 0))
    tile = pl.BlockSpec((tm, tn), lambda i, j: (i, j))
    if swiglu:
        out_shape = [jax.ShapeDtypeStruct((T, Dm), BF16)] + [jax.ShapeDtypeStruct((T, N), BF16)] * 3
        out_specs = [row, tile, tile, tile]
    else:
        out_shape = [jax.ShapeDtypeStruct((T, Dm), BF16), jax.ShapeDtypeStruct((T, N), F32)]
        out_specs = [row, tile]
    return pl.pallas_call(
        body, name=name, grid=(T // tm, N // tn),
        in_specs=[row, pl.BlockSpec((1, Dm), lambda i, j: (0, 0))]
        + [pl.BlockSpec((tn, Dm), lambda i, j: (j, 0)) if wt else pl.BlockSpec((Dm, tn), lambda i, j: (0, j))] * nw,
        out_specs=out_specs, out_shape=out_shape,
        compiler_params=_params(("arbitrary", "arbitrary")),
    )(h, gain, *ws)


def mm_norm_res(As, Ws, h, gain, scale, *, name):
    T, Dm = h.shape
    n = len(As)
    tm = _row_tile(T, sum(a.shape[1] * a.dtype.itemsize for a in As) + 3 * Dm * 4,
                   sum(w.size * w.dtype.itemsize for w in Ws))

    def body(*refs):
        a_refs, w_refs = refs[:n], refs[n:2 * n]
        h_ref, g_ref, y_ref, hn_ref = refs[2 * n:]
        y = _nn(a_refs[0][...].astype(BF16), w_refs[0][...])
        for a, w in zip(a_refs[1:], w_refs[1:]):
            y = y + _nn(a[...].astype(BF16), w[...])
        y_ref[...] = y
        hn_ref[...] = h_ref[...] + scale * _rms_fwd(y, g_ref[...])

    row = pl.BlockSpec((tm, Dm), lambda i: (i, 0))
    return pl.pallas_call(
        body, name=name, grid=(T // tm,),
        in_specs=[pl.BlockSpec((tm, a.shape[1]), lambda i: (i, 0)) for a in As]
        + [pl.BlockSpec(w.shape, lambda i: (0, 0)) for w in Ws]
        + [row, pl.BlockSpec((1, Dm), lambda i: (0, 0))],
        out_specs=[row, row], out_shape=[jax.ShapeDtypeStruct((T, Dm), F32)] * 2,
        compiler_params=_params(("arbitrary",)),
    )(*As, *Ws, h, gain)


def normbwd_mm_nt(dh, y, gain, w, scale, gu=None, *, name):
    T, Dm = dh.shape
    N = w.shape[0]
    tm, tn = _tile(T, 640), _tile(N, 1408)
    swiglu = gu is not None

    def body(dh_ref, y_ref, g_ref, w_ref, *refs):
        if swiglu:
            gate_ref, up_ref, dy_ref, dg_ref, dgate_ref, dup_ref, a_ref = refs
        else:
            dy_ref, dg_ref, da_ref = refs
        i, j = pl.program_id(0), pl.program_id(1)

        @pl.when(j == 0)
        def _():
            dy, dg = _rms_bwd(y_ref[...], g_ref[...], scale * dh_ref[...])
            dy_ref[...] = dy.astype(BF16)

            @pl.when(i == 0)
            def _():
                dg_ref[...] = jnp.zeros_like(dg_ref)

            dg_ref[...] += dg

        da = _nt(dy_ref[...], w_ref[...])
        if swiglu:
            gate, up = gate_ref[...].astype(F32), up_ref[...].astype(F32)
            s = _sigmoid(gate)
            dgate_ref[...] = (da * up * s * (1.0 + gate * (1.0 - s))).astype(BF16)
            dup_ref[...] = (da * gate * s).astype(BF16)
            a_ref[...] = (gate * s * up).astype(BF16)
        else:
            da_ref[...] = da

    row = pl.BlockSpec((tm, Dm), lambda i, j: (i, 0))
    vec = pl.BlockSpec((1, Dm), lambda i, j: (0, 0))
    tile = pl.BlockSpec((tm, tn), lambda i, j: (i, j))
    in_specs = [row, row, vec, pl.BlockSpec((tn, Dm), lambda i, j: (j, 0))]
    out_shape = [jax.ShapeDtypeStruct((T, Dm), BF16), jax.ShapeDtypeStruct((1, Dm), F32)]
    if swiglu:
        in_specs += [tile, tile]
        out_shape += [jax.ShapeDtypeStruct((T, N), BF16)] * 3
        out_specs = [row, vec, tile, tile, tile]
        args = (dh, y, gain, w, *gu)
    else:
        out_shape += [jax.ShapeDtypeStruct((T, N), F32)]
        out_specs = [row, vec, tile]
        args = (dh, y, gain, w)
    return pl.pallas_call(
        body, name=name, grid=(T // tm, N // tn), in_specs=in_specs, out_specs=out_specs,
        out_shape=out_shape, compiler_params=_params(("arbitrary", "arbitrary")),
    )(*args)


def mm_nt_normbwd(dPs, Ws, h, gain, dh_in, *, name, wt=False):
    T, Dm = h.shape
    n = len(dPs)
    tm = _row_tile(T, sum(p.shape[1] * p.dtype.itemsize for p in dPs) + 3 * Dm * 4,
                   sum(w.size * w.dtype.itemsize for w in Ws))
    mm = _nn if wt else _nt

    def body(*refs):
        p_refs, w_refs = refs[:n], refs[n:2 * n]
        h_ref, g_ref, dhin_ref, dh_ref, dg_ref = refs[2 * n:]
        du = mm(p_refs[0][...].astype(BF16), w_refs[0][...])
        for p, w in zip(p_refs[1:], w_refs[1:]):
            du = du + mm(p[...].astype(BF16), w[...])
        dx, dg = _rms_bwd(h_ref[...], g_ref[...], du)
        dh_ref[...] = dhin_ref[...] + dx

        @pl.when(pl.program_id(0) == 0)
        def _():
            dg_ref[...] = jnp.zeros_like(dg_ref)

        dg_ref[...] += dg

    row = pl.BlockSpec((tm, Dm), lambda i: (i, 0))
    vec = pl.BlockSpec((1, Dm), lambda i: (0, 0))
    return pl.pallas_call(
        body, name=name, grid=(T // tm,),
        in_specs=[pl.BlockSpec((tm, p.shape[1]), lambda i: (i, 0)) for p in dPs]
        + [pl.BlockSpec(w.shape, lambda i: (0, 0)) for w in Ws] + [row, vec, row],
        out_specs=[row, vec],
        out_shape=[jax.ShapeDtypeStruct((T, Dm), F32), jax.ShapeDtypeStruct((1, Dm), F32)],
        compiler_params=_params(("arbitrary",)),
    )(*dPs, *Ws, h, gain, dh_in)


def mm_tn(a, b, *, name):
    T, M = a.shape
    N = b.shape[1]
    tm, tn, tk = _tile(M, 1408), _tile(N, 1408), _tile(T, 640)

    def body(a_ref, b_ref, o_ref):
        @pl.when(pl.program_id(2) == 0)
        def _():
            o_ref[...] = jnp.zeros_like(o_ref)

        o_ref[...] += _tn(a_ref[...].astype(BF16), b_ref[...].astype(BF16))

    return pl.pallas_call(
        body, name=name, grid=(M // tm, N // tn, T // tk),
        in_specs=[pl.BlockSpec((tk, tm), lambda i, j, k: (k, i)), pl.BlockSpec((tk, tn), lambda i, j, k: (k, j))],
        out_specs=pl.BlockSpec((tm, tn), lambda i, j, k: (i, j)),
        out_shape=jax.ShapeDtypeStruct((M, N), F32),
        compiler_params=_params(("arbitrary", "arbitrary", "arbitrary")),
    )(a, b)


def loss_and_grad(h, target, *, name):
    T, Dm = h.shape

    def body(h_ref, t_ref, loss_ref, dh_ref):
        b = pl.program_id(0)

        @pl.when(b == 0)
        def _():
            loss_ref[...] = jnp.zeros_like(loss_ref)
            dh_ref[...] = jnp.zeros_like(dh_ref)

        @pl.when(b > 0)
        def _():
            e = h_ref[...] - t_ref[...]
            dh_ref[...] = e * (1.0 / Dm)
            loss_ref[...] += jnp.sum(e * e) * (0.5 / Dm)

    return pl.pallas_call(
        body, name=name, grid=(T // BLK,),
        in_specs=[pl.BlockSpec((BLK, Dm), lambda b: (b, 0)),
                  pl.BlockSpec((BLK, Dm), lambda b: (jnp.maximum(b - 1, 0), 0))],
        out_specs=[pl.BlockSpec((8, LANES), lambda b: (0, 0)), pl.BlockSpec((BLK, Dm), lambda b: (b, 0))],
        out_shape=[jax.ShapeDtypeStruct((8, LANES), F32), jax.ShapeDtypeStruct((T, Dm), F32)],
        compiler_params=_params(("arbitrary",)),
    )(h, target)


def adamw(w, g, m, v, *, name):
    R, C = w.shape
    tr = R
    for t in (512, 352, 256):
        if R > t and R % t == 0:
            tr = t
            break

    def body(w_ref, g_ref, m_ref, v_ref, d_ref, nm_ref, nv_ref):
        g_ = g_ref[...]
        m_ = ADAM_B1 * m_ref[...] + (1.0 - ADAM_B1) * g_
        v_ = ADAM_B2 * v_ref[...] + (1.0 - ADAM_B2) * (g_ * g_)
        m_hat = m_ / (1.0 - ADAM_B1 ** ADAM_STEP)
        v_hat = v_ / (1.0 - ADAM_B2 ** ADAM_STEP)
        d_ref[...] = -ADAM_LR * (m_hat / (jnp.sqrt(v_hat) + ADAM_EPS) + ADAM_WD * w_ref[...])
        nm_ref[...] = m_
        nv_ref[...] = v_

    spec = pl.BlockSpec((tr, C), lambda i: (i, 0))
    return pl.pallas_call(
        body, name=name, grid=(R // tr,), in_specs=[spec] * 4, out_specs=[spec] * 3,
        out_shape=[jax.ShapeDtypeStruct((R, C), F32)] * 3, compiler_params=_params(("arbitrary",)),
    )(w, g, m, v)


def _me():
    return lax.axis_index("x"), lax.axis_index("y"), lax.axis_index("c")


def _flip(pos, rel):
    return tuple(1 - p if r else p for p, r in zip(pos, rel))


def _slot(pos):
    return 4 * pos[0] + 2 * pos[1] + pos[2]


HBM_SPEC = pl.BlockSpec(memory_space=pltpu.HBM)
CHIP_RELS = ((1, 0), (0, 1), (1, 1))


def all_gather_big(xs, *, name):
    n = len(xs)

    def body(*refs):
        x_refs, out_refs = refs[:n], refs[n:2 * n]
        send_sems, recv_sems, local_sems = refs[2 * n:]
        me = _me()
        sibling = _flip(me, (0, 0, 1))
        chips = [_flip(me, rel + (0,)) for rel in CHIP_RELS]

        def copy(i, k, block, to, src=None):
            dst = out_refs[i].at[_slot(block)]
            return pltpu.make_async_remote_copy(
                src_ref=dst if src is None else src, dst_ref=dst, send_sem=send_sems.at[i, k],
                recv_sem=recv_sems.at[i, k], device_id=to, device_id_type=MESH)

        sent, local = [], []
        for i in range(n):
            mine = pltpu.make_async_copy(x_refs[i], out_refs[i].at[_slot(me)], local_sems.at[i])
            mine.start()
            local.append(mine)
            sent += [copy(i, 0, me, sibling, src=x_refs[i])]
            sent += [copy(i, 1 + j, me, chip, src=x_refs[i]) for j, chip in enumerate(chips)]
        for cp in sent:
            cp.start()
        for i in range(n):
            for j, chip in enumerate(chips):
                copy(i, 1 + j, chip, me).wait_recv()
                passed = copy(i, 4 + j, chip, sibling)
                passed.start()
                sent.append(passed)
        for i in range(n):
            copy(i, 0, sibling, me).wait_recv()
            for j, chip in enumerate(chips):
                copy(i, 4 + j, _flip(chip, (0, 0, 1)), me).wait_recv()
        for cp in sent:
            cp.wait_send()
        for mine in local:
            mine.wait()

    return pl.pallas_call(
        body, name=name, in_specs=[HBM_SPEC] * n, out_specs=[HBM_SPEC] * n,
        out_shape=[jax.ShapeDtypeStruct((NDEV,) + x.shape, x.dtype) for x in xs],
        scratch_shapes=[pltpu.SemaphoreType.DMA((n, 7)), pltpu.SemaphoreType.DMA((n, 7)), pltpu.SemaphoreType.DMA((n,))],
    )(*xs)


def all_to_all_small(src, *, name):
    _, r, C = src.shape

    def body(src_ref, out_ref, send_sems, recv_sems):
        me = _me()
        my = _slot(me)
        out_ref[my] = src_ref[my]
        copies = []
        for k in range(1, NDEV):
            peer = _flip(me, ((k >> 2) & 1, (k >> 1) & 1, k & 1))
            cp = pltpu.make_async_remote_copy(
                src_ref=src_ref.at[_slot(peer)], dst_ref=out_ref.at[my], send_sem=send_sems.at[k - 1],
                recv_sem=recv_sems.at[k - 1], device_id=peer, device_id_type=MESH)
            cp.start()
            copies.append((cp, peer))
        for k, (cp, peer) in enumerate(copies):
            pltpu.make_async_remote_copy(
                src_ref=src_ref.at[my], dst_ref=out_ref.at[_slot(peer)], send_sem=send_sems.at[k],
                recv_sem=recv_sems.at[k], device_id=peer, device_id_type=MESH).wait_recv()
        for cp, _ in copies:
            cp.wait_send()

    vm = pl.BlockSpec(memory_space=pltpu.VMEM)
    return pl.pallas_call(
        body, name=name, in_specs=[vm], out_specs=vm, out_shape=jax.ShapeDtypeStruct(src.shape, src.dtype),
        scratch_shapes=[pltpu.SemaphoreType.DMA((7,)), pltpu.SemaphoreType.DMA((7,))],
    )(src)


def sum_slots(a, *, name):
    n, r, C = a.shape

    def body(a_ref, o_ref):
        s = a_ref[0]
        for k in range(1, n):
            s = s + a_ref[k]
        o_ref[...] = s

    vm = pl.BlockSpec(memory_space=pltpu.VMEM)
    return pl.pallas_call(body, name=name, in_specs=[vm], out_specs=vm,
                          out_shape=jax.ShapeDtypeStruct((r, C), F32))(a)


def rs_exchange_sibling(gs, *, name):
    n = len(gs)

    def body(*refs):
        g_refs, out_refs, send_sems, recv_sems = refs[:n], refs[n:2 * n], refs[2 * n], refs[2 * n + 1]
        sibling = _flip(_me(), (0, 0, 1))
        copies = []
        for i in range(n):
            for chip in range(4):
                cp = pltpu.make_async_remote_copy(
                    src_ref=g_refs[i].at[2 * chip + sibling[2]], dst_ref=out_refs[i].at[chip],
                    send_sem=send_sems.at[i, chip], recv_sem=recv_sems.at[i, chip], device_id=sibling,
                    device_id_type=MESH)
                cp.start()
                copies.append(cp)
        for cp in copies:
            cp.wait()

    return pl.pallas_call(
        body, name=name, in_specs=[HBM_SPEC] * n, out_specs=[HBM_SPEC] * n,
        out_shape=[jax.ShapeDtypeStruct((4,) + g.shape[1:], g.dtype) for g in gs],
        scratch_shapes=[pltpu.SemaphoreType.DMA((n, 4)), pltpu.SemaphoreType.DMA((n, 4))],
    )(*gs)


def rs_chip_partials(g, got, *, name):
    _, R, C = g.shape
    tr = _tile(R, 768)

    def body(c_ref, g_ref, got_ref, o_ref):
        o_ref[...] = (g_ref[...].astype(F32) + got_ref[...].astype(F32)).astype(o_ref.dtype)

    c = jnp.reshape(lax.axis_index("c"), (1,)).astype(jnp.int32)
    return pl.pallas_call(
        body, name=name,
        grid_spec=pltpu.PrefetchScalarGridSpec(
            num_scalar_prefetch=1, grid=(4, R // tr),
            in_specs=[pl.BlockSpec((None, tr, C), lambda k, i, c_ref: (2 * k + c_ref[0], i, 0)),
                      pl.BlockSpec((None, tr, C), lambda k, i, c_ref: (k, i, 0))],
            out_specs=pl.BlockSpec((None, tr, C), lambda k, i, c_ref: (k, i, 0))),
        out_shape=jax.ShapeDtypeStruct((4, R, C), g.dtype), compiler_params=_params(("arbitrary", "arbitrary")),
    )(c, g, got)


def rs_exchange_chips(ps, *, name):
    n = len(ps)

    def body(*refs):
        p_refs, out_refs, send_sems, recv_sems = refs[:n], refs[n:2 * n], refs[2 * n], refs[2 * n + 1]
        me = _me()
        copies = []
        for i in range(n):
            for j, rel in enumerate(CHIP_RELS):
                peer = _flip(me, rel + (0,))
                cp = pltpu.make_async_remote_copy(
                    src_ref=p_refs[i].at[2 * peer[0] + peer[1]], dst_ref=out_refs[i].at[j], send_sem=send_sems.at[i, j],
                    recv_sem=recv_sems.at[i, j], device_id=peer, device_id_type=MESH)
                cp.start()
                copies.append(cp)
        for cp in copies:
            cp.wait()

    return pl.pallas_call(
        body, name=name, in_specs=[HBM_SPEC] * n, out_specs=[HBM_SPEC] * n,
        out_shape=[jax.ShapeDtypeStruct((3,) + p.shape[1:], p.dtype) for p in ps],
        scratch_shapes=[pltpu.SemaphoreType.DMA((n, 3)), pltpu.SemaphoreType.DMA((n, 3))],
    )(*ps)


def rs_final_sum(p, got, *, name):
    _, R, C = p.shape
    tr = _tile(R, 768)

    def body(chip_ref, p_ref, got_ref, o_ref):
        s = p_ref[...].astype(F32)
        for j in range(3):
            s = s + got_ref[j].astype(F32)
        o_ref[...] = s

    mychip = jnp.reshape(2 * lax.axis_index("x") + lax.axis_index("y"), (1,)).astype(jnp.int32)
    return pl.pallas_call(
        body, name=name,
        grid_spec=pltpu.PrefetchScalarGridSpec(
            num_scalar_prefetch=1, grid=(R // tr,),
            in_specs=[pl.BlockSpec((None, tr, C), lambda i, chip_ref: (chip_ref[0], i, 0)),
                      pl.BlockSpec((3, tr, C), lambda i, chip_ref: (0, i, 0))],
            out_specs=pl.BlockSpec((tr, C), lambda i, chip_ref: (i, 0))),
        out_shape=jax.ShapeDtypeStruct((R, C), F32), compiler_params=_params(("arbitrary",)),
    )(mychip, p, got)


def reduce_scatter_big(gs):
    got = rs_exchange_sibling(gs, name="rs_sibling")
    parts = [rs_chip_partials(g, t, name=f"rs_chip_partials_{i}") for i, (g, t) in enumerate(zip(gs, got))]
    got2 = rs_exchange_chips(parts, name="rs_chips")
    return [rs_final_sum(p, t, name=f"rs_final_sum_{i}") for i, (p, t) in enumerate(zip(parts, got2))]


def _blk(off):
    return pl.BlockSpec((BLK, LANES), lambda h, n: (n, off + h))


def _const_spec(shape):
    return pl.BlockSpec(shape, lambda *_: (0,) * len(shape))


def retention_tables(T):
    pos = jnp.arange(T, dtype=F32) - float(PAD)
    inv_freq = 1.0 / (10000.0 ** jnp.linspace(0.0, 1.0, 64, dtype=F32))
    ang = pos[:, None] * inv_freq[None, :]
    cos = jnp.repeat(jnp.cos(ang), 2, axis=1)
    sin = jnp.repeat(jnp.sin(ang), 2, axis=1) * jnp.tile(jnp.array([-1.0, 1.0], F32), 64)[None, :]
    lane = np.arange(LANES)
    perm = jnp.asarray((lane[:, None] == (lane[None, :] ^ 1)).astype(np.float32))
    log_gamma = jnp.log1p(-jnp.exp2(-5.0 - jnp.arange(4, dtype=F32)))
    idx = jnp.arange(BLK, dtype=F32)
    diff = idx[:, None] - idx[None, :]
    intra = jnp.where(diff >= 0, jnp.exp(jnp.maximum(diff, 0.0) * log_gamma[:, None, None]), 0.0)
    zeta = jnp.exp((BLK - 1.0 - idx)[None, :] * log_gamma[:, None])
    xi = jnp.exp((idx + 1.0)[None, :] * log_gamma[:, None])
    bc = lambda t: jnp.broadcast_to(t[:, :, None], (4, BLK, LANES))
    return cos, sin, perm, jnp.stack([intra, bc(zeta), bc(xi)], axis=1)


def _ret_chunk(rq, rk, rv, rg, S, cos, sin, tab, perm):
    intra, zeta, xi = tab[0], tab[1], tab[2]
    q = rq * cos + hdot(rq, perm) * sin
    k = (rk * cos + hdot(rk, perm) * sin) * (128.0 ** -0.5)
    ret = bdot(bdot(q, k.T) * intra, rv) + bdot(q * xi, S)
    S_new = S * xi[BLK - 1:BLK, :] + bdot((k * zeta).T, rv)
    c = ret - jnp.mean(ret, axis=-1, keepdims=True)
    out = c * lax.rsqrt(jnp.mean(c * c, axis=-1, keepdims=True) + EPS) * _silu(rg)
    return out, S_new


def retention_fwd(p, tables, *, name):
    T = p.shape[0]
    N = T // BLK
    cos, sin, perm, tab = tables

    def body(rq, rk, rv, rg, cos_ref, sin_ref, tab_ref, perm_ref, out_ref, sall_ref, s_scr):
        @pl.when(pl.program_id(1) == 0)
        def _():
            s_scr[...] = jnp.zeros_like(s_scr)

        S = s_scr[...]
        sall_ref[...] = S
        out, S_new = _ret_chunk(rq[...], rk[...], rv[...], rg[...], S, cos_ref[...], sin_ref[...], tab_ref[...],
                                perm_ref[...])
        out_ref[...] = out.astype(BF16)
        s_scr[...] = S_new

    rowtab = pl.BlockSpec((BLK, LANES), lambda h, n: (n, 0))
    return pl.pallas_call(
        body, name=name, grid=(4, N),
        in_specs=[_blk(0), _blk(4), _blk(8), _blk(12), rowtab, rowtab,
                  pl.BlockSpec((None, 3, BLK, LANES), lambda h, n: (h, 0, 0, 0)), _const_spec((LANES, LANES))],
        out_specs=[_blk(0), pl.BlockSpec((None, None, LANES, LANES), lambda h, n: (h, n, 0, 0))],
        out_shape=[jax.ShapeDtypeStruct((T, 512), BF16), jax.ShapeDtypeStruct((4, N, LANES, LANES), F32)],
        scratch_shapes=[pltpu.VMEM((LANES, LANES), F32)],
        compiler_params=_params(("arbitrary", "arbitrary")),
    )(p, p, p, p, cos, sin, tab, perm)


def _row_mask(n):
    return (n * BLK + _iota2((BLK, 1), 0) >= PAD).astype(F32)


def retention_bwd(p, sall, dmixed, tables, *, name):
    T = p.shape[0]
    N = T // BLK
    cos, sin, perm, tab = tables

    def body(rq, rk, rv, rg, cos_ref, sin_ref, tab_ref, perm_ref, sall_ref, do_ref, drq, drk, drv, drg, ds_scr):
        n = N - 1 - pl.program_id(1)

        @pl.when(pl.program_id(1) == 0)
        def _():
            ds_scr[...] = jnp.zeros_like(ds_scr)

        f = lambda a, b, c, d, s: _ret_chunk(a, b, c, d, s, cos_ref[...], sin_ref[...], tab_ref[...], perm_ref[...])
        _, vjp = jax.vjp(f, rq[...], rk[...], rv[...], rg[...], sall_ref[...])
        g = vjp((do_ref[...], ds_scr[...]))
        mask = _row_mask(n)
        for ref, val in zip((drq, drk, drv, drg), g[:4]):
            ref[...] = val * mask
        ds_scr[...] = g[4]

    def rblk(off):
        return pl.BlockSpec((BLK, LANES), lambda h, n: (N - 1 - n, off + h))

    rowtab = pl.BlockSpec((BLK, LANES), lambda h, n: (N - 1 - n, 0))
    return pl.pallas_call(
        body, name=name, grid=(4, N),
        in_specs=[rblk(0), rblk(4), rblk(8), rblk(12), rowtab, rowtab,
                  pl.BlockSpec((None, 3, BLK, LANES), lambda h, n: (h, 0, 0, 0)), _const_spec((LANES, LANES)),
                  pl.BlockSpec((None, None, LANES, LANES), lambda h, n: (h, N - 1 - n, 0, 0)), rblk(0)],
        out_specs=[rblk(0)] * 4, out_shape=[jax.ShapeDtypeStruct((T, 512), F32)] * 4,
        scratch_shapes=[pltpu.VMEM((LANES, LANES), F32)],
        compiler_params=_params(("arbitrary", "arbitrary")),
    )(p, p, p, p, cos, sin, tab, perm, sall, dmixed)


def conv_silu_fwd(p, w, *, name):
    T = p.shape[0]
    N = T // BLK

    def body(x_ref, xp_ref, w_ref, o_ref):
        n = pl.program_id(0)
        cur = x_ref[...]
        cat = jnp.concatenate([jnp.where(n > 0, xp_ref[...], 0.0), cur], axis=0)
        y = w_ref[3:4, :] * cur
        for s in (1, 2, 3):
            y = y + w_ref[3 - s:4 - s, :] * pltpu.roll(cat, s, 0)[BLK:]
        o_ref[...] = _silu(y)

    return pl.pallas_call(
        body, name=name, grid=(N, 12),
        in_specs=[pl.BlockSpec((BLK, LANES), lambda n, c: (n, 16 + c)),
                  pl.BlockSpec((BLK, LANES), lambda n, c: (jnp.maximum(n - 1, 0), 16 + c)),
                  pl.BlockSpec((4, LANES), lambda n, c: (0, c))],
        out_specs=pl.BlockSpec((BLK, LANES), lambda n, c: (n, c)),
        out_shape=jax.ShapeDtypeStruct((T, 1536), F32), compiler_params=_params(("arbitrary", "arbitrary")),
    )(p, p, w)


def conv_silu_bwd(p, w, dact, part, *, name):
    T = p.shape[0]
    N = T // BLK
    off = 16 + 4 * part

    def body(xp_ref, x_ref, xn_ref, w_ref, da_ref, dan_ref, dx_ref, dw_ref):
        n = pl.program_id(1)
        last = n == N - 1
        cat = jnp.concatenate([jnp.where(n > 0, xp_ref[...], 0.0), x_ref[...], jnp.where(last, 0.0, xn_ref[...])], axis=0)
        shifted = [cat] + [pltpu.roll(cat, s, 0) for s in (1, 2, 3)]
        y = w_ref[3:4, :] * shifted[0]
        for s in (1, 2, 3):
            y = y + w_ref[3 - s:4 - s, :] * shifted[s]
        y = y[BLK:]
        da = jnp.concatenate([da_ref[...], jnp.where(last, 0.0, dan_ref[...])], axis=0)
        sg = _sigmoid(y)
        dy = da * sg * (1.0 + y * (1.0 - sg))
        dx = w_ref[3:4, :] * dy[:BLK]
        for s in (1, 2, 3):
            dx = dx + w_ref[3 - s:4 - s, :] * pltpu.roll(dy, 2 * BLK - s, 0)[:BLK]
        dx_ref[...] = dx * _row_mask(n)

        @pl.when(n == 0)
        def _():
            dw_ref[...] = jnp.zeros_like(dw_ref)

        for s in (0, 1, 2, 3):
            dw_ref[3 - s:4 - s, :] += jnp.sum(dy[:BLK] * shifted[s][BLK:2 * BLK], axis=0, keepdims=True)

    def xs(d):
        return pl.BlockSpec((BLK, LANES), lambda c, n: (jnp.clip(n + d, 0, N - 1), off + c))

    return pl.pallas_call(
        body, name=name, grid=(4, N),
        in_specs=[xs(-1), xs(0), xs(1), pl.BlockSpec((4, LANES), lambda c, n: (0, 4 * part + c)),
                  pl.BlockSpec((BLK, LANES), lambda c, n: (n, c)),
                  pl.BlockSpec((BLK, LANES), lambda c, n: (jnp.minimum(n + 1, N - 1), c))],
        out_specs=[pl.BlockSpec((BLK, LANES), lambda c, n: (n, c)), pl.BlockSpec((4, LANES), lambda c, n: (0, c))],
        out_shape=[jax.ShapeDtypeStruct((T, 512), F32), jax.ShapeDtypeStruct((4, 512), F32)],
        compiler_params=_params(("arbitrary", "arbitrary")),
    )(p, p, p, w, dact, dact)


def _softplus(x):
    return jnp.maximum(x, 0.0) + jnp.log1p(jnp.exp(-jnp.abs(x)))


def _gdn_chunk(qa, ka, va, z, ba, S, alog, dtb, onorm, head, rowmask, lincl):
    r, c = _iota2((BLK, BLK), 0), _iota2((BLK, BLK), 1)
    incl, strict = r >= c, r > c
    eye = (r == c).astype(F32)
    q = qa * lax.rsqrt(jnp.sum(qa * qa, axis=-1, keepdims=True) + EPS) * (128.0 ** -0.5)
    k = ka * lax.rsqrt(jnp.sum(ka * ka, axis=-1, keepdims=True) + EPS)
    beta = _sigmoid(_lane_pick(ba, head)) * rowmask
    a = -jnp.exp(_lane_pick(alog, head))
    g = a * _softplus(_lane_pick(ba, 4 + head) + _lane_pick(dtb, head)) * rowmask
    gc = hdot(lincl, jnp.broadcast_to(g, (BLK, LANES)))
    decay = jnp.where(incl, jnp.exp(jnp.where(incl, gc - gc.T, 0.0)), 0.0)
    kb = k * beta
    amat = jnp.where(strict, bdot(kb, k.T) * decay, 0.0)
    m = -amat
    inv = eye + m
    pw = hdot(m, m)
    for t in range(6):
        inv = inv + hdot(inv, pw)
        if t < 5:
            pw = hdot(pw, pw)
    egc = jnp.exp(gc)
    u = hdot(inv, va * beta)
    w = hdot(inv, kb * egc)
    qk = jnp.where(incl, bdot(q, k.T) * decay, 0.0)
    glast = gc[BLK - 1:BLK, :]
    vnew = u - bdot(w, S)
    o = bdot(q * egc, S) + bdot(qk, vnew)
    S_new = S * jnp.exp(glast) + bdot((k * jnp.exp(glast - gc)).T, vnew)
    out = o * lax.rsqrt(jnp.mean(o * o, axis=-1, keepdims=True) + EPS) * onorm * _silu(z)
    return out, S_new


def _lincl():
    i = np.arange(BLK)
    return jnp.asarray((i[:, None] >= i[None, :]).astype(np.float32))


def gdn_fwd(act, p, alog, dtb, onorm, *, name):
    T = p.shape[0]
    N = T // BLK

    def body(qa, ka, va, z, ba, alog_ref, dtb_ref, on_ref, l_ref, out_ref, sall_ref, s_scr):
        n, h = pl.program_id(0), pl.program_id(1)

        @pl.when(n == 0)
        def _():
            s_scr[h] = jnp.zeros((LANES, LANES), F32)

        S = s_scr[h]
        sall_ref[...] = S
        out, S_new = _gdn_chunk(qa[...], ka[...], va[...], z[...], ba[...], S, alog_ref[...], dtb_ref[...],
                                on_ref[...], h, _row_mask(n), l_ref[...])
        out_ref[...] = out.astype(BF16)
        s_scr[h] = S_new

    def blk(off):
        return pl.BlockSpec((BLK, LANES), lambda n, h: (n, off + h))

    vec = _const_spec((1, LANES))
    return pl.pallas_call(
        body, name=name, grid=(N, 4),
        in_specs=[blk(0), blk(4), blk(8), blk(28), pl.BlockSpec((BLK, LANES), lambda n, h: (n, 32)), vec, vec, vec,
                  _const_spec((BLK, BLK))],
        out_specs=[blk(0), pl.BlockSpec((None, None, LANES, LANES), lambda n, h: (n, h, 0, 0))],
        out_shape=[jax.ShapeDtypeStruct((T, 512), BF16), jax.ShapeDtypeStruct((N, 4, LANES, LANES), F32)],
        scratch_shapes=[pltpu.VMEM((4, LANES, LANES), F32)],
        compiler_params=_params(("arbitrary", "arbitrary")),
    )(act, act, act, p, p, alog, dtb, onorm, _lincl())


def gdn_bwd(act, p, alog, dtb, onorm, sall, dmixed, *, name):
    T = p.shape[0]
    N = T // BLK

    def body(qa, ka, va, z, ba, alog_ref, dtb_ref, on_ref, l_ref, sall_ref, do_ref,
             dq_ref, dk_ref, dv_ref, dz_ref, dba_ref, dal_ref, ddt_ref, don_ref, ds_scr):
        step, h = pl.program_id(0), pl.program_id(1)
        n = N - 1 - step

        @pl.when(step == 0)
        def _():
            ds_scr[h] = jnp.zeros((LANES, LANES), F32)

        @pl.when((step == 0) & (h == 0))
        def _():
            dal_ref[...] = jnp.zeros_like(dal_ref)
            ddt_ref[...] = jnp.zeros_like(ddt_ref)
            don_ref[...] = jnp.zeros_like(don_ref)

        @pl.when(h == 0)
        def _():
            dba_ref[...] = jnp.zeros_like(dba_ref)

        rowmask, lincl = _row_mask(n), l_ref[...]
        f = lambda *a: _gdn_chunk(*a, h, rowmask, lincl)
        _, vjp = jax.vjp(f, qa[...], ka[...], va[...], z[...], ba[...], sall_ref[...], alog_ref[...], dtb_ref[...],
                         on_ref[...])
        g = vjp((do_ref[...], ds_scr[h]))
        dq_ref[...] = g[0] * rowmask
        dk_ref[...] = g[1] * rowmask
        dv_ref[...] = g[2] * rowmask
        dz_ref[...] = g[3] * rowmask
        dba_ref[...] += g[4] * rowmask
        ds_scr[h] = g[5]
        dal_ref[...] += g[6]
        ddt_ref[...] += g[7]
        don_ref[...] += g[8]

    def blk(off):
        return pl.BlockSpec((BLK, LANES), lambda s, h: (N - 1 - s, off + h))

    vec = _const_spec((1, LANES))
    col = pl.BlockSpec((BLK, LANES), lambda s, h: (N - 1 - s, 0))
    return pl.pallas_call(
        body, name=name, grid=(N, 4),
        in_specs=[blk(0), blk(4), blk(8), blk(28), pl.BlockSpec((BLK, LANES), lambda s, h: (N - 1 - s, 32)), vec, vec,
                  vec, _const_spec((BLK, BLK)),
                  pl.BlockSpec((None, None, LANES, LANES), lambda s, h: (N - 1 - s, h, 0, 0)), blk(4)],
        out_specs=[blk(0)] * 4 + [col, vec, vec, vec],
        out_shape=[jax.ShapeDtypeStruct((T, 512), F32)] * 4 + [jax.ShapeDtypeStruct((T, LANES), F32)]
        + [jax.ShapeDtypeStruct((1, LANES), F32)] * 3,
        scratch_shapes=[pltpu.VMEM((4, LANES, LANES), F32)],
        compiler_params=_params(("arbitrary", "arbitrary")),
    )(act, act, act, p, p, alog, dtb, onorm, _lincl(), sall, dmixed)


NEG = -1e30


def _swa_block(q, k0, kp, kc, v0, vp, vc, sinkrow, head, n):
    r, c = _iota2((BLK, BLK), 0), _iota2((BLK, BLK), 1)
    m0 = (c >= PAD) & (c <= n * BLK + r)
    mp = (n >= 2) & (c > r)
    mc = (n >= 1) & (r >= c)
    qs = q * (64.0 ** -0.5)
    s0 = jnp.where(m0, bdot(qs, k0.T), NEG)
    sp = jnp.where(mp, bdot(qs, kp.T), NEG)
    sc = jnp.where(mc, bdot(qs, kc.T), NEG)
    sink = _lane_pick(sinkrow, head)
    mx = jnp.maximum(jnp.max(jnp.maximum(jnp.maximum(s0, sp), sc), axis=-1, keepdims=True), sink)
    mx = lax.stop_gradient(mx)
    p0, pp, pc = jnp.exp(s0 - mx), jnp.exp(sp - mx), jnp.exp(sc - mx)
    den = (jnp.sum(p0, axis=-1, keepdims=True) + jnp.sum(pp, axis=-1, keepdims=True)
           + jnp.sum(pc, axis=-1, keepdims=True) + jnp.exp(sink - mx))
    return (bdot(p0, v0) + bdot(pp, vp) + bdot(pc, vc)) / den


def swa_fwd(p2, sinkrow, *, name):
    T = p2.shape[0]
    N = T // BLK

    def body(q, k0, kp, kc, v0, vp, vc, sink_ref, o_ref):
        h, n = pl.program_id(0), pl.program_id(1)
        o_ref[...] = _swa_block(q[...], k0[...], kp[...], kc[...], v0[...], vp[...], vc[...], sink_ref[...], h,
                                n).astype(BF16)

    def kv(off, which):
        row = {"meta": lambda n: 0, "prev": lambda n: jnp.maximum(n - 1, 0), "cur": lambda n: n}[which]
        return pl.BlockSpec((BLK, LANES), lambda h, n: (row(n), off + h // 4))

    return pl.pallas_call(
        body, name=name, grid=(8, N),
        in_specs=[_blk(0), kv(8, "meta"), kv(8, "prev"), kv(8, "cur"), kv(10, "meta"), kv(10, "prev"), kv(10, "cur"),
                  _const_spec((1, LANES))],
        out_specs=_blk(0), out_shape=jax.ShapeDtypeStruct((T, 1024), BF16),
        compiler_params=_params(("arbitrary", "arbitrary")),
    )(p2, p2, p2, p2, p2, p2, p2, sinkrow)


def swa_bwd(p2, sinkrow, dmixed, *, name):
    T = p2.shape[0]
    N = T // BLK

    def body(q, k0, kp, kc, v0, vp, vc, sink_ref, do_ref, dq_ref, dk_ref, dv_ref, dsink_ref):
        kvh, g, n = pl.program_id(0), pl.program_id(1), pl.program_id(2)
        h = 4 * kvh + g

        @pl.when((g == 0) & (n == 0))
        def _():
            dk_ref[...] = jnp.zeros_like(dk_ref)
            dv_ref[...] = jnp.zeros_like(dv_ref)

        @pl.when((kvh == 0) & (g == 0) & (n == 0))
        def _():
            dsink_ref[...] = jnp.zeros_like(dsink_ref)

        f = lambda *a: _swa_block(*a, h, n)
        _, vjp = jax.vjp(f, q[...], k0[...], kp[...], kc[...], v0[...], vp[...], vc[...], sink_ref[...])
        dq, dk0, dkp, dkc, dv0, dvp, dvc, dsink = vjp(do_ref[...])
        dq_ref[...] = dq
        prev = pl.ds(pl.multiple_of(jnp.maximum(n - 1, 0) * BLK, BLK), BLK)
        cur = pl.ds(pl.multiple_of(n * BLK, BLK), BLK)
        for ref, d0, dp, dc in ((dk_ref, dk0, dkp, dkc), (dv_ref, dv0, dvp, dvc)):
            ref[0:BLK, :] += d0
            ref[prev, :] += dp
            ref[cur, :] += dc
        dsink_ref[...] += dsink

    def kv(off, which):
        row = {"meta": lambda n: 0, "prev": lambda n: jnp.maximum(n - 1, 0), "cur": lambda n: n}[which]
        return pl.BlockSpec((BLK, LANES), lambda kvh, g, n: (row(n), off + kvh))

    qspec = pl.BlockSpec((BLK, LANES), lambda kvh, g, n: (n, 4 * kvh + g))
    slab = pl.BlockSpec((T, LANES), lambda kvh, g, n: (0, kvh))
    return pl.pallas_call(
        body, name=name, grid=(2, 4, N),
        in_specs=[qspec, kv(8, "meta"), kv(8, "prev"), kv(8, "cur"), kv(10, "meta"), kv(10, "prev"), kv(10, "cur"),
                  _const_spec((1, LANES)), qspec],
        out_specs=[qspec, slab, slab, _const_spec((1, LANES))],
        out_shape=[jax.ShapeDtypeStruct((T, 1024), F32), jax.ShapeDtypeStruct((T, 256), F32),
                   jax.ShapeDtypeStruct((T, 256), F32), jax.ShapeDtypeStruct((1, LANES), F32)],
        compiler_params=_params(("arbitrary", "arbitrary", "arbitrary")),
    )(p2, p2, p2, p2, p2, p2, p2, sinkrow, dmixed)


def _split_dot(x, m):
    rows = x.shape[0]
    hi = x.astype(BF16)
    lo = (x - hi.astype(F32)).astype(BF16)
    r = _nn(jnp.concatenate([hi, lo], axis=0), m)
    return r[:rows] + r[rows:]


def _tri(strict):
    i = np.arange(BLK)
    m = (i[:, None] > i[None, :]) if strict else (i[:, None] >= i[None, :])
    return jnp.asarray(m.astype(np.float32), dtype=BF16)


def _chunk_carries(rowsums, carry, kb):
    out = [None] * kb
    for c in reversed(range(kb)):
        out[c] = carry
        carry = carry + rowsums[c * BLK:(c + 1) * BLK]
    return jnp.concatenate(out, axis=0), carry


def _sb_weights(qb, ks, base, n, carry, after):
    kb = len(ks)
    z = jnp.concatenate([_nt(qb, kc) for kc in ks], axis=0)
    row, lane = _iota2(z.shape, 0), _iota2(z.shape, 1)
    kpos = (base * BLK + lane) + (row & ~(BLK - 1))
    valid = (kpos < n * BLK + (row & (BLK - 1))) & (kpos >= PAD)
    lb = jnp.minimum(z, 0.0) - jnp.log1p(jnp.exp(-jnp.abs(z)))
    lm = jnp.where(valid, lb - z, 0.0)
    behind, carry = _chunk_carries(jnp.sum(lm, axis=1, keepdims=True), carry, kb)
    a = jnp.where(valid, jnp.exp(lb + behind + _split_dot(lm, after)), 0.0)
    return valid, lb, a, carry


def _key_blocks(n_blocks):
    return next(k for k in (5, 3, 1) if n_blocks % k == 0)


def sb_fwd(p2, *, name):
    T = p2.shape[0]
    N = T // BLK
    kb = _key_blocks(N)

    def body(q_ref, k_ref, v_ref, after_ref, o_ref, of_ref):
        n = pl.program_id(1)
        qb = (q_ref[...] * (64.0 ** -0.5)).astype(BF16)
        after = after_ref[...]

        nsup = n // kb + 1

        def step(t, c):
            acc, carry = c
            base = (nsup - 1 - t) * kb
            rows = [pl.ds(pl.multiple_of((base + sub) * BLK, BLK), BLK) for sub in range(kb)]
            _, _, a, carry = _sb_weights(qb, [k_ref[r, :].astype(BF16) for r in rows], base, n, carry, after)
            for sub, r in enumerate(rows):
                acc = acc + _split_dot(a[sub * BLK:(sub + 1) * BLK], v_ref[r, :].astype(BF16))
            return acc, carry

        acc, _ = lax.fori_loop(0, nsup, step, (jnp.zeros((BLK, LANES), F32), jnp.zeros((BLK, 1), F32)))
        o_ref[...] = acc.astype(BF16)
        of_ref[...] = acc

    def slab(off):
        return pl.BlockSpec((T, LANES), lambda h, n: (0, off + h))

    return pl.pallas_call(
        body, name=name, grid=(8, N),
        in_specs=[_blk(12), slab(20), slab(28), _const_spec((BLK, BLK))],
        out_specs=[_blk(0), _blk(0)],
        out_shape=[jax.ShapeDtypeStruct((T, 1024), BF16), jax.ShapeDtypeStruct((T, 1024), F32)],
        compiler_params=_params(("arbitrary", "arbitrary")),
    )(p2, p2, p2, _tri(True))


def sb_bwd(p2, o, dmixed, *, name):
    T = p2.shape[0]
    N = T // BLK
    kb = _key_blocks(N)

    def body(q_ref, k_ref, v_ref, after_ref, from_ref, o_ref, do_ref, dq_ref, dk_ref, dv_ref, dkt_scr, dvt_scr):
        n = pl.program_id(1)

        @pl.when(n == 0)
        def _():
            dkt_scr[...] = jnp.zeros_like(dkt_scr)
            dvt_scr[...] = jnp.zeros_like(dvt_scr)

        q = q_ref[...]
        qb = (q * (64.0 ** -0.5)).astype(BF16)
        qt = q.T.astype(BF16)
        do = do_ref[...]
        dob = do.astype(BF16)
        dot_ = do.T.astype(BF16)
        total = jnp.sum(dob.astype(F32) * o_ref[...], axis=1, keepdims=True)
        total = jnp.concatenate([total] * kb, axis=0)
        after, frm = after_ref[...], from_ref[...]

        nsup = n // kb + 1

        def step(t, c):
            dq, carry, gcarry = c
            base = (nsup - 1 - t) * kb
            rows = [pl.ds(pl.multiple_of((base + sub) * BLK, BLK), BLK) for sub in range(kb)]
            ks = [k_ref[r, :].astype(BF16) for r in rows]
            valid, lb, a, carry = _sb_weights(qb, ks, base, n, carry, after)
            g = jnp.concatenate([_nt(dob, v_ref[r, :].astype(BF16)) for r in rows], axis=0) * a
            behind, gcarry = _chunk_carries(jnp.sum(g, axis=1, keepdims=True), gcarry, kb)
            before = total - (behind + _split_dot(g, frm))
            beta = jnp.exp(lb)
            dz = (jnp.where(valid, g * (1.0 - beta) - beta * before, 0.0) * (64.0 ** -0.5)).astype(BF16)
            ab = a.astype(BF16)
            for sub in range(kb):
                part = slice(sub * BLK, (sub + 1) * BLK)
                dq = dq + _nn(dz[part], ks[sub])
                dkt_scr[base + sub] += _nn(qt, dz[part])
                dvt_scr[base + sub] += _nn(dot_, ab[part])
            return dq, carry, gcarry

        zero = jnp.zeros((BLK, 1), F32)
        dq, _, _ = lax.fori_loop(0, nsup, step, (jnp.zeros((BLK, LANES), F32), zero, zero))
        dq_ref[...] = dq

        @pl.when(n == N - 1)
        def _():
            def flush(j, _):
                rows = pl.ds(pl.multiple_of(j * BLK, BLK), BLK)
                dk_ref[rows, :] = dkt_scr[j].T
                dv_ref[rows, :] = dvt_scr[j].T
                return 0

            lax.fori_loop(0, N, flush, 0)

    def slab(off):
        return pl.BlockSpec((T, LANES), lambda h, n: (0, off + h))

    return pl.pallas_call(
        body, name=name, grid=(8, N),
        in_specs=[_blk(12), slab(20), slab(28), _const_spec((BLK, BLK)), _const_spec((BLK, BLK)), _blk(0), _blk(8)],
        out_specs=[_blk(0), slab(0), slab(0)],
        out_shape=[jax.ShapeDtypeStruct((T, 1024), F32)] * 3,
        scratch_shapes=[pltpu.VMEM((N, LANES, LANES), F32), pltpu.VMEM((N, LANES, LANES), F32)],
        compiler_params=_params(("arbitrary", "arbitrary")),
    )(p2, p2, p2, _tri(True), _tri(False), o, dmixed)


def ffn_fwd(h, g_pre, g_post, wg, wu, wd, tag):
    u, gate, up, act = norm_mm(h, g_pre, (wg, wu), swiglu=True, wt=True, name=f"ffn_up_{tag}")
    y, h_new = mm_norm_res([act], [wd], h, g_post, 0.5, name=f"ffn_down_{tag}")
    return h_new, (h, u, gate, up, y)


def ffn_bwd(saved, dh, g_pre, g_post, wg, wu, wd, tag):
    h, u, gate, up, y = saved
    dy, dg_post, dgate, dup, act = normbwd_mm_nt(dh, y, g_post, wd, 0.5, (gate, up), name=f"ffn_bwd_down_{tag}")
    dwd = mm_tn(act, dy, name=f"ffn_dwd_{tag}")
    dwg = mm_tn(dgate, u, name=f"ffn_dwg_{tag}")
    dwu = mm_tn(dup, u, name=f"ffn_dwu_{tag}")
    dh_in, dg_pre = mm_nt_normbwd([dgate, dup], [wg, wu], h, g_pre, dh, wt=True, name=f"ffn_bwd_up_{tag}")
    return dh_in, (dg_pre, dg_post), (dwg, dwu, dwd)


def _lane_row(v):
    v = v.reshape(1, -1)
    return jnp.pad(v, ((0, 0), (0, LANES - v.shape[1])))


AB_WIDTHS = (512,) * 8 + (LANES,)


def mixer_ab_fwd(h, g_pre, g_post, w_in, conv_w, a_log, dt_bias, out_norm, w_out, tables):
    u, p = norm_mm(h, g_pre, (w_in,), swiglu=False, name="ab_in")
    ret, sall_r = retention_fwd(p, tables, name="retention_fwd")
    act = conv_silu_fwd(p, conv_w, name="conv_fwd")
    gdn, sall_g = gdn_fwd(act, p, _lane_row(a_log), _lane_row(dt_bias), out_norm.reshape(1, LANES), name="gdn_fwd")
    y, h_new = mm_norm_res([ret, gdn], [w_out[:512], w_out[512:]], h, g_post, 1.0, name="ab_out")
    return h_new, (h, u, p, ret, sall_r, act, gdn, sall_g, y)


def mixer_ab_bwd(saved, dh, g_pre, g_post, w_in, conv_w, a_log, dt_bias, out_norm, w_out, tables):
    h, u, p, ret, sall_r, act, gdn, sall_g, y = saved
    dy, dg_post, dmixed = normbwd_mm_nt(dh, y, g_post, w_out, 1.0, name="ab_bwd_out")
    dw_out = jnp.concatenate([mm_tn(ret, dy, name="ab_dwout_ret"), mm_tn(gdn, dy, name="ab_dwout_gdn")], axis=0)
    pieces = list(retention_bwd(p, sall_r, dmixed, tables, name="retention_bwd"))
    dqa, dka, dva, dz, dba, dalog, ddtb, donorm = gdn_bwd(
        act, p, _lane_row(a_log), _lane_row(dt_bias), out_norm.reshape(1, LANES), sall_g, dmixed, name="gdn_bwd")
    dconv = []
    for part, dact in enumerate((dqa, dka, dva)):
        dx, dw = conv_silu_bwd(p, conv_w, dact, part, name=f"conv_bwd_{part}")
        pieces.append(dx)
        dconv.append(dw)
    pieces += [dz, dba]
    offs = np.cumsum((0,) + AB_WIDTHS)
    w_parts = [w_in[:, a:b] for a, b in zip(offs[:-1], offs[1:])]
    dh_in, dg_pre = mm_nt_normbwd(pieces, w_parts, h, g_pre, dh, name="ab_bwd_in")
    dw_in = jnp.concatenate([mm_tn(u, pc, name=f"ab_dwin_{i}") for i, pc in enumerate(pieces)], axis=1)
    small = (jnp.concatenate(dconv, axis=1), dalog[:, :4], ddtb[:, :4], donorm)
    return dh_in, (dg_pre, dg_post), (dw_in, dw_out), small


CD_WIDTHS = (1024, 256, 256, 1024, 1024, 1024)


def mixer_cd_fwd(h, g_pre, g_post, w_in, sinks, w_out):
    u, p2 = norm_mm(h, g_pre, (w_in,), swiglu=False, name="cd_in")
    swa = swa_fwd(p2, _lane_row(sinks), name="swa_fwd")
    sb, sb_f32 = sb_fwd(p2, name="sb_fwd")
    y, h_new = mm_norm_res([swa, sb], [w_out[:1024], w_out[1024:]], h, g_post, 1.0, name="cd_out")
    return h_new, (h, u, p2, swa, sb, sb_f32, y)


def mixer_cd_bwd(saved, dh, g_pre, g_post, w_in, sinks, w_out):
    h, u, p2, swa, sb, sb_f32, y = saved
    dy, dg_post, dmixed = normbwd_mm_nt(dh, y, g_post, w_out, 1.0, name="cd_bwd_out")
    dw_out = jnp.concatenate([mm_tn(swa, dy, name="cd_dwout_swa"), mm_tn(sb, dy, name="cd_dwout_sb")], axis=0)
    dq_c, dk_c, dv_c, dsink = swa_bwd(p2, _lane_row(sinks), dmixed, name="swa_bwd")
    pieces = [dq_c, dk_c, dv_c] + list(sb_bwd(p2, sb_f32, dmixed, name="sb_bwd"))
    offs = np.cumsum((0,) + CD_WIDTHS)
    w_parts = [w_in[:, a:b] for a, b in zip(offs[:-1], offs[1:])]
    dh_in, dg_pre = mm_nt_normbwd(pieces, w_parts, h, g_pre, dh, name="cd_bwd_in")
    dw_in = jnp.concatenate([mm_tn(u, pc, name=f"cd_dwin_{i}") for i, pc in enumerate(pieces)], axis=1)
    return dh_in, (dg_pre, dg_post), (dw_in, dw_out), dsink[:, :8]


def _pad_heads(w, axis):
    shape = w.shape
    w = w.reshape(shape[:axis] + (shape[axis] // 64, 64) + shape[axis + 1:])
    pad = [(0, 0)] * w.ndim
    pad[axis + 1] = (0, 64)
    return jnp.pad(w, pad).reshape(shape[:axis] + (2 * shape[axis],) + shape[axis + 1:])


def _unpad_heads(w, axis):
    shape = w.shape
    w = w.reshape(shape[:axis] + (shape[axis] // 128, 128) + shape[axis + 1:])
    w = lax.slice_in_dim(w, 0, 64, axis=axis + 1)
    return w.reshape(shape[:axis] + (shape[axis] // 2,) + shape[axis + 1:])


SMALL_SHARDED = (("meta_tokens", (NMETA, LANES), 1), ("norm_gains", (2, 6, LANES), 2), ("ab_conv_w", (1, 4, 192), 2))
SMALL_REPL = (("ab_a_log", (1, 4)), ("ab_dt_bias", (1, 4)), ("ab_out_norm", (1, LANES)), ("cd_sinks", (1, 8)))


def _stack_shards(g, axis):
    full = jnp.moveaxis(g, 0, axis)
    shape = full.shape
    return full.reshape(shape[:axis] + (shape[axis] * shape[axis + 1],) + shape[axis + 2:])


def _split_shards(full, axis):
    shape = full.shape
    g = full.reshape(shape[:axis] + (NDEV, shape[axis] // NDEV) + shape[axis + 1:])
    return jnp.moveaxis(g, axis, 0)


def _pad_rows8(a):
    rows = []
    for x in a:
        flat = x.reshape(x.shape[0], -1)
        n = -(-flat.shape[1] // LANES) * LANES
        rows.append(jnp.pad(flat, ((0, 0), (0, n - flat.shape[1]))).reshape(x.shape[0], n // LANES, LANES))
    cat = jnp.concatenate(rows, axis=1)
    return jnp.pad(cat, ((0, 0), (0, -cat.shape[1] % 8), (0, 0)))


def _unpad_rows8(packed, shapes):
    out, at = [], 0
    for shape in shapes:
        size = int(np.prod(shape))
        nrow = -(-size // LANES)
        blk = packed[:, at:at + nrow].reshape(packed.shape[0], -1)[:, :size]
        out.append(blk.reshape((packed.shape[0],) + tuple(shape)))
        at += nrow
    return out


def kernel(x, meta_tokens, norm_gains, ffn_w_gate, ffn_w_up, ffn_w_down, ab_w_in, ab_conv_w, ab_a_log, ab_dt_bias, ab_out_norm, ab_w_out, cd_w_in, cd_sinks, cd_w_out, loss_target, m_meta_tokens, m_norm_gains, m_ffn_w_gate, m_ffn_w_up, m_ffn_w_down, m_ab_w_in, m_ab_conv_w, m_ab_a_log, m_ab_dt_bias, m_ab_out_norm, m_ab_w_out, m_cd_w_in, m_cd_sinks, m_cd_w_out, v_meta_tokens, v_norm_gains, v_ffn_w_gate, v_ffn_w_up, v_ffn_w_down, v_ab_w_in, v_ab_conv_w, v_ab_a_log, v_ab_dt_bias, v_ab_out_norm, v_ab_w_out, v_cd_w_in, v_cd_sinks, v_cd_w_out):
    w = dict(meta_tokens=meta_tokens, norm_gains=norm_gains, ffn_w_gate=ffn_w_gate, ffn_w_up=ffn_w_up,
             ffn_w_down=ffn_w_down, ab_w_in=ab_w_in, ab_conv_w=ab_conv_w, ab_a_log=ab_a_log, ab_dt_bias=ab_dt_bias,
             ab_out_norm=ab_out_norm, ab_w_out=ab_w_out, cd_w_in=cd_w_in, cd_sinks=cd_sinks, cd_w_out=cd_w_out)
    m = dict(meta_tokens=m_meta_tokens, norm_gains=m_norm_gains, ffn_w_gate=m_ffn_w_gate, ffn_w_up=m_ffn_w_up,
             ffn_w_down=m_ffn_w_down, ab_w_in=m_ab_w_in, ab_conv_w=m_ab_conv_w, ab_a_log=m_ab_a_log,
             ab_dt_bias=m_ab_dt_bias, ab_out_norm=m_ab_out_norm, ab_w_out=m_ab_w_out, cd_w_in=m_cd_w_in,
             cd_sinks=m_cd_sinks, cd_w_out=m_cd_w_out)
    v = dict(meta_tokens=v_meta_tokens, norm_gains=v_norm_gains, ffn_w_gate=v_ffn_w_gate, ffn_w_up=v_ffn_w_up,
             ffn_w_down=v_ffn_w_down, ab_w_in=v_ab_w_in, ab_conv_w=v_ab_conv_w, ab_a_log=v_ab_a_log,
             ab_dt_bias=v_ab_dt_bias, ab_out_norm=v_ab_out_norm, ab_w_out=v_ab_w_out, cd_w_in=v_cd_w_in,
             cd_sinks=v_cd_sinks, cd_w_out=v_cd_w_out)
    order = list(w)
    S = x.shape[1]
    T = S + BLK

    fs = DFF // NDEV
    ffn_local = jnp.concatenate([jnp.swapaxes(ffn_w_gate, 2, 3).reshape(4 * fs, D),
                                 jnp.swapaxes(ffn_w_up, 2, 3).reshape(4 * fs, D), ffn_w_down.reshape(4 * fs, D)],
                                axis=0).astype(BF16)
    outs_local = jnp.concatenate([ab_w_out[0], cd_w_out[0]], axis=0).astype(BF16)
    ffn_all, abin_all, outs_all, cdin_all = all_gather_big(
        [ffn_local, ab_w_in[0].astype(BF16), outs_local, cd_w_in[0].astype(BF16)], name="gather_weights")
    ffn_mat = lambda k: ffn_all[:, k * fs:(k + 1) * fs].reshape(DFF, D)
    layers = [(i, j) for i in range(2) for j in range(2)]
    wg = {ij: ffn_mat(k) for k, ij in enumerate(layers)}
    wu = {ij: ffn_mat(4 + k) for k, ij in enumerate(layers)}
    wd = {ij: ffn_mat(8 + k) for k, ij in enumerate(layers)}
    ab_in = jnp.pad(_stack_shards(abin_all, 1), ((0, 0), (0, AB_INP - AB_IN)))
    ab_out = outs_all[:, :D // NDEV].reshape(D, D)
    cd_in = _pad_heads(_stack_shards(cdin_all, 1), 1)
    cd_out = _pad_heads(outs_all[:, D // NDEV:].reshape(D, D), 0)
    small_src = jnp.broadcast_to(_pad_rows8([w[n][None] for n, _, _ in SMALL_SHARDED]), (NDEV, 40, LANES))
    small_all = _unpad_rows8(all_to_all_small(small_src, name="gather_small"), [s for _, s, _ in SMALL_SHARDED])
    full = {n: _stack_shards(g, ax) for (n, _, ax), g in zip(SMALL_SHARDED, small_all)}
    conv_w = full["ab_conv_w"][0]
    gains = full["norm_gains"].reshape(2, 6, 1, D)
    tables = retention_tables(T)

    h = jnp.concatenate([jnp.zeros((PAD, D), F32), full["meta_tokens"], x[0]], axis=0)
    h, s00 = ffn_fwd(h, gains[0, 0], gains[0, 1], wg[0, 0], wu[0, 0], wd[0, 0], "00")
    h, sab = mixer_ab_fwd(h, gains[0, 2], gains[0, 3], ab_in, conv_w, ab_a_log, ab_dt_bias, ab_out_norm, ab_out, tables)
    h, s01 = ffn_fwd(h, gains[0, 4], gains[0, 5], wg[0, 1], wu[0, 1], wd[0, 1], "01")
    h, s10 = ffn_fwd(h, gains[1, 0], gains[1, 1], wg[1, 0], wu[1, 0], wd[1, 0], "10")
    h, scd = mixer_cd_fwd(h, gains[1, 2], gains[1, 3], cd_in, cd_sinks, cd_out)
    h, s11 = ffn_fwd(h, gains[1, 4], gains[1, 5], wg[1, 1], wu[1, 1], wd[1, 1], "11")
    loss_tile, dh = loss_and_grad(h, loss_target[0], name="loss")
    loss = lax.psum(loss_tile[0, 0], ("x", "y", "c"))

    dgain = [[None] * 6, [None] * 6]
    dffn = {}
    dh, (dgain[1][4], dgain[1][5]), dffn[1, 1] = ffn_bwd(s11, dh, gains[1, 4], gains[1, 5], wg[1, 1], wu[1, 1], wd[1, 1], "11")
    dh, (dgain[1][2], dgain[1][3]), (dcd_in, dcd_out), dsinks = mixer_cd_bwd(scd, dh, gains[1, 2], gains[1, 3], cd_in, cd_sinks, cd_out)
    dh, (dgain[1][0], dgain[1][1]), dffn[1, 0] = ffn_bwd(s10, dh, gains[1, 0], gains[1, 1], wg[1, 0], wu[1, 0], wd[1, 0], "10")
    dh, (dgain[0][4], dgain[0][5]), dffn[0, 1] = ffn_bwd(s01, dh, gains[0, 4], gains[0, 5], wg[0, 1], wu[0, 1], wd[0, 1], "01")
    dh, (dgain[0][2], dgain[0][3]), (dab_in, dab_out), (dconv, dalog, ddtb, donorm) = mixer_ab_bwd(
        sab, dh, gains[0, 2], gains[0, 3], ab_in, conv_w, ab_a_log, ab_dt_bias, ab_out_norm, ab_out, tables)
    dh, (dgain[0][0], dgain[0][1]), dffn[0, 0] = ffn_bwd(s00, dh, gains[0, 0], gains[0, 1], wg[0, 0], wu[0, 0], wd[0, 0], "00")
    grad_x = dh[BLK:][None]

    gfull = dict(meta_tokens=dh[PAD:BLK], norm_gains=jnp.stack([jnp.concatenate(r, axis=0) for r in dgain]),
                 ab_conv_w=dconv[None])
    ffn_send = jnp.concatenate([dffn[ij][k].astype(BF16).reshape(NDEV, fs, D) for k in range(3) for ij in layers], axis=1)
    outs_send = jnp.concatenate([dab_out.astype(BF16).reshape(NDEV, D // NDEV, D),
                                 _unpad_heads(dcd_out, 0).astype(BF16).reshape(NDEV, D // NDEV, D)], axis=1)
    abin_send = _split_shards(dab_in[:, :AB_IN].astype(BF16), 1)
    cdin_send = _split_shards(_unpad_heads(dcd_in, 1).astype(BF16), 1)
    ffn_g, abin_g, outs_g, cdin_g = reduce_scatter_big([ffn_send, abin_send, outs_send, cdin_send])
    ffn_g = ffn_g.reshape(3, 2, 2, fs, D)
    grads = dict(ffn_w_gate=jnp.swapaxes(ffn_g[0], 2, 3), ffn_w_up=jnp.swapaxes(ffn_g[1], 2, 3), ffn_w_down=ffn_g[2],
                 ab_w_in=abin_g[None], ab_w_out=outs_g[None, :D // NDEV], cd_w_in=cdin_g[None],
                 cd_w_out=outs_g[None, D // NDEV:])
    repl = [jnp.broadcast_to(t[None], (NDEV,) + t.shape) for t in (dalog, ddtb, donorm, dsinks)]
    ssend = _pad_rows8([_split_shards(gfull[n], ax) for n, _, ax in SMALL_SHARDED] + repl)
    ssum = sum_slots(all_to_all_small(ssend, name="exchange_small_grads"), name="sum_small_grads")[None]
    small = _unpad_rows8(ssum, [s for _, s, _ in SMALL_SHARDED] + [s for _, s in SMALL_REPL])
    grads.update({n: g[0] for n, g in zip([n for n, _, _ in SMALL_SHARDED] + [n for n, _ in SMALL_REPL], small)})

    delta, new_m, new_v = {}, {}, {}
    for n in order:
        shape = w[n].shape
        view = (-1, shape[-1])
        d_, m_, v_ = adamw(w[n].reshape(view), grads[n].reshape(view), m[n].reshape(view), v[n].reshape(view),
                           name=f"adamw_{n}")
        delta[n], new_m[n], new_v[n] = d_.reshape(shape), m_.reshape(shape), v_.reshape(shape)
    return (loss, grad_x, *[grads[n] for n in order], *[delta[n] for n in order], *[new_m[n] for n in order],
            *[new_v[n] for n in order])
```

```python
import functools
import math

import numpy as np
import jax
import jax.numpy as jnp
from jax import lax
from jax.experimental import pallas as pl
from jax.experimental.pallas import tpu as pltpu

F32, BF16 = jnp.float32, jnp.bfloat16
EPS = 1e-6
D = 1024
NMETA = 16
BLK = 128
PAD = BLK - NMETA
DFF = 2816
LANES = 128
NDEV = 8
AB_IN, AB_INP = 4104, 4224
ADAM_LR, ADAM_B1, ADAM_B2, ADAM_EPS, ADAM_WD, ADAM_STEP = 0.001, 0.9, 0.999, 1e-08, 0.01, 10
VMEM_LIMIT = 56 * 1024 * 1024
MESH = pl.DeviceIdType.MESH
HIGHEST = lax.Precision.HIGHEST


def _params(sem):
    return pltpu.CompilerParams(dimension_semantics=sem, vmem_limit_bytes=VMEM_LIMIT)


def _row_tile(T, streamed, resident):
    for tm in (640, 320, 128):
        if T % tm == 0 and 2 * (tm * streamed + resident) <= VMEM_LIMIT - 14 * 1024 * 1024:
            return tm
    return _tile(T, 128)


def _tile(n, cap):
    if n <= cap:
        return n
    best = None
    for t in range(LANES, cap + 1, LANES):
        if n % t == 0:
            best = t
    assert best is not None, (n, cap)
    return best


def _rms_fwd(x, g):
    return x * lax.rsqrt(jnp.mean(x * x, axis=-1, keepdims=True) + EPS) * g


def _rms_bwd(x, g, dz):
    r = lax.rsqrt(jnp.mean(x * x, axis=-1, keepdims=True) + EPS)
    xh = x * r
    dg = jnp.sum(dz * xh, axis=0, keepdims=True)
    t = dz * g
    return r * (t - xh * jnp.mean(t * xh, axis=-1, keepdims=True)), dg


def _sigmoid(x):
    return 1.0 / (1.0 + jnp.exp(-x))


def _silu(x):
    return x * _sigmoid(x)


def _nn(a, b, precision=None):
    return lax.dot_general(a, b, (((1,), (0,)), ((), ())), preferred_element_type=F32, precision=precision)


def _nt(a, b):
    return lax.dot_general(a, b, (((1,), (1,)), ((), ())), preferred_element_type=F32)


def _tn(a, b):
    return lax.dot_general(a, b, (((0,), (0,)), ((), ())), preferred_element_type=F32)


def _mm(a, b, precision=None):
    if a.ndim == 3:
        return lax.dot_general(a, b, (((2,), (1,)), ((0,), (0,))), preferred_element_type=F32, precision=precision)
    return _nn(a, b, precision)


def _t(x):
    return jnp.swapaxes(x, -1, -2)


@jax.custom_vjp
def bdot(a, b):
    return _mm(a.astype(BF16), b.astype(BF16))


def _bdot_fwd(a, b):
    return bdot(a, b), (a, b)


def _bdot_bwd(res, g):
    a, b = res
    return bdot(g, _t(b)), bdot(_t(a), g)


bdot.defvjp(_bdot_fwd, _bdot_bwd)


@jax.custom_vjp
def hdot(a, b):
    return _mm(a, b, HIGHEST)


def _hdot_fwd(a, b):
    return hdot(a, b), (a, b)


def _hdot_bwd(res, g):
    a, b = res
    return hdot(g, _t(b)), hdot(_t(a), g)


hdot.defvjp(_hdot_fwd, _hdot_bwd)


def _iota2(shape, axis):
    return lax.broadcasted_iota(jnp.int32, shape, axis)


def _lane_pick(row, lane):
    return jnp.sum(jnp.where(_iota2(row.shape, 1) == lane, row, 0.0), axis=1, keepdims=True)


def norm_mm(h, gain, ws, *, swiglu, name, wt=False):
    T, Dm = h.shape
    N = ws[0].shape[0 if wt else 1]
    tm, tn = _tile(T, 640), _tile(N, 1408)
    nw = len(ws)
    mm = _nt if wt else _nn

    def body(h_ref, g_ref, *refs):
        w_refs, u_ref, o_refs = refs[:nw], refs[nw], refs[nw + 1:]

        @pl.when(pl.program_id(1) == 0)
        def _():
            u_ref[...] = _rms_fwd(h_ref[...], g_ref[...]).astype(BF16)

        u = u_ref[...]
        acc = [mm(u, w[...]) for w in w_refs]
        if swiglu:
            o_refs[0][...] = acc[0].astype(BF16)
            o_refs[1][...] = acc[1].astype(BF16)
            o_refs[2][...] = (_silu(acc[0]) * acc[1]).astype(BF16)
        else:
            o_refs[0][...] = acc[0]

    row = pl.BlockSpec((tm, Dm), lambda i, j: (i, 0))
    tile = pl.BlockSpec((tm, tn), lambda i, j: (i, j))
    if swiglu:
        out_shape = [jax.ShapeDtypeStruct((T, Dm), BF16)] + [jax.ShapeDtypeStruct((T, N), BF16)] * 3
        out_specs = [row, tile, tile, tile]
    else:
        out_shape = [jax.ShapeDtypeStruct((T, Dm), BF16), jax.ShapeDtypeStruct((T, N), F32)]
        out_specs = [row, tile]
    return pl.pallas_call(
        body, name=name, grid=(T // tm, N // tn),
        in_specs=[row, pl.BlockSpec((1, Dm), lambda i, j: (0, 0))]
        + [pl.BlockSpec((tn, Dm), lambda i, j: (j, 0)) if wt else pl.BlockSpec((Dm, tn), lambda i, j: (0, j))] * nw,
        out_specs=out_specs, out_shape=out_shape,
        compiler_params=_params(("arbitrary", "arbitrary")),
    )(h, gain, *ws)


def mm_norm_res(As, Ws, h, gain, scale, *, name):
    T, Dm = h.shape
    n = len(As)
    tm = _row_tile(T, sum(a.shape[1] * a.dtype.itemsize for a in As) + 3 * Dm * 4,
                   sum(w.size * w.dtype.itemsize for w in Ws))

    def body(*refs):
        a_refs, w_refs = refs[:n], refs[n:2 * n]
        h_ref, g_ref, y_ref, hn_ref = refs[2 * n:]
        y = _nn(a_refs[0][...].astype(BF16), w_refs[0][...])
        for a, w in zip(a_refs[1:], w_refs[1:]):
            y = y + _nn(a[...].astype(BF16), w[...])
        y_ref[...] = y
        hn_ref[...] = h_ref[...] + scale * _rms_fwd(y, g_ref[...])

    row = pl.BlockSpec((tm, Dm), lambda i: (i, 0))
    return pl.pallas_call(
        body, name=name, grid=(T // tm,),
        in_specs=[pl.BlockSpec((tm, a.shape[1]), lambda i: (i, 0)) for a in As]
        + [pl.BlockSpec(w.shape, lambda i: (0, 0)) for w in Ws]
        + [row, pl.BlockSpec((1, Dm), lambda i: (0, 0))],
        out_specs=[row, row], out_shape=[jax.ShapeDtypeStruct((T, Dm), F32)] * 2,
        compiler_params=_params(("arbitrary",)),
    )(*As, *Ws, h, gain)


def normbwd_mm_nt(dh, y, gain, w, scale, gu=None, *, name):
    T, Dm = dh.shape
    N = w.shape[0]
    tm, tn = _tile(T, 640), _tile(N, 1408)
    swiglu = gu is not None

    def body(dh_ref, y_ref, g_ref, w_ref, *refs):
        if swiglu:
            gate_ref, up_ref, dy_ref, dg_ref, dgate_ref, dup_ref, a_ref = refs
        else:
            dy_ref, dg_ref, da_ref = refs
        i, j = pl.program_id(0), pl.program_id(1)

        @pl.when(j == 0)
        def _():
            dy, dg = _rms_bwd(y_ref[...], g_ref[...], scale * dh_ref[...])
            dy_ref[...] = dy.astype(BF16)

            @pl.when(i == 0)
            def _():
                dg_ref[...] = jnp.zeros_like(dg_ref)

            dg_ref[...] += dg

        da = _nt(dy_ref[...], w_ref[...])
        if swiglu:
            gate, up = gate_ref[...].astype(F32), up_ref[...].astype(F32)
            s = _sigmoid(gate)
            dgate_ref[...] = (da * up * s * (1.0 + gate * (1.0 - s))).astype(BF16)
            dup_ref[...] = (da * gate * s).astype(BF16)
            a_ref[...] = (gate * s * up).astype(BF16)
        else:
            da_ref[...] = da

    row = pl.BlockSpec((tm, Dm), lambda i, j: (i, 0))
    vec = pl.BlockSpec((1, Dm), lambda i, j: (0, 0))
    tile = pl.BlockSpec((tm, tn), lambda i, j: (i, j))
    in_specs = [row, row, vec, pl.BlockSpec((tn, Dm), lambda i, j: (j, 0))]
    out_shape = [jax.ShapeDtypeStruct((T, Dm), BF16), jax.ShapeDtypeStruct((1, Dm), F32)]
    if swiglu:
        in_specs += [tile, tile]
        out_shape += [jax.ShapeDtypeStruct((T, N), BF16)] * 3
        out_specs = [row, vec, tile, tile, tile]
        args = (dh, y, gain, w, *gu)
    else:
        out_shape += [jax.ShapeDtypeStruct((T, N), F32)]
        out_specs = [row, vec, tile]
        args = (dh, y, gain, w)
    return pl.pallas_call(
        body, name=name, grid=(T // tm, N // tn), in_specs=in_specs, out_specs=out_specs,
        out_shape=out_shape, compiler_params=_params(("arbitrary", "arbitrary")),
    )(*args)


def mm_nt_normbwd(dPs, Ws, h, gain, dh_in, *, name, wt=False):
    T, Dm = h.shape
    n = len(dPs)
    tm = _row_tile(T, sum(p.shape[1] * p.dtype.itemsize for p in dPs) + 3 * Dm * 4,
                   sum(w.size * w.dtype.itemsize for w in Ws))
    mm = _nn if wt else _nt

    def body(*refs):
        p_refs, w_refs = refs[:n], refs[n:2 * n]
        h_ref, g_ref, dhin_ref, dh_ref, dg_ref = refs[2 * n:]
        du = mm(p_refs[0][...].astype(BF16), w_refs[0][...])
        for p, w in zip(p_refs[1:], w_refs[1:]):
            du = du + mm(p[...].astype(BF16), w[...])
        dx, dg = _rms_bwd(h_ref[...], g_ref[...], du)
        dh_ref[...] = dhin_ref[...] + dx

        @pl.when(pl.program_id(0) == 0)
        def _():
            dg_ref[...] = jnp.zeros_like(dg_ref)

        dg_ref[...] += dg

    row = pl.BlockSpec((tm, Dm), lambda i: (i, 0))
    vec = pl.BlockSpec((1, Dm), lambda i: (0, 0))
    return pl.pallas_call(
        body, name=name, grid=(T // tm,),
        in_specs=[pl.BlockSpec((tm, p.shape[1]), lambda i: (i, 0)) for p in dPs]
        + [pl.BlockSpec(w.shape, lambda i: (0, 0)) for w in Ws] + [row, vec, row],
        out_specs=[row, vec],
        out_shape=[jax.ShapeDtypeStruct((T, Dm), F32), jax.ShapeDtypeStruct((1, Dm), F32)],
        compiler_params=_params(("arbitrary",)),
    )(*dPs, *Ws, h, gain, dh_in)


def mm_tn(a, b, *, name):
    T, M = a.shape
    N = b.shape[1]
    tm, tn, tk = _tile(M, 1408), _tile(N, 1408), _tile(T, 640)

    def body(a_ref, b_ref, o_ref):
        @pl.when(pl.program_id(2) == 0)
        def _():
            o_ref[...] = jnp.zeros_like(o_ref)

        o_ref[...] += _tn(a_ref[...].astype(BF16), b_ref[...].astype(BF16))

    return pl.pallas_call(
        body, name=name, grid=(M // tm, N // tn, T // tk),
        in_specs=[pl.BlockSpec((tk, tm), lambda i, j, k: (k, i)), pl.BlockSpec((tk, tn), lambda i, j, k: (k, j))],
        out_specs=pl.BlockSpec((tm, tn), lambda i, j, k: (i, j)),
        out_shape=jax.ShapeDtypeStruct((M, N), F32),
        compiler_params=_params(("arbitrary", "arbitrary", "arbitrary")),
    )(a, b)


def loss_and_grad(h, target, *, name):
    T, Dm = h.shape

    def body(h_ref, t_ref, loss_ref, dh_ref):
        b = pl.program_id(0)

        @pl.when(b == 0)
        def _():
            loss_ref[...] = jnp.zeros_like(loss_ref)
            dh_ref[...] = jnp.zeros_like(dh_ref)

        @pl.when(b > 0)
        def _():
            e = h_ref[...] - t_ref[...]
            dh_ref[...] = e * (1.0 / Dm)
            loss_ref[...] += jnp.sum(e * e) * (0.5 / Dm)

    return pl.pallas_call(
        body, name=name, grid=(T // BLK,),
        in_specs=[pl.BlockSpec((BLK, Dm), lambda b: (b, 0)),
                  pl.BlockSpec((BLK, Dm), lambda b: (jnp.maximum(b - 1, 0), 0))],
        out_specs=[pl.BlockSpec((8, LANES), lambda b: (0, 0)), pl.BlockSpec((BLK, Dm), lambda b: (b, 0))],
        out_shape=[jax.ShapeDtypeStruct((8, LANES), F32), jax.ShapeDtypeStruct((T, Dm), F32)],
        compiler_params=_params(("arbitrary",)),
    )(h, target)


def adamw(w, g, m, v, *, name):
    R, C = w.shape
    tr = R
    for t in (512, 352, 256):
        if R > t and R % t == 0:
            tr = t
            break

    def body(w_ref, g_ref, m_ref, v_ref, d_ref, nm_ref, nv_ref):
        g_ = g_ref[...]
        m_ = ADAM_B1 * m_ref[...] + (1.0 - ADAM_B1) * g_
        v_ = ADAM_B2 * v_ref[...] + (1.0 - ADAM_B2) * (g_ * g_)
        m_hat = m_ / (1.0 - ADAM_B1 ** ADAM_STEP)
        v_hat = v_ / (1.0 - ADAM_B2 ** ADAM_STEP)
        d_ref[...] = -ADAM_LR * (m_hat / (jnp.sqrt(v_hat) + ADAM_EPS) + ADAM_WD * w_ref[...])
        nm_ref[...] = m_
        nv_ref[...] = v_

    spec = pl.BlockSpec((tr, C), lambda i: (i, 0))
    return pl.pallas_call(
        body, name=name, grid=(R // tr,), in_specs=[spec] * 4, out_specs=[spec] * 3,
        out_shape=[jax.ShapeDtypeStruct((R, C), F32)] * 3, compiler_params=_params(("arbitrary",)),
    )(w, g, m, v)


def _me():
    return lax.axis_index("x"), lax.axis_index("y"), lax.axis_index("c")


def _flip(pos, rel):
    return tuple(1 - p if r else p for p, r in zip(pos, rel))


def _slot(pos):
    return 4 * pos[0] + 2 * pos[1] + pos[2]


HBM_SPEC = pl.BlockSpec(memory_space=pltpu.HBM)
CHIP_RELS = ((1, 0), (0, 1), (1, 1))


def all_gather_big(xs, *, name):
    n = len(xs)

    def body(*refs):
        x_refs, out_refs = refs[:n], refs[n:2 * n]
        send_sems, recv_sems, local_sems = refs[2 * n:]
        me = _me()
        sibling = _flip(me, (0, 0, 1))
        chips = [_flip(me, rel + (0,)) for rel in CHIP_RELS]

        def copy(i, k, block, to, src=None):
            dst = out_refs[i].at[_slot(block)]
            return pltpu.make_async_remote_copy(
                src_ref=dst if src is None else src, dst_ref=dst, send_sem=send_sems.at[i, k],
                recv_sem=recv_sems.at[i, k], device_id=to, device_id_type=MESH)

        sent, local = [], []
        for i in range(n):
            mine = pltpu.make_async_copy(x_refs[i], out_refs[i].at[_slot(me)], local_sems.at[i])
            mine.start()
            local.append(mine)
            sent += [copy(i, 0, me, sibling, src=x_refs[i])]
            sent += [copy(i, 1 + j, me, chip, src=x_refs[i]) for j, chip in enumerate(chips)]
        for cp in sent:
            cp.start()
        for i in range(n):
            for j, chip in enumerate(chips):
                copy(i, 1 + j, chip, me).wait_recv()
                passed = copy(i, 4 + j, chip, sibling)
                passed.start()
                sent.append(passed)
        for i in range(n):
            copy(i, 0, sibling, me).wait_recv()
            for j, chip in enumerate(chips):
                copy(i, 4 + j, _flip(chip, (0, 0, 1)), me).wait_recv()
        for cp in sent:
            cp.wait_send()
        for mine in local:
            mine.wait()

    return pl.pallas_call(
        body, name=name, in_specs=[HBM_SPEC] * n, out_specs=[HBM_SPEC] * n,
        out_shape=[jax.ShapeDtypeStruct((NDEV,) + x.shape, x.dtype) for x in xs],
        scratch_shapes=[pltpu.SemaphoreType.DMA((n, 7)), pltpu.SemaphoreType.DMA((n, 7)), pltpu.SemaphoreType.DMA((n,))],
    )(*xs)


def all_to_all_small(src, *, name):
    _, r, C = src.shape

    def body(src_ref, out_ref, send_sems, recv_sems):
        me = _me()
        my = _slot(me)
        out_ref[my] = src_ref[my]
        copies = []
        for k in range(1, NDEV):
            peer = _flip(me, ((k >> 2) & 1, (k >> 1) & 1, k & 1))
            cp = pltpu.make_async_remote_copy(
                src_ref=src_ref.at[_slot(peer)], dst_ref=out_ref.at[my], send_sem=send_sems.at[k - 1],
                recv_sem=recv_sems.at[k - 1], device_id=peer, device_id_type=MESH)
            cp.start()
            copies.append((cp, peer))
        for k, (cp, peer) in enumerate(copies):
            pltpu.make_async_remote_copy(
                src_ref=src_ref.at[my], dst_ref=out_ref.at[_slot(peer)], send_sem=send_sems.at[k],
                recv_sem=recv_sems.at[k], device_id=peer, device_id_type=MESH).wait_recv()
        for cp, _ in copies:
            cp.wait_send()

    vm = pl.BlockSpec(memory_space=pltpu.VMEM)
    return pl.pallas_call(
        body, name=name, in_specs=[vm], out_specs=vm, out_shape=jax.ShapeDtypeStruct(src.shape, src.dtype),
        scratch_shapes=[pltpu.SemaphoreType.DMA((7,)), pltpu.SemaphoreType.DMA((7,))],
    )(src)


def sum_slots(a, *, name):
    n, r, C = a.shape

    def body(a_ref, o_ref):
        s = a_ref[0]
        for k in range(1, n):
            s = s + a_ref[k]
        o_ref[...] = s

    vm = pl.BlockSpec(memory_space=pltpu.VMEM)
    return pl.pallas_call(body, name=name, in_specs=[vm], out_specs=vm,
                          out_shape=jax.ShapeDtypeStruct((r, C), F32))(a)


def rs_exchange_sibling(gs, *, name):
    n = len(gs)

    def body(*refs):
        g_refs, out_refs, send_sems, recv_sems = refs[:n], refs[n:2 * n], refs[2 * n], refs[2 * n + 1]
        sibling = _flip(_me(), (0, 0, 1))
        copies = []
        for i in range(n):
            for chip in range(4):
                cp = pltpu.make_async_remote_copy(
                    src_ref=g_refs[i].at[2 * chip + sibling[2]], dst_ref=out_refs[i].at[chip],
                    send_sem=send_sems.at[i, chip], recv_sem=recv_sems.at[i, chip], device_id=sibling,
                    device_id_type=MESH)
                cp.start()
                copies.append(cp)
        for cp in copies:
            cp.wait()

    return pl.pallas_call(
        body, name=name, in_specs=[HBM_SPEC] * n, out_specs=[HBM_SPEC] * n,
        out_shape=[jax.ShapeDtypeStruct((4,) + g.shape[1:], g.dtype) for g in gs],
        scratch_shapes=[pltpu.SemaphoreType.DMA((n, 4)), pltpu.SemaphoreType.DMA((n, 4))],
    )(*gs)


def rs_chip_partials(g, got, *, name):
    _, R, C = g.shape
    tr = _tile(R, 768)

    def body(c_ref, g_ref, got_ref, o_ref):
        o_ref[...] = (g_ref[...].astype(F32) + got_ref[...].astype(F32)).astype(o_ref.dtype)

    c = jnp.reshape(lax.axis_index("c"), (1,)).astype(jnp.int32)
    return pl.pallas_call(
        body, name=name,
        grid_spec=pltpu.PrefetchScalarGridSpec(
            num_scalar_prefetch=1, grid=(4, R // tr),
            in_specs=[pl.BlockSpec((None, tr, C), lambda k, i, c_ref: (2 * k + c_ref[0], i, 0)),
                      pl.BlockSpec((None, tr, C), lambda k, i, c_ref: (k, i, 0))],
            out_specs=pl.BlockSpec((None, tr, C), lambda k, i, c_ref: (k, i, 0))),
        out_shape=jax.ShapeDtypeStruct((4, R, C), g.dtype), compiler_params=_params(("arbitrary", "arbitrary")),
    )(c, g, got)


def rs_exchange_chips(ps, *, name):
    n = len(ps)

    def body(*refs):
        p_refs, out_refs, send_sems, recv_sems = refs[:n], refs[n:2 * n], refs[2 * n], refs[2 * n + 1]
        me = _me()
        copies = []
        for i in range(n):
            for j, rel in enumerate(CHIP_RELS):
                peer = _flip(me, rel + (0,))
                cp = pltpu.make_async_remote_copy(
                    src_ref=p_refs[i].at[2 * peer[0] + peer[1]], dst_ref=out_refs[i].at[j], send_sem=send_sems.at[i, j],
                    recv_sem=recv_sems.at[i, j], device_id=peer, device_id_type=MESH)
                cp.start()
                copies.append(cp)
        for cp in copies:
            cp.wait()

    return pl.pallas_call(
        body, name=name, in_specs=[HBM_SPEC] * n, out_specs=[HBM_SPEC] * n,
        out_shape=[jax.ShapeDtypeStruct((3,) + p.shape[1:], p.dtype) for p in ps],
        scratch_shapes=[pltpu.SemaphoreType.DMA((n, 3)), pltpu.SemaphoreType.DMA((n, 3))],
    )(*ps)


def rs_final_sum(p, got, *, name):
    _, R, C = p.shape
    tr = _tile(R, 768)

    def body(chip_ref, p_ref, got_ref, o_ref):
        s = p_ref[...].astype(F32)
        for j in range(3):
            s = s + got_ref[j].astype(F32)
        o_ref[...] = s

    mychip = jnp.reshape(2 * lax.axis_index("x") + lax.axis_index("y"), (1,)).astype(jnp.int32)
    return pl.pallas_call(
        body, name=name,
        grid_spec=pltpu.PrefetchScalarGridSpec(
            num_scalar_prefetch=1, grid=(R // tr,),
            in_specs=[pl.BlockSpec((None, tr, C), lambda i, chip_ref: (chip_ref[0], i, 0)),
                      pl.BlockSpec((3, tr, C), lambda i, chip_ref: (0, i, 0))],
            out_specs=pl.BlockSpec((tr, C), lambda i, chip_ref: (i, 0))),
        out_shape=jax.ShapeDtypeStruct((R, C), F32), compiler_params=_params(("arbitrary",)),
    )(mychip, p, got)


def reduce_scatter_big(gs):
    got = rs_exchange_sibling(gs, name="rs_sibling")
    parts = [rs_chip_partials(g, t, name=f"rs_chip_partials_{i}") for i, (g, t) in enumerate(zip(gs, got))]
    got2 = rs_exchange_chips(parts, name="rs_chips")
    return [rs_final_sum(p, t, name=f"rs_final_sum_{i}") for i, (p, t) in enumerate(zip(parts, got2))]


def _blk(off):
    return pl.BlockSpec((BLK, LANES), lambda h, n: (n, off + h))


def _const_spec(shape):
    return pl.BlockSpec(shape, lambda *_: (0,) * len(shape))


def retention_tables(T):
    pos = jnp.arange(T, dtype=F32) - float(PAD)
    inv_freq = 1.0 / (10000.0 ** jnp.linspace(0.0, 1.0, 64, dtype=F32))
    ang = pos[:, None] * inv_freq[None, :]
    cos = jnp.repeat(jnp.cos(ang), 2, axis=1)
    sin = jnp.repeat(jnp.sin(ang), 2, axis=1) * jnp.tile(jnp.array([-1.0, 1.0], F32), 64)[None, :]
    lane = np.arange(LANES)
    perm = jnp.broadcast_to(jnp.asarray((lane[:, None] == (lane[None, :] ^ 1)).astype(np.float32)), (4, LANES, LANES))
    log_gamma = jnp.log1p(-jnp.exp2(-5.0 - jnp.arange(4, dtype=F32)))
    idx = jnp.arange(BLK, dtype=F32)
    diff = idx[:, None] - idx[None, :]
    intra = jnp.where(diff >= 0, jnp.exp(jnp.maximum(diff, 0.0) * log_gamma[:, None, None]), 0.0)
    zeta = jnp.exp((BLK - 1.0 - idx)[None, :] * log_gamma[:, None])
    xi = jnp.exp((idx + 1.0)[None, :] * log_gamma[:, None])
    bc = lambda t: jnp.broadcast_to(t[:, :, None], (4, BLK, LANES))
    return cos, sin, perm, intra, bc(zeta), bc(xi)


def _heads(x):
    return jnp.stack([x[:, h * LANES:(h + 1) * LANES] for h in range(4)])


def _unheads(y):
    return jnp.concatenate([y[h] for h in range(4)], axis=1)


def _ret_chunk(rq, rk, rv, rg, S, cos, sin, intra, zeta, xi, perm):
    q = rq * cos + hdot(rq, perm) * sin
    k = (rk * cos + hdot(rk, perm) * sin) * (128.0 ** -0.5)
    ret = bdot(bdot(q, _t(k)) * intra, rv) + bdot(q * xi, S)
    S_new = S * xi[..., BLK - 1:BLK, :] + bdot(_t(k * zeta), rv)
    c = ret - jnp.mean(ret, axis=-1, keepdims=True)
    out = c * lax.rsqrt(jnp.mean(c * c, axis=-1, keepdims=True) + EPS) * _silu(rg)
    return out, S_new


def _wide(off):
    return pl.BlockSpec((BLK, 4 * LANES), lambda n: (n, off))


def retention_fwd(p, tables, *, name):
    T = p.shape[0]
    N = T // BLK
    cos, sin, perm, intra, zeta, xi = tables

    def body(rq, rk, rv, rg, cos_ref, sin_ref, in_ref, ze_ref, xi_ref, perm_ref, out_ref, sall_ref, s_scr):
        @pl.when(pl.program_id(0) == 0)
        def _():
            s_scr[...] = jnp.zeros_like(s_scr)

        S = s_scr[...]
        sall_ref[...] = S
        out, S_new = _ret_chunk(_heads(rq[...]), _heads(rk[...]), _heads(rv[...]), _heads(rg[...]), S, cos_ref[...],
                                sin_ref[...], in_ref[...], ze_ref[...], xi_ref[...], perm_ref[...])
        out_ref[...] = _unheads(out).astype(BF16)
        s_scr[...] = S_new

    rowtab = pl.BlockSpec((BLK, LANES), lambda n: (n, 0))
    tab = _const_spec((4, BLK, LANES))
    return pl.pallas_call(
        body, name=name, grid=(N,),
        in_specs=[_wide(0), _wide(1), _wide(2), _wide(3), rowtab, rowtab, tab, tab, tab, tab],
        out_specs=[_wide(0), pl.BlockSpec((None, 4, LANES, LANES), lambda n: (n, 0, 0, 0))],
        out_shape=[jax.ShapeDtypeStruct((T, 512), BF16), jax.ShapeDtypeStruct((N, 4, LANES, LANES), F32)],
        scratch_shapes=[pltpu.VMEM((4, LANES, LANES), F32)],
        compiler_params=_params(("arbitrary",)),
    )(p, p, p, p, cos, sin, intra, zeta, xi, perm)


def _row_mask(n):
    return (n * BLK + _iota2((BLK, 1), 0) >= PAD).astype(F32)


def retention_bwd(p, sall, dmixed, tables, *, name):
    T = p.shape[0]
    N = T // BLK
    cos, sin, perm, intra, zeta, xi = tables

    def body(rq, rk, rv, rg, cos_ref, sin_ref, in_ref, ze_ref, xi_ref, perm_ref, sall_ref, do_ref, drq, drk, drv, drg,
             ds_scr):
        n = N - 1 - pl.program_id(0)

        @pl.when(pl.program_id(0) == 0)
        def _():
            ds_scr[...] = jnp.zeros_like(ds_scr)

        f = lambda a, b, c, d, s: _ret_chunk(a, b, c, d, s, cos_ref[...], sin_ref[...], in_ref[...], ze_ref[...],
                                             xi_ref[...], perm_ref[...])
        _, vjp = jax.vjp(f, _heads(rq[...]), _heads(rk[...]), _heads(rv[...]), _heads(rg[...]), sall_ref[...])
        g = vjp((_heads(do_ref[...]), ds_scr[...]))
        mask = _row_mask(n)
        for ref, val in zip((drq, drk, drv, drg), g[:4]):
            ref[...] = _unheads(val) * mask
        ds_scr[...] = g[4]

    def rwide(off):
        return pl.BlockSpec((BLK, 4 * LANES), lambda n: (N - 1 - n, off))

    rowtab = pl.BlockSpec((BLK, LANES), lambda n: (N - 1 - n, 0))
    tab = _const_spec((4, BLK, LANES))
    return pl.pallas_call(
        body, name=name, grid=(N,),
        in_specs=[rwide(0), rwide(1), rwide(2), rwide(3), rowtab, rowtab, tab, tab, tab, tab,
                  pl.BlockSpec((None, 4, LANES, LANES), lambda n: (N - 1 - n, 0, 0, 0)), rwide(0)],
        out_specs=[rwide(0)] * 4, out_shape=[jax.ShapeDtypeStruct((T, 512), F32)] * 4,
        scratch_shapes=[pltpu.VMEM((4, LANES, LANES), F32)],
        compiler_params=_params(("arbitrary",)),
    )(p, p, p, p, cos, sin, intra, zeta, xi, perm, sall, dmixed)


def conv_silu_fwd(p, w, *, name):
    T = p.shape[0]
    N = T // BLK

    def body(x_ref, xp_ref, w_ref, o_ref):
        n = pl.program_id(0)
        cur = x_ref[...]
        cat = jnp.concatenate([jnp.where(n > 0, xp_ref[...], 0.0), cur], axis=0)
        y = w_ref[3:4, :] * cur
        for s in (1, 2, 3):
            y = y + w_ref[3 - s:4 - s, :] * pltpu.roll(cat, s, 0)[BLK:]
        o_ref[...] = _silu(y)

    cw = 4 * LANES
    return pl.pallas_call(
        body, name=name, grid=(N, 3),
        in_specs=[pl.BlockSpec((BLK, cw), lambda n, c: (n, 4 + c)),
                  pl.BlockSpec((BLK, cw), lambda n, c: (jnp.maximum(n - 1, 0), 4 + c)),
                  pl.BlockSpec((4, cw), lambda n, c: (0, c))],
        out_specs=pl.BlockSpec((BLK, cw), lambda n, c: (n, c)),
        out_shape=jax.ShapeDtypeStruct((T, 1536), F32), compiler_params=_params(("arbitrary", "arbitrary")),
    )(p, p, w)


def conv_silu_bwd(p, w, dact, part, *, name):
    T = p.shape[0]
    N = T // BLK
    cw = 4 * LANES

    def body(xp_ref, x_ref, xn_ref, w_ref, da_ref, dan_ref, dx_ref, dw_ref):
        n = pl.program_id(0)
        last = n == N - 1
        cat = jnp.concatenate([jnp.where(n > 0, xp_ref[...], 0.0), x_ref[...], jnp.where(last, 0.0, xn_ref[...])], axis=0)
        shifted = [cat] + [pltpu.roll(cat, s, 0) for s in (1, 2, 3)]
        y = w_ref[3:4, :] * shifted[0]
        for s in (1, 2, 3):
            y = y + w_ref[3 - s:4 - s, :] * shifted[s]
        y = y[BLK:]
        da = jnp.concatenate([da_ref[...], jnp.where(last, 0.0, dan_ref[...])], axis=0)
        sg = _sigmoid(y)
        dy = da * sg * (1.0 + y * (1.0 - sg))
        dx = w_ref[3:4, :] * dy[:BLK]
        for s in (1, 2, 3):
            dx = dx + w_ref[3 - s:4 - s, :] * pltpu.roll(dy, 2 * BLK - s, 0)[:BLK]
        dx_ref[...] = dx * _row_mask(n)

        @pl.when(n == 0)
        def _():
            dw_ref[...] = jnp.zeros_like(dw_ref)

        for s in (0, 1, 2, 3):
            dw_ref[3 - s:4 - s, :] += jnp.sum(dy[:BLK] * shifted[s][BLK:2 * BLK], axis=0, keepdims=True)

    def xs(d):
        return pl.BlockSpec((BLK, cw), lambda n: (jnp.clip(n + d, 0, N - 1), 4 + part))

    return pl.pallas_call(
        body, name=name, grid=(N,),
        in_specs=[xs(-1), xs(0), xs(1), pl.BlockSpec((4, cw), lambda n: (0, part)),
                  pl.BlockSpec((BLK, cw), lambda n: (n, 0)),
                  pl.BlockSpec((BLK, cw), lambda n: (jnp.minimum(n + 1, N - 1), 0))],
        out_specs=[pl.BlockSpec((BLK, cw), lambda n: (n, 0)), pl.BlockSpec((4, cw), lambda n: (0, 0))],
        out_shape=[jax.ShapeDtypeStruct((T, 512), F32), jax.ShapeDtypeStruct((4, 512), F32)],
        compiler_params=_params(("arbitrary",)),
    )(p, p, p, w, dact, dact)


def _softplus(x):
    return jnp.maximum(x, 0.0) + jnp.log1p(jnp.exp(-jnp.abs(x)))


def _pick4(tile, off):
    return jnp.stack([_lane_pick(tile, off + h) for h in range(4)])


def _spread4(v4, off, rows):
    lane = _iota2((rows, LANES), 1)
    out = jnp.where(lane == off, v4[0], 0.0)
    for h in range(1, 4):
        out = out + jnp.where(lane == off + h, v4[h], 0.0)
    return out


def _gdn_chunk(qa, ka, va, z, braw, araw, S, alog, dtb, onorm, rowmask, lincl):
    r, c = _iota2((BLK, BLK), 0), _iota2((BLK, BLK), 1)
    incl, strict = r >= c, r > c
    eye = (r == c).astype(F32)
    q = qa * lax.rsqrt(jnp.sum(qa * qa, axis=-1, keepdims=True) + EPS) * (128.0 ** -0.5)
    k = ka * lax.rsqrt(jnp.sum(ka * ka, axis=-1, keepdims=True) + EPS)
    beta = _sigmoid(braw) * rowmask
    g = -jnp.exp(alog) * _softplus(araw + dtb) * rowmask
    gc = hdot(lincl, jnp.broadcast_to(g, qa.shape))
    decay = jnp.where(incl, jnp.exp(jnp.where(incl, gc - _t(gc), 0.0)), 0.0)
    kb = k * beta
    amat = jnp.where(strict, bdot(kb, _t(k)) * decay, 0.0)
    m = -amat
    inv = eye + m
    pw = hdot(m, m)
    for t in range(6):
        inv = inv + hdot(inv, pw)
        if t < 5:
            pw = hdot(pw, pw)
    egc = jnp.exp(gc)
    u = hdot(inv, va * beta)
    w = hdot(inv, kb * egc)
    qk = jnp.where(incl, bdot(q, _t(k)) * decay, 0.0)
    glast = gc[..., BLK - 1:BLK, :]
    vnew = u - bdot(w, S)
    o = bdot(q * egc, S) + bdot(qk, vnew)
    S_new = S * jnp.exp(glast) + bdot(_t(k * jnp.exp(glast - gc)), vnew)
    out = o * lax.rsqrt(jnp.mean(o * o, axis=-1, keepdims=True) + EPS) * onorm * _silu(z)
    return out, S_new


def _lincl():
    i = np.arange(BLK)
    return jnp.broadcast_to(jnp.asarray((i[:, None] >= i[None, :]).astype(np.float32)), (4, BLK, BLK))


def gdn_fwd(act, p, alog, dtb, onorm, *, name):
    T = p.shape[0]
    N = T // BLK

    def body(qa, ka, va, z, ba, alog_ref, dtb_ref, on_ref, l_ref, out_ref, sall_ref, s_scr):
        n = pl.program_id(0)

        @pl.when(n == 0)
        def _():
            s_scr[...] = jnp.zeros_like(s_scr)

        S = s_scr[...]
        sall_ref[...] = S
        out, S_new = _gdn_chunk(_heads(qa[...]), _heads(ka[...]), _heads(va[...]), _heads(z[...]), _pick4(ba[...], 0),
                                _pick4(ba[...], 4), S, _pick4(alog_ref[...], 0), _pick4(dtb_ref[...], 0), on_ref[...],
                                _row_mask(n), l_ref[...])
        out_ref[...] = _unheads(out).astype(BF16)
        s_scr[...] = S_new

    vec = _const_spec((1, LANES))
    return pl.pallas_call(
        body, name=name, grid=(N,),
        in_specs=[_wide(0), _wide(1), _wide(2), _wide(7), pl.BlockSpec((BLK, LANES), lambda n: (n, 32)), vec, vec, vec,
                  _const_spec((4, BLK, BLK))],
        out_specs=[_wide(0), pl.BlockSpec((None, 4, LANES, LANES), lambda n: (n, 0, 0, 0))],
        out_shape=[jax.ShapeDtypeStruct((T, 512), BF16), jax.ShapeDtypeStruct((N, 4, LANES, LANES), F32)],
        scratch_shapes=[pltpu.VMEM((4, LANES, LANES), F32)],
        compiler_params=_params(("arbitrary",)),
    )(act, act, act, p, p, alog, dtb, onorm, _lincl())


def gdn_bwd(act, p, alog, dtb, onorm, sall, dmixed, *, name):
    T = p.shape[0]
    N = T // BLK

    def body(qa, ka, va, z, ba, alog_ref, dtb_ref, on_ref, l_ref, sall_ref, do_ref,
             dq_ref, dk_ref, dv_ref, dz_ref, dba_ref, dal_ref, ddt_ref, don_ref, ds_scr):
        step = pl.program_id(0)
        n = N - 1 - step

        @pl.when(step == 0)
        def _():
            ds_scr[...] = jnp.zeros_like(ds_scr)
            dal_ref[...] = jnp.zeros_like(dal_ref)
            ddt_ref[...] = jnp.zeros_like(ddt_ref)
            don_ref[...] = jnp.zeros_like(don_ref)

        rowmask, lincl = _row_mask(n), l_ref[...]
        f = lambda *a: _gdn_chunk(*a, rowmask, lincl)
        _, vjp = jax.vjp(f, _heads(qa[...]), _heads(ka[...]), _heads(va[...]), _heads(z[...]), _pick4(ba[...], 0),
                         _pick4(ba[...], 4), sall_ref[...], _pick4(alog_ref[...], 0), _pick4(dtb_ref[...], 0),
                         on_ref[...])
        g = vjp((_heads(do_ref[...]), ds_scr[...]))
        dq_ref[...] = _unheads(g[0]) * rowmask
        dk_ref[...] = _unheads(g[1]) * rowmask
        dv_ref[...] = _unheads(g[2]) * rowmask
        dz_ref[...] = _unheads(g[3]) * rowmask
        dba_ref[...] = (_spread4(g[4], 0, BLK) + _spread4(g[5], 4, BLK)) * rowmask
        ds_scr[...] = g[6]
        dal_ref[...] += _spread4(g[7], 0, 1)
        ddt_ref[...] += _spread4(g[8], 0, 1)
        don_ref[...] += g[9]

    def rwide(off):
        return pl.BlockSpec((BLK, 4 * LANES), lambda s: (N - 1 - s, off))

    vec = _const_spec((1, LANES))
    col = pl.BlockSpec((BLK, LANES), lambda s: (N - 1 - s, 0))
    return pl.pallas_call(
        body, name=name, grid=(N,),
        in_specs=[rwide(0), rwide(1), rwide(2), rwide(7), pl.BlockSpec((BLK, LANES), lambda s: (N - 1 - s, 32)), vec, vec,
                  vec, _const_spec((4, BLK, BLK)),
                  pl.BlockSpec((None, 4, LANES, LANES), lambda s: (N - 1 - s, 0, 0, 0)), rwide(1)],
        out_specs=[rwide(0)] * 4 + [col, vec, vec, vec],
        out_shape=[jax.ShapeDtypeStruct((T, 512), F32)] * 4 + [jax.ShapeDtypeStruct((T, LANES), F32)]
        + [jax.ShapeDtypeStruct((1, LANES), F32)] * 3,
        scratch_shapes=[pltpu.VMEM((4, LANES, LANES), F32)],
        compiler_params=_params(("arbitrary",)),
    )(act, act, act, p, p, alog, dtb, onorm, _lincl(), sall, dmixed)


NEG = -1e30


def _swa_block(q, k0, kp, kc, v0, vp, vc, sink, n):
    r, c = _iota2((BLK, BLK), 0), _iota2((BLK, BLK), 1)
    m0 = (c >= PAD) & (c <= n * BLK + r)
    mp = (n >= 2) & (c > r)
    mc = (n >= 1) & (r >= c)
    b = lambda t: jnp.broadcast_to(t, (4,) + t.shape)
    qs = q * (64.0 ** -0.5)
    s0 = jnp.where(m0, bdot(qs, _t(b(k0))), NEG)
    sp = jnp.where(mp, bdot(qs, _t(b(kp))), NEG)
    sc = jnp.where(mc, bdot(qs, _t(b(kc))), NEG)
    mx = jnp.maximum(jnp.max(jnp.maximum(jnp.maximum(s0, sp), sc), axis=-1, keepdims=True), sink)
    mx = lax.stop_gradient(mx)
    p0, pp, pc = jnp.exp(s0 - mx), jnp.exp(sp - mx), jnp.exp(sc - mx)
    den = (jnp.sum(p0, axis=-1, keepdims=True) + jnp.sum(pp, axis=-1, keepdims=True)
           + jnp.sum(pc, axis=-1, keepdims=True) + jnp.exp(sink - mx))
    return (bdot(p0, b(v0)) + bdot(pp, b(vp)) + bdot(pc, b(vc))) / den


def _swa_specs():
    rows = (lambda n: 0, lambda n: jnp.maximum(n - 1, 0), lambda n: n)

    def kv_spec(off, row):
        return pl.BlockSpec((BLK, LANES), lambda g, n: (row(n), off + g))

    q = pl.BlockSpec((BLK, 4 * LANES), lambda g, n: (n, g))
    return q, [kv_spec(off, row) for off in (8, 10) for row in rows]


def swa_fwd(p2, sinkrow, *, name):
    T = p2.shape[0]
    N = T // BLK

    def body(q, k0, kp, kc, v0, vp, vc, sink_ref, o_ref):
        g, n = pl.program_id(0), pl.program_id(1)
        o = _swa_block(_heads(q[...]), k0[...], kp[...], kc[...], v0[...], vp[...], vc[...],
                       _pick4(sink_ref[...], 4 * g), n)
        o_ref[...] = _unheads(o).astype(BF16)

    q, kv = _swa_specs()
    return pl.pallas_call(
        body, name=name, grid=(2, N), in_specs=[q] + kv + [_const_spec((1, LANES))],
        out_specs=q, out_shape=jax.ShapeDtypeStruct((T, 1024), BF16),
        compiler_params=_params(("arbitrary", "arbitrary")),
    )(p2, p2, p2, p2, p2, p2, p2, sinkrow)


def swa_bwd(p2, sinkrow, dmixed, *, name):
    T = p2.shape[0]
    N = T // BLK

    def body(q, k0, kp, kc, v0, vp, vc, sink_ref, do_ref, dq_ref, dk_ref, dv_ref, dsink_ref):
        g, n = pl.program_id(0), pl.program_id(1)

        @pl.when(n == 0)
        def _():
            dk_ref[...] = jnp.zeros_like(dk_ref)
            dv_ref[...] = jnp.zeros_like(dv_ref)

        @pl.when((g == 0) & (n == 0))
        def _():
            dsink_ref[...] = jnp.zeros_like(dsink_ref)

        f = lambda *a: _swa_block(*a, n)
        _, vjp = jax.vjp(f, _heads(q[...]), k0[...], kp[...], kc[...], v0[...], vp[...], vc[...],
                         _pick4(sink_ref[...], 4 * g))
        dq, dk0, dkp, dkc, dv0, dvp, dvc, dsink = vjp(_heads(do_ref[...]))
        dq_ref[...] = _unheads(dq)
        prev = pl.ds(pl.multiple_of(jnp.maximum(n - 1, 0) * BLK, BLK), BLK)
        cur = pl.ds(pl.multiple_of(n * BLK, BLK), BLK)
        for ref, d0, dp, dc in ((dk_ref, dk0, dkp, dkc), (dv_ref, dv0, dvp, dvc)):
            ref[0:BLK, :] += d0
            ref[prev, :] += dp
            ref[cur, :] += dc
        dsink_ref[...] += _spread4(dsink, 4 * g, 1)

    qspec, kv = _swa_specs()
    slab = pl.BlockSpec((T, LANES), lambda g, n: (0, g))
    return pl.pallas_call(
        body, name=name, grid=(2, N), in_specs=[qspec] + kv + [_const_spec((1, LANES)), qspec],
        out_specs=[qspec, slab, slab, _const_spec((1, LANES))],
        out_shape=[jax.ShapeDtypeStruct((T, 1024), F32), jax.ShapeDtypeStruct((T, 256), F32),
                   jax.ShapeDtypeStruct((T, 256), F32), jax.ShapeDtypeStruct((1, LANES), F32)],
        compiler_params=_params(("arbitrary", "arbitrary")),
    )(p2, p2, p2, p2, p2, p2, p2, sinkrow, dmixed)


def _split_dot(x, m):
    rows = x.shape[0]
    hi = x.astype(BF16)
    lo = (x - hi.astype(F32)).astype(BF16)
    r = _nn(jnp.concatenate([hi, lo], axis=0), m)
    return r[:rows] + r[rows:]


def _tri(strict):
    i = np.arange(BLK)
    m = (i[:, None] > i[None, :]) if strict else (i[:, None] >= i[None, :])
    return jnp.asarray(m.astype(np.float32), dtype=BF16)


def _chunk_carries(rowsums, carry, kb):
    out = [None] * kb
    for c in reversed(range(kb)):
        out[c] = carry
        carry = carry + rowsums[c * BLK:(c + 1) * BLK]
    return jnp.concatenate(out, axis=0), carry


def _sb_weights(qb, ks, base, n, carry, after):
    kb = len(ks)
    z = jnp.concatenate([_nt(qb, kc) for kc in ks], axis=0)
    row, lane = _iota2(z.shape, 0), _iota2(z.shape, 1)
    kpos = (base * BLK + lane) + (row & ~(BLK - 1))
    valid = (kpos < n * BLK + (row & (BLK - 1))) & (kpos >= PAD)
    lb = jnp.minimum(z, 0.0) - jnp.log1p(jnp.exp(-jnp.abs(z)))
    lm = jnp.where(valid, lb - z, 0.0)
    behind, carry = _chunk_carries(jnp.sum(lm, axis=1, keepdims=True), carry, kb)
    a = jnp.where(valid, jnp.exp(lb + behind + _split_dot(lm, after)), 0.0)
    return valid, lb, a, carry


def _key_blocks(n_blocks):
    return next(k for k in (5, 3, 1) if n_blocks % k == 0)


def sb_fwd(p2, *, name):
    T = p2.shape[0]
    N = T // BLK
    kb = _key_blocks(N)

    def body(q_ref, k_ref, v_ref, after_ref, o_ref, of_ref):
        n = pl.program_id(1)
        qb = (q_ref[...] * (64.0 ** -0.5)).astype(BF16)
        after = after_ref[...]

        nsup = n // kb + 1

        def step(t, c):
            acc, carry = c
            base = (nsup - 1 - t) * kb
            rows = [pl.ds(pl.multiple_of((base + sub) * BLK, BLK), BLK) for sub in range(kb)]
            _, _, a, carry = _sb_weights(qb, [k_ref[r, :].astype(BF16) for r in rows], base, n, carry, after)
            for sub, r in enumerate(rows):
                acc = acc + _split_dot(a[sub * BLK:(sub + 1) * BLK], v_ref[r, :].astype(BF16))
            return acc, carry

        acc, _ = lax.fori_loop(0, nsup, step, (jnp.zeros((BLK, LANES), F32), jnp.zeros((BLK, 1), F32)))
        o_ref[...] = acc.astype(BF16)
        of_ref[...] = acc

    def slab(off):
        return pl.BlockSpec((T, LANES), lambda h, n: (0, off + h))

    return pl.pallas_call(
        body, name=name, grid=(8, N),
        in_specs=[_blk(12), slab(20), slab(28), _const_spec((BLK, BLK))],
        out_specs=[_blk(0), _blk(0)],
        out_shape=[jax.ShapeDtypeStruct((T, 1024), BF16), jax.ShapeDtypeStruct((T, 1024), F32)],
        compiler_params=_params(("arbitrary", "arbitrary")),
    )(p2, p2, p2, _tri(True))


def sb_bwd(p2, o, dmixed, *, name):
    T = p2.shape[0]
    N = T // BLK
    kb = _key_blocks(N)

    def body(q_ref, k_ref, v_ref, after_ref, from_ref, o_ref, do_ref, dq_ref, dk_ref, dv_ref, dkt_scr, dvt_scr):
        n = pl.program_id(1)

        @pl.when(n == 0)
        def _():
            dkt_scr[...] = jnp.zeros_like(dkt_scr)
            dvt_scr[...] = jnp.zeros_like(dvt_scr)

        q = q_ref[...]
        qb = (q * (64.0 ** -0.5)).astype(BF16)
        qt = q.T.astype(BF16)
        do = do_ref[...]
        dob = do.astype(BF16)
        dot_ = do.T.astype(BF16)
        total = jnp.sum(dob.astype(F32) * o_ref[...], axis=1, keepdims=True)
        total = jnp.concatenate([total] * kb, axis=0)
        after, frm = after_ref[...], from_ref[...]

        nsup = n // kb + 1

        def step(t, c):
            dq, carry, gcarry = c
            base = (nsup - 1 - t) * kb
            rows = [pl.ds(pl.multiple_of((base + sub) * BLK, BLK), BLK) for sub in range(kb)]
            ks = [k_ref[r, :].astype(BF16) for r in rows]
            valid, lb, a, carry = _sb_weights(qb, ks, base, n, carry, after)
            g = jnp.concatenate([_nt(dob, v_ref[r, :].astype(BF16)) for r in rows], axis=0) * a
            behind, gcarry = _chunk_carries(jnp.sum(g, axis=1, keepdims=True), gcarry, kb)
            before = total - (behind + _split_dot(g, frm))
            beta = jnp.exp(lb)
            dz = (jnp.where(valid, g * (1.0 - beta) - beta * before, 0.0) * (64.0 ** -0.5)).astype(BF16)
            ab = a.astype(BF16)
            for sub in range(kb):
                part = slice(sub * BLK, (sub + 1) * BLK)
                dq = dq + _nn(dz[part], ks[sub])
                dkt_scr[base + sub] += _nn(qt, dz[part])
                dvt_scr[base + sub] += _nn(dot_, ab[part])
            return dq, carry, gcarry

        zero = jnp.zeros((BLK, 1), F32)
        dq, _, _ = lax.fori_loop(0, nsup, step, (jnp.zeros((BLK, LANES), F32), zero, zero))
        dq_ref[...] = dq

        @pl.when(n == N - 1)
        def _():
            def flush(j, _):
                rows = pl.ds(pl.multiple_of(j * BLK, BLK), BLK)
                dk_ref[rows, :] = dkt_scr[j].T
                dv_ref[rows, :] = dvt_scr[j].T
                return 0

            lax.fori_loop(0, N, flush, 0)

    def slab(off):
        return pl.BlockSpec((T, LANES), lambda h, n: (0, off + h))

    return pl.pallas_call(
        body, name=name, grid=(8, N),
        in_specs=[_blk(12), slab(20), slab(28), _const_spec((BLK, BLK)), _const_spec((BLK, BLK)), _blk(0), _blk(8)],
        out_specs=[_blk(0), slab(0), slab(0)],
        out_shape=[jax.ShapeDtypeStruct((T, 1024), F32)] * 3,
        scratch_shapes=[pltpu.VMEM((N, LANES, LANES), F32), pltpu.VMEM((N, LANES, LANES), F32)],
        compiler_params=_params(("arbitrary", "arbitrary")),
    )(p2, p2, p2, _tri(True), _tri(False), o, dmixed)


def ffn_fwd(h, g_pre, g_post, wg, wu, wd, tag):
    u, gate, up, act = norm_mm(h, g_pre, (wg, wu), swiglu=True, wt=True, name=f"ffn_up_{tag}")
    y, h_new = mm_norm_res([act], [wd], h, g_post, 0.5, name=f"ffn_down_{tag}")
    return h_new, (h, u, gate, up, y)


def ffn_bwd(saved, dh, g_pre, g_post, wg, wu, wd, tag):
    h, u, gate, up, y = saved
    dy, dg_post, dgate, dup, act = normbwd_mm_nt(dh, y, g_post, wd, 0.5, (gate, up), name=f"ffn_bwd_down_{tag}")
    dwd = mm_tn(act, dy, name=f"ffn_dwd_{tag}")
    dwg = mm_tn(dgate, u, name=f"ffn_dwg_{tag}")
    dwu = mm_tn(dup, u, name=f"ffn_dwu_{tag}")
    dh_in, dg_pre = mm_nt_normbwd([dgate, dup], [wg, wu], h, g_pre, dh, wt=True, name=f"ffn_bwd_up_{tag}")
    return dh_in, (dg_pre, dg_post), (dwg, dwu, dwd)


def _lane_row(v):
    v = v.reshape(1, -1)
    return jnp.pad(v, ((0, 0), (0, LANES - v.shape[1])))


AB_WIDTHS = (512,) * 8 + (LANES,)


def mixer_ab_fwd(h, g_pre, g_post, w_in, conv_w, a_log, dt_bias, out_norm, w_out, tables):
    u, p = norm_mm(h, g_pre, (w_in,), swiglu=False, name="ab_in")
    ret, sall_r = retention_fwd(p, tables, name="retention_fwd")
    act = conv_silu_fwd(p, conv_w, name="conv_fwd")
    gdn, sall_g = gdn_fwd(act, p, _lane_row(a_log), _lane_row(dt_bias), out_norm.reshape(1, LANES), name="gdn_fwd")
    y, h_new = mm_norm_res([ret, gdn], [w_out[:512], w_out[512:]], h, g_post, 1.0, name="ab_out")
    return h_new, (h, u, p, ret, sall_r, act, gdn, sall_g, y)


def mixer_ab_bwd(saved, dh, g_pre, g_post, w_in, conv_w, a_log, dt_bias, out_norm, w_out, tables):
    h, u, p, ret, sall_r, act, gdn, sall_g, y = saved
    dy, dg_post, dmixed = normbwd_mm_nt(dh, y, g_post, w_out, 1.0, name="ab_bwd_out")
    dw_out = jnp.concatenate([mm_tn(ret, dy, name="ab_dwout_ret"), mm_tn(gdn, dy, name="ab_dwout_gdn")], axis=0)
    pieces = list(retention_bwd(p, sall_r, dmixed, tables, name="retention_bwd"))
    dqa, dka, dva, dz, dba, dalog, ddtb, donorm = gdn_bwd(
        act, p, _lane_row(a_log), _lane_row(dt_bias), out_norm.reshape(1, LANES), sall_g, dmixed, name="gdn_bwd")
    dconv = []
    for part, dact in enumerate((dqa, dka, dva)):
        dx, dw = conv_silu_bwd(p, conv_w, dact, part, name=f"conv_bwd_{part}")
        pieces.append(dx)
        dconv.append(dw)
    pieces += [dz, dba]
    offs = np.cumsum((0,) + AB_WIDTHS)
    w_parts = [w_in[:, a:b] for a, b in zip(offs[:-1], offs[1:])]
    dh_in, dg_pre = mm_nt_normbwd(pieces, w_parts, h, g_pre, dh, name="ab_bwd_in")
    dw_in = jnp.concatenate([mm_tn(u, pc, name=f"ab_dwin_{i}") for i, pc in enumerate(pieces)], axis=1)
    small = (jnp.concatenate(dconv, axis=1), dalog[:, :4], ddtb[:, :4], donorm)
    return dh_in, (dg_pre, dg_post), (dw_in, dw_out), small


CD_WIDTHS = (1024, 256, 256, 1024, 1024, 1024)


def mixer_cd_fwd(h, g_pre, g_post, w_in, sinks, w_out):
    u, p2 = norm_mm(h, g_pre, (w_in,), swiglu=False, name="cd_in")
    swa = swa_fwd(p2, _lane_row(sinks), name="swa_fwd")
    sb, sb_f32 = sb_fwd(p2, name="sb_fwd")
    y, h_new = mm_norm_res([swa, sb], [w_out[:1024], w_out[1024:]], h, g_post, 1.0, name="cd_out")
    return h_new, (h, u, p2, swa, sb, sb_f32, y)


def mixer_cd_bwd(saved, dh, g_pre, g_post, w_in, sinks, w_out):
    h, u, p2, swa, sb, sb_f32, y = saved
    dy, dg_post, dmixed = normbwd_mm_nt(dh, y, g_post, w_out, 1.0, name="cd_bwd_out")
    dw_out = jnp.concatenate([mm_tn(swa, dy, name="cd_dwout_swa"), mm_tn(sb, dy, name="cd_dwout_sb")], axis=0)
    dq_c, dk_c, dv_c, dsink = swa_bwd(p2, _lane_row(sinks), dmixed, name="swa_bwd")
    pieces = [dq_c, dk_c, dv_c] + list(sb_bwd(p2, sb_f32, dmixed, name="sb_bwd"))
    offs = np.cumsum((0,) + CD_WIDTHS)
    w_parts = [w_in[:, a:b] for a, b in zip(offs[:-1], offs[1:])]
    dh_in, dg_pre = mm_nt_normbwd(pieces, w_parts, h, g_pre, dh, name="cd_bwd_in")
    dw_in = jnp.concatenate([mm_tn(u, pc, name=f"cd_dwin_{i}") for i, pc in enumerate(pieces)], axis=1)
    return dh_in, (dg_pre, dg_post), (dw_in, dw_out), dsink[:, :8]


def _pad_heads(w, axis):
    shape = w.shape
    w = w.reshape(shape[:axis] + (shape[axis] // 64, 64) + shape[axis + 1:])
    pad = [(0, 0)] * w.ndim
    pad[axis + 1] = (0, 64)
    return jnp.pad(w, pad).reshape(shape[:axis] + (2 * shape[axis],) + shape[axis + 1:])


def _unpad_heads(w, axis):
    shape = w.shape
    w = w.reshape(shape[:axis] + (shape[axis] // 128, 128) + shape[axis + 1:])
    w = lax.slice_in_dim(w, 0, 64, axis=axis + 1)
    return w.reshape(shape[:axis] + (shape[axis] // 2,) + shape[axis + 1:])


SMALL_SHARDED = (("meta_tokens", (NMETA, LANES), 1), ("norm_gains", (2, 6, LANES), 2), ("ab_conv_w", (1, 4, 192), 2))
SMALL_REPL = (("ab_a_log", (1, 4)), ("ab_dt_bias", (1, 4)), ("ab_out_norm", (1, LANES)), ("cd_sinks", (1, 8)))


def _stack_shards(g, axis):
    full = jnp.moveaxis(g, 0, axis)
    shape = full.shape
    return full.reshape(shape[:axis] + (shape[axis] * shape[axis + 1],) + shape[axis + 2:])


def _split_shards(full, axis):
    shape = full.shape
    g = full.reshape(shape[:axis] + (NDEV, shape[axis] // NDEV) + shape[axis + 1:])
    return jnp.moveaxis(g, axis, 0)


def _pad_rows8(a):
    rows = []
    for x in a:
        flat = x.reshape(x.shape[0], -1)
        n = -(-flat.shape[1] // LANES) * LANES
        rows.append(jnp.pad(flat, ((0, 0), (0, n - flat.shape[1]))).reshape(x.shape[0], n // LANES, LANES))
    cat = jnp.concatenate(rows, axis=1)
    return jnp.pad(cat, ((0, 0), (0, -cat.shape[1] % 8), (0, 0)))


def _unpad_rows8(packed, shapes):
    out, at = [], 0
    for shape in shapes:
        size = int(np.prod(shape))
        nrow = -(-size // LANES)
        blk = packed[:, at:at + nrow].reshape(packed.shape[0], -1)[:, :size]
        out.append(blk.reshape((packed.shape[0],) + tuple(shape)))
        at += nrow
    return out


def kernel(x, meta_tokens, norm_gains, ffn_w_gate, ffn_w_up, ffn_w_down, ab_w_in, ab_conv_w, ab_a_log, ab_dt_bias, ab_out_norm, ab_w_out, cd_w_in, cd_sinks, cd_w_out, loss_target, m_meta_tokens, m_norm_gains, m_ffn_w_gate, m_ffn_w_up, m_ffn_w_down, m_ab_w_in, m_ab_conv_w, m_ab_a_log, m_ab_dt_bias, m_ab_out_norm, m_ab_w_out, m_cd_w_in, m_cd_sinks, m_cd_w_out, v_meta_tokens, v_norm_gains, v_ffn_w_gate, v_ffn_w_up, v_ffn_w_down, v_ab_w_in, v_ab_conv_w, v_ab_a_log, v_ab_dt_bias, v_ab_out_norm, v_ab_w_out, v_cd_w_in, v_cd_sinks, v_cd_w_out):
    w = dict(meta_tokens=meta_tokens, norm_gains=norm_gains, ffn_w_gate=ffn_w_gate, ffn_w_up=ffn_w_up,
             ffn_w_down=ffn_w_down, ab_w_in=ab_w_in, ab_conv_w=ab_conv_w, ab_a_log=ab_a_log, ab_dt_bias=ab_dt_bias,
             ab_out_norm=ab_out_norm, ab_w_out=ab_w_out, cd_w_in=cd_w_in, cd_sinks=cd_sinks, cd_w_out=cd_w_out)
    m = dict(meta_tokens=m_meta_tokens, norm_gains=m_norm_gains, ffn_w_gate=m_ffn_w_gate, ffn_w_up=m_ffn_w_up,
             ffn_w_down=m_ffn_w_down, ab_w_in=m_ab_w_in, ab_conv_w=m_ab_conv_w, ab_a_log=m_ab_a_log,
             ab_dt_bias=m_ab_dt_bias, ab_out_norm=m_ab_out_norm, ab_w_out=m_ab_w_out, cd_w_in=m_cd_w_in,
             cd_sinks=m_cd_sinks, cd_w_out=m_cd_w_out)
    v = dict(meta_tokens=v_meta_tokens, norm_gains=v_norm_gains, ffn_w_gate=v_ffn_w_gate, ffn_w_up=v_ffn_w_up,
             ffn_w_down=v_ffn_w_down, ab_w_in=v_ab_w_in, ab_conv_w=v_ab_conv_w, ab_a_log=v_ab_a_log,
             ab_dt_bias=v_ab_dt_bias, ab_out_norm=v_ab_out_norm, ab_w_out=v_ab_w_out, cd_w_in=v_cd_w_in,
             cd_sinks=v_cd_sinks, cd_w_out=v_cd_w_out)
    order = list(w)
    S = x.shape[1]
    T = S + BLK

    fs = DFF // NDEV
    ffn_local = jnp.concatenate([jnp.swapaxes(ffn_w_gate, 2, 3).reshape(4 * fs, D),
                                 jnp.swapaxes(ffn_w_up, 2, 3).reshape(4 * fs, D), ffn_w_down.reshape(4 * fs, D)],
                                axis=0).astype(BF16)
    outs_local = jnp.concatenate([ab_w_out[0], cd_w_out[0]], axis=0).astype(BF16)
    ffn_all, abin_all, outs_all, cdin_all = all_gather_big(
        [ffn_local, ab_w_in[0].astype(BF16), outs_local, cd_w_in[0].astype(BF16)], name="gather_weights")
    ffn_mat = lambda k: ffn_all[:, k * fs:(k + 1) * fs].reshape(DFF, D)
    layers = [(i, j) for i in range(2) for j in range(2)]
    wg = {ij: ffn_mat(k) for k, ij in enumerate(layers)}
    wu = {ij: ffn_mat(4 + k) for k, ij in enumerate(layers)}
    wd = {ij: ffn_mat(8 + k) for k, ij in enumerate(layers)}
    ab_in = jnp.pad(_stack_shards(abin_all, 1), ((0, 0), (0, AB_INP - AB_IN)))
    ab_out = outs_all[:, :D // NDEV].reshape(D, D)
    cd_in = _pad_heads(_stack_shards(cdin_all, 1), 1)
    cd_out = _pad_heads(outs_all[:, D // NDEV:].reshape(D, D), 0)
    small_src = jnp.broadcast_to(_pad_rows8([w[n][None] for n, _, _ in SMALL_SHARDED]), (NDEV, 40, LANES))
    small_all = _unpad_rows8(all_to_all_small(small_src, name="gather_small"), [s for _, s, _ in SMALL_SHARDED])
    full = {n: _stack_shards(g, ax) for (n, _, ax), g in zip(SMALL_SHARDED, small_all)}
    conv_w = full["ab_conv_w"][0]
    gains = full["norm_gains"].reshape(2, 6, 1, D)
    tables = retention_tables(T)

    h = jnp.concatenate([jnp.zeros((PAD, D), F32), full["meta_tokens"], x[0]], axis=0)
    h, s00 = ffn_fwd(h, gains[0, 0], gains[0, 1], wg[0, 0], wu[0, 0], wd[0, 0], "00")
    h, sab = mixer_ab_fwd(h, gains[0, 2], gains[0, 3], ab_in, conv_w, ab_a_log, ab_dt_bias, ab_out_norm, ab_out, tables)
    h, s01 = ffn_fwd(h, gains[0, 4], gains[0, 5], wg[0, 1], wu[0, 1], wd[0, 1], "01")
    h, s10 = ffn_fwd(h, gains[1, 0], gains[1, 1], wg[1, 0], wu[1, 0], wd[1, 0], "10")
    h, scd = mixer_cd_fwd(h, gains[1, 2], gains[1, 3], cd_in, cd_sinks, cd_out)
    h, s11 = ffn_fwd(h, gains[1, 4], gains[1, 5], wg[1, 1], wu[1, 1], wd[1, 1], "11")
    loss_tile, dh = loss_and_grad(h, loss_target[0], name="loss")
    loss = lax.psum(loss_tile[0, 0], ("x", "y", "c"))

    dgain = [[None] * 6, [None] * 6]
    dffn = {}
    dh, (dgain[1][4], dgain[1][5]), dffn[1, 1] = ffn_bwd(s11, dh, gains[1, 4], gains[1, 5], wg[1, 1], wu[1, 1], wd[1, 1], "11")
    dh, (dgain[1][2], dgain[1][3]), (dcd_in, dcd_out), dsinks = mixer_cd_bwd(scd, dh, gains[1, 2], gains[1, 3], cd_in, cd_sinks, cd_out)
    dh, (dgain[1][0], dgain[1][1]), dffn[1, 0] = ffn_bwd(s10, dh, gains[1, 0], gains[1, 1], wg[1, 0], wu[1, 0], wd[1, 0], "10")
    dh, (dgain[0][4], dgain[0][5]), dffn[0, 1] = ffn_bwd(s01, dh, gains[0, 4], gains[0, 5], wg[0, 1], wu[0, 1], wd[0, 1], "01")
    dh, (dgain[0][2], dgain[0][3]), (dab_in, dab_out), (dconv, dalog, ddtb, donorm) = mixer_ab_bwd(
        sab, dh, gains[0, 2], gains[0, 3], ab_in, conv_w, ab_a_log, ab_dt_bias, ab_out_norm, ab_out, tables)
    dh, (dgain[0][0], dgain[0][1]), dffn[0, 0] = ffn_bwd(s00, dh, gains[0, 0], gains[0, 1], wg[0, 0], wu[0, 0], wd[0, 0], "00")
    grad_x = dh[BLK:][None]

    gfull = dict(meta_tokens=dh[PAD:BLK], norm_gains=jnp.stack([jnp.concatenate(r, axis=0) for r in dgain]),
                 ab_conv_w=dconv[None])
    ffn_send = jnp.concatenate([dffn[ij][k].astype(BF16).reshape(NDEV, fs, D) for k in range(3) for ij in layers], axis=1)
    outs_send = jnp.concatenate([dab_out.astype(BF16).reshape(NDEV, D // NDEV, D),
                                 _unpad_heads(dcd_out, 0).astype(BF16).reshape(NDEV, D // NDEV, D)], axis=1)
    abin_send = _split_shards(dab_in[:, :AB_IN].astype(BF16), 1)
    cdin_send = _split_shards(_unpad_heads(dcd_in, 1).astype(BF16), 1)
    ffn_g, abin_g, outs_g, cdin_g = reduce_scatter_big([ffn_send, abin_send, outs_send, cdin_send])
    ffn_g = ffn_g.reshape(3, 2, 2, fs, D)
    grads = dict(ffn_w_gate=jnp.swapaxes(ffn_g[0], 2, 3), ffn_w_up=jnp.swapaxes(ffn_g[1], 2, 3), ffn_w_down=ffn_g[2],
                 ab_w_in=abin_g[None], ab_w_out=outs_g[None, :D // NDEV], cd_w_in=cdin_g[None],
                 cd_w_out=outs_g[None, D // NDEV:])
    repl = [jnp.broadcast_to(t[None], (NDEV,) + t.shape) for t in (dalog, ddtb, donorm, dsinks)]
    ssend = _pad_rows8([_split_shards(gfull[n], ax) for n, _, ax in SMALL_SHARDED] + repl)
    ssum = sum_slots(all_to_all_small(ssend, name="exchange_small_grads"), name="sum_small_grads")[None]
    small = _unpad_rows8(ssum, [s for _, s, _ in SMALL_SHARDED] + [s for _, s in SMALL_REPL])
    grads.update({n: g[0] for n, g in zip([n for n, _, _ in SMALL_SHARDED] + [n for n, _ in SMALL_REPL], small)})

    delta, new_m, new_v = {}, {}, {}
    for n in order:
        shape = w[n].shape
        view = (-1, shape[-1])
        d_, m_, v_ = adamw(w[n].reshape(view), grads[n].reshape(view), m[n].reshape(view), v[n].reshape(view),
                           name=f"adamw_{n}")
        delta[n], new_m[n], new_v[n] = d_.reshape(shape), m_.reshape(shape), v_.reshape(shape)
    return (loss, grad_x, *[grads[n] for n in order], *[delta[n] for n in order], *[new_m[n] for n in order],
            *[new_v[n] for n in order])
```

```python
import functools
import math

import numpy as np
import jax
import jax.numpy as jnp
from jax import lax
from jax.experimental import pallas as pl
from jax.experimental.pallas import tpu as pltpu

F32, BF16 = jnp.float32, jnp.bfloat16
EPS = 1e-6
D = 1024
NMETA = 16
BLK = 128
PAD = BLK - NMETA
DFF = 2816
LANES = 128
NDEV = 8
AB_IN, AB_INP = 4104, 4224
ADAM_LR, ADAM_B1, ADAM_B2, ADAM_EPS, ADAM_WD, ADAM_STEP = 0.001, 0.9, 0.999, 1e-08, 0.01, 10
VMEM_LIMIT = 56 * 1024 * 1024
MESH = pl.DeviceIdType.MESH
HIGHEST = lax.Precision.HIGHEST


def _params(sem):
    return pltpu.CompilerParams(dimension_semantics=sem, vmem_limit_bytes=VMEM_LIMIT)


def _row_tile(T, streamed, resident):
    for tm in (640, 320, 128):
        if T % tm == 0 and 2 * (tm * streamed + resident) <= VMEM_LIMIT - 14 * 1024 * 1024:
            return tm
    return _tile(T, 128)


def _tile(n, cap):
    if n <= cap:
        return n
    best = None
    for t in range(LANES, cap + 1, LANES):
        if n % t == 0:
            best = t
    assert best is not None, (n, cap)
    return best


def _rms_fwd(x, g):
    return x * lax.rsqrt(jnp.mean(x * x, axis=-1, keepdims=True) + EPS) * g


def _rms_bwd(x, g, dz):
    r = lax.rsqrt(jnp.mean(x * x, axis=-1, keepdims=True) + EPS)
    xh = x * r
    dg = jnp.sum(dz * xh, axis=0, keepdims=True)
    t = dz * g
    return r * (t - xh * jnp.mean(t * xh, axis=-1, keepdims=True)), dg


def _sigmoid(x):
    return 1.0 / (1.0 + jnp.exp(-x))


def _silu(x):
    return x * _sigmoid(x)


def _nn(a, b, precision=None):
    return lax.dot_general(a, b, (((1,), (0,)), ((), ())), preferred_element_type=F32, precision=precision)


def _nt(a, b):
    return lax.dot_general(a, b, (((1,), (1,)), ((), ())), preferred_element_type=F32)


def _tn(a, b):
    return lax.dot_general(a, b, (((0,), (0,)), ((), ())), preferred_element_type=F32)


def _mm(a, b, precision=None):
    if a.ndim == 3:
        return lax.dot_general(a, b, (((2,), (1,)), ((0,), (0,))), preferred_element_type=F32, precision=precision)
    return _nn(a, b, precision)


def _t(x):
    return jnp.swapaxes(x, -1, -2)


@jax.custom_vjp
def bdot(a, b):
    return _mm(a.astype(BF16), b.astype(BF16))


def _bdot_fwd(a, b):
    return bdot(a, b), (a, b)


def _bdot_bwd(res, g):
    a, b = res
    return bdot(g, _t(b)), bdot(_t(a), g)


bdot.defvjp(_bdot_fwd, _bdot_bwd)


@jax.custom_vjp
def hdot(a, b):
    return _mm(a, b, HIGHEST)


def _hdot_fwd(a, b):
    return hdot(a, b), (a, b)


def _hdot_bwd(res, g):
    a, b = res
    return hdot(g, _t(b)), hdot(_t(a), g)


hdot.defvjp(_hdot_fwd, _hdot_bwd)


def _iota2(shape, axis):
    return lax.broadcasted_iota(jnp.int32, shape, axis)


def _lane_pick(row, lane):
    return jnp.sum(jnp.where(_iota2(row.shape, 1) == lane, row, 0.0), axis=1, keepdims=True)


def norm_mm(h, gain, ws, *, swiglu, name, wt=False, out_dtype=F32):
    T, Dm = h.shape
    N = ws[0].shape[0 if wt else 1]
    tm, tn = _tile(T, 640), _tile(N, 1408)
    nw = len(ws)
    mm = _nt if wt else _nn

    def body(h_ref, g_ref, *refs):
        w_refs, u_ref, o_refs = refs[:nw], refs[nw], refs[nw + 1:]

        @pl.when(pl.program_id(1) == 0)
        def _():
            u_ref[...] = _rms_fwd(h_ref[...], g_ref[...]).astype(BF16)

        u = u_ref[...]
        acc = [mm(u, w[...]) for w in w_refs]
        if swiglu:
            o_refs[0][...] = acc[0].astype(BF16)
            o_refs[1][...] = acc[1].astype(BF16)
            o_refs[2][...] = (_silu(acc[0]) * acc[1]).astype(BF16)
        else:
            o_refs[0][...] = acc[0].astype(out_dtype)

    row = pl.BlockSpec((tm, Dm), lambda i, j: (i, 0))
    tile = pl.BlockSpec((tm, tn), lambda i, j: (i, j))
    if swiglu:
        out_shape = [jax.ShapeDtypeStruct((T, Dm), BF16)] + [jax.ShapeDtypeStruct((T, N), BF16)] * 3
        out_specs = [row, tile, tile, tile]
    else:
        out_shape = [jax.ShapeDtypeStruct((T, Dm), BF16), jax.ShapeDtypeStruct((T, N), out_dtype)]
        out_specs = [row, tile]
    return pl.pallas_call(
        body, name=name, grid=(T // tm, N // tn),
        in_specs=[row, pl.BlockSpec((1, Dm), lambda i, j: (0, 0))]
        + [pl.BlockSpec((tn, Dm), lambda i, j: (j, 0)) if wt else pl.BlockSpec((Dm, tn), lambda i, j: (0, j))] * nw,
        out_specs=out_specs, out_shape=out_shape,
        compiler_params=_params(("arbitrary", "arbitrary")),
    )(h, gain, *ws)


def mm_norm_res(As, Ws, h, gain, scale, *, name):
    T, Dm = h.shape
    n = len(As)
    tm = _row_tile(T, sum(a.shape[1] * a.dtype.itemsize for a in As) + 3 * Dm * 4,
                   sum(w.size * w.dtype.itemsize for w in Ws))

    def body(*refs):
        a_refs, w_refs = refs[:n], refs[n:2 * n]
        h_ref, g_ref, y_ref, hn_ref = refs[2 * n:]
        y = _nn(a_refs[0][...].astype(BF16), w_refs[0][...])
        for a, w in zip(a_refs[1:], w_refs[1:]):
            y = y + _nn(a[...].astype(BF16), w[...])
        y_ref[...] = y
        hn_ref[...] = h_ref[...] + scale * _rms_fwd(y, g_ref[...])

    row = pl.BlockSpec((tm, Dm), lambda i: (i, 0))
    return pl.pallas_call(
        body, name=name, grid=(T // tm,),
        in_specs=[pl.BlockSpec((tm, a.shape[1]), lambda i: (i, 0)) for a in As]
        + [pl.BlockSpec(w.shape, lambda i: (0, 0)) for w in Ws]
        + [row, pl.BlockSpec((1, Dm), lambda i: (0, 0))],
        out_specs=[row, row], out_shape=[jax.ShapeDtypeStruct((T, Dm), F32)] * 2,
        compiler_params=_params(("arbitrary",)),
    )(*As, *Ws, h, gain)


def normbwd_mm_nt(dh, y, gain, w, scale, gu=None, *, name):
    T, Dm = dh.shape
    N = w.shape[0]
    tm, tn = _tile(T, 640), _tile(N, 1408)
    swiglu = gu is not None

    def body(dh_ref, y_ref, g_ref, w_ref, *refs):
        if swiglu:
            gate_ref, up_ref, dy_ref, dg_ref, dgate_ref, dup_ref, a_ref = refs
        else:
            dy_ref, dg_ref, da_ref = refs
        i, j = pl.program_id(0), pl.program_id(1)

        @pl.when(j == 0)
        def _():
            dy, dg = _rms_bwd(y_ref[...], g_ref[...], scale * dh_ref[...])
            dy_ref[...] = dy.astype(BF16)

            @pl.when(i == 0)
            def _():
                dg_ref[...] = jnp.zeros_like(dg_ref)

            dg_ref[...] += dg

        da = _nt(dy_ref[...], w_ref[...])
        if swiglu:
            gate, up = gate_ref[...].astype(F32), up_ref[...].astype(F32)
            s = _sigmoid(gate)
            dgate_ref[...] = (da * up * s * (1.0 + gate * (1.0 - s))).astype(BF16)
            dup_ref[...] = (da * gate * s).astype(BF16)
            a_ref[...] = (gate * s * up).astype(BF16)
        else:
            da_ref[...] = da

    row = pl.BlockSpec((tm, Dm), lambda i, j: (i, 0))
    vec = pl.BlockSpec((1, Dm), lambda i, j: (0, 0))
    tile = pl.BlockSpec((tm, tn), lambda i, j: (i, j))
    in_specs = [row, row, vec, pl.BlockSpec((tn, Dm), lambda i, j: (j, 0))]
    out_shape = [jax.ShapeDtypeStruct((T, Dm), BF16), jax.ShapeDtypeStruct((1, Dm), F32)]
    if swiglu:
        in_specs += [tile, tile]
        out_shape += [jax.ShapeDtypeStruct((T, N), BF16)] * 3
        out_specs = [row, vec, tile, tile, tile]
        args = (dh, y, gain, w, *gu)
    else:
        out_shape += [jax.ShapeDtypeStruct((T, N), F32)]
        out_specs = [row, vec, tile]
        args = (dh, y, gain, w)
    return pl.pallas_call(
        body, name=name, grid=(T // tm, N // tn), in_specs=in_specs, out_specs=out_specs,
        out_shape=out_shape, compiler_params=_params(("arbitrary", "arbitrary")),
    )(*args)


def mm_nt_normbwd(dPs, Ws, h, gain, dh_in, *, name, wt=False):
    T, Dm = h.shape
    n = len(dPs)
    tm = _row_tile(T, sum(p.shape[1] * p.dtype.itemsize for p in dPs) + 3 * Dm * 4,
                   sum(w.size * w.dtype.itemsize for w in Ws))
    mm = _nn if wt else _nt

    def body(*refs):
        p_refs, w_refs = refs[:n], refs[n:2 * n]
        h_ref, g_ref, dhin_ref, dh_ref, dg_ref = refs[2 * n:]
        du = mm(p_refs[0][...].astype(BF16), w_refs[0][...])
        for p, w in zip(p_refs[1:], w_refs[1:]):
            du = du + mm(p[...].astype(BF16), w[...])
        dx, dg = _rms_bwd(h_ref[...], g_ref[...], du)
        dh_ref[...] = dhin_ref[...] + dx

        @pl.when(pl.program_id(0) == 0)
        def _():
            dg_ref[...] = jnp.zeros_like(dg_ref)

        dg_ref[...] += dg

    row = pl.BlockSpec((tm, Dm), lambda i: (i, 0))
    vec = pl.BlockSpec((1, Dm), lambda i: (0, 0))
    return pl.pallas_call(
        body, name=name, grid=(T // tm,),
        in_specs=[pl.BlockSpec((tm, p.shape[1]), lambda i: (i, 0)) for p in dPs]
        + [pl.BlockSpec(w.shape, lambda i: (0, 0)) for w in Ws] + [row, vec, row],
        out_specs=[row, vec],
        out_shape=[jax.ShapeDtypeStruct((T, Dm), F32), jax.ShapeDtypeStruct((1, Dm), F32)],
        compiler_params=_params(("arbitrary",)),
    )(*dPs, *Ws, h, gain, dh_in)


def mm_tn(a, b, *, name):
    T, M = a.shape
    N = b.shape[1]
    tm, tn, tk = _tile(M, 1408), _tile(N, 1408), _tile(T, 640)

    def body(a_ref, b_ref, o_ref):
        @pl.when(pl.program_id(2) == 0)
        def _():
            o_ref[...] = jnp.zeros_like(o_ref)

        o_ref[...] += _tn(a_ref[...].astype(BF16), b_ref[...].astype(BF16))

    return pl.pallas_call(
        body, name=name, grid=(M // tm, N // tn, T // tk),
        in_specs=[pl.BlockSpec((tk, tm), lambda i, j, k: (k, i)), pl.BlockSpec((tk, tn), lambda i, j, k: (k, j))],
        out_specs=pl.BlockSpec((tm, tn), lambda i, j, k: (i, j)),
        out_shape=jax.ShapeDtypeStruct((M, N), F32),
        compiler_params=_params(("arbitrary", "arbitrary", "arbitrary")),
    )(a, b)


def loss_and_grad(h, target, *, name):
    T, Dm = h.shape

    def body(h_ref, t_ref, loss_ref, dh_ref):
        b = pl.program_id(0)

        @pl.when(b == 0)
        def _():
            loss_ref[...] = jnp.zeros_like(loss_ref)
            dh_ref[...] = jnp.zeros_like(dh_ref)

        @pl.when(b > 0)
        def _():
            e = h_ref[...] - t_ref[...]
            dh_ref[...] = e * (1.0 / Dm)
            loss_ref[...] += jnp.sum(e * e) * (0.5 / Dm)

    return pl.pallas_call(
        body, name=name, grid=(T // BLK,),
        in_specs=[pl.BlockSpec((BLK, Dm), lambda b: (b, 0)),
                  pl.BlockSpec((BLK, Dm), lambda b: (jnp.maximum(b - 1, 0), 0))],
        out_specs=[pl.BlockSpec((8, LANES), lambda b: (0, 0)), pl.BlockSpec((BLK, Dm), lambda b: (b, 0))],
        out_shape=[jax.ShapeDtypeStruct((8, LANES), F32), jax.ShapeDtypeStruct((T, Dm), F32)],
        compiler_params=_params(("arbitrary",)),
    )(h, target)


def adamw(w, g, m, v, *, name):
    R, C = w.shape
    tr = R
    for t in (512, 352, 256):
        if R > t and R % t == 0:
            tr = t
            break

    def body(w_ref, g_ref, m_ref, v_ref, d_ref, nm_ref, nv_ref):
        g_ = g_ref[...]
        m_ = ADAM_B1 * m_ref[...] + (1.0 - ADAM_B1) * g_
        v_ = ADAM_B2 * v_ref[...] + (1.0 - ADAM_B2) * (g_ * g_)
        m_hat = m_ / (1.0 - ADAM_B1 ** ADAM_STEP)
        v_hat = v_ / (1.0 - ADAM_B2 ** ADAM_STEP)
        d_ref[...] = -ADAM_LR * (m_hat / (jnp.sqrt(v_hat) + ADAM_EPS) + ADAM_WD * w_ref[...])
        nm_ref[...] = m_
        nv_ref[...] = v_

    spec = pl.BlockSpec((tr, C), lambda i: (i, 0))
    return pl.pallas_call(
        body, name=name, grid=(R // tr,), in_specs=[spec] * 4, out_specs=[spec] * 3,
        out_shape=[jax.ShapeDtypeStruct((R, C), F32)] * 3, compiler_params=_params(("arbitrary",)),
    )(w, g, m, v)


def _me():
    return lax.axis_index("x"), lax.axis_index("y"), lax.axis_index("c")


def _flip(pos, rel):
    return tuple(1 - p if r else p for p, r in zip(pos, rel))


def _slot(pos):
    return 4 * pos[0] + 2 * pos[1] + pos[2]


HBM_SPEC = pl.BlockSpec(memory_space=pltpu.HBM)
CHIP_RELS = ((1, 0), (0, 1), (1, 1))


def all_gather_big(xs, *, name):
    n = len(xs)

    def body(*refs):
        x_refs, out_refs = refs[:n], refs[n:2 * n]
        send_sems, recv_sems, local_sems = refs[2 * n:]
        me = _me()
        sibling = _flip(me, (0, 0, 1))
        chips = [_flip(me, rel + (0,)) for rel in CHIP_RELS]

        def copy(i, k, block, to, src=None):
            dst = out_refs[i].at[_slot(block)]
            return pltpu.make_async_remote_copy(
                src_ref=dst if src is None else src, dst_ref=dst, send_sem=send_sems.at[i, k],
                recv_sem=recv_sems.at[i, k], device_id=to, device_id_type=MESH)

        sent, local = [], []
        for i in range(n):
            mine = pltpu.make_async_copy(x_refs[i], out_refs[i].at[_slot(me)], local_sems.at[i])
            mine.start()
            local.append(mine)
            sent += [copy(i, 0, me, sibling, src=x_refs[i])]
            sent += [copy(i, 1 + j, me, chip, src=x_refs[i]) for j, chip in enumerate(chips)]
        for cp in sent:
            cp.start()
        for i in range(n):
            for j, chip in enumerate(chips):
                copy(i, 1 + j, chip, me).wait_recv()
                passed = copy(i, 4 + j, chip, sibling)
                passed.start()
                sent.append(passed)
        for i in range(n):
            copy(i, 0, sibling, me).wait_recv()
            for j, chip in enumerate(chips):
                copy(i, 4 + j, _flip(chip, (0, 0, 1)), me).wait_recv()
        for cp in sent:
            cp.wait_send()
        for mine in local:
            mine.wait()

    return pl.pallas_call(
        body, name=name, in_specs=[HBM_SPEC] * n, out_specs=[HBM_SPEC] * n,
        out_shape=[jax.ShapeDtypeStruct((NDEV,) + x.shape, x.dtype) for x in xs],
        scratch_shapes=[pltpu.SemaphoreType.DMA((n, 7)), pltpu.SemaphoreType.DMA((n, 7)), pltpu.SemaphoreType.DMA((n,))],
    )(*xs)


def all_to_all_small(src, *, name):
    _, r, C = src.shape

    def body(src_ref, out_ref, send_sems, recv_sems):
        me = _me()
        my = _slot(me)
        out_ref[my] = src_ref[my]
        copies = []
        for k in range(1, NDEV):
            peer = _flip(me, ((k >> 2) & 1, (k >> 1) & 1, k & 1))
            cp = pltpu.make_async_remote_copy(
                src_ref=src_ref.at[_slot(peer)], dst_ref=out_ref.at[my], send_sem=send_sems.at[k - 1],
                recv_sem=recv_sems.at[k - 1], device_id=peer, device_id_type=MESH)
            cp.start()
            copies.append((cp, peer))
        for k, (cp, peer) in enumerate(copies):
            pltpu.make_async_remote_copy(
                src_ref=src_ref.at[my], dst_ref=out_ref.at[_slot(peer)], send_sem=send_sems.at[k],
                recv_sem=recv_sems.at[k], device_id=peer, device_id_type=MESH).wait_recv()
        for cp, _ in copies:
            cp.wait_send()

    vm = pl.BlockSpec(memory_space=pltpu.VMEM)
    return pl.pallas_call(
        body, name=name, in_specs=[vm], out_specs=vm, out_shape=jax.ShapeDtypeStruct(src.shape, src.dtype),
        scratch_shapes=[pltpu.SemaphoreType.DMA((7,)), pltpu.SemaphoreType.DMA((7,))],
    )(src)


def sum_slots(a, *, name):
    n, r, C = a.shape

    def body(a_ref, o_ref):
        s = a_ref[0]
        for k in range(1, n):
            s = s + a_ref[k]
        o_ref[...] = s

    vm = pl.BlockSpec(memory_space=pltpu.VMEM)
    return pl.pallas_call(body, name=name, in_specs=[vm], out_specs=vm,
                          out_shape=jax.ShapeDtypeStruct((r, C), F32))(a)


def rs_exchange_sibling(gs, *, name):
    n = len(gs)

    def body(*refs):
        g_refs, out_refs, send_sems, recv_sems = refs[:n], refs[n:2 * n], refs[2 * n], refs[2 * n + 1]
        sibling = _flip(_me(), (0, 0, 1))
        copies = []
        for i in range(n):
            for chip in range(4):
                cp = pltpu.make_async_remote_copy(
                    src_ref=g_refs[i].at[2 * chip + sibling[2]], dst_ref=out_refs[i].at[chip],
                    send_sem=send_sems.at[i, chip], recv_sem=recv_sems.at[i, chip], device_id=sibling,
                    device_id_type=MESH)
                cp.start()
                copies.append(cp)
        for cp in copies:
            cp.wait()

    return pl.pallas_call(
        body, name=name, in_specs=[HBM_SPEC] * n, out_specs=[HBM_SPEC] * n,
        out_shape=[jax.ShapeDtypeStruct((4,) + g.shape[1:], g.dtype) for g in gs],
        scratch_shapes=[pltpu.SemaphoreType.DMA((n, 4)), pltpu.SemaphoreType.DMA((n, 4))],
    )(*gs)


def rs_chip_partials(g, got, *, name):
    _, R, C = g.shape
    tr = _tile(R, 768)

    def body(c_ref, g_ref, got_ref, o_ref):
        o_ref[...] = (g_ref[...].astype(F32) + got_ref[...].astype(F32)).astype(o_ref.dtype)

    c = jnp.reshape(lax.axis_index("c"), (1,)).astype(jnp.int32)
    return pl.pallas_call(
        body, name=name,
        grid_spec=pltpu.PrefetchScalarGridSpec(
            num_scalar_prefetch=1, grid=(4, R // tr),
            in_specs=[pl.BlockSpec((None, tr, C), lambda k, i, c_ref: (2 * k + c_ref[0], i, 0)),
                      pl.BlockSpec((None, tr, C), lambda k, i, c_ref: (k, i, 0))],
            out_specs=pl.BlockSpec((None, tr, C), lambda k, i, c_ref: (k, i, 0))),
        out_shape=jax.ShapeDtypeStruct((4, R, C), g.dtype), compiler_params=_params(("arbitrary", "arbitrary")),
    )(c, g, got)


def rs_exchange_chips(ps, *, name):
    n = len(ps)

    def body(*refs):
        p_refs, out_refs, send_sems, recv_sems = refs[:n], refs[n:2 * n], refs[2 * n], refs[2 * n + 1]
        me = _me()
        copies = []
        for i in range(n):
            for j, rel in enumerate(CHIP_RELS):
                peer = _flip(me, rel + (0,))
                cp = pltpu.make_async_remote_copy(
                    src_ref=p_refs[i].at[2 * peer[0] + peer[1]], dst_ref=out_refs[i].at[j], send_sem=send_sems.at[i, j],
                    recv_sem=recv_sems.at[i, j], device_id=peer, device_id_type=MESH)
                cp.start()
                copies.append(cp)
        for cp in copies:
            cp.wait()

    return pl.pallas_call(
        body, name=name, in_specs=[HBM_SPEC] * n, out_specs=[HBM_SPEC] * n,
        out_shape=[jax.ShapeDtypeStruct((3,) + p.shape[1:], p.dtype) for p in ps],
        scratch_shapes=[pltpu.SemaphoreType.DMA((n, 3)), pltpu.SemaphoreType.DMA((n, 3))],
    )(*ps)


def rs_final_sum(p, got, *, name):
    _, R, C = p.shape
    tr = _tile(R, 768)

    def body(chip_ref, p_ref, got_ref, o_ref):
        s = p_ref[...].astype(F32)
        for j in range(3):
            s = s + got_ref[j].astype(F32)
        o_ref[...] = s

    mychip = jnp.reshape(2 * lax.axis_index("x") + lax.axis_index("y"), (1,)).astype(jnp.int32)
    return pl.pallas_call(
        body, name=name,
        grid_spec=pltpu.PrefetchScalarGridSpec(
            num_scalar_prefetch=1, grid=(R // tr,),
            in_specs=[pl.BlockSpec((None, tr, C), lambda i, chip_ref: (chip_ref[0], i, 0)),
                      pl.BlockSpec((3, tr, C), lambda i, chip_ref: (0, i, 0))],
            out_specs=pl.BlockSpec((tr, C), lambda i, chip_ref: (i, 0))),
        out_shape=jax.ShapeDtypeStruct((R, C), F32), compiler_params=_params(("arbitrary",)),
    )(mychip, p, got)


def reduce_scatter_big(gs):
    got = rs_exchange_sibling(gs, name="rs_sibling")
    parts = [rs_chip_partials(g, t, name=f"rs_chip_partials_{i}") for i, (g, t) in enumerate(zip(gs, got))]
    got2 = rs_exchange_chips(parts, name="rs_chips")
    return [rs_final_sum(p, t, name=f"rs_final_sum_{i}") for i, (p, t) in enumerate(zip(parts, got2))]


def _blk(off):
    return pl.BlockSpec((BLK, LANES), lambda h, n: (n, off + h))


def _const_spec(shape):
    return pl.BlockSpec(shape, lambda *_: (0,) * len(shape))


def retention_tables(T):
    pos = jnp.arange(T, dtype=F32) - float(PAD)
    inv_freq = 1.0 / (10000.0 ** jnp.linspace(0.0, 1.0, 64, dtype=F32))
    ang = pos[:, None] * inv_freq[None, :]
    cos = jnp.repeat(jnp.cos(ang), 2, axis=1)
    sin = jnp.repeat(jnp.sin(ang), 2, axis=1) * jnp.tile(jnp.array([-1.0, 1.0], F32), 64)[None, :]
    lane = np.arange(LANES)
    perm = jnp.broadcast_to(jnp.asarray((lane[:, None] == (lane[None, :] ^ 1)).astype(np.float32)), (4, LANES, LANES))
    log_gamma = jnp.log1p(-jnp.exp2(-5.0 - jnp.arange(4, dtype=F32)))
    idx = jnp.arange(BLK, dtype=F32)
    diff = idx[:, None] - idx[None, :]
    intra = jnp.where(diff >= 0, jnp.exp(jnp.maximum(diff, 0.0) * log_gamma[:, None, None]), 0.0)
    zeta = jnp.exp((BLK - 1.0 - idx)[None, :] * log_gamma[:, None])
    xi = jnp.exp((idx + 1.0)[None, :] * log_gamma[:, None])
    bc = lambda t: jnp.broadcast_to(t[:, :, None], (4, BLK, LANES))
    return cos, sin, perm, intra, bc(zeta), bc(xi)


def _heads(x):
    return jnp.stack([x[:, h * LANES:(h + 1) * LANES] for h in range(4)])


def _unheads(y):
    return jnp.concatenate([y[h] for h in range(4)], axis=1)


def _ret_chunk(rq, rk, rv, rg, S, cos, sin, intra, zeta, xi, perm):
    q = rq * cos + hdot(rq, perm) * sin
    k = (rk * cos + hdot(rk, perm) * sin) * (128.0 ** -0.5)
    ret = bdot(bdot(q, _t(k)) * intra, rv) + bdot(q * xi, S)
    S_new = S * xi[..., BLK - 1:BLK, :] + bdot(_t(k * zeta), rv)
    c = ret - jnp.mean(ret, axis=-1, keepdims=True)
    out = c * lax.rsqrt(jnp.mean(c * c, axis=-1, keepdims=True) + EPS) * _silu(rg)
    return out, S_new


def _wide(off):
    return pl.BlockSpec((BLK, 4 * LANES), lambda n: (n, off))


def retention_fwd(p, tables, *, name):
    T = p.shape[0]
    N = T // BLK
    cos, sin, perm, intra, zeta, xi = tables

    def body(rq, rk, rv, rg, cos_ref, sin_ref, in_ref, ze_ref, xi_ref, perm_ref, out_ref, sall_ref, s_scr):
        @pl.when(pl.program_id(0) == 0)
        def _():
            s_scr[...] = jnp.zeros_like(s_scr)

        S = s_scr[...]
        sall_ref[...] = S
        out, S_new = _ret_chunk(_heads(rq[...]), _heads(rk[...]), _heads(rv[...]), _heads(rg[...]), S, cos_ref[...],
                                sin_ref[...], in_ref[...], ze_ref[...], xi_ref[...], perm_ref[...])
        out_ref[...] = _unheads(out).astype(BF16)
        s_scr[...] = S_new

    rowtab = pl.BlockSpec((BLK, LANES), lambda n: (n, 0))
    tab = _const_spec((4, BLK, LANES))
    return pl.pallas_call(
        body, name=name, grid=(N,),
        in_specs=[_wide(0), _wide(1), _wide(2), _wide(3), rowtab, rowtab, tab, tab, tab, tab],
        out_specs=[_wide(0), pl.BlockSpec((None, 4, LANES, LANES), lambda n: (n, 0, 0, 0))],
        out_shape=[jax.ShapeDtypeStruct((T, 512), BF16), jax.ShapeDtypeStruct((N, 4, LANES, LANES), F32)],
        scratch_shapes=[pltpu.VMEM((4, LANES, LANES), F32)],
        compiler_params=_params(("arbitrary",)),
    )(p, p, p, p, cos, sin, intra, zeta, xi, perm)


def _row_mask(n):
    return (n * BLK + _iota2((BLK, 1), 0) >= PAD).astype(F32)


def retention_bwd(p, sall, dmixed, tables, *, name):
    T = p.shape[0]
    N = T // BLK
    cos, sin, perm, intra, zeta, xi = tables

    def body(rq, rk, rv, rg, cos_ref, sin_ref, in_ref, ze_ref, xi_ref, perm_ref, sall_ref, do_ref, drq, drk, drv, drg,
             ds_scr):
        n = N - 1 - pl.program_id(0)

        @pl.when(pl.program_id(0) == 0)
        def _():
            ds_scr[...] = jnp.zeros_like(ds_scr)

        f = lambda a, b, c, d, s: _ret_chunk(a, b, c, d, s, cos_ref[...], sin_ref[...], in_ref[...], ze_ref[...],
                                             xi_ref[...], perm_ref[...])
        _, vjp = jax.vjp(f, _heads(rq[...]), _heads(rk[...]), _heads(rv[...]), _heads(rg[...]), sall_ref[...])
        g = vjp((_heads(do_ref[...]), ds_scr[...]))
        mask = _row_mask(n)
        for ref, val in zip((drq, drk, drv, drg), g[:4]):
            ref[...] = _unheads(val) * mask
        ds_scr[...] = g[4]

    def rwide(off):
        return pl.BlockSpec((BLK, 4 * LANES), lambda n: (N - 1 - n, off))

    rowtab = pl.BlockSpec((BLK, LANES), lambda n: (N - 1 - n, 0))
    tab = _const_spec((4, BLK, LANES))
    return pl.pallas_call(
        body, name=name, grid=(N,),
        in_specs=[rwide(0), rwide(1), rwide(2), rwide(3), rowtab, rowtab, tab, tab, tab, tab,
                  pl.BlockSpec((None, 4, LANES, LANES), lambda n: (N - 1 - n, 0, 0, 0)), rwide(0)],
        out_specs=[rwide(0)] * 4, out_shape=[jax.ShapeDtypeStruct((T, 512), F32)] * 4,
        scratch_shapes=[pltpu.VMEM((4, LANES, LANES), F32)],
        compiler_params=_params(("arbitrary",)),
    )(p, p, p, p, cos, sin, intra, zeta, xi, perm, sall, dmixed)


def conv_silu_fwd(p, w, *, name):
    T = p.shape[0]
    N = T // BLK

    def body(x_ref, xp_ref, w_ref, o_ref):
        n = pl.program_id(0)
        cur = x_ref[...]
        cat = jnp.concatenate([jnp.where(n > 0, xp_ref[...], 0.0), cur], axis=0)
        y = w_ref[3:4, :] * cur
        for s in (1, 2, 3):
            y = y + w_ref[3 - s:4 - s, :] * pltpu.roll(cat, s, 0)[BLK:]
        o_ref[...] = _silu(y)

    cw = 4 * LANES
    return pl.pallas_call(
        body, name=name, grid=(N, 3),
        in_specs=[pl.BlockSpec((BLK, cw), lambda n, c: (n, 4 + c)),
                  pl.BlockSpec((BLK, cw), lambda n, c: (jnp.maximum(n - 1, 0), 4 + c)),
                  pl.BlockSpec((4, cw), lambda n, c: (0, c))],
        out_specs=pl.BlockSpec((BLK, cw), lambda n, c: (n, c)),
        out_shape=jax.ShapeDtypeStruct((T, 1536), F32), compiler_params=_params(("arbitrary", "arbitrary")),
    )(p, p, w)


def conv_silu_bwd(p, w, dact, part, *, name):
    T = p.shape[0]
    N = T // BLK
    cw = 4 * LANES

    def body(xp_ref, x_ref, xn_ref, w_ref, da_ref, dan_ref, dx_ref, dw_ref):
        n = pl.program_id(0)
        last = n == N - 1
        cat = jnp.concatenate([jnp.where(n > 0, xp_ref[...], 0.0), x_ref[...], jnp.where(last, 0.0, xn_ref[...])], axis=0)
        shifted = [cat] + [pltpu.roll(cat, s, 0) for s in (1, 2, 3)]
        y = w_ref[3:4, :] * shifted[0]
        for s in (1, 2, 3):
            y = y + w_ref[3 - s:4 - s, :] * shifted[s]
        y = y[BLK:]
        da = jnp.concatenate([da_ref[...], jnp.where(last, 0.0, dan_ref[...])], axis=0)
        sg = _sigmoid(y)
        dy = da * sg * (1.0 + y * (1.0 - sg))
        dx = w_ref[3:4, :] * dy[:BLK]
        for s in (1, 2, 3):
            dx = dx + w_ref[3 - s:4 - s, :] * pltpu.roll(dy, 2 * BLK - s, 0)[:BLK]
        dx_ref[...] = dx * _row_mask(n)

        @pl.when(n == 0)
        def _():
            dw_ref[...] = jnp.zeros_like(dw_ref)

        for s in (0, 1, 2, 3):
            dw_ref[3 - s:4 - s, :] += jnp.sum(dy[:BLK] * shifted[s][BLK:2 * BLK], axis=0, keepdims=True)

    def xs(d):
        return pl.BlockSpec((BLK, cw), lambda n: (jnp.clip(n + d, 0, N - 1), 4 + part))

    return pl.pallas_call(
        body, name=name, grid=(N,),
        in_specs=[xs(-1), xs(0), xs(1), pl.BlockSpec((4, cw), lambda n: (0, part)),
                  pl.BlockSpec((BLK, cw), lambda n: (n, 0)),
                  pl.BlockSpec((BLK, cw), lambda n: (jnp.minimum(n + 1, N - 1), 0))],
        out_specs=[pl.BlockSpec((BLK, cw), lambda n: (n, 0)), pl.BlockSpec((4, cw), lambda n: (0, 0))],
        out_shape=[jax.ShapeDtypeStruct((T, 512), F32), jax.ShapeDtypeStruct((4, 512), F32)],
        compiler_params=_params(("arbitrary",)),
    )(p, p, p, w, dact, dact)


def _softplus(x):
    return jnp.maximum(x, 0.0) + jnp.log1p(jnp.exp(-jnp.abs(x)))


def _pick4(tile, off):
    return jnp.stack([_lane_pick(tile, off + h) for h in range(4)])


def _spread4(v4, off, rows):
    lane = _iota2((rows, LANES), 1)
    out = jnp.where(lane == off, v4[0], 0.0)
    for h in range(1, 4):
        out = out + jnp.where(lane == off + h, v4[h], 0.0)
    return out


def _gdn_chunk(qa, ka, va, z, braw, araw, S, alog, dtb, onorm, rowmask, lincl):
    r, c = _iota2((BLK, BLK), 0), _iota2((BLK, BLK), 1)
    incl, strict = r >= c, r > c
    eye = (r == c).astype(F32)
    q = qa * lax.rsqrt(jnp.sum(qa * qa, axis=-1, keepdims=True) + EPS) * (128.0 ** -0.5)
    k = ka * lax.rsqrt(jnp.sum(ka * ka, axis=-1, keepdims=True) + EPS)
    beta = _sigmoid(braw) * rowmask
    g = -jnp.exp(alog) * _softplus(araw + dtb) * rowmask
    gc = hdot(lincl, jnp.broadcast_to(g, qa.shape))
    decay = jnp.where(incl, jnp.exp(jnp.where(incl, gc - _t(gc), 0.0)), 0.0)
    kb = k * beta
    amat = jnp.where(strict, bdot(kb, _t(k)) * decay, 0.0)
    m = -amat
    inv = eye + m
    pw = hdot(m, m)
    for t in range(6):
        inv = inv + hdot(inv, pw)
        if t < 5:
            pw = hdot(pw, pw)
    egc = jnp.exp(gc)
    u = hdot(inv, va * beta)
    w = hdot(inv, kb * egc)
    qk = jnp.where(incl, bdot(q, _t(k)) * decay, 0.0)
    glast = gc[..., BLK - 1:BLK, :]
    vnew = u - bdot(w, S)
    o = bdot(q * egc, S) + bdot(qk, vnew)
    S_new = S * jnp.exp(glast) + bdot(_t(k * jnp.exp(glast - gc)), vnew)
    out = o * lax.rsqrt(jnp.mean(o * o, axis=-1, keepdims=True) + EPS) * onorm * _silu(z)
    return out, S_new


def _lincl():
    i = np.arange(BLK)
    return jnp.broadcast_to(jnp.asarray((i[:, None] >= i[None, :]).astype(np.float32)), (4, BLK, BLK))


def gdn_fwd(act, p, alog, dtb, onorm, *, name):
    T = p.shape[0]
    N = T // BLK

    def body(qa, ka, va, z, ba, alog_ref, dtb_ref, on_ref, l_ref, out_ref, sall_ref, s_scr):
        n = pl.program_id(0)

        @pl.when(n == 0)
        def _():
            s_scr[...] = jnp.zeros_like(s_scr)

        S = s_scr[...]
        sall_ref[...] = S
        out, S_new = _gdn_chunk(_heads(qa[...]), _heads(ka[...]), _heads(va[...]), _heads(z[...]), _pick4(ba[...], 0),
                                _pick4(ba[...], 4), S, _pick4(alog_ref[...], 0), _pick4(dtb_ref[...], 0), on_ref[...],
                                _row_mask(n), l_ref[...])
        out_ref[...] = _unheads(out).astype(BF16)
        s_scr[...] = S_new

    vec = _const_spec((1, LANES))
    return pl.pallas_call(
        body, name=name, grid=(N,),
        in_specs=[_wide(0), _wide(1), _wide(2), _wide(7), pl.BlockSpec((BLK, LANES), lambda n: (n, 32)), vec, vec, vec,
                  _const_spec((4, BLK, BLK))],
        out_specs=[_wide(0), pl.BlockSpec((None, 4, LANES, LANES), lambda n: (n, 0, 0, 0))],
        out_shape=[jax.ShapeDtypeStruct((T, 512), BF16), jax.ShapeDtypeStruct((N, 4, LANES, LANES), F32)],
        scratch_shapes=[pltpu.VMEM((4, LANES, LANES), F32)],
        compiler_params=_params(("arbitrary",)),
    )(act, act, act, p, p, alog, dtb, onorm, _lincl())


def gdn_bwd(act, p, alog, dtb, onorm, sall, dmixed, *, name):
    T = p.shape[0]
    N = T // BLK

    def body(qa, ka, va, z, ba, alog_ref, dtb_ref, on_ref, l_ref, sall_ref, do_ref,
             dq_ref, dk_ref, dv_ref, dz_ref, dba_ref, dal_ref, ddt_ref, don_ref, ds_scr):
        step = pl.program_id(0)
        n = N - 1 - step

        @pl.when(step == 0)
        def _():
            ds_scr[...] = jnp.zeros_like(ds_scr)
            dal_ref[...] = jnp.zeros_like(dal_ref)
            ddt_ref[...] = jnp.zeros_like(ddt_ref)
            don_ref[...] = jnp.zeros_like(don_ref)

        rowmask, lincl = _row_mask(n), l_ref[...]
        f = lambda *a: _gdn_chunk(*a, rowmask, lincl)
        _, vjp = jax.vjp(f, _heads(qa[...]), _heads(ka[...]), _heads(va[...]), _heads(z[...]), _pick4(ba[...], 0),
                         _pick4(ba[...], 4), sall_ref[...], _pick4(alog_ref[...], 0), _pick4(dtb_ref[...], 0),
                         on_ref[...])
        g = vjp((_heads(do_ref[...]), ds_scr[...]))
        dq_ref[...] = _unheads(g[0]) * rowmask
        dk_ref[...] = _unheads(g[1]) * rowmask
        dv_ref[...] = _unheads(g[2]) * rowmask
        dz_ref[...] = _unheads(g[3]) * rowmask
        dba_ref[...] = (_spread4(g[4], 0, BLK) + _spread4(g[5], 4, BLK)) * rowmask
        ds_scr[...] = g[6]
        dal_ref[...] += _spread4(g[7], 0, 1)
        ddt_ref[...] += _spread4(g[8], 0, 1)
        don_ref[...] += g[9]

    def rwide(off):
        return pl.BlockSpec((BLK, 4 * LANES), lambda s: (N - 1 - s, off))

    vec = _const_spec((1, LANES))
    col = pl.BlockSpec((BLK, LANES), lambda s: (N - 1 - s, 0))
    return pl.pallas_call(
        body, name=name, grid=(N,),
        in_specs=[rwide(0), rwide(1), rwide(2), rwide(7), pl.BlockSpec((BLK, LANES), lambda s: (N - 1 - s, 32)), vec, vec,
                  vec, _const_spec((4, BLK, BLK)),
                  pl.BlockSpec((None, 4, LANES, LANES), lambda s: (N - 1 - s, 0, 0, 0)), rwide(1)],
        out_specs=[rwide(0)] * 4 + [col, vec, vec, vec],
        out_shape=[jax.ShapeDtypeStruct((T, 512), F32)] * 4 + [jax.ShapeDtypeStruct((T, LANES), F32)]
        + [jax.ShapeDtypeStruct((1, LANES), F32)] * 3,
        scratch_shapes=[pltpu.VMEM((4, LANES, LANES), F32)],
        compiler_params=_params(("arbitrary",)),
    )(act, act, act, p, p, alog, dtb, onorm, _lincl(), sall, dmixed)


NEG = -1e30


def _swa_block(q, k0, kp, kc, v0, vp, vc, sink, n):
    r, c = _iota2((BLK, BLK), 0), _iota2((BLK, BLK), 1)
    m0 = (c >= PAD) & (c <= n * BLK + r)
    mp = (n >= 2) & (c > r)
    mc = (n >= 1) & (r >= c)
    b = lambda t: jnp.broadcast_to(t, (4,) + t.shape)
    qs = q * (64.0 ** -0.5)
    s0 = jnp.where(m0, bdot(qs, _t(b(k0))), NEG)
    sp = jnp.where(mp, bdot(qs, _t(b(kp))), NEG)
    sc = jnp.where(mc, bdot(qs, _t(b(kc))), NEG)
    mx = jnp.maximum(jnp.max(jnp.maximum(jnp.maximum(s0, sp), sc), axis=-1, keepdims=True), sink)
    mx = lax.stop_gradient(mx)
    p0, pp, pc = jnp.exp(s0 - mx), jnp.exp(sp - mx), jnp.exp(sc - mx)
    den = (jnp.sum(p0, axis=-1, keepdims=True) + jnp.sum(pp, axis=-1, keepdims=True)
           + jnp.sum(pc, axis=-1, keepdims=True) + jnp.exp(sink - mx))
    return (bdot(p0, b(v0)) + bdot(pp, b(vp)) + bdot(pc, b(vc))) / den


def _swa_specs():
    rows = (lambda n: 0, lambda n: jnp.maximum(n - 1, 0), lambda n: n)

    def kv_spec(off, row):
        return pl.BlockSpec((BLK, LANES), lambda g, n: (row(n), off + g))

    q = pl.BlockSpec((BLK, 4 * LANES), lambda g, n: (n, g))
    return q, [kv_spec(off, row) for off in (8, 10) for row in rows]


def swa_fwd(p2, sinkrow, *, name):
    T = p2.shape[0]
    N = T // BLK

    def body(q, k0, kp, kc, v0, vp, vc, sink_ref, o_ref):
        g, n = pl.program_id(0), pl.program_id(1)
        f32 = lambda ref: ref[...].astype(F32)
        o = _swa_block(_heads(f32(q)), f32(k0), f32(kp), f32(kc), f32(v0), f32(vp), f32(vc),
                       _pick4(sink_ref[...], 4 * g), n)
        o_ref[...] = _unheads(o).astype(BF16)

    q, kv = _swa_specs()
    return pl.pallas_call(
        body, name=name, grid=(2, N), in_specs=[q] + kv + [_const_spec((1, LANES))],
        out_specs=q, out_shape=jax.ShapeDtypeStruct((T, 1024), BF16),
        compiler_params=_params(("arbitrary", "arbitrary")),
    )(p2, p2, p2, p2, p2, p2, p2, sinkrow)


def swa_bwd(p2, sinkrow, dmixed, *, name):
    T = p2.shape[0]
    N = T // BLK

    def body(q, k0, kp, kc, v0, vp, vc, sink_ref, do_ref, dq_ref, dk_ref, dv_ref, dsink_ref):
        g, n = pl.program_id(0), pl.program_id(1)

        @pl.when(n == 0)
        def _():
            dk_ref[...] = jnp.zeros_like(dk_ref)
            dv_ref[...] = jnp.zeros_like(dv_ref)

        @pl.when((g == 0) & (n == 0))
        def _():
            dsink_ref[...] = jnp.zeros_like(dsink_ref)

        f = lambda *a: _swa_block(*a, n)
        f32 = lambda ref: ref[...].astype(F32)
        _, vjp = jax.vjp(f, _heads(f32(q)), f32(k0), f32(kp), f32(kc), f32(v0), f32(vp), f32(vc),
                         _pick4(sink_ref[...], 4 * g))
        dq, dk0, dkp, dkc, dv0, dvp, dvc, dsink = vjp(_heads(do_ref[...]))
        dq_ref[...] = _unheads(dq)
        prev = pl.ds(pl.multiple_of(jnp.maximum(n - 1, 0) * BLK, BLK), BLK)
        cur = pl.ds(pl.multiple_of(n * BLK, BLK), BLK)
        for ref, d0, dp, dc in ((dk_ref, dk0, dkp, dkc), (dv_ref, dv0, dvp, dvc)):
            ref[0:BLK, :] += d0
            ref[prev, :] += dp
            ref[cur, :] += dc
        dsink_ref[...] += _spread4(dsink, 4 * g, 1)

    qspec, kv = _swa_specs()
    slab = pl.BlockSpec((T, LANES), lambda g, n: (0, g))
    return pl.pallas_call(
        body, name=name, grid=(2, N), in_specs=[qspec] + kv + [_const_spec((1, LANES)), qspec],
        out_specs=[qspec, slab, slab, _const_spec((1, LANES))],
        out_shape=[jax.ShapeDtypeStruct((T, 1024), F32), jax.ShapeDtypeStruct((T, 256), F32),
                   jax.ShapeDtypeStruct((T, 256), F32), jax.ShapeDtypeStruct((1, LANES), F32)],
        compiler_params=_params(("arbitrary", "arbitrary")),
    )(p2, p2, p2, p2, p2, p2, p2, sinkrow, dmixed)


def _split_dot(x, m):
    rows = x.shape[0]
    hi = x.astype(BF16)
    lo = (x - hi.astype(F32)).astype(BF16)
    r = _nn(jnp.concatenate([hi, lo], axis=0), m)
    return r[:rows] + r[rows:]


def _tri_and_ones(strict, ones=True):
    i = np.arange(BLK)
    m = (i[:, None] > i[None, :]) if strict else (i[:, None] >= i[None, :])
    if ones:
        m = np.concatenate([m, np.ones((BLK, BLK), bool)], axis=1)
    return jnp.asarray(m.astype(np.float32), dtype=BF16)


def _later_and_row_sums(x, m):
    r = _split_dot(x, m)
    if m.shape[1] == 2 * BLK:
        return r[:, :BLK], r[:, BLK:]
    return r, jnp.broadcast_to(jnp.sum(x, axis=1, keepdims=True), x.shape)


def _sb_offsets(kb):
    row, lane = _iota2((kb * BLK, BLK), 0), _iota2((kb * BLK, BLK), 1)
    chunk = row & ~(BLK - 1)
    return chunk + lane - (row & (BLK - 1)), chunk + lane


def _sb_weights(qb, ks, base, n, offsets, carry, after_ones):
    kb = len(ks)
    zs = [_nt(qb, kc) for kc in ks]
    valid, lb, sums = [], [], []
    for c, z in enumerate(zs):
        part = slice(c * BLK, (c + 1) * BLK)
        valid.append((offsets[0][part] < (n - base) * BLK) & (offsets[1][part] >= PAD - base * BLK))
        lb.append(jnp.minimum(z, 0.0) - jnp.log(1.0 + jnp.exp(-jnp.abs(z))))
        sums.append(_later_and_row_sums(jnp.where(valid[c], lb[c] - z, 0.0), after_ones))
    a = [None] * kb
    for c in reversed(range(kb)):
        a[c] = jnp.where(valid[c], jnp.exp(lb[c] + carry + sums[c][0]), 0.0)
        carry = carry + sums[c][1]
    cat = lambda parts: jnp.concatenate(parts, axis=0)
    return cat(valid), cat(lb), cat(a), carry


def _key_blocks(n_blocks):
    return next(k for k in (5, 3, 1) if n_blocks % k == 0)


def sb_fwd(p2, *, name):
    T = p2.shape[0]
    N = T // BLK
    kb = _key_blocks(N)

    def body(q_ref, k_ref, v_ref, after_ref, o_ref, of_ref):
        n = pl.program_id(1)
        qb = (q_ref[...].astype(F32) * (64.0 ** -0.5)).astype(BF16)
        after, offsets = after_ref[...], _sb_offsets(kb)

        nsup = n // kb + 1

        def step(t, c):
            acc, carry = c
            base = (nsup - 1 - t) * kb
            rows = [pl.ds(pl.multiple_of((base + sub) * BLK, BLK), BLK) for sub in range(kb)]
            _, _, a, carry = _sb_weights(qb, [k_ref[r, :].astype(BF16) for r in rows], base, n, offsets, carry, after)
            ab = a.astype(BF16)
            for sub, r in enumerate(rows):
                acc = acc + _nn(ab[sub * BLK:(sub + 1) * BLK], v_ref[r, :].astype(BF16))
            return acc, carry

        acc, _ = lax.fori_loop(0, nsup, step, (jnp.zeros((BLK, LANES), F32), jnp.zeros((BLK, LANES), F32)))
        o_ref[...] = acc.astype(BF16)
        of_ref[...] = acc

    def slab(off):
        return pl.BlockSpec((T, LANES), lambda h, n: (0, off + h))

    return pl.pallas_call(
        body, name=name, grid=(8, N),
        in_specs=[_blk(12), slab(20), slab(28), _const_spec((BLK, BLK))],
        out_specs=[_blk(0), _blk(0)],
        out_shape=[jax.ShapeDtypeStruct((T, 1024), BF16), jax.ShapeDtypeStruct((T, 1024), F32)],
        compiler_params=_params(("arbitrary", "arbitrary")),
    )(p2, p2, p2, _tri_and_ones(True, ones=False))


def sb_bwd(p2, o, dmixed, *, name):
    T = p2.shape[0]
    N = T // BLK
    kb = _key_blocks(N)

    def body(q_ref, k_ref, v_ref, after_ref, from_ref, o_ref, do_ref, dq_ref, dk_ref, dv_ref, dkt_scr, dvt_scr):
        n = pl.program_id(1)

        @pl.when(n == 0)
        def _():
            dkt_scr[...] = jnp.zeros_like(dkt_scr)
            dvt_scr[...] = jnp.zeros_like(dvt_scr)

        q = q_ref[...].astype(F32)
        qb = (q * (64.0 ** -0.5)).astype(BF16)
        qt = q.T.astype(BF16)
        do = do_ref[...]
        dob = do.astype(BF16)
        dot_ = do.T.astype(BF16)
        total = jnp.sum(dob.astype(F32) * o_ref[...], axis=1, keepdims=True)
        total = jnp.broadcast_to(total, (BLK, LANES))
        after, frm, offsets = after_ref[...], from_ref[...], _sb_offsets(kb)

        nsup = n // kb + 1

        def step(t, c):
            dq, carry, gcarry = c
            base = (nsup - 1 - t) * kb
            rows = [pl.ds(pl.multiple_of((base + sub) * BLK, BLK), BLK) for sub in range(kb)]
            ks = [k_ref[r, :].astype(BF16) for r in rows]
            valid, lb, a, carry = _sb_weights(qb, ks, base, n, offsets, carry, after)
            ab = a.astype(BF16)
            das = [_nt(dob, v_ref[r, :].astype(BF16)) for r in rows]
            g, sums = [], []
            for sub in range(kb):
                part = slice(sub * BLK, (sub + 1) * BLK)
                g.append(das[sub] * ab[part].astype(F32))
                sums.append(_later_and_row_sums(g[sub], frm))
            for sub in reversed(range(kb)):
                part = slice(sub * BLK, (sub + 1) * BLK)
                before = total - (gcarry + sums[sub][0])
                gcarry = gcarry + sums[sub][1]
                beta = jnp.exp(lb[part])
                dz = jnp.where(valid[part], g[sub] * (1.0 - beta) - beta * before, 0.0) * (64.0 ** -0.5)
                dz = dz.astype(BF16)
                dq = dq + _nn(dz, ks[sub])
                dkt_scr[base + sub] += _nn(qt, dz)
                dvt_scr[base + sub] += _nn(dot_, ab[part])
            return dq, carry, gcarry

        zero = jnp.zeros((BLK, LANES), F32)
        dq, _, _ = lax.fori_loop(0, nsup, step, (zero, zero, zero))
        dq_ref[...] = dq

        @pl.when(n == N - 1)
        def _():
            def flush(j, _):
                rows = pl.ds(pl.multiple_of(j * BLK, BLK), BLK)
                dk_ref[rows, :] = dkt_scr[j].T
                dv_ref[rows, :] = dvt_scr[j].T
                return 0

            lax.fori_loop(0, N, flush, 0)

    def slab(off):
        return pl.BlockSpec((T, LANES), lambda h, n: (0, off + h))

    return pl.pallas_call(
        body, name=name, grid=(8, N),
        in_specs=[_blk(12), slab(20), slab(28), _const_spec((BLK, BLK)), _const_spec((BLK, BLK)), _blk(0), _blk(8)],
        out_specs=[_blk(0), slab(0), slab(0)],
        out_shape=[jax.ShapeDtypeStruct((T, 1024), F32)] * 3,
        scratch_shapes=[pltpu.VMEM((N, LANES, LANES), F32), pltpu.VMEM((N, LANES, LANES), F32)],
        compiler_params=_params(("arbitrary", "arbitrary")),
    )(p2, p2, p2, _tri_and_ones(True, ones=False), _tri_and_ones(False, ones=False), o, dmixed)


def ffn_fwd(h, g_pre, g_post, wg, wu, wd, tag):
    u, gate, up, act = norm_mm(h, g_pre, (wg, wu), swiglu=True, wt=True, name=f"ffn_up_{tag}")
    y, h_new = mm_norm_res([act], [wd], h, g_post, 0.5, name=f"ffn_down_{tag}")
    return h_new, (h, u, gate, up, y)


def ffn_bwd(saved, dh, g_pre, g_post, wg, wu, wd, tag):
    h, u, gate, up, y = saved
    dy, dg_post, dgate, dup, act = normbwd_mm_nt(dh, y, g_post, wd, 0.5, (gate, up), name=f"ffn_bwd_down_{tag}")
    dwd = mm_tn(act, dy, name=f"ffn_dwd_{tag}")
    dwg = mm_tn(dgate, u, name=f"ffn_dwg_{tag}")
    dwu = mm_tn(dup, u, name=f"ffn_dwu_{tag}")
    dh_in, dg_pre = mm_nt_normbwd([dgate, dup], [wg, wu], h, g_pre, dh, wt=True, name=f"ffn_bwd_up_{tag}")
    return dh_in, (dg_pre, dg_post), (dwg, dwu, dwd)


def _lane_row(v):
    v = v.reshape(1, -1)
    return jnp.pad(v, ((0, 0), (0, LANES - v.shape[1])))


AB_WIDTHS = (512,) * 8 + (LANES,)


def mixer_ab_fwd(h, g_pre, g_post, w_in, conv_w, a_log, dt_bias, out_norm, w_out, tables):
    u, p = norm_mm(h, g_pre, (w_in,), swiglu=False, name="ab_in")
    ret, sall_r = retention_fwd(p, tables, name="retention_fwd")
    act = conv_silu_fwd(p, conv_w, name="conv_fwd")
    gdn, sall_g = gdn_fwd(act, p, _lane_row(a_log), _lane_row(dt_bias), out_norm.reshape(1, LANES), name="gdn_fwd")
    y, h_new = mm_norm_res([ret, gdn], [w_out[:512], w_out[512:]], h, g_post, 1.0, name="ab_out")
    return h_new, (h, u, p, ret, sall_r, act, gdn, sall_g, y)


def mixer_ab_bwd(saved, dh, g_pre, g_post, w_in, conv_w, a_log, dt_bias, out_norm, w_out, tables):
    h, u, p, ret, sall_r, act, gdn, sall_g, y = saved
    dy, dg_post, dmixed = normbwd_mm_nt(dh, y, g_post, w_out, 1.0, name="ab_bwd_out")
    dw_out = jnp.concatenate([mm_tn(ret, dy, name="ab_dwout_ret"), mm_tn(gdn, dy, name="ab_dwout_gdn")], axis=0)
    pieces = list(retention_bwd(p, sall_r, dmixed, tables, name="retention_bwd"))
    dqa, dka, dva, dz, dba, dalog, ddtb, donorm = gdn_bwd(
        act, p, _lane_row(a_log), _lane_row(dt_bias), out_norm.reshape(1, LANES), sall_g, dmixed, name="gdn_bwd")
    dconv = []
    for part, dact in enumerate((dqa, dka, dva)):
        dx, dw = conv_silu_bwd(p, conv_w, dact, part, name=f"conv_bwd_{part}")
        pieces.append(dx)
        dconv.append(dw)
    pieces += [dz, dba]
    offs = np.cumsum((0,) + AB_WIDTHS)
    w_parts = [w_in[:, a:b] for a, b in zip(offs[:-1], offs[1:])]
    dh_in, dg_pre = mm_nt_normbwd(pieces, w_parts, h, g_pre, dh, name="ab_bwd_in")
    dw_in = jnp.concatenate([mm_tn(u, pc, name=f"ab_dwin_{i}") for i, pc in enumerate(pieces)], axis=1)
    small = (jnp.concatenate(dconv, axis=1), dalog[:, :4], ddtb[:, :4], donorm)
    return dh_in, (dg_pre, dg_post), (dw_in, dw_out), small


CD_WIDTHS = (1024, 256, 256, 1024, 1024, 1024)


def mixer_cd_fwd(h, g_pre, g_post, w_in, sinks, w_out):
    u, p2 = norm_mm(h, g_pre, (w_in,), swiglu=False, out_dtype=BF16, name="cd_in")
    swa = swa_fwd(p2, _lane_row(sinks), name="swa_fwd")
    sb, sb_f32 = sb_fwd(p2, name="sb_fwd")
    y, h_new = mm_norm_res([swa, sb], [w_out[:1024], w_out[1024:]], h, g_post, 1.0, name="cd_out")
    return h_new, (h, u, p2, swa, sb, sb_f32, y)


def mixer_cd_bwd(saved, dh, g_pre, g_post, w_in, sinks, w_out):
    h, u, p2, swa, sb, sb_f32, y = saved
    dy, dg_post, dmixed = normbwd_mm_nt(dh, y, g_post, w_out, 1.0, name="cd_bwd_out")
    dw_out = jnp.concatenate([mm_tn(swa, dy, name="cd_dwout_swa"), mm_tn(sb, dy, name="cd_dwout_sb")], axis=0)
    dq_c, dk_c, dv_c, dsink = swa_bwd(p2, _lane_row(sinks), dmixed, name="swa_bwd")
    pieces = [dq_c, dk_c, dv_c] + list(sb_bwd(p2, sb_f32, dmixed, name="sb_bwd"))
    offs = np.cumsum((0,) + CD_WIDTHS)
    w_parts = [w_in[:, a:b] for a, b in zip(offs[:-1], offs[1:])]
    dh_in, dg_pre = mm_nt_normbwd(pieces, w_parts, h, g_pre, dh, name="cd_bwd_in")
    dw_in = jnp.concatenate([mm_tn(u, pc, name=f"cd_dwin_{i}") for i, pc in enumerate(pieces)], axis=1)
    return dh_in, (dg_pre, dg_post), (dw_in, dw_out), dsink[:, :8]


def _pad_heads(w, axis):
    shape = w.shape
    w = w.reshape(shape[:axis] + (shape[axis] // 64, 64) + shape[axis + 1:])
    pad = [(0, 0)] * w.ndim
    pad[axis + 1] = (0, 64)
    return jnp.pad(w, pad).reshape(shape[:axis] + (2 * shape[axis],) + shape[axis + 1:])


def _unpad_heads(w, axis):
    shape = w.shape
    w = w.reshape(shape[:axis] + (shape[axis] // 128, 128) + shape[axis + 1:])
    w = lax.slice_in_dim(w, 0, 64, axis=axis + 1)
    return w.reshape(shape[:axis] + (shape[axis] // 2,) + shape[axis + 1:])


SMALL_SHARDED = (("meta_tokens", (NMETA, LANES), 1), ("norm_gains", (2, 6, LANES), 2), ("ab_conv_w", (1, 4, 192), 2))
SMALL_REPL = (("ab_a_log", (1, 4)), ("ab_dt_bias", (1, 4)), ("ab_out_norm", (1, LANES)), ("cd_sinks", (1, 8)))


def _stack_shards(g, axis):
    full = jnp.moveaxis(g, 0, axis)
    shape = full.shape
    return full.reshape(shape[:axis] + (shape[axis] * shape[axis + 1],) + shape[axis + 2:])


def _split_shards(full, axis):
    shape = full.shape
    g = full.reshape(shape[:axis] + (NDEV, shape[axis] // NDEV) + shape[axis + 1:])
    return jnp.moveaxis(g, axis, 0)


def _pad_rows8(a):
    rows = []
    for x in a:
        flat = x.reshape(x.shape[0], -1)
        n = -(-flat.shape[1] // LANES) * LANES
        rows.append(jnp.pad(flat, ((0, 0), (0, n - flat.shape[1]))).reshape(x.shape[0], n // LANES, LANES))
    cat = jnp.concatenate(rows, axis=1)
    return jnp.pad(cat, ((0, 0), (0, -cat.shape[1] % 8), (0, 0)))


def _unpad_rows8(packed, shapes):
    out, at = [], 0
    for shape in shapes:
        size = int(np.prod(shape))
        nrow = -(-size // LANES)
        blk = packed[:, at:at + nrow].reshape(packed.shape[0], -1)[:, :size]
        out.append(blk.reshape((packed.shape[0],) + tuple(shape)))
        at += nrow
    return out


def kernel(x, meta_tokens, norm_gains, ffn_w_gate, ffn_w_up, ffn_w_down, ab_w_in, ab_conv_w, ab_a_log, ab_dt_bias, ab_out_norm, ab_w_out, cd_w_in, cd_sinks, cd_w_out, loss_target, m_meta_tokens, m_norm_gains, m_ffn_w_gate, m_ffn_w_up, m_ffn_w_down, m_ab_w_in, m_ab_conv_w, m_ab_a_log, m_ab_dt_bias, m_ab_out_norm, m_ab_w_out, m_cd_w_in, m_cd_sinks, m_cd_w_out, v_meta_tokens, v_norm_gains, v_ffn_w_gate, v_ffn_w_up, v_ffn_w_down, v_ab_w_in, v_ab_conv_w, v_ab_a_log, v_ab_dt_bias, v_ab_out_norm, v_ab_w_out, v_cd_w_in, v_cd_sinks, v_cd_w_out):
    w = dict(meta_tokens=meta_tokens, norm_gains=norm_gains, ffn_w_gate=ffn_w_gate, ffn_w_up=ffn_w_up,
             ffn_w_down=ffn_w_down, ab_w_in=ab_w_in, ab_conv_w=ab_conv_w, ab_a_log=ab_a_log, ab_dt_bias=ab_dt_bias,
             ab_out_norm=ab_out_norm, ab_w_out=ab_w_out, cd_w_in=cd_w_in, cd_sinks=cd_sinks, cd_w_out=cd_w_out)
    m = dict(meta_tokens=m_meta_tokens, norm_gains=m_norm_gains, ffn_w_gate=m_ffn_w_gate, ffn_w_up=m_ffn_w_up,
             ffn_w_down=m_ffn_w_down, ab_w_in=m_ab_w_in, ab_conv_w=m_ab_conv_w, ab_a_log=m_ab_a_log,
             ab_dt_bias=m_ab_dt_bias, ab_out_norm=m_ab_out_norm, ab_w_out=m_ab_w_out, cd_w_in=m_cd_w_in,
             cd_sinks=m_cd_sinks, cd_w_out=m_cd_w_out)
    v = dict(meta_tokens=v_meta_tokens, norm_gains=v_norm_gains, ffn_w_gate=v_ffn_w_gate, ffn_w_up=v_ffn_w_up,
             ffn_w_down=v_ffn_w_down, ab_w_in=v_ab_w_in, ab_conv_w=v_ab_conv_w, ab_a_log=v_ab_a_log,
             ab_dt_bias=v_ab_dt_bias, ab_out_norm=v_ab_out_norm, ab_w_out=v_ab_w_out, cd_w_in=v_cd_w_in,
             cd_sinks=v_cd_sinks, cd_w_out=v_cd_w_out)
    order = list(w)
    S = x.shape[1]
    T = S + BLK

    fs = DFF // NDEV
    ffn_local = jnp.concatenate([jnp.swapaxes(ffn_w_gate, 2, 3).reshape(4 * fs, D),
                                 jnp.swapaxes(ffn_w_up, 2, 3).reshape(4 * fs, D), ffn_w_down.reshape(4 * fs, D)],
                                axis=0).astype(BF16)
    outs_local = jnp.concatenate([ab_w_out[0], cd_w_out[0]], axis=0).astype(BF16)
    ffn_all, abin_all, outs_all, cdin_all = all_gather_big(
        [ffn_local, ab_w_in[0].astype(BF16), outs_local, cd_w_in[0].astype(BF16)], name="gather_weights")
    ffn_mat = lambda k: ffn_all[:, k * fs:(k + 1) * fs].reshape(DFF, D)
    layers = [(i, j) for i in range(2) for j in range(2)]
    wg = {ij: ffn_mat(k) for k, ij in enumerate(layers)}
    wu = {ij: ffn_mat(4 + k) for k, ij in enumerate(layers)}
    wd = {ij: ffn_mat(8 + k) for k, ij in enumerate(layers)}
    ab_in = jnp.pad(_stack_shards(abin_all, 1), ((0, 0), (0, AB_INP - AB_IN)))
    ab_out = outs_all[:, :D // NDEV].reshape(D, D)
    cd_in = _pad_heads(_stack_shards(cdin_all, 1), 1)
    cd_out = _pad_heads(outs_all[:, D // NDEV:].reshape(D, D), 0)
    small_src = jnp.broadcast_to(_pad_rows8([w[n][None] for n, _, _ in SMALL_SHARDED]), (NDEV, 40, LANES))
    small_all = _unpad_rows8(all_to_all_small(small_src, name="gather_small"), [s for _, s, _ in SMALL_SHARDED])
    full = {n: _stack_shards(g, ax) for (n, _, ax), g in zip(SMALL_SHARDED, small_all)}
    conv_w = full["ab_conv_w"][0]
    gains = full["norm_gains"].reshape(2, 6, 1, D)
    tables = retention_tables(T)

    h = jnp.concatenate([jnp.zeros((PAD, D), F32), full["meta_tokens"], x[0]], axis=0)
    h, s00 = ffn_fwd(h, gains[0, 0], gains[0, 1], wg[0, 0], wu[0, 0], wd[0, 0], "00")
    h, sab = mixer_ab_fwd(h, gains[0, 2], gains[0, 3], ab_in, conv_w, ab_a_log, ab_dt_bias, ab_out_norm, ab_out, tables)
    h, s01 = ffn_fwd(h, gains[0, 4], gains[0, 5], wg[0, 1], wu[0, 1], wd[0, 1], "01")
    h, s10 = ffn_fwd(h, gains[1, 0], gains[1, 1], wg[1, 0], wu[1, 0], wd[1, 0], "10")
    h, scd = mixer_cd_fwd(h, gains[1, 2], gains[1, 3], cd_in, cd_sinks, cd_out)
    h, s11 = ffn_fwd(h, gains[1, 4], gains[1, 5], wg[1, 1], wu[1, 1], wd[1, 1], "11")
    loss_tile, dh = loss_and_grad(h, loss_target[0], name="loss")
    loss = lax.psum(loss_tile[0, 0], ("x", "y", "c"))

    dgain = [[None] * 6, [None] * 6]
    dffn = {}
    dh, (dgain[1][4], dgain[1][5]), dffn[1, 1] = ffn_bwd(s11, dh, gains[1, 4], gains[1, 5], wg[1, 1], wu[1, 1], wd[1, 1], "11")
    dh, (dgain[1][2], dgain[1][3]), (dcd_in, dcd_out), dsinks = mixer_cd_bwd(scd, dh, gains[1, 2], gains[1, 3], cd_in, cd_sinks, cd_out)
    dh, (dgain[1][0], dgain[1][1]), dffn[1, 0] = ffn_bwd(s10, dh, gains[1, 0], gains[1, 1], wg[1, 0], wu[1, 0], wd[1, 0], "10")
    dh, (dgain[0][4], dgain[0][5]), dffn[0, 1] = ffn_bwd(s01, dh, gains[0, 4], gains[0, 5], wg[0, 1], wu[0, 1], wd[0, 1], "01")
    dh, (dgain[0][2], dgain[0][3]), (dab_in, dab_out), (dconv, dalog, ddtb, donorm) = mixer_ab_bwd(
        sab, dh, gains[0, 2], gains[0, 3], ab_in, conv_w, ab_a_log, ab_dt_bias, ab_out_norm, ab_out, tables)
    dh, (dgain[0][0], dgain[0][1]), dffn[0, 0] = ffn_bwd(s00, dh, gains[0, 0], gains[0, 1], wg[0, 0], wu[0, 0], wd[0, 0], "00")
    grad_x = dh[BLK:][None]

    gfull = dict(meta_tokens=dh[PAD:BLK], norm_gains=jnp.stack([jnp.concatenate(r, axis=0) for r in dgain]),
                 ab_conv_w=dconv[None])
    ffn_send = jnp.concatenate([dffn[ij][k].astype(BF16).reshape(NDEV, fs, D) for k in range(3) for ij in layers], axis=1)
    outs_send = jnp.concatenate([dab_out.astype(BF16).reshape(NDEV, D // NDEV, D),
                                 _unpad_heads(dcd_out, 0).astype(BF16).reshape(NDEV, D // NDEV, D)], axis=1)
    abin_send = _split_shards(dab_in[:, :AB_IN].astype(BF16), 1)
    cdin_send = _split_shards(_unpad_heads(dcd_in, 1).astype(BF16), 1)
    ffn_g, abin_g, outs_g, cdin_g = reduce_scatter_big([ffn_send, abin_send, outs_send, cdin_send])
    ffn_g = ffn_g.reshape(3, 2, 2, fs, D)
    grads = dict(ffn_w_gate=jnp.swapaxes(ffn_g[0], 2, 3), ffn_w_up=jnp.swapaxes(ffn_g[1], 2, 3), ffn_w_down=ffn_g[2],
                 ab_w_in=abin_g[None], ab_w_out=outs_g[None, :D // NDEV], cd_w_in=cdin_g[None],
                 cd_w_out=outs_g[None, D // NDEV:])
    repl = [jnp.broadcast_to(t[None], (NDEV,) + t.shape) for t in (dalog, ddtb, donorm, dsinks)]
    ssend = _pad_rows8([_split_shards(gfull[n], ax) for n, _, ax in SMALL_SHARDED] + repl)
    ssum = sum_slots(all_to_all_small(ssend, name="exchange_small_grads"), name="sum_small_grads")[None]
    small = _unpad_rows8(ssum, [s for _, s, _ in SMALL_SHARDED] + [s for _, s in SMALL_REPL])
    grads.update({n: g[0] for n, g in zip([n for n, _, _ in SMALL_SHARDED] + [n for n, _ in SMALL_REPL], small)})

    delta, new_m, new_v = {}, {}, {}
    for n in order:
        shape = w[n].shape
        view = (-1, shape[-1])
        d_, m_, v_ = adamw(w[n].reshape(view), grads[n].reshape(view), m[n].reshape(view), v[n].reshape(view),
                           name=f"adamw_{n}")
        delta[n], new_m[n], new_v[n] = d_.reshape(shape), m_.reshape(shape), v_.reshape(shape)
    return (loss, grad_x, *[grads[n] for n in order], *[delta[n] for n in order], *[new_m[n] for n in order],
            *[new_v[n] for n in order])
```

```python
import functools
import math

import numpy as np
import jax
import jax.numpy as jnp
from jax import lax
from jax.experimental import pallas as pl
from jax.experimental.pallas import tpu as pltpu

F32, BF16 = jnp.float32, jnp.bfloat16
EPS = 1e-6
D = 1024
NMETA = 16
BLK = 128
PAD = BLK - NMETA
DFF = 2816
LANES = 128
NDEV = 8
AB_IN, AB_INP = 4104, 4224
ADAM_LR, ADAM_B1, ADAM_B2, ADAM_EPS, ADAM_WD, ADAM_STEP = 0.001, 0.9, 0.999, 1e-08, 0.01, 10
VMEM_LIMIT = 56 * 1024 * 1024
MESH = pl.DeviceIdType.MESH
HIGHEST = lax.Precision.HIGHEST


def _params(sem):
    return pltpu.CompilerParams(dimension_semantics=sem, vmem_limit_bytes=VMEM_LIMIT)


def _row_tile(T, streamed, resident):
    for tm in (640, 320, 128):
        if T % tm == 0 and 2 * (tm * streamed + resident) <= VMEM_LIMIT - 14 * 1024 * 1024:
            return tm
    return _tile(T, 128)


def _tile(n, cap):
    if n <= cap:
        return n
    best = None
    for t in range(LANES, cap + 1, LANES):
        if n % t == 0:
            best = t
    assert best is not None, (n, cap)
    return best


def _rms_fwd(x, g):
    return x * lax.rsqrt(jnp.mean(x * x, axis=-1, keepdims=True) + EPS) * g


def _rms_bwd(x, g, dz):
    r = lax.rsqrt(jnp.mean(x * x, axis=-1, keepdims=True) + EPS)
    xh = x * r
    dg = jnp.sum(dz * xh, axis=0, keepdims=True)
    t = dz * g
    return r * (t - xh * jnp.mean(t * xh, axis=-1, keepdims=True)), dg


def _sigmoid(x):
    return 1.0 / (1.0 + jnp.exp(-x))


def _silu(x):
    return x * _sigmoid(x)


def _nn(a, b, precision=None):
    return lax.dot_general(a, b, (((1,), (0,)), ((), ())), preferred_element_type=F32, precision=precision)


def _nt(a, b):
    return lax.dot_general(a, b, (((1,), (1,)), ((), ())), preferred_element_type=F32)


def _tn(a, b):
    return lax.dot_general(a, b, (((0,), (0,)), ((), ())), preferred_element_type=F32)


def _mm(a, b, precision=None):
    if a.ndim == 3:
        return lax.dot_general(a, b, (((2,), (1,)), ((0,), (0,))), preferred_element_type=F32, precision=precision)
    return _nn(a, b, precision)


def _t(x):
    return jnp.swapaxes(x, -1, -2)


@jax.custom_vjp
def bdot(a, b):
    return _mm(a.astype(BF16), b.astype(BF16))


def _bdot_fwd(a, b):
    return bdot(a, b), (a, b)


def _bdot_bwd(res, g):
    a, b = res
    return bdot(g, _t(b)), bdot(_t(a), g)


bdot.defvjp(_bdot_fwd, _bdot_bwd)


@jax.custom_vjp
def hdot(a, b):
    return _mm(a, b, HIGHEST)


def _hdot_fwd(a, b):
    return hdot(a, b), (a, b)


def _hdot_bwd(res, g):
    a, b = res
    return hdot(g, _t(b)), hdot(_t(a), g)


hdot.defvjp(_hdot_fwd, _hdot_bwd)


def _iota2(shape, axis):
    return lax.broadcasted_iota(jnp.int32, shape, axis)


def _lane_pick(row, lane):
    return jnp.sum(jnp.where(_iota2(row.shape, 1) == lane, row, 0.0), axis=1, keepdims=True)


def norm_mm(h, gain, ws, *, swiglu, name, wt=False, out_dtype=F32):
    T, Dm = h.shape
    N = ws[0].shape[0 if wt else 1]
    tm, tn = _tile(T, 640), _tile(N, 1408)
    nw = len(ws)
    mm = _nt if wt else _nn

    def body(h_ref, g_ref, *refs):
        w_refs, u_ref, o_refs = refs[:nw], refs[nw], refs[nw + 1:]

        @pl.when(pl.program_id(1) == 0)
        def _():
            u_ref[...] = _rms_fwd(h_ref[...], g_ref[...]).astype(BF16)

        u = u_ref[...]
        acc = [mm(u, w[...]) for w in w_refs]
        if swiglu:
            o_refs[0][...] = acc[0].astype(BF16)
            o_refs[1][...] = acc[1].astype(BF16)
            o_refs[2][...] = (_silu(acc[0]) * acc[1]).astype(BF16)
        else:
            o_refs[0][...] = acc[0].astype(out_dtype)

    row = pl.BlockSpec((tm, Dm), lambda i, j: (i, 0))
    tile = pl.BlockSpec((tm, tn), lambda i, j: (i, j))
    if swiglu:
        out_shape = [jax.ShapeDtypeStruct((T, Dm), BF16)] + [jax.ShapeDtypeStruct((T, N), BF16)] * 3
        out_specs = [row, tile, tile, tile]
    else:
        out_shape = [jax.ShapeDtypeStruct((T, Dm), BF16), jax.ShapeDtypeStruct((T, N), out_dtype)]
        out_specs = [row, tile]
    return pl.pallas_call(
        body, name=name, grid=(T // tm, N // tn),
        in_specs=[row, pl.BlockSpec((1, Dm), lambda i, j: (0, 0))]
        + [pl.BlockSpec((tn, Dm), lambda i, j: (j, 0)) if wt else pl.BlockSpec((Dm, tn), lambda i, j: (0, j))] * nw,
        out_specs=out_specs, out_shape=out_shape,
        compiler_params=_params(("arbitrary", "arbitrary")),
    )(h, gain, *ws)


def mm_norm_res(As, Ws, h, gain, scale, *, name):
    T, Dm = h.shape
    n = len(As)
    tm = _row_tile(T, sum(a.shape[1] * a.dtype.itemsize for a in As) + 3 * Dm * 4,
                   sum(w.size * w.dtype.itemsize for w in Ws))

    def body(*refs):
        a_refs, w_refs = refs[:n], refs[n:2 * n]
        h_ref, g_ref, y_ref, hn_ref = refs[2 * n:]
        y = _nn(a_refs[0][...].astype(BF16), w_refs[0][...])
        for a, w in zip(a_refs[1:], w_refs[1:]):
            y = y + _nn(a[...].astype(BF16), w[...])
        y_ref[...] = y
        hn_ref[...] = h_ref[...] + scale * _rms_fwd(y, g_ref[...])

    row = pl.BlockSpec((tm, Dm), lambda i: (i, 0))
    return pl.pallas_call(
        body, name=name, grid=(T // tm,),
        in_specs=[pl.BlockSpec((tm, a.shape[1]), lambda i: (i, 0)) for a in As]
        + [pl.BlockSpec(w.shape, lambda i: (0, 0)) for w in Ws]
        + [row, pl.BlockSpec((1, Dm), lambda i: (0, 0))],
        out_specs=[row, row], out_shape=[jax.ShapeDtypeStruct((T, Dm), F32)] * 2,
        compiler_params=_params(("arbitrary",)),
    )(*As, *Ws, h, gain)


def normbwd_mm_nt(dh, y, gain, w, scale, gu=None, *, name):
    T, Dm = dh.shape
    N = w.shape[0]
    tm, tn = _tile(T, 640), _tile(N, 1408)
    swiglu = gu is not None

    def body(dh_ref, y_ref, g_ref, w_ref, *refs):
        if swiglu:
            gate_ref, up_ref, dy_ref, dg_ref, dgate_ref, dup_ref, a_ref = refs
        else:
            dy_ref, dg_ref, da_ref = refs
        i, j = pl.program_id(0), pl.program_id(1)

        @pl.when(j == 0)
        def _():
            dy, dg = _rms_bwd(y_ref[...], g_ref[...], scale * dh_ref[...])
            dy_ref[...] = dy.astype(BF16)

            @pl.when(i == 0)
            def _():
                dg_ref[...] = jnp.zeros_like(dg_ref)

            dg_ref[...] += dg

        da = _nt(dy_ref[...], w_ref[...])
        if swiglu:
            gate, up = gate_ref[...].astype(F32), up_ref[...].astype(F32)
            s = _sigmoid(gate)
            dgate_ref[...] = (da * up * s * (1.0 + gate * (1.0 - s))).astype(BF16)
            dup_ref[...] = (da * gate * s).astype(BF16)
            a_ref[...] = (gate * s * up).astype(BF16)
        else:
            da_ref[...] = da

    row = pl.BlockSpec((tm, Dm), lambda i, j: (i, 0))
    vec = pl.BlockSpec((1, Dm), lambda i, j: (0, 0))
    tile = pl.BlockSpec((tm, tn), lambda i, j: (i, j))
    in_specs = [row, row, vec, pl.BlockSpec((tn, Dm), lambda i, j: (j, 0))]
    out_shape = [jax.ShapeDtypeStruct((T, Dm), BF16), jax.ShapeDtypeStruct((1, Dm), F32)]
    if swiglu:
        in_specs += [tile, tile]
        out_shape += [jax.ShapeDtypeStruct((T, N), BF16)] * 3
        out_specs = [row, vec, tile, tile, tile]
        args = (dh, y, gain, w, *gu)
    else:
        out_shape += [jax.ShapeDtypeStruct((T, N), F32)]
        out_specs = [row, vec, tile]
        args = (dh, y, gain, w)
    return pl.pallas_call(
        body, name=name, grid=(T // tm, N // tn), in_specs=in_specs, out_specs=out_specs,
        out_shape=out_shape, compiler_params=_params(("arbitrary", "arbitrary")),
    )(*args)


def mm_nt_normbwd(dPs, Ws, h, gain, dh_in, *, name, wt=False):
    T, Dm = h.shape
    n = len(dPs)
    tm = _row_tile(T, sum(p.shape[1] * p.dtype.itemsize for p in dPs) + 3 * Dm * 4,
                   sum(w.size * w.dtype.itemsize for w in Ws))
    mm = _nn if wt else _nt

    def body(*refs):
        p_refs, w_refs = refs[:n], refs[n:2 * n]
        h_ref, g_ref, dhin_ref, dh_ref, dg_ref = refs[2 * n:]
        du = mm(p_refs[0][...].astype(BF16), w_refs[0][...])
        for p, w in zip(p_refs[1:], w_refs[1:]):
            du = du + mm(p[...].astype(BF16), w[...])
        dx, dg = _rms_bwd(h_ref[...], g_ref[...], du)
        dh_ref[...] = dhin_ref[...] + dx

        @pl.when(pl.program_id(0) == 0)
        def _():
            dg_ref[...] = jnp.zeros_like(dg_ref)

        dg_ref[...] += dg

    row = pl.BlockSpec((tm, Dm), lambda i: (i, 0))
    vec = pl.BlockSpec((1, Dm), lambda i: (0, 0))
    return pl.pallas_call(
        body, name=name, grid=(T // tm,),
        in_specs=[pl.BlockSpec((tm, p.shape[1]), lambda i: (i, 0)) for p in dPs]
        + [pl.BlockSpec(w.shape, lambda i: (0, 0)) for w in Ws] + [row, vec, row],
        out_specs=[row, vec],
        out_shape=[jax.ShapeDtypeStruct((T, Dm), F32), jax.ShapeDtypeStruct((1, Dm), F32)],
        compiler_params=_params(("arbitrary",)),
    )(*dPs, *Ws, h, gain, dh_in)


def mm_tn(a, b, *, name):
    T, M = a.shape
    N = b.shape[1]
    tm, tn, tk = _tile(M, 1408), _tile(N, 1408), _tile(T, 640)

    def body(a_ref, b_ref, o_ref):
        @pl.when(pl.program_id(2) == 0)
        def _():
            o_ref[...] = jnp.zeros_like(o_ref)

        o_ref[...] += _tn(a_ref[...].astype(BF16), b_ref[...].astype(BF16))

    return pl.pallas_call(
        body, name=name, grid=(M // tm, N // tn, T // tk),
        in_specs=[pl.BlockSpec((tk, tm), lambda i, j, k: (k, i)), pl.BlockSpec((tk, tn), lambda i, j, k: (k, j))],
        out_specs=pl.BlockSpec((tm, tn), lambda i, j, k: (i, j)),
        out_shape=jax.ShapeDtypeStruct((M, N), F32),
        compiler_params=_params(("arbitrary", "arbitrary", "arbitrary")),
    )(a, b)


def loss_and_grad(h, target, *, name):
    T, Dm = h.shape

    def body(h_ref, t_ref, loss_ref, dh_ref):
        b = pl.program_id(0)

        @pl.when(b == 0)
        def _():
            loss_ref[...] = jnp.zeros_like(loss_ref)
            dh_ref[...] = jnp.zeros_like(dh_ref)

        @pl.when(b > 0)
        def _():
            e = h_ref[...] - t_ref[...]
            dh_ref[...] = e * (1.0 / Dm)
            loss_ref[...] += jnp.sum(e * e) * (0.5 / Dm)

    return pl.pallas_call(
        body, name=name, grid=(T // BLK,),
        in_specs=[pl.BlockSpec((BLK, Dm), lambda b: (b, 0)),
                  pl.BlockSpec((BLK, Dm), lambda b: (jnp.maximum(b - 1, 0), 0))],
        out_specs=[pl.BlockSpec((8, LANES), lambda b: (0, 0)), pl.BlockSpec((BLK, Dm), lambda b: (b, 0))],
        out_shape=[jax.ShapeDtypeStruct((8, LANES), F32), jax.ShapeDtypeStruct((T, Dm), F32)],
        compiler_params=_params(("arbitrary",)),
    )(h, target)


def adamw(w, g, m, v, *, name):
    R, C = w.shape
    tr = R
    for t in (512, 352, 256):
        if R > t and R % t == 0:
            tr = t
            break

    def body(w_ref, g_ref, m_ref, v_ref, d_ref, nm_ref, nv_ref):
        g_ = g_ref[...]
        m_ = ADAM_B1 * m_ref[...] + (1.0 - ADAM_B1) * g_
        v_ = ADAM_B2 * v_ref[...] + (1.0 - ADAM_B2) * (g_ * g_)
        m_hat = m_ / (1.0 - ADAM_B1 ** ADAM_STEP)
        v_hat = v_ / (1.0 - ADAM_B2 ** ADAM_STEP)
        d_ref[...] = -ADAM_LR * (m_hat / (jnp.sqrt(v_hat) + ADAM_EPS) + ADAM_WD * w_ref[...])
        nm_ref[...] = m_
        nv_ref[...] = v_

    spec = pl.BlockSpec((tr, C), lambda i: (i, 0))
    return pl.pallas_call(
        body, name=name, grid=(R // tr,), in_specs=[spec] * 4, out_specs=[spec] * 3,
        out_shape=[jax.ShapeDtypeStruct((R, C), F32)] * 3, compiler_params=_params(("arbitrary",)),
    )(w, g, m, v)


def _me():
    return lax.axis_index("x"), lax.axis_index("y"), lax.axis_index("c")


def _flip(pos, rel):
    return tuple(1 - p if r else p for p, r in zip(pos, rel))


def _slot(pos):
    return 4 * pos[0] + 2 * pos[1] + pos[2]


HBM_SPEC = pl.BlockSpec(memory_space=pltpu.HBM)
CHIP_RELS = ((1, 0), (0, 1), (1, 1))


def all_gather_big(xs, *, name):
    n = len(xs)

    def body(*refs):
        x_refs, out_refs = refs[:n], refs[n:2 * n]
        send_sems, recv_sems, local_sems = refs[2 * n:]
        me = _me()
        sibling = _flip(me, (0, 0, 1))
        chips = [_flip(me, rel + (0,)) for rel in CHIP_RELS]

        def copy(i, k, block, to, src=None):
            dst = out_refs[i].at[_slot(block)]
            return pltpu.make_async_remote_copy(
                src_ref=dst if src is None else src, dst_ref=dst, send_sem=send_sems.at[i, k],
                recv_sem=recv_sems.at[i, k], device_id=to, device_id_type=MESH)

        sent, local = [], []
        for i in range(n):
            mine = pltpu.make_async_copy(x_refs[i], out_refs[i].at[_slot(me)], local_sems.at[i])
            mine.start()
            local.append(mine)
            sent += [copy(i, 0, me, sibling, src=x_refs[i])]
            sent += [copy(i, 1 + j, me, chip, src=x_refs[i]) for j, chip in enumerate(chips)]
        for cp in sent:
            cp.start()
        for i in range(n):
            for j, chip in enumerate(chips):
                copy(i, 1 + j, chip, me).wait_recv()
                passed = copy(i, 4 + j, chip, sibling)
                passed.start()
                sent.append(passed)
        for i in range(n):
            copy(i, 0, sibling, me).wait_recv()
            for j, chip in enumerate(chips):
                copy(i, 4 + j, _flip(chip, (0, 0, 1)), me).wait_recv()
        for cp in sent:
            cp.wait_send()
        for mine in local:
            mine.wait()

    return pl.pallas_call(
        body, name=name, in_specs=[HBM_SPEC] * n, out_specs=[HBM_SPEC] * n,
        out_shape=[jax.ShapeDtypeStruct((NDEV,) + x.shape, x.dtype) for x in xs],
        scratch_shapes=[pltpu.SemaphoreType.DMA((n, 7)), pltpu.SemaphoreType.DMA((n, 7)), pltpu.SemaphoreType.DMA((n,))],
    )(*xs)


def all_to_all_small(src, *, name):
    _, r, C = src.shape

    def body(src_ref, out_ref, send_sems, recv_sems):
        me = _me()
        my = _slot(me)
        out_ref[my] = src_ref[my]
        copies = []
        for k in range(1, NDEV):
            peer = _flip(me, ((k >> 2) & 1, (k >> 1) & 1, k & 1))
            cp = pltpu.make_async_remote_copy(
                src_ref=src_ref.at[_slot(peer)], dst_ref=out_ref.at[my], send_sem=send_sems.at[k - 1],
                recv_sem=recv_sems.at[k - 1], device_id=peer, device_id_type=MESH)
            cp.start()
            copies.append((cp, peer))
        for k, (cp, peer) in enumerate(copies):
            pltpu.make_async_remote_copy(
                src_ref=src_ref.at[my], dst_ref=out_ref.at[_slot(peer)], send_sem=send_sems.at[k],
                recv_sem=recv_sems.at[k], device_id=peer, device_id_type=MESH).wait_recv()
        for cp, _ in copies:
            cp.wait_send()

    vm = pl.BlockSpec(memory_space=pltpu.VMEM)
    return pl.pallas_call(
        body, name=name, in_specs=[vm], out_specs=vm, out_shape=jax.ShapeDtypeStruct(src.shape, src.dtype),
        scratch_shapes=[pltpu.SemaphoreType.DMA((7,)), pltpu.SemaphoreType.DMA((7,))],
    )(src)


def sum_slots(a, *, name):
    n, r, C = a.shape

    def body(a_ref, o_ref):
        s = a_ref[0]
        for k in range(1, n):
            s = s + a_ref[k]
        o_ref[...] = s

    vm = pl.BlockSpec(memory_space=pltpu.VMEM)
    return pl.pallas_call(body, name=name, in_specs=[vm], out_specs=vm,
                          out_shape=jax.ShapeDtypeStruct((r, C), F32))(a)


def rs_exchange_sibling(gs, *, name):
    n = len(gs)

    def body(*refs):
        g_refs, out_refs, send_sems, recv_sems = refs[:n], refs[n:2 * n], refs[2 * n], refs[2 * n + 1]
        sibling = _flip(_me(), (0, 0, 1))
        copies = []
        for i in range(n):
            for chip in range(4):
                cp = pltpu.make_async_remote_copy(
                    src_ref=g_refs[i].at[2 * chip + sibling[2]], dst_ref=out_refs[i].at[chip],
                    send_sem=send_sems.at[i, chip], recv_sem=recv_sems.at[i, chip], device_id=sibling,
                    device_id_type=MESH)
                cp.start()
                copies.append(cp)
        for cp in copies:
            cp.wait()

    return pl.pallas_call(
        body, name=name, in_specs=[HBM_SPEC] * n, out_specs=[HBM_SPEC] * n,
        out_shape=[jax.ShapeDtypeStruct((4,) + g.shape[1:], g.dtype) for g in gs],
        scratch_shapes=[pltpu.SemaphoreType.DMA((n, 4)), pltpu.SemaphoreType.DMA((n, 4))],
    )(*gs)


def rs_chip_partials(g, got, *, name):
    _, R, C = g.shape
    tr = _tile(R, 768)

    def body(c_ref, g_ref, got_ref, o_ref):
        o_ref[...] = (g_ref[...].astype(F32) + got_ref[...].astype(F32)).astype(o_ref.dtype)

    c = jnp.reshape(lax.axis_index("c"), (1,)).astype(jnp.int32)
    return pl.pallas_call(
        body, name=name,
        grid_spec=pltpu.PrefetchScalarGridSpec(
            num_scalar_prefetch=1, grid=(4, R // tr),
            in_specs=[pl.BlockSpec((None, tr, C), lambda k, i, c_ref: (2 * k + c_ref[0], i, 0)),
                      pl.BlockSpec((None, tr, C), lambda k, i, c_ref: (k, i, 0))],
            out_specs=pl.BlockSpec((None, tr, C), lambda k, i, c_ref: (k, i, 0))),
        out_shape=jax.ShapeDtypeStruct((4, R, C), g.dtype), compiler_params=_params(("arbitrary", "arbitrary")),
    )(c, g, got)


def rs_exchange_chips(ps, *, name):
    n = len(ps)

    def body(*refs):
        p_refs, out_refs, send_sems, recv_sems = refs[:n], refs[n:2 * n], refs[2 * n], refs[2 * n + 1]
        me = _me()
        copies = []
        for i in range(n):
            for j, rel in enumerate(CHIP_RELS):
                peer = _flip(me, rel + (0,))
                cp = pltpu.make_async_remote_copy(
                    src_ref=p_refs[i].at[2 * peer[0] + peer[1]], dst_ref=out_refs[i].at[j], send_sem=send_sems.at[i, j],
                    recv_sem=recv_sems.at[i, j], device_id=peer, device_id_type=MESH)
                cp.start()
                copies.append(cp)
        for cp in copies:
            cp.wait()

    return pl.pallas_call(
        body, name=name, in_specs=[HBM_SPEC] * n, out_specs=[HBM_SPEC] * n,
        out_shape=[jax.ShapeDtypeStruct((3,) + p.shape[1:], p.dtype) for p in ps],
        scratch_shapes=[pltpu.SemaphoreType.DMA((n, 3)), pltpu.SemaphoreType.DMA((n, 3))],
    )(*ps)


def rs_final_sum(p, got, *, name):
    _, R, C = p.shape
    tr = _tile(R, 768)

    def body(chip_ref, p_ref, got_ref, o_ref):
        s = p_ref[...].astype(F32)
        for j in range(3):
            s = s + got_ref[j].astype(F32)
        o_ref[...] = s

    mychip = jnp.reshape(2 * lax.axis_index("x") + lax.axis_index("y"), (1,)).astype(jnp.int32)
    return pl.pallas_call(
        body, name=name,
        grid_spec=pltpu.PrefetchScalarGridSpec(
            num_scalar_prefetch=1, grid=(R // tr,),
            in_specs=[pl.BlockSpec((None, tr, C), lambda i, chip_ref: (chip_ref[0], i, 0)),
                      pl.BlockSpec((3, tr, C), lambda i, chip_ref: (0, i, 0))],
            out_specs=pl.BlockSpec((tr, C), lambda i, chip_ref: (i, 0))),
        out_shape=jax.ShapeDtypeStruct((R, C), F32), compiler_params=_params(("arbitrary",)),
    )(mychip, p, got)


def reduce_scatter_big(gs):
    got = rs_exchange_sibling(gs, name="rs_sibling")
    parts = [rs_chip_partials(g, t, name=f"rs_chip_partials_{i}") for i, (g, t) in enumerate(zip(gs, got))]
    got2 = rs_exchange_chips(parts, name="rs_chips")
    return [rs_final_sum(p, t, name=f"rs_final_sum_{i}") for i, (p, t) in enumerate(zip(parts, got2))]


def _blk(off):
    return pl.BlockSpec((BLK, LANES), lambda h, n: (n, off + h))


def _const_spec(shape):
    return pl.BlockSpec(shape, lambda *_: (0,) * len(shape))


def retention_tables(T):
    pos = jnp.arange(T, dtype=F32) - float(PAD)
    inv_freq = 1.0 / (10000.0 ** jnp.linspace(0.0, 1.0, 64, dtype=F32))
    ang = pos[:, None] * inv_freq[None, :]
    cos = jnp.repeat(jnp.cos(ang), 2, axis=1)
    sin = jnp.repeat(jnp.sin(ang), 2, axis=1) * jnp.tile(jnp.array([-1.0, 1.0], F32), 64)[None, :]
    lane = np.arange(LANES)
    perm = jnp.broadcast_to(jnp.asarray((lane[:, None] == (lane[None, :] ^ 1)).astype(np.float32)), (4, LANES, LANES))
    log_gamma = jnp.log1p(-jnp.exp2(-5.0 - jnp.arange(4, dtype=F32)))
    idx = jnp.arange(BLK, dtype=F32)
    diff = idx[:, None] - idx[None, :]
    intra = jnp.where(diff >= 0, jnp.exp(jnp.maximum(diff, 0.0) * log_gamma[:, None, None]), 0.0)
    zeta = jnp.exp((BLK - 1.0 - idx)[None, :] * log_gamma[:, None])
    xi = jnp.exp((idx + 1.0)[None, :] * log_gamma[:, None])
    bc = lambda t: jnp.broadcast_to(t[:, :, None], (4, BLK, LANES))
    return cos, sin, perm, intra, bc(zeta), bc(xi)


def _heads(x):
    return jnp.stack([x[:, h * LANES:(h + 1) * LANES] for h in range(4)])


def _unheads(y):
    return jnp.concatenate([y[h] for h in range(4)], axis=1)


def _ret_chunk(rq, rk, rv, rg, S, cos, sin, intra, zeta, xi, perm):
    q = rq * cos + hdot(rq, perm) * sin
    k = (rk * cos + hdot(rk, perm) * sin) * (128.0 ** -0.5)
    ret = bdot(bdot(q, _t(k)) * intra, rv) + bdot(q * xi, S)
    S_new = S * xi[..., BLK - 1:BLK, :] + bdot(_t(k * zeta), rv)
    c = ret - jnp.mean(ret, axis=-1, keepdims=True)
    out = c * lax.rsqrt(jnp.mean(c * c, axis=-1, keepdims=True) + EPS) * _silu(rg)
    return out, S_new


def _wide(off):
    return pl.BlockSpec((BLK, 4 * LANES), lambda n: (n, off))


def retention_fwd(p, tables, *, name):
    T = p.shape[0]
    N = T // BLK
    cos, sin, perm, intra, zeta, xi = tables

    def body(rq, rk, rv, rg, cos_ref, sin_ref, in_ref, ze_ref, xi_ref, perm_ref, out_ref, sall_ref, s_scr):
        @pl.when(pl.program_id(0) == 0)
        def _():
            s_scr[...] = jnp.zeros_like(s_scr)

        S = s_scr[...]
        sall_ref[...] = S
        out, S_new = _ret_chunk(_heads(rq[...]), _heads(rk[...]), _heads(rv[...]), _heads(rg[...]), S, cos_ref[...],
                                sin_ref[...], in_ref[...], ze_ref[...], xi_ref[...], perm_ref[...])
        out_ref[...] = _unheads(out).astype(BF16)
        s_scr[...] = S_new

    rowtab = pl.BlockSpec((BLK, LANES), lambda n: (n, 0))
    tab = _const_spec((4, BLK, LANES))
    return pl.pallas_call(
        body, name=name, grid=(N,),
        in_specs=[_wide(0), _wide(1), _wide(2), _wide(3), rowtab, rowtab, tab, tab, tab, tab],
        out_specs=[_wide(0), pl.BlockSpec((None, 4, LANES, LANES), lambda n: (n, 0, 0, 0))],
        out_shape=[jax.ShapeDtypeStruct((T, 512), BF16), jax.ShapeDtypeStruct((N, 4, LANES, LANES), F32)],
        scratch_shapes=[pltpu.VMEM((4, LANES, LANES), F32)],
        compiler_params=_params(("arbitrary",)),
    )(p, p, p, p, cos, sin, intra, zeta, xi, perm)


def _row_mask(n):
    return (n * BLK + _iota2((BLK, 1), 0) >= PAD).astype(F32)


def retention_bwd(p, sall, dmixed, tables, *, name):
    T = p.shape[0]
    N = T // BLK
    cos, sin, perm, intra, zeta, xi = tables

    def body(rq, rk, rv, rg, cos_ref, sin_ref, in_ref, ze_ref, xi_ref, perm_ref, sall_ref, do_ref, drq, drk, drv, drg,
             ds_scr):
        n = N - 1 - pl.program_id(0)

        @pl.when(pl.program_id(0) == 0)
        def _():
            ds_scr[...] = jnp.zeros_like(ds_scr)

        f = lambda a, b, c, d, s: _ret_chunk(a, b, c, d, s, cos_ref[...], sin_ref[...], in_ref[...], ze_ref[...],
                                             xi_ref[...], perm_ref[...])
        _, vjp = jax.vjp(f, _heads(rq[...]), _heads(rk[...]), _heads(rv[...]), _heads(rg[...]), sall_ref[...])
        g = vjp((_heads(do_ref[...]), ds_scr[...]))
        mask = _row_mask(n)
        for ref, val in zip((drq, drk, drv, drg), g[:4]):
            ref[...] = _unheads(val) * mask
        ds_scr[...] = g[4]

    def rwide(off):
        return pl.BlockSpec((BLK, 4 * LANES), lambda n: (N - 1 - n, off))

    rowtab = pl.BlockSpec((BLK, LANES), lambda n: (N - 1 - n, 0))
    tab = _const_spec((4, BLK, LANES))
    return pl.pallas_call(
        body, name=name, grid=(N,),
        in_specs=[rwide(0), rwide(1), rwide(2), rwide(3), rowtab, rowtab, tab, tab, tab, tab,
                  pl.BlockSpec((None, 4, LANES, LANES), lambda n: (N - 1 - n, 0, 0, 0)), rwide(0)],
        out_specs=[rwide(0)] * 4, out_shape=[jax.ShapeDtypeStruct((T, 512), F32)] * 4,
        scratch_shapes=[pltpu.VMEM((4, LANES, LANES), F32)],
        compiler_params=_params(("arbitrary",)),
    )(p, p, p, p, cos, sin, intra, zeta, xi, perm, sall, dmixed)


def conv_silu_fwd(p, w, *, name):
    T = p.shape[0]
    N = T // BLK

    def body(x_ref, xp_ref, w_ref, o_ref):
        n = pl.program_id(0)
        cur = x_ref[...]
        cat = jnp.concatenate([jnp.where(n > 0, xp_ref[...], 0.0), cur], axis=0)
        y = w_ref[3:4, :] * cur
        for s in (1, 2, 3):
            y = y + w_ref[3 - s:4 - s, :] * pltpu.roll(cat, s, 0)[BLK:]
        o_ref[...] = _silu(y)

    cw = 4 * LANES
    return pl.pallas_call(
        body, name=name, grid=(N, 3),
        in_specs=[pl.BlockSpec((BLK, cw), lambda n, c: (n, 4 + c)),
                  pl.BlockSpec((BLK, cw), lambda n, c: (jnp.maximum(n - 1, 0), 4 + c)),
                  pl.BlockSpec((4, cw), lambda n, c: (0, c))],
        out_specs=pl.BlockSpec((BLK, cw), lambda n, c: (n, c)),
        out_shape=jax.ShapeDtypeStruct((T, 1536), F32), compiler_params=_params(("arbitrary", "arbitrary")),
    )(p, p, w)


def conv_silu_bwd(p, w, dact, part, *, name):
    T = p.shape[0]
    N = T // BLK
    cw = 4 * LANES

    def body(xp_ref, x_ref, xn_ref, w_ref, da_ref, dan_ref, dx_ref, dw_ref):
        n = pl.program_id(0)
        last = n == N - 1
        cat = jnp.concatenate([jnp.where(n > 0, xp_ref[...], 0.0), x_ref[...], jnp.where(last, 0.0, xn_ref[...])], axis=0)
        shifted = [cat] + [pltpu.roll(cat, s, 0) for s in (1, 2, 3)]
        y = w_ref[3:4, :] * shifted[0]
        for s in (1, 2, 3):
            y = y + w_ref[3 - s:4 - s, :] * shifted[s]
        y = y[BLK:]
        da = jnp.concatenate([da_ref[...], jnp.where(last, 0.0, dan_ref[...])], axis=0)
        sg = _sigmoid(y)
        dy = da * sg * (1.0 + y * (1.0 - sg))
        dx = w_ref[3:4, :] * dy[:BLK]
        for s in (1, 2, 3):
            dx = dx + w_ref[3 - s:4 - s, :] * pltpu.roll(dy, 2 * BLK - s, 0)[:BLK]
        dx_ref[...] = dx * _row_mask(n)

        @pl.when(n == 0)
        def _():
            dw_ref[...] = jnp.zeros_like(dw_ref)

        for s in (0, 1, 2, 3):
            dw_ref[3 - s:4 - s, :] += jnp.sum(dy[:BLK] * shifted[s][BLK:2 * BLK], axis=0, keepdims=True)

    def xs(d):
        return pl.BlockSpec((BLK, cw), lambda n: (jnp.clip(n + d, 0, N - 1), 4 + part))

    return pl.pallas_call(
        body, name=name, grid=(N,),
        in_specs=[xs(-1), xs(0), xs(1), pl.BlockSpec((4, cw), lambda n: (0, part)),
                  pl.BlockSpec((BLK, cw), lambda n: (n, 0)),
                  pl.BlockSpec((BLK, cw), lambda n: (jnp.minimum(n + 1, N - 1), 0))],
        out_specs=[pl.BlockSpec((BLK, cw), lambda n: (n, 0)), pl.BlockSpec((4, cw), lambda n: (0, 0))],
        out_shape=[jax.ShapeDtypeStruct((T, 512), F32), jax.ShapeDtypeStruct((4, 512), F32)],
        compiler_params=_params(("arbitrary",)),
    )(p, p, p, w, dact, dact)


def _softplus(x):
    return jnp.maximum(x, 0.0) + jnp.log1p(jnp.exp(-jnp.abs(x)))


def _pick4(tile, off):
    return jnp.stack([_lane_pick(tile, off + h) for h in range(4)])


def _spread4(v4, off, rows):
    lane = _iota2((rows, LANES), 1)
    out = jnp.where(lane == off, v4[0], 0.0)
    for h in range(1, 4):
        out = out + jnp.where(lane == off + h, v4[h], 0.0)
    return out


def _gdn_chunk(qa, ka, va, z, braw, araw, S, alog, dtb, onorm, rowmask, lincl):
    r, c = _iota2((BLK, BLK), 0), _iota2((BLK, BLK), 1)
    incl, strict = r >= c, r > c
    eye = (r == c).astype(F32)
    q = qa * lax.rsqrt(jnp.sum(qa * qa, axis=-1, keepdims=True) + EPS) * (128.0 ** -0.5)
    k = ka * lax.rsqrt(jnp.sum(ka * ka, axis=-1, keepdims=True) + EPS)
    beta = _sigmoid(braw) * rowmask
    g = -jnp.exp(alog) * _softplus(araw + dtb) * rowmask
    gc = hdot(lincl, jnp.broadcast_to(g, qa.shape))
    decay = jnp.where(incl, jnp.exp(jnp.where(incl, gc - _t(gc), 0.0)), 0.0)
    kb = k * beta
    amat = jnp.where(strict, bdot(kb, _t(k)) * decay, 0.0)
    m = -amat
    inv = eye + m
    pw = hdot(m, m)
    for t in range(6):
        inv = inv + hdot(inv, pw)
        if t < 5:
            pw = hdot(pw, pw)
    egc = jnp.exp(gc)
    u = hdot(inv, va * beta)
    w = hdot(inv, kb * egc)
    qk = jnp.where(incl, bdot(q, _t(k)) * decay, 0.0)
    glast = gc[..., BLK - 1:BLK, :]
    vnew = u - bdot(w, S)
    o = bdot(q * egc, S) + bdot(qk, vnew)
    S_new = S * jnp.exp(glast) + bdot(_t(k * jnp.exp(glast - gc)), vnew)
    out = o * lax.rsqrt(jnp.mean(o * o, axis=-1, keepdims=True) + EPS) * onorm * _silu(z)
    return out, S_new


def _lincl():
    i = np.arange(BLK)
    return jnp.broadcast_to(jnp.asarray((i[:, None] >= i[None, :]).astype(np.float32)), (4, BLK, BLK))


def gdn_fwd(act, p, alog, dtb, onorm, *, name):
    T = p.shape[0]
    N = T // BLK

    def body(qa, ka, va, z, ba, alog_ref, dtb_ref, on_ref, l_ref, out_ref, sall_ref, s_scr):
        n = pl.program_id(0)

        @pl.when(n == 0)
        def _():
            s_scr[...] = jnp.zeros_like(s_scr)

        S = s_scr[...]
        sall_ref[...] = S
        out, S_new = _gdn_chunk(_heads(qa[...]), _heads(ka[...]), _heads(va[...]), _heads(z[...]), _pick4(ba[...], 0),
                                _pick4(ba[...], 4), S, _pick4(alog_ref[...], 0), _pick4(dtb_ref[...], 0), on_ref[...],
                                _row_mask(n), l_ref[...])
        out_ref[...] = _unheads(out).astype(BF16)
        s_scr[...] = S_new

    vec = _const_spec((1, LANES))
    return pl.pallas_call(
        body, name=name, grid=(N,),
        in_specs=[_wide(0), _wide(1), _wide(2), _wide(7), pl.BlockSpec((BLK, LANES), lambda n: (n, 32)), vec, vec, vec,
                  _const_spec((4, BLK, BLK))],
        out_specs=[_wide(0), pl.BlockSpec((None, 4, LANES, LANES), lambda n: (n, 0, 0, 0))],
        out_shape=[jax.ShapeDtypeStruct((T, 512), BF16), jax.ShapeDtypeStruct((N, 4, LANES, LANES), F32)],
        scratch_shapes=[pltpu.VMEM((4, LANES, LANES), F32)],
        compiler_params=_params(("arbitrary",)),
    )(act, act, act, p, p, alog, dtb, onorm, _lincl())


def gdn_bwd(act, p, alog, dtb, onorm, sall, dmixed, *, name):
    T = p.shape[0]
    N = T // BLK

    def body(qa, ka, va, z, ba, alog_ref, dtb_ref, on_ref, l_ref, sall_ref, do_ref,
             dq_ref, dk_ref, dv_ref, dz_ref, dba_ref, dal_ref, ddt_ref, don_ref, ds_scr):
        step = pl.program_id(0)
        n = N - 1 - step

        @pl.when(step == 0)
        def _():
            ds_scr[...] = jnp.zeros_like(ds_scr)
            dal_ref[...] = jnp.zeros_like(dal_ref)
            ddt_ref[...] = jnp.zeros_like(ddt_ref)
            don_ref[...] = jnp.zeros_like(don_ref)

        rowmask, lincl = _row_mask(n), l_ref[...]
        f = lambda *a: _gdn_chunk(*a, rowmask, lincl)
        _, vjp = jax.vjp(f, _heads(qa[...]), _heads(ka[...]), _heads(va[...]), _heads(z[...]), _pick4(ba[...], 0),
                         _pick4(ba[...], 4), sall_ref[...], _pick4(alog_ref[...], 0), _pick4(dtb_ref[...], 0),
                         on_ref[...])
        g = vjp((_heads(do_ref[...]), ds_scr[...]))
        dq_ref[...] = _unheads(g[0]) * rowmask
        dk_ref[...] = _unheads(g[1]) * rowmask
        dv_ref[...] = _unheads(g[2]) * rowmask
        dz_ref[...] = _unheads(g[3]) * rowmask
        dba_ref[...] = (_spread4(g[4], 0, BLK) + _spread4(g[5], 4, BLK)) * rowmask
        ds_scr[...] = g[6]
        dal_ref[...] += _spread4(g[7], 0, 1)
        ddt_ref[...] += _spread4(g[8], 0, 1)
        don_ref[...] += g[9]

    def rwide(off):
        return pl.BlockSpec((BLK, 4 * LANES), lambda s: (N - 1 - s, off))

    vec = _const_spec((1, LANES))
    col = pl.BlockSpec((BLK, LANES), lambda s: (N - 1 - s, 0))
    return pl.pallas_call(
        body, name=name, grid=(N,),
        in_specs=[rwide(0), rwide(1), rwide(2), rwide(7), pl.BlockSpec((BLK, LANES), lambda s: (N - 1 - s, 32)), vec, vec,
                  vec, _const_spec((4, BLK, BLK)),
                  pl.BlockSpec((None, 4, LANES, LANES), lambda s: (N - 1 - s, 0, 0, 0)), rwide(1)],
        out_specs=[rwide(0)] * 4 + [col, vec, vec, vec],
        out_shape=[jax.ShapeDtypeStruct((T, 512), F32)] * 4 + [jax.ShapeDtypeStruct((T, LANES), F32)]
        + [jax.ShapeDtypeStruct((1, LANES), F32)] * 3,
        scratch_shapes=[pltpu.VMEM((4, LANES, LANES), F32)],
        compiler_params=_params(("arbitrary",)),
    )(act, act, act, p, p, alog, dtb, onorm, _lincl(), sall, dmixed)


NEG = -1e30


def _swa_block(q, k0, kp, kc, v0, vp, vc, sink, n):
    r, c = _iota2((BLK, BLK), 0), _iota2((BLK, BLK), 1)
    m0 = (c >= PAD) & (c <= n * BLK + r)
    mp = (n >= 2) & (c > r)
    mc = (n >= 1) & (r >= c)
    b = lambda t: jnp.broadcast_to(t, (4,) + t.shape)
    qs = q * (64.0 ** -0.5)
    s0 = jnp.where(m0, bdot(qs, _t(b(k0))), NEG)
    sp = jnp.where(mp, bdot(qs, _t(b(kp))), NEG)
    sc = jnp.where(mc, bdot(qs, _t(b(kc))), NEG)
    mx = jnp.maximum(jnp.max(jnp.maximum(jnp.maximum(s0, sp), sc), axis=-1, keepdims=True), sink)
    mx = lax.stop_gradient(mx)
    p0, pp, pc = jnp.exp(s0 - mx), jnp.exp(sp - mx), jnp.exp(sc - mx)
    den = (jnp.sum(p0, axis=-1, keepdims=True) + jnp.sum(pp, axis=-1, keepdims=True)
           + jnp.sum(pc, axis=-1, keepdims=True) + jnp.exp(sink - mx))
    return (bdot(p0, b(v0)) + bdot(pp, b(vp)) + bdot(pc, b(vc))) / den


def _swa_specs():
    rows = (lambda n: 0, lambda n: jnp.maximum(n - 1, 0), lambda n: n)

    def kv_spec(off, row):
        return pl.BlockSpec((BLK, LANES), lambda g, n: (row(n), off + g))

    q = pl.BlockSpec((BLK, 4 * LANES), lambda g, n: (n, g))
    return q, [kv_spec(off, row) for off in (8, 10) for row in rows]


def swa_fwd(p2, sinkrow, *, name):
    T = p2.shape[0]
    N = T // BLK

    def body(q, k0, kp, kc, v0, vp, vc, sink_ref, o_ref):
        g, n = pl.program_id(0), pl.program_id(1)
        f32 = lambda ref: ref[...].astype(F32)
        o = _swa_block(_heads(f32(q)), f32(k0), f32(kp), f32(kc), f32(v0), f32(vp), f32(vc),
                       _pick4(sink_ref[...], 4 * g), n)
        o_ref[...] = _unheads(o).astype(BF16)

    q, kv = _swa_specs()
    return pl.pallas_call(
        body, name=name, grid=(2, N), in_specs=[q] + kv + [_const_spec((1, LANES))],
        out_specs=q, out_shape=jax.ShapeDtypeStruct((T, 1024), BF16),
        compiler_params=_params(("arbitrary", "arbitrary")),
    )(p2, p2, p2, p2, p2, p2, p2, sinkrow)


def swa_bwd(p2, sinkrow, dmixed, *, name):
    T = p2.shape[0]
    N = T // BLK

    def body(q, k0, kp, kc, v0, vp, vc, sink_ref, do_ref, dq_ref, dk_ref, dv_ref, dsink_ref):
        g, n = pl.program_id(0), pl.program_id(1)

        @pl.when(n == 0)
        def _():
            dk_ref[...] = jnp.zeros_like(dk_ref)
            dv_ref[...] = jnp.zeros_like(dv_ref)

        @pl.when((g == 0) & (n == 0))
        def _():
            dsink_ref[...] = jnp.zeros_like(dsink_ref)

        f = lambda *a: _swa_block(*a, n)
        f32 = lambda ref: ref[...].astype(F32)
        _, vjp = jax.vjp(f, _heads(f32(q)), f32(k0), f32(kp), f32(kc), f32(v0), f32(vp), f32(vc),
                         _pick4(sink_ref[...], 4 * g))
        dq, dk0, dkp, dkc, dv0, dvp, dvc, dsink = vjp(_heads(do_ref[...]))
        dq_ref[...] = _unheads(dq)
        prev = pl.ds(pl.multiple_of(jnp.maximum(n - 1, 0) * BLK, BLK), BLK)
        cur = pl.ds(pl.multiple_of(n * BLK, BLK), BLK)
        for ref, d0, dp, dc in ((dk_ref, dk0, dkp, dkc), (dv_ref, dv0, dvp, dvc)):
            ref[0:BLK, :] += d0
            ref[prev, :] += dp
            ref[cur, :] += dc
        dsink_ref[...] += _spread4(dsink, 4 * g, 1)

    qspec, kv = _swa_specs()
    slab = pl.BlockSpec((T, LANES), lambda g, n: (0, g))
    return pl.pallas_call(
        body, name=name, grid=(2, N), in_specs=[qspec] + kv + [_const_spec((1, LANES)), qspec],
        out_specs=[qspec, slab, slab, _const_spec((1, LANES))],
        out_shape=[jax.ShapeDtypeStruct((T, 1024), F32), jax.ShapeDtypeStruct((T, 256), F32),
                   jax.ShapeDtypeStruct((T, 256), F32), jax.ShapeDtypeStruct((1, LANES), F32)],
        compiler_params=_params(("arbitrary", "arbitrary")),
    )(p2, p2, p2, p2, p2, p2, p2, sinkrow, dmixed)


def _split_dot(x, m):
    rows = x.shape[0]
    hi = x.astype(BF16)
    lo = (x - hi.astype(F32)).astype(BF16)
    r = _nn(jnp.concatenate([hi, lo], axis=0), m)
    return r[:rows] + r[rows:]


def _tri_and_ones(strict, ones=True):
    i = np.arange(BLK)
    m = (i[:, None] > i[None, :]) if strict else (i[:, None] >= i[None, :])
    if ones:
        m = np.concatenate([m, np.ones((BLK, BLK), bool)], axis=1)
    return jnp.asarray(m.astype(np.float32), dtype=BF16)


def _later_and_row_sums(x, m):
    r = _split_dot(x, m)
    if m.shape[1] == 2 * BLK:
        return r[:, :BLK], r[:, BLK:]
    return r, jnp.broadcast_to(jnp.sum(x, axis=1, keepdims=True), x.shape)


SB_PAIR = 2
SB_FWD_GROUP = 4


def _sb_positions():
    r, s = _iota2((BLK, BLK), 0), _iota2((BLK, BLK), 1)
    return s - r, s


def _sb_weights(qbs, ks, base, n, pos, carries, after):
    nh, kb = len(qbs), len(ks[0])
    zs = [[_nt(qbs[h], ks[h][c]) for c in range(kb)] for h in range(nh)]
    valid = [(pos[0] < (n - base - c) * BLK) & (pos[1] >= PAD - (base + c) * BLK) for c in range(kb)]
    lb = [[None] * kb for _ in range(nh)]
    sums = [[None] * kb for _ in range(nh)]
    for c in range(kb):
        for h in range(nh):
            z = zs[h][c]
            lb[h][c] = jnp.minimum(z, 0.0) - jnp.log(1.0 + jnp.exp(-jnp.abs(z)))
            sums[h][c] = _later_and_row_sums(jnp.where(valid[c], lb[h][c] - z, 0.0), after)
    a = [[None] * kb for _ in range(nh)]
    carries = list(carries)
    for c in reversed(range(kb)):
        for h in range(nh):
            a[h][c] = jnp.where(valid[c], jnp.exp(lb[h][c] + carries[h] + sums[h][c][0]), 0.0)
            carries[h] = carries[h] + sums[h][c][1]
    return valid, lb, a, carries


def _key_blocks(n_blocks):
    return next(k for k in (5, 3, 1) if n_blocks % k == 0)


def sb_fwd(p2, *, name):
    T = p2.shape[0]
    N = T // BLK
    kb = _key_blocks(N)
    nh = SB_FWD_GROUP
    heads = [slice(h * LANES, (h + 1) * LANES) for h in range(nh)]

    def body(q_ref, k_ref, v_ref, after_ref, o_ref, of_ref):
        n = pl.program_id(1)
        qbs = [(q_ref[:, hs].astype(F32) * (64.0 ** -0.5)).astype(BF16) for hs in heads]
        after, pos = after_ref[...], _sb_positions()
        nsup = n // kb + 1

        def step(t, c):
            accs, carries = c
            base = (nsup - 1 - t) * kb
            rows = [pl.ds(pl.multiple_of((base + sub) * BLK, BLK), BLK) for sub in range(kb)]
            ks = [[k_ref[r, hs] for r in rows] for hs in heads]
            _, _, a, carries = _sb_weights(qbs, ks, base, n, pos, carries, after)
            accs = list(accs)
            for sub, r in enumerate(rows):
                for h, hs in enumerate(heads):
                    accs[h] = accs[h] + _nn(a[h][sub].astype(BF16), v_ref[r, hs])
            return accs, carries

        zero = [jnp.zeros((BLK, LANES), F32)] * nh
        accs, _ = lax.fori_loop(0, nsup, step, (zero, zero))
        acc = jnp.concatenate(accs, axis=1)
        o_ref[...] = acc.astype(BF16)
        of_ref[...] = acc

    wide = nh * LANES

    def slab(off):
        return pl.BlockSpec((T, wide), lambda g, n: (0, off + g))

    def blk(off):
        return pl.BlockSpec((BLK, wide), lambda g, n: (n, off + g))

    return pl.pallas_call(
        body, name=name, grid=(8 // nh, N),
        in_specs=[blk(12 // nh), slab(20 // nh), slab(28 // nh), _const_spec((BLK, BLK))],
        out_specs=[blk(0), blk(0)],
        out_shape=[jax.ShapeDtypeStruct((T, 1024), BF16), jax.ShapeDtypeStruct((T, 1024), F32)],
        compiler_params=_params(("arbitrary", "arbitrary")),
    )(p2, p2, p2, _tri_and_ones(True, ones=False))


def sb_bwd(p2, o, dmixed, *, name):
    T = p2.shape[0]
    N = T // BLK
    kb = _key_blocks(N)

    heads = [slice(h * LANES, (h + 1) * LANES) for h in range(SB_PAIR)]
    scale = 64.0 ** -0.5

    def body(q_ref, k_ref, v_ref, after_ref, from_ref, o_ref, do_ref, dq_ref, dk_ref, dv_ref, dkt_scr, dvt_scr):
        n = pl.program_id(1)

        @pl.when(n == 0)
        def _():
            dkt_scr[...] = jnp.zeros_like(dkt_scr)
            dvt_scr[...] = jnp.zeros_like(dvt_scr)

        qbs, qts, dobs, dots, totals = [], [], [], [], []
        for hs in heads:
            qs = q_ref[:, hs].astype(F32) * scale
            do = do_ref[:, hs]
            qbs.append(qs.astype(BF16))
            qts.append(qs.T.astype(BF16))
            dobs.append(do.astype(BF16))
            dots.append(do.T.astype(BF16))
            total = jnp.sum(dobs[-1].astype(F32) * o_ref[:, hs], axis=1, keepdims=True)
            totals.append(jnp.broadcast_to(total, (BLK, LANES)))
        after, frm, pos = after_ref[...], from_ref[...], _sb_positions()
        nsup = n // kb + 1

        def step(t, c):
            dqs, carries, gcarries = c
            base = (nsup - 1 - t) * kb
            rows = [pl.ds(pl.multiple_of((base + sub) * BLK, BLK), BLK) for sub in range(kb)]
            ks = [[k_ref[r, hs] for r in rows] for hs in heads]
            valid, lb, a, carries = _sb_weights(qbs, ks, base, n, pos, carries, after)
            das = [[_nt(dobs[h], v_ref[r, hs]) for r in rows] for h, hs in enumerate(heads)]
            ab = [[a[h][sub].astype(BF16) for sub in range(kb)] for h in range(SB_PAIR)]
            g = [[None] * kb for _ in heads]
            sums = [[None] * kb for _ in heads]
            for sub in range(kb):
                for h in range(SB_PAIR):
                    g[h][sub] = das[h][sub] * ab[h][sub].astype(F32)
                    sums[h][sub] = _later_and_row_sums(g[h][sub], frm)
            dqs, gcarries = list(dqs), list(gcarries)
            for sub in reversed(range(kb)):
                for h in range(SB_PAIR):
                    before = totals[h] - (gcarries[h] + sums[h][sub][0])
                    gcarries[h] = gcarries[h] + sums[h][sub][1]
                    beta = jnp.exp(lb[h][sub])
                    dz = jnp.where(valid[sub], g[h][sub] - beta * (g[h][sub] + before), 0.0).astype(BF16)
                    dqs[h] = dqs[h] + _nn(dz, ks[h][sub])
                    dkt_scr[h * N + base + sub] += _nn(qts[h], dz)
                    dvt_scr[h * N + base + sub] += _nn(dots[h], ab[h][sub])
            return dqs, carries, gcarries

        zero = [jnp.zeros((BLK, LANES), F32)] * SB_PAIR
        dqs, _, _ = lax.fori_loop(0, nsup, step, (zero, zero, zero))
        dq_ref[...] = (jnp.concatenate(dqs, axis=1) * scale).astype(dq_ref.dtype)

        @pl.when(n == N - 1)
        def _():
            def flush(j, _):
                rows = pl.ds(pl.multiple_of(j * BLK, BLK), BLK)
                for h, hs in enumerate(heads):
                    dk_ref[rows, hs] = dkt_scr[h * N + j].T.astype(dk_ref.dtype)
                    dv_ref[rows, hs] = dvt_scr[h * N + j].T.astype(dv_ref.dtype)
                return 0

            lax.fori_loop(0, N, flush, 0)

    wide = SB_PAIR * LANES

    def slab(off):
        return pl.BlockSpec((T, wide), lambda g, n: (0, off + g))

    def blk(off):
        return pl.BlockSpec((BLK, wide), lambda g, n: (n, off + g))

    tri = _const_spec((BLK, BLK))
    return pl.pallas_call(
        body, name=name, grid=(8 // SB_PAIR, N),
        in_specs=[blk(12 // SB_PAIR), slab(20 // SB_PAIR), slab(28 // SB_PAIR), tri, tri, blk(0), blk(8 // SB_PAIR)],
        out_specs=[blk(0), slab(0), slab(0)],
        out_shape=[jax.ShapeDtypeStruct((T, 1024), BF16)] * 3,
        scratch_shapes=[pltpu.VMEM((SB_PAIR * N, LANES, LANES), F32), pltpu.VMEM((SB_PAIR * N, LANES, LANES), F32)],
        compiler_params=_params(("arbitrary", "arbitrary")),
    )(p2, p2, p2, _tri_and_ones(True, ones=False), _tri_and_ones(False, ones=False), o, dmixed)


def ffn_fwd(h, g_pre, g_post, wg, wu, wd, tag):
    u, gate, up, act = norm_mm(h, g_pre, (wg, wu), swiglu=True, wt=True, name=f"ffn_up_{tag}")
    y, h_new = mm_norm_res([act], [wd], h, g_post, 0.5, name=f"ffn_down_{tag}")
    return h_new, (h, u, gate, up, y)


def ffn_bwd(saved, dh, g_pre, g_post, wg, wu, wd, tag):
    h, u, gate, up, y = saved
    dy, dg_post, dgate, dup, act = normbwd_mm_nt(dh, y, g_post, wd, 0.5, (gate, up), name=f"ffn_bwd_down_{tag}")
    dwd = mm_tn(act, dy, name=f"ffn_dwd_{tag}")
    dwg = mm_tn(dgate, u, name=f"ffn_dwg_{tag}")
    dwu = mm_tn(dup, u, name=f"ffn_dwu_{tag}")
    dh_in, dg_pre = mm_nt_normbwd([dgate, dup], [wg, wu], h, g_pre, dh, wt=True, name=f"ffn_bwd_up_{tag}")
    return dh_in, (dg_pre, dg_post), (dwg, dwu, dwd)


def _lane_row(v):
    v = v.reshape(1, -1)
    return jnp.pad(v, ((0, 0), (0, LANES - v.shape[1])))


AB_WIDTHS = (512,) * 8 + (LANES,)


def mixer_ab_fwd(h, g_pre, g_post, w_in, conv_w, a_log, dt_bias, out_norm, w_out, tables):
    u, p = norm_mm(h, g_pre, (w_in,), swiglu=False, name="ab_in")
    ret, sall_r = retention_fwd(p, tables, name="retention_fwd")
    act = conv_silu_fwd(p, conv_w, name="conv_fwd")
    gdn, sall_g = gdn_fwd(act, p, _lane_row(a_log), _lane_row(dt_bias), out_norm.reshape(1, LANES), name="gdn_fwd")
    y, h_new = mm_norm_res([ret, gdn], [w_out[:512], w_out[512:]], h, g_post, 1.0, name="ab_out")
    return h_new, (h, u, p, ret, sall_r, act, gdn, sall_g, y)


def mixer_ab_bwd(saved, dh, g_pre, g_post, w_in, conv_w, a_log, dt_bias, out_norm, w_out, tables):
    h, u, p, ret, sall_r, act, gdn, sall_g, y = saved
    dy, dg_post, dmixed = normbwd_mm_nt(dh, y, g_post, w_out, 1.0, name="ab_bwd_out")
    dw_out = jnp.concatenate([mm_tn(ret, dy, name="ab_dwout_ret"), mm_tn(gdn, dy, name="ab_dwout_gdn")], axis=0)
    pieces = list(retention_bwd(p, sall_r, dmixed, tables, name="retention_bwd"))
    dqa, dka, dva, dz, dba, dalog, ddtb, donorm = gdn_bwd(
        act, p, _lane_row(a_log), _lane_row(dt_bias), out_norm.reshape(1, LANES), sall_g, dmixed, name="gdn_bwd")
    dconv = []
    for part, dact in enumerate((dqa, dka, dva)):
        dx, dw = conv_silu_bwd(p, conv_w, dact, part, name=f"conv_bwd_{part}")
        pieces.append(dx)
        dconv.append(dw)
    pieces += [dz, dba]
    offs = np.cumsum((0,) + AB_WIDTHS)
    w_parts = [w_in[:, a:b] for a, b in zip(offs[:-1], offs[1:])]
    dh_in, dg_pre = mm_nt_normbwd(pieces, w_parts, h, g_pre, dh, name="ab_bwd_in")
    dw_in = jnp.concatenate([mm_tn(u, pc, name=f"ab_dwin_{i}") for i, pc in enumerate(pieces)], axis=1)
    small = (jnp.concatenate(dconv, axis=1), dalog[:, :4], ddtb[:, :4], donorm)
    return dh_in, (dg_pre, dg_post), (dw_in, dw_out), small


CD_WIDTHS = (1024, 256, 256, 1024, 1024, 1024)


def mixer_cd_fwd(h, g_pre, g_post, w_in, sinks, w_out):
    u, p2 = norm_mm(h, g_pre, (w_in,), swiglu=False, out_dtype=BF16, name="cd_in")
    swa = swa_fwd(p2, _lane_row(sinks), name="swa_fwd")
    sb, sb_f32 = sb_fwd(p2, name="sb_fwd")
    y, h_new = mm_norm_res([swa, sb], [w_out[:1024], w_out[1024:]], h, g_post, 1.0, name="cd_out")
    return h_new, (h, u, p2, swa, sb, sb_f32, y)


def mixer_cd_bwd(saved, dh, g_pre, g_post, w_in, sinks, w_out):
    h, u, p2, swa, sb, sb_f32, y = saved
    dy, dg_post, dmixed = normbwd_mm_nt(dh, y, g_post, w_out, 1.0, name="cd_bwd_out")
    dw_out = jnp.concatenate([mm_tn(swa, dy, name="cd_dwout_swa"), mm_tn(sb, dy, name="cd_dwout_sb")], axis=0)
    dq_c, dk_c, dv_c, dsink = swa_bwd(p2, _lane_row(sinks), dmixed, name="swa_bwd")
    pieces = [dq_c, dk_c, dv_c] + list(sb_bwd(p2, sb_f32, dmixed, name="sb_bwd"))
    offs = np.cumsum((0,) + CD_WIDTHS)
    w_parts = [w_in[:, a:b] for a, b in zip(offs[:-1], offs[1:])]
    dh_in, dg_pre = mm_nt_normbwd(pieces, w_parts, h, g_pre, dh, name="cd_bwd_in")
    dw_in = jnp.concatenate([mm_tn(u, pc, name=f"cd_dwin_{i}") for i, pc in enumerate(pieces)], axis=1)
    return dh_in, (dg_pre, dg_post), (dw_in, dw_out), dsink[:, :8]


def _pad_heads(w, axis):
    shape = w.shape
    w = w.reshape(shape[:axis] + (shape[axis] // 64, 64) + shape[axis + 1:])
    pad = [(0, 0)] * w.ndim
    pad[axis + 1] = (0, 64)
    return jnp.pad(w, pad).reshape(shape[:axis] + (2 * shape[axis],) + shape[axis + 1:])


def _unpad_heads(w, axis):
    shape = w.shape
    w = w.reshape(shape[:axis] + (shape[axis] // 128, 128) + shape[axis + 1:])
    w = lax.slice_in_dim(w, 0, 64, axis=axis + 1)
    return w.reshape(shape[:axis] + (shape[axis] // 2,) + shape[axis + 1:])


SMALL_SHARDED = (("meta_tokens", (NMETA, LANES), 1), ("norm_gains", (2, 6, LANES), 2), ("ab_conv_w", (1, 4, 192), 2))
SMALL_REPL = (("ab_a_log", (1, 4)), ("ab_dt_bias", (1, 4)), ("ab_out_norm", (1, LANES)), ("cd_sinks", (1, 8)))


def _stack_shards(g, axis):
    full = jnp.moveaxis(g, 0, axis)
    shape = full.shape
    return full.reshape(shape[:axis] + (shape[axis] * shape[axis + 1],) + shape[axis + 2:])


def _split_shards(full, axis):
    shape = full.shape
    g = full.reshape(shape[:axis] + (NDEV, shape[axis] // NDEV) + shape[axis + 1:])
    return jnp.moveaxis(g, axis, 0)


def _pad_rows8(a):
    rows = []
    for x in a:
        flat = x.reshape(x.shape[0], -1)
        n = -(-flat.shape[1] // LANES) * LANES
        rows.append(jnp.pad(flat, ((0, 0), (0, n - flat.shape[1]))).reshape(x.shape[0], n // LANES, LANES))
    cat = jnp.concatenate(rows, axis=1)
    return jnp.pad(cat, ((0, 0), (0, -cat.shape[1] % 8), (0, 0)))


def _unpad_rows8(packed, shapes):
    out, at = [], 0
    for shape in shapes:
        size = int(np.prod(shape))
        nrow = -(-size // LANES)
        blk = packed[:, at:at + nrow].reshape(packed.shape[0], -1)[:, :size]
        out.append(blk.reshape((packed.shape[0],) + tuple(shape)))
        at += nrow
    return out


def kernel(x, meta_tokens, norm_gains, ffn_w_gate, ffn_w_up, ffn_w_down, ab_w_in, ab_conv_w, ab_a_log, ab_dt_bias, ab_out_norm, ab_w_out, cd_w_in, cd_sinks, cd_w_out, loss_target, m_meta_tokens, m_norm_gains, m_ffn_w_gate, m_ffn_w_up, m_ffn_w_down, m_ab_w_in, m_ab_conv_w, m_ab_a_log, m_ab_dt_bias, m_ab_out_norm, m_ab_w_out, m_cd_w_in, m_cd_sinks, m_cd_w_out, v_meta_tokens, v_norm_gains, v_ffn_w_gate, v_ffn_w_up, v_ffn_w_down, v_ab_w_in, v_ab_conv_w, v_ab_a_log, v_ab_dt_bias, v_ab_out_norm, v_ab_w_out, v_cd_w_in, v_cd_sinks, v_cd_w_out):
    w = dict(meta_tokens=meta_tokens, norm_gains=norm_gains, ffn_w_gate=ffn_w_gate, ffn_w_up=ffn_w_up,
             ffn_w_down=ffn_w_down, ab_w_in=ab_w_in, ab_conv_w=ab_conv_w, ab_a_log=ab_a_log, ab_dt_bias=ab_dt_bias,
             ab_out_norm=ab_out_norm, ab_w_out=ab_w_out, cd_w_in=cd_w_in, cd_sinks=cd_sinks, cd_w_out=cd_w_out)
    m = dict(meta_tokens=m_meta_tokens, norm_gains=m_norm_gains, ffn_w_gate=m_ffn_w_gate, ffn_w_up=m_ffn_w_up,
             ffn_w_down=m_ffn_w_down, ab_w_in=m_ab_w_in, ab_conv_w=m_ab_conv_w, ab_a_log=m_ab_a_log,
             ab_dt_bias=m_ab_dt_bias, ab_out_norm=m_ab_out_norm, ab_w_out=m_ab_w_out, cd_w_in=m_cd_w_in,
             cd_sinks=m_cd_sinks, cd_w_out=m_cd_w_out)
    v = dict(meta_tokens=v_meta_tokens, norm_gains=v_norm_gains, ffn_w_gate=v_ffn_w_gate, ffn_w_up=v_ffn_w_up,
             ffn_w_down=v_ffn_w_down, ab_w_in=v_ab_w_in, ab_conv_w=v_ab_conv_w, ab_a_log=v_ab_a_log,
             ab_dt_bias=v_ab_dt_bias, ab_out_norm=v_ab_out_norm, ab_w_out=v_ab_w_out, cd_w_in=v_cd_w_in,
             cd_sinks=v_cd_sinks, cd_w_out=v_cd_w_out)
    order = list(w)
    S = x.shape[1]
    T = S + BLK

    fs = DFF // NDEV
    ffn_local = jnp.concatenate([jnp.swapaxes(ffn_w_gate, 2, 3).reshape(4 * fs, D),
                                 jnp.swapaxes(ffn_w_up, 2, 3).reshape(4 * fs, D), ffn_w_down.reshape(4 * fs, D)],
                                axis=0).astype(BF16)
    outs_local = jnp.concatenate([ab_w_out[0], cd_w_out[0]], axis=0).astype(BF16)
    ffn_all, abin_all, outs_all, cdin_all = all_gather_big(
        [ffn_local, ab_w_in[0].astype(BF16), outs_local, cd_w_in[0].astype(BF16)], name="gather_weights")
    ffn_mat = lambda k: ffn_all[:, k * fs:(k + 1) * fs].reshape(DFF, D)
    layers = [(i, j) for i in range(2) for j in range(2)]
    wg = {ij: ffn_mat(k) for k, ij in enumerate(layers)}
    wu = {ij: ffn_mat(4 + k) for k, ij in enumerate(layers)}
    wd = {ij: ffn_mat(8 + k) for k, ij in enumerate(layers)}
    ab_in = jnp.pad(_stack_shards(abin_all, 1), ((0, 0), (0, AB_INP - AB_IN)))
    ab_out = outs_all[:, :D // NDEV].reshape(D, D)
    cd_in = _pad_heads(_stack_shards(cdin_all, 1), 1)
    cd_out = _pad_heads(outs_all[:, D // NDEV:].reshape(D, D), 0)
    small_src = jnp.broadcast_to(_pad_rows8([w[n][None] for n, _, _ in SMALL_SHARDED]), (NDEV, 40, LANES))
    small_all = _unpad_rows8(all_to_all_small(small_src, name="gather_small"), [s for _, s, _ in SMALL_SHARDED])
    full = {n: _stack_shards(g, ax) for (n, _, ax), g in zip(SMALL_SHARDED, small_all)}
    conv_w = full["ab_conv_w"][0]
    gains = full["norm_gains"].reshape(2, 6, 1, D)
    tables = retention_tables(T)

    h = jnp.concatenate([jnp.zeros((PAD, D), F32), full["meta_tokens"], x[0]], axis=0)
    h, s00 = ffn_fwd(h, gains[0, 0], gains[0, 1], wg[0, 0], wu[0, 0], wd[0, 0], "00")
    h, sab = mixer_ab_fwd(h, gains[0, 2], gains[0, 3], ab_in, conv_w, ab_a_log, ab_dt_bias, ab_out_norm, ab_out, tables)
    h, s01 = ffn_fwd(h, gains[0, 4], gains[0, 5], wg[0, 1], wu[0, 1], wd[0, 1], "01")
    h, s10 = ffn_fwd(h, gains[1, 0], gains[1, 1], wg[1, 0], wu[1, 0], wd[1, 0], "10")
    h, scd = mixer_cd_fwd(h, gains[1, 2], gains[1, 3], cd_in, cd_sinks, cd_out)
    h, s11 = ffn_fwd(h, gains[1, 4], gains[1, 5], wg[1, 1], wu[1, 1], wd[1, 1], "11")
    loss_tile, dh = loss_and_grad(h, loss_target[0], name="loss")
    loss = lax.psum(loss_tile[0, 0], ("x", "y", "c"))

    dgain = [[None] * 6, [None] * 6]
    dffn = {}
    dh, (dgain[1][4], dgain[1][5]), dffn[1, 1] = ffn_bwd(s11, dh, gains[1, 4], gains[1, 5], wg[1, 1], wu[1, 1], wd[1, 1], "11")
    dh, (dgain[1][2], dgain[1][3]), (dcd_in, dcd_out), dsinks = mixer_cd_bwd(scd, dh, gains[1, 2], gains[1, 3], cd_in, cd_sinks, cd_out)
    dh, (dgain[1][0], dgain[1][1]), dffn[1, 0] = ffn_bwd(s10, dh, gains[1, 0], gains[1, 1], wg[1, 0], wu[1, 0], wd[1, 0], "10")
    dh, (dgain[0][4], dgain[0][5]), dffn[0, 1] = ffn_bwd(s01, dh, gains[0, 4], gains[0, 5], wg[0, 1], wu[0, 1], wd[0, 1], "01")
    dh, (dgain[0][2], dgain[0][3]), (dab_in, dab_out), (dconv, dalog, ddtb, donorm) = mixer_ab_bwd(
        sab, dh, gains[0, 2], gains[0, 3], ab_in, conv_w, ab_a_log, ab_dt_bias, ab_out_norm, ab_out, tables)
    dh, (dgain[0][0], dgain[0][1]), dffn[0, 0] = ffn_bwd(s00, dh, gains[0, 0], gains[0, 1], wg[0, 0], wu[0, 0], wd[0, 0], "00")
    grad_x = dh[BLK:][None]

    gfull = dict(meta_tokens=dh[PAD:BLK], norm_gains=jnp.stack([jnp.concatenate(r, axis=0) for r in dgain]),
                 ab_conv_w=dconv[None])
    ffn_send = jnp.concatenate([dffn[ij][k].astype(BF16).reshape(NDEV, fs, D) for k in range(3) for ij in layers], axis=1)
    outs_send = jnp.concatenate([dab_out.astype(BF16).reshape(NDEV, D // NDEV, D),
                                 _unpad_heads(dcd_out, 0).astype(BF16).reshape(NDEV, D // NDEV, D)], axis=1)
    abin_send = _split_shards(dab_in[:, :AB_IN].astype(BF16), 1)
    cdin_send = _split_shards(_unpad_heads(dcd_in, 1).astype(BF16), 1)
    ffn_g, abin_g, outs_g, cdin_g = reduce_scatter_big([ffn_send, abin_send, outs_send, cdin_send])
    ffn_g = ffn_g.reshape(3, 2, 2, fs, D)
    grads = dict(ffn_w_gate=jnp.swapaxes(ffn_g[0], 2, 3), ffn_w_up=jnp.swapaxes(ffn_g[1], 2, 3), ffn_w_down=ffn_g[2],
                 ab_w_in=abin_g[None], ab_w_out=outs_g[None, :D // NDEV], cd_w_in=cdin_g[None],
                 cd_w_out=outs_g[None, D // NDEV:])
    repl = [jnp.broadcast_to(t[None], (NDEV,) + t.shape) for t in (dalog, ddtb, donorm, dsinks)]
    ssend = _pad_rows8([_split_shards(gfull[n], ax) for n, _, ax in SMALL_SHARDED] + repl)
    ssum = sum_slots(all_to_all_small(ssend, name="exchange_small_grads"), name="sum_small_grads")[None]
    small = _unpad_rows8(ssum, [s for _, s, _ in SMALL_SHARDED] + [s for _, s in SMALL_REPL])
    grads.update({n: g[0] for n, g in zip([n for n, _, _ in SMALL_SHARDED] + [n for n, _ in SMALL_REPL], small)})

    delta, new_m, new_v = {}, {}, {}
    for n in order:
        shape = w[n].shape
        view = (-1, shape[-1])
        d_, m_, v_ = adamw(w[n].reshape(view), grads[n].reshape(view), m[n].reshape(view), v[n].reshape(view),
                           name=f"adamw_{n}")
        delta[n], new_m[n], new_v[n] = d_.reshape(shape), m_.reshape(shape), v_.reshape(shape)
    return (loss, grad_x, *[grads[n] for n in order], *[delta[n] for n in order], *[new_m[n] for n in order],
            *[new_v[n] for n in order])
```

```python
import functools
import math

import numpy as np
import jax
import jax.numpy as jnp
from jax import lax
from jax.experimental import pallas as pl
from jax.experimental.pallas import tpu as pltpu

F32, BF16 = jnp.float32, jnp.bfloat16
EPS = 1e-6
D = 1024
NMETA = 16
BLK = 128
PAD = BLK - NMETA
DFF = 2816
LANES = 128
NDEV = 8
AB_IN, AB_INP = 4104, 4224
ADAM_LR, ADAM_B1, ADAM_B2, ADAM_EPS, ADAM_WD, ADAM_STEP = 0.001, 0.9, 0.999, 1e-08, 0.01, 10
VMEM_LIMIT = 56 * 1024 * 1024
MESH = pl.DeviceIdType.MESH
HIGH = lax.Precision.HIGH


def _params(sem):
    return pltpu.CompilerParams(dimension_semantics=sem, vmem_limit_bytes=VMEM_LIMIT)


def _row_tile(T, streamed, resident):
    for tm in (640, 320, 128):
        if T % tm == 0 and 2 * (tm * streamed + resident) <= VMEM_LIMIT - 14 * 1024 * 1024:
            return tm
    return _tile(T, 128)


def _tile(n, cap):
    if n <= cap:
        return n
    best = None
    for t in range(LANES, cap + 1, LANES):
        if n % t == 0:
            best = t
    assert best is not None, (n, cap)
    return best


def _rms_fwd(x, g):
    return x * lax.rsqrt(jnp.mean(x * x, axis=-1, keepdims=True) + EPS) * g


def _rms_bwd(x, g, dz):
    r = lax.rsqrt(jnp.mean(x * x, axis=-1, keepdims=True) + EPS)
    xh = x * r
    dg = jnp.sum(dz * xh, axis=0, keepdims=True)
    t = dz * g
    return r * (t - xh * jnp.mean(t * xh, axis=-1, keepdims=True)), dg


def _sigmoid(x):
    return 1.0 / (1.0 + jnp.exp(-x))


def _silu(x):
    return x * _sigmoid(x)


def _nn(a, b, precision=None):
    return lax.dot_general(a, b, (((1,), (0,)), ((), ())), preferred_element_type=F32, precision=precision)


def _nt(a, b):
    return lax.dot_general(a, b, (((1,), (1,)), ((), ())), preferred_element_type=F32)


def _tn(a, b):
    return lax.dot_general(a, b, (((0,), (0,)), ((), ())), preferred_element_type=F32)


def _mm(a, b, precision=None):
    if a.ndim == 3:
        return lax.dot_general(a, b, (((2,), (1,)), ((0,), (0,))), preferred_element_type=F32, precision=precision)
    return _nn(a, b, precision)


def _t(x):
    return jnp.swapaxes(x, -1, -2)


@jax.custom_vjp
def bdot(a, b):
    return _mm(a.astype(BF16), b.astype(BF16))


def _bdot_fwd(a, b):
    return bdot(a, b), (a, b)


def _bdot_bwd(res, g):
    a, b = res
    return bdot(g, _t(b)), bdot(_t(a), g)


bdot.defvjp(_bdot_fwd, _bdot_bwd)


@jax.custom_vjp
def hdot(a, b):
    return _mm(a, b, HIGH)


def _hdot_fwd(a, b):
    return hdot(a, b), (a, b)


def _hdot_bwd(res, g):
    a, b = res
    return hdot(g, _t(b)), hdot(_t(a), g)


hdot.defvjp(_hdot_fwd, _hdot_bwd)


def _iota2(shape, axis):
    return lax.broadcasted_iota(jnp.int32, shape, axis)


def _lane_pick(row, lane):
    return jnp.sum(jnp.where(_iota2(row.shape, 1) == lane, row, 0.0), axis=1, keepdims=True)


def norm_mm(h, gain, ws, *, swiglu, name, wt=False, out_dtype=F32):
    T, Dm = h.shape
    N = ws[0].shape[0 if wt else 1]
    tm, tn = _tile(T, 640), _tile(N, 1408)
    nw = len(ws)
    mm = _nt if wt else _nn

    def body(h_ref, g_ref, *refs):
        w_refs, u_ref, o_refs = refs[:nw], refs[nw], refs[nw + 1:]

        @pl.when(pl.program_id(1) == 0)
        def _():
            u_ref[...] = _rms_fwd(h_ref[...], g_ref[...]).astype(BF16)

        u = u_ref[...]
        acc = [mm(u, w[...]) for w in w_refs]
        if swiglu:
            o_refs[0][...] = acc[0].astype(BF16)
            o_refs[1][...] = acc[1].astype(BF16)
            o_refs[2][...] = (_silu(acc[0]) * acc[1]).astype(BF16)
        else:
            o_refs[0][...] = acc[0].astype(out_dtype)

    row = pl.BlockSpec((tm, Dm), lambda i, j: (i, 0))
    tile = pl.BlockSpec((tm, tn), lambda i, j: (i, j))
    if swiglu:
        out_shape = [jax.ShapeDtypeStruct((T, Dm), BF16)] + [jax.ShapeDtypeStruct((T, N), BF16)] * 3
        out_specs = [row, tile, tile, tile]
    else:
        out_shape = [jax.ShapeDtypeStruct((T, Dm), BF16), jax.ShapeDtypeStruct((T, N), out_dtype)]
        out_specs = [row, tile]
    return pl.pallas_call(
        body, name=name, grid=(T // tm, N // tn),
        in_specs=[row, pl.BlockSpec((1, Dm), lambda i, j: (0, 0))]
        + [pl.BlockSpec((tn, Dm), lambda i, j: (j, 0)) if wt else pl.BlockSpec((Dm, tn), lambda i, j: (0, j))] * nw,
        out_specs=out_specs, out_shape=out_shape,
        compiler_params=_params(("arbitrary", "arbitrary")),
    )(h, gain, *ws)


def mm_norm_res(As, Ws, h, gain, scale, *, name):
    T, Dm = h.shape
    n = len(As)
    tm = _row_tile(T, sum(a.shape[1] * a.dtype.itemsize for a in As) + 3 * Dm * 4,
                   sum(w.size * w.dtype.itemsize for w in Ws))

    def body(*refs):
        a_refs, w_refs = refs[:n], refs[n:2 * n]
        h_ref, g_ref, y_ref, hn_ref = refs[2 * n:]
        y = _nn(a_refs[0][...].astype(BF16), w_refs[0][...])
        for a, w in zip(a_refs[1:], w_refs[1:]):
            y = y + _nn(a[...].astype(BF16), w[...])
        y_ref[...] = y
        hn_ref[...] = h_ref[...] + scale * _rms_fwd(y, g_ref[...])

    row = pl.BlockSpec((tm, Dm), lambda i: (i, 0))
    return pl.pallas_call(
        body, name=name, grid=(T // tm,),
        in_specs=[pl.BlockSpec((tm, a.shape[1]), lambda i: (i, 0)) for a in As]
        + [pl.BlockSpec(w.shape, lambda i: (0, 0)) for w in Ws]
        + [row, pl.BlockSpec((1, Dm), lambda i: (0, 0))],
        out_specs=[row, row], out_shape=[jax.ShapeDtypeStruct((T, Dm), F32)] * 2,
        compiler_params=_params(("arbitrary",)),
    )(*As, *Ws, h, gain)


def normbwd_mm_nt(dh, y, gain, w, scale, gu=None, *, name):
    T, Dm = dh.shape
    N = w.shape[0]
    tm, tn = _tile(T, 640), _tile(N, 1408)
    swiglu = gu is not None

    def body(dh_ref, y_ref, g_ref, w_ref, *refs):
        if swiglu:
            gate_ref, up_ref, dy_ref, dg_ref, dgate_ref, dup_ref, a_ref = refs
        else:
            dy_ref, dg_ref, da_ref = refs
        i, j = pl.program_id(0), pl.program_id(1)

        @pl.when(j == 0)
        def _():
            dy, dg = _rms_bwd(y_ref[...], g_ref[...], scale * dh_ref[...])
            dy_ref[...] = dy.astype(BF16)

            @pl.when(i == 0)
            def _():
                dg_ref[...] = jnp.zeros_like(dg_ref)

            dg_ref[...] += dg

        da = _nt(dy_ref[...], w_ref[...])
        if swiglu:
            gate, up = gate_ref[...].astype(F32), up_ref[...].astype(F32)
            s = _sigmoid(gate)
            dgate_ref[...] = (da * up * s * (1.0 + gate * (1.0 - s))).astype(BF16)
            dup_ref[...] = (da * gate * s).astype(BF16)
            a_ref[...] = (gate * s * up).astype(BF16)
        else:
            da_ref[...] = da

    row = pl.BlockSpec((tm, Dm), lambda i, j: (i, 0))
    vec = pl.BlockSpec((1, Dm), lambda i, j: (0, 0))
    tile = pl.BlockSpec((tm, tn), lambda i, j: (i, j))
    in_specs = [row, row, vec, pl.BlockSpec((tn, Dm), lambda i, j: (j, 0))]
    out_shape = [jax.ShapeDtypeStruct((T, Dm), BF16), jax.ShapeDtypeStruct((1, Dm), F32)]
    if swiglu:
        in_specs += [tile, tile]
        out_shape += [jax.ShapeDtypeStruct((T, N), BF16)] * 3
        out_specs = [row, vec, tile, tile, tile]
        args = (dh, y, gain, w, *gu)
    else:
        out_shape += [jax.ShapeDtypeStruct((T, N), F32)]
        out_specs = [row, vec, tile]
        args = (dh, y, gain, w)
    return pl.pallas_call(
        body, name=name, grid=(T // tm, N // tn), in_specs=in_specs, out_specs=out_specs,
        out_shape=out_shape, compiler_params=_params(("arbitrary", "arbitrary")),
    )(*args)


def mm_nt_normbwd(dPs, Ws, h, gain, dh_in, *, name, wt=False):
    T, Dm = h.shape
    n = len(dPs)
    tm = _row_tile(T, sum(p.shape[1] * p.dtype.itemsize for p in dPs) + 3 * Dm * 4,
                   sum(w.size * w.dtype.itemsize for w in Ws))
    mm = _nn if wt else _nt

    def body(*refs):
        p_refs, w_refs = refs[:n], refs[n:2 * n]
        h_ref, g_ref, dhin_ref, dh_ref, dg_ref = refs[2 * n:]
        du = mm(p_refs[0][...].astype(BF16), w_refs[0][...])
        for p, w in zip(p_refs[1:], w_refs[1:]):
            du = du + mm(p[...].astype(BF16), w[...])
        dx, dg = _rms_bwd(h_ref[...], g_ref[...], du)
        dh_ref[...] = dhin_ref[...] + dx

        @pl.when(pl.program_id(0) == 0)
        def _():
            dg_ref[...] = jnp.zeros_like(dg_ref)

        dg_ref[...] += dg

    row = pl.BlockSpec((tm, Dm), lambda i: (i, 0))
    vec = pl.BlockSpec((1, Dm), lambda i: (0, 0))
    return pl.pallas_call(
        body, name=name, grid=(T // tm,),
        in_specs=[pl.BlockSpec((tm, p.shape[1]), lambda i: (i, 0)) for p in dPs]
        + [pl.BlockSpec(w.shape, lambda i: (0, 0)) for w in Ws] + [row, vec, row],
        out_specs=[row, vec],
        out_shape=[jax.ShapeDtypeStruct((T, Dm), F32), jax.ShapeDtypeStruct((1, Dm), F32)],
        compiler_params=_params(("arbitrary",)),
    )(*dPs, *Ws, h, gain, dh_in)


def mm_tn(a, b, *, name):
    T, M = a.shape
    N = b.shape[1]
    tm, tn, tk = _tile(M, 1408), _tile(N, 1408), _tile(T, 640)

    def body(a_ref, b_ref, o_ref):
        @pl.when(pl.program_id(2) == 0)
        def _():
            o_ref[...] = jnp.zeros_like(o_ref)

        o_ref[...] += _tn(a_ref[...].astype(BF16), b_ref[...].astype(BF16))

    return pl.pallas_call(
        body, name=name, grid=(M // tm, N // tn, T // tk),
        in_specs=[pl.BlockSpec((tk, tm), lambda i, j, k: (k, i)), pl.BlockSpec((tk, tn), lambda i, j, k: (k, j))],
        out_specs=pl.BlockSpec((tm, tn), lambda i, j, k: (i, j)),
        out_shape=jax.ShapeDtypeStruct((M, N), F32),
        compiler_params=_params(("arbitrary", "arbitrary", "arbitrary")),
    )(a, b)


def loss_and_grad(h, target, *, name):
    T, Dm = h.shape

    def body(h_ref, t_ref, loss_ref, dh_ref):
        b = pl.program_id(0)

        @pl.when(b == 0)
        def _():
            loss_ref[...] = jnp.zeros_like(loss_ref)
            dh_ref[...] = jnp.zeros_like(dh_ref)

        @pl.when(b > 0)
        def _():
            e = h_ref[...] - t_ref[...]
            dh_ref[...] = e * (1.0 / Dm)
            loss_ref[...] += jnp.sum(e * e) * (0.5 / Dm)

    return pl.pallas_call(
        body, name=name, grid=(T // BLK,),
        in_specs=[pl.BlockSpec((BLK, Dm), lambda b: (b, 0)),
                  pl.BlockSpec((BLK, Dm), lambda b: (jnp.maximum(b - 1, 0), 0))],
        out_specs=[pl.BlockSpec((8, LANES), lambda b: (0, 0)), pl.BlockSpec((BLK, Dm), lambda b: (b, 0))],
        out_shape=[jax.ShapeDtypeStruct((8, LANES), F32), jax.ShapeDtypeStruct((T, Dm), F32)],
        compiler_params=_params(("arbitrary",)),
    )(h, target)


def adamw(w, g, m, v, *, name):
    R, C = w.shape
    tr = R
    for t in (512, 352, 256):
        if R > t and R % t == 0:
            tr = t
            break

    def body(w_ref, g_ref, m_ref, v_ref, d_ref, nm_ref, nv_ref):
        g_ = g_ref[...]
        m_ = ADAM_B1 * m_ref[...] + (1.0 - ADAM_B1) * g_
        v_ = ADAM_B2 * v_ref[...] + (1.0 - ADAM_B2) * (g_ * g_)
        m_hat = m_ / (1.0 - ADAM_B1 ** ADAM_STEP)
        v_hat = v_ / (1.0 - ADAM_B2 ** ADAM_STEP)
        d_ref[...] = -ADAM_LR * (m_hat / (jnp.sqrt(v_hat) + ADAM_EPS) + ADAM_WD * w_ref[...])
        nm_ref[...] = m_
        nv_ref[...] = v_

    spec = pl.BlockSpec((tr, C), lambda i: (i, 0))
    return pl.pallas_call(
        body, name=name, grid=(R // tr,), in_specs=[spec] * 4, out_specs=[spec] * 3,
        out_shape=[jax.ShapeDtypeStruct((R, C), F32)] * 3, compiler_params=_params(("arbitrary",)),
    )(w, g, m, v)


def _me():
    return lax.axis_index("x"), lax.axis_index("y"), lax.axis_index("c")


def _flip(pos, rel):
    return tuple(1 - p if r else p for p, r in zip(pos, rel))


def _slot(pos):
    return 4 * pos[0] + 2 * pos[1] + pos[2]


HBM_SPEC = pl.BlockSpec(memory_space=pltpu.HBM)
CHIP_RELS = ((1, 0), (0, 1), (1, 1))


def all_gather_big(xs, *, name):
    n = len(xs)

    def body(*refs):
        x_refs, out_refs = refs[:n], refs[n:2 * n]
        send_sems, recv_sems, local_sems = refs[2 * n:]
        me = _me()
        sibling = _flip(me, (0, 0, 1))
        chips = [_flip(me, rel + (0,)) for rel in CHIP_RELS]

        def copy(i, k, block, to, src=None):
            dst = out_refs[i].at[_slot(block)]
            return pltpu.make_async_remote_copy(
                src_ref=dst if src is None else src, dst_ref=dst, send_sem=send_sems.at[i, k],
                recv_sem=recv_sems.at[i, k], device_id=to, device_id_type=MESH)

        sent, local = [], []
        for i in range(n):
            mine = pltpu.make_async_copy(x_refs[i], out_refs[i].at[_slot(me)], local_sems.at[i])
            mine.start()
            local.append(mine)
            sent += [copy(i, 0, me, sibling, src=x_refs[i])]
            sent += [copy(i, 1 + j, me, chip, src=x_refs[i]) for j, chip in enumerate(chips)]
        for cp in sent:
            cp.start()
        for i in range(n):
            for j, chip in enumerate(chips):
                copy(i, 1 + j, chip, me).wait_recv()
                passed = copy(i, 4 + j, chip, sibling)
                passed.start()
                sent.append(passed)
        for i in range(n):
            copy(i, 0, sibling, me).wait_recv()
            for j, chip in enumerate(chips):
                copy(i, 4 + j, _flip(chip, (0, 0, 1)), me).wait_recv()
        for cp in sent:
            cp.wait_send()
        for mine in local:
            mine.wait()

    return pl.pallas_call(
        body, name=name, in_specs=[HBM_SPEC] * n, out_specs=[HBM_SPEC] * n,
        out_shape=[jax.ShapeDtypeStruct((NDEV,) + x.shape, x.dtype) for x in xs],
        scratch_shapes=[pltpu.SemaphoreType.DMA((n, 7)), pltpu.SemaphoreType.DMA((n, 7)), pltpu.SemaphoreType.DMA((n,))],
    )(*xs)


def all_to_all_small(src, *, name):
    _, r, C = src.shape

    def body(src_ref, out_ref, send_sems, recv_sems):
        me = _me()
        my = _slot(me)
        out_ref[my] = src_ref[my]
        copies = []
        for k in range(1, NDEV):
            peer = _flip(me, ((k >> 2) & 1, (k >> 1) & 1, k & 1))
            cp = pltpu.make_async_remote_copy(
                src_ref=src_ref.at[_slot(peer)], dst_ref=out_ref.at[my], send_sem=send_sems.at[k - 1],
                recv_sem=recv_sems.at[k - 1], device_id=peer, device_id_type=MESH)
            cp.start()
            copies.append((cp, peer))
        for k, (cp, peer) in enumerate(copies):
            pltpu.make_async_remote_copy(
                src_ref=src_ref.at[my], dst_ref=out_ref.at[_slot(peer)], send_sem=send_sems.at[k],
                recv_sem=recv_sems.at[k], device_id=peer, device_id_type=MESH).wait_recv()
        for cp, _ in copies:
            cp.wait_send()

    vm = pl.BlockSpec(memory_space=pltpu.VMEM)
    return pl.pallas_call(
        body, name=name, in_specs=[vm], out_specs=vm, out_shape=jax.ShapeDtypeStruct(src.shape, src.dtype),
        scratch_shapes=[pltpu.SemaphoreType.DMA((7,)), pltpu.SemaphoreType.DMA((7,))],
    )(src)


def sum_slots(a, *, name):
    n, r, C = a.shape

    def body(a_ref, o_ref):
        s = a_ref[0]
        for k in range(1, n):
            s = s + a_ref[k]
        o_ref[...] = s

    vm = pl.BlockSpec(memory_space=pltpu.VMEM)
    return pl.pallas_call(body, name=name, in_specs=[vm], out_specs=vm,
                          out_shape=jax.ShapeDtypeStruct((r, C), F32))(a)


def rs_exchange_sibling(gs, *, name):
    n = len(gs)

    def body(*refs):
        g_refs, out_refs, send_sems, recv_sems = refs[:n], refs[n:2 * n], refs[2 * n], refs[2 * n + 1]
        sibling = _flip(_me(), (0, 0, 1))
        copies = []
        for i in range(n):
            for chip in range(4):
                cp = pltpu.make_async_remote_copy(
                    src_ref=g_refs[i].at[2 * chip + sibling[2]], dst_ref=out_refs[i].at[chip],
                    send_sem=send_sems.at[i, chip], recv_sem=recv_sems.at[i, chip], device_id=sibling,
                    device_id_type=MESH)
                cp.start()
                copies.append(cp)
        for cp in copies:
            cp.wait()

    return pl.pallas_call(
        body, name=name, in_specs=[HBM_SPEC] * n, out_specs=[HBM_SPEC] * n,
        out_shape=[jax.ShapeDtypeStruct((4,) + g.shape[1:], g.dtype) for g in gs],
        scratch_shapes=[pltpu.SemaphoreType.DMA((n, 4)), pltpu.SemaphoreType.DMA((n, 4))],
    )(*gs)


def rs_chip_partials(g, got, *, name):
    _, R, C = g.shape
    tr = _tile(R, 768)

    def body(c_ref, g_ref, got_ref, o_ref):
        o_ref[...] = (g_ref[...].astype(F32) + got_ref[...].astype(F32)).astype(o_ref.dtype)

    c = jnp.reshape(lax.axis_index("c"), (1,)).astype(jnp.int32)
    return pl.pallas_call(
        body, name=name,
        grid_spec=pltpu.PrefetchScalarGridSpec(
            num_scalar_prefetch=1, grid=(4, R // tr),
            in_specs=[pl.BlockSpec((None, tr, C), lambda k, i, c_ref: (2 * k + c_ref[0], i, 0)),
                      pl.BlockSpec((None, tr, C), lambda k, i, c_ref: (k, i, 0))],
            out_specs=pl.BlockSpec((None, tr, C), lambda k, i, c_ref: (k, i, 0))),
        out_shape=jax.ShapeDtypeStruct((4, R, C), g.dtype), compiler_params=_params(("arbitrary", "arbitrary")),
    )(c, g, got)


def rs_exchange_chips(ps, *, name):
    n = len(ps)

    def body(*refs):
        p_refs, out_refs, send_sems, recv_sems = refs[:n], refs[n:2 * n], refs[2 * n], refs[2 * n + 1]
        me = _me()
        copies = []
        for i in range(n):
            for j, rel in enumerate(CHIP_RELS):
                peer = _flip(me, rel + (0,))
                cp = pltpu.make_async_remote_copy(
                    src_ref=p_refs[i].at[2 * peer[0] + peer[1]], dst_ref=out_refs[i].at[j], send_sem=send_sems.at[i, j],
                    recv_sem=recv_sems.at[i, j], device_id=peer, device_id_type=MESH)
                cp.start()
                copies.append(cp)
        for cp in copies:
            cp.wait()

    return pl.pallas_call(
        body, name=name, in_specs=[HBM_SPEC] * n, out_specs=[HBM_SPEC] * n,
        out_shape=[jax.ShapeDtypeStruct((3,) + p.shape[1:], p.dtype) for p in ps],
        scratch_shapes=[pltpu.SemaphoreType.DMA((n, 3)), pltpu.SemaphoreType.DMA((n, 3))],
    )(*ps)


def rs_final_sum(p, got, *, name):
    _, R, C = p.shape
    tr = _tile(R, 768)

    def body(chip_ref, p_ref, got_ref, o_ref):
        s = p_ref[...].astype(F32)
        for j in range(3):
            s = s + got_ref[j].astype(F32)
        o_ref[...] = s

    mychip = jnp.reshape(2 * lax.axis_index("x") + lax.axis_index("y"), (1,)).astype(jnp.int32)
    return pl.pallas_call(
        body, name=name,
        grid_spec=pltpu.PrefetchScalarGridSpec(
            num_scalar_prefetch=1, grid=(R // tr,),
            in_specs=[pl.BlockSpec((None, tr, C), lambda i, chip_ref: (chip_ref[0], i, 0)),
                      pl.BlockSpec((3, tr, C), lambda i, chip_ref: (0, i, 0))],
            out_specs=pl.BlockSpec((tr, C), lambda i, chip_ref: (i, 0))),
        out_shape=jax.ShapeDtypeStruct((R, C), F32), compiler_params=_params(("arbitrary",)),
    )(mychip, p, got)


def reduce_scatter_big(gs):
    got = rs_exchange_sibling(gs, name="rs_sibling")
    parts = [rs_chip_partials(g, t, name=f"rs_chip_partials_{i}") for i, (g, t) in enumerate(zip(gs, got))]
    got2 = rs_exchange_chips(parts, name="rs_chips")
    return [rs_final_sum(p, t, name=f"rs_final_sum_{i}") for i, (p, t) in enumerate(zip(parts, got2))]


def _blk(off):
    return pl.BlockSpec((BLK, LANES), lambda h, n: (n, off + h))


def _const_spec(shape):
    return pl.BlockSpec(shape, lambda *_: (0,) * len(shape))


def retention_tables(T):
    pos = jnp.arange(T, dtype=F32) - float(PAD)
    inv_freq = 1.0 / (10000.0 ** jnp.linspace(0.0, 1.0, 64, dtype=F32))
    ang = pos[:, None] * inv_freq[None, :]
    cos = jnp.repeat(jnp.cos(ang), 2, axis=1)
    sin = jnp.repeat(jnp.sin(ang), 2, axis=1) * jnp.tile(jnp.array([-1.0, 1.0], F32), 64)[None, :]
    lane = np.arange(LANES)
    perm = jnp.broadcast_to(jnp.asarray((lane[:, None] == (lane[None, :] ^ 1)).astype(np.float32)), (4, LANES, LANES))
    log_gamma = jnp.log1p(-jnp.exp2(-5.0 - jnp.arange(4, dtype=F32)))
    idx = jnp.arange(BLK, dtype=F32)
    diff = idx[:, None] - idx[None, :]
    intra = jnp.where(diff >= 0, jnp.exp(jnp.maximum(diff, 0.0) * log_gamma[:, None, None]), 0.0)
    zeta = jnp.exp((BLK - 1.0 - idx)[None, :] * log_gamma[:, None])
    xi = jnp.exp((idx + 1.0)[None, :] * log_gamma[:, None])
    bc = lambda t: jnp.broadcast_to(t[:, :, None], (4, BLK, LANES))
    return cos, sin, perm, intra, bc(zeta), bc(xi)


def _heads(x):
    return jnp.stack([x[:, h * LANES:(h + 1) * LANES] for h in range(4)])


def _unheads(y):
    return jnp.concatenate([y[h] for h in range(4)], axis=1)


def _ret_chunk(rq, rk, rv, rg, S, cos, sin, intra, zeta, xi, perm):
    q = rq * cos + hdot(rq, perm) * sin
    k = (rk * cos + hdot(rk, perm) * sin) * (128.0 ** -0.5)
    ret = bdot(bdot(q, _t(k)) * intra, rv) + bdot(q * xi, S)
    S_new = S * xi[..., BLK - 1:BLK, :] + bdot(_t(k * zeta), rv)
    c = ret - jnp.mean(ret, axis=-1, keepdims=True)
    out = c * lax.rsqrt(jnp.mean(c * c, axis=-1, keepdims=True) + EPS) * _silu(rg)
    return out, S_new


def _wide(off):
    return pl.BlockSpec((BLK, 4 * LANES), lambda n: (n, off))


def retention_fwd(p, tables, *, name):
    T = p.shape[0]
    N = T // BLK
    cos, sin, perm, intra, zeta, xi = tables

    def body(rq, rk, rv, rg, cos_ref, sin_ref, in_ref, ze_ref, xi_ref, perm_ref, out_ref, sall_ref, s_scr):
        @pl.when(pl.program_id(0) == 0)
        def _():
            s_scr[...] = jnp.zeros_like(s_scr)

        S = s_scr[...]
        sall_ref[...] = S
        out, S_new = _ret_chunk(_heads(rq[...]), _heads(rk[...]), _heads(rv[...]), _heads(rg[...]), S, cos_ref[...],
                                sin_ref[...], in_ref[...], ze_ref[...], xi_ref[...], perm_ref[...])
        out_ref[...] = _unheads(out).astype(BF16)
        s_scr[...] = S_new

    rowtab = pl.BlockSpec((BLK, LANES), lambda n: (n, 0))
    tab = _const_spec((4, BLK, LANES))
    return pl.pallas_call(
        body, name=name, grid=(N,),
        in_specs=[_wide(0), _wide(1), _wide(2), _wide(3), rowtab, rowtab, tab, tab, tab, tab],
        out_specs=[_wide(0), pl.BlockSpec((None, 4, LANES, LANES), lambda n: (n, 0, 0, 0))],
        out_shape=[jax.ShapeDtypeStruct((T, 512), BF16), jax.ShapeDtypeStruct((N, 4, LANES, LANES), F32)],
        scratch_shapes=[pltpu.VMEM((4, LANES, LANES), F32)],
        compiler_params=_params(("arbitrary",)),
    )(p, p, p, p, cos, sin, intra, zeta, xi, perm)


def _row_mask(n):
    return (n * BLK + _iota2((BLK, 1), 0) >= PAD).astype(F32)


def retention_bwd(p, sall, dmixed, tables, *, name):
    T = p.shape[0]
    N = T // BLK
    cos, sin, perm, intra, zeta, xi = tables

    def body(rq, rk, rv, rg, cos_ref, sin_ref, in_ref, ze_ref, xi_ref, perm_ref, sall_ref, do_ref, drq, drk, drv, drg,
             ds_scr):
        n = N - 1 - pl.program_id(0)

        @pl.when(pl.program_id(0) == 0)
        def _():
            ds_scr[...] = jnp.zeros_like(ds_scr)

        f = lambda a, b, c, d, s: _ret_chunk(a, b, c, d, s, cos_ref[...], sin_ref[...], in_ref[...], ze_ref[...],
                                             xi_ref[...], perm_ref[...])
        _, vjp = jax.vjp(f, _heads(rq[...]), _heads(rk[...]), _heads(rv[...]), _heads(rg[...]), sall_ref[...])
        g = vjp((_heads(do_ref[...]), ds_scr[...]))
        mask = _row_mask(n)
        for ref, val in zip((drq, drk, drv, drg), g[:4]):
            ref[...] = _unheads(val) * mask
        ds_scr[...] = g[4]

    def rwide(off):
        return pl.BlockSpec((BLK, 4 * LANES), lambda n: (N - 1 - n, off))

    rowtab = pl.BlockSpec((BLK, LANES), lambda n: (N - 1 - n, 0))
    tab = _const_spec((4, BLK, LANES))
    return pl.pallas_call(
        body, name=name, grid=(N,),
        in_specs=[rwide(0), rwide(1), rwide(2), rwide(3), rowtab, rowtab, tab, tab, tab, tab,
                  pl.BlockSpec((None, 4, LANES, LANES), lambda n: (N - 1 - n, 0, 0, 0)), rwide(0)],
        out_specs=[rwide(0)] * 4, out_shape=[jax.ShapeDtypeStruct((T, 512), F32)] * 4,
        scratch_shapes=[pltpu.VMEM((4, LANES, LANES), F32)],
        compiler_params=_params(("arbitrary",)),
    )(p, p, p, p, cos, sin, intra, zeta, xi, perm, sall, dmixed)


def conv_silu_fwd(p, w, *, name):
    T = p.shape[0]
    N = T // BLK

    def body(x_ref, xp_ref, w_ref, o_ref):
        n = pl.program_id(0)
        cur = x_ref[...]
        cat = jnp.concatenate([jnp.where(n > 0, xp_ref[...], 0.0), cur], axis=0)
        y = w_ref[3:4, :] * cur
        for s in (1, 2, 3):
            y = y + w_ref[3 - s:4 - s, :] * pltpu.roll(cat, s, 0)[BLK:]
        o_ref[...] = _silu(y)

    cw = 4 * LANES
    return pl.pallas_call(
        body, name=name, grid=(N, 3),
        in_specs=[pl.BlockSpec((BLK, cw), lambda n, c: (n, 4 + c)),
                  pl.BlockSpec((BLK, cw), lambda n, c: (jnp.maximum(n - 1, 0), 4 + c)),
                  pl.BlockSpec((4, cw), lambda n, c: (0, c))],
        out_specs=pl.BlockSpec((BLK, cw), lambda n, c: (n, c)),
        out_shape=jax.ShapeDtypeStruct((T, 1536), F32), compiler_params=_params(("arbitrary", "arbitrary")),
    )(p, p, w)


def conv_silu_bwd(p, w, dact, part, *, name):
    T = p.shape[0]
    N = T // BLK
    cw = 4 * LANES

    def body(xp_ref, x_ref, xn_ref, w_ref, da_ref, dan_ref, dx_ref, dw_ref):
        n = pl.program_id(0)
        last = n == N - 1
        cat = jnp.concatenate([jnp.where(n > 0, xp_ref[...], 0.0), x_ref[...], jnp.where(last, 0.0, xn_ref[...])], axis=0)
        shifted = [cat] + [pltpu.roll(cat, s, 0) for s in (1, 2, 3)]
        y = w_ref[3:4, :] * shifted[0]
        for s in (1, 2, 3):
            y = y + w_ref[3 - s:4 - s, :] * shifted[s]
        y = y[BLK:]
        da = jnp.concatenate([da_ref[...], jnp.where(last, 0.0, dan_ref[...])], axis=0)
        sg = _sigmoid(y)
        dy = da * sg * (1.0 + y * (1.0 - sg))
        dx = w_ref[3:4, :] * dy[:BLK]
        for s in (1, 2, 3):
            dx = dx + w_ref[3 - s:4 - s, :] * pltpu.roll(dy, 2 * BLK - s, 0)[:BLK]
        dx_ref[...] = dx * _row_mask(n)

        @pl.when(n == 0)
        def _():
            dw_ref[...] = jnp.zeros_like(dw_ref)

        for s in (0, 1, 2, 3):
            dw_ref[3 - s:4 - s, :] += jnp.sum(dy[:BLK] * shifted[s][BLK:2 * BLK], axis=0, keepdims=True)

    def xs(d):
        return pl.BlockSpec((BLK, cw), lambda n: (jnp.clip(n + d, 0, N - 1), 4 + part))

    return pl.pallas_call(
        body, name=name, grid=(N,),
        in_specs=[xs(-1), xs(0), xs(1), pl.BlockSpec((4, cw), lambda n: (0, part)),
                  pl.BlockSpec((BLK, cw), lambda n: (n, 0)),
                  pl.BlockSpec((BLK, cw), lambda n: (jnp.minimum(n + 1, N - 1), 0))],
        out_specs=[pl.BlockSpec((BLK, cw), lambda n: (n, 0)), pl.BlockSpec((4, cw), lambda n: (0, 0))],
        out_shape=[jax.ShapeDtypeStruct((T, 512), F32), jax.ShapeDtypeStruct((4, 512), F32)],
        compiler_params=_params(("arbitrary",)),
    )(p, p, p, w, dact, dact)


def _softplus(x):
    return jnp.maximum(x, 0.0) + jnp.log1p(jnp.exp(-jnp.abs(x)))


def _pick4(tile, off):
    return jnp.stack([_lane_pick(tile, off + h) for h in range(4)])


def _spread4(v4, off, rows):
    lane = _iota2((rows, LANES), 1)
    out = jnp.where(lane == off, v4[0], 0.0)
    for h in range(1, 4):
        out = out + jnp.where(lane == off + h, v4[h], 0.0)
    return out


def _gdn_chunk(qa, ka, va, z, braw, araw, S, alog, dtb, onorm, rowmask, lincl):
    r, c = _iota2((BLK, BLK), 0), _iota2((BLK, BLK), 1)
    incl, strict = r >= c, r > c
    eye = (r == c).astype(F32)
    q = qa * lax.rsqrt(jnp.sum(qa * qa, axis=-1, keepdims=True) + EPS) * (128.0 ** -0.5)
    k = ka * lax.rsqrt(jnp.sum(ka * ka, axis=-1, keepdims=True) + EPS)
    beta = _sigmoid(braw) * rowmask
    g = -jnp.exp(alog) * _softplus(araw + dtb) * rowmask
    gc = hdot(lincl, jnp.broadcast_to(g, qa.shape))
    decay = jnp.where(incl, jnp.exp(jnp.where(incl, gc - _t(gc), 0.0)), 0.0)
    kb = k * beta
    amat = jnp.where(strict, bdot(kb, _t(k)) * decay, 0.0)
    m = -amat
    inv = eye + m
    pw = hdot(m, m)
    for t in range(6):
        inv = inv + hdot(inv, pw)
        if t < 5:
            pw = hdot(pw, pw)
    egc = jnp.exp(gc)
    u = hdot(inv, va * beta)
    w = hdot(inv, kb * egc)
    qk = jnp.where(incl, bdot(q, _t(k)) * decay, 0.0)
    glast = gc[..., BLK - 1:BLK, :]
    vnew = u - bdot(w, S)
    o = bdot(q * egc, S) + bdot(qk, vnew)
    S_new = S * jnp.exp(glast) + bdot(_t(k * jnp.exp(glast - gc)), vnew)
    out = o * lax.rsqrt(jnp.mean(o * o, axis=-1, keepdims=True) + EPS) * onorm * _silu(z)
    return out, S_new


def _lincl():
    i = np.arange(BLK)
    return jnp.broadcast_to(jnp.asarray((i[:, None] >= i[None, :]).astype(np.float32)), (4, BLK, BLK))


def gdn_fwd(act, p, alog, dtb, onorm, *, name):
    T = p.shape[0]
    N = T // BLK

    def body(qa, ka, va, z, ba, alog_ref, dtb_ref, on_ref, l_ref, out_ref, sall_ref, s_scr):
        n = pl.program_id(0)

        @pl.when(n == 0)
        def _():
            s_scr[...] = jnp.zeros_like(s_scr)

        S = s_scr[...]
        sall_ref[...] = S
        out, S_new = _gdn_chunk(_heads(qa[...]), _heads(ka[...]), _heads(va[...]), _heads(z[...]), _pick4(ba[...], 0),
                                _pick4(ba[...], 4), S, _pick4(alog_ref[...], 0), _pick4(dtb_ref[...], 0), on_ref[...],
                                _row_mask(n), l_ref[...])
        out_ref[...] = _unheads(out).astype(BF16)
        s_scr[...] = S_new

    vec = _const_spec((1, LANES))
    return pl.pallas_call(
        body, name=name, grid=(N,),
        in_specs=[_wide(0), _wide(1), _wide(2), _wide(7), pl.BlockSpec((BLK, LANES), lambda n: (n, 32)), vec, vec, vec,
                  _const_spec((4, BLK, BLK))],
        out_specs=[_wide(0), pl.BlockSpec((None, 4, LANES, LANES), lambda n: (n, 0, 0, 0))],
        out_shape=[jax.ShapeDtypeStruct((T, 512), BF16), jax.ShapeDtypeStruct((N, 4, LANES, LANES), F32)],
        scratch_shapes=[pltpu.VMEM((4, LANES, LANES), F32)],
        compiler_params=_params(("arbitrary",)),
    )(act, act, act, p, p, alog, dtb, onorm, _lincl())


def gdn_bwd(act, p, alog, dtb, onorm, sall, dmixed, *, name):
    T = p.shape[0]
    N = T // BLK

    def body(qa, ka, va, z, ba, alog_ref, dtb_ref, on_ref, l_ref, sall_ref, do_ref,
             dq_ref, dk_ref, dv_ref, dz_ref, dba_ref, dal_ref, ddt_ref, don_ref, ds_scr):
        step = pl.program_id(0)
        n = N - 1 - step

        @pl.when(step == 0)
        def _():
            ds_scr[...] = jnp.zeros_like(ds_scr)
            dal_ref[...] = jnp.zeros_like(dal_ref)
            ddt_ref[...] = jnp.zeros_like(ddt_ref)
            don_ref[...] = jnp.zeros_like(don_ref)

        rowmask, lincl = _row_mask(n), l_ref[...]
        f = lambda *a: _gdn_chunk(*a, rowmask, lincl)
        _, vjp = jax.vjp(f, _heads(qa[...]), _heads(ka[...]), _heads(va[...]), _heads(z[...]), _pick4(ba[...], 0),
                         _pick4(ba[...], 4), sall_ref[...], _pick4(alog_ref[...], 0), _pick4(dtb_ref[...], 0),
                         on_ref[...])
        g = vjp((_heads(do_ref[...]), ds_scr[...]))
        dq_ref[...] = _unheads(g[0]) * rowmask
        dk_ref[...] = _unheads(g[1]) * rowmask
        dv_ref[...] = _unheads(g[2]) * rowmask
        dz_ref[...] = _unheads(g[3]) * rowmask
        dba_ref[...] = (_spread4(g[4], 0, BLK) + _spread4(g[5], 4, BLK)) * rowmask
        ds_scr[...] = g[6]
        dal_ref[...] += _spread4(g[7], 0, 1)
        ddt_ref[...] += _spread4(g[8], 0, 1)
        don_ref[...] += g[9]

    def rwide(off):
        return pl.BlockSpec((BLK, 4 * LANES), lambda s: (N - 1 - s, off))

    vec = _const_spec((1, LANES))
    col = pl.BlockSpec((BLK, LANES), lambda s: (N - 1 - s, 0))
    return pl.pallas_call(
        body, name=name, grid=(N,),
        in_specs=[rwide(0), rwide(1), rwide(2), rwide(7), pl.BlockSpec((BLK, LANES), lambda s: (N - 1 - s, 32)), vec, vec,
                  vec, _const_spec((4, BLK, BLK)),
                  pl.BlockSpec((None, 4, LANES, LANES), lambda s: (N - 1 - s, 0, 0, 0)), rwide(1)],
        out_specs=[rwide(0)] * 4 + [col, vec, vec, vec],
        out_shape=[jax.ShapeDtypeStruct((T, 512), F32)] * 4 + [jax.ShapeDtypeStruct((T, LANES), F32)]
        + [jax.ShapeDtypeStruct((1, LANES), F32)] * 3,
        scratch_shapes=[pltpu.VMEM((4, LANES, LANES), F32)],
        compiler_params=_params(("arbitrary",)),
    )(act, act, act, p, p, alog, dtb, onorm, _lincl(), sall, dmixed)


NEG = -1e30


def _swa_block(q, k0, kp, kc, v0, vp, vc, sink, n):
    r, c = _iota2((BLK, BLK), 0), _iota2((BLK, BLK), 1)
    m0 = (c >= PAD) & (c <= n * BLK + r)
    mp = (n >= 2) & (c > r)
    mc = (n >= 1) & (r >= c)
    b = lambda t: jnp.broadcast_to(t, (4,) + t.shape)
    qs = q * (64.0 ** -0.5)
    s0 = jnp.where(m0, bdot(qs, _t(b(k0))), NEG)
    sp = jnp.where(mp, bdot(qs, _t(b(kp))), NEG)
    sc = jnp.where(mc, bdot(qs, _t(b(kc))), NEG)
    mx = jnp.maximum(jnp.max(jnp.maximum(jnp.maximum(s0, sp), sc), axis=-1, keepdims=True), sink)
    mx = lax.stop_gradient(mx)
    p0, pp, pc = jnp.exp(s0 - mx), jnp.exp(sp - mx), jnp.exp(sc - mx)
    den = (jnp.sum(p0, axis=-1, keepdims=True) + jnp.sum(pp, axis=-1, keepdims=True)
           + jnp.sum(pc, axis=-1, keepdims=True) + jnp.exp(sink - mx))
    return (bdot(p0, b(v0)) + bdot(pp, b(vp)) + bdot(pc, b(vc))) / den


def _swa_specs():
    rows = (lambda n: 0, lambda n: jnp.maximum(n - 1, 0), lambda n: n)

    def kv_spec(off, row):
        return pl.BlockSpec((BLK, LANES), lambda g, n: (row(n), off + g))

    q = pl.BlockSpec((BLK, 4 * LANES), lambda g, n: (n, g))
    return q, [kv_spec(off, row) for off in (8, 10) for row in rows]


def swa_fwd(p2, sinkrow, *, name):
    T = p2.shape[0]
    N = T // BLK

    def body(q, k0, kp, kc, v0, vp, vc, sink_ref, o_ref):
        g, n = pl.program_id(0), pl.program_id(1)
        f32 = lambda ref: ref[...].astype(F32)
        o = _swa_block(_heads(f32(q)), f32(k0), f32(kp), f32(kc), f32(v0), f32(vp), f32(vc),
                       _pick4(sink_ref[...], 4 * g), n)
        o_ref[...] = _unheads(o).astype(BF16)

    q, kv = _swa_specs()
    return pl.pallas_call(
        body, name=name, grid=(2, N), in_specs=[q] + kv + [_const_spec((1, LANES))],
        out_specs=q, out_shape=jax.ShapeDtypeStruct((T, 1024), BF16),
        compiler_params=_params(("arbitrary", "arbitrary")),
    )(p2, p2, p2, p2, p2, p2, p2, sinkrow)


def swa_bwd(p2, sinkrow, dmixed, *, name):
    T = p2.shape[0]
    N = T // BLK

    def body(q, k0, kp, kc, v0, vp, vc, sink_ref, do_ref, dq_ref, dk_ref, dv_ref, dsink_ref):
        g, n = pl.program_id(0), pl.program_id(1)

        @pl.when(n == 0)
        def _():
            dk_ref[...] = jnp.zeros_like(dk_ref)
            dv_ref[...] = jnp.zeros_like(dv_ref)

        @pl.when((g == 0) & (n == 0))
        def _():
            dsink_ref[...] = jnp.zeros_like(dsink_ref)

        f = lambda *a: _swa_block(*a, n)
        f32 = lambda ref: ref[...].astype(F32)
        _, vjp = jax.vjp(f, _heads(f32(q)), f32(k0), f32(kp), f32(kc), f32(v0), f32(vp), f32(vc),
                         _pick4(sink_ref[...], 4 * g))
        dq, dk0, dkp, dkc, dv0, dvp, dvc, dsink = vjp(_heads(do_ref[...]))
        dq_ref[...] = _unheads(dq)
        prev = pl.ds(pl.multiple_of(jnp.maximum(n - 1, 0) * BLK, BLK), BLK)
        cur = pl.ds(pl.multiple_of(n * BLK, BLK), BLK)
        for ref, d0, dp, dc in ((dk_ref, dk0, dkp, dkc), (dv_ref, dv0, dvp, dvc)):
            ref[0:BLK, :] += d0
            ref[prev, :] += dp
            ref[cur, :] += dc
        dsink_ref[...] += _spread4(dsink, 4 * g, 1)

    qspec, kv = _swa_specs()
    slab = pl.BlockSpec((T, LANES), lambda g, n: (0, g))
    return pl.pallas_call(
        body, name=name, grid=(2, N), in_specs=[qspec] + kv + [_const_spec((1, LANES)), qspec],
        out_specs=[qspec, slab, slab, _const_spec((1, LANES))],
        out_shape=[jax.ShapeDtypeStruct((T, 1024), F32), jax.ShapeDtypeStruct((T, 256), F32),
                   jax.ShapeDtypeStruct((T, 256), F32), jax.ShapeDtypeStruct((1, LANES), F32)],
        compiler_params=_params(("arbitrary", "arbitrary")),
    )(p2, p2, p2, p2, p2, p2, p2, sinkrow, dmixed)


def _split_dot(x, m):
    rows = x.shape[0]
    hi = x.astype(BF16)
    lo = (x - hi.astype(F32)).astype(BF16)
    r = _nn(jnp.concatenate([hi, lo], axis=0), m)
    return r[:rows] + r[rows:]


def _tri_and_ones(strict, ones=True):
    i = np.arange(BLK)
    m = (i[:, None] > i[None, :]) if strict else (i[:, None] >= i[None, :])
    if ones:
        m = np.concatenate([m, np.ones((BLK, BLK), bool)], axis=1)
    return jnp.asarray(m.astype(np.float32), dtype=BF16)


def _later_and_row_sums(x, m):
    r = _split_dot(x, m)
    if m.shape[1] == 2 * BLK:
        return r[:, :BLK], r[:, BLK:]
    return r, jnp.broadcast_to(jnp.sum(x, axis=1, keepdims=True), x.shape)


SB_PAIR = 2
SB_FWD_GROUP = 4


def _sb_positions():
    r, s = _iota2((BLK, BLK), 0), _iota2((BLK, BLK), 1)
    return s - r, s


def _sb_weights(qbs, ks, base, n, pos, carries, after):
    nh, kb = len(qbs), len(ks[0])
    zs = [[_nt(qbs[h], ks[h][c]) for c in range(kb)] for h in range(nh)]
    valid = [(pos[0] < (n - base - c) * BLK) & (pos[1] >= PAD - (base + c) * BLK) for c in range(kb)]
    lb = [[None] * kb for _ in range(nh)]
    sums = [[None] * kb for _ in range(nh)]
    for c in range(kb):
        for h in range(nh):
            z = zs[h][c]
            lb[h][c] = jnp.minimum(z, 0.0) - jnp.log(1.0 + jnp.exp(-jnp.abs(z)))
            sums[h][c] = _later_and_row_sums(jnp.where(valid[c], lb[h][c] - z, 0.0), after)
    a = [[None] * kb for _ in range(nh)]
    carries = list(carries)
    for c in reversed(range(kb)):
        for h in range(nh):
            a[h][c] = jnp.where(valid[c], jnp.exp(lb[h][c] + carries[h] + sums[h][c][0]), 0.0)
            carries[h] = carries[h] + sums[h][c][1]
    return valid, lb, a, carries


def _key_blocks(n_blocks):
    return next(k for k in (5, 3, 1) if n_blocks % k == 0)


def sb_fwd(p2, *, name):
    T = p2.shape[0]
    N = T // BLK
    kb = _key_blocks(N)
    nh = SB_FWD_GROUP
    heads = [slice(h * LANES, (h + 1) * LANES) for h in range(nh)]

    def body(q_ref, k_ref, v_ref, after_ref, o_ref, of_ref):
        n = pl.program_id(1)
        qbs = [(q_ref[:, hs].astype(F32) * (64.0 ** -0.5)).astype(BF16) for hs in heads]
        after, pos = after_ref[...], _sb_positions()
        nsup = n // kb + 1

        def step(t, c):
            accs, carries = c
            base = (nsup - 1 - t) * kb
            rows = [pl.ds(pl.multiple_of((base + sub) * BLK, BLK), BLK) for sub in range(kb)]
            ks = [[k_ref[r, hs] for r in rows] for hs in heads]
            _, _, a, carries = _sb_weights(qbs, ks, base, n, pos, carries, after)
            accs = list(accs)
            for sub, r in enumerate(rows):
                for h, hs in enumerate(heads):
                    accs[h] = accs[h] + _nn(a[h][sub].astype(BF16), v_ref[r, hs])
            return accs, carries

        zero = [jnp.zeros((BLK, LANES), F32)] * nh
        accs, _ = lax.fori_loop(0, nsup, step, (zero, zero))
        acc = jnp.concatenate(accs, axis=1)
        o_ref[...] = acc.astype(BF16)
        of_ref[...] = acc

    wide = nh * LANES

    def slab(off):
        return pl.BlockSpec((T, wide), lambda g, n: (0, off + g))

    def blk(off):
        return pl.BlockSpec((BLK, wide), lambda g, n: (n, off + g))

    return pl.pallas_call(
        body, name=name, grid=(8 // nh, N),
        in_specs=[blk(12 // nh), slab(20 // nh), slab(28 // nh), _const_spec((BLK, BLK))],
        out_specs=[blk(0), blk(0)],
        out_shape=[jax.ShapeDtypeStruct((T, 1024), BF16), jax.ShapeDtypeStruct((T, 1024), F32)],
        compiler_params=_params(("arbitrary", "arbitrary")),
    )(p2, p2, p2, _tri_and_ones(True, ones=False))


def sb_bwd(p2, o, dmixed, *, name):
    T = p2.shape[0]
    N = T // BLK
    kb = _key_blocks(N)

    heads = [slice(h * LANES, (h + 1) * LANES) for h in range(SB_PAIR)]
    scale = 64.0 ** -0.5

    def body(q_ref, k_ref, v_ref, after_ref, from_ref, o_ref, do_ref, dq_ref, dk_ref, dv_ref, dkt_scr, dvt_scr):
        n = pl.program_id(1)

        @pl.when(n == 0)
        def _():
            dkt_scr[...] = jnp.zeros_like(dkt_scr)
            dvt_scr[...] = jnp.zeros_like(dvt_scr)

        qbs, qts, dobs, dots, totals = [], [], [], [], []
        for hs in heads:
            qs = q_ref[:, hs].astype(F32) * scale
            do = do_ref[:, hs]
            qbs.append(qs.astype(BF16))
            qts.append(qs.T.astype(BF16))
            dobs.append(do.astype(BF16))
            dots.append(do.T.astype(BF16))
            total = jnp.sum(dobs[-1].astype(F32) * o_ref[:, hs], axis=1, keepdims=True)
            totals.append(jnp.broadcast_to(total, (BLK, LANES)))
        after, frm, pos = after_ref[...], from_ref[...], _sb_positions()
        nsup = n // kb + 1

        def step(t, c):
            dqs, carries, gcarries = c
            base = (nsup - 1 - t) * kb
            rows = [pl.ds(pl.multiple_of((base + sub) * BLK, BLK), BLK) for sub in range(kb)]
            ks = [[k_ref[r, hs] for r in rows] for hs in heads]
            valid, lb, a, carries = _sb_weights(qbs, ks, base, n, pos, carries, after)
            das = [[_nt(dobs[h], v_ref[r, hs]) for r in rows] for h, hs in enumerate(heads)]
            ab = [[a[h][sub].astype(BF16) for sub in range(kb)] for h in range(SB_PAIR)]
            g = [[None] * kb for _ in heads]
            sums = [[None] * kb for _ in heads]
            for sub in range(kb):
                for h in range(SB_PAIR):
                    g[h][sub] = das[h][sub] * ab[h][sub].astype(F32)
                    sums[h][sub] = _later_and_row_sums(g[h][sub], frm)
            dqs, gcarries = list(dqs), list(gcarries)
            for sub in reversed(range(kb)):
                for h in range(SB_PAIR):
                    before = totals[h] - (gcarries[h] + sums[h][sub][0])
                    gcarries[h] = gcarries[h] + sums[h][sub][1]
                    beta = jnp.exp(lb[h][sub])
                    dz = jnp.where(valid[sub], g[h][sub] - beta * (g[h][sub] + before), 0.0).astype(BF16)
                    dqs[h] = dqs[h] + _nn(dz, ks[h][sub])
                    dkt_scr[h * N + base + sub] += _nn(qts[h], dz)
                    dvt_scr[h * N + base + sub] += _nn(dots[h], ab[h][sub])
            return dqs, carries, gcarries

        zero = [jnp.zeros((BLK, LANES), F32)] * SB_PAIR
        dqs, _, _ = lax.fori_loop(0, nsup, step, (zero, zero, zero))
        dq_ref[...] = (jnp.concatenate(dqs, axis=1) * scale).astype(dq_ref.dtype)

        @pl.when(n == N - 1)
        def _():
            def flush(j, _):
                rows = pl.ds(pl.multiple_of(j * BLK, BLK), BLK)
                for h, hs in enumerate(heads):
                    dk_ref[rows, hs] = dkt_scr[h * N + j].T.astype(dk_ref.dtype)
                    dv_ref[rows, hs] = dvt_scr[h * N + j].T.astype(dv_ref.dtype)
                return 0

            lax.fori_loop(0, N, flush, 0)

    wide = SB_PAIR * LANES

    def slab(off):
        return pl.BlockSpec((T, wide), lambda g, n: (0, off + g))

    def blk(off):
        return pl.BlockSpec((BLK, wide), lambda g, n: (n, off + g))

    tri = _const_spec((BLK, BLK))
    return pl.pallas_call(
        body, name=name, grid=(8 // SB_PAIR, N),
        in_specs=[blk(12 // SB_PAIR), slab(20 // SB_PAIR), slab(28 // SB_PAIR), tri, tri, blk(0), blk(8 // SB_PAIR)],
        out_specs=[blk(0), slab(0), slab(0)],
        out_shape=[jax.ShapeDtypeStruct((T, 1024), BF16)] * 3,
        scratch_shapes=[pltpu.VMEM((SB_PAIR * N, LANES, LANES), F32), pltpu.VMEM((SB_PAIR * N, LANES, LANES), F32)],
        compiler_params=_params(("arbitrary", "arbitrary")),
    )(p2, p2, p2, _tri_and_ones(True, ones=False), _tri_and_ones(False, ones=False), o, dmixed)


def ffn_fwd(h, g_pre, g_post, wg, wu, wd, tag):
    u, gate, up, act = norm_mm(h, g_pre, (wg, wu), swiglu=True, wt=True, name=f"ffn_up_{tag}")
    y, h_new = mm_norm_res([act], [wd], h, g_post, 0.5, name=f"ffn_down_{tag}")
    return h_new, (h, u, gate, up, y)


def ffn_bwd(saved, dh, g_pre, g_post, wg, wu, wd, tag):
    h, u, gate, up, y = saved
    dy, dg_post, dgate, dup, act = normbwd_mm_nt(dh, y, g_post, wd, 0.5, (gate, up), name=f"ffn_bwd_down_{tag}")
    dwd = mm_tn(act, dy, name=f"ffn_dwd_{tag}")
    dwg = mm_tn(dgate, u, name=f"ffn_dwg_{tag}")
    dwu = mm_tn(dup, u, name=f"ffn_dwu_{tag}")
    dh_in, dg_pre = mm_nt_normbwd([dgate, dup], [wg, wu], h, g_pre, dh, wt=True, name=f"ffn_bwd_up_{tag}")
    return dh_in, (dg_pre, dg_post), (dwg, dwu, dwd)


def _lane_row(v):
    v = v.reshape(1, -1)
    return jnp.pad(v, ((0, 0), (0, LANES - v.shape[1])))


AB_WIDTHS = (512,) * 8 + (LANES,)


def mixer_ab_fwd(h, g_pre, g_post, w_in, conv_w, a_log, dt_bias, out_norm, w_out, tables):
    u, p = norm_mm(h, g_pre, (w_in,), swiglu=False, name="ab_in")
    ret, sall_r = retention_fwd(p, tables, name="retention_fwd")
    act = conv_silu_fwd(p, conv_w, name="conv_fwd")
    gdn, sall_g = gdn_fwd(act, p, _lane_row(a_log), _lane_row(dt_bias), out_norm.reshape(1, LANES), name="gdn_fwd")
    y, h_new = mm_norm_res([ret, gdn], [w_out[:512], w_out[512:]], h, g_post, 1.0, name="ab_out")
    return h_new, (h, u, p, ret, sall_r, act, gdn, sall_g, y)


def mixer_ab_bwd(saved, dh, g_pre, g_post, w_in, conv_w, a_log, dt_bias, out_norm, w_out, tables):
    h, u, p, ret, sall_r, act, gdn, sall_g, y = saved
    dy, dg_post, dmixed = normbwd_mm_nt(dh, y, g_post, w_out, 1.0, name="ab_bwd_out")
    dw_out = jnp.concatenate([mm_tn(ret, dy, name="ab_dwout_ret"), mm_tn(gdn, dy, name="ab_dwout_gdn")], axis=0)
    pieces = list(retention_bwd(p, sall_r, dmixed, tables, name="retention_bwd"))
    dqa, dka, dva, dz, dba, dalog, ddtb, donorm = gdn_bwd(
        act, p, _lane_row(a_log), _lane_row(dt_bias), out_norm.reshape(1, LANES), sall_g, dmixed, name="gdn_bwd")
    dconv = []
    for part, dact in enumerate((dqa, dka, dva)):
        dx, dw = conv_silu_bwd(p, conv_w, dact, part, name=f"conv_bwd_{part}")
        pieces.append(dx)
        dconv.append(dw)
    pieces += [dz, dba]
    offs = np.cumsum((0,) + AB_WIDTHS)
    w_parts = [w_in[:, a:b] for a, b in zip(offs[:-1], offs[1:])]
    dh_in, dg_pre = mm_nt_normbwd(pieces, w_parts, h, g_pre, dh, name="ab_bwd_in")
    dw_in = jnp.concatenate([mm_tn(u, pc, name=f"ab_dwin_{i}") for i, pc in enumerate(pieces)], axis=1)
    small = (jnp.concatenate(dconv, axis=1), dalog[:, :4], ddtb[:, :4], donorm)
    return dh_in, (dg_pre, dg_post), (dw_in, dw_out), small


CD_WIDTHS = (1024, 256, 256, 1024, 1024, 1024)


def mixer_cd_fwd(h, g_pre, g_post, w_in, sinks, w_out):
    u, p2 = norm_mm(h, g_pre, (w_in,), swiglu=False, out_dtype=BF16, name="cd_in")
    swa = swa_fwd(p2, _lane_row(sinks), name="swa_fwd")
    sb, sb_f32 = sb_fwd(p2, name="sb_fwd")
    y, h_new = mm_norm_res([swa, sb], [w_out[:1024], w_out[1024:]], h, g_post, 1.0, name="cd_out")
    return h_new, (h, u, p2, swa, sb, sb_f32, y)


def mixer_cd_bwd(saved, dh, g_pre, g_post, w_in, sinks, w_out):
    h, u, p2, swa, sb, sb_f32, y = saved
    dy, dg_post, dmixed = normbwd_mm_nt(dh, y, g_post, w_out, 1.0, name="cd_bwd_out")
    dw_out = jnp.concatenate([mm_tn(swa, dy, name="cd_dwout_swa"), mm_tn(sb, dy, name="cd_dwout_sb")], axis=0)
    dq_c, dk_c, dv_c, dsink = swa_bwd(p2, _lane_row(sinks), dmixed, name="swa_bwd")
    pieces = [dq_c, dk_c, dv_c] + list(sb_bwd(p2, sb_f32, dmixed, name="sb_bwd"))
    offs = np.cumsum((0,) + CD_WIDTHS)
    w_parts = [w_in[:, a:b] for a, b in zip(offs[:-1], offs[1:])]
    dh_in, dg_pre = mm_nt_normbwd(pieces, w_parts, h, g_pre, dh, name="cd_bwd_in")
    dw_in = jnp.concatenate([mm_tn(u, pc, name=f"cd_dwin_{i}") for i, pc in enumerate(pieces)], axis=1)
    return dh_in, (dg_pre, dg_post), (dw_in, dw_out), dsink[:, :8]


def _pad_heads(w, axis):
    shape = w.shape
    w = w.reshape(shape[:axis] + (shape[axis] // 64, 64) + shape[axis + 1:])
    pad = [(0, 0)] * w.ndim
    pad[axis + 1] = (0, 64)
    return jnp.pad(w, pad).reshape(shape[:axis] + (2 * shape[axis],) + shape[axis + 1:])


def _unpad_heads(w, axis):
    shape = w.shape
    w = w.reshape(shape[:axis] + (shape[axis] // 128, 128) + shape[axis + 1:])
    w = lax.slice_in_dim(w, 0, 64, axis=axis + 1)
    return w.reshape(shape[:axis] + (shape[axis] // 2,) + shape[axis + 1:])


SMALL_SHARDED = (("meta_tokens", (NMETA, LANES), 1), ("norm_gains", (2, 6, LANES), 2), ("ab_conv_w", (1, 4, 192), 2))
SMALL_REPL = (("ab_a_log", (1, 4)), ("ab_dt_bias", (1, 4)), ("ab_out_norm", (1, LANES)), ("cd_sinks", (1, 8)))


def _stack_shards(g, axis):
    full = jnp.moveaxis(g, 0, axis)
    shape = full.shape
    return full.reshape(shape[:axis] + (shape[axis] * shape[axis + 1],) + shape[axis + 2:])


def _split_shards(full, axis):
    shape = full.shape
    g = full.reshape(shape[:axis] + (NDEV, shape[axis] // NDEV) + shape[axis + 1:])
    return jnp.moveaxis(g, axis, 0)


def _pad_rows8(a):
    rows = []
    for x in a:
        flat = x.reshape(x.shape[0], -1)
        n = -(-flat.shape[1] // LANES) * LANES
        rows.append(jnp.pad(flat, ((0, 0), (0, n - flat.shape[1]))).reshape(x.shape[0], n // LANES, LANES))
    cat = jnp.concatenate(rows, axis=1)
    return jnp.pad(cat, ((0, 0), (0, -cat.shape[1] % 8), (0, 0)))


def _unpad_rows8(packed, shapes):
    out, at = [], 0
    for shape in shapes:
        size = int(np.prod(shape))
        nrow = -(-size // LANES)
        blk = packed[:, at:at + nrow].reshape(packed.shape[0], -1)[:, :size]
        out.append(blk.reshape((packed.shape[0],) + tuple(shape)))
        at += nrow
    return out


def kernel(x, meta_tokens, norm_gains, ffn_w_gate, ffn_w_up, ffn_w_down, ab_w_in, ab_conv_w, ab_a_log, ab_dt_bias, ab_out_norm, ab_w_out, cd_w_in, cd_sinks, cd_w_out, loss_target, m_meta_tokens, m_norm_gains, m_ffn_w_gate, m_ffn_w_up, m_ffn_w_down, m_ab_w_in, m_ab_conv_w, m_ab_a_log, m_ab_dt_bias, m_ab_out_norm, m_ab_w_out, m_cd_w_in, m_cd_sinks, m_cd_w_out, v_meta_tokens, v_norm_gains, v_ffn_w_gate, v_ffn_w_up, v_ffn_w_down, v_ab_w_in, v_ab_conv_w, v_ab_a_log, v_ab_dt_bias, v_ab_out_norm, v_ab_w_out, v_cd_w_in, v_cd_sinks, v_cd_w_out):
    w = dict(meta_tokens=meta_tokens, norm_gains=norm_gains, ffn_w_gate=ffn_w_gate, ffn_w_up=ffn_w_up,
             ffn_w_down=ffn_w_down, ab_w_in=ab_w_in, ab_conv_w=ab_conv_w, ab_a_log=ab_a_log, ab_dt_bias=ab_dt_bias,
             ab_out_norm=ab_out_norm, ab_w_out=ab_w_out, cd_w_in=cd_w_in, cd_sinks=cd_sinks, cd_w_out=cd_w_out)
    m = dict(meta_tokens=m_meta_tokens, norm_gains=m_norm_gains, ffn_w_gate=m_ffn_w_gate, ffn_w_up=m_ffn_w_up,
             ffn_w_down=m_ffn_w_down, ab_w_in=m_ab_w_in, ab_conv_w=m_ab_conv_w, ab_a_log=m_ab_a_log,
             ab_dt_bias=m_ab_dt_bias, ab_out_norm=m_ab_out_norm, ab_w_out=m_ab_w_out, cd_w_in=m_cd_w_in,
             cd_sinks=m_cd_sinks, cd_w_out=m_cd_w_out)
    v = dict(meta_tokens=v_meta_tokens, norm_gains=v_norm_gains, ffn_w_gate=v_ffn_w_gate, ffn_w_up=v_ffn_w_up,
             ffn_w_down=v_ffn_w_down, ab_w_in=v_ab_w_in, ab_conv_w=v_ab_conv_w, ab_a_log=v_ab_a_log,
             ab_dt_bias=v_ab_dt_bias, ab_out_norm=v_ab_out_norm, ab_w_out=v_ab_w_out, cd_w_in=v_cd_w_in,
             cd_sinks=v_cd_sinks, cd_w_out=v_cd_w_out)
    order = list(w)
    S = x.shape[1]
    T = S + BLK

    fs = DFF // NDEV
    ffn_local = jnp.concatenate([jnp.swapaxes(ffn_w_gate, 2, 3).reshape(4 * fs, D),
                                 jnp.swapaxes(ffn_w_up, 2, 3).reshape(4 * fs, D), ffn_w_down.reshape(4 * fs, D)],
                                axis=0).astype(BF16)
    outs_local = jnp.concatenate([ab_w_out[0], cd_w_out[0]], axis=0).astype(BF16)
    ffn_all, abin_all, outs_all, cdin_all = all_gather_big(
        [ffn_local, ab_w_in[0].astype(BF16), outs_local, cd_w_in[0].astype(BF16)], name="gather_weights")
    ffn_mat = lambda k: ffn_all[:, k * fs:(k + 1) * fs].reshape(DFF, D)
    layers = [(i, j) for i in range(2) for j in range(2)]
    wg = {ij: ffn_mat(k) for k, ij in enumerate(layers)}
    wu = {ij: ffn_mat(4 + k) for k, ij in enumerate(layers)}
    wd = {ij: ffn_mat(8 + k) for k, ij in enumerate(layers)}
    ab_in = jnp.pad(_stack_shards(abin_all, 1), ((0, 0), (0, AB_INP - AB_IN)))
    ab_out = outs_all[:, :D // NDEV].reshape(D, D)
    cd_in = _pad_heads(_stack_shards(cdin_all, 1), 1)
    cd_out = _pad_heads(outs_all[:, D // NDEV:].reshape(D, D), 0)
    small_src = jnp.broadcast_to(_pad_rows8([w[n][None] for n, _, _ in SMALL_SHARDED]), (NDEV, 40, LANES))
    small_all = _unpad_rows8(all_to_all_small(small_src, name="gather_small"), [s for _, s, _ in SMALL_SHARDED])
    full = {n: _stack_shards(g, ax) for (n, _, ax), g in zip(SMALL_SHARDED, small_all)}
    conv_w = full["ab_conv_w"][0]
    gains = full["norm_gains"].reshape(2, 6, 1, D)
    tables = retention_tables(T)

    h = jnp.concatenate([jnp.zeros((PAD, D), F32), full["meta_tokens"], x[0]], axis=0)
    h, s00 = ffn_fwd(h, gains[0, 0], gains[0, 1], wg[0, 0], wu[0, 0], wd[0, 0], "00")
    h, sab = mixer_ab_fwd(h, gains[0, 2], gains[0, 3], ab_in, conv_w, ab_a_log, ab_dt_bias, ab_out_norm, ab_out, tables)
    h, s01 = ffn_fwd(h, gains[0, 4], gains[0, 5], wg[0, 1], wu[0, 1], wd[0, 1], "01")
    h, s10 = ffn_fwd(h, gains[1, 0], gains[1, 1], wg[1, 0], wu[1, 0], wd[1, 0], "10")
    h, scd = mixer_cd_fwd(h, gains[1, 2], gains[1, 3], cd_in, cd_sinks, cd_out)
    h, s11 = ffn_fwd(h, gains[1, 4], gains[1, 5], wg[1, 1], wu[1, 1], wd[1, 1], "11")
    loss_tile, dh = loss_and_grad(h, loss_target[0], name="loss")
    loss = lax.psum(loss_tile[0, 0], ("x", "y", "c"))

    dgain = [[None] * 6, [None] * 6]
    dffn = {}
    dh, (dgain[1][4], dgain[1][5]), dffn[1, 1] = ffn_bwd(s11, dh, gains[1, 4], gains[1, 5], wg[1, 1], wu[1, 1], wd[1, 1], "11")
    dh, (dgain[1][2], dgain[1][3]), (dcd_in, dcd_out), dsinks = mixer_cd_bwd(scd, dh, gains[1, 2], gains[1, 3], cd_in, cd_sinks, cd_out)
    dh, (dgain[1][0], dgain[1][1]), dffn[1, 0] = ffn_bwd(s10, dh, gains[1, 0], gains[1, 1], wg[1, 0], wu[1, 0], wd[1, 0], "10")
    dh, (dgain[0][4], dgain[0][5]), dffn[0, 1] = ffn_bwd(s01, dh, gains[0, 4], gains[0, 5], wg[0, 1], wu[0, 1], wd[0, 1], "01")
    dh, (dgain[0][2], dgain[0][3]), (dab_in, dab_out), (dconv, dalog, ddtb, donorm) = mixer_ab_bwd(
        sab, dh, gains[0, 2], gains[0, 3], ab_in, conv_w, ab_a_log, ab_dt_bias, ab_out_norm, ab_out, tables)
    dh, (dgain[0][0], dgain[0][1]), dffn[0, 0] = ffn_bwd(s00, dh, gains[0, 0], gains[0, 1], wg[0, 0], wu[0, 0], wd[0, 0], "00")
    grad_x = dh[BLK:][None]

    gfull = dict(meta_tokens=dh[PAD:BLK], norm_gains=jnp.stack([jnp.concatenate(r, axis=0) for r in dgain]),
                 ab_conv_w=dconv[None])
    ffn_send = jnp.concatenate([dffn[ij][k].astype(BF16).reshape(NDEV, fs, D) for k in range(3) for ij in layers], axis=1)
    outs_send = jnp.concatenate([dab_out.astype(BF16).reshape(NDEV, D // NDEV, D),
                                 _unpad_heads(dcd_out, 0).astype(BF16).reshape(NDEV, D // NDEV, D)], axis=1)
    abin_send = _split_shards(dab_in[:, :AB_IN].astype(BF16), 1)
    cdin_send = _split_shards(_unpad_heads(dcd_in, 1).astype(BF16), 1)
    ffn_g, abin_g, outs_g, cdin_g = reduce_scatter_big([ffn_send, abin_send, outs_send, cdin_send])
    ffn_g = ffn_g.reshape(3, 2, 2, fs, D)
    grads = dict(ffn_w_gate=jnp.swapaxes(ffn_g[0], 2, 3), ffn_w_up=jnp.swapaxes(ffn_g[1], 2, 3), ffn_w_down=ffn_g[2],
                 ab_w_in=abin_g[None], ab_w_out=outs_g[None, :D // NDEV], cd_w_in=cdin_g[None],
                 cd_w_out=outs_g[None, D // NDEV:])
    repl = [jnp.broadcast_to(t[None], (NDEV,) + t.shape) for t in (dalog, ddtb, donorm, dsinks)]
    ssend = _pad_rows8([_split_shards(gfull[n], ax) for n, _, ax in SMALL_SHARDED] + repl)
    ssum = sum_slots(all_to_all_small(ssend, name="exchange_small_grads"), name="sum_small_grads")[None]
    small = _unpad_rows8(ssum, [s for _, s, _ in SMALL_SHARDED] + [s for _, s in SMALL_REPL])
    grads.update({n: g[0] for n, g in zip([n for n, _, _ in SMALL_SHARDED] + [n for n, _ in SMALL_REPL], small)})

    delta, new_m, new_v = {}, {}, {}
    for n in order:
        shape = w[n].shape
        view = (-1, shape[-1])
        d_, m_, v_ = adamw(w[n].reshape(view), grads[n].reshape(view), m[n].reshape(view), v[n].reshape(view),
                           name=f"adamw_{n}")
        delta[n], new_m[n], new_v[n] = d_.reshape(shape), m_.reshape(shape), v_.reshape(shape)
    return (loss, grad_x, *[grads[n] for n in order], *[delta[n] for n in order], *[new_m[n] for n in order],
            *[new_v[n] for n in order])
```

```python
import functools
import math

import numpy as np
import jax
import jax.numpy as jnp
from jax import lax
from jax.experimental import pallas as pl
from jax.experimental.pallas import tpu as pltpu

F32, BF16 = jnp.float32, jnp.bfloat16
EPS = 1e-6
D = 1024
NMETA = 16
BLK = 128
PAD = BLK - NMETA
DFF = 2816
LANES = 128
NDEV = 8
AB_IN, AB_INP = 4104, 4224
ADAM_LR, ADAM_B1, ADAM_B2, ADAM_EPS, ADAM_WD, ADAM_STEP = 0.001, 0.9, 0.999, 1e-08, 0.01, 10
VMEM_LIMIT = 56 * 1024 * 1024
MESH = pl.DeviceIdType.MESH
HIGH = lax.Precision.HIGH


def _params(sem):
    return pltpu.CompilerParams(dimension_semantics=sem, vmem_limit_bytes=VMEM_LIMIT)


def _row_tile(T, streamed, resident):
    for tm in (640, 320, 128):
        if T % tm == 0 and 2 * (tm * streamed + resident) <= VMEM_LIMIT - 14 * 1024 * 1024:
            return tm
    return _tile(T, 128)


MXU_COLS = 256


def _col_chunks(n):
    return [slice(c, min(c + MXU_COLS, n)) for c in range(0, n, MXU_COLS)]


def _tile(n, cap):
    if n <= cap:
        return n
    best = None
    for t in range(LANES, cap + 1, LANES):
        if n % t == 0:
            best = t
    assert best is not None, (n, cap)
    return best


def _rms_fwd(x, g):
    return x * lax.rsqrt(jnp.mean(x * x, axis=-1, keepdims=True) + EPS) * g


def _rms_bwd(x, g, dz):
    r = lax.rsqrt(jnp.mean(x * x, axis=-1, keepdims=True) + EPS)
    xh = x * r
    dg = jnp.sum(dz * xh, axis=0, keepdims=True)
    t = dz * g
    return r * (t - xh * jnp.mean(t * xh, axis=-1, keepdims=True)), dg


def _sigmoid(x):
    return 0.5 * jnp.tanh(0.5 * x) + 0.5


def _silu(x):
    return x * _sigmoid(x)


def _nn(a, b, precision=None):
    return lax.dot_general(a, b, (((1,), (0,)), ((), ())), preferred_element_type=F32, precision=precision)


def _nt(a, b):
    return lax.dot_general(a, b, (((1,), (1,)), ((), ())), preferred_element_type=F32)


def _tn(a, b):
    return lax.dot_general(a, b, (((0,), (0,)), ((), ())), preferred_element_type=F32)


def _mm(a, b, precision=None):
    if a.ndim == 3:
        return lax.dot_general(a, b, (((2,), (1,)), ((0,), (0,))), preferred_element_type=F32, precision=precision)
    return _nn(a, b, precision)


def _t(x):
    return jnp.swapaxes(x, -1, -2)


@jax.custom_vjp
def bdot(a, b):
    return _mm(a.astype(BF16), b.astype(BF16))


def _bdot_fwd(a, b):
    return bdot(a, b), (a, b)


def _bdot_bwd(res, g):
    a, b = res
    return bdot(g, _t(b)), bdot(_t(a), g)


bdot.defvjp(_bdot_fwd, _bdot_bwd)


@jax.custom_vjp
def hdot(a, b):
    return _mm(a, b, HIGH)


def _hdot_fwd(a, b):
    return hdot(a, b), (a, b)


def _hdot_bwd(res, g):
    a, b = res
    return hdot(g, _t(b)), hdot(_t(a), g)


hdot.defvjp(_hdot_fwd, _hdot_bwd)


def _iota2(shape, axis):
    return lax.broadcasted_iota(jnp.int32, shape, axis)


def _lane_pick(row, lane):
    return jnp.sum(jnp.where(_iota2(row.shape, 1) == lane, row, 0.0), axis=1, keepdims=True)


def norm_mm(h, gain, ws, *, swiglu, name, wt=False, out_dtype=F32):
    T, Dm = h.shape
    N = ws[0].shape[0 if wt else 1]
    tm, tn = _tile(T, 640), _tile(N, 1408)
    nw = len(ws)
    mm = _nt if wt else _nn

    def body(h_ref, g_ref, *refs):
        w_refs, u_ref, o_refs = refs[:nw], refs[nw], refs[nw + 1:]

        @pl.when(pl.program_id(1) == 0)
        def _():
            u_ref[...] = _rms_fwd(h_ref[...], g_ref[...]).astype(BF16)

        u = u_ref[...]
        for cols in _col_chunks(tn):
            acc = [mm(u, w[cols, :] if wt else w[:, cols]) for w in w_refs]
            if swiglu:
                o_refs[0][:, cols] = acc[0].astype(BF16)
                o_refs[1][:, cols] = acc[1].astype(BF16)
                o_refs[2][:, cols] = (_silu(acc[0]) * acc[1]).astype(BF16)
            else:
                o_refs[0][:, cols] = acc[0].astype(out_dtype)

    row = pl.BlockSpec((tm, Dm), lambda i, j: (i, 0))
    tile = pl.BlockSpec((tm, tn), lambda i, j: (i, j))
    if swiglu:
        out_shape = [jax.ShapeDtypeStruct((T, Dm), BF16)] + [jax.ShapeDtypeStruct((T, N), BF16)] * 3
        out_specs = [row, tile, tile, tile]
    else:
        out_shape = [jax.ShapeDtypeStruct((T, Dm), BF16), jax.ShapeDtypeStruct((T, N), out_dtype)]
        out_specs = [row, tile]
    return pl.pallas_call(
        body, name=name, grid=(T // tm, N // tn),
        in_specs=[row, pl.BlockSpec((1, Dm), lambda i, j: (0, 0))]
        + [pl.BlockSpec((tn, Dm), lambda i, j: (j, 0)) if wt else pl.BlockSpec((Dm, tn), lambda i, j: (0, j))] * nw,
        out_specs=out_specs, out_shape=out_shape,
        compiler_params=_params(("arbitrary", "arbitrary")),
    )(h, gain, *ws)


def mm_norm_res(As, Ws, h, gain, scale, *, name):
    T, Dm = h.shape
    n = len(As)
    tm = _row_tile(T, sum(a.shape[1] * a.dtype.itemsize for a in As) + 3 * Dm * 4,
                   sum(w.size * w.dtype.itemsize for w in Ws))

    def body(*refs):
        a_refs, w_refs = refs[:n], refs[n:2 * n]
        h_ref, g_ref, y_ref, hn_ref = refs[2 * n:]
        y = _nn(a_refs[0][...].astype(BF16), w_refs[0][...])
        for a, w in zip(a_refs[1:], w_refs[1:]):
            y = y + _nn(a[...].astype(BF16), w[...])
        y_ref[...] = y
        hn_ref[...] = h_ref[...] + scale * _rms_fwd(y, g_ref[...])

    row = pl.BlockSpec((tm, Dm), lambda i: (i, 0))
    return pl.pallas_call(
        body, name=name, grid=(T // tm,),
        in_specs=[pl.BlockSpec((tm, a.shape[1]), lambda i: (i, 0)) for a in As]
        + [pl.BlockSpec(w.shape, lambda i: (0, 0)) for w in Ws]
        + [row, pl.BlockSpec((1, Dm), lambda i: (0, 0))],
        out_specs=[row, row], out_shape=[jax.ShapeDtypeStruct((T, Dm), F32)] * 2,
        compiler_params=_params(("arbitrary",)),
    )(*As, *Ws, h, gain)


def normbwd_mm_nt(dh, y, gain, w, scale, gu=None, *, name):
    T, Dm = dh.shape
    N = w.shape[0]
    tm, tn = _tile(T, 640), _tile(N, 1408)
    swiglu = gu is not None

    def body(dh_ref, y_ref, g_ref, w_ref, *refs):
        if swiglu:
            gate_ref, up_ref, dy_ref, dg_ref, dgate_ref, dup_ref, a_ref = refs
        else:
            dy_ref, dg_ref, da_ref = refs
        i, j = pl.program_id(0), pl.program_id(1)

        @pl.when(j == 0)
        def _():
            dy, dg = _rms_bwd(y_ref[...], g_ref[...], scale * dh_ref[...])
            dy_ref[...] = dy.astype(BF16)

            @pl.when(i == 0)
            def _():
                dg_ref[...] = jnp.zeros_like(dg_ref)

            dg_ref[...] += dg

        dy = dy_ref[...]
        for cols in _col_chunks(tn):
            da = _nt(dy, w_ref[cols, :])
            if swiglu:
                gate, up = gate_ref[:, cols].astype(F32), up_ref[:, cols].astype(F32)
                s = _sigmoid(gate)
                dgate_ref[:, cols] = (da * up * s * (1.0 + gate * (1.0 - s))).astype(BF16)
                dup_ref[:, cols] = (da * gate * s).astype(BF16)
                a_ref[:, cols] = (gate * s * up).astype(BF16)
            else:
                da_ref[:, cols] = da

    row = pl.BlockSpec((tm, Dm), lambda i, j: (i, 0))
    vec = pl.BlockSpec((1, Dm), lambda i, j: (0, 0))
    tile = pl.BlockSpec((tm, tn), lambda i, j: (i, j))
    in_specs = [row, row, vec, pl.BlockSpec((tn, Dm), lambda i, j: (j, 0))]
    out_shape = [jax.ShapeDtypeStruct((T, Dm), BF16), jax.ShapeDtypeStruct((1, Dm), F32)]
    if swiglu:
        in_specs += [tile, tile]
        out_shape += [jax.ShapeDtypeStruct((T, N), BF16)] * 3
        out_specs = [row, vec, tile, tile, tile]
        args = (dh, y, gain, w, *gu)
    else:
        out_shape += [jax.ShapeDtypeStruct((T, N), F32)]
        out_specs = [row, vec, tile]
        args = (dh, y, gain, w)
    return pl.pallas_call(
        body, name=name, grid=(T // tm, N // tn), in_specs=in_specs, out_specs=out_specs,
        out_shape=out_shape, compiler_params=_params(("arbitrary", "arbitrary")),
    )(*args)


def mm_nt_normbwd(dPs, Ws, h, gain, dh_in, *, name, wt=False):
    T, Dm = h.shape
    n = len(dPs)
    tm = _row_tile(T, sum(p.shape[1] * p.dtype.itemsize for p in dPs) + 3 * Dm * 4,
                   sum(w.size * w.dtype.itemsize for w in Ws))
    mm = _nn if wt else _nt

    def body(*refs):
        p_refs, w_refs = refs[:n], refs[n:2 * n]
        h_ref, g_ref, dhin_ref, dh_ref, dg_ref = refs[2 * n:]
        du = mm(p_refs[0][...].astype(BF16), w_refs[0][...])
        for p, w in zip(p_refs[1:], w_refs[1:]):
            du = du + mm(p[...].astype(BF16), w[...])
        dx, dg = _rms_bwd(h_ref[...], g_ref[...], du)
        dh_ref[...] = dhin_ref[...] + dx

        @pl.when(pl.program_id(0) == 0)
        def _():
            dg_ref[...] = jnp.zeros_like(dg_ref)

        dg_ref[...] += dg

    row = pl.BlockSpec((tm, Dm), lambda i: (i, 0))
    vec = pl.BlockSpec((1, Dm), lambda i: (0, 0))
    return pl.pallas_call(
        body, name=name, grid=(T // tm,),
        in_specs=[pl.BlockSpec((tm, p.shape[1]), lambda i: (i, 0)) for p in dPs]
        + [pl.BlockSpec(w.shape, lambda i: (0, 0)) for w in Ws] + [row, vec, row],
        out_specs=[row, vec],
        out_shape=[jax.ShapeDtypeStruct((T, Dm), F32), jax.ShapeDtypeStruct((1, Dm), F32)],
        compiler_params=_params(("arbitrary",)),
    )(*dPs, *Ws, h, gain, dh_in)


def mm_tn(a, b, *, name):
    T, M = a.shape
    N = b.shape[1]
    tm, tn, tk = _tile(M, 1408), _tile(N, 1408), _tile(T, 640)

    def body(a_ref, b_ref, o_ref):
        @pl.when(pl.program_id(2) == 0)
        def _():
            o_ref[...] = jnp.zeros_like(o_ref)

        o_ref[...] += _tn(a_ref[...].astype(BF16), b_ref[...].astype(BF16))

    return pl.pallas_call(
        body, name=name, grid=(M // tm, N // tn, T // tk),
        in_specs=[pl.BlockSpec((tk, tm), lambda i, j, k: (k, i)), pl.BlockSpec((tk, tn), lambda i, j, k: (k, j))],
        out_specs=pl.BlockSpec((tm, tn), lambda i, j, k: (i, j)),
        out_shape=jax.ShapeDtypeStruct((M, N), F32),
        compiler_params=_params(("arbitrary", "arbitrary", "arbitrary")),
    )(a, b)


def loss_and_grad(h, target, *, name):
    T, Dm = h.shape

    def body(h_ref, t_ref, loss_ref, dh_ref):
        b = pl.program_id(0)

        @pl.when(b == 0)
        def _():
            loss_ref[...] = jnp.zeros_like(loss_ref)
            dh_ref[...] = jnp.zeros_like(dh_ref)

        @pl.when(b > 0)
        def _():
            e = h_ref[...] - t_ref[...]
            dh_ref[...] = e * (1.0 / Dm)
            loss_ref[...] += jnp.sum(e * e) * (0.5 / Dm)

    return pl.pallas_call(
        body, name=name, grid=(T // BLK,),
        in_specs=[pl.BlockSpec((BLK, Dm), lambda b: (b, 0)),
                  pl.BlockSpec((BLK, Dm), lambda b: (jnp.maximum(b - 1, 0), 0))],
        out_specs=[pl.BlockSpec((8, LANES), lambda b: (0, 0)), pl.BlockSpec((BLK, Dm), lambda b: (b, 0))],
        out_shape=[jax.ShapeDtypeStruct((8, LANES), F32), jax.ShapeDtypeStruct((T, Dm), F32)],
        compiler_params=_params(("arbitrary",)),
    )(h, target)


def adamw(w, g, m, v, *, name):
    R, C = w.shape
    tr = R
    for t in (512, 352, 256):
        if R > t and R % t == 0:
            tr = t
            break

    def body(w_ref, g_ref, m_ref, v_ref, d_ref, nm_ref, nv_ref):
        g_ = g_ref[...]
        m_ = ADAM_B1 * m_ref[...] + (1.0 - ADAM_B1) * g_
        v_ = ADAM_B2 * v_ref[...] + (1.0 - ADAM_B2) * (g_ * g_)
        m_hat = m_ / (1.0 - ADAM_B1 ** ADAM_STEP)
        v_hat = v_ / (1.0 - ADAM_B2 ** ADAM_STEP)
        d_ref[...] = -ADAM_LR * (m_hat / (jnp.sqrt(v_hat) + ADAM_EPS) + ADAM_WD * w_ref[...])
        nm_ref[...] = m_
        nv_ref[...] = v_

    spec = pl.BlockSpec((tr, C), lambda i: (i, 0))
    return pl.pallas_call(
        body, name=name, grid=(R // tr,), in_specs=[spec] * 4, out_specs=[spec] * 3,
        out_shape=[jax.ShapeDtypeStruct((R, C), F32)] * 3, compiler_params=_params(("arbitrary",)),
    )(w, g, m, v)


def _me():
    return lax.axis_index("x"), lax.axis_index("y"), lax.axis_index("c")


def _flip(pos, rel):
    return tuple(1 - p if r else p for p, r in zip(pos, rel))


def _slot(pos):
    return 4 * pos[0] + 2 * pos[1] + pos[2]


HBM_SPEC = pl.BlockSpec(memory_space=pltpu.HBM)
CHIP_RELS = ((1, 0), (0, 1), (1, 1))


def all_gather_big(xs, *, name):
    n = len(xs)

    def body(*refs):
        x_refs, out_refs = refs[:n], refs[n:2 * n]
        send_sems, recv_sems, local_sems = refs[2 * n:]
        me = _me()
        sibling = _flip(me, (0, 0, 1))
        chips = [_flip(me, rel + (0,)) for rel in CHIP_RELS]

        def copy(i, k, block, to, src=None):
            dst = out_refs[i].at[_slot(block)]
            return pltpu.make_async_remote_copy(
                src_ref=dst if src is None else src, dst_ref=dst, send_sem=send_sems.at[i, k],
                recv_sem=recv_sems.at[i, k], device_id=to, device_id_type=MESH)

        sent, local = [], []
        for i in range(n):
            mine = pltpu.make_async_copy(x_refs[i], out_refs[i].at[_slot(me)], local_sems.at[i])
            mine.start()
            local.append(mine)
            sent += [copy(i, 0, me, sibling, src=x_refs[i])]
            sent += [copy(i, 1 + j, me, chip, src=x_refs[i]) for j, chip in enumerate(chips)]
        for cp in sent:
            cp.start()
        for i in range(n):
            for j, chip in enumerate(chips):
                copy(i, 1 + j, chip, me).wait_recv()
                passed = copy(i, 4 + j, chip, sibling)
                passed.start()
                sent.append(passed)
        for i in range(n):
            copy(i, 0, sibling, me).wait_recv()
            for j, chip in enumerate(chips):
                copy(i, 4 + j, _flip(chip, (0, 0, 1)), me).wait_recv()
        for cp in sent:
            cp.wait_send()
        for mine in local:
            mine.wait()

    return pl.pallas_call(
        body, name=name, in_specs=[HBM_SPEC] * n, out_specs=[HBM_SPEC] * n,
        out_shape=[jax.ShapeDtypeStruct((NDEV,) + x.shape, x.dtype) for x in xs],
        scratch_shapes=[pltpu.SemaphoreType.DMA((n, 7)), pltpu.SemaphoreType.DMA((n, 7)), pltpu.SemaphoreType.DMA((n,))],
    )(*xs)


def all_to_all_small(src, *, name):
    _, r, C = src.shape

    def body(src_ref, out_ref, send_sems, recv_sems):
        me = _me()
        my = _slot(me)
        out_ref[my] = src_ref[my]
        copies = []
        for k in range(1, NDEV):
            peer = _flip(me, ((k >> 2) & 1, (k >> 1) & 1, k & 1))
            cp = pltpu.make_async_remote_copy(
                src_ref=src_ref.at[_slot(peer)], dst_ref=out_ref.at[my], send_sem=send_sems.at[k - 1],
                recv_sem=recv_sems.at[k - 1], device_id=peer, device_id_type=MESH)
            cp.start()
            copies.append((cp, peer))
        for k, (cp, peer) in enumerate(copies):
            pltpu.make_async_remote_copy(
                src_ref=src_ref.at[my], dst_ref=out_ref.at[_slot(peer)], send_sem=send_sems.at[k],
                recv_sem=recv_sems.at[k], device_id=peer, device_id_type=MESH).wait_recv()
        for cp, _ in copies:
            cp.wait_send()

    vm = pl.BlockSpec(memory_space=pltpu.VMEM)
    return pl.pallas_call(
        body, name=name, in_specs=[vm], out_specs=vm, out_shape=jax.ShapeDtypeStruct(src.shape, src.dtype),
        scratch_shapes=[pltpu.SemaphoreType.DMA((7,)), pltpu.SemaphoreType.DMA((7,))],
    )(src)


def sum_slots(a, *, name):
    n, r, C = a.shape

    def body(a_ref, o_ref):
        s = a_ref[0]
        for k in range(1, n):
            s = s + a_ref[k]
        o_ref[...] = s

    vm = pl.BlockSpec(memory_space=pltpu.VMEM)
    return pl.pallas_call(body, name=name, in_specs=[vm], out_specs=vm,
                          out_shape=jax.ShapeDtypeStruct((r, C), F32))(a)


def rs_exchange_sibling(gs, *, name):
    n = len(gs)

    def body(*refs):
        g_refs, out_refs, send_sems, recv_sems = refs[:n], refs[n:2 * n], refs[2 * n], refs[2 * n + 1]
        sibling = _flip(_me(), (0, 0, 1))
        copies = []
        for i in range(n):
            for chip in range(4):
                cp = pltpu.make_async_remote_copy(
                    src_ref=g_refs[i].at[2 * chip + sibling[2]], dst_ref=out_refs[i].at[chip],
                    send_sem=send_sems.at[i, chip], recv_sem=recv_sems.at[i, chip], device_id=sibling,
                    device_id_type=MESH)
                cp.start()
                copies.append(cp)
        for cp in copies:
            cp.wait()

    return pl.pallas_call(
        body, name=name, in_specs=[HBM_SPEC] * n, out_specs=[HBM_SPEC] * n,
        out_shape=[jax.ShapeDtypeStruct((4,) + g.shape[1:], g.dtype) for g in gs],
        scratch_shapes=[pltpu.SemaphoreType.DMA((n, 4)), pltpu.SemaphoreType.DMA((n, 4))],
    )(*gs)


def rs_chip_partials(g, got, *, name):
    _, R, C = g.shape
    tr = _tile(R, 768)

    def body(c_ref, g_ref, got_ref, o_ref):
        o_ref[...] = (g_ref[...].astype(F32) + got_ref[...].astype(F32)).astype(o_ref.dtype)

    c = jnp.reshape(lax.axis_index("c"), (1,)).astype(jnp.int32)
    return pl.pallas_call(
        body, name=name,
        grid_spec=pltpu.PrefetchScalarGridSpec(
            num_scalar_prefetch=1, grid=(4, R // tr),
            in_specs=[pl.BlockSpec((None, tr, C), lambda k, i, c_ref: (2 * k + c_ref[0], i, 0)),
                      pl.BlockSpec((None, tr, C), lambda k, i, c_ref: (k, i, 0))],
            out_specs=pl.BlockSpec((None, tr, C), lambda k, i, c_ref: (k, i, 0))),
        out_shape=jax.ShapeDtypeStruct((4, R, C), g.dtype), compiler_params=_params(("arbitrary", "arbitrary")),
    )(c, g, got)


def rs_exchange_chips(ps, *, name):
    n = len(ps)

    def body(*refs):
        p_refs, out_refs, send_sems, recv_sems = refs[:n], refs[n:2 * n], refs[2 * n], refs[2 * n + 1]
        me = _me()
        copies = []
        for i in range(n):
            for j, rel in enumerate(CHIP_RELS):
                peer = _flip(me, rel + (0,))
                cp = pltpu.make_async_remote_copy(
                    src_ref=p_refs[i].at[2 * peer[0] + peer[1]], dst_ref=out_refs[i].at[j], send_sem=send_sems.at[i, j],
                    recv_sem=recv_sems.at[i, j], device_id=peer, device_id_type=MESH)
                cp.start()
                copies.append(cp)
        for cp in copies:
            cp.wait()

    return pl.pallas_call(
        body, name=name, in_specs=[HBM_SPEC] * n, out_specs=[HBM_SPEC] * n,
        out_shape=[jax.ShapeDtypeStruct((3,) + p.shape[1:], p.dtype) for p in ps],
        scratch_shapes=[pltpu.SemaphoreType.DMA((n, 3)), pltpu.SemaphoreType.DMA((n, 3))],
    )(*ps)


def rs_final_sum(p, got, *, name):
    _, R, C = p.shape
    tr = _tile(R, 768)

    def body(chip_ref, p_ref, got_ref, o_ref):
        s = p_ref[...].astype(F32)
        for j in range(3):
            s = s + got_ref[j].astype(F32)
        o_ref[...] = s

    mychip = jnp.reshape(2 * lax.axis_index("x") + lax.axis_index("y"), (1,)).astype(jnp.int32)
    return pl.pallas_call(
        body, name=name,
        grid_spec=pltpu.PrefetchScalarGridSpec(
            num_scalar_prefetch=1, grid=(R // tr,),
            in_specs=[pl.BlockSpec((None, tr, C), lambda i, chip_ref: (chip_ref[0], i, 0)),
                      pl.BlockSpec((3, tr, C), lambda i, chip_ref: (0, i, 0))],
            out_specs=pl.BlockSpec((tr, C), lambda i, chip_ref: (i, 0))),
        out_shape=jax.ShapeDtypeStruct((R, C), F32), compiler_params=_params(("arbitrary",)),
    )(mychip, p, got)


def reduce_scatter_big(gs):
    got = rs_exchange_sibling(gs, name="rs_sibling")
    parts = [rs_chip_partials(g, t, name=f"rs_chip_partials_{i}") for i, (g, t) in enumerate(zip(gs, got))]
    got2 = rs_exchange_chips(parts, name="rs_chips")
    return [rs_final_sum(p, t, name=f"rs_final_sum_{i}") for i, (p, t) in enumerate(zip(parts, got2))]


def _blk(off):
    return pl.BlockSpec((BLK, LANES), lambda h, n: (n, off + h))


def _const_spec(shape):
    return pl.BlockSpec(shape, lambda *_: (0,) * len(shape))


def retention_tables(T):
    pos = jnp.arange(T, dtype=F32) - float(PAD)
    inv_freq = 1.0 / (10000.0 ** jnp.linspace(0.0, 1.0, 64, dtype=F32))
    ang = pos[:, None] * inv_freq[None, :]
    cos = jnp.repeat(jnp.cos(ang), 2, axis=1)
    sin = jnp.repeat(jnp.sin(ang), 2, axis=1) * jnp.tile(jnp.array([-1.0, 1.0], F32), 64)[None, :]
    lane = np.arange(LANES)
    perm = jnp.broadcast_to(jnp.asarray((lane[:, None] == (lane[None, :] ^ 1)).astype(np.float32)), (4, LANES, LANES))
    log_gamma = jnp.log1p(-jnp.exp2(-5.0 - jnp.arange(4, dtype=F32)))
    idx = jnp.arange(BLK, dtype=F32)
    diff = idx[:, None] - idx[None, :]
    intra = jnp.where(diff >= 0, jnp.exp(jnp.maximum(diff, 0.0) * log_gamma[:, None, None]), 0.0)
    zeta = jnp.exp((BLK - 1.0 - idx)[None, :] * log_gamma[:, None])
    xi = jnp.exp((idx + 1.0)[None, :] * log_gamma[:, None])
    bc = lambda t: jnp.broadcast_to(t[:, :, None], (4, BLK, LANES))
    return cos, sin, perm, intra, bc(zeta), bc(xi)


def _heads(x):
    return jnp.stack([x[:, h * LANES:(h + 1) * LANES] for h in range(4)])


def _unheads(y):
    return jnp.concatenate([y[h] for h in range(4)], axis=1)


def _ret_chunk(rq, rk, rv, rg, S, cos, sin, intra, zeta, xi, perm):
    q = rq * cos + hdot(rq, perm) * sin
    k = (rk * cos + hdot(rk, perm) * sin) * (128.0 ** -0.5)
    ret = bdot(bdot(q, _t(k)) * intra, rv) + bdot(q * xi, S)
    S_new = S * xi[..., BLK - 1:BLK, :] + bdot(_t(k * zeta), rv)
    c = ret - jnp.mean(ret, axis=-1, keepdims=True)
    out = c * lax.rsqrt(jnp.mean(c * c, axis=-1, keepdims=True) + EPS) * _silu(rg)
    return out, S_new


def _wide(off):
    return pl.BlockSpec((BLK, 4 * LANES), lambda n: (n, off))


def retention_fwd(p, tables, *, name):
    T = p.shape[0]
    N = T // BLK
    cos, sin, perm, intra, zeta, xi = tables

    def body(rq, rk, rv, rg, cos_ref, sin_ref, in_ref, ze_ref, xi_ref, perm_ref, out_ref, sall_ref, s_scr):
        @pl.when(pl.program_id(0) == 0)
        def _():
            s_scr[...] = jnp.zeros_like(s_scr)

        S = s_scr[...]
        sall_ref[...] = S
        out, S_new = _ret_chunk(_heads(rq[...]), _heads(rk[...]), _heads(rv[...]), _heads(rg[...]), S, cos_ref[...],
                                sin_ref[...], in_ref[...], ze_ref[...], xi_ref[...], perm_ref[...])
        out_ref[...] = _unheads(out).astype(BF16)
        s_scr[...] = S_new

    rowtab = pl.BlockSpec((BLK, LANES), lambda n: (n, 0))
    tab = _const_spec((4, BLK, LANES))
    return pl.pallas_call(
        body, name=name, grid=(N,),
        in_specs=[_wide(0), _wide(1), _wide(2), _wide(3), rowtab, rowtab, tab, tab, tab, tab],
        out_specs=[_wide(0), pl.BlockSpec((None, 4, LANES, LANES), lambda n: (n, 0, 0, 0))],
        out_shape=[jax.ShapeDtypeStruct((T, 512), BF16), jax.ShapeDtypeStruct((N, 4, LANES, LANES), F32)],
        scratch_shapes=[pltpu.VMEM((4, LANES, LANES), F32)],
        compiler_params=_params(("arbitrary",)),
    )(p, p, p, p, cos, sin, intra, zeta, xi, perm)


def _row_mask(n):
    return (n * BLK + _iota2((BLK, 1), 0) >= PAD).astype(F32)


def retention_bwd(p, sall, dmixed, tables, *, name):
    T = p.shape[0]
    N = T // BLK
    cos, sin, perm, intra, zeta, xi = tables

    def body(rq, rk, rv, rg, cos_ref, sin_ref, in_ref, ze_ref, xi_ref, perm_ref, sall_ref, do_ref, drq, drk, drv, drg,
             ds_scr):
        n = N - 1 - pl.program_id(0)

        @pl.when(pl.program_id(0) == 0)
        def _():
            ds_scr[...] = jnp.zeros_like(ds_scr)

        f = lambda a, b, c, d, s: _ret_chunk(a, b, c, d, s, cos_ref[...], sin_ref[...], in_ref[...], ze_ref[...],
                                             xi_ref[...], perm_ref[...])
        _, vjp = jax.vjp(f, _heads(rq[...]), _heads(rk[...]), _heads(rv[...]), _heads(rg[...]), sall_ref[...])
        g = vjp((_heads(do_ref[...]), ds_scr[...]))
        mask = _row_mask(n)
        for ref, val in zip((drq, drk, drv, drg), g[:4]):
            ref[...] = _unheads(val) * mask
        ds_scr[...] = g[4]

    def rwide(off):
        return pl.BlockSpec((BLK, 4 * LANES), lambda n: (N - 1 - n, off))

    rowtab = pl.BlockSpec((BLK, LANES), lambda n: (N - 1 - n, 0))
    tab = _const_spec((4, BLK, LANES))
    return pl.pallas_call(
        body, name=name, grid=(N,),
        in_specs=[rwide(0), rwide(1), rwide(2), rwide(3), rowtab, rowtab, tab, tab, tab, tab,
                  pl.BlockSpec((None, 4, LANES, LANES), lambda n: (N - 1 - n, 0, 0, 0)), rwide(0)],
        out_specs=[rwide(0)] * 4, out_shape=[jax.ShapeDtypeStruct((T, 512), F32)] * 4,
        scratch_shapes=[pltpu.VMEM((4, LANES, LANES), F32)],
        compiler_params=_params(("arbitrary",)),
    )(p, p, p, p, cos, sin, intra, zeta, xi, perm, sall, dmixed)


def conv_silu_fwd(p, w, *, name):
    T = p.shape[0]
    N = T // BLK

    def body(x_ref, xp_ref, w_ref, o_ref):
        n = pl.program_id(0)
        cur = x_ref[...]
        cat = jnp.concatenate([jnp.where(n > 0, xp_ref[...], 0.0), cur], axis=0)
        y = w_ref[3:4, :] * cur
        for s in (1, 2, 3):
            y = y + w_ref[3 - s:4 - s, :] * pltpu.roll(cat, s, 0)[BLK:]
        o_ref[...] = _silu(y)

    cw = 4 * LANES
    return pl.pallas_call(
        body, name=name, grid=(N, 3),
        in_specs=[pl.BlockSpec((BLK, cw), lambda n, c: (n, 4 + c)),
                  pl.BlockSpec((BLK, cw), lambda n, c: (jnp.maximum(n - 1, 0), 4 + c)),
                  pl.BlockSpec((4, cw), lambda n, c: (0, c))],
        out_specs=pl.BlockSpec((BLK, cw), lambda n, c: (n, c)),
        out_shape=jax.ShapeDtypeStruct((T, 1536), F32), compiler_params=_params(("arbitrary", "arbitrary")),
    )(p, p, w)


def conv_silu_bwd(p, w, dact, part, *, name):
    T = p.shape[0]
    N = T // BLK
    cw = 4 * LANES

    def body(xp_ref, x_ref, xn_ref, w_ref, da_ref, dan_ref, dx_ref, dw_ref):
        n = pl.program_id(0)
        last = n == N - 1
        cat = jnp.concatenate([jnp.where(n > 0, xp_ref[...], 0.0), x_ref[...], jnp.where(last, 0.0, xn_ref[...])], axis=0)
        shifted = [cat] + [pltpu.roll(cat, s, 0) for s in (1, 2, 3)]
        y = w_ref[3:4, :] * shifted[0]
        for s in (1, 2, 3):
            y = y + w_ref[3 - s:4 - s, :] * shifted[s]
        y = y[BLK:]
        da = jnp.concatenate([da_ref[...], jnp.where(last, 0.0, dan_ref[...])], axis=0)
        sg = _sigmoid(y)
        dy = da * sg * (1.0 + y * (1.0 - sg))
        dx = w_ref[3:4, :] * dy[:BLK]
        for s in (1, 2, 3):
            dx = dx + w_ref[3 - s:4 - s, :] * pltpu.roll(dy, 2 * BLK - s, 0)[:BLK]
        dx_ref[...] = dx * _row_mask(n)

        @pl.when(n == 0)
        def _():
            dw_ref[...] = jnp.zeros_like(dw_ref)

        for s in (0, 1, 2, 3):
            dw_ref[3 - s:4 - s, :] += jnp.sum(dy[:BLK] * shifted[s][BLK:2 * BLK], axis=0, keepdims=True)

    def xs(d):
        return pl.BlockSpec((BLK, cw), lambda n: (jnp.clip(n + d, 0, N - 1), 4 + part))

    return pl.pallas_call(
        body, name=name, grid=(N,),
        in_specs=[xs(-1), xs(0), xs(1), pl.BlockSpec((4, cw), lambda n: (0, part)),
                  pl.BlockSpec((BLK, cw), lambda n: (n, 0)),
                  pl.BlockSpec((BLK, cw), lambda n: (jnp.minimum(n + 1, N - 1), 0))],
        out_specs=[pl.BlockSpec((BLK, cw), lambda n: (n, 0)), pl.BlockSpec((4, cw), lambda n: (0, 0))],
        out_shape=[jax.ShapeDtypeStruct((T, 512), F32), jax.ShapeDtypeStruct((4, 512), F32)],
        compiler_params=_params(("arbitrary",)),
    )(p, p, p, w, dact, dact)


def _softplus(x):
    return jnp.maximum(x, 0.0) + jnp.log1p(jnp.exp(-jnp.abs(x)))


def _pick4(tile, off):
    return jnp.stack([_lane_pick(tile, off + h) for h in range(4)])


def _spread4(v4, off, rows):
    lane = _iota2((rows, LANES), 1)
    out = jnp.where(lane == off, v4[0], 0.0)
    for h in range(1, 4):
        out = out + jnp.where(lane == off + h, v4[h], 0.0)
    return out


def _gdn_chunk(qa, ka, va, z, braw, araw, S, alog, dtb, onorm, rowmask, lincl):
    r, c = _iota2((BLK, BLK), 0), _iota2((BLK, BLK), 1)
    incl, strict = r >= c, r > c
    eye = (r == c).astype(F32)
    q = qa * lax.rsqrt(jnp.sum(qa * qa, axis=-1, keepdims=True) + EPS) * (128.0 ** -0.5)
    k = ka * lax.rsqrt(jnp.sum(ka * ka, axis=-1, keepdims=True) + EPS)
    beta = _sigmoid(braw) * rowmask
    g = -jnp.exp(alog) * _softplus(araw + dtb) * rowmask
    gc = hdot(lincl, jnp.broadcast_to(g, qa.shape))
    decay = jnp.where(incl, jnp.exp(jnp.where(incl, gc - _t(gc), 0.0)), 0.0)
    kb = k * beta
    amat = jnp.where(strict, bdot(kb, _t(k)) * decay, 0.0)
    m = -amat
    inv = eye + m
    pw = hdot(m, m)
    for t in range(6):
        inv = inv + hdot(inv, pw)
        if t < 5:
            pw = hdot(pw, pw)
    egc = jnp.exp(gc)
    u = hdot(inv, va * beta)
    w = hdot(inv, kb * egc)
    qk = jnp.where(incl, bdot(q, _t(k)) * decay, 0.0)
    glast = gc[..., BLK - 1:BLK, :]
    vnew = u - bdot(w, S)
    o = bdot(q * egc, S) + bdot(qk, vnew)
    S_new = S * jnp.exp(glast) + bdot(_t(k * jnp.exp(glast - gc)), vnew)
    out = o * lax.rsqrt(jnp.mean(o * o, axis=-1, keepdims=True) + EPS) * onorm * _silu(z)
    return out, S_new


def _lincl():
    i = np.arange(BLK)
    return jnp.broadcast_to(jnp.asarray((i[:, None] >= i[None, :]).astype(np.float32)), (4, BLK, BLK))


def gdn_fwd(act, p, alog, dtb, onorm, *, name):
    T = p.shape[0]
    N = T // BLK

    def body(qa, ka, va, z, ba, alog_ref, dtb_ref, on_ref, l_ref, out_ref, sall_ref, s_scr):
        n = pl.program_id(0)

        @pl.when(n == 0)
        def _():
            s_scr[...] = jnp.zeros_like(s_scr)

        S = s_scr[...]
        sall_ref[...] = S
        out, S_new = _gdn_chunk(_heads(qa[...]), _heads(ka[...]), _heads(va[...]), _heads(z[...]), _pick4(ba[...], 0),
                                _pick4(ba[...], 4), S, _pick4(alog_ref[...], 0), _pick4(dtb_ref[...], 0), on_ref[...],
                                _row_mask(n), l_ref[...])
        out_ref[...] = _unheads(out).astype(BF16)
        s_scr[...] = S_new

    vec = _const_spec((1, LANES))
    return pl.pallas_call(
        body, name=name, grid=(N,),
        in_specs=[_wide(0), _wide(1), _wide(2), _wide(7), pl.BlockSpec((BLK, LANES), lambda n: (n, 32)), vec, vec, vec,
                  _const_spec((4, BLK, BLK))],
        out_specs=[_wide(0), pl.BlockSpec((None, 4, LANES, LANES), lambda n: (n, 0, 0, 0))],
        out_shape=[jax.ShapeDtypeStruct((T, 512), BF16), jax.ShapeDtypeStruct((N, 4, LANES, LANES), F32)],
        scratch_shapes=[pltpu.VMEM((4, LANES, LANES), F32)],
        compiler_params=_params(("arbitrary",)),
    )(act, act, act, p, p, alog, dtb, onorm, _lincl())


def gdn_bwd(act, p, alog, dtb, onorm, sall, dmixed, *, name):
    T = p.shape[0]
    N = T // BLK

    def body(qa, ka, va, z, ba, alog_ref, dtb_ref, on_ref, l_ref, sall_ref, do_ref,
             dq_ref, dk_ref, dv_ref, dz_ref, dba_ref, dal_ref, ddt_ref, don_ref, ds_scr):
        step = pl.program_id(0)
        n = N - 1 - step

        @pl.when(step == 0)
        def _():
            ds_scr[...] = jnp.zeros_like(ds_scr)
            dal_ref[...] = jnp.zeros_like(dal_ref)
            ddt_ref[...] = jnp.zeros_like(ddt_ref)
            don_ref[...] = jnp.zeros_like(don_ref)

        rowmask, lincl = _row_mask(n), l_ref[...]
        f = lambda *a: _gdn_chunk(*a, rowmask, lincl)
        _, vjp = jax.vjp(f, _heads(qa[...]), _heads(ka[...]), _heads(va[...]), _heads(z[...]), _pick4(ba[...], 0),
                         _pick4(ba[...], 4), sall_ref[...], _pick4(alog_ref[...], 0), _pick4(dtb_ref[...], 0),
                         on_ref[...])
        g = vjp((_heads(do_ref[...]), ds_scr[...]))
        dq_ref[...] = _unheads(g[0]) * rowmask
        dk_ref[...] = _unheads(g[1]) * rowmask
        dv_ref[...] = _unheads(g[2]) * rowmask
        dz_ref[...] = _unheads(g[3]) * rowmask
        dba_ref[...] = (_spread4(g[4], 0, BLK) + _spread4(g[5], 4, BLK)) * rowmask
        ds_scr[...] = g[6]
        dal_ref[...] += _spread4(g[7], 0, 1)
        ddt_ref[...] += _spread4(g[8], 0, 1)
        don_ref[...] += g[9]

    def rwide(off):
        return pl.BlockSpec((BLK, 4 * LANES), lambda s: (N - 1 - s, off))

    vec = _const_spec((1, LANES))
    col = pl.BlockSpec((BLK, LANES), lambda s: (N - 1 - s, 0))
    return pl.pallas_call(
        body, name=name, grid=(N,),
        in_specs=[rwide(0), rwide(1), rwide(2), rwide(7), pl.BlockSpec((BLK, LANES), lambda s: (N - 1 - s, 32)), vec, vec,
                  vec, _const_spec((4, BLK, BLK)),
                  pl.BlockSpec((None, 4, LANES, LANES), lambda s: (N - 1 - s, 0, 0, 0)), rwide(1)],
        out_specs=[rwide(0)] * 4 + [col, vec, vec, vec],
        out_shape=[jax.ShapeDtypeStruct((T, 512), F32)] * 4 + [jax.ShapeDtypeStruct((T, LANES), F32)]
        + [jax.ShapeDtypeStruct((1, LANES), F32)] * 3,
        scratch_shapes=[pltpu.VMEM((4, LANES, LANES), F32)],
        compiler_params=_params(("arbitrary",)),
    )(act, act, act, p, p, alog, dtb, onorm, _lincl(), sall, dmixed)


NEG = -1e30


def _swa_block(q, k0, kp, kc, v0, vp, vc, sink, n):
    r, c = _iota2((BLK, BLK), 0), _iota2((BLK, BLK), 1)
    m0 = (c >= PAD) & (c <= n * BLK + r)
    mp = (n >= 2) & (c > r)
    mc = (n >= 1) & (r >= c)
    b = lambda t: jnp.broadcast_to(t, (4,) + t.shape)
    qs = q * (64.0 ** -0.5)
    s0 = jnp.where(m0, bdot(qs, _t(b(k0))), NEG)
    sp = jnp.where(mp, bdot(qs, _t(b(kp))), NEG)
    sc = jnp.where(mc, bdot(qs, _t(b(kc))), NEG)
    mx = jnp.maximum(jnp.max(jnp.maximum(jnp.maximum(s0, sp), sc), axis=-1, keepdims=True), sink)
    mx = lax.stop_gradient(mx)
    p0, pp, pc = jnp.exp(s0 - mx), jnp.exp(sp - mx), jnp.exp(sc - mx)
    den = (jnp.sum(p0, axis=-1, keepdims=True) + jnp.sum(pp, axis=-1, keepdims=True)
           + jnp.sum(pc, axis=-1, keepdims=True) + jnp.exp(sink - mx))
    return (bdot(p0, b(v0)) + bdot(pp, b(vp)) + bdot(pc, b(vc))) / den


def _swa_specs():
    rows = (lambda n: 0, lambda n: jnp.maximum(n - 1, 0), lambda n: n)

    def kv_spec(off, row):
        return pl.BlockSpec((BLK, LANES), lambda g, n: (row(n), off + g))

    q = pl.BlockSpec((BLK, 4 * LANES), lambda g, n: (n, g))
    return q, [kv_spec(off, row) for off in (8, 10) for row in rows]


def swa_fwd(p2, sinkrow, *, name):
    T = p2.shape[0]
    N = T // BLK

    def body(q, k0, kp, kc, v0, vp, vc, sink_ref, o_ref):
        g, n = pl.program_id(0), pl.program_id(1)
        f32 = lambda ref: ref[...].astype(F32)
        o = _swa_block(_heads(f32(q)), f32(k0), f32(kp), f32(kc), f32(v0), f32(vp), f32(vc),
                       _pick4(sink_ref[...], 4 * g), n)
        o_ref[...] = _unheads(o).astype(BF16)

    q, kv = _swa_specs()
    return pl.pallas_call(
        body, name=name, grid=(2, N), in_specs=[q] + kv + [_const_spec((1, LANES))],
        out_specs=q, out_shape=jax.ShapeDtypeStruct((T, 1024), BF16),
        compiler_params=_params(("arbitrary", "arbitrary")),
    )(p2, p2, p2, p2, p2, p2, p2, sinkrow)


def swa_bwd(p2, sinkrow, dmixed, *, name):
    T = p2.shape[0]
    N = T // BLK

    def body(q, k0, kp, kc, v0, vp, vc, sink_ref, do_ref, dq_ref, dk_ref, dv_ref, dsink_ref):
        g, n = pl.program_id(0), pl.program_id(1)

        @pl.when(n == 0)
        def _():
            dk_ref[...] = jnp.zeros_like(dk_ref)
            dv_ref[...] = jnp.zeros_like(dv_ref)

        @pl.when((g == 0) & (n == 0))
        def _():
            dsink_ref[...] = jnp.zeros_like(dsink_ref)

        f = lambda *a: _swa_block(*a, n)
        f32 = lambda ref: ref[...].astype(F32)
        _, vjp = jax.vjp(f, _heads(f32(q)), f32(k0), f32(kp), f32(kc), f32(v0), f32(vp), f32(vc),
                         _pick4(sink_ref[...], 4 * g))
        dq, dk0, dkp, dkc, dv0, dvp, dvc, dsink = vjp(_heads(do_ref[...]))
        dq_ref[...] = _unheads(dq)
        prev = pl.ds(pl.multiple_of(jnp.maximum(n - 1, 0) * BLK, BLK), BLK)
        cur = pl.ds(pl.multiple_of(n * BLK, BLK), BLK)
        for ref, d0, dp, dc in ((dk_ref, dk0, dkp, dkc), (dv_ref, dv0, dvp, dvc)):
            ref[0:BLK, :] += d0
            ref[prev, :] += dp
            ref[cur, :] += dc
        dsink_ref[...] += _spread4(dsink, 4 * g, 1)

    qspec, kv = _swa_specs()
    slab = pl.BlockSpec((T, LANES), lambda g, n: (0, g))
    return pl.pallas_call(
        body, name=name, grid=(2, N), in_specs=[qspec] + kv + [_const_spec((1, LANES)), qspec],
        out_specs=[qspec, slab, slab, _const_spec((1, LANES))],
        out_shape=[jax.ShapeDtypeStruct((T, 1024), F32), jax.ShapeDtypeStruct((T, 256), F32),
                   jax.ShapeDtypeStruct((T, 256), F32), jax.ShapeDtypeStruct((1, LANES), F32)],
        compiler_params=_params(("arbitrary", "arbitrary")),
    )(p2, p2, p2, p2, p2, p2, p2, sinkrow, dmixed)


def _split_dot(x, m):
    rows = x.shape[0]
    hi = x.astype(BF16)
    lo = (x - hi.astype(F32)).astype(BF16)
    r = _nn(jnp.concatenate([hi, lo], axis=0), m)
    return r[:rows] + r[rows:]


def _tri_and_ones(strict, ones=True):
    i = np.arange(BLK)
    m = (i[:, None] > i[None, :]) if strict else (i[:, None] >= i[None, :])
    if ones:
        m = np.concatenate([m, np.ones((BLK, BLK), bool)], axis=1)
    return jnp.asarray(m.astype(np.float32), dtype=BF16)


def _later_and_row_sums(x, m):
    r = _split_dot(x, m)
    if m.shape[1] == 2 * BLK:
        return r[:, :BLK], r[:, BLK:]
    return r, jnp.broadcast_to(jnp.sum(x, axis=1, keepdims=True), x.shape)


SB_PAIR = 2
SB_FWD_GROUP = 4


def _sb_positions():
    r, s = _iota2((BLK, BLK), 0), _iota2((BLK, BLK), 1)
    return s - r, s


def _sb_weights(qbs, ks, base, n, pos, carries, after):
    nh, kb = len(qbs), len(ks[0])
    zs = [[_nt(qbs[h], ks[h][c]) for c in range(kb)] for h in range(nh)]
    valid = [(pos[0] < (n - base - c) * BLK) & (pos[1] >= PAD - (base + c) * BLK) for c in range(kb)]
    lb = [[None] * kb for _ in range(nh)]
    sums = [[None] * kb for _ in range(nh)]
    for c in range(kb):
        for h in range(nh):
            z = zs[h][c]
            lb[h][c] = jnp.minimum(z, 0.0) - jnp.log(1.0 + jnp.exp(-jnp.abs(z)))
            sums[h][c] = _later_and_row_sums(jnp.where(valid[c], lb[h][c] - z, 0.0), after)
    a = [[None] * kb for _ in range(nh)]
    carries = list(carries)
    for c in reversed(range(kb)):
        for h in range(nh):
            a[h][c] = jnp.where(valid[c], jnp.exp(lb[h][c] + carries[h] + sums[h][c][0]), 0.0)
            carries[h] = carries[h] + sums[h][c][1]
    return valid, lb, a, carries


def _key_blocks(n_blocks):
    return next(k for k in (5, 3, 1) if n_blocks % k == 0)


def sb_fwd(p2, *, name):
    T = p2.shape[0]
    N = T // BLK
    kb = _key_blocks(N)
    nh = SB_FWD_GROUP
    heads = [slice(h * LANES, (h + 1) * LANES) for h in range(nh)]

    def body(q_ref, k_ref, v_ref, after_ref, o_ref, of_ref):
        n = pl.program_id(1)
        qbs = [(q_ref[:, hs].astype(F32) * (64.0 ** -0.5)).astype(BF16) for hs in heads]
        after, pos = after_ref[...], _sb_positions()
        nsup = n // kb + 1

        def step(t, c):
            accs, carries = c
            base = (nsup - 1 - t) * kb
            rows = [pl.ds(pl.multiple_of((base + sub) * BLK, BLK), BLK) for sub in range(kb)]
            ks = [[k_ref[r, hs] for r in rows] for hs in heads]
            _, _, a, carries = _sb_weights(qbs, ks, base, n, pos, carries, after)
            accs = list(accs)
            for sub, r in enumerate(rows):
                for h, hs in enumerate(heads):
                    accs[h] = accs[h] + _nn(a[h][sub].astype(BF16), v_ref[r, hs])
            return accs, carries

        zero = [jnp.zeros((BLK, LANES), F32)] * nh
        accs, _ = lax.fori_loop(0, nsup, step, (zero, zero))
        acc = jnp.concatenate(accs, axis=1)
        o_ref[...] = acc.astype(BF16)
        of_ref[...] = acc

    wide = nh * LANES

    def slab(off):
        return pl.BlockSpec((T, wide), lambda g, n: (0, off + g))

    def blk(off):
        return pl.BlockSpec((BLK, wide), lambda g, n: (n, off + g))

    return pl.pallas_call(
        body, name=name, grid=(8 // nh, N),
        in_specs=[blk(12 // nh), slab(20 // nh), slab(28 // nh), _const_spec((BLK, BLK))],
        out_specs=[blk(0), blk(0)],
        out_shape=[jax.ShapeDtypeStruct((T, 1024), BF16), jax.ShapeDtypeStruct((T, 1024), F32)],
        compiler_params=_params(("arbitrary", "arbitrary")),
    )(p2, p2, p2, _tri_and_ones(True, ones=False))


def sb_bwd(p2, o, dmixed, *, name):
    T = p2.shape[0]
    N = T // BLK
    kb = _key_blocks(N)

    heads = [slice(h * LANES, (h + 1) * LANES) for h in range(SB_PAIR)]
    scale = 64.0 ** -0.5

    def body(q_ref, k_ref, v_ref, after_ref, from_ref, o_ref, do_ref, dq_ref, dk_ref, dv_ref, dkt_scr, dvt_scr):
        n = pl.program_id(1)

        @pl.when(n == 0)
        def _():
            dkt_scr[...] = jnp.zeros_like(dkt_scr)
            dvt_scr[...] = jnp.zeros_like(dvt_scr)

        qbs, qts, dobs, dots, totals = [], [], [], [], []
        for hs in heads:
            qs = q_ref[:, hs].astype(F32) * scale
            do = do_ref[:, hs]
            qbs.append(qs.astype(BF16))
            qts.append(qs.T.astype(BF16))
            dobs.append(do.astype(BF16))
            dots.append(do.T.astype(BF16))
            total = jnp.sum(dobs[-1].astype(F32) * o_ref[:, hs], axis=1, keepdims=True)
            totals.append(jnp.broadcast_to(total, (BLK, LANES)))
        after, frm, pos = after_ref[...], from_ref[...], _sb_positions()
        nsup = n // kb + 1

        def step(t, c):
            dqs, carries, gcarries = c
            base = (nsup - 1 - t) * kb
            rows = [pl.ds(pl.multiple_of((base + sub) * BLK, BLK), BLK) for sub in range(kb)]
            ks = [[k_ref[r, hs] for r in rows] for hs in heads]
            valid, lb, a, carries = _sb_weights(qbs, ks, base, n, pos, carries, after)
            das = [[_nt(dobs[h], v_ref[r, hs]) for r in rows] for h, hs in enumerate(heads)]
            ab = [[a[h][sub].astype(BF16) for sub in range(kb)] for h in range(SB_PAIR)]
            g = [[None] * kb for _ in heads]
            sums = [[None] * kb for _ in heads]
            for sub in range(kb):
                for h in range(SB_PAIR):
                    g[h][sub] = das[h][sub] * ab[h][sub].astype(F32)
                    sums[h][sub] = _later_and_row_sums(g[h][sub], frm)
            dqs, gcarries = list(dqs), list(gcarries)
            for sub in reversed(range(kb)):
                for h in range(SB_PAIR):
                    before = totals[h] - (gcarries[h] + sums[h][sub][0])
                    gcarries[h] = gcarries[h] + sums[h][sub][1]
                    beta = jnp.exp(lb[h][sub])
                    dz = jnp.where(valid[sub], g[h][sub] - beta * (g[h][sub] + before), 0.0).astype(BF16)
                    dqs[h] = dqs[h] + _nn(dz, ks[h][sub])
                    dkt_scr[h * N + base + sub] += _nn(qts[h], dz)
                    dvt_scr[h * N + base + sub] += _nn(dots[h], ab[h][sub])
            return dqs, carries, gcarries

        zero = [jnp.zeros((BLK, LANES), F32)] * SB_PAIR
        dqs, _, _ = lax.fori_loop(0, nsup, step, (zero, zero, zero))
        dq_ref[...] = (jnp.concatenate(dqs, axis=1) * scale).astype(dq_ref.dtype)

        @pl.when(n == N - 1)
        def _():
            def flush(j, _):
                rows = pl.ds(pl.multiple_of(j * BLK, BLK), BLK)
                for h, hs in enumerate(heads):
                    dk_ref[rows, hs] = dkt_scr[h * N + j].T.astype(dk_ref.dtype)
                    dv_ref[rows, hs] = dvt_scr[h * N + j].T.astype(dv_ref.dtype)
                return 0

            lax.fori_loop(0, N, flush, 0)

    wide = SB_PAIR * LANES

    def slab(off):
        return pl.BlockSpec((T, wide), lambda g, n: (0, off + g))

    def blk(off):
        return pl.BlockSpec((BLK, wide), lambda g, n: (n, off + g))

    tri = _const_spec((BLK, BLK))
    return pl.pallas_call(
        body, name=name, grid=(8 // SB_PAIR, N),
        in_specs=[blk(12 // SB_PAIR), slab(20 // SB_PAIR), slab(28 // SB_PAIR), tri, tri, blk(0), blk(8 // SB_PAIR)],
        out_specs=[blk(0), slab(0), slab(0)],
        out_shape=[jax.ShapeDtypeStruct((T, 1024), BF16)] * 3,
        scratch_shapes=[pltpu.VMEM((SB_PAIR * N, LANES, LANES), F32), pltpu.VMEM((SB_PAIR * N, LANES, LANES), F32)],
        compiler_params=_params(("arbitrary", "arbitrary")),
    )(p2, p2, p2, _tri_and_ones(True, ones=False), _tri_and_ones(False, ones=False), o, dmixed)


def ffn_fwd(h, g_pre, g_post, wg, wu, wd, tag):
    u, gate, up, act = norm_mm(h, g_pre, (wg, wu), swiglu=True, wt=True, name=f"ffn_up_{tag}")
    y, h_new = mm_norm_res([act], [wd], h, g_post, 0.5, name=f"ffn_down_{tag}")
    return h_new, (h, u, gate, up, y)


def ffn_bwd(saved, dh, g_pre, g_post, wg, wu, wd, tag):
    h, u, gate, up, y = saved
    dy, dg_post, dgate, dup, act = normbwd_mm_nt(dh, y, g_post, wd, 0.5, (gate, up), name=f"ffn_bwd_down_{tag}")
    dwd = mm_tn(act, dy, name=f"ffn_dwd_{tag}")
    dwg = mm_tn(dgate, u, name=f"ffn_dwg_{tag}")
    dwu = mm_tn(dup, u, name=f"ffn_dwu_{tag}")
    dh_in, dg_pre = mm_nt_normbwd([dgate, dup], [wg, wu], h, g_pre, dh, wt=True, name=f"ffn_bwd_up_{tag}")
    return dh_in, (dg_pre, dg_post), (dwg, dwu, dwd)


def _lane_row(v):
    v = v.reshape(1, -1)
    return jnp.pad(v, ((0, 0), (0, LANES - v.shape[1])))


AB_WIDTHS = (512,) * 8 + (LANES,)


def mixer_ab_fwd(h, g_pre, g_post, w_in, conv_w, a_log, dt_bias, out_norm, w_out, tables):
    u, p = norm_mm(h, g_pre, (w_in,), swiglu=False, name="ab_in")
    ret, sall_r = retention_fwd(p, tables, name="retention_fwd")
    act = conv_silu_fwd(p, conv_w, name="conv_fwd")
    gdn, sall_g = gdn_fwd(act, p, _lane_row(a_log), _lane_row(dt_bias), out_norm.reshape(1, LANES), name="gdn_fwd")
    y, h_new = mm_norm_res([ret, gdn], [w_out[:512], w_out[512:]], h, g_post, 1.0, name="ab_out")
    return h_new, (h, u, p, ret, sall_r, act, gdn, sall_g, y)


def mixer_ab_bwd(saved, dh, g_pre, g_post, w_in, conv_w, a_log, dt_bias, out_norm, w_out, tables):
    h, u, p, ret, sall_r, act, gdn, sall_g, y = saved
    dy, dg_post, dmixed = normbwd_mm_nt(dh, y, g_post, w_out, 1.0, name="ab_bwd_out")
    dw_out = jnp.concatenate([mm_tn(ret, dy, name="ab_dwout_ret"), mm_tn(gdn, dy, name="ab_dwout_gdn")], axis=0)
    pieces = list(retention_bwd(p, sall_r, dmixed, tables, name="retention_bwd"))
    dqa, dka, dva, dz, dba, dalog, ddtb, donorm = gdn_bwd(
        act, p, _lane_row(a_log), _lane_row(dt_bias), out_norm.reshape(1, LANES), sall_g, dmixed, name="gdn_bwd")
    dconv = []
    for part, dact in enumerate((dqa, dka, dva)):
        dx, dw = conv_silu_bwd(p, conv_w, dact, part, name=f"conv_bwd_{part}")
        pieces.append(dx)
        dconv.append(dw)
    pieces += [dz, dba]
    offs = np.cumsum((0,) + AB_WIDTHS)
    w_parts = [w_in[:, a:b] for a, b in zip(offs[:-1], offs[1:])]
    dh_in, dg_pre = mm_nt_normbwd(pieces, w_parts, h, g_pre, dh, name="ab_bwd_in")
    dw_in = jnp.concatenate([mm_tn(u, pc, name=f"ab_dwin_{i}") for i, pc in enumerate(pieces)], axis=1)
    small = (jnp.concatenate(dconv, axis=1), dalog[:, :4], ddtb[:, :4], donorm)
    return dh_in, (dg_pre, dg_post), (dw_in, dw_out), small


CD_WIDTHS = (1024, 256, 256, 1024, 1024, 1024)


def mixer_cd_fwd(h, g_pre, g_post, w_in, sinks, w_out):
    u, p2 = norm_mm(h, g_pre, (w_in,), swiglu=False, out_dtype=BF16, name="cd_in")
    swa = swa_fwd(p2, _lane_row(sinks), name="swa_fwd")
    sb, sb_f32 = sb_fwd(p2, name="sb_fwd")
    y, h_new = mm_norm_res([swa, sb], [w_out[:1024], w_out[1024:]], h, g_post, 1.0, name="cd_out")
    return h_new, (h, u, p2, swa, sb, sb_f32, y)


def mixer_cd_bwd(saved, dh, g_pre, g_post, w_in, sinks, w_out):
    h, u, p2, swa, sb, sb_f32, y = saved
    dy, dg_post, dmixed = normbwd_mm_nt(dh, y, g_post, w_out, 1.0, name="cd_bwd_out")
    dw_out = jnp.concatenate([mm_tn(swa, dy, name="cd_dwout_swa"), mm_tn(sb, dy, name="cd_dwout_sb")], axis=0)
    dq_c, dk_c, dv_c, dsink = swa_bwd(p2, _lane_row(sinks), dmixed, name="swa_bwd")
    pieces = [dq_c, dk_c, dv_c] + list(sb_bwd(p2, sb_f32, dmixed, name="sb_bwd"))
    offs = np.cumsum((0,) + CD_WIDTHS)
    w_parts = [w_in[:, a:b] for a, b in zip(offs[:-1], offs[1:])]
    dh_in, dg_pre = mm_nt_normbwd(pieces, w_parts, h, g_pre, dh, name="cd_bwd_in")
    dw_in = jnp.concatenate([mm_tn(u, pc, name=f"cd_dwin_{i}") for i, pc in enumerate(pieces)], axis=1)
    return dh_in, (dg_pre, dg_post), (dw_in, dw_out), dsink[:, :8]


def _pad_heads(w, axis):
    shape = w.shape
    w = w.reshape(shape[:axis] + (shape[axis] // 64, 64) + shape[axis + 1:])
    pad = [(0, 0)] * w.ndim
    pad[axis + 1] = (0, 64)
    return jnp.pad(w, pad).reshape(shape[:axis] + (2 * shape[axis],) + shape[axis + 1:])


def _unpad_heads(w, axis):
    shape = w.shape
    w = w.reshape(shape[:axis] + (shape[axis] // 128, 128) + shape[axis + 1:])
    w = lax.slice_in_dim(w, 0, 64, axis=axis + 1)
    return w.reshape(shape[:axis] + (shape[axis] // 2,) + shape[axis + 1:])


SMALL_SHARDED = (("meta_tokens", (NMETA, LANES), 1), ("norm_gains", (2, 6, LANES), 2), ("ab_conv_w", (1, 4, 192), 2))
SMALL_REPL = (("ab_a_log", (1, 4)), ("ab_dt_bias", (1, 4)), ("ab_out_norm", (1, LANES)), ("cd_sinks", (1, 8)))


def _stack_shards(g, axis):
    full = jnp.moveaxis(g, 0, axis)
    shape = full.shape
    return full.reshape(shape[:axis] + (shape[axis] * shape[axis + 1],) + shape[axis + 2:])


def _split_shards(full, axis):
    shape = full.shape
    g = full.reshape(shape[:axis] + (NDEV, shape[axis] // NDEV) + shape[axis + 1:])
    return jnp.moveaxis(g, axis, 0)


def _pad_rows8(a):
    rows = []
    for x in a:
        flat = x.reshape(x.shape[0], -1)
        n = -(-flat.shape[1] // LANES) * LANES
        rows.append(jnp.pad(flat, ((0, 0), (0, n - flat.shape[1]))).reshape(x.shape[0], n // LANES, LANES))
    cat = jnp.concatenate(rows, axis=1)
    return jnp.pad(cat, ((0, 0), (0, -cat.shape[1] % 8), (0, 0)))


def _unpad_rows8(packed, shapes):
    out, at = [], 0
    for shape in shapes:
        size = int(np.prod(shape))
        nrow = -(-size // LANES)
        blk = packed[:, at:at + nrow].reshape(packed.shape[0], -1)[:, :size]
        out.append(blk.reshape((packed.shape[0],) + tuple(shape)))
        at += nrow
    return out


def kernel(x, meta_tokens, norm_gains, ffn_w_gate, ffn_w_up, ffn_w_down, ab_w_in, ab_conv_w, ab_a_log, ab_dt_bias, ab_out_norm, ab_w_out, cd_w_in, cd_sinks, cd_w_out, loss_target, m_meta_tokens, m_norm_gains, m_ffn_w_gate, m_ffn_w_up, m_ffn_w_down, m_ab_w_in, m_ab_conv_w, m_ab_a_log, m_ab_dt_bias, m_ab_out_norm, m_ab_w_out, m_cd_w_in, m_cd_sinks, m_cd_w_out, v_meta_tokens, v_norm_gains, v_ffn_w_gate, v_ffn_w_up, v_ffn_w_down, v_ab_w_in, v_ab_conv_w, v_ab_a_log, v_ab_dt_bias, v_ab_out_norm, v_ab_w_out, v_cd_w_in, v_cd_sinks, v_cd_w_out):
    w = dict(meta_tokens=meta_tokens, norm_gains=norm_gains, ffn_w_gate=ffn_w_gate, ffn_w_up=ffn_w_up,
             ffn_w_down=ffn_w_down, ab_w_in=ab_w_in, ab_conv_w=ab_conv_w, ab_a_log=ab_a_log, ab_dt_bias=ab_dt_bias,
             ab_out_norm=ab_out_norm, ab_w_out=ab_w_out, cd_w_in=cd_w_in, cd_sinks=cd_sinks, cd_w_out=cd_w_out)
    m = dict(meta_tokens=m_meta_tokens, norm_gains=m_norm_gains, ffn_w_gate=m_ffn_w_gate, ffn_w_up=m_ffn_w_up,
             ffn_w_down=m_ffn_w_down, ab_w_in=m_ab_w_in, ab_conv_w=m_ab_conv_w, ab_a_log=m_ab_a_log,
             ab_dt_bias=m_ab_dt_bias, ab_out_norm=m_ab_out_norm, ab_w_out=m_ab_w_out, cd_w_in=m_cd_w_in,
             cd_sinks=m_cd_sinks, cd_w_out=m_cd_w_out)
    v = dict(meta_tokens=v_meta_tokens, norm_gains=v_norm_gains, ffn_w_gate=v_ffn_w_gate, ffn_w_up=v_ffn_w_up,
             ffn_w_down=v_ffn_w_down, ab_w_in=v_ab_w_in, ab_conv_w=v_ab_conv_w, ab_a_log=v_ab_a_log,
             ab_dt_bias=v_ab_dt_bias, ab_out_norm=v_ab_out_norm, ab_w_out=v_ab_w_out, cd_w_in=v_cd_w_in,
             cd_sinks=v_cd_sinks, cd_w_out=v_cd_w_out)
    order = list(w)
    S = x.shape[1]
    T = S + BLK

    fs = DFF // NDEV
    ffn_local = jnp.concatenate([jnp.swapaxes(ffn_w_gate, 2, 3).reshape(4 * fs, D),
                                 jnp.swapaxes(ffn_w_up, 2, 3).reshape(4 * fs, D), ffn_w_down.reshape(4 * fs, D)],
                                axis=0).astype(BF16)
    outs_local = jnp.concatenate([ab_w_out[0], cd_w_out[0]], axis=0).astype(BF16)
    ffn_all, abin_all, outs_all, cdin_all = all_gather_big(
        [ffn_local, ab_w_in[0].astype(BF16), outs_local, cd_w_in[0].astype(BF16)], name="gather_weights")
    ffn_mat = lambda k: ffn_all[:, k * fs:(k + 1) * fs].reshape(DFF, D)
    layers = [(i, j) for i in range(2) for j in range(2)]
    wg = {ij: ffn_mat(k) for k, ij in enumerate(layers)}
    wu = {ij: ffn_mat(4 + k) for k, ij in enumerate(layers)}
    wd = {ij: ffn_mat(8 + k) for k, ij in enumerate(layers)}
    ab_in = jnp.pad(_stack_shards(abin_all, 1), ((0, 0), (0, AB_INP - AB_IN)))
    ab_out = outs_all[:, :D // NDEV].reshape(D, D)
    cd_in = _pad_heads(_stack_shards(cdin_all, 1), 1)
    cd_out = _pad_heads(outs_all[:, D // NDEV:].reshape(D, D), 0)
    small_src = jnp.broadcast_to(_pad_rows8([w[n][None] for n, _, _ in SMALL_SHARDED]), (NDEV, 40, LANES))
    small_all = _unpad_rows8(all_to_all_small(small_src, name="gather_small"), [s for _, s, _ in SMALL_SHARDED])
    full = {n: _stack_shards(g, ax) for (n, _, ax), g in zip(SMALL_SHARDED, small_all)}
    conv_w = full["ab_conv_w"][0]
    gains = full["norm_gains"].reshape(2, 6, 1, D)
    tables = retention_tables(T)

    h = jnp.concatenate([jnp.zeros((PAD, D), F32), full["meta_tokens"], x[0]], axis=0)
    h, s00 = ffn_fwd(h, gains[0, 0], gains[0, 1], wg[0, 0], wu[0, 0], wd[0, 0], "00")
    h, sab = mixer_ab_fwd(h, gains[0, 2], gains[0, 3], ab_in, conv_w, ab_a_log, ab_dt_bias, ab_out_norm, ab_out, tables)
    h, s01 = ffn_fwd(h, gains[0, 4], gains[0, 5], wg[0, 1], wu[0, 1], wd[0, 1], "01")
    h, s10 = ffn_fwd(h, gains[1, 0], gains[1, 1], wg[1, 0], wu[1, 0], wd[1, 0], "10")
    h, scd = mixer_cd_fwd(h, gains[1, 2], gains[1, 3], cd_in, cd_sinks, cd_out)
    h, s11 = ffn_fwd(h, gains[1, 4], gains[1, 5], wg[1, 1], wu[1, 1], wd[1, 1], "11")
    loss_tile, dh = loss_and_grad(h, loss_target[0], name="loss")
    loss = lax.psum(loss_tile[0, 0], ("x", "y", "c"))

    dgain = [[None] * 6, [None] * 6]
    dffn = {}
    dh, (dgain[1][4], dgain[1][5]), dffn[1, 1] = ffn_bwd(s11, dh, gains[1, 4], gains[1, 5], wg[1, 1], wu[1, 1], wd[1, 1], "11")
    dh, (dgain[1][2], dgain[1][3]), (dcd_in, dcd_out), dsinks = mixer_cd_bwd(scd, dh, gains[1, 2], gains[1, 3], cd_in, cd_sinks, cd_out)
    dh, (dgain[1][0], dgain[1][1]), dffn[1, 0] = ffn_bwd(s10, dh, gains[1, 0], gains[1, 1], wg[1, 0], wu[1, 0], wd[1, 0], "10")
    dh, (dgain[0][4], dgain[0][5]), dffn[0, 1] = ffn_bwd(s01, dh, gains[0, 4], gains[0, 5], wg[0, 1], wu[0, 1], wd[0, 1], "01")
    dh, (dgain[0][2], dgain[0][3]), (dab_in, dab_out), (dconv, dalog, ddtb, donorm) = mixer_ab_bwd(
        sab, dh, gains[0, 2], gains[0, 3], ab_in, conv_w, ab_a_log, ab_dt_bias, ab_out_norm, ab_out, tables)
    dh, (dgain[0][0], dgain[0][1]), dffn[0, 0] = ffn_bwd(s00, dh, gains[0, 0], gains[0, 1], wg[0, 0], wu[0, 0], wd[0, 0], "00")
    grad_x = dh[BLK:][None]

    gfull = dict(meta_tokens=dh[PAD:BLK], norm_gains=jnp.stack([jnp.concatenate(r, axis=0) for r in dgain]),
                 ab_conv_w=dconv[None])
    ffn_send = jnp.concatenate([dffn[ij][k].astype(BF16).reshape(NDEV, fs, D) for k in range(3) for ij in layers], axis=1)
    outs_send = jnp.concatenate([dab_out.astype(BF16).reshape(NDEV, D // NDEV, D),
                                 _unpad_heads(dcd_out, 0).astype(BF16).reshape(NDEV, D // NDEV, D)], axis=1)
    abin_send = _split_shards(dab_in[:, :AB_IN].astype(BF16), 1)
    cdin_send = _split_shards(_unpad_heads(dcd_in, 1).astype(BF16), 1)
    ffn_g, abin_g, outs_g, cdin_g = reduce_scatter_big([ffn_send, abin_send, outs_send, cdin_send])
    ffn_g = ffn_g.reshape(3, 2, 2, fs, D)
    grads = dict(ffn_w_gate=jnp.swapaxes(ffn_g[0], 2, 3), ffn_w_up=jnp.swapaxes(ffn_g[1], 2, 3), ffn_w_down=ffn_g[2],
                 ab_w_in=abin_g[None], ab_w_out=outs_g[None, :D // NDEV], cd_w_in=cdin_g[None],
                 cd_w_out=outs_g[None, D // NDEV:])
    repl = [jnp.broadcast_to(t[None], (NDEV,) + t.shape) for t in (dalog, ddtb, donorm, dsinks)]
    ssend = _pad_rows8([_split_shards(gfull[n], ax) for n, _, ax in SMALL_SHARDED] + repl)
    ssum = sum_slots(all_to_all_small(ssend, name="exchange_small_grads"), name="sum_small_grads")[None]
    small = _unpad_rows8(ssum, [s for _, s, _ in SMALL_SHARDED] + [s for _, s in SMALL_REPL])
    grads.update({n: g[0] for n, g in zip([n for n, _, _ in SMALL_SHARDED] + [n for n, _ in SMALL_REPL], small)})

    delta, new_m, new_v = {}, {}, {}
    for n in order:
        shape = w[n].shape
        view = (-1, shape[-1])
        d_, m_, v_ = adamw(w[n].reshape(view), grads[n].reshape(view), m[n].reshape(view), v[n].reshape(view),
                           name=f"adamw_{n}")
        delta[n], new_m[n], new_v[n] = d_.reshape(shape), m_.reshape(shape), v_.reshape(shape)
    return (loss, grad_x, *[grads[n] for n in order], *[delta[n] for n in order], *[new_m[n] for n in order],
            *[new_v[n] for n in order])
```

```python
import functools
import math

import numpy as np
import jax
import jax.numpy as jnp
from jax import lax
from jax.experimental import pallas as pl
from jax.experimental.pallas import tpu as pltpu

F32, BF16 = jnp.float32, jnp.bfloat16
EPS = 1e-6
D = 1024
NMETA = 16
BLK = 128
PAD = BLK - NMETA
DFF = 2816
LANES = 128
NDEV = 8
AB_IN, AB_INP = 4104, 4224
ADAM_LR, ADAM_B1, ADAM_B2, ADAM_EPS, ADAM_WD, ADAM_STEP = 0.001, 0.9, 0.999, 1e-08, 0.01, 10
VMEM_LIMIT = 56 * 1024 * 1024
MESH = pl.DeviceIdType.MESH
HIGH = lax.Precision.HIGH


def _params(sem):
    return pltpu.CompilerParams(dimension_semantics=sem, vmem_limit_bytes=VMEM_LIMIT)


def _row_tile(T, streamed, resident):
    for tm in (640, 320, 128):
        if T % tm == 0 and 2 * (tm * streamed + resident) <= VMEM_LIMIT - 14 * 1024 * 1024:
            return tm
    return _tile(T, 128)


MXU_COLS = 256


def _col_chunks(n):
    return [slice(c, min(c + MXU_COLS, n)) for c in range(0, n, MXU_COLS)]


def _tile(n, cap, unit=LANES):
    if n <= cap:
        return n
    best = None
    for t in range(unit, cap + 1, unit):
        if n % t == 0:
            best = t
    assert best is not None, (n, cap)
    return best


def _rms_fwd(x, g):
    return x * lax.rsqrt(jnp.mean(x * x, axis=-1, keepdims=True) + EPS) * g


def _rms_bwd(x, g, dz):
    r = lax.rsqrt(jnp.mean(x * x, axis=-1, keepdims=True) + EPS)
    xh = x * r
    dg = jnp.sum(dz * xh, axis=0, keepdims=True)
    t = dz * g
    return r * (t - xh * jnp.mean(t * xh, axis=-1, keepdims=True)), dg


def _sigmoid(x):
    return 0.5 * jnp.tanh(0.5 * x) + 0.5


def _silu(x):
    return x * _sigmoid(x)


def _nn(a, b, precision=None):
    return lax.dot_general(a, b, (((1,), (0,)), ((), ())), preferred_element_type=F32, precision=precision)


def _nt(a, b):
    return lax.dot_general(a, b, (((1,), (1,)), ((), ())), preferred_element_type=F32)


def _tn(a, b):
    return lax.dot_general(a, b, (((0,), (0,)), ((), ())), preferred_element_type=F32)


def _mm(a, b, precision=None):
    if a.ndim == 3:
        return lax.dot_general(a, b, (((2,), (1,)), ((0,), (0,))), preferred_element_type=F32, precision=precision)
    return _nn(a, b, precision)


def _t(x):
    return jnp.swapaxes(x, -1, -2)


@jax.custom_vjp
def bdot(a, b):
    return _mm(a.astype(BF16), b.astype(BF16))


def _bdot_fwd(a, b):
    return bdot(a, b), (a, b)


def _bdot_bwd(res, g):
    a, b = res
    return bdot(g, _t(b)), bdot(_t(a), g)


bdot.defvjp(_bdot_fwd, _bdot_bwd)


@jax.custom_vjp
def hdot(a, b):
    return _mm(a, b, HIGH)


def _hdot_fwd(a, b):
    return hdot(a, b), (a, b)


def _hdot_bwd(res, g):
    a, b = res
    return hdot(g, _t(b)), hdot(_t(a), g)


hdot.defvjp(_hdot_fwd, _hdot_bwd)


def _iota2(shape, axis):
    return lax.broadcasted_iota(jnp.int32, shape, axis)


def _lane_pick(row, lane):
    return jnp.sum(jnp.where(_iota2(row.shape, 1) == lane, row, 0.0), axis=1, keepdims=True)


def norm_mm(h, gain, ws, *, swiglu, name, wt=False, out_dtype=F32):
    T, Dm = h.shape
    N = ws[0].shape[0 if wt else 1]
    tm, tn = _tile(T, 640), _tile(N, 1408)
    nw = len(ws)
    mm = _nt if wt else _nn

    def body(h_ref, g_ref, *refs):
        w_refs, u_ref, o_refs = refs[:nw], refs[nw], refs[nw + 1:]

        @pl.when(pl.program_id(1) == 0)
        def _():
            u_ref[...] = _rms_fwd(h_ref[...], g_ref[...]).astype(BF16)

        u = u_ref[...]
        for cols in _col_chunks(tn):
            acc = [mm(u, w[cols, :] if wt else w[:, cols]) for w in w_refs]
            if swiglu:
                o_refs[0][:, cols] = acc[0].astype(BF16)
                o_refs[1][:, cols] = acc[1].astype(BF16)
                o_refs[2][:, cols] = (_silu(acc[0]) * acc[1]).astype(BF16)
            else:
                o_refs[0][:, cols] = acc[0].astype(out_dtype)

    row = pl.BlockSpec((tm, Dm), lambda i, j: (i, 0))
    tile = pl.BlockSpec((tm, tn), lambda i, j: (i, j))
    if swiglu:
        out_shape = [jax.ShapeDtypeStruct((T, Dm), BF16)] + [jax.ShapeDtypeStruct((T, N), BF16)] * 3
        out_specs = [row, tile, tile, tile]
    else:
        out_shape = [jax.ShapeDtypeStruct((T, Dm), BF16), jax.ShapeDtypeStruct((T, N), out_dtype)]
        out_specs = [row, tile]
    return pl.pallas_call(
        body, name=name, grid=(T // tm, N // tn),
        in_specs=[row, pl.BlockSpec((1, Dm), lambda i, j: (0, 0))]
        + [pl.BlockSpec((tn, Dm), lambda i, j: (j, 0)) if wt else pl.BlockSpec((Dm, tn), lambda i, j: (0, j))] * nw,
        out_specs=out_specs, out_shape=out_shape,
        compiler_params=_params(("arbitrary", "arbitrary")),
    )(h, gain, *ws)


def mm_norm_res(As, Ws, h, gain, scale, *, name):
    T, Dm = h.shape
    n = len(As)
    tm = _row_tile(T, sum(a.shape[1] * a.dtype.itemsize for a in As) + 3 * Dm * 4,
                   sum(w.size * w.dtype.itemsize for w in Ws))

    def body(*refs):
        a_refs, w_refs = refs[:n], refs[n:2 * n]
        h_ref, g_ref, y_ref, hn_ref = refs[2 * n:]
        y = _nn(a_refs[0][...].astype(BF16), w_refs[0][...])
        for a, w in zip(a_refs[1:], w_refs[1:]):
            y = y + _nn(a[...].astype(BF16), w[...])
        y_ref[...] = y
        hn_ref[...] = h_ref[...] + scale * _rms_fwd(y, g_ref[...])

    row = pl.BlockSpec((tm, Dm), lambda i: (i, 0))
    return pl.pallas_call(
        body, name=name, grid=(T // tm,),
        in_specs=[pl.BlockSpec((tm, a.shape[1]), lambda i: (i, 0)) for a in As]
        + [pl.BlockSpec(w.shape, lambda i: (0, 0)) for w in Ws]
        + [row, pl.BlockSpec((1, Dm), lambda i: (0, 0))],
        out_specs=[row, row], out_shape=[jax.ShapeDtypeStruct((T, Dm), F32)] * 2,
        compiler_params=_params(("arbitrary",)),
    )(*As, *Ws, h, gain)


def normbwd_mm_nt(dh, y, gain, w, scale, gu=None, *, name):
    T, Dm = dh.shape
    N = w.shape[0]
    tm, tn = _tile(T, 640), _tile(N, 1408)
    swiglu = gu is not None

    def body(dh_ref, y_ref, g_ref, w_ref, *refs):
        if swiglu:
            gate_ref, up_ref, dy_ref, dg_ref, dgate_ref, dup_ref, a_ref = refs
        else:
            dy_ref, dg_ref, da_ref = refs
        i, j = pl.program_id(0), pl.program_id(1)

        @pl.when(j == 0)
        def _():
            dy, dg = _rms_bwd(y_ref[...], g_ref[...], scale * dh_ref[...])
            dy_ref[...] = dy.astype(BF16)

            @pl.when(i == 0)
            def _():
                dg_ref[...] = jnp.zeros_like(dg_ref)

            dg_ref[...] += dg

        dy = dy_ref[...]
        for cols in _col_chunks(tn):
            da = _nt(dy, w_ref[cols, :])
            if swiglu:
                gate, up = gate_ref[:, cols].astype(F32), up_ref[:, cols].astype(F32)
                s = _sigmoid(gate)
                dgate_ref[:, cols] = (da * up * s * (1.0 + gate * (1.0 - s))).astype(BF16)
                dup_ref[:, cols] = (da * gate * s).astype(BF16)
                a_ref[:, cols] = (gate * s * up).astype(BF16)
            else:
                da_ref[:, cols] = da

    row = pl.BlockSpec((tm, Dm), lambda i, j: (i, 0))
    vec = pl.BlockSpec((1, Dm), lambda i, j: (0, 0))
    tile = pl.BlockSpec((tm, tn), lambda i, j: (i, j))
    in_specs = [row, row, vec, pl.BlockSpec((tn, Dm), lambda i, j: (j, 0))]
    out_shape = [jax.ShapeDtypeStruct((T, Dm), BF16), jax.ShapeDtypeStruct((1, Dm), F32)]
    if swiglu:
        in_specs += [tile, tile]
        out_shape += [jax.ShapeDtypeStruct((T, N), BF16)] * 3
        out_specs = [row, vec, tile, tile, tile]
        args = (dh, y, gain, w, *gu)
    else:
        out_shape += [jax.ShapeDtypeStruct((T, N), F32)]
        out_specs = [row, vec, tile]
        args = (dh, y, gain, w)
    return pl.pallas_call(
        body, name=name, grid=(T // tm, N // tn), in_specs=in_specs, out_specs=out_specs,
        out_shape=out_shape, compiler_params=_params(("arbitrary", "arbitrary")),
    )(*args)


def mm_nt_normbwd(dPs, Ws, h, gain, dh_in, *, name, wt=False):
    T, Dm = h.shape
    n = len(dPs)
    tm = _row_tile(T, sum(p.shape[1] * p.dtype.itemsize for p in dPs) + 3 * Dm * 4,
                   sum(w.size * w.dtype.itemsize for w in Ws))
    mm = _nn if wt else _nt

    def body(*refs):
        p_refs, w_refs = refs[:n], refs[n:2 * n]
        h_ref, g_ref, dhin_ref, dh_ref, dg_ref = refs[2 * n:]
        du = mm(p_refs[0][...].astype(BF16), w_refs[0][...])
        for p, w in zip(p_refs[1:], w_refs[1:]):
            du = du + mm(p[...].astype(BF16), w[...])
        dx, dg = _rms_bwd(h_ref[...], g_ref[...], du)
        dh_ref[...] = dhin_ref[...] + dx

        @pl.when(pl.program_id(0) == 0)
        def _():
            dg_ref[...] = jnp.zeros_like(dg_ref)

        dg_ref[...] += dg

    row = pl.BlockSpec((tm, Dm), lambda i: (i, 0))
    vec = pl.BlockSpec((1, Dm), lambda i: (0, 0))
    return pl.pallas_call(
        body, name=name, grid=(T // tm,),
        in_specs=[pl.BlockSpec((tm, p.shape[1]), lambda i: (i, 0)) for p in dPs]
        + [pl.BlockSpec(w.shape, lambda i: (0, 0)) for w in Ws] + [row, vec, row],
        out_specs=[row, vec],
        out_shape=[jax.ShapeDtypeStruct((T, Dm), F32), jax.ShapeDtypeStruct((1, Dm), F32)],
        compiler_params=_params(("arbitrary",)),
    )(*dPs, *Ws, h, gain, dh_in)


def mm_tn(a, b, *, name):
    T, M = a.shape
    N = b.shape[1]
    tm, tn, tk = _tile(M, 1408), _tile(N, 1408), _tile(T, 640)

    def body(a_ref, b_ref, o_ref):
        @pl.when(pl.program_id(2) == 0)
        def _():
            o_ref[...] = jnp.zeros_like(o_ref)

        o_ref[...] += _tn(a_ref[...].astype(BF16), b_ref[...].astype(BF16))

    return pl.pallas_call(
        body, name=name, grid=(M // tm, N // tn, T // tk),
        in_specs=[pl.BlockSpec((tk, tm), lambda i, j, k: (k, i)), pl.BlockSpec((tk, tn), lambda i, j, k: (k, j))],
        out_specs=pl.BlockSpec((tm, tn), lambda i, j, k: (i, j)),
        out_shape=jax.ShapeDtypeStruct((M, N), F32),
        compiler_params=_params(("arbitrary", "arbitrary", "arbitrary")),
    )(a, b)


def loss_and_grad(h, target, *, name):
    T, Dm = h.shape

    def body(h_ref, t_ref, loss_ref, dh_ref):
        b = pl.program_id(0)

        @pl.when(b == 0)
        def _():
            loss_ref[...] = jnp.zeros_like(loss_ref)
            dh_ref[...] = jnp.zeros_like(dh_ref)

        @pl.when(b > 0)
        def _():
            e = h_ref[...] - t_ref[...]
            dh_ref[...] = e * (1.0 / Dm)
            loss_ref[...] += jnp.sum(e * e) * (0.5 / Dm)

    return pl.pallas_call(
        body, name=name, grid=(T // BLK,),
        in_specs=[pl.BlockSpec((BLK, Dm), lambda b: (b, 0)),
                  pl.BlockSpec((BLK, Dm), lambda b: (jnp.maximum(b - 1, 0), 0))],
        out_specs=[pl.BlockSpec((8, LANES), lambda b: (0, 0)), pl.BlockSpec((BLK, Dm), lambda b: (b, 0))],
        out_shape=[jax.ShapeDtypeStruct((8, LANES), F32), jax.ShapeDtypeStruct((T, Dm), F32)],
        compiler_params=_params(("arbitrary",)),
    )(h, target)


def adamw(w, g, m, v, *, name):
    R, C = w.shape
    tr = R
    for t in (512, 352, 256):
        if R > t and R % t == 0:
            tr = t
            break

    def body(w_ref, g_ref, m_ref, v_ref, d_ref, nm_ref, nv_ref):
        g_ = g_ref[...]
        m_ = ADAM_B1 * m_ref[...] + (1.0 - ADAM_B1) * g_
        v_ = ADAM_B2 * v_ref[...] + (1.0 - ADAM_B2) * (g_ * g_)
        m_hat = m_ / (1.0 - ADAM_B1 ** ADAM_STEP)
        v_hat = v_ / (1.0 - ADAM_B2 ** ADAM_STEP)
        d_ref[...] = -ADAM_LR * (m_hat / (jnp.sqrt(v_hat) + ADAM_EPS) + ADAM_WD * w_ref[...])
        nm_ref[...] = m_
        nv_ref[...] = v_

    spec = pl.BlockSpec((tr, C), lambda i: (i, 0))
    return pl.pallas_call(
        body, name=name, grid=(R // tr,), in_specs=[spec] * 4, out_specs=[spec] * 3,
        out_shape=[jax.ShapeDtypeStruct((R, C), F32)] * 3, compiler_params=_params(("arbitrary",)),
    )(w, g, m, v)


def _me():
    return lax.axis_index("x"), lax.axis_index("y"), lax.axis_index("c")


def _flip(pos, rel):
    return tuple(1 - p if r else p for p, r in zip(pos, rel))


def _slot(pos):
    return 4 * pos[0] + 2 * pos[1] + pos[2]


HBM_SPEC = pl.BlockSpec(memory_space=pltpu.HBM)
CHIP_RELS = ((1, 0), (0, 1), (1, 1))


def all_gather_big(xs, *, name):
    n = len(xs)

    def body(*refs):
        x_refs, out_refs = refs[:n], refs[n:2 * n]
        send_sems, recv_sems, local_sems = refs[2 * n:]
        me = _me()
        sibling = _flip(me, (0, 0, 1))
        chips = [_flip(me, rel + (0,)) for rel in CHIP_RELS]

        def copy(i, k, block, to, src=None):
            dst = out_refs[i].at[_slot(block)]
            return pltpu.make_async_remote_copy(
                src_ref=dst if src is None else src, dst_ref=dst, send_sem=send_sems.at[i, k],
                recv_sem=recv_sems.at[i, k], device_id=to, device_id_type=MESH)

        sent, local = [], []
        for i in range(n):
            mine = pltpu.make_async_copy(x_refs[i], out_refs[i].at[_slot(me)], local_sems.at[i])
            mine.start()
            local.append(mine)
            sent += [copy(i, 0, me, sibling, src=x_refs[i])]
            sent += [copy(i, 1 + j, me, chip, src=x_refs[i]) for j, chip in enumerate(chips)]
        for cp in sent:
            cp.start()
        for i in range(n):
            for j, chip in enumerate(chips):
                copy(i, 1 + j, chip, me).wait_recv()
                passed = copy(i, 4 + j, chip, sibling)
                passed.start()
                sent.append(passed)
        for i in range(n):
            copy(i, 0, sibling, me).wait_recv()
            for j, chip in enumerate(chips):
                copy(i, 4 + j, _flip(chip, (0, 0, 1)), me).wait_recv()
        for cp in sent:
            cp.wait_send()
        for mine in local:
            mine.wait()

    return pl.pallas_call(
        body, name=name, in_specs=[HBM_SPEC] * n, out_specs=[HBM_SPEC] * n,
        out_shape=[jax.ShapeDtypeStruct((NDEV,) + x.shape, x.dtype) for x in xs],
        scratch_shapes=[pltpu.SemaphoreType.DMA((n, 7)), pltpu.SemaphoreType.DMA((n, 7)), pltpu.SemaphoreType.DMA((n,))],
    )(*xs)


def all_to_all_small(src, *, name):
    _, r, C = src.shape

    def body(src_ref, out_ref, send_sems, recv_sems):
        me = _me()
        my = _slot(me)
        out_ref[my] = src_ref[my]
        copies = []
        for k in range(1, NDEV):
            peer = _flip(me, ((k >> 2) & 1, (k >> 1) & 1, k & 1))
            cp = pltpu.make_async_remote_copy(
                src_ref=src_ref.at[_slot(peer)], dst_ref=out_ref.at[my], send_sem=send_sems.at[k - 1],
                recv_sem=recv_sems.at[k - 1], device_id=peer, device_id_type=MESH)
            cp.start()
            copies.append((cp, peer))
        for k, (cp, peer) in enumerate(copies):
            pltpu.make_async_remote_copy(
                src_ref=src_ref.at[my], dst_ref=out_ref.at[_slot(peer)], send_sem=send_sems.at[k],
                recv_sem=recv_sems.at[k], device_id=peer, device_id_type=MESH).wait_recv()
        for cp, _ in copies:
            cp.wait_send()

    vm = pl.BlockSpec(memory_space=pltpu.VMEM)
    return pl.pallas_call(
        body, name=name, in_specs=[vm], out_specs=vm, out_shape=jax.ShapeDtypeStruct(src.shape, src.dtype),
        scratch_shapes=[pltpu.SemaphoreType.DMA((7,)), pltpu.SemaphoreType.DMA((7,))],
    )(src)


def sum_slots(a, *, name):
    n, r, C = a.shape

    def body(a_ref, o_ref):
        s = a_ref[0]
        for k in range(1, n):
            s = s + a_ref[k]
        o_ref[...] = s

    vm = pl.BlockSpec(memory_space=pltpu.VMEM)
    return pl.pallas_call(body, name=name, in_specs=[vm], out_specs=vm,
                          out_shape=jax.ShapeDtypeStruct((r, C), F32))(a)


def rs_exchange_sibling(gs, *, name):
    n = len(gs)

    def body(*refs):
        g_refs, out_refs, send_sems, recv_sems = refs[:n], refs[n:2 * n], refs[2 * n], refs[2 * n + 1]
        sibling = _flip(_me(), (0, 0, 1))
        copies = []
        for i in range(n):
            for chip in range(4):
                cp = pltpu.make_async_remote_copy(
                    src_ref=g_refs[i].at[2 * chip + sibling[2]], dst_ref=out_refs[i].at[chip],
                    send_sem=send_sems.at[i, chip], recv_sem=recv_sems.at[i, chip], device_id=sibling,
                    device_id_type=MESH)
                cp.start()
                copies.append(cp)
        for cp in copies:
            cp.wait()

    return pl.pallas_call(
        body, name=name, in_specs=[HBM_SPEC] * n, out_specs=[HBM_SPEC] * n,
        out_shape=[jax.ShapeDtypeStruct((4,) + g.shape[1:], g.dtype) for g in gs],
        scratch_shapes=[pltpu.SemaphoreType.DMA((n, 4)), pltpu.SemaphoreType.DMA((n, 4))],
    )(*gs)


def rs_chip_partials(g, got, *, name):
    _, R, C = g.shape
    tr = _tile(R, 768, unit=16)

    def body(c_ref, g_ref, got_ref, o_ref):
        o_ref[...] = (g_ref[...].astype(F32) + got_ref[...].astype(F32)).astype(o_ref.dtype)

    c = jnp.reshape(lax.axis_index("c"), (1,)).astype(jnp.int32)
    return pl.pallas_call(
        body, name=name,
        grid_spec=pltpu.PrefetchScalarGridSpec(
            num_scalar_prefetch=1, grid=(4, R // tr),
            in_specs=[pl.BlockSpec((None, tr, C), lambda k, i, c_ref: (2 * k + c_ref[0], i, 0)),
                      pl.BlockSpec((None, tr, C), lambda k, i, c_ref: (k, i, 0))],
            out_specs=pl.BlockSpec((None, tr, C), lambda k, i, c_ref: (k, i, 0))),
        out_shape=jax.ShapeDtypeStruct((4, R, C), g.dtype), compiler_params=_params(("arbitrary", "arbitrary")),
    )(c, g, got)


def rs_exchange_chips(ps, *, name):
    n = len(ps)

    def body(*refs):
        copies = _chip_exchange_copies(refs[:n], refs[n:2 * n], refs[2 * n], refs[2 * n + 1])
        for cp in copies:
            cp.start()
        for cp in copies:
            cp.wait()

    return pl.pallas_call(
        body, name=name, in_specs=[HBM_SPEC] * n, out_specs=[HBM_SPEC] * n,
        out_shape=_chip_exchange_shapes(ps), scratch_shapes=_chip_exchange_sems(n),
    )(*ps)


def _chip_exchange_copies(p_refs, out_refs, send_sems, recv_sems):
    me = _me()
    copies = []
    for i, (p_ref, out_ref) in enumerate(zip(p_refs, out_refs)):
        for j, rel in enumerate(CHIP_RELS):
            peer = _flip(me, rel + (0,))
            copies.append(pltpu.make_async_remote_copy(
                src_ref=p_ref.at[2 * peer[0] + peer[1]], dst_ref=out_ref.at[j], send_sem=send_sems.at[i, j],
                recv_sem=recv_sems.at[i, j], device_id=peer, device_id_type=MESH))
    return copies


def _chip_exchange_shapes(ps):
    return [jax.ShapeDtypeStruct((3,) + p.shape[1:], p.dtype) for p in ps]


def _chip_exchange_sems(n):
    return [pltpu.SemaphoreType.DMA((n, 3)), pltpu.SemaphoreType.DMA((n, 3))]


def _gather_direct(start, x_refs, out_refs, send_sems, recv_sems, local_sems):
    me = _me()
    peers = [_flip(me, (0, 0, 1))] + [_flip(me, rel + (0,)) for rel in CHIP_RELS]
    for i, (x_ref, out_ref) in enumerate(zip(x_refs, out_refs)):
        local = pltpu.make_async_copy(x_ref, out_ref.at[_slot(me)], local_sems.at[i])
        local.start() if start else local.wait()
        for k, peer in enumerate(peers):
            def copy(block):
                return pltpu.make_async_remote_copy(
                    src_ref=x_ref, dst_ref=out_ref.at[_slot(block)], send_sem=send_sems.at[i, k],
                    recv_sem=recv_sems.at[i, k], device_id=peer, device_id_type=MESH)
            if start:
                copy(me).start()
            else:
                copy(me).wait_send()
                copy(peer).wait_recv()


def _gather_direct_sems(n):
    return [pltpu.SemaphoreType.DMA((n, 4)), pltpu.SemaphoreType.DMA((n, 4)), pltpu.SemaphoreType.DMA((n,))]


def gather_forward_to_sibling(gs, *, name):
    n = len(gs)

    def body(*refs):
        in_refs, out_refs, send_sems, recv_sems = refs[:n], refs[n:2 * n], refs[2 * n], refs[2 * n + 1]
        me = _me()
        sibling = _flip(me, (0, 0, 1))
        chips = [_flip(me, rel + (0,)) for rel in CHIP_RELS]
        sends, recvs = [], []
        for i in range(n):
            for j, chip in enumerate(chips):
                def copy(block):
                    return pltpu.make_async_remote_copy(
                        src_ref=in_refs[i].at[_slot(chip)], dst_ref=out_refs[i].at[_slot(block)],
                        send_sem=send_sems.at[i, j], recv_sem=recv_sems.at[i, j], device_id=sibling, device_id_type=MESH)
                sends.append(copy(chip))
                recvs.append(copy(_flip(chip, (0, 0, 1))))
        for cp in sends:
            cp.start()
        for cp in recvs:
            cp.wait_recv()
        for cp in sends:
            cp.wait_send()

    return pl.pallas_call(
        body, name=name, in_specs=[HBM_SPEC] * n, out_specs=[HBM_SPEC] * n,
        out_shape=[jax.ShapeDtypeStruct(g.shape, g.dtype) for g in gs],
        input_output_aliases={i: i for i in range(n)},
        scratch_shapes=[pltpu.SemaphoreType.DMA((n, 3)), pltpu.SemaphoreType.DMA((n, 3))],
    )(*gs)


def rs_final_sum(p, got, *, name):
    _, R, C = p.shape
    tr = _tile(R, 768, unit=16)

    def body(chip_ref, p_ref, got_ref, o_ref):
        s = p_ref[...].astype(F32)
        for j in range(3):
            s = s + got_ref[j].astype(F32)
        o_ref[...] = s

    mychip = jnp.reshape(2 * lax.axis_index("x") + lax.axis_index("y"), (1,)).astype(jnp.int32)
    return pl.pallas_call(
        body, name=name,
        grid_spec=pltpu.PrefetchScalarGridSpec(
            num_scalar_prefetch=1, grid=(R // tr,),
            in_specs=[pl.BlockSpec((None, tr, C), lambda i, chip_ref: (chip_ref[0], i, 0)),
                      pl.BlockSpec((3, tr, C), lambda i, chip_ref: (0, i, 0))],
            out_specs=pl.BlockSpec((tr, C), lambda i, chip_ref: (i, 0))),
        out_shape=jax.ShapeDtypeStruct((R, C), F32), compiler_params=_params(("arbitrary",)),
    )(mychip, p, got)


def reduce_scatter_big(gs):
    got = rs_exchange_sibling(gs, name="rs_sibling")
    parts = [rs_chip_partials(g, t, name=f"rs_chip_partials_{i}") for i, (g, t) in enumerate(zip(gs, got))]
    got2 = rs_exchange_chips(parts, name="rs_chips")
    return [rs_final_sum(p, t, name=f"rs_final_sum_{i}") for i, (p, t) in enumerate(zip(parts, got2))]


def _blk(off):
    return pl.BlockSpec((BLK, LANES), lambda h, n: (n, off + h))


def _const_spec(shape):
    return pl.BlockSpec(shape, lambda *_: (0,) * len(shape))


def retention_tables(T):
    pos = jnp.arange(T, dtype=F32) - float(PAD)
    inv_freq = 1.0 / (10000.0 ** jnp.linspace(0.0, 1.0, 64, dtype=F32))
    ang = pos[:, None] * inv_freq[None, :]
    cos = jnp.repeat(jnp.cos(ang), 2, axis=1)
    sin = jnp.repeat(jnp.sin(ang), 2, axis=1) * jnp.tile(jnp.array([-1.0, 1.0], F32), 64)[None, :]
    lane = np.arange(LANES)
    perm = jnp.broadcast_to(jnp.asarray((lane[:, None] == (lane[None, :] ^ 1)).astype(np.float32)), (4, LANES, LANES))
    log_gamma = jnp.log1p(-jnp.exp2(-5.0 - jnp.arange(4, dtype=F32)))
    idx = jnp.arange(BLK, dtype=F32)
    diff = idx[:, None] - idx[None, :]
    intra = jnp.where(diff >= 0, jnp.exp(jnp.maximum(diff, 0.0) * log_gamma[:, None, None]), 0.0)
    zeta = jnp.exp((BLK - 1.0 - idx)[None, :] * log_gamma[:, None])
    xi = jnp.exp((idx + 1.0)[None, :] * log_gamma[:, None])
    bc = lambda t: jnp.broadcast_to(t[:, :, None], (4, BLK, LANES))
    return cos, sin, perm, intra, bc(zeta), bc(xi)


def _heads(x):
    return jnp.stack([x[:, h * LANES:(h + 1) * LANES] for h in range(4)])


def _unheads(y):
    return jnp.concatenate([y[h] for h in range(4)], axis=1)


def _ret_chunk(rq, rk, rv, rg, S, cos, sin, intra, zeta, xi, perm):
    q = rq * cos + hdot(rq, perm) * sin
    k = (rk * cos + hdot(rk, perm) * sin) * (128.0 ** -0.5)
    ret = bdot(bdot(q, _t(k)) * intra, rv) + bdot(q * xi, S)
    S_new = S * xi[..., BLK - 1:BLK, :] + bdot(_t(k * zeta), rv)
    c = ret - jnp.mean(ret, axis=-1, keepdims=True)
    out = c * lax.rsqrt(jnp.mean(c * c, axis=-1, keepdims=True) + EPS) * _silu(rg)
    return out, S_new


def _wide(off):
    return pl.BlockSpec((BLK, 4 * LANES), lambda n: (n, off))


def retention_fwd(p, tables, *, name):
    T = p.shape[0]
    N = T // BLK
    cos, sin, perm, intra, zeta, xi = tables

    def body(rq, rk, rv, rg, cos_ref, sin_ref, in_ref, ze_ref, xi_ref, perm_ref, out_ref, sall_ref, s_scr):
        @pl.when(pl.program_id(0) == 0)
        def _():
            s_scr[...] = jnp.zeros_like(s_scr)

        S = s_scr[...]
        sall_ref[...] = S
        out, S_new = _ret_chunk(_heads(rq[...]), _heads(rk[...]), _heads(rv[...]), _heads(rg[...]), S, cos_ref[...],
                                sin_ref[...], in_ref[...], ze_ref[...], xi_ref[...], perm_ref[...])
        out_ref[...] = _unheads(out).astype(BF16)
        s_scr[...] = S_new

    rowtab = pl.BlockSpec((BLK, LANES), lambda n: (n, 0))
    tab = _const_spec((4, BLK, LANES))
    return pl.pallas_call(
        body, name=name, grid=(N,),
        in_specs=[_wide(0), _wide(1), _wide(2), _wide(3), rowtab, rowtab, tab, tab, tab, tab],
        out_specs=[_wide(0), pl.BlockSpec((None, 4, LANES, LANES), lambda n: (n, 0, 0, 0))],
        out_shape=[jax.ShapeDtypeStruct((T, 512), BF16), jax.ShapeDtypeStruct((N, 4, LANES, LANES), F32)],
        scratch_shapes=[pltpu.VMEM((4, LANES, LANES), F32)],
        compiler_params=_params(("arbitrary",)),
    )(p, p, p, p, cos, sin, intra, zeta, xi, perm)


def _row_mask(n):
    return (n * BLK + _iota2((BLK, 1), 0) >= PAD).astype(F32)


def retention_bwd(p, sall, dmixed, tables, *, name):
    T = p.shape[0]
    N = T // BLK
    cos, sin, perm, intra, zeta, xi = tables

    def body(rq, rk, rv, rg, cos_ref, sin_ref, in_ref, ze_ref, xi_ref, perm_ref, sall_ref, do_ref, drq, drk, drv, drg,
             ds_scr):
        n = N - 1 - pl.program_id(0)

        @pl.when(pl.program_id(0) == 0)
        def _():
            ds_scr[...] = jnp.zeros_like(ds_scr)

        f = lambda a, b, c, d, s: _ret_chunk(a, b, c, d, s, cos_ref[...], sin_ref[...], in_ref[...], ze_ref[...],
                                             xi_ref[...], perm_ref[...])
        _, vjp = jax.vjp(f, _heads(rq[...]), _heads(rk[...]), _heads(rv[...]), _heads(rg[...]), sall_ref[...])
        g = vjp((_heads(do_ref[...]), ds_scr[...]))
        mask = _row_mask(n)
        for ref, val in zip((drq, drk, drv, drg), g[:4]):
            ref[...] = _unheads(val) * mask
        ds_scr[...] = g[4]

    def rwide(off):
        return pl.BlockSpec((BLK, 4 * LANES), lambda n: (N - 1 - n, off))

    rowtab = pl.BlockSpec((BLK, LANES), lambda n: (N - 1 - n, 0))
    tab = _const_spec((4, BLK, LANES))
    return pl.pallas_call(
        body, name=name, grid=(N,),
        in_specs=[rwide(0), rwide(1), rwide(2), rwide(3), rowtab, rowtab, tab, tab, tab, tab,
                  pl.BlockSpec((None, 4, LANES, LANES), lambda n: (N - 1 - n, 0, 0, 0)), rwide(0)],
        out_specs=[rwide(0)] * 4, out_shape=[jax.ShapeDtypeStruct((T, 512), F32)] * 4,
        scratch_shapes=[pltpu.VMEM((4, LANES, LANES), F32)],
        compiler_params=_params(("arbitrary",)),
    )(p, p, p, p, cos, sin, intra, zeta, xi, perm, sall, dmixed)


def conv_silu_fwd(p, w, *, name):
    T = p.shape[0]
    N = T // BLK

    def body(x_ref, xp_ref, w_ref, o_ref):
        n = pl.program_id(0)
        cur = x_ref[...]
        cat = jnp.concatenate([jnp.where(n > 0, xp_ref[...], 0.0), cur], axis=0)
        y = w_ref[3:4, :] * cur
        for s in (1, 2, 3):
            y = y + w_ref[3 - s:4 - s, :] * pltpu.roll(cat, s, 0)[BLK:]
        o_ref[...] = _silu(y)

    cw = 4 * LANES
    return pl.pallas_call(
        body, name=name, grid=(N, 3),
        in_specs=[pl.BlockSpec((BLK, cw), lambda n, c: (n, 4 + c)),
                  pl.BlockSpec((BLK, cw), lambda n, c: (jnp.maximum(n - 1, 0), 4 + c)),
                  pl.BlockSpec((4, cw), lambda n, c: (0, c))],
        out_specs=pl.BlockSpec((BLK, cw), lambda n, c: (n, c)),
        out_shape=jax.ShapeDtypeStruct((T, 1536), F32), compiler_params=_params(("arbitrary", "arbitrary")),
    )(p, p, w)


def conv_silu_bwd(p, w, dact, part, *, name):
    T = p.shape[0]
    N = T // BLK
    cw = 4 * LANES

    def body(xp_ref, x_ref, xn_ref, w_ref, da_ref, dan_ref, dx_ref, dw_ref):
        n = pl.program_id(0)
        last = n == N - 1
        cat = jnp.concatenate([jnp.where(n > 0, xp_ref[...], 0.0), x_ref[...], jnp.where(last, 0.0, xn_ref[...])], axis=0)
        shifted = [cat] + [pltpu.roll(cat, s, 0) for s in (1, 2, 3)]
        y = w_ref[3:4, :] * shifted[0]
        for s in (1, 2, 3):
            y = y + w_ref[3 - s:4 - s, :] * shifted[s]
        y = y[BLK:]
        da = jnp.concatenate([da_ref[...], jnp.where(last, 0.0, dan_ref[...])], axis=0)
        sg = _sigmoid(y)
        dy = da * sg * (1.0 + y * (1.0 - sg))
        dx = w_ref[3:4, :] * dy[:BLK]
        for s in (1, 2, 3):
            dx = dx + w_ref[3 - s:4 - s, :] * pltpu.roll(dy, 2 * BLK - s, 0)[:BLK]
        dx_ref[...] = dx * _row_mask(n)

        @pl.when(n == 0)
        def _():
            dw_ref[...] = jnp.zeros_like(dw_ref)

        for s in (0, 1, 2, 3):
            dw_ref[3 - s:4 - s, :] += jnp.sum(dy[:BLK] * shifted[s][BLK:2 * BLK], axis=0, keepdims=True)

    def xs(d):
        return pl.BlockSpec((BLK, cw), lambda n: (jnp.clip(n + d, 0, N - 1), 4 + part))

    return pl.pallas_call(
        body, name=name, grid=(N,),
        in_specs=[xs(-1), xs(0), xs(1), pl.BlockSpec((4, cw), lambda n: (0, part)),
                  pl.BlockSpec((BLK, cw), lambda n: (n, 0)),
                  pl.BlockSpec((BLK, cw), lambda n: (jnp.minimum(n + 1, N - 1), 0))],
        out_specs=[pl.BlockSpec((BLK, cw), lambda n: (n, 0)), pl.BlockSpec((4, cw), lambda n: (0, 0))],
        out_shape=[jax.ShapeDtypeStruct((T, 512), F32), jax.ShapeDtypeStruct((4, 512), F32)],
        compiler_params=_params(("arbitrary",)),
    )(p, p, p, w, dact, dact)


def _softplus(x):
    return jnp.maximum(x, 0.0) + jnp.log1p(jnp.exp(-jnp.abs(x)))


def _pick4(tile, off):
    return jnp.stack([_lane_pick(tile, off + h) for h in range(4)])


def _spread4(v4, off, rows):
    lane = _iota2((rows, LANES), 1)
    out = jnp.where(lane == off, v4[0], 0.0)
    for h in range(1, 4):
        out = out + jnp.where(lane == off + h, v4[h], 0.0)
    return out


def _gdn_chunk(qa, ka, va, z, braw, araw, S, alog, dtb, onorm, rowmask, lincl):
    r, c = _iota2((BLK, BLK), 0), _iota2((BLK, BLK), 1)
    incl, strict = r >= c, r > c
    eye = (r == c).astype(F32)
    q = qa * lax.rsqrt(jnp.sum(qa * qa, axis=-1, keepdims=True) + EPS) * (128.0 ** -0.5)
    k = ka * lax.rsqrt(jnp.sum(ka * ka, axis=-1, keepdims=True) + EPS)
    beta = _sigmoid(braw) * rowmask
    g = -jnp.exp(alog) * _softplus(araw + dtb) * rowmask
    gc = hdot(lincl, jnp.broadcast_to(g, qa.shape))
    decay = jnp.where(incl, jnp.exp(jnp.where(incl, gc - _t(gc), 0.0)), 0.0)
    kb = k * beta
    amat = jnp.where(strict, bdot(kb, _t(k)) * decay, 0.0)
    m = -amat
    inv = eye + m
    pw = hdot(m, m)
    for t in range(6):
        inv = inv + hdot(inv, pw)
        if t < 5:
            pw = hdot(pw, pw)
    egc = jnp.exp(gc)
    u = hdot(inv, va * beta)
    w = hdot(inv, kb * egc)
    qk = jnp.where(incl, bdot(q, _t(k)) * decay, 0.0)
    glast = gc[..., BLK - 1:BLK, :]
    vnew = u - bdot(w, S)
    o = bdot(q * egc, S) + bdot(qk, vnew)
    S_new = S * jnp.exp(glast) + bdot(_t(k * jnp.exp(glast - gc)), vnew)
    out = o * lax.rsqrt(jnp.mean(o * o, axis=-1, keepdims=True) + EPS) * onorm * _silu(z)
    return out, S_new


def _lincl():
    i = np.arange(BLK)
    return jnp.broadcast_to(jnp.asarray((i[:, None] >= i[None, :]).astype(np.float32)), (4, BLK, BLK))


def gdn_fwd(act, p, alog, dtb, onorm, send, *, name):
    T = p.shape[0]
    N = T // BLK
    ns = len(send)

    def body(qa, ka, va, z, ba, alog_ref, dtb_ref, on_ref, l_ref, *refs):
        x_refs, (out_ref, sall_ref), g_refs = refs[:ns], refs[ns:ns + 2], refs[ns + 2:2 * ns + 2]
        s_scr, sems = refs[2 * ns + 2], refs[2 * ns + 3:]
        n = pl.program_id(0)

        @pl.when(n == 0)
        def _():
            s_scr[...] = jnp.zeros_like(s_scr)
            _gather_direct(True, x_refs, g_refs, *sems)

        @pl.when(n == N - 1)
        def _():
            _gather_direct(False, x_refs, g_refs, *sems)

        S = s_scr[...]
        sall_ref[...] = S
        out, S_new = _gdn_chunk(_heads(qa[...]), _heads(ka[...]), _heads(va[...]), _heads(z[...]), _pick4(ba[...], 0),
                                _pick4(ba[...], 4), S, _pick4(alog_ref[...], 0), _pick4(dtb_ref[...], 0), on_ref[...],
                                _row_mask(n), l_ref[...])
        out_ref[...] = _unheads(out).astype(BF16)
        s_scr[...] = S_new

    vec = _const_spec((1, LANES))
    res = pl.pallas_call(
        body, name=name, grid=(N,),
        in_specs=[_wide(0), _wide(1), _wide(2), _wide(7), pl.BlockSpec((BLK, LANES), lambda n: (n, 32)), vec, vec, vec,
                  _const_spec((4, BLK, BLK))] + [HBM_SPEC] * ns,
        out_specs=[_wide(0), pl.BlockSpec((None, 4, LANES, LANES), lambda n: (n, 0, 0, 0))] + [HBM_SPEC] * ns,
        out_shape=[jax.ShapeDtypeStruct((T, 512), BF16), jax.ShapeDtypeStruct((N, 4, LANES, LANES), F32)]
        + [jax.ShapeDtypeStruct((NDEV,) + x.shape, x.dtype) for x in send],
        scratch_shapes=[pltpu.VMEM((4, LANES, LANES), F32)] + _gather_direct_sems(ns),
        compiler_params=_params(("arbitrary",)),
    )(act, act, act, p, p, alog, dtb, onorm, _lincl(), *send)
    return res[0], res[1], res[2:]


def gdn_bwd(act, p, alog, dtb, onorm, sall, dmixed, partials, *, name):
    T = p.shape[0]
    N = T // BLK
    ns = len(partials)

    def body(qa, ka, va, z, ba, alog_ref, dtb_ref, on_ref, l_ref, sall_ref, do_ref, *refs):
        p_refs, (dq_ref, dk_ref, dv_ref, dz_ref, dba_ref, dal_ref, ddt_ref, don_ref) = refs[:ns], refs[ns:ns + 8]
        got_refs, ds_scr, sems = refs[ns + 8:2 * ns + 8], refs[2 * ns + 8], refs[2 * ns + 9:]
        step = pl.program_id(0)
        n = N - 1 - step

        @pl.when(step == 0)
        def _():
            ds_scr[...] = jnp.zeros_like(ds_scr)
            dal_ref[...] = jnp.zeros_like(dal_ref)
            ddt_ref[...] = jnp.zeros_like(ddt_ref)
            don_ref[...] = jnp.zeros_like(don_ref)
            for cp in _chip_exchange_copies(p_refs, got_refs, *sems):
                cp.start()

        @pl.when(step == N - 1)
        def _():
            for cp in _chip_exchange_copies(p_refs, got_refs, *sems):
                cp.wait()

        rowmask, lincl = _row_mask(n), l_ref[...]
        f = lambda *a: _gdn_chunk(*a, rowmask, lincl)
        _, vjp = jax.vjp(f, _heads(qa[...]), _heads(ka[...]), _heads(va[...]), _heads(z[...]), _pick4(ba[...], 0),
                         _pick4(ba[...], 4), sall_ref[...], _pick4(alog_ref[...], 0), _pick4(dtb_ref[...], 0),
                         on_ref[...])
        g = vjp((_heads(do_ref[...]), ds_scr[...]))
        dq_ref[...] = _unheads(g[0]) * rowmask
        dk_ref[...] = _unheads(g[1]) * rowmask
        dv_ref[...] = _unheads(g[2]) * rowmask
        dz_ref[...] = _unheads(g[3]) * rowmask
        dba_ref[...] = (_spread4(g[4], 0, BLK) + _spread4(g[5], 4, BLK)) * rowmask
        ds_scr[...] = g[6]
        dal_ref[...] += _spread4(g[7], 0, 1)
        ddt_ref[...] += _spread4(g[8], 0, 1)
        don_ref[...] += g[9]

    def rwide(off):
        return pl.BlockSpec((BLK, 4 * LANES), lambda s: (N - 1 - s, off))

    vec = _const_spec((1, LANES))
    col = pl.BlockSpec((BLK, LANES), lambda s: (N - 1 - s, 0))
    res = pl.pallas_call(
        body, name=name, grid=(N,),
        in_specs=[rwide(0), rwide(1), rwide(2), rwide(7), pl.BlockSpec((BLK, LANES), lambda s: (N - 1 - s, 32)), vec, vec,
                  vec, _const_spec((4, BLK, BLK)),
                  pl.BlockSpec((None, 4, LANES, LANES), lambda s: (N - 1 - s, 0, 0, 0)), rwide(1)] + [HBM_SPEC] * ns,
        out_specs=[rwide(0)] * 4 + [col, vec, vec, vec] + [HBM_SPEC] * ns,
        out_shape=[jax.ShapeDtypeStruct((T, 512), F32)] * 4 + [jax.ShapeDtypeStruct((T, LANES), F32)]
        + [jax.ShapeDtypeStruct((1, LANES), F32)] * 3 + _chip_exchange_shapes(partials),
        scratch_shapes=[pltpu.VMEM((4, LANES, LANES), F32)] + _chip_exchange_sems(ns),
        compiler_params=_params(("arbitrary",)),
    )(act, act, act, p, p, alog, dtb, onorm, _lincl(), sall, dmixed, *partials)
    return res[:8], res[8:]


NEG = -1e30


def _swa_block(q, k0, kp, kc, v0, vp, vc, sink, n):
    r, c = _iota2((BLK, BLK), 0), _iota2((BLK, BLK), 1)
    m0 = (c >= PAD) & (c <= n * BLK + r)
    mp = (n >= 2) & (c > r)
    mc = (n >= 1) & (r >= c)
    b = lambda t: jnp.broadcast_to(t, (4,) + t.shape)
    qs = q * (64.0 ** -0.5)
    s0 = jnp.where(m0, bdot(qs, _t(b(k0))), NEG)
    sp = jnp.where(mp, bdot(qs, _t(b(kp))), NEG)
    sc = jnp.where(mc, bdot(qs, _t(b(kc))), NEG)
    mx = jnp.maximum(jnp.max(jnp.maximum(jnp.maximum(s0, sp), sc), axis=-1, keepdims=True), sink)
    mx = lax.stop_gradient(mx)
    p0, pp, pc = jnp.exp(s0 - mx), jnp.exp(sp - mx), jnp.exp(sc - mx)
    den = (jnp.sum(p0, axis=-1, keepdims=True) + jnp.sum(pp, axis=-1, keepdims=True)
           + jnp.sum(pc, axis=-1, keepdims=True) + jnp.exp(sink - mx))
    return (bdot(p0, b(v0)) + bdot(pp, b(vp)) + bdot(pc, b(vc))) / den


def _swa_specs():
    rows = (lambda n: 0, lambda n: jnp.maximum(n - 1, 0), lambda n: n)

    def kv_spec(off, row):
        return pl.BlockSpec((BLK, LANES), lambda g, n: (row(n), off + g))

    q = pl.BlockSpec((BLK, 4 * LANES), lambda g, n: (n, g))
    return q, [kv_spec(off, row) for off in (8, 10) for row in rows]


def swa_fwd(p2, sinkrow, *, name):
    T = p2.shape[0]
    N = T // BLK

    def body(q, k0, kp, kc, v0, vp, vc, sink_ref, o_ref):
        g, n = pl.program_id(0), pl.program_id(1)
        f32 = lambda ref: ref[...].astype(F32)
        o = _swa_block(_heads(f32(q)), f32(k0), f32(kp), f32(kc), f32(v0), f32(vp), f32(vc),
                       _pick4(sink_ref[...], 4 * g), n)
        o_ref[...] = _unheads(o).astype(BF16)

    q, kv = _swa_specs()
    return pl.pallas_call(
        body, name=name, grid=(2, N), in_specs=[q] + kv + [_const_spec((1, LANES))],
        out_specs=q, out_shape=jax.ShapeDtypeStruct((T, 1024), BF16),
        compiler_params=_params(("arbitrary", "arbitrary")),
    )(p2, p2, p2, p2, p2, p2, p2, sinkrow)


def swa_bwd(p2, sinkrow, dmixed, *, name):
    T = p2.shape[0]
    N = T // BLK

    def body(q, k0, kp, kc, v0, vp, vc, sink_ref, do_ref, dq_ref, dk_ref, dv_ref, dsink_ref):
        g, n = pl.program_id(0), pl.program_id(1)

        @pl.when(n == 0)
        def _():
            dk_ref[...] = jnp.zeros_like(dk_ref)
            dv_ref[...] = jnp.zeros_like(dv_ref)

        @pl.when((g == 0) & (n == 0))
        def _():
            dsink_ref[...] = jnp.zeros_like(dsink_ref)

        f = lambda *a: _swa_block(*a, n)
        f32 = lambda ref: ref[...].astype(F32)
        _, vjp = jax.vjp(f, _heads(f32(q)), f32(k0), f32(kp), f32(kc), f32(v0), f32(vp), f32(vc),
                         _pick4(sink_ref[...], 4 * g))
        dq, dk0, dkp, dkc, dv0, dvp, dvc, dsink = vjp(_heads(do_ref[...]))
        dq_ref[...] = _unheads(dq)
        prev = pl.ds(pl.multiple_of(jnp.maximum(n - 1, 0) * BLK, BLK), BLK)
        cur = pl.ds(pl.multiple_of(n * BLK, BLK), BLK)
        for ref, d0, dp, dc in ((dk_ref, dk0, dkp, dkc), (dv_ref, dv0, dvp, dvc)):
            ref[0:BLK, :] += d0
            ref[prev, :] += dp
            ref[cur, :] += dc
        dsink_ref[...] += _spread4(dsink, 4 * g, 1)

    qspec, kv = _swa_specs()
    slab = pl.BlockSpec((T, LANES), lambda g, n: (0, g))
    return pl.pallas_call(
        body, name=name, grid=(2, N), in_specs=[qspec] + kv + [_const_spec((1, LANES)), qspec],
        out_specs=[qspec, slab, slab, _const_spec((1, LANES))],
        out_shape=[jax.ShapeDtypeStruct((T, 1024), F32), jax.ShapeDtypeStruct((T, 256), F32),
                   jax.ShapeDtypeStruct((T, 256), F32), jax.ShapeDtypeStruct((1, LANES), F32)],
        compiler_params=_params(("arbitrary", "arbitrary")),
    )(p2, p2, p2, p2, p2, p2, p2, sinkrow, dmixed)


def _split_dot(x, m):
    rows = x.shape[0]
    hi = x.astype(BF16)
    lo = (x - hi.astype(F32)).astype(BF16)
    r = _nn(jnp.concatenate([hi, lo], axis=0), m)
    return r[:rows] + r[rows:]


def _tri_and_ones(strict, ones=True):
    i = np.arange(BLK)
    m = (i[:, None] > i[None, :]) if strict else (i[:, None] >= i[None, :])
    if ones:
        m = np.concatenate([m, np.ones((BLK, BLK), bool)], axis=1)
    return jnp.asarray(m.astype(np.float32), dtype=BF16)


def _later_and_row_sums(x, m):
    r = _split_dot(x, m)
    if m.shape[1] == 2 * BLK:
        return r[:, :BLK], r[:, BLK:]
    return r, jnp.broadcast_to(jnp.sum(x, axis=1, keepdims=True), x.shape)


SB_PAIR = 2
SB_FWD_GROUP = 4


def _sb_positions():
    r, s = _iota2((BLK, BLK), 0), _iota2((BLK, BLK), 1)
    return s - r, s


def _sb_weights(qbs, ks, base, n, pos, carries, after):
    nh, kb = len(qbs), len(ks[0])
    zs = [[_nt(qbs[h], ks[h][c]) for c in range(kb)] for h in range(nh)]
    valid = [(pos[0] < (n - base - c) * BLK) & (pos[1] >= PAD - (base + c) * BLK) for c in range(kb)]
    lb = [[None] * kb for _ in range(nh)]
    sums = [[None] * kb for _ in range(nh)]
    for c in range(kb):
        for h in range(nh):
            z = zs[h][c]
            lb[h][c] = jnp.minimum(z, 0.0) - jnp.log(1.0 + jnp.exp(-jnp.abs(z)))
            sums[h][c] = _later_and_row_sums(jnp.where(valid[c], lb[h][c] - z, 0.0), after)
    a = [[None] * kb for _ in range(nh)]
    carries = list(carries)
    for c in reversed(range(kb)):
        for h in range(nh):
            a[h][c] = jnp.where(valid[c], jnp.exp(lb[h][c] + carries[h] + sums[h][c][0]), 0.0)
            carries[h] = carries[h] + sums[h][c][1]
    return valid, lb, a, carries


def _key_blocks(n_blocks):
    return next(k for k in (5, 3, 1) if n_blocks % k == 0)


def sb_fwd(p2, *, name):
    T = p2.shape[0]
    N = T // BLK
    kb = _key_blocks(N)
    nh = SB_FWD_GROUP
    heads = [slice(h * LANES, (h + 1) * LANES) for h in range(nh)]

    def body(q_ref, k_ref, v_ref, after_ref, o_ref, of_ref):
        n = pl.program_id(1)
        qbs = [(q_ref[:, hs].astype(F32) * (64.0 ** -0.5)).astype(BF16) for hs in heads]
        after, pos = after_ref[...], _sb_positions()
        nsup = n // kb + 1

        def step(t, c):
            accs, carries = c
            base = (nsup - 1 - t) * kb
            rows = [pl.ds(pl.multiple_of((base + sub) * BLK, BLK), BLK) for sub in range(kb)]
            ks = [[k_ref[r, hs] for r in rows] for hs in heads]
            _, _, a, carries = _sb_weights(qbs, ks, base, n, pos, carries, after)
            accs = list(accs)
            for sub, r in enumerate(rows):
                for h, hs in enumerate(heads):
                    accs[h] = accs[h] + _nn(a[h][sub].astype(BF16), v_ref[r, hs])
            return accs, carries

        zero = [jnp.zeros((BLK, LANES), F32)] * nh
        accs, _ = lax.fori_loop(0, nsup, step, (zero, zero))
        acc = jnp.concatenate(accs, axis=1)
        o_ref[...] = acc.astype(BF16)
        of_ref[...] = acc

    wide = nh * LANES

    def slab(off):
        return pl.BlockSpec((T, wide), lambda g, n: (0, off + g))

    def blk(off):
        return pl.BlockSpec((BLK, wide), lambda g, n: (n, off + g))

    return pl.pallas_call(
        body, name=name, grid=(8 // nh, N),
        in_specs=[blk(12 // nh), slab(20 // nh), slab(28 // nh), _const_spec((BLK, BLK))],
        out_specs=[blk(0), blk(0)],
        out_shape=[jax.ShapeDtypeStruct((T, 1024), BF16), jax.ShapeDtypeStruct((T, 1024), F32)],
        compiler_params=_params(("arbitrary", "arbitrary")),
    )(p2, p2, p2, _tri_and_ones(True, ones=False))


def sb_bwd(p2, o, dmixed, *, name):
    T = p2.shape[0]
    N = T // BLK
    kb = _key_blocks(N)

    heads = [slice(h * LANES, (h + 1) * LANES) for h in range(SB_PAIR)]
    scale = 64.0 ** -0.5

    def body(q_ref, k_ref, v_ref, after_ref, from_ref, o_ref, do_ref, dq_ref, dk_ref, dv_ref, dkt_scr, dvt_scr):
        n = pl.program_id(1)

        @pl.when(n == 0)
        def _():
            dkt_scr[...] = jnp.zeros_like(dkt_scr)
            dvt_scr[...] = jnp.zeros_like(dvt_scr)

        qbs, qts, dobs, dots, totals = [], [], [], [], []
        for hs in heads:
            qs = q_ref[:, hs].astype(F32) * scale
            do = do_ref[:, hs]
            qbs.append(qs.astype(BF16))
            qts.append(qs.T.astype(BF16))
            dobs.append(do.astype(BF16))
            dots.append(do.T.astype(BF16))
            total = jnp.sum(dobs[-1].astype(F32) * o_ref[:, hs], axis=1, keepdims=True)
            totals.append(jnp.broadcast_to(total, (BLK, LANES)))
        after, frm, pos = after_ref[...], from_ref[...], _sb_positions()
        nsup = n // kb + 1

        def step(t, c):
            dqs, carries, gcarries = c
            base = (nsup - 1 - t) * kb
            rows = [pl.ds(pl.multiple_of((base + sub) * BLK, BLK), BLK) for sub in range(kb)]
            ks = [[k_ref[r, hs] for r in rows] for hs in heads]
            valid, lb, a, carries = _sb_weights(qbs, ks, base, n, pos, carries, after)
            das = [[_nt(dobs[h], v_ref[r, hs]) for r in rows] for h, hs in enumerate(heads)]
            ab = [[a[h][sub].astype(BF16) for sub in range(kb)] for h in range(SB_PAIR)]
            g = [[None] * kb for _ in heads]
            sums = [[None] * kb for _ in heads]
            for sub in range(kb):
                for h in range(SB_PAIR):
                    g[h][sub] = das[h][sub] * ab[h][sub].astype(F32)
                    sums[h][sub] = _later_and_row_sums(g[h][sub], frm)
            dqs, gcarries = list(dqs), list(gcarries)
            for sub in reversed(range(kb)):
                for h in range(SB_PAIR):
                    before = totals[h] - (gcarries[h] + sums[h][sub][0])
                    gcarries[h] = gcarries[h] + sums[h][sub][1]
                    beta = jnp.exp(lb[h][sub])
                    dz = jnp.where(valid[sub], g[h][sub] - beta * (g[h][sub] + before), 0.0).astype(BF16)
                    dqs[h] = dqs[h] + _nn(dz, ks[h][sub])
                    dkt_scr[h * N + base + sub] += _nn(qts[h], dz)
                    dvt_scr[h * N + base + sub] += _nn(dots[h], ab[h][sub])
            return dqs, carries, gcarries

        zero = [jnp.zeros((BLK, LANES), F32)] * SB_PAIR
        dqs, _, _ = lax.fori_loop(0, nsup, step, (zero, zero, zero))
        dq_ref[...] = (jnp.concatenate(dqs, axis=1) * scale).astype(dq_ref.dtype)

        @pl.when(n == N - 1)
        def _():
            def flush(j, _):
                rows = pl.ds(pl.multiple_of(j * BLK, BLK), BLK)
                for h, hs in enumerate(heads):
                    dk_ref[rows, hs] = dkt_scr[h * N + j].T.astype(dk_ref.dtype)
                    dv_ref[rows, hs] = dvt_scr[h * N + j].T.astype(dv_ref.dtype)
                return 0

            lax.fori_loop(0, N, flush, 0)

    wide = SB_PAIR * LANES

    def slab(off):
        return pl.BlockSpec((T, wide), lambda g, n: (0, off + g))

    def blk(off):
        return pl.BlockSpec((BLK, wide), lambda g, n: (n, off + g))

    tri = _const_spec((BLK, BLK))
    return pl.pallas_call(
        body, name=name, grid=(8 // SB_PAIR, N),
        in_specs=[blk(12 // SB_PAIR), slab(20 // SB_PAIR), slab(28 // SB_PAIR), tri, tri, blk(0), blk(8 // SB_PAIR)],
        out_specs=[blk(0), slab(0), slab(0)],
        out_shape=[jax.ShapeDtypeStruct((T, 1024), BF16)] * 3,
        scratch_shapes=[pltpu.VMEM((SB_PAIR * N, LANES, LANES), F32), pltpu.VMEM((SB_PAIR * N, LANES, LANES), F32)],
        compiler_params=_params(("arbitrary", "arbitrary")),
    )(p2, p2, p2, _tri_and_ones(True, ones=False), _tri_and_ones(False, ones=False), o, dmixed)


def ffn_fwd(h, g_pre, g_post, wg, wu, wd, tag):
    u, gate, up, act = norm_mm(h, g_pre, (wg, wu), swiglu=True, wt=True, name=f"ffn_up_{tag}")
    y, h_new = mm_norm_res([act], [wd], h, g_post, 0.5, name=f"ffn_down_{tag}")
    return h_new, (h, u, gate, up, y)


def ffn_bwd(saved, dh, g_pre, g_post, wg, wu, wd, tag):
    h, u, gate, up, y = saved
    dy, dg_post, dgate, dup, act = normbwd_mm_nt(dh, y, g_post, wd, 0.5, (gate, up), name=f"ffn_bwd_down_{tag}")
    dwd = mm_tn(act, dy, name=f"ffn_dwd_{tag}")
    dwg = mm_tn(dgate, u, name=f"ffn_dwg_{tag}")
    dwu = mm_tn(dup, u, name=f"ffn_dwu_{tag}")
    dh_in, dg_pre = mm_nt_normbwd([dgate, dup], [wg, wu], h, g_pre, dh, wt=True, name=f"ffn_bwd_up_{tag}")
    return dh_in, (dg_pre, dg_post), (dwg, dwu, dwd)


def _lane_row(v):
    v = v.reshape(1, -1)
    return jnp.pad(v, ((0, 0), (0, LANES - v.shape[1])))


AB_WIDTHS = (512,) * 8 + (LANES,)


def mixer_ab_fwd(h, g_pre, g_post, w_in, conv_w, a_log, dt_bias, out_norm, w_out, tables, send):
    u, p = norm_mm(h, g_pre, (w_in,), swiglu=False, name="ab_in")
    ret, sall_r = retention_fwd(p, tables, name="retention_fwd")
    act = conv_silu_fwd(p, conv_w, name="conv_fwd")
    gdn, sall_g, gathered = gdn_fwd(act, p, _lane_row(a_log), _lane_row(dt_bias), out_norm.reshape(1, LANES), send,
                                    name="gdn_fwd")
    y, h_new = mm_norm_res([ret, gdn], [w_out[:512], w_out[512:]], h, g_post, 1.0, name="ab_out")
    return h_new, (h, u, p, ret, sall_r, act, gdn, sall_g, y), gathered


def mixer_ab_bwd(saved, dh, g_pre, g_post, w_in, conv_w, a_log, dt_bias, out_norm, w_out, tables, partials):
    h, u, p, ret, sall_r, act, gdn, sall_g, y = saved
    dy, dg_post, dmixed = normbwd_mm_nt(dh, y, g_post, w_out, 1.0, name="ab_bwd_out")
    dw_out = jnp.concatenate([mm_tn(ret, dy, name="ab_dwout_ret"), mm_tn(gdn, dy, name="ab_dwout_gdn")], axis=0)
    pieces = list(retention_bwd(p, sall_r, dmixed, tables, name="retention_bwd"))
    (dqa, dka, dva, dz, dba, dalog, ddtb, donorm), arrived = gdn_bwd(
        act, p, _lane_row(a_log), _lane_row(dt_bias), out_norm.reshape(1, LANES), sall_g, dmixed, partials,
        name="gdn_bwd")
    dconv = []
    for part, dact in enumerate((dqa, dka, dva)):
        dx, dw = conv_silu_bwd(p, conv_w, dact, part, name=f"conv_bwd_{part}")
        pieces.append(dx)
        dconv.append(dw)
    pieces += [dz, dba]
    offs = np.cumsum((0,) + AB_WIDTHS)
    w_parts = [w_in[:, a:b] for a, b in zip(offs[:-1], offs[1:])]
    dh_in, dg_pre = mm_nt_normbwd(pieces, w_parts, h, g_pre, dh, name="ab_bwd_in")
    dw_in = jnp.concatenate([mm_tn(u, pc, name=f"ab_dwin_{i}") for i, pc in enumerate(pieces)], axis=1)
    small = (jnp.concatenate(dconv, axis=1), dalog[:, :4], ddtb[:, :4], donorm)
    return dh_in, (dg_pre, dg_post), (dw_in, dw_out), small, arrived


CD_WIDTHS = (1024, 256, 256, 1024, 1024, 1024)


def mixer_cd_fwd(h, g_pre, g_post, w_in, sinks, w_out):
    u, p2 = norm_mm(h, g_pre, (w_in,), swiglu=False, out_dtype=BF16, name="cd_in")
    swa = swa_fwd(p2, _lane_row(sinks), name="swa_fwd")
    sb, sb_f32 = sb_fwd(p2, name="sb_fwd")
    y, h_new = mm_norm_res([swa, sb], [w_out[:1024], w_out[1024:]], h, g_post, 1.0, name="cd_out")
    return h_new, (h, u, p2, swa, sb, sb_f32, y)


def mixer_cd_bwd(saved, dh, g_pre, g_post, w_in, sinks, w_out):
    h, u, p2, swa, sb, sb_f32, y = saved
    dy, dg_post, dmixed = normbwd_mm_nt(dh, y, g_post, w_out, 1.0, name="cd_bwd_out")
    dw_out = jnp.concatenate([mm_tn(swa, dy, name="cd_dwout_swa"), mm_tn(sb, dy, name="cd_dwout_sb")], axis=0)
    dq_c, dk_c, dv_c, dsink = swa_bwd(p2, _lane_row(sinks), dmixed, name="swa_bwd")
    pieces = [dq_c, dk_c, dv_c] + list(sb_bwd(p2, sb_f32, dmixed, name="sb_bwd"))
    offs = np.cumsum((0,) + CD_WIDTHS)
    w_parts = [w_in[:, a:b] for a, b in zip(offs[:-1], offs[1:])]
    dh_in, dg_pre = mm_nt_normbwd(pieces, w_parts, h, g_pre, dh, name="cd_bwd_in")
    dw_in = jnp.concatenate([mm_tn(u, pc, name=f"cd_dwin_{i}") for i, pc in enumerate(pieces)], axis=1)
    return dh_in, (dg_pre, dg_post), (dw_in, dw_out), dsink[:, :8]


def _pad_heads(w, axis):
    shape = w.shape
    w = w.reshape(shape[:axis] + (shape[axis] // 64, 64) + shape[axis + 1:])
    pad = [(0, 0)] * w.ndim
    pad[axis + 1] = (0, 64)
    return jnp.pad(w, pad).reshape(shape[:axis] + (2 * shape[axis],) + shape[axis + 1:])


def _unpad_heads(w, axis):
    shape = w.shape
    w = w.reshape(shape[:axis] + (shape[axis] // 128, 128) + shape[axis + 1:])
    w = lax.slice_in_dim(w, 0, 64, axis=axis + 1)
    return w.reshape(shape[:axis] + (shape[axis] // 2,) + shape[axis + 1:])


SMALL_SHARDED = (("meta_tokens", (NMETA, LANES), 1), ("norm_gains", (2, 6, LANES), 2), ("ab_conv_w", (1, 4, 192), 2))
SMALL_REPL = (("ab_a_log", (1, 4)), ("ab_dt_bias", (1, 4)), ("ab_out_norm", (1, LANES)), ("cd_sinks", (1, 8)))


def _stack_shards(g, axis):
    full = jnp.moveaxis(g, 0, axis)
    shape = full.shape
    return full.reshape(shape[:axis] + (shape[axis] * shape[axis + 1],) + shape[axis + 2:])


def _split_shards(full, axis):
    shape = full.shape
    g = full.reshape(shape[:axis] + (NDEV, shape[axis] // NDEV) + shape[axis + 1:])
    return jnp.moveaxis(g, axis, 0)


def _pad_rows8(a):
    rows = []
    for x in a:
        flat = x.reshape(x.shape[0], -1)
        n = -(-flat.shape[1] // LANES) * LANES
        rows.append(jnp.pad(flat, ((0, 0), (0, n - flat.shape[1]))).reshape(x.shape[0], n // LANES, LANES))
    cat = jnp.concatenate(rows, axis=1)
    return jnp.pad(cat, ((0, 0), (0, -cat.shape[1] % 8), (0, 0)))


def _unpad_rows8(packed, shapes):
    out, at = [], 0
    for shape in shapes:
        size = int(np.prod(shape))
        nrow = -(-size // LANES)
        blk = packed[:, at:at + nrow].reshape(packed.shape[0], -1)[:, :size]
        out.append(blk.reshape((packed.shape[0],) + tuple(shape)))
        at += nrow
    return out


def kernel(x, meta_tokens, norm_gains, ffn_w_gate, ffn_w_up, ffn_w_down, ab_w_in, ab_conv_w, ab_a_log, ab_dt_bias, ab_out_norm, ab_w_out, cd_w_in, cd_sinks, cd_w_out, loss_target, m_meta_tokens, m_norm_gains, m_ffn_w_gate, m_ffn_w_up, m_ffn_w_down, m_ab_w_in, m_ab_conv_w, m_ab_a_log, m_ab_dt_bias, m_ab_out_norm, m_ab_w_out, m_cd_w_in, m_cd_sinks, m_cd_w_out, v_meta_tokens, v_norm_gains, v_ffn_w_gate, v_ffn_w_up, v_ffn_w_down, v_ab_w_in, v_ab_conv_w, v_ab_a_log, v_ab_dt_bias, v_ab_out_norm, v_ab_w_out, v_cd_w_in, v_cd_sinks, v_cd_w_out):
    w = dict(meta_tokens=meta_tokens, norm_gains=norm_gains, ffn_w_gate=ffn_w_gate, ffn_w_up=ffn_w_up,
             ffn_w_down=ffn_w_down, ab_w_in=ab_w_in, ab_conv_w=ab_conv_w, ab_a_log=ab_a_log, ab_dt_bias=ab_dt_bias,
             ab_out_norm=ab_out_norm, ab_w_out=ab_w_out, cd_w_in=cd_w_in, cd_sinks=cd_sinks, cd_w_out=cd_w_out)
    m = dict(meta_tokens=m_meta_tokens, norm_gains=m_norm_gains, ffn_w_gate=m_ffn_w_gate, ffn_w_up=m_ffn_w_up,
             ffn_w_down=m_ffn_w_down, ab_w_in=m_ab_w_in, ab_conv_w=m_ab_conv_w, ab_a_log=m_ab_a_log,
             ab_dt_bias=m_ab_dt_bias, ab_out_norm=m_ab_out_norm, ab_w_out=m_ab_w_out, cd_w_in=m_cd_w_in,
             cd_sinks=m_cd_sinks, cd_w_out=m_cd_w_out)
    v = dict(meta_tokens=v_meta_tokens, norm_gains=v_norm_gains, ffn_w_gate=v_ffn_w_gate, ffn_w_up=v_ffn_w_up,
             ffn_w_down=v_ffn_w_down, ab_w_in=v_ab_w_in, ab_conv_w=v_ab_conv_w, ab_a_log=v_ab_a_log,
             ab_dt_bias=v_ab_dt_bias, ab_out_norm=v_ab_out_norm, ab_w_out=v_ab_w_out, cd_w_in=v_cd_w_in,
             cd_sinks=v_cd_sinks, cd_w_out=v_cd_w_out)
    order = list(w)
    S = x.shape[1]
    T = S + BLK

    fs = DFF // NDEV

    def ffn_local(i):
        return jnp.concatenate([jnp.swapaxes(ffn_w_gate[i], 1, 2).reshape(2 * fs, D),
                                jnp.swapaxes(ffn_w_up[i], 1, 2).reshape(2 * fs, D), ffn_w_down[i].reshape(2 * fs, D)],
                               axis=0).astype(BF16)

    def ffn_mats(gathered, i):
        mat = lambda b: gathered[:, b * fs:(b + 1) * fs].reshape(DFF, D)
        return [{(i, j): mat(2 * kind + j) for j in range(2)} for kind in range(3)]

    ffn0_all, abin_all, about_all = all_gather_big(
        [ffn_local(0), ab_w_in[0].astype(BF16), ab_w_out[0].astype(BF16)], name="gather_layer0")
    wg, wu, wd = ffn_mats(ffn0_all, 0)
    ab_in = jnp.pad(_stack_shards(abin_all, 1), ((0, 0), (0, AB_INP - AB_IN)))
    ab_out = about_all.reshape(D, D)
    layer1_local = [ffn_local(1), cd_w_in[0].astype(BF16), cd_w_out[0].astype(BF16)]
    small_src = jnp.broadcast_to(_pad_rows8([w[n][None] for n, _, _ in SMALL_SHARDED]), (NDEV, 40, LANES))
    small_all = _unpad_rows8(all_to_all_small(small_src, name="gather_small"), [s for _, s, _ in SMALL_SHARDED])
    full = {n: _stack_shards(g, ax) for (n, _, ax), g in zip(SMALL_SHARDED, small_all)}
    conv_w = full["ab_conv_w"][0]
    gains = full["norm_gains"].reshape(2, 6, 1, D)
    tables = retention_tables(T)

    h = jnp.concatenate([jnp.zeros((PAD, D), F32), full["meta_tokens"], x[0]], axis=0)
    h, s00 = ffn_fwd(h, gains[0, 0], gains[0, 1], wg[0, 0], wu[0, 0], wd[0, 0], "00")
    h, sab, layer1_part = mixer_ab_fwd(h, gains[0, 2], gains[0, 3], ab_in, conv_w, ab_a_log, ab_dt_bias, ab_out_norm,
                                       ab_out, tables, layer1_local)
    ffn1_all, cdin_all, cdout_all = gather_forward_to_sibling(list(layer1_part), name="gather_layer1_finish")
    for full1, new in zip((wg, wu, wd), ffn_mats(ffn1_all, 1)):
        full1.update(new)
    cd_in = _pad_heads(_stack_shards(cdin_all, 1), 1)
    cd_out = _pad_heads(cdout_all.reshape(D, D), 0)
    h, s01 = ffn_fwd(h, gains[0, 4], gains[0, 5], wg[0, 1], wu[0, 1], wd[0, 1], "01")
    h, s10 = ffn_fwd(h, gains[1, 0], gains[1, 1], wg[1, 0], wu[1, 0], wd[1, 0], "10")
    h, scd = mixer_cd_fwd(h, gains[1, 2], gains[1, 3], cd_in, cd_sinks, cd_out)
    h, s11 = ffn_fwd(h, gains[1, 4], gains[1, 5], wg[1, 1], wu[1, 1], wd[1, 1], "11")
    loss_tile, dh = loss_and_grad(h, loss_target[0], name="loss")
    loss = lax.psum(loss_tile[0, 0], ("x", "y", "c"))

    dgain = [[None] * 6, [None] * 6]
    dffn = {}
    dh, (dgain[1][4], dgain[1][5]), dffn[1, 1] = ffn_bwd(s11, dh, gains[1, 4], gains[1, 5], wg[1, 1], wu[1, 1], wd[1, 1], "11")
    dh, (dgain[1][2], dgain[1][3]), (dcd_in, dcd_out), dsinks = mixer_cd_bwd(scd, dh, gains[1, 2], gains[1, 3], cd_in, cd_sinks, cd_out)
    dh, (dgain[1][0], dgain[1][1]), dffn[1, 0] = ffn_bwd(s10, dh, gains[1, 0], gains[1, 1], wg[1, 0], wu[1, 0], wd[1, 0], "10")
    ffn_send = lambda i: jnp.concatenate(
        [dffn[i, j][kind].astype(BF16).reshape(NDEV, fs, D) for kind in range(3) for j in range(2)], axis=1)
    layer1_send = [ffn_send(1), _split_shards(_unpad_heads(dcd_in, 1).astype(BF16), 1),
                   _unpad_heads(dcd_out, 0).astype(BF16).reshape(NDEV, D // NDEV, D)]
    layer1_sib = rs_exchange_sibling(layer1_send, name="rs_sibling_layer1")
    layer1_parts = [rs_chip_partials(g, t, name=f"rs_chip_partials_layer1_{i}")
                    for i, (g, t) in enumerate(zip(layer1_send, layer1_sib))]
    dh, (dgain[0][4], dgain[0][5]), dffn[0, 1] = ffn_bwd(s01, dh, gains[0, 4], gains[0, 5], wg[0, 1], wu[0, 1], wd[0, 1], "01")
    dh, (dgain[0][2], dgain[0][3]), (dab_in, dab_out), (dconv, dalog, ddtb, donorm), layer1_got = mixer_ab_bwd(
        sab, dh, gains[0, 2], gains[0, 3], ab_in, conv_w, ab_a_log, ab_dt_bias, ab_out_norm, ab_out, tables, layer1_parts)
    dh, (dgain[0][0], dgain[0][1]), dffn[0, 0] = ffn_bwd(s00, dh, gains[0, 0], gains[0, 1], wg[0, 0], wu[0, 0], wd[0, 0], "00")
    grad_x = dh[BLK:][None]

    gfull = dict(meta_tokens=dh[PAD:BLK], norm_gains=jnp.stack([jnp.concatenate(r, axis=0) for r in dgain]),
                 ab_conv_w=dconv[None])
    ffn1_g, cdin_g, cdout_g = [rs_final_sum(p, t, name=f"rs_final_sum_layer1_{i}")
                               for i, (p, t) in enumerate(zip(layer1_parts, layer1_got))]
    ffn0_g, abin_g, about_g = reduce_scatter_big(
        [ffn_send(0), _split_shards(dab_in[:, :AB_IN].astype(BF16), 1), dab_out.astype(BF16).reshape(NDEV, D // NDEV, D)])
    ffn_g = jnp.stack([ffn0_g.reshape(3, 2, fs, D), ffn1_g.reshape(3, 2, fs, D)], axis=1)
    grads = dict(ffn_w_gate=jnp.swapaxes(ffn_g[0], 2, 3), ffn_w_up=jnp.swapaxes(ffn_g[1], 2, 3), ffn_w_down=ffn_g[2],
                 ab_w_in=abin_g[None], ab_w_out=about_g[None], cd_w_in=cdin_g[None], cd_w_out=cdout_g[None])
    repl = [jnp.broadcast_to(t[None], (NDEV,) + t.shape) for t in (dalog, ddtb, donorm, dsinks)]
    ssend = _pad_rows8([_split_shards(gfull[n], ax) for n, _, ax in SMALL_SHARDED] + repl)
    ssum = sum_slots(all_to_all_small(ssend, name="exchange_small_grads"), name="sum_small_grads")[None]
    small = _unpad_rows8(ssum, [s for _, s, _ in SMALL_SHARDED] + [s for _, s in SMALL_REPL])
    grads.update({n: g[0] for n, g in zip([n for n, _, _ in SMALL_SHARDED] + [n for n, _ in SMALL_REPL], small)})

    delta, new_m, new_v = {}, {}, {}
    for n in order:
        shape = w[n].shape
        view = (-1, shape[-1])
        d_, m_, v_ = adamw(w[n].reshape(view), grads[n].reshape(view), m[n].reshape(view), v[n].reshape(view),
                           name=f"adamw_{n}")
        delta[n], new_m[n], new_v[n] = d_.reshape(shape), m_.reshape(shape), v_.reshape(shape)
    return (loss, grad_x, *[grads[n] for n in order], *[delta[n] for n in order], *[new_m[n] for n in order],
            *[new_v[n] for n in order])
```

```python
import functools
import math

import numpy as np
import jax
import jax.numpy as jnp
from jax import lax
from jax.experimental import pallas as pl
from jax.experimental.pallas import tpu as pltpu

F32, BF16 = jnp.float32, jnp.bfloat16
EPS = 1e-6
D = 1024
NMETA = 16
BLK = 128
PAD = BLK - NMETA
DFF = 2816
LANES = 128
NDEV = 8
AB_IN, AB_INP = 4104, 4224
ADAM_LR, ADAM_B1, ADAM_B2, ADAM_EPS, ADAM_WD, ADAM_STEP = 0.001, 0.9, 0.999, 1e-08, 0.01, 10
VMEM_LIMIT = 56 * 1024 * 1024
MESH = pl.DeviceIdType.MESH
HIGH = lax.Precision.HIGH


def _params(sem):
    return pltpu.CompilerParams(dimension_semantics=sem, vmem_limit_bytes=VMEM_LIMIT)


def _row_tile(T, streamed, resident):
    for tm in (640, 320, 128):
        if T % tm == 0 and 2 * (tm * streamed + resident) <= VMEM_LIMIT - 14 * 1024 * 1024:
            return tm
    return _tile(T, 128)


MXU_COLS = 256


def _col_chunks(n):
    return [slice(c, min(c + MXU_COLS, n)) for c in range(0, n, MXU_COLS)]


def _tile(n, cap, unit=LANES):
    if n <= cap:
        return n
    best = None
    for t in range(unit, cap + 1, unit):
        if n % t == 0:
            best = t
    assert best is not None, (n, cap)
    return best


def _rms_fwd(x, g):
    return x * lax.rsqrt(jnp.mean(x * x, axis=-1, keepdims=True) + EPS) * g


def _rms_bwd(x, g, dz):
    r = lax.rsqrt(jnp.mean(x * x, axis=-1, keepdims=True) + EPS)
    xh = x * r
    dg = jnp.sum(dz * xh, axis=0, keepdims=True)
    t = dz * g
    return r * (t - xh * jnp.mean(t * xh, axis=-1, keepdims=True)), dg


def _sigmoid(x):
    return 0.5 * jnp.tanh(0.5 * x) + 0.5


def _silu(x):
    return x * _sigmoid(x)


def _nn(a, b, precision=None):
    return lax.dot_general(a, b, (((1,), (0,)), ((), ())), preferred_element_type=F32, precision=precision)


def _nt(a, b):
    return lax.dot_general(a, b, (((1,), (1,)), ((), ())), preferred_element_type=F32)


def _tn(a, b):
    return lax.dot_general(a, b, (((0,), (0,)), ((), ())), preferred_element_type=F32)


def _mm(a, b, precision=None):
    if a.ndim == 3:
        return lax.dot_general(a, b, (((2,), (1,)), ((0,), (0,))), preferred_element_type=F32, precision=precision)
    return _nn(a, b, precision)


def _t(x):
    return jnp.swapaxes(x, -1, -2)


@jax.custom_vjp
def bdot(a, b):
    return _mm(a.astype(BF16), b.astype(BF16))


def _bdot_fwd(a, b):
    return bdot(a, b), (a, b)


def _bdot_bwd(res, g):
    a, b = res
    return bdot(g, _t(b)), bdot(_t(a), g)


bdot.defvjp(_bdot_fwd, _bdot_bwd)


@jax.custom_vjp
def hdot(a, b):
    return _mm(a, b, HIGH)


def _hdot_fwd(a, b):
    return hdot(a, b), (a, b)


def _hdot_bwd(res, g):
    a, b = res
    return hdot(g, _t(b)), hdot(_t(a), g)


hdot.defvjp(_hdot_fwd, _hdot_bwd)


def _iota2(shape, axis):
    return lax.broadcasted_iota(jnp.int32, shape, axis)


def _lane_pick(row, lane):
    return jnp.sum(jnp.where(_iota2(row.shape, 1) == lane, row, 0.0), axis=1, keepdims=True)


def norm_mm(h, gain, ws, *, swiglu, name, wt=False, out_dtype=F32):
    T, Dm = h.shape
    N = ws[0].shape[0 if wt else 1]
    tm, tn = _tile(T, 640), _tile(N, 1408)
    nw = len(ws)
    mm = _nt if wt else _nn

    def body(h_ref, g_ref, *refs):
        w_refs, u_ref, o_refs = refs[:nw], refs[nw], refs[nw + 1:]

        @pl.when(pl.program_id(1) == 0)
        def _():
            u_ref[...] = _rms_fwd(h_ref[...], g_ref[...]).astype(BF16)

        u = u_ref[...]
        for cols in _col_chunks(tn):
            acc = [mm(u, w[cols, :] if wt else w[:, cols]) for w in w_refs]
            if swiglu:
                o_refs[0][:, cols] = acc[0].astype(BF16)
                o_refs[1][:, cols] = acc[1].astype(BF16)
                o_refs[2][:, cols] = (_silu(acc[0]) * acc[1]).astype(BF16)
            else:
                o_refs[0][:, cols] = acc[0].astype(out_dtype)

    row = pl.BlockSpec((tm, Dm), lambda i, j: (i, 0))
    tile = pl.BlockSpec((tm, tn), lambda i, j: (i, j))
    if swiglu:
        out_shape = [jax.ShapeDtypeStruct((T, Dm), BF16)] + [jax.ShapeDtypeStruct((T, N), BF16)] * 3
        out_specs = [row, tile, tile, tile]
    else:
        out_shape = [jax.ShapeDtypeStruct((T, Dm), BF16), jax.ShapeDtypeStruct((T, N), out_dtype)]
        out_specs = [row, tile]
    return pl.pallas_call(
        body, name=name, grid=(T // tm, N // tn),
        in_specs=[row, pl.BlockSpec((1, Dm), lambda i, j: (0, 0))]
        + [pl.BlockSpec((tn, Dm), lambda i, j: (j, 0)) if wt else pl.BlockSpec((Dm, tn), lambda i, j: (0, j))] * nw,
        out_specs=out_specs, out_shape=out_shape,
        compiler_params=_params(("arbitrary", "arbitrary")),
    )(h, gain, *ws)


def mm_norm_res(As, Ws, h, gain, scale, *, name):
    T, Dm = h.shape
    n = len(As)
    tm = _row_tile(T, sum(a.shape[1] * a.dtype.itemsize for a in As) + 3 * Dm * 4,
                   sum(w.size * w.dtype.itemsize for w in Ws))

    def body(*refs):
        a_refs, w_refs = refs[:n], refs[n:2 * n]
        h_ref, g_ref, y_ref, hn_ref = refs[2 * n:]
        y = _nn(a_refs[0][...].astype(BF16), w_refs[0][...])
        for a, w in zip(a_refs[1:], w_refs[1:]):
            y = y + _nn(a[...].astype(BF16), w[...])
        y_ref[...] = y
        hn_ref[...] = h_ref[...] + scale * _rms_fwd(y, g_ref[...])

    row = pl.BlockSpec((tm, Dm), lambda i: (i, 0))
    return pl.pallas_call(
        body, name=name, grid=(T // tm,),
        in_specs=[pl.BlockSpec((tm, a.shape[1]), lambda i: (i, 0)) for a in As]
        + [pl.BlockSpec(w.shape, lambda i: (0, 0)) for w in Ws]
        + [row, pl.BlockSpec((1, Dm), lambda i: (0, 0))],
        out_specs=[row, row], out_shape=[jax.ShapeDtypeStruct((T, Dm), F32)] * 2,
        compiler_params=_params(("arbitrary",)),
    )(*As, *Ws, h, gain)


def normbwd_mm_nt(dh, y, gain, w, scale, gu=None, *, name):
    T, Dm = dh.shape
    N = w.shape[0]
    tm, tn = _tile(T, 640), _tile(N, 1408)
    swiglu = gu is not None

    def body(dh_ref, y_ref, g_ref, w_ref, *refs):
        if swiglu:
            gate_ref, up_ref, dy_ref, dg_ref, dgate_ref, dup_ref, a_ref = refs
        else:
            dy_ref, dg_ref, da_ref = refs
        i, j = pl.program_id(0), pl.program_id(1)

        @pl.when(j == 0)
        def _():
            dy, dg = _rms_bwd(y_ref[...], g_ref[...], scale * dh_ref[...])
            dy_ref[...] = dy.astype(BF16)

            @pl.when(i == 0)
            def _():
                dg_ref[...] = jnp.zeros_like(dg_ref)

            dg_ref[...] += dg

        dy = dy_ref[...]
        for cols in _col_chunks(tn):
            da = _nt(dy, w_ref[cols, :])
            if swiglu:
                gate, up = gate_ref[:, cols].astype(F32), up_ref[:, cols].astype(F32)
                s = _sigmoid(gate)
                dgate_ref[:, cols] = (da * up * s * (1.0 + gate * (1.0 - s))).astype(BF16)
                dup_ref[:, cols] = (da * gate * s).astype(BF16)
                a_ref[:, cols] = (gate * s * up).astype(BF16)
            else:
                da_ref[:, cols] = da

    row = pl.BlockSpec((tm, Dm), lambda i, j: (i, 0))
    vec = pl.BlockSpec((1, Dm), lambda i, j: (0, 0))
    tile = pl.BlockSpec((tm, tn), lambda i, j: (i, j))
    in_specs = [row, row, vec, pl.BlockSpec((tn, Dm), lambda i, j: (j, 0))]
    out_shape = [jax.ShapeDtypeStruct((T, Dm), BF16), jax.ShapeDtypeStruct((1, Dm), F32)]
    if swiglu:
        in_specs += [tile, tile]
        out_shape += [jax.ShapeDtypeStruct((T, N), BF16)] * 3
        out_specs = [row, vec, tile, tile, tile]
        args = (dh, y, gain, w, *gu)
    else:
        out_shape += [jax.ShapeDtypeStruct((T, N), F32)]
        out_specs = [row, vec, tile]
        args = (dh, y, gain, w)
    return pl.pallas_call(
        body, name=name, grid=(T // tm, N // tn), in_specs=in_specs, out_specs=out_specs,
        out_shape=out_shape, compiler_params=_params(("arbitrary", "arbitrary")),
    )(*args)


def mm_nt_normbwd(dPs, Ws, h, gain, dh_in, *, name, wt=False):
    T, Dm = h.shape
    n = len(dPs)
    tm = _row_tile(T, sum(p.shape[1] * p.dtype.itemsize for p in dPs) + 3 * Dm * 4,
                   sum(w.size * w.dtype.itemsize for w in Ws))
    mm = _nn if wt else _nt

    def body(*refs):
        p_refs, w_refs = refs[:n], refs[n:2 * n]
        h_ref, g_ref, dhin_ref, dh_ref, dg_ref = refs[2 * n:]
        du = mm(p_refs[0][...].astype(BF16), w_refs[0][...])
        for p, w in zip(p_refs[1:], w_refs[1:]):
            du = du + mm(p[...].astype(BF16), w[...])
        dx, dg = _rms_bwd(h_ref[...], g_ref[...], du)
        dh_ref[...] = dhin_ref[...] + dx

        @pl.when(pl.program_id(0) == 0)
        def _():
            dg_ref[...] = jnp.zeros_like(dg_ref)

        dg_ref[...] += dg

    row = pl.BlockSpec((tm, Dm), lambda i: (i, 0))
    vec = pl.BlockSpec((1, Dm), lambda i: (0, 0))
    return pl.pallas_call(
        body, name=name, grid=(T // tm,),
        in_specs=[pl.BlockSpec((tm, p.shape[1]), lambda i: (i, 0)) for p in dPs]
        + [pl.BlockSpec(w.shape, lambda i: (0, 0)) for w in Ws] + [row, vec, row],
        out_specs=[row, vec],
        out_shape=[jax.ShapeDtypeStruct((T, Dm), F32), jax.ShapeDtypeStruct((1, Dm), F32)],
        compiler_params=_params(("arbitrary",)),
    )(*dPs, *Ws, h, gain, dh_in)


def mm_tn(a, b, *, name):
    T, M = a.shape
    N = b.shape[1]
    tm, tn, tk = _tile(M, 1408), _tile(N, 1408), _tile(T, 640)

    def body(a_ref, b_ref, o_ref):
        @pl.when(pl.program_id(2) == 0)
        def _():
            o_ref[...] = jnp.zeros_like(o_ref)

        o_ref[...] += _tn(a_ref[...].astype(BF16), b_ref[...].astype(BF16))

    return pl.pallas_call(
        body, name=name, grid=(M // tm, N // tn, T // tk),
        in_specs=[pl.BlockSpec((tk, tm), lambda i, j, k: (k, i)), pl.BlockSpec((tk, tn), lambda i, j, k: (k, j))],
        out_specs=pl.BlockSpec((tm, tn), lambda i, j, k: (i, j)),
        out_shape=jax.ShapeDtypeStruct((M, N), F32),
        compiler_params=_params(("arbitrary", "arbitrary", "arbitrary")),
    )(a, b)


def loss_and_grad(h, target, *, name):
    T, Dm = h.shape

    def body(h_ref, t_ref, loss_ref, dh_ref):
        b = pl.program_id(0)

        @pl.when(b == 0)
        def _():
            loss_ref[...] = jnp.zeros_like(loss_ref)
            dh_ref[...] = jnp.zeros_like(dh_ref)

        @pl.when(b > 0)
        def _():
            e = h_ref[...] - t_ref[...]
            dh_ref[...] = e * (1.0 / Dm)
            loss_ref[...] += jnp.sum(e * e) * (0.5 / Dm)

    return pl.pallas_call(
        body, name=name, grid=(T // BLK,),
        in_specs=[pl.BlockSpec((BLK, Dm), lambda b: (b, 0)),
                  pl.BlockSpec((BLK, Dm), lambda b: (jnp.maximum(b - 1, 0), 0))],
        out_specs=[pl.BlockSpec((8, LANES), lambda b: (0, 0)), pl.BlockSpec((BLK, Dm), lambda b: (b, 0))],
        out_shape=[jax.ShapeDtypeStruct((8, LANES), F32), jax.ShapeDtypeStruct((T, Dm), F32)],
        compiler_params=_params(("arbitrary",)),
    )(h, target)


def adamw(w, g, m, v, *, name):
    R, C = w.shape
    tr = R
    for t in (512, 352, 256):
        if R > t and R % t == 0:
            tr = t
            break

    def body(w_ref, g_ref, m_ref, v_ref, d_ref, nm_ref, nv_ref):
        g_ = g_ref[...]
        m_ = ADAM_B1 * m_ref[...] + (1.0 - ADAM_B1) * g_
        v_ = ADAM_B2 * v_ref[...] + (1.0 - ADAM_B2) * (g_ * g_)
        m_hat = m_ / (1.0 - ADAM_B1 ** ADAM_STEP)
        v_hat = v_ / (1.0 - ADAM_B2 ** ADAM_STEP)
        d_ref[...] = -ADAM_LR * (m_hat / (jnp.sqrt(v_hat) + ADAM_EPS) + ADAM_WD * w_ref[...])
        nm_ref[...] = m_
        nv_ref[...] = v_

    spec = pl.BlockSpec((tr, C), lambda i: (i, 0))
    return pl.pallas_call(
        body, name=name, grid=(R // tr,), in_specs=[spec] * 4, out_specs=[spec] * 3,
        out_shape=[jax.ShapeDtypeStruct((R, C), F32)] * 3, compiler_params=_params(("arbitrary",)),
    )(w, g, m, v)


def _me():
    return lax.axis_index("x"), lax.axis_index("y"), lax.axis_index("c")


def _flip(pos, rel):
    return tuple(1 - p if r else p for p, r in zip(pos, rel))


def _slot(pos):
    return 4 * pos[0] + 2 * pos[1] + pos[2]


HBM_SPEC = pl.BlockSpec(memory_space=pltpu.HBM)
CHIP_RELS = ((1, 0), (0, 1), (1, 1))


def all_gather_big(xs, *, name):
    n = len(xs)

    def body(*refs):
        x_refs, out_refs = refs[:n], refs[n:2 * n]
        send_sems, recv_sems, local_sems = refs[2 * n:]
        me = _me()
        sibling = _flip(me, (0, 0, 1))
        chips = [_flip(me, rel + (0,)) for rel in CHIP_RELS]

        def copy(i, k, block, to, src=None):
            dst = out_refs[i].at[_slot(block)]
            return pltpu.make_async_remote_copy(
                src_ref=dst if src is None else src, dst_ref=dst, send_sem=send_sems.at[i, k],
                recv_sem=recv_sems.at[i, k], device_id=to, device_id_type=MESH)

        sent, local = [], []
        for i in range(n):
            mine = pltpu.make_async_copy(x_refs[i], out_refs[i].at[_slot(me)], local_sems.at[i])
            mine.start()
            local.append(mine)
            sent += [copy(i, 0, me, sibling, src=x_refs[i])]
            sent += [copy(i, 1 + j, me, chip, src=x_refs[i]) for j, chip in enumerate(chips)]
        for cp in sent:
            cp.start()
        for i in range(n):
            for j, chip in enumerate(chips):
                copy(i, 1 + j, chip, me).wait_recv()
                passed = copy(i, 4 + j, chip, sibling)
                passed.start()
                sent.append(passed)
        for i in range(n):
            copy(i, 0, sibling, me).wait_recv()
            for j, chip in enumerate(chips):
                copy(i, 4 + j, _flip(chip, (0, 0, 1)), me).wait_recv()
        for cp in sent:
            cp.wait_send()
        for mine in local:
            mine.wait()

    return pl.pallas_call(
        body, name=name, in_specs=[HBM_SPEC] * n, out_specs=[HBM_SPEC] * n,
        out_shape=[jax.ShapeDtypeStruct((NDEV,) + x.shape, x.dtype) for x in xs],
        scratch_shapes=[pltpu.SemaphoreType.DMA((n, 7)), pltpu.SemaphoreType.DMA((n, 7)), pltpu.SemaphoreType.DMA((n,))],
    )(*xs)


def all_to_all_small(src, *, name):
    _, r, C = src.shape

    def body(src_ref, out_ref, send_sems, recv_sems):
        me = _me()
        my = _slot(me)
        out_ref[my] = src_ref[my]
        copies = []
        for k in range(1, NDEV):
            peer = _flip(me, ((k >> 2) & 1, (k >> 1) & 1, k & 1))
            cp = pltpu.make_async_remote_copy(
                src_ref=src_ref.at[_slot(peer)], dst_ref=out_ref.at[my], send_sem=send_sems.at[k - 1],
                recv_sem=recv_sems.at[k - 1], device_id=peer, device_id_type=MESH)
            cp.start()
            copies.append((cp, peer))
        for k, (cp, peer) in enumerate(copies):
            pltpu.make_async_remote_copy(
                src_ref=src_ref.at[my], dst_ref=out_ref.at[_slot(peer)], send_sem=send_sems.at[k],
                recv_sem=recv_sems.at[k], device_id=peer, device_id_type=MESH).wait_recv()
        for cp, _ in copies:
            cp.wait_send()

    vm = pl.BlockSpec(memory_space=pltpu.VMEM)
    return pl.pallas_call(
        body, name=name, in_specs=[vm], out_specs=vm, out_shape=jax.ShapeDtypeStruct(src.shape, src.dtype),
        scratch_shapes=[pltpu.SemaphoreType.DMA((7,)), pltpu.SemaphoreType.DMA((7,))],
    )(src)


def sum_slots(a, *, name):
    n, r, C = a.shape

    def body(a_ref, o_ref):
        s = a_ref[0]
        for k in range(1, n):
            s = s + a_ref[k]
        o_ref[...] = s

    vm = pl.BlockSpec(memory_space=pltpu.VMEM)
    return pl.pallas_call(body, name=name, in_specs=[vm], out_specs=vm,
                          out_shape=jax.ShapeDtypeStruct((r, C), F32))(a)


def rs_exchange_sibling(gs, *, name):
    n = len(gs)

    def body(*refs):
        g_refs, out_refs, send_sems, recv_sems = refs[:n], refs[n:2 * n], refs[2 * n], refs[2 * n + 1]
        sibling = _flip(_me(), (0, 0, 1))
        copies = []
        for i in range(n):
            for chip in range(4):
                cp = pltpu.make_async_remote_copy(
                    src_ref=g_refs[i].at[2 * chip + sibling[2]], dst_ref=out_refs[i].at[chip],
                    send_sem=send_sems.at[i, chip], recv_sem=recv_sems.at[i, chip], device_id=sibling,
                    device_id_type=MESH)
                cp.start()
                copies.append(cp)
        for cp in copies:
            cp.wait()

    return pl.pallas_call(
        body, name=name, in_specs=[HBM_SPEC] * n, out_specs=[HBM_SPEC] * n,
        out_shape=[jax.ShapeDtypeStruct((4,) + g.shape[1:], g.dtype) for g in gs],
        scratch_shapes=[pltpu.SemaphoreType.DMA((n, 4)), pltpu.SemaphoreType.DMA((n, 4))],
    )(*gs)


def rs_chip_partials(g, got, *, name):
    _, R, C = g.shape
    tr = _tile(R, 768, unit=16)

    def body(c_ref, g_ref, got_ref, o_ref):
        o_ref[...] = (g_ref[...].astype(F32) + got_ref[...].astype(F32)).astype(o_ref.dtype)

    c = jnp.reshape(lax.axis_index("c"), (1,)).astype(jnp.int32)
    return pl.pallas_call(
        body, name=name,
        grid_spec=pltpu.PrefetchScalarGridSpec(
            num_scalar_prefetch=1, grid=(4, R // tr),
            in_specs=[pl.BlockSpec((None, tr, C), lambda k, i, c_ref: (2 * k + c_ref[0], i, 0)),
                      pl.BlockSpec((None, tr, C), lambda k, i, c_ref: (k, i, 0))],
            out_specs=pl.BlockSpec((None, tr, C), lambda k, i, c_ref: (k, i, 0))),
        out_shape=jax.ShapeDtypeStruct((4, R, C), g.dtype), compiler_params=_params(("arbitrary", "arbitrary")),
    )(c, g, got)


def rs_exchange_chips(ps, *, name):
    n = len(ps)

    def body(*refs):
        copies = _chip_exchange_copies(refs[:n], refs[n:2 * n], refs[2 * n], refs[2 * n + 1])
        for cp in copies:
            cp.start()
        for cp in copies:
            cp.wait()

    return pl.pallas_call(
        body, name=name, in_specs=[HBM_SPEC] * n, out_specs=[HBM_SPEC] * n,
        out_shape=_chip_exchange_shapes(ps), scratch_shapes=_chip_exchange_sems(n),
    )(*ps)


def _chip_exchange_copies(p_refs, out_refs, send_sems, recv_sems):
    me = _me()
    copies = []
    for i, (p_ref, out_ref) in enumerate(zip(p_refs, out_refs)):
        for j, rel in enumerate(CHIP_RELS):
            peer = _flip(me, rel + (0,))
            copies.append(pltpu.make_async_remote_copy(
                src_ref=p_ref.at[2 * peer[0] + peer[1]], dst_ref=out_ref.at[j], send_sem=send_sems.at[i, j],
                recv_sem=recv_sems.at[i, j], device_id=peer, device_id_type=MESH))
    return copies


def _chip_exchange_shapes(ps):
    return [jax.ShapeDtypeStruct((3,) + p.shape[1:], p.dtype) for p in ps]


def _chip_exchange_sems(n):
    return [pltpu.SemaphoreType.DMA((n, 3)), pltpu.SemaphoreType.DMA((n, 3))]


def _gather_direct(start, x_refs, out_refs, send_sems, recv_sems, local_sems):
    me = _me()
    peers = [_flip(me, (0, 0, 1))] + [_flip(me, rel + (0,)) for rel in CHIP_RELS]
    for i, (x_ref, out_ref) in enumerate(zip(x_refs, out_refs)):
        local = pltpu.make_async_copy(x_ref, out_ref.at[_slot(me)], local_sems.at[i])
        local.start() if start else local.wait()
        for k, peer in enumerate(peers):
            def copy(block):
                return pltpu.make_async_remote_copy(
                    src_ref=x_ref, dst_ref=out_ref.at[_slot(block)], send_sem=send_sems.at[i, k],
                    recv_sem=recv_sems.at[i, k], device_id=peer, device_id_type=MESH)
            if start:
                copy(me).start()
            else:
                copy(me).wait_send()
                copy(peer).wait_recv()


def _gather_direct_sems(n):
    return [pltpu.SemaphoreType.DMA((n, 4)), pltpu.SemaphoreType.DMA((n, 4)), pltpu.SemaphoreType.DMA((n,))]


def gather_forward_to_sibling(gs, *, name):
    n = len(gs)

    def body(*refs):
        in_refs, out_refs, send_sems, recv_sems = refs[:n], refs[n:2 * n], refs[2 * n], refs[2 * n + 1]
        me = _me()
        sibling = _flip(me, (0, 0, 1))
        chips = [_flip(me, rel + (0,)) for rel in CHIP_RELS]
        sends, recvs = [], []
        for i in range(n):
            for j, chip in enumerate(chips):
                def copy(block):
                    return pltpu.make_async_remote_copy(
                        src_ref=in_refs[i].at[_slot(chip)], dst_ref=out_refs[i].at[_slot(block)],
                        send_sem=send_sems.at[i, j], recv_sem=recv_sems.at[i, j], device_id=sibling, device_id_type=MESH)
                sends.append(copy(chip))
                recvs.append(copy(_flip(chip, (0, 0, 1))))
        for cp in sends:
            cp.start()
        for cp in recvs:
            cp.wait_recv()
        for cp in sends:
            cp.wait_send()

    return pl.pallas_call(
        body, name=name, in_specs=[HBM_SPEC] * n, out_specs=[HBM_SPEC] * n,
        out_shape=[jax.ShapeDtypeStruct(g.shape, g.dtype) for g in gs],
        input_output_aliases={i: i for i in range(n)},
        scratch_shapes=[pltpu.SemaphoreType.DMA((n, 3)), pltpu.SemaphoreType.DMA((n, 3))],
    )(*gs)


def rs_final_sum(p, got, *, name):
    _, R, C = p.shape
    tr = _tile(R, 768, unit=16)

    def body(chip_ref, p_ref, got_ref, o_ref):
        s = p_ref[...].astype(F32)
        for j in range(3):
            s = s + got_ref[j].astype(F32)
        o_ref[...] = s

    mychip = jnp.reshape(2 * lax.axis_index("x") + lax.axis_index("y"), (1,)).astype(jnp.int32)
    return pl.pallas_call(
        body, name=name,
        grid_spec=pltpu.PrefetchScalarGridSpec(
            num_scalar_prefetch=1, grid=(R // tr,),
            in_specs=[pl.BlockSpec((None, tr, C), lambda i, chip_ref: (chip_ref[0], i, 0)),
                      pl.BlockSpec((3, tr, C), lambda i, chip_ref: (0, i, 0))],
            out_specs=pl.BlockSpec((tr, C), lambda i, chip_ref: (i, 0))),
        out_shape=jax.ShapeDtypeStruct((R, C), F32), compiler_params=_params(("arbitrary",)),
    )(mychip, p, got)


def reduce_scatter_big(gs):
    got = rs_exchange_sibling(gs, name="rs_sibling")
    parts = [rs_chip_partials(g, t, name=f"rs_chip_partials_{i}") for i, (g, t) in enumerate(zip(gs, got))]
    got2 = rs_exchange_chips(parts, name="rs_chips")
    return [rs_final_sum(p, t, name=f"rs_final_sum_{i}") for i, (p, t) in enumerate(zip(parts, got2))]


def _blk(off):
    return pl.BlockSpec((BLK, LANES), lambda h, n: (n, off + h))


def _const_spec(shape):
    return pl.BlockSpec(shape, lambda *_: (0,) * len(shape))


def retention_tables(T):
    pos = jnp.arange(T, dtype=F32) - float(PAD)
    inv_freq = 1.0 / (10000.0 ** jnp.linspace(0.0, 1.0, 64, dtype=F32))
    ang = pos[:, None] * inv_freq[None, :]
    cos = jnp.repeat(jnp.cos(ang), 2, axis=1)
    sin = jnp.repeat(jnp.sin(ang), 2, axis=1) * jnp.tile(jnp.array([-1.0, 1.0], F32), 64)[None, :]
    lane = np.arange(LANES)
    perm = jnp.broadcast_to(jnp.asarray((lane[:, None] == (lane[None, :] ^ 1)).astype(np.float32)), (4, LANES, LANES))
    log_gamma = jnp.log1p(-jnp.exp2(-5.0 - jnp.arange(4, dtype=F32)))
    idx = jnp.arange(BLK, dtype=F32)
    diff = idx[:, None] - idx[None, :]
    intra = jnp.where(diff >= 0, jnp.exp(jnp.maximum(diff, 0.0) * log_gamma[:, None, None]), 0.0)
    zeta = jnp.exp((BLK - 1.0 - idx)[None, :] * log_gamma[:, None])
    xi = jnp.exp((idx + 1.0)[None, :] * log_gamma[:, None])
    bc = lambda t: jnp.broadcast_to(t[:, :, None], (4, BLK, LANES))
    return cos, sin, perm, intra, bc(zeta), bc(xi)


def _heads(x):
    return jnp.stack([x[:, h * LANES:(h + 1) * LANES] for h in range(4)])


def _unheads(y):
    return jnp.concatenate([y[h] for h in range(4)], axis=1)


def _ret_chunk(rq, rk, rv, rg, S, cos, sin, intra, zeta, xi, perm):
    q = rq * cos + hdot(rq, perm) * sin
    k = (rk * cos + hdot(rk, perm) * sin) * (128.0 ** -0.5)
    ret = bdot(bdot(q, _t(k)) * intra, rv) + bdot(q * xi, S)
    S_new = S * xi[..., BLK - 1:BLK, :] + bdot(_t(k * zeta), rv)
    c = ret - jnp.mean(ret, axis=-1, keepdims=True)
    out = c * lax.rsqrt(jnp.mean(c * c, axis=-1, keepdims=True) + EPS) * _silu(rg)
    return out, S_new


def _wide(off):
    return pl.BlockSpec((BLK, 4 * LANES), lambda n: (n, off))


def retention_fwd(p, tables, *, name):
    T = p.shape[0]
    N = T // BLK
    cos, sin, perm, intra, zeta, xi = tables

    def body(rq, rk, rv, rg, cos_ref, sin_ref, in_ref, ze_ref, xi_ref, perm_ref, out_ref, sall_ref, s_scr):
        @pl.when(pl.program_id(0) == 0)
        def _():
            s_scr[...] = jnp.zeros_like(s_scr)

        S = s_scr[...]
        sall_ref[...] = S
        out, S_new = _ret_chunk(_heads(rq[...]), _heads(rk[...]), _heads(rv[...]), _heads(rg[...]), S, cos_ref[...],
                                sin_ref[...], in_ref[...], ze_ref[...], xi_ref[...], perm_ref[...])
        out_ref[...] = _unheads(out).astype(BF16)
        s_scr[...] = S_new

    rowtab = pl.BlockSpec((BLK, LANES), lambda n: (n, 0))
    tab = _const_spec((4, BLK, LANES))
    return pl.pallas_call(
        body, name=name, grid=(N,),
        in_specs=[_wide(0), _wide(1), _wide(2), _wide(3), rowtab, rowtab, tab, tab, tab, tab],
        out_specs=[_wide(0), pl.BlockSpec((None, 4, LANES, LANES), lambda n: (n, 0, 0, 0))],
        out_shape=[jax.ShapeDtypeStruct((T, 512), BF16), jax.ShapeDtypeStruct((N, 4, LANES, LANES), F32)],
        scratch_shapes=[pltpu.VMEM((4, LANES, LANES), F32)],
        compiler_params=_params(("arbitrary",)),
    )(p, p, p, p, cos, sin, intra, zeta, xi, perm)


def _row_mask(n):
    return (n * BLK + _iota2((BLK, 1), 0) >= PAD).astype(F32)


def retention_bwd(p, sall, dmixed, tables, *, name):
    T = p.shape[0]
    N = T // BLK
    cos, sin, perm, intra, zeta, xi = tables

    def body(rq, rk, rv, rg, cos_ref, sin_ref, in_ref, ze_ref, xi_ref, perm_ref, sall_ref, do_ref, drq, drk, drv, drg,
             ds_scr):
        n = N - 1 - pl.program_id(0)

        @pl.when(pl.program_id(0) == 0)
        def _():
            ds_scr[...] = jnp.zeros_like(ds_scr)

        f = lambda a, b, c, d, s: _ret_chunk(a, b, c, d, s, cos_ref[...], sin_ref[...], in_ref[...], ze_ref[...],
                                             xi_ref[...], perm_ref[...])
        _, vjp = jax.vjp(f, _heads(rq[...]), _heads(rk[...]), _heads(rv[...]), _heads(rg[...]), sall_ref[...])
        g = vjp((_heads(do_ref[...]), ds_scr[...]))
        mask = _row_mask(n)
        for ref, val in zip((drq, drk, drv, drg), g[:4]):
            ref[...] = _unheads(val) * mask
        ds_scr[...] = g[4]

    def rwide(off):
        return pl.BlockSpec((BLK, 4 * LANES), lambda n: (N - 1 - n, off))

    rowtab = pl.BlockSpec((BLK, LANES), lambda n: (N - 1 - n, 0))
    tab = _const_spec((4, BLK, LANES))
    return pl.pallas_call(
        body, name=name, grid=(N,),
        in_specs=[rwide(0), rwide(1), rwide(2), rwide(3), rowtab, rowtab, tab, tab, tab, tab,
                  pl.BlockSpec((None, 4, LANES, LANES), lambda n: (N - 1 - n, 0, 0, 0)), rwide(0)],
        out_specs=[rwide(0)] * 4, out_shape=[jax.ShapeDtypeStruct((T, 512), F32)] * 4,
        scratch_shapes=[pltpu.VMEM((4, LANES, LANES), F32)],
        compiler_params=_params(("arbitrary",)),
    )(p, p, p, p, cos, sin, intra, zeta, xi, perm, sall, dmixed)


def conv_silu_fwd(p, w, *, name):
    T = p.shape[0]
    N = T // BLK

    def body(x_ref, xp_ref, w_ref, o_ref):
        n = pl.program_id(0)
        cur = x_ref[...]
        cat = jnp.concatenate([jnp.where(n > 0, xp_ref[...], 0.0), cur], axis=0)
        y = w_ref[3:4, :] * cur
        for s in (1, 2, 3):
            y = y + w_ref[3 - s:4 - s, :] * pltpu.roll(cat, s, 0)[BLK:]
        o_ref[...] = _silu(y)

    cw = 4 * LANES
    return pl.pallas_call(
        body, name=name, grid=(N, 3),
        in_specs=[pl.BlockSpec((BLK, cw), lambda n, c: (n, 4 + c)),
                  pl.BlockSpec((BLK, cw), lambda n, c: (jnp.maximum(n - 1, 0), 4 + c)),
                  pl.BlockSpec((4, cw), lambda n, c: (0, c))],
        out_specs=pl.BlockSpec((BLK, cw), lambda n, c: (n, c)),
        out_shape=jax.ShapeDtypeStruct((T, 1536), F32), compiler_params=_params(("arbitrary", "arbitrary")),
    )(p, p, w)


def conv_silu_bwd(p, w, dact, part, *, name):
    T = p.shape[0]
    N = T // BLK
    cw = 4 * LANES

    def body(xp_ref, x_ref, xn_ref, w_ref, da_ref, dan_ref, dx_ref, dw_ref):
        n = pl.program_id(0)
        last = n == N - 1
        cat = jnp.concatenate([jnp.where(n > 0, xp_ref[...], 0.0), x_ref[...], jnp.where(last, 0.0, xn_ref[...])], axis=0)
        shifted = [cat] + [pltpu.roll(cat, s, 0) for s in (1, 2, 3)]
        y = w_ref[3:4, :] * shifted[0]
        for s in (1, 2, 3):
            y = y + w_ref[3 - s:4 - s, :] * shifted[s]
        y = y[BLK:]
        da = jnp.concatenate([da_ref[...], jnp.where(last, 0.0, dan_ref[...])], axis=0)
        sg = _sigmoid(y)
        dy = da * sg * (1.0 + y * (1.0 - sg))
        dx = w_ref[3:4, :] * dy[:BLK]
        for s in (1, 2, 3):
            dx = dx + w_ref[3 - s:4 - s, :] * pltpu.roll(dy, 2 * BLK - s, 0)[:BLK]
        dx_ref[...] = dx * _row_mask(n)

        @pl.when(n == 0)
        def _():
            dw_ref[...] = jnp.zeros_like(dw_ref)

        for s in (0, 1, 2, 3):
            dw_ref[3 - s:4 - s, :] += jnp.sum(dy[:BLK] * shifted[s][BLK:2 * BLK], axis=0, keepdims=True)

    def xs(d):
        return pl.BlockSpec((BLK, cw), lambda n: (jnp.clip(n + d, 0, N - 1), 4 + part))

    return pl.pallas_call(
        body, name=name, grid=(N,),
        in_specs=[xs(-1), xs(0), xs(1), pl.BlockSpec((4, cw), lambda n: (0, part)),
                  pl.BlockSpec((BLK, cw), lambda n: (n, 0)),
                  pl.BlockSpec((BLK, cw), lambda n: (jnp.minimum(n + 1, N - 1), 0))],
        out_specs=[pl.BlockSpec((BLK, cw), lambda n: (n, 0)), pl.BlockSpec((4, cw), lambda n: (0, 0))],
        out_shape=[jax.ShapeDtypeStruct((T, 512), F32), jax.ShapeDtypeStruct((4, 512), F32)],
        compiler_params=_params(("arbitrary",)),
    )(p, p, p, w, dact, dact)


def _softplus(x):
    return jnp.maximum(x, 0.0) + jnp.log1p(jnp.exp(-jnp.abs(x)))


def _pick4(tile, off):
    return jnp.stack([_lane_pick(tile, off + h) for h in range(4)])


def _spread4(v4, off, rows):
    lane = _iota2((rows, LANES), 1)
    out = jnp.where(lane == off, v4[0], 0.0)
    for h in range(1, 4):
        out = out + jnp.where(lane == off + h, v4[h], 0.0)
    return out


def _gdn_chunk(qa, ka, va, z, braw, araw, S, alog, dtb, onorm, rowmask, lincl):
    r, c = _iota2((BLK, BLK), 0), _iota2((BLK, BLK), 1)
    incl, strict = r >= c, r > c
    eye = (r == c).astype(F32)
    q = qa * lax.rsqrt(jnp.sum(qa * qa, axis=-1, keepdims=True) + EPS) * (128.0 ** -0.5)
    k = ka * lax.rsqrt(jnp.sum(ka * ka, axis=-1, keepdims=True) + EPS)
    beta = _sigmoid(braw) * rowmask
    g = -jnp.exp(alog) * _softplus(araw + dtb) * rowmask
    gc = hdot(lincl, jnp.broadcast_to(g, qa.shape))
    decay = jnp.where(incl, jnp.exp(jnp.where(incl, gc - _t(gc), 0.0)), 0.0)
    kb = k * beta
    amat = jnp.where(strict, bdot(kb, _t(k)) * decay, 0.0)
    m = -amat
    inv = eye + m
    pw = hdot(m, m)
    for t in range(6):
        inv = inv + hdot(inv, pw)
        if t < 5:
            pw = hdot(pw, pw)
    egc = jnp.exp(gc)
    u = hdot(inv, va * beta)
    w = hdot(inv, kb * egc)
    qk = jnp.where(incl, bdot(q, _t(k)) * decay, 0.0)
    glast = gc[..., BLK - 1:BLK, :]
    vnew = u - bdot(w, S)
    o = bdot(q * egc, S) + bdot(qk, vnew)
    S_new = S * jnp.exp(glast) + bdot(_t(k * jnp.exp(glast - gc)), vnew)
    out = o * lax.rsqrt(jnp.mean(o * o, axis=-1, keepdims=True) + EPS) * onorm * _silu(z)
    return out, S_new


def _lincl():
    i = np.arange(BLK)
    return jnp.broadcast_to(jnp.asarray((i[:, None] >= i[None, :]).astype(np.float32)), (4, BLK, BLK))


def gdn_fwd(act, p, alog, dtb, onorm, send, *, name):
    T = p.shape[0]
    N = T // BLK
    ns = len(send)

    def body(qa, ka, va, z, ba, alog_ref, dtb_ref, on_ref, l_ref, *refs):
        x_refs, (out_ref, sall_ref), g_refs = refs[:ns], refs[ns:ns + 2], refs[ns + 2:2 * ns + 2]
        s_scr, sems = refs[2 * ns + 2], refs[2 * ns + 3:]
        n = pl.program_id(0)

        @pl.when(n == 0)
        def _():
            s_scr[...] = jnp.zeros_like(s_scr)
            _gather_direct(True, x_refs, g_refs, *sems)

        @pl.when(n == N - 1)
        def _():
            _gather_direct(False, x_refs, g_refs, *sems)

        S = s_scr[...]
        sall_ref[...] = S
        out, S_new = _gdn_chunk(_heads(qa[...]), _heads(ka[...]), _heads(va[...]), _heads(z[...]), _pick4(ba[...], 0),
                                _pick4(ba[...], 4), S, _pick4(alog_ref[...], 0), _pick4(dtb_ref[...], 0), on_ref[...],
                                _row_mask(n), l_ref[...])
        out_ref[...] = _unheads(out).astype(BF16)
        s_scr[...] = S_new

    vec = _const_spec((1, LANES))
    res = pl.pallas_call(
        body, name=name, grid=(N,),
        in_specs=[_wide(0), _wide(1), _wide(2), _wide(7), pl.BlockSpec((BLK, LANES), lambda n: (n, 32)), vec, vec, vec,
                  _const_spec((4, BLK, BLK))] + [HBM_SPEC] * ns,
        out_specs=[_wide(0), pl.BlockSpec((None, 4, LANES, LANES), lambda n: (n, 0, 0, 0))] + [HBM_SPEC] * ns,
        out_shape=[jax.ShapeDtypeStruct((T, 512), BF16), jax.ShapeDtypeStruct((N, 4, LANES, LANES), F32)]
        + [jax.ShapeDtypeStruct((NDEV,) + x.shape, x.dtype) for x in send],
        scratch_shapes=[pltpu.VMEM((4, LANES, LANES), F32)] + _gather_direct_sems(ns),
        compiler_params=_params(("arbitrary",)),
    )(act, act, act, p, p, alog, dtb, onorm, _lincl(), *send)
    return res[0], res[1], res[2:]


def gdn_bwd(act, p, alog, dtb, onorm, sall, dmixed, partials, *, name):
    T = p.shape[0]
    N = T // BLK
    ns = len(partials)

    def body(qa, ka, va, z, ba, alog_ref, dtb_ref, on_ref, l_ref, sall_ref, do_ref, *refs):
        p_refs, (dq_ref, dk_ref, dv_ref, dz_ref, dba_ref, dal_ref, ddt_ref, don_ref) = refs[:ns], refs[ns:ns + 8]
        got_refs, ds_scr, sems = refs[ns + 8:2 * ns + 8], refs[2 * ns + 8], refs[2 * ns + 9:]
        step = pl.program_id(0)
        n = N - 1 - step

        @pl.when(step == 0)
        def _():
            ds_scr[...] = jnp.zeros_like(ds_scr)
            dal_ref[...] = jnp.zeros_like(dal_ref)
            ddt_ref[...] = jnp.zeros_like(ddt_ref)
            don_ref[...] = jnp.zeros_like(don_ref)
            for cp in _chip_exchange_copies(p_refs, got_refs, *sems):
                cp.start()

        @pl.when(step == N - 1)
        def _():
            for cp in _chip_exchange_copies(p_refs, got_refs, *sems):
                cp.wait()

        rowmask, lincl = _row_mask(n), l_ref[...]
        f = lambda *a: _gdn_chunk(*a, rowmask, lincl)
        _, vjp = jax.vjp(f, _heads(qa[...]), _heads(ka[...]), _heads(va[...]), _heads(z[...]), _pick4(ba[...], 0),
                         _pick4(ba[...], 4), sall_ref[...], _pick4(alog_ref[...], 0), _pick4(dtb_ref[...], 0),
                         on_ref[...])
        g = vjp((_heads(do_ref[...]), ds_scr[...]))
        dq_ref[...] = _unheads(g[0]) * rowmask
        dk_ref[...] = _unheads(g[1]) * rowmask
        dv_ref[...] = _unheads(g[2]) * rowmask
        dz_ref[...] = _unheads(g[3]) * rowmask
        dba_ref[...] = (_spread4(g[4], 0, BLK) + _spread4(g[5], 4, BLK)) * rowmask
        ds_scr[...] = g[6]
        dal_ref[...] += _spread4(g[7], 0, 1)
        ddt_ref[...] += _spread4(g[8], 0, 1)
        don_ref[...] += g[9]

    def rwide(off):
        return pl.BlockSpec((BLK, 4 * LANES), lambda s: (N - 1 - s, off))

    vec = _const_spec((1, LANES))
    col = pl.BlockSpec((BLK, LANES), lambda s: (N - 1 - s, 0))
    res = pl.pallas_call(
        body, name=name, grid=(N,),
        in_specs=[rwide(0), rwide(1), rwide(2), rwide(7), pl.BlockSpec((BLK, LANES), lambda s: (N - 1 - s, 32)), vec, vec,
                  vec, _const_spec((4, BLK, BLK)),
                  pl.BlockSpec((None, 4, LANES, LANES), lambda s: (N - 1 - s, 0, 0, 0)), rwide(1)] + [HBM_SPEC] * ns,
        out_specs=[rwide(0)] * 4 + [col, vec, vec, vec] + [HBM_SPEC] * ns,
        out_shape=[jax.ShapeDtypeStruct((T, 512), F32)] * 4 + [jax.ShapeDtypeStruct((T, LANES), F32)]
        + [jax.ShapeDtypeStruct((1, LANES), F32)] * 3 + _chip_exchange_shapes(partials),
        scratch_shapes=[pltpu.VMEM((4, LANES, LANES), F32)] + _chip_exchange_sems(ns),
        compiler_params=_params(("arbitrary",)),
    )(act, act, act, p, p, alog, dtb, onorm, _lincl(), sall, dmixed, *partials)
    return res[:8], res[8:]


NEG = -1e30


def _swa_block(q, k0, kp, kc, v0, vp, vc, sink, n):
    r, c = _iota2((BLK, BLK), 0), _iota2((BLK, BLK), 1)
    m0 = (c >= PAD) & (c <= n * BLK + r)
    mp = (n >= 2) & (c > r)
    mc = (n >= 1) & (r >= c)
    b = lambda t: jnp.broadcast_to(t, (4,) + t.shape)
    qs = q * (64.0 ** -0.5)
    s0 = jnp.where(m0, bdot(qs, _t(b(k0))), NEG)
    sp = jnp.where(mp, bdot(qs, _t(b(kp))), NEG)
    sc = jnp.where(mc, bdot(qs, _t(b(kc))), NEG)
    mx = jnp.maximum(jnp.max(jnp.maximum(jnp.maximum(s0, sp), sc), axis=-1, keepdims=True), sink)
    mx = lax.stop_gradient(mx)
    p0, pp, pc = jnp.exp(s0 - mx), jnp.exp(sp - mx), jnp.exp(sc - mx)
    den = (jnp.sum(p0, axis=-1, keepdims=True) + jnp.sum(pp, axis=-1, keepdims=True)
           + jnp.sum(pc, axis=-1, keepdims=True) + jnp.exp(sink - mx))
    return (bdot(p0, b(v0)) + bdot(pp, b(vp)) + bdot(pc, b(vc))) / den


def _swa_specs():
    rows = (lambda n: 0, lambda n: jnp.maximum(n - 1, 0), lambda n: n)

    def kv_spec(off, row):
        return pl.BlockSpec((BLK, LANES), lambda g, n: (row(n), off + g))

    q = pl.BlockSpec((BLK, 4 * LANES), lambda g, n: (n, g))
    return q, [kv_spec(off, row) for off in (8, 10) for row in rows]


def swa_fwd(p2, sinkrow, *, name):
    T = p2.shape[0]
    N = T // BLK

    def body(q, k0, kp, kc, v0, vp, vc, sink_ref, o_ref):
        g, n = pl.program_id(0), pl.program_id(1)
        f32 = lambda ref: ref[...].astype(F32)
        o = _swa_block(_heads(f32(q)), f32(k0), f32(kp), f32(kc), f32(v0), f32(vp), f32(vc),
                       _pick4(sink_ref[...], 4 * g), n)
        o_ref[...] = _unheads(o).astype(BF16)

    q, kv = _swa_specs()
    return pl.pallas_call(
        body, name=name, grid=(2, N), in_specs=[q] + kv + [_const_spec((1, LANES))],
        out_specs=q, out_shape=jax.ShapeDtypeStruct((T, 1024), BF16),
        compiler_params=_params(("arbitrary", "arbitrary")),
    )(p2, p2, p2, p2, p2, p2, p2, sinkrow)


def swa_bwd(p2, sinkrow, dmixed, *, name):
    T = p2.shape[0]
    N = T // BLK

    def body(q, k0, kp, kc, v0, vp, vc, sink_ref, do_ref, dq_ref, dk_ref, dv_ref, dsink_ref):
        g, n = pl.program_id(0), pl.program_id(1)

        @pl.when(n == 0)
        def _():
            dk_ref[...] = jnp.zeros_like(dk_ref)
            dv_ref[...] = jnp.zeros_like(dv_ref)

        @pl.when((g == 0) & (n == 0))
        def _():
            dsink_ref[...] = jnp.zeros_like(dsink_ref)

        f = lambda *a: _swa_block(*a, n)
        f32 = lambda ref: ref[...].astype(F32)
        _, vjp = jax.vjp(f, _heads(f32(q)), f32(k0), f32(kp), f32(kc), f32(v0), f32(vp), f32(vc),
                         _pick4(sink_ref[...], 4 * g))
        dq, dk0, dkp, dkc, dv0, dvp, dvc, dsink = vjp(_heads(do_ref[...]))
        dq_ref[...] = _unheads(dq)
        prev = pl.ds(pl.multiple_of(jnp.maximum(n - 1, 0) * BLK, BLK), BLK)
        cur = pl.ds(pl.multiple_of(n * BLK, BLK), BLK)
        for ref, d0, dp, dc in ((dk_ref, dk0, dkp, dkc), (dv_ref, dv0, dvp, dvc)):
            ref[0:BLK, :] += d0
            ref[prev, :] += dp
            ref[cur, :] += dc
        dsink_ref[...] += _spread4(dsink, 4 * g, 1)

    qspec, kv = _swa_specs()
    slab = pl.BlockSpec((T, LANES), lambda g, n: (0, g))
    return pl.pallas_call(
        body, name=name, grid=(2, N), in_specs=[qspec] + kv + [_const_spec((1, LANES)), qspec],
        out_specs=[qspec, slab, slab, _const_spec((1, LANES))],
        out_shape=[jax.ShapeDtypeStruct((T, 1024), F32), jax.ShapeDtypeStruct((T, 256), F32),
                   jax.ShapeDtypeStruct((T, 256), F32), jax.ShapeDtypeStruct((1, LANES), F32)],
        compiler_params=_params(("arbitrary", "arbitrary")),
    )(p2, p2, p2, p2, p2, p2, p2, sinkrow, dmixed)


def _split_dot(x, m):
    rows = x.shape[0]
    hi = x.astype(BF16)
    lo = (x - hi.astype(F32)).astype(BF16)
    r = _nn(jnp.concatenate([hi, lo], axis=0), m)
    return r[:rows] + r[rows:]


def _tri_and_ones(strict, ones=True):
    i = np.arange(BLK)
    m = (i[:, None] > i[None, :]) if strict else (i[:, None] >= i[None, :])
    if ones:
        m = np.concatenate([m, np.ones((BLK, BLK), bool)], axis=1)
    return jnp.asarray(m.astype(np.float32), dtype=BF16)


def _later_and_row_sums(x, m):
    r = _split_dot(x, m)
    if m.shape[1] == 2 * BLK:
        return r[:, :BLK], r[:, BLK:]
    return r, jnp.broadcast_to(jnp.sum(x, axis=1, keepdims=True), x.shape)


SB_PAIR = 2
SB_FWD_GROUP = 4


def _sb_positions():
    r, s = _iota2((BLK, BLK), 0), _iota2((BLK, BLK), 1)
    return s - r, s


def _sb_weights(qbs, ks, base, n, pos, carries, after):
    nh, kb = len(qbs), len(ks[0])
    zs = [[_nt(qbs[h], ks[h][c]) for c in range(kb)] for h in range(nh)]
    valid = [(pos[0] < (n - base - c) * BLK) & (pos[1] >= PAD - (base + c) * BLK) for c in range(kb)]
    lb = [[None] * kb for _ in range(nh)]
    sums = [[None] * kb for _ in range(nh)]
    for c in range(kb):
        for h in range(nh):
            z = zs[h][c]
            lb[h][c] = jnp.minimum(z, 0.0) - jnp.log(1.0 + jnp.exp(-jnp.abs(z)))
            sums[h][c] = _later_and_row_sums(jnp.where(valid[c], lb[h][c] - z, 0.0), after)
    a = [[None] * kb for _ in range(nh)]
    carries = list(carries)
    for c in reversed(range(kb)):
        for h in range(nh):
            a[h][c] = jnp.where(valid[c], jnp.exp(lb[h][c] + carries[h] + sums[h][c][0]), 0.0)
            carries[h] = carries[h] + sums[h][c][1]
    return valid, lb, a, carries


def _key_blocks(n_blocks):
    return next(k for k in (5, 3, 1) if n_blocks % k == 0)


def sb_fwd(p2, *, name):
    T = p2.shape[0]
    N = T // BLK
    kb = _key_blocks(N)
    nh = SB_FWD_GROUP
    heads = [slice(h * LANES, (h + 1) * LANES) for h in range(nh)]

    def body(q_ref, k_ref, v_ref, after_ref, o_ref, of_ref):
        n = pl.program_id(1)
        qbs = [(q_ref[:, hs].astype(F32) * (64.0 ** -0.5)).astype(BF16) for hs in heads]
        after, pos = after_ref[...], _sb_positions()
        nsup = n // kb + 1

        def step(t, c):
            accs, carries = c
            base = (nsup - 1 - t) * kb
            rows = [pl.ds(pl.multiple_of((base + sub) * BLK, BLK), BLK) for sub in range(kb)]
            ks = [[k_ref[r, hs] for r in rows] for hs in heads]
            _, _, a, carries = _sb_weights(qbs, ks, base, n, pos, carries, after)
            accs = list(accs)
            for sub, r in enumerate(rows):
                for h, hs in enumerate(heads):
                    accs[h] = accs[h] + _nn(a[h][sub].astype(BF16), v_ref[r, hs])
            return accs, carries

        zero = [jnp.zeros((BLK, LANES), F32)] * nh
        accs, _ = lax.fori_loop(0, nsup, step, (zero, zero))
        acc = jnp.concatenate(accs, axis=1)
        o_ref[...] = acc.astype(BF16)
        of_ref[...] = acc

    wide = nh * LANES

    def slab(off):
        return pl.BlockSpec((T, wide), lambda g, n: (0, off + g))

    def blk(off):
        return pl.BlockSpec((BLK, wide), lambda g, n: (n, off + g))

    return pl.pallas_call(
        body, name=name, grid=(8 // nh, N),
        in_specs=[blk(12 // nh), slab(20 // nh), slab(28 // nh), _const_spec((BLK, BLK))],
        out_specs=[blk(0), blk(0)],
        out_shape=[jax.ShapeDtypeStruct((T, 1024), BF16), jax.ShapeDtypeStruct((T, 1024), F32)],
        compiler_params=_params(("arbitrary", "arbitrary")),
    )(p2, p2, p2, _tri_and_ones(True, ones=False))


def sb_bwd(p2, o, dmixed, *, name):
    T = p2.shape[0]
    N = T // BLK
    kb = _key_blocks(N)

    heads = [slice(h * LANES, (h + 1) * LANES) for h in range(SB_PAIR)]
    scale = 64.0 ** -0.5

    def body(q_ref, k_ref, v_ref, after_ref, from_ref, o_ref, do_ref, dq_ref, dk_ref, dv_ref, dkt_scr, dvt_scr):
        n = pl.program_id(1)

        @pl.when(n == 0)
        def _():
            dkt_scr[...] = jnp.zeros_like(dkt_scr)
            dvt_scr[...] = jnp.zeros_like(dvt_scr)

        qbs, qts, dobs, dots, totals = [], [], [], [], []
        for hs in heads:
            qs = q_ref[:, hs].astype(F32) * scale
            do = do_ref[:, hs]
            qbs.append(qs.astype(BF16))
            qts.append(qs.T.astype(BF16))
            dobs.append(do.astype(BF16))
            dots.append(do.T.astype(BF16))
            total = jnp.sum(dobs[-1].astype(F32) * o_ref[:, hs], axis=1, keepdims=True)
            totals.append(jnp.broadcast_to(total, (BLK, LANES)))
        after, frm, pos = after_ref[...], from_ref[...], _sb_positions()
        nsup = n // kb + 1

        def step(t, c):
            dqs, carries, gcarries = c
            base = (nsup - 1 - t) * kb
            rows = [pl.ds(pl.multiple_of((base + sub) * BLK, BLK), BLK) for sub in range(kb)]
            ks = [[k_ref[r, hs] for r in rows] for hs in heads]
            valid, lb, a, carries = _sb_weights(qbs, ks, base, n, pos, carries, after)
            das = [[_nt(dobs[h], v_ref[r, hs]) for r in rows] for h, hs in enumerate(heads)]
            ab = [[a[h][sub].astype(BF16) for sub in range(kb)] for h in range(SB_PAIR)]
            g = [[None] * kb for _ in heads]
            sums = [[None] * kb for _ in heads]
            for sub in range(kb):
                for h in range(SB_PAIR):
                    g[h][sub] = das[h][sub] * ab[h][sub].astype(F32)
                    sums[h][sub] = _later_and_row_sums(g[h][sub], frm)
            dqs, gcarries = list(dqs), list(gcarries)
            for sub in reversed(range(kb)):
                for h in range(SB_PAIR):
                    before = totals[h] - (gcarries[h] + sums[h][sub][0])
                    gcarries[h] = gcarries[h] + sums[h][sub][1]
                    beta = jnp.exp(lb[h][sub])
                    dz = jnp.where(valid[sub], g[h][sub] - beta * (g[h][sub] + before), 0.0).astype(BF16)
                    dqs[h] = dqs[h] + _nn(dz, ks[h][sub])
                    dkt_scr[h * N + base + sub] += _nn(qts[h], dz)
                    dvt_scr[h * N + base + sub] += _nn(dots[h], ab[h][sub])
            return dqs, carries, gcarries

        zero = [jnp.zeros((BLK, LANES), F32)] * SB_PAIR
        dqs, _, _ = lax.fori_loop(0, nsup, step, (zero, zero, zero))
        dq_ref[...] = (jnp.concatenate(dqs, axis=1) * scale).astype(dq_ref.dtype)

        @pl.when(n == N - 1)
        def _():
            def flush(j, _):
                rows = pl.ds(pl.multiple_of(j * BLK, BLK), BLK)
                for h, hs in enumerate(heads):
                    dk_ref[rows, hs] = dkt_scr[h * N + j].T.astype(dk_ref.dtype)
                    dv_ref[rows, hs] = dvt_scr[h * N + j].T.astype(dv_ref.dtype)
                return 0

            lax.fori_loop(0, N, flush, 0)

    wide = SB_PAIR * LANES

    def slab(off):
        return pl.BlockSpec((T, wide), lambda g, n: (0, off + g))

    def blk(off):
        return pl.BlockSpec((BLK, wide), lambda g, n: (n, off + g))

    tri = _const_spec((BLK, BLK))
    return pl.pallas_call(
        body, name=name, grid=(8 // SB_PAIR, N),
        in_specs=[blk(12 // SB_PAIR), slab(20 // SB_PAIR), slab(28 // SB_PAIR), tri, tri, blk(0), blk(8 // SB_PAIR)],
        out_specs=[blk(0), slab(0), slab(0)],
        out_shape=[jax.ShapeDtypeStruct((T, 1024), BF16)] * 3,
        scratch_shapes=[pltpu.VMEM((SB_PAIR * N, LANES, LANES), F32), pltpu.VMEM((SB_PAIR * N, LANES, LANES), F32)],
        compiler_params=_params(("arbitrary", "arbitrary")),
    )(p2, p2, p2, _tri_and_ones(True, ones=False), _tri_and_ones(False, ones=False), o, dmixed)


def ffn_fwd(h, g_pre, g_post, wg, wu, wd, tag):
    u, gate, up, act = norm_mm(h, g_pre, (wg, wu), swiglu=True, wt=True, name=f"ffn_up_{tag}")
    y, h_new = mm_norm_res([act], [wd], h, g_post, 0.5, name=f"ffn_down_{tag}")
    return h_new, (h, u, gate, up, y)


def ffn_bwd(saved, dh, g_pre, g_post, wg, wu, wd, tag):
    h, u, gate, up, y = saved
    dy, dg_post, dgate, dup, act = normbwd_mm_nt(dh, y, g_post, wd, 0.5, (gate, up), name=f"ffn_bwd_down_{tag}")
    dwd = mm_tn(act, dy, name=f"ffn_dwd_{tag}")
    dwg = mm_tn(dgate, u, name=f"ffn_dwg_{tag}")
    dwu = mm_tn(dup, u, name=f"ffn_dwu_{tag}")
    dh_in, dg_pre = mm_nt_normbwd([dgate, dup], [wg, wu], h, g_pre, dh, wt=True, name=f"ffn_bwd_up_{tag}")
    return dh_in, (dg_pre, dg_post), (dwg, dwu, dwd)


def _lane_row(v):
    v = v.reshape(1, -1)
    return jnp.pad(v, ((0, 0), (0, LANES - v.shape[1])))


AB_WIDTHS = (512,) * 8 + (LANES,)


def mixer_ab_fwd(h, g_pre, g_post, w_in, conv_w, a_log, dt_bias, out_norm, w_out, tables, send):
    u, p = norm_mm(h, g_pre, (w_in,), swiglu=False, name="ab_in")
    ret, sall_r = retention_fwd(p, tables, name="retention_fwd")
    act = conv_silu_fwd(p, conv_w, name="conv_fwd")
    gdn, sall_g, gathered = gdn_fwd(act, p, _lane_row(a_log), _lane_row(dt_bias), out_norm.reshape(1, LANES), send,
                                    name="gdn_fwd")
    y, h_new = mm_norm_res([ret, gdn], [w_out[:512], w_out[512:]], h, g_post, 1.0, name="ab_out")
    return h_new, (h, u, p, ret, sall_r, act, gdn, sall_g, y), gathered


def mixer_ab_bwd(saved, dh, g_pre, g_post, w_in, conv_w, a_log, dt_bias, out_norm, w_out, tables, partials):
    h, u, p, ret, sall_r, act, gdn, sall_g, y = saved
    dy, dg_post, dmixed = normbwd_mm_nt(dh, y, g_post, w_out, 1.0, name="ab_bwd_out")
    dw_out = jnp.concatenate([mm_tn(ret, dy, name="ab_dwout_ret"), mm_tn(gdn, dy, name="ab_dwout_gdn")], axis=0)
    pieces = list(retention_bwd(p, sall_r, dmixed, tables, name="retention_bwd"))
    (dqa, dka, dva, dz, dba, dalog, ddtb, donorm), arrived = gdn_bwd(
        act, p, _lane_row(a_log), _lane_row(dt_bias), out_norm.reshape(1, LANES), sall_g, dmixed, partials,
        name="gdn_bwd")
    dconv = []
    for part, dact in enumerate((dqa, dka, dva)):
        dx, dw = conv_silu_bwd(p, conv_w, dact, part, name=f"conv_bwd_{part}")
        pieces.append(dx)
        dconv.append(dw)
    pieces += [dz, dba]
    offs = np.cumsum((0,) + AB_WIDTHS)
    w_parts = [w_in[:, a:b] for a, b in zip(offs[:-1], offs[1:])]
    dh_in, dg_pre = mm_nt_normbwd(pieces, w_parts, h, g_pre, dh, name="ab_bwd_in")
    dw_in = jnp.concatenate([mm_tn(u, pc, name=f"ab_dwin_{i}") for i, pc in enumerate(pieces)], axis=1)
    small = (jnp.concatenate(dconv, axis=1), dalog[:, :4], ddtb[:, :4], donorm)
    return dh_in, (dg_pre, dg_post), (dw_in, dw_out), small, arrived


CD_WIDTHS = (1024, 256, 256, 1024, 1024, 1024)


def mixer_cd_fwd(h, g_pre, g_post, w_in, sinks, w_out):
    u, p2 = norm_mm(h, g_pre, (w_in,), swiglu=False, out_dtype=BF16, name="cd_in")
    swa = swa_fwd(p2, _lane_row(sinks), name="swa_fwd")
    sb, sb_f32 = sb_fwd(p2, name="sb_fwd")
    y, h_new = mm_norm_res([swa, sb], [w_out[:1024], w_out[1024:]], h, g_post, 1.0, name="cd_out")
    return h_new, (h, u, p2, swa, sb, sb_f32, y)


def mixer_cd_bwd(saved, dh, g_pre, g_post, w_in, sinks, w_out):
    h, u, p2, swa, sb, sb_f32, y = saved
    dy, dg_post, dmixed = normbwd_mm_nt(dh, y, g_post, w_out, 1.0, name="cd_bwd_out")
    dw_out = jnp.concatenate([mm_tn(swa, dy, name="cd_dwout_swa"), mm_tn(sb, dy, name="cd_dwout_sb")], axis=0)
    dq_c, dk_c, dv_c, dsink = swa_bwd(p2, _lane_row(sinks), dmixed, name="swa_bwd")
    pieces = [dq_c, dk_c, dv_c] + list(sb_bwd(p2, sb_f32, dmixed, name="sb_bwd"))
    offs = np.cumsum((0,) + CD_WIDTHS)
    w_parts = [w_in[:, a:b] for a, b in zip(offs[:-1], offs[1:])]
    dh_in, dg_pre = mm_nt_normbwd(pieces, w_parts, h, g_pre, dh, name="cd_bwd_in")
    dw_in = jnp.concatenate([mm_tn(u, pc, name=f"cd_dwin_{i}") for i, pc in enumerate(pieces)], axis=1)
    return dh_in, (dg_pre, dg_post), (dw_in, dw_out), dsink[:, :8]


def _pad_heads(w, axis):
    shape = w.shape
    w = w.reshape(shape[:axis] + (shape[axis] // 64, 64) + shape[axis + 1:])
    pad = [(0, 0)] * w.ndim
    pad[axis + 1] = (0, 64)
    return jnp.pad(w, pad).reshape(shape[:axis] + (2 * shape[axis],) + shape[axis + 1:])


def _unpad_heads(w, axis):
    shape = w.shape
    w = w.reshape(shape[:axis] + (shape[axis] // 128, 128) + shape[axis + 1:])
    w = lax.slice_in_dim(w, 0, 64, axis=axis + 1)
    return w.reshape(shape[:axis] + (shape[axis] // 2,) + shape[axis + 1:])


SMALL_SHARDED = (("meta_tokens", (NMETA, LANES), 1), ("norm_gains", (2, 6, LANES), 2), ("ab_conv_w", (1, 4, 192), 2))
SMALL_REPL = (("ab_a_log", (1, 4)), ("ab_dt_bias", (1, 4)), ("ab_out_norm", (1, LANES)), ("cd_sinks", (1, 8)))


def _stack_shards(g, axis):
    full = jnp.moveaxis(g, 0, axis)
    shape = full.shape
    return full.reshape(shape[:axis] + (shape[axis] * shape[axis + 1],) + shape[axis + 2:])


def _split_shards(full, axis):
    shape = full.shape
    g = full.reshape(shape[:axis] + (NDEV, shape[axis] // NDEV) + shape[axis + 1:])
    return jnp.moveaxis(g, axis, 0)


def _pad_rows8(a):
    rows = []
    for x in a:
        flat = x.reshape(x.shape[0], -1)
        n = -(-flat.shape[1] // LANES) * LANES
        rows.append(jnp.pad(flat, ((0, 0), (0, n - flat.shape[1]))).reshape(x.shape[0], n // LANES, LANES))
    cat = jnp.concatenate(rows, axis=1)
    return jnp.pad(cat, ((0, 0), (0, -cat.shape[1] % 8), (0, 0)))


def _unpad_rows8(packed, shapes):
    out, at = [], 0
    for shape in shapes:
        size = int(np.prod(shape))
        nrow = -(-size // LANES)
        blk = packed[:, at:at + nrow].reshape(packed.shape[0], -1)[:, :size]
        out.append(blk.reshape((packed.shape[0],) + tuple(shape)))
        at += nrow
    return out


def kernel(x, meta_tokens, norm_gains, ffn_w_gate, ffn_w_up, ffn_w_down, ab_w_in, ab_conv_w, ab_a_log, ab_dt_bias, ab_out_norm, ab_w_out, cd_w_in, cd_sinks, cd_w_out, loss_target, m_meta_tokens, m_norm_gains, m_ffn_w_gate, m_ffn_w_up, m_ffn_w_down, m_ab_w_in, m_ab_conv_w, m_ab_a_log, m_ab_dt_bias, m_ab_out_norm, m_ab_w_out, m_cd_w_in, m_cd_sinks, m_cd_w_out, v_meta_tokens, v_norm_gains, v_ffn_w_gate, v_ffn_w_up, v_ffn_w_down, v_ab_w_in, v_ab_conv_w, v_ab_a_log, v_ab_dt_bias, v_ab_out_norm, v_ab_w_out, v_cd_w_in, v_cd_sinks, v_cd_w_out):
    w = dict(meta_tokens=meta_tokens, norm_gains=norm_gains, ffn_w_gate=ffn_w_gate, ffn_w_up=ffn_w_up,
             ffn_w_down=ffn_w_down, ab_w_in=ab_w_in, ab_conv_w=ab_conv_w, ab_a_log=ab_a_log, ab_dt_bias=ab_dt_bias,
             ab_out_norm=ab_out_norm, ab_w_out=ab_w_out, cd_w_in=cd_w_in, cd_sinks=cd_sinks, cd_w_out=cd_w_out)
    m = dict(meta_tokens=m_meta_tokens, norm_gains=m_norm_gains, ffn_w_gate=m_ffn_w_gate, ffn_w_up=m_ffn_w_up,
             ffn_w_down=m_ffn_w_down, ab_w_in=m_ab_w_in, ab_conv_w=m_ab_conv_w, ab_a_log=m_ab_a_log,
             ab_dt_bias=m_ab_dt_bias, ab_out_norm=m_ab_out_norm, ab_w_out=m_ab_w_out, cd_w_in=m_cd_w_in,
             cd_sinks=m_cd_sinks, cd_w_out=m_cd_w_out)
    v = dict(meta_tokens=v_meta_tokens, norm_gains=v_norm_gains, ffn_w_gate=v_ffn_w_gate, ffn_w_up=v_ffn_w_up,
             ffn_w_down=v_ffn_w_down, ab_w_in=v_ab_w_in, ab_conv_w=v_ab_conv_w, ab_a_log=v_ab_a_log,
             ab_dt_bias=v_ab_dt_bias, ab_out_norm=v_ab_out_norm, ab_w_out=v_ab_w_out, cd_w_in=v_cd_w_in,
             cd_sinks=v_cd_sinks, cd_w_out=v_cd_w_out)
    order = list(w)
    S = x.shape[1]
    T = S + BLK

    fs = DFF // NDEV

    def ffn_local(i):
        return jnp.concatenate([jnp.swapaxes(ffn_w_gate[i], 1, 2).reshape(2 * fs, D),
                                jnp.swapaxes(ffn_w_up[i], 1, 2).reshape(2 * fs, D), ffn_w_down[i].reshape(2 * fs, D)],
                               axis=0).astype(BF16)

    def ffn_mats(gathered, i):
        mat = lambda b: gathered[:, b * fs:(b + 1) * fs].reshape(DFF, D)
        return [{(i, j): mat(2 * kind + j) for j in range(2)} for kind in range(3)]

    ffn0_all, abin_all, about_all = all_gather_big(
        [ffn_local(0), ab_w_in[0].astype(BF16), ab_w_out[0].astype(BF16)], name="gather_layer0")
    wg, wu, wd = ffn_mats(ffn0_all, 0)
    ab_in = jnp.pad(_stack_shards(abin_all, 1), ((0, 0), (0, AB_INP - AB_IN)))
    ab_out = about_all.reshape(D, D)
    layer1_local = [ffn_local(1), cd_w_in[0].astype(BF16), cd_w_out[0].astype(BF16)]
    small_src = jnp.broadcast_to(_pad_rows8([w[n][None] for n, _, _ in SMALL_SHARDED]), (NDEV, 40, LANES))
    small_all = _unpad_rows8(all_to_all_small(small_src, name="gather_small"), [s for _, s, _ in SMALL_SHARDED])
    full = {n: _stack_shards(g, ax) for (n, _, ax), g in zip(SMALL_SHARDED, small_all)}
    conv_w = full["ab_conv_w"][0]
    gains = full["norm_gains"].reshape(2, 6, 1, D)
    tables = retention_tables(T)

    h = jnp.concatenate([jnp.zeros((PAD, D), F32), full["meta_tokens"], x[0]], axis=0)
    h, s00 = ffn_fwd(h, gains[0, 0], gains[0, 1], wg[0, 0], wu[0, 0], wd[0, 0], "00")
    h, sab, layer1_part = mixer_ab_fwd(h, gains[0, 2], gains[0, 3], ab_in, conv_w, ab_a_log, ab_dt_bias, ab_out_norm,
                                       ab_out, tables, layer1_local)
    ffn1_all, cdin_all, cdout_all = gather_forward_to_sibling(list(layer1_part), name="gather_layer1_finish")
    for full1, new in zip((wg, wu, wd), ffn_mats(ffn1_all, 1)):
        full1.update(new)
    cd_in = _pad_heads(_stack_shards(cdin_all, 1), 1)
    cd_out = _pad_heads(cdout_all.reshape(D, D), 0)
    h, s01 = ffn_fwd(h, gains[0, 4], gains[0, 5], wg[0, 1], wu[0, 1], wd[0, 1], "01")
    h, s10 = ffn_fwd(h, gains[1, 0], gains[1, 1], wg[1, 0], wu[1, 0], wd[1, 0], "10")
    h, scd = mixer_cd_fwd(h, gains[1, 2], gains[1, 3], cd_in, cd_sinks, cd_out)
    h, s11 = ffn_fwd(h, gains[1, 4], gains[1, 5], wg[1, 1], wu[1, 1], wd[1, 1], "11")
    loss_tile, dh = loss_and_grad(h, loss_target[0], name="loss")
    loss = lax.psum(loss_tile[0, 0], ("x", "y", "c"))

    dgain = [[None] * 6, [None] * 6]
    dffn = {}
    dh, (dgain[1][4], dgain[1][5]), dffn[1, 1] = ffn_bwd(s11, dh, gains[1, 4], gains[1, 5], wg[1, 1], wu[1, 1], wd[1, 1], "11")
    dh, (dgain[1][2], dgain[1][3]), (dcd_in, dcd_out), dsinks = mixer_cd_bwd(scd, dh, gains[1, 2], gains[1, 3], cd_in, cd_sinks, cd_out)
    dh, (dgain[1][0], dgain[1][1]), dffn[1, 0] = ffn_bwd(s10, dh, gains[1, 0], gains[1, 1], wg[1, 0], wu[1, 0], wd[1, 0], "10")
    dh, (dgain[0][4], dgain[0][5]), dffn[0, 1] = ffn_bwd(s01, dh, gains[0, 4], gains[0, 5], wg[0, 1], wu[0, 1], wd[0, 1], "01")
    ffn_send = lambda ij: jnp.concatenate([t.astype(BF16).reshape(NDEV, fs, D) for t in dffn[ij]], axis=1)
    early = [(0, 1), (1, 0), (1, 1)]
    early_send = [ffn_send(ij) for ij in early] + [_split_shards(_unpad_heads(dcd_in, 1).astype(BF16), 1),
                                                   _unpad_heads(dcd_out, 0).astype(BF16).reshape(NDEV, D // NDEV, D)]
    early_sib = rs_exchange_sibling(early_send, name="rs_sibling_early")
    early_parts = [rs_chip_partials(g, t, name=f"rs_chip_partials_early_{i}")
                   for i, (g, t) in enumerate(zip(early_send, early_sib))]
    dh, (dgain[0][2], dgain[0][3]), (dab_in, dab_out), (dconv, dalog, ddtb, donorm), early_got = mixer_ab_bwd(
        sab, dh, gains[0, 2], gains[0, 3], ab_in, conv_w, ab_a_log, ab_dt_bias, ab_out_norm, ab_out, tables, early_parts)
    dh, (dgain[0][0], dgain[0][1]), dffn[0, 0] = ffn_bwd(s00, dh, gains[0, 0], gains[0, 1], wg[0, 0], wu[0, 0], wd[0, 0], "00")
    grad_x = dh[BLK:][None]

    gfull = dict(meta_tokens=dh[PAD:BLK], norm_gains=jnp.stack([jnp.concatenate(r, axis=0) for r in dgain]),
                 ab_conv_w=dconv[None])
    early_g = [rs_final_sum(p, t, name=f"rs_final_sum_early_{i}") for i, (p, t) in enumerate(zip(early_parts, early_got))]
    ffn00_g, abin_g, about_g = reduce_scatter_big(
        [ffn_send((0, 0)), _split_shards(dab_in[:, :AB_IN].astype(BF16), 1),
         dab_out.astype(BF16).reshape(NDEV, D // NDEV, D)])
    cdin_g, cdout_g = early_g[3:]
    ffn_by = dict(zip(early, early_g[:3]))
    ffn_by[0, 0] = ffn00_g
    ffn_g = jnp.stack([jnp.stack([ffn_by[i, j].reshape(3, fs, D) for j in range(2)]) for i in range(2)])
    ffn_g = jnp.moveaxis(ffn_g, 2, 0)
    grads = dict(ffn_w_gate=jnp.swapaxes(ffn_g[0], 2, 3), ffn_w_up=jnp.swapaxes(ffn_g[1], 2, 3), ffn_w_down=ffn_g[2],
                 ab_w_in=abin_g[None], ab_w_out=about_g[None], cd_w_in=cdin_g[None], cd_w_out=cdout_g[None])
    repl = [jnp.broadcast_to(t[None], (NDEV,) + t.shape) for t in (dalog, ddtb, donorm, dsinks)]
    ssend = _pad_rows8([_split_shards(gfull[n], ax) for n, _, ax in SMALL_SHARDED] + repl)
    ssum = sum_slots(all_to_all_small(ssend, name="exchange_small_grads"), name="sum_small_grads")[None]
    small = _unpad_rows8(ssum, [s for _, s, _ in SMALL_SHARDED] + [s for _, s in SMALL_REPL])
    grads.update({n: g[0] for n, g in zip([n for n, _, _ in SMALL_SHARDED] + [n for n, _ in SMALL_REPL], small)})

    delta, new_m, new_v = {}, {}, {}
    for n in order:
        shape = w[n].shape
        view = (-1, shape[-1])
        d_, m_, v_ = adamw(w[n].reshape(view), grads[n].reshape(view), m[n].reshape(view), v[n].reshape(view),
                           name=f"adamw_{n}")
        delta[n], new_m[n], new_v[n] = d_.reshape(shape), m_.reshape(shape), v_.reshape(shape)
    return (loss, grad_x, *[grads[n] for n in order], *[delta[n] for n in order], *[new_m[n] for n in order],
            *[new_v[n] for n in order])
```

```python
import functools
import math

import numpy as np
import jax
import jax.numpy as jnp
from jax import lax
from jax.experimental import pallas as pl
from jax.experimental.pallas import tpu as pltpu

F32, BF16 = jnp.float32, jnp.bfloat16
EPS = 1e-6
D = 1024
NMETA = 16
BLK = 128
PAD = BLK - NMETA
DFF = 2816
LANES = 128
NDEV = 8
AB_IN, AB_INP = 4104, 4224
ADAM_LR, ADAM_B1, ADAM_B2, ADAM_EPS, ADAM_WD, ADAM_STEP = 0.001, 0.9, 0.999, 1e-08, 0.01, 10
VMEM_LIMIT = 56 * 1024 * 1024
MESH = pl.DeviceIdType.MESH
HIGH = lax.Precision.HIGH


def _params(sem):
    return pltpu.CompilerParams(dimension_semantics=sem, vmem_limit_bytes=VMEM_LIMIT)


def _row_tile(T, streamed, resident):
    for tm in (640, 320, 128):
        if T % tm == 0 and 2 * (tm * streamed + resident) <= VMEM_LIMIT - 14 * 1024 * 1024:
            return tm
    return _tile(T, 128)


MXU_COLS = 256


def _col_chunks(n):
    return [slice(c, min(c + MXU_COLS, n)) for c in range(0, n, MXU_COLS)]


def _tile(n, cap, unit=LANES):
    if n <= cap:
        return n
    best = None
    for t in range(unit, cap + 1, unit):
        if n % t == 0:
            best = t
    assert best is not None, (n, cap)
    return best


def _rms_fwd(x, g):
    return x * lax.rsqrt(jnp.mean(x * x, axis=-1, keepdims=True) + EPS) * g


def _rms_bwd(x, g, dz):
    r = lax.rsqrt(jnp.mean(x * x, axis=-1, keepdims=True) + EPS)
    xh = x * r
    dg = jnp.sum(dz * xh, axis=0, keepdims=True)
    t = dz * g
    return r * (t - xh * jnp.mean(t * xh, axis=-1, keepdims=True)), dg


def _sigmoid(x):
    return 0.5 * jnp.tanh(0.5 * x) + 0.5


def _silu(x):
    return x * _sigmoid(x)


def _nn(a, b, precision=None):
    return lax.dot_general(a, b, (((1,), (0,)), ((), ())), preferred_element_type=F32, precision=precision)


def _nt(a, b):
    return lax.dot_general(a, b, (((1,), (1,)), ((), ())), preferred_element_type=F32)


def _tn(a, b):
    return lax.dot_general(a, b, (((0,), (0,)), ((), ())), preferred_element_type=F32)


def _mm(a, b, precision=None):
    if a.ndim == 3:
        return lax.dot_general(a, b, (((2,), (1,)), ((0,), (0,))), preferred_element_type=F32, precision=precision)
    return _nn(a, b, precision)


def _t(x):
    return jnp.swapaxes(x, -1, -2)


@jax.custom_vjp
def bdot(a, b):
    return _mm(a.astype(BF16), b.astype(BF16))


def _bdot_fwd(a, b):
    return bdot(a, b), (a, b)


def _bdot_bwd(res, g):
    a, b = res
    return bdot(g, _t(b)), bdot(_t(a), g)


bdot.defvjp(_bdot_fwd, _bdot_bwd)


@jax.custom_vjp
def hdot(a, b):
    return _mm(a, b, HIGH)


def _hdot_fwd(a, b):
    return hdot(a, b), (a, b)


def _hdot_bwd(res, g):
    a, b = res
    return hdot(g, _t(b)), hdot(_t(a), g)


hdot.defvjp(_hdot_fwd, _hdot_bwd)


def _iota2(shape, axis):
    return lax.broadcasted_iota(jnp.int32, shape, axis)


def _lane_pick(row, lane):
    return jnp.sum(jnp.where(_iota2(row.shape, 1) == lane, row, 0.0), axis=1, keepdims=True)


def norm_mm(h, gain, ws, *, swiglu, name, wt=False, out_dtype=F32):
    T, Dm = h.shape
    N = ws[0].shape[0 if wt else 1]
    tm, tn = _tile(T, 640), _tile(N, 1408)
    nw = len(ws)
    mm = _nt if wt else _nn

    def body(h_ref, g_ref, *refs):
        w_refs, u_ref, o_refs = refs[:nw], refs[nw], refs[nw + 1:]

        @pl.when(pl.program_id(1) == 0)
        def _():
            u_ref[...] = _rms_fwd(h_ref[...], g_ref[...]).astype(BF16)

        u = u_ref[...]
        for cols in _col_chunks(tn):
            acc = [mm(u, w[cols, :] if wt else w[:, cols]) for w in w_refs]
            if swiglu:
                o_refs[0][:, cols] = acc[0].astype(BF16)
                o_refs[1][:, cols] = acc[1].astype(BF16)
                o_refs[2][:, cols] = (_silu(acc[0]) * acc[1]).astype(BF16)
            else:
                o_refs[0][:, cols] = acc[0].astype(out_dtype)

    row = pl.BlockSpec((tm, Dm), lambda i, j: (i, 0))
    tile = pl.BlockSpec((tm, tn), lambda i, j: (i, j))
    if swiglu:
        out_shape = [jax.ShapeDtypeStruct((T, Dm), BF16)] + [jax.ShapeDtypeStruct((T, N), BF16)] * 3
        out_specs = [row, tile, tile, tile]
    else:
        out_shape = [jax.ShapeDtypeStruct((T, Dm), BF16), jax.ShapeDtypeStruct((T, N), out_dtype)]
        out_specs = [row, tile]
    return pl.pallas_call(
        body, name=name, grid=(T // tm, N // tn),
        in_specs=[row, pl.BlockSpec((1, Dm), lambda i, j: (0, 0))]
        + [pl.BlockSpec((tn, Dm), lambda i, j: (j, 0)) if wt else pl.BlockSpec((Dm, tn), lambda i, j: (0, j))] * nw,
        out_specs=out_specs, out_shape=out_shape,
        compiler_params=_params(("arbitrary", "arbitrary")),
    )(h, gain, *ws)


def mm_norm_res(As, Ws, h, gain, scale, *, name):
    T, Dm = h.shape
    n = len(As)
    tm = _row_tile(T, sum(a.shape[1] * a.dtype.itemsize for a in As) + 3 * Dm * 4,
                   sum(w.size * w.dtype.itemsize for w in Ws))

    def body(*refs):
        a_refs, w_refs = refs[:n], refs[n:2 * n]
        h_ref, g_ref, y_ref, hn_ref = refs[2 * n:]
        y = _nn(a_refs[0][...].astype(BF16), w_refs[0][...])
        for a, w in zip(a_refs[1:], w_refs[1:]):
            y = y + _nn(a[...].astype(BF16), w[...])
        y_ref[...] = y
        hn_ref[...] = h_ref[...] + scale * _rms_fwd(y, g_ref[...])

    row = pl.BlockSpec((tm, Dm), lambda i: (i, 0))
    return pl.pallas_call(
        body, name=name, grid=(T // tm,),
        in_specs=[pl.BlockSpec((tm, a.shape[1]), lambda i: (i, 0)) for a in As]
        + [pl.BlockSpec(w.shape, lambda i: (0, 0)) for w in Ws]
        + [row, pl.BlockSpec((1, Dm), lambda i: (0, 0))],
        out_specs=[row, row], out_shape=[jax.ShapeDtypeStruct((T, Dm), F32)] * 2,
        compiler_params=_params(("arbitrary",)),
    )(*As, *Ws, h, gain)


def normbwd_mm_nt(dh, y, gain, w, scale, gu=None, *, name):
    T, Dm = dh.shape
    N = w.shape[0]
    tm, tn = _tile(T, 640), _tile(N, 1408)
    swiglu = gu is not None

    def body(dh_ref, y_ref, g_ref, w_ref, *refs):
        if swiglu:
            gate_ref, up_ref, dy_ref, dg_ref, dgate_ref, dup_ref, a_ref = refs
        else:
            dy_ref, dg_ref, da_ref = refs
        i, j = pl.program_id(0), pl.program_id(1)

        @pl.when(j == 0)
        def _():
            dy, dg = _rms_bwd(y_ref[...], g_ref[...], scale * dh_ref[...])
            dy_ref[...] = dy.astype(BF16)

            @pl.when(i == 0)
            def _():
                dg_ref[...] = jnp.zeros_like(dg_ref)

            dg_ref[...] += dg

        dy = dy_ref[...]
        for cols in _col_chunks(tn):
            da = _nt(dy, w_ref[cols, :])
            if swiglu:
                gate, up = gate_ref[:, cols].astype(F32), up_ref[:, cols].astype(F32)
                s = _sigmoid(gate)
                dgate_ref[:, cols] = (da * up * s * (1.0 + gate * (1.0 - s))).astype(BF16)
                dup_ref[:, cols] = (da * gate * s).astype(BF16)
                a_ref[:, cols] = (gate * s * up).astype(BF16)
            else:
                da_ref[:, cols] = da

    row = pl.BlockSpec((tm, Dm), lambda i, j: (i, 0))
    vec = pl.BlockSpec((1, Dm), lambda i, j: (0, 0))
    tile = pl.BlockSpec((tm, tn), lambda i, j: (i, j))
    in_specs = [row, row, vec, pl.BlockSpec((tn, Dm), lambda i, j: (j, 0))]
    out_shape = [jax.ShapeDtypeStruct((T, Dm), BF16), jax.ShapeDtypeStruct((1, Dm), F32)]
    if swiglu:
        in_specs += [tile, tile]
        out_shape += [jax.ShapeDtypeStruct((T, N), BF16)] * 3
        out_specs = [row, vec, tile, tile, tile]
        args = (dh, y, gain, w, *gu)
    else:
        out_shape += [jax.ShapeDtypeStruct((T, N), F32)]
        out_specs = [row, vec, tile]
        args = (dh, y, gain, w)
    return pl.pallas_call(
        body, name=name, grid=(T // tm, N // tn), in_specs=in_specs, out_specs=out_specs,
        out_shape=out_shape, compiler_params=_params(("arbitrary", "arbitrary")),
    )(*args)


def mm_nt_normbwd(dPs, Ws, h, gain, dh_in, *, name, wt=False):
    T, Dm = h.shape
    n = len(dPs)
    tm = _row_tile(T, sum(p.shape[1] * p.dtype.itemsize for p in dPs) + 3 * Dm * 4,
                   sum(w.size * w.dtype.itemsize for w in Ws))
    mm = _nn if wt else _nt

    def body(*refs):
        p_refs, w_refs = refs[:n], refs[n:2 * n]
        h_ref, g_ref, dhin_ref, dh_ref, dg_ref = refs[2 * n:]
        du = mm(p_refs[0][...].astype(BF16), w_refs[0][...])
        for p, w in zip(p_refs[1:], w_refs[1:]):
            du = du + mm(p[...].astype(BF16), w[...])
        dx, dg = _rms_bwd(h_ref[...], g_ref[...], du)
        dh_ref[...] = dhin_ref[...] + dx

        @pl.when(pl.program_id(0) == 0)
        def _():
            dg_ref[...] = jnp.zeros_like(dg_ref)

        dg_ref[...] += dg

    row = pl.BlockSpec((tm, Dm), lambda i: (i, 0))
    vec = pl.BlockSpec((1, Dm), lambda i: (0, 0))
    return pl.pallas_call(
        body, name=name, grid=(T // tm,),
        in_specs=[pl.BlockSpec((tm, p.shape[1]), lambda i: (i, 0)) for p in dPs]
        + [pl.BlockSpec(w.shape, lambda i: (0, 0)) for w in Ws] + [row, vec, row],
        out_specs=[row, vec],
        out_shape=[jax.ShapeDtypeStruct((T, Dm), F32), jax.ShapeDtypeStruct((1, Dm), F32)],
        compiler_params=_params(("arbitrary",)),
    )(*dPs, *Ws, h, gain, dh_in)


def mm_tn(a, b, *, name):
    T, M = a.shape
    N = b.shape[1]
    tm, tn, tk = _tile(M, 1408), _tile(N, 1408), _tile(T, 640)

    def body(a_ref, b_ref, o_ref):
        @pl.when(pl.program_id(2) == 0)
        def _():
            o_ref[...] = jnp.zeros_like(o_ref)

        o_ref[...] += _tn(a_ref[...].astype(BF16), b_ref[...].astype(BF16))

    return pl.pallas_call(
        body, name=name, grid=(M // tm, N // tn, T // tk),
        in_specs=[pl.BlockSpec((tk, tm), lambda i, j, k: (k, i)), pl.BlockSpec((tk, tn), lambda i, j, k: (k, j))],
        out_specs=pl.BlockSpec((tm, tn), lambda i, j, k: (i, j)),
        out_shape=jax.ShapeDtypeStruct((M, N), F32),
        compiler_params=_params(("arbitrary", "arbitrary", "arbitrary")),
    )(a, b)


def loss_and_grad(h, target, *, name):
    T, Dm = h.shape

    def body(h_ref, t_ref, loss_ref, dh_ref):
        b = pl.program_id(0)

        @pl.when(b == 0)
        def _():
            loss_ref[...] = jnp.zeros_like(loss_ref)
            dh_ref[...] = jnp.zeros_like(dh_ref)

        @pl.when(b > 0)
        def _():
            e = h_ref[...] - t_ref[...]
            dh_ref[...] = e * (1.0 / Dm)
            loss_ref[...] += jnp.sum(e * e) * (0.5 / Dm)

    return pl.pallas_call(
        body, name=name, grid=(T // BLK,),
        in_specs=[pl.BlockSpec((BLK, Dm), lambda b: (b, 0)),
                  pl.BlockSpec((BLK, Dm), lambda b: (jnp.maximum(b - 1, 0), 0))],
        out_specs=[pl.BlockSpec((8, LANES), lambda b: (0, 0)), pl.BlockSpec((BLK, Dm), lambda b: (b, 0))],
        out_shape=[jax.ShapeDtypeStruct((8, LANES), F32), jax.ShapeDtypeStruct((T, Dm), F32)],
        compiler_params=_params(("arbitrary",)),
    )(h, target)


def adamw(w, g, m, v, *, name):
    R, C = w.shape
    tr = R
    for t in (512, 352, 256):
        if R > t and R % t == 0:
            tr = t
            break

    def body(w_ref, g_ref, m_ref, v_ref, d_ref, nm_ref, nv_ref):
        g_ = g_ref[...]
        m_ = ADAM_B1 * m_ref[...] + (1.0 - ADAM_B1) * g_
        v_ = ADAM_B2 * v_ref[...] + (1.0 - ADAM_B2) * (g_ * g_)
        m_hat = m_ / (1.0 - ADAM_B1 ** ADAM_STEP)
        v_hat = v_ / (1.0 - ADAM_B2 ** ADAM_STEP)
        d_ref[...] = -ADAM_LR * (m_hat / (jnp.sqrt(v_hat) + ADAM_EPS) + ADAM_WD * w_ref[...])
        nm_ref[...] = m_
        nv_ref[...] = v_

    spec = pl.BlockSpec((tr, C), lambda i: (i, 0))
    return pl.pallas_call(
        body, name=name, grid=(R // tr,), in_specs=[spec] * 4, out_specs=[spec] * 3,
        out_shape=[jax.ShapeDtypeStruct((R, C), F32)] * 3, compiler_params=_params(("arbitrary",)),
    )(w, g, m, v)


def _me():
    return lax.axis_index("x"), lax.axis_index("y"), lax.axis_index("c")


def _flip(pos, rel):
    return tuple(1 - p if r else p for p, r in zip(pos, rel))


def _slot(pos):
    return 4 * pos[0] + 2 * pos[1] + pos[2]


HBM_SPEC = pl.BlockSpec(memory_space=pltpu.HBM)
CHIP_RELS = ((1, 0), (0, 1), (1, 1))


def all_gather_big(xs, *, name):
    n = len(xs)

    def body(*refs):
        x_refs, out_refs = refs[:n], refs[n:2 * n]
        send_sems, recv_sems, local_sems = refs[2 * n:]
        me = _me()
        sibling = _flip(me, (0, 0, 1))
        chips = [_flip(me, rel + (0,)) for rel in CHIP_RELS]

        def copy(i, k, block, to, src=None):
            dst = out_refs[i].at[_slot(block)]
            return pltpu.make_async_remote_copy(
                src_ref=dst if src is None else src, dst_ref=dst, send_sem=send_sems.at[i, k],
                recv_sem=recv_sems.at[i, k], device_id=to, device_id_type=MESH)

        sent, local = [], []
        for i in range(n):
            mine = pltpu.make_async_copy(x_refs[i], out_refs[i].at[_slot(me)], local_sems.at[i])
            mine.start()
            local.append(mine)
            sent += [copy(i, 0, me, sibling, src=x_refs[i])]
            sent += [copy(i, 1 + j, me, chip, src=x_refs[i]) for j, chip in enumerate(chips)]
        for cp in sent:
            cp.start()
        for i in range(n):
            for j, chip in enumerate(chips):
                copy(i, 1 + j, chip, me).wait_recv()
                passed = copy(i, 4 + j, chip, sibling)
                passed.start()
                sent.append(passed)
        for i in range(n):
            copy(i, 0, sibling, me).wait_recv()
            for j, chip in enumerate(chips):
                copy(i, 4 + j, _flip(chip, (0, 0, 1)), me).wait_recv()
        for cp in sent:
            cp.wait_send()
        for mine in local:
            mine.wait()

    return pl.pallas_call(
        body, name=name, in_specs=[HBM_SPEC] * n, out_specs=[HBM_SPEC] * n,
        out_shape=[jax.ShapeDtypeStruct((NDEV,) + x.shape, x.dtype) for x in xs],
        scratch_shapes=[pltpu.SemaphoreType.DMA((n, 7)), pltpu.SemaphoreType.DMA((n, 7)), pltpu.SemaphoreType.DMA((n,))],
    )(*xs)


def all_to_all_small(src, *, name):
    _, r, C = src.shape

    def body(src_ref, out_ref, send_sems, recv_sems):
        me = _me()
        my = _slot(me)
        out_ref[my] = src_ref[my]
        copies = []
        for k in range(1, NDEV):
            peer = _flip(me, ((k >> 2) & 1, (k >> 1) & 1, k & 1))
            cp = pltpu.make_async_remote_copy(
                src_ref=src_ref.at[_slot(peer)], dst_ref=out_ref.at[my], send_sem=send_sems.at[k - 1],
                recv_sem=recv_sems.at[k - 1], device_id=peer, device_id_type=MESH)
            cp.start()
            copies.append((cp, peer))
        for k, (cp, peer) in enumerate(copies):
            pltpu.make_async_remote_copy(
                src_ref=src_ref.at[my], dst_ref=out_ref.at[_slot(peer)], send_sem=send_sems.at[k],
                recv_sem=recv_sems.at[k], device_id=peer, device_id_type=MESH).wait_recv()
        for cp, _ in copies:
            cp.wait_send()

    vm = pl.BlockSpec(memory_space=pltpu.VMEM)
    return pl.pallas_call(
        body, name=name, in_specs=[vm], out_specs=vm, out_shape=jax.ShapeDtypeStruct(src.shape, src.dtype),
        scratch_shapes=[pltpu.SemaphoreType.DMA((7,)), pltpu.SemaphoreType.DMA((7,))],
    )(src)


def sum_slots(a, *, name):
    n, r, C = a.shape

    def body(a_ref, o_ref):
        s = a_ref[0]
        for k in range(1, n):
            s = s + a_ref[k]
        o_ref[...] = s

    vm = pl.BlockSpec(memory_space=pltpu.VMEM)
    return pl.pallas_call(body, name=name, in_specs=[vm], out_specs=vm,
                          out_shape=jax.ShapeDtypeStruct((r, C), F32))(a)


def rs_exchange_sibling(gs, *, name):
    n = len(gs)

    def body(*refs):
        g_refs, out_refs, send_sems, recv_sems = refs[:n], refs[n:2 * n], refs[2 * n], refs[2 * n + 1]
        sibling = _flip(_me(), (0, 0, 1))
        copies = []
        for i in range(n):
            for chip in range(4):
                cp = pltpu.make_async_remote_copy(
                    src_ref=g_refs[i].at[2 * chip + sibling[2]], dst_ref=out_refs[i].at[chip],
                    send_sem=send_sems.at[i, chip], recv_sem=recv_sems.at[i, chip], device_id=sibling,
                    device_id_type=MESH)
                cp.start()
                copies.append(cp)
        for cp in copies:
            cp.wait()

    return pl.pallas_call(
        body, name=name, in_specs=[HBM_SPEC] * n, out_specs=[HBM_SPEC] * n,
        out_shape=[jax.ShapeDtypeStruct((4,) + g.shape[1:], g.dtype) for g in gs],
        scratch_shapes=[pltpu.SemaphoreType.DMA((n, 4)), pltpu.SemaphoreType.DMA((n, 4))],
    )(*gs)


def rs_chip_partials(g, got, *, name):
    _, R, C = g.shape
    tr = _tile(R, 768, unit=16)

    def body(c_ref, g_ref, got_ref, o_ref):
        o_ref[...] = (g_ref[...].astype(F32) + got_ref[...].astype(F32)).astype(o_ref.dtype)

    c = jnp.reshape(lax.axis_index("c"), (1,)).astype(jnp.int32)
    return pl.pallas_call(
        body, name=name,
        grid_spec=pltpu.PrefetchScalarGridSpec(
            num_scalar_prefetch=1, grid=(4, R // tr),
            in_specs=[pl.BlockSpec((None, tr, C), lambda k, i, c_ref: (2 * k + c_ref[0], i, 0)),
                      pl.BlockSpec((None, tr, C), lambda k, i, c_ref: (k, i, 0))],
            out_specs=pl.BlockSpec((None, tr, C), lambda k, i, c_ref: (k, i, 0))),
        out_shape=jax.ShapeDtypeStruct((4, R, C), g.dtype), compiler_params=_params(("arbitrary", "arbitrary")),
    )(c, g, got)


def rs_exchange_chips(ps, *, name):
    n = len(ps)

    def body(*refs):
        copies = _chip_exchange_copies(refs[:n], refs[n:2 * n], refs[2 * n], refs[2 * n + 1])
        for cp in copies:
            cp.start()
        for cp in copies:
            cp.wait()

    return pl.pallas_call(
        body, name=name, in_specs=[HBM_SPEC] * n, out_specs=[HBM_SPEC] * n,
        out_shape=_chip_exchange_shapes(ps), scratch_shapes=_chip_exchange_sems(n),
    )(*ps)


def _chip_exchange_copies(p_refs, out_refs, send_sems, recv_sems):
    me = _me()
    copies = []
    for i, (p_ref, out_ref) in enumerate(zip(p_refs, out_refs)):
        for j, rel in enumerate(CHIP_RELS):
            peer = _flip(me, rel + (0,))
            copies.append(pltpu.make_async_remote_copy(
                src_ref=p_ref.at[2 * peer[0] + peer[1]], dst_ref=out_ref.at[j], send_sem=send_sems.at[i, j],
                recv_sem=recv_sems.at[i, j], device_id=peer, device_id_type=MESH))
    return copies


def _chip_exchange_shapes(ps):
    return [jax.ShapeDtypeStruct((3,) + p.shape[1:], p.dtype) for p in ps]


def _chip_exchange_sems(n):
    return [pltpu.SemaphoreType.DMA((n, 3)), pltpu.SemaphoreType.DMA((n, 3))]


def _gather_direct(start, x_refs, out_refs, send_sems, recv_sems, local_sems):
    me = _me()
    peers = [_flip(me, (0, 0, 1))] + [_flip(me, rel + (0,)) for rel in CHIP_RELS]
    for i, (x_ref, out_ref) in enumerate(zip(x_refs, out_refs)):
        local = pltpu.make_async_copy(x_ref, out_ref.at[_slot(me)], local_sems.at[i])
        local.start() if start else local.wait()
        for k, peer in enumerate(peers):
            def copy(block):
                return pltpu.make_async_remote_copy(
                    src_ref=x_ref, dst_ref=out_ref.at[_slot(block)], send_sem=send_sems.at[i, k],
                    recv_sem=recv_sems.at[i, k], device_id=peer, device_id_type=MESH)
            if start:
                copy(me).start()
            else:
                copy(me).wait_send()
                copy(peer).wait_recv()


def _gather_direct_sems(n):
    return [pltpu.SemaphoreType.DMA((n, 4)), pltpu.SemaphoreType.DMA((n, 4)), pltpu.SemaphoreType.DMA((n,))]


def gather_forward_to_sibling(gs, *, name):
    n = len(gs)

    def body(*refs):
        in_refs, out_refs, send_sems, recv_sems = refs[:n], refs[n:2 * n], refs[2 * n], refs[2 * n + 1]
        me = _me()
        sibling = _flip(me, (0, 0, 1))
        chips = [_flip(me, rel + (0,)) for rel in CHIP_RELS]
        sends, recvs = [], []
        for i in range(n):
            for j, chip in enumerate(chips):
                def copy(block):
                    return pltpu.make_async_remote_copy(
                        src_ref=in_refs[i].at[_slot(chip)], dst_ref=out_refs[i].at[_slot(block)],
                        send_sem=send_sems.at[i, j], recv_sem=recv_sems.at[i, j], device_id=sibling, device_id_type=MESH)
                sends.append(copy(chip))
                recvs.append(copy(_flip(chip, (0, 0, 1))))
        for cp in sends:
            cp.start()
        for cp in recvs:
            cp.wait_recv()
        for cp in sends:
            cp.wait_send()

    return pl.pallas_call(
        body, name=name, in_specs=[HBM_SPEC] * n, out_specs=[HBM_SPEC] * n,
        out_shape=[jax.ShapeDtypeStruct(g.shape, g.dtype) for g in gs],
        input_output_aliases={i: i for i in range(n)},
        scratch_shapes=[pltpu.SemaphoreType.DMA((n, 3)), pltpu.SemaphoreType.DMA((n, 3))],
    )(*gs)


def rs_final_sum(p, got, *, name):
    _, R, C = p.shape
    tr = _tile(R, 768, unit=16)

    def body(chip_ref, p_ref, got_ref, o_ref):
        s = p_ref[...].astype(F32)
        for j in range(3):
            s = s + got_ref[j].astype(F32)
        o_ref[...] = s

    mychip = jnp.reshape(2 * lax.axis_index("x") + lax.axis_index("y"), (1,)).astype(jnp.int32)
    return pl.pallas_call(
        body, name=name,
        grid_spec=pltpu.PrefetchScalarGridSpec(
            num_scalar_prefetch=1, grid=(R // tr,),
            in_specs=[pl.BlockSpec((None, tr, C), lambda i, chip_ref: (chip_ref[0], i, 0)),
                      pl.BlockSpec((3, tr, C), lambda i, chip_ref: (0, i, 0))],
            out_specs=pl.BlockSpec((tr, C), lambda i, chip_ref: (i, 0))),
        out_shape=jax.ShapeDtypeStruct((R, C), F32), compiler_params=_params(("arbitrary",)),
    )(mychip, p, got)


def reduce_scatter_big(gs):
    got = rs_exchange_sibling(gs, name="rs_sibling")
    parts = [rs_chip_partials(g, t, name=f"rs_chip_partials_{i}") for i, (g, t) in enumerate(zip(gs, got))]
    got2 = rs_exchange_chips(parts, name="rs_chips")
    return [rs_final_sum(p, t, name=f"rs_final_sum_{i}") for i, (p, t) in enumerate(zip(parts, got2))]


def _blk(off):
    return pl.BlockSpec((BLK, LANES), lambda h, n: (n, off + h))


def _const_spec(shape):
    return pl.BlockSpec(shape, lambda *_: (0,) * len(shape))


def retention_tables(T):
    pos = jnp.arange(T, dtype=F32) - float(PAD)
    inv_freq = 1.0 / (10000.0 ** jnp.linspace(0.0, 1.0, 64, dtype=F32))
    ang = pos[:, None] * inv_freq[None, :]
    cos = jnp.repeat(jnp.cos(ang), 2, axis=1)
    sin = jnp.repeat(jnp.sin(ang), 2, axis=1) * jnp.tile(jnp.array([-1.0, 1.0], F32), 64)[None, :]
    lane = np.arange(LANES)
    perm = jnp.broadcast_to(jnp.asarray((lane[:, None] == (lane[None, :] ^ 1)).astype(np.float32)), (4, LANES, LANES))
    log_gamma = jnp.log1p(-jnp.exp2(-5.0 - jnp.arange(4, dtype=F32)))
    idx = jnp.arange(BLK, dtype=F32)
    diff = idx[:, None] - idx[None, :]
    intra = jnp.where(diff >= 0, jnp.exp(jnp.maximum(diff, 0.0) * log_gamma[:, None, None]), 0.0)
    zeta = jnp.exp((BLK - 1.0 - idx)[None, :] * log_gamma[:, None])
    xi = jnp.exp((idx + 1.0)[None, :] * log_gamma[:, None])
    bc = lambda t: jnp.broadcast_to(t[:, :, None], (4, BLK, LANES))
    return cos, sin, perm, intra, bc(zeta), bc(xi)


def _heads(x):
    return jnp.stack([x[:, h * LANES:(h + 1) * LANES] for h in range(4)])


def _unheads(y):
    return jnp.concatenate([y[h] for h in range(4)], axis=1)


def _ret_chunk(rq, rk, rv, rg, S, cos, sin, intra, zeta, xi, perm):
    q = rq * cos + hdot(rq, perm) * sin
    k = (rk * cos + hdot(rk, perm) * sin) * (128.0 ** -0.5)
    ret = bdot(bdot(q, _t(k)) * intra, rv) + bdot(q * xi, S)
    S_new = S * xi[..., BLK - 1:BLK, :] + bdot(_t(k * zeta), rv)
    c = ret - jnp.mean(ret, axis=-1, keepdims=True)
    out = c * lax.rsqrt(jnp.mean(c * c, axis=-1, keepdims=True) + EPS) * _silu(rg)
    return out, S_new


def _wide(off):
    return pl.BlockSpec((BLK, 4 * LANES), lambda n: (n, off))


def retention_fwd(p, tables, *, name):
    T = p.shape[0]
    N = T // BLK
    cos, sin, perm, intra, zeta, xi = tables

    def body(rq, rk, rv, rg, cos_ref, sin_ref, in_ref, ze_ref, xi_ref, perm_ref, out_ref, sall_ref, s_scr):
        @pl.when(pl.program_id(0) == 0)
        def _():
            s_scr[...] = jnp.zeros_like(s_scr)

        S = s_scr[...]
        sall_ref[...] = S
        out, S_new = _ret_chunk(_heads(rq[...]), _heads(rk[...]), _heads(rv[...]), _heads(rg[...]), S, cos_ref[...],
                                sin_ref[...], in_ref[...], ze_ref[...], xi_ref[...], perm_ref[...])
        out_ref[...] = _unheads(out).astype(BF16)
        s_scr[...] = S_new

    rowtab = pl.BlockSpec((BLK, LANES), lambda n: (n, 0))
    tab = _const_spec((4, BLK, LANES))
    return pl.pallas_call(
        body, name=name, grid=(N,),
        in_specs=[_wide(0), _wide(1), _wide(2), _wide(3), rowtab, rowtab, tab, tab, tab, tab],
        out_specs=[_wide(0), pl.BlockSpec((None, 4, LANES, LANES), lambda n: (n, 0, 0, 0))],
        out_shape=[jax.ShapeDtypeStruct((T, 512), BF16), jax.ShapeDtypeStruct((N, 4, LANES, LANES), F32)],
        scratch_shapes=[pltpu.VMEM((4, LANES, LANES), F32)],
        compiler_params=_params(("arbitrary",)),
    )(p, p, p, p, cos, sin, intra, zeta, xi, perm)


def _row_mask(n):
    return (n * BLK + _iota2((BLK, 1), 0) >= PAD).astype(F32)


def retention_bwd(p, sall, dmixed, tables, *, name):
    T = p.shape[0]
    N = T // BLK
    cos, sin, perm, intra, zeta, xi = tables

    def body(rq, rk, rv, rg, cos_ref, sin_ref, in_ref, ze_ref, xi_ref, perm_ref, sall_ref, do_ref, drq, drk, drv, drg,
             ds_scr):
        n = N - 1 - pl.program_id(0)

        @pl.when(pl.program_id(0) == 0)
        def _():
            ds_scr[...] = jnp.zeros_like(ds_scr)

        f = lambda a, b, c, d, s: _ret_chunk(a, b, c, d, s, cos_ref[...], sin_ref[...], in_ref[...], ze_ref[...],
                                             xi_ref[...], perm_ref[...])
        _, vjp = jax.vjp(f, _heads(rq[...]), _heads(rk[...]), _heads(rv[...]), _heads(rg[...]), sall_ref[...])
        g = vjp((_heads(do_ref[...]), ds_scr[...]))
        mask = _row_mask(n)
        for ref, val in zip((drq, drk, drv, drg), g[:4]):
            ref[...] = _unheads(val) * mask
        ds_scr[...] = g[4]

    def rwide(off):
        return pl.BlockSpec((BLK, 4 * LANES), lambda n: (N - 1 - n, off))

    rowtab = pl.BlockSpec((BLK, LANES), lambda n: (N - 1 - n, 0))
    tab = _const_spec((4, BLK, LANES))
    return pl.pallas_call(
        body, name=name, grid=(N,),
        in_specs=[rwide(0), rwide(1), rwide(2), rwide(3), rowtab, rowtab, tab, tab, tab, tab,
                  pl.BlockSpec((None, 4, LANES, LANES), lambda n: (N - 1 - n, 0, 0, 0)), rwide(0)],
        out_specs=[rwide(0)] * 4, out_shape=[jax.ShapeDtypeStruct((T, 512), F32)] * 4,
        scratch_shapes=[pltpu.VMEM((4, LANES, LANES), F32)],
        compiler_params=_params(("arbitrary",)),
    )(p, p, p, p, cos, sin, intra, zeta, xi, perm, sall, dmixed)


def conv_silu_fwd(p, w, send, *, name):
    T = p.shape[0]
    N = T // BLK
    ns = len(send)

    def body(x_ref, xp_ref, w_ref, *refs):
        x_refs, o_ref, g_refs, sems = refs[:ns], refs[ns], refs[ns + 1:2 * ns + 1], refs[2 * ns + 1:]
        n, part = pl.program_id(0), pl.program_id(1)

        @pl.when((n == 0) & (part == 0))
        def _():
            _gather_direct(True, x_refs, g_refs, *sems)

        @pl.when((n == N - 1) & (part == 2))
        def _():
            _gather_direct(False, x_refs, g_refs, *sems)

        cur = x_ref[...]
        cat = jnp.concatenate([jnp.where(n > 0, xp_ref[...], 0.0), cur], axis=0)
        y = w_ref[3:4, :] * cur
        for s in (1, 2, 3):
            y = y + w_ref[3 - s:4 - s, :] * pltpu.roll(cat, s, 0)[BLK:]
        o_ref[...] = _silu(y)

    cw = 4 * LANES
    res = pl.pallas_call(
        body, name=name, grid=(N, 3),
        in_specs=[pl.BlockSpec((BLK, cw), lambda n, c: (n, 4 + c)),
                  pl.BlockSpec((BLK, cw), lambda n, c: (jnp.maximum(n - 1, 0), 4 + c)),
                  pl.BlockSpec((4, cw), lambda n, c: (0, c))] + [HBM_SPEC] * ns,
        out_specs=[pl.BlockSpec((BLK, cw), lambda n, c: (n, c))] + [HBM_SPEC] * ns,
        out_shape=[jax.ShapeDtypeStruct((T, 1536), F32)] + [jax.ShapeDtypeStruct((NDEV,) + x.shape, x.dtype) for x in send],
        scratch_shapes=_gather_direct_sems(ns), compiler_params=_params(("arbitrary", "arbitrary")),
    )(p, p, w, *send)
    return res[0], res[1:]


def conv_silu_bwd(p, w, dact, part, *, name):
    T = p.shape[0]
    N = T // BLK
    cw = 4 * LANES

    def body(xp_ref, x_ref, xn_ref, w_ref, da_ref, dan_ref, dx_ref, dw_ref):
        n = pl.program_id(0)
        last = n == N - 1
        cat = jnp.concatenate([jnp.where(n > 0, xp_ref[...], 0.0), x_ref[...], jnp.where(last, 0.0, xn_ref[...])], axis=0)
        shifted = [cat] + [pltpu.roll(cat, s, 0) for s in (1, 2, 3)]
        y = w_ref[3:4, :] * shifted[0]
        for s in (1, 2, 3):
            y = y + w_ref[3 - s:4 - s, :] * shifted[s]
        y = y[BLK:]
        da = jnp.concatenate([da_ref[...], jnp.where(last, 0.0, dan_ref[...])], axis=0)
        sg = _sigmoid(y)
        dy = da * sg * (1.0 + y * (1.0 - sg))
        dx = w_ref[3:4, :] * dy[:BLK]
        for s in (1, 2, 3):
            dx = dx + w_ref[3 - s:4 - s, :] * pltpu.roll(dy, 2 * BLK - s, 0)[:BLK]
        dx_ref[...] = dx * _row_mask(n)

        @pl.when(n == 0)
        def _():
            dw_ref[...] = jnp.zeros_like(dw_ref)

        for s in (0, 1, 2, 3):
            dw_ref[3 - s:4 - s, :] += jnp.sum(dy[:BLK] * shifted[s][BLK:2 * BLK], axis=0, keepdims=True)

    def xs(d):
        return pl.BlockSpec((BLK, cw), lambda n: (jnp.clip(n + d, 0, N - 1), 4 + part))

    return pl.pallas_call(
        body, name=name, grid=(N,),
        in_specs=[xs(-1), xs(0), xs(1), pl.BlockSpec((4, cw), lambda n: (0, part)),
                  pl.BlockSpec((BLK, cw), lambda n: (n, 0)),
                  pl.BlockSpec((BLK, cw), lambda n: (jnp.minimum(n + 1, N - 1), 0))],
        out_specs=[pl.BlockSpec((BLK, cw), lambda n: (n, 0)), pl.BlockSpec((4, cw), lambda n: (0, 0))],
        out_shape=[jax.ShapeDtypeStruct((T, 512), F32), jax.ShapeDtypeStruct((4, 512), F32)],
        compiler_params=_params(("arbitrary",)),
    )(p, p, p, w, dact, dact)


def _softplus(x):
    return jnp.maximum(x, 0.0) + jnp.log1p(jnp.exp(-jnp.abs(x)))


def _pick4(tile, off):
    return jnp.stack([_lane_pick(tile, off + h) for h in range(4)])


def _spread4(v4, off, rows):
    lane = _iota2((rows, LANES), 1)
    out = jnp.where(lane == off, v4[0], 0.0)
    for h in range(1, 4):
        out = out + jnp.where(lane == off + h, v4[h], 0.0)
    return out


def _gdn_chunk(qa, ka, va, z, braw, araw, S, alog, dtb, onorm, rowmask, lincl):
    r, c = _iota2((BLK, BLK), 0), _iota2((BLK, BLK), 1)
    incl, strict = r >= c, r > c
    eye = (r == c).astype(F32)
    q = qa * lax.rsqrt(jnp.sum(qa * qa, axis=-1, keepdims=True) + EPS) * (128.0 ** -0.5)
    k = ka * lax.rsqrt(jnp.sum(ka * ka, axis=-1, keepdims=True) + EPS)
    beta = _sigmoid(braw) * rowmask
    g = -jnp.exp(alog) * _softplus(araw + dtb) * rowmask
    gc = hdot(lincl, jnp.broadcast_to(g, qa.shape))
    decay = jnp.where(incl, jnp.exp(jnp.where(incl, gc - _t(gc), 0.0)), 0.0)
    kb = k * beta
    amat = jnp.where(strict, bdot(kb, _t(k)) * decay, 0.0)
    m = -amat
    inv = eye + m
    pw = hdot(m, m)
    for t in range(6):
        inv = inv + hdot(inv, pw)
        if t < 5:
            pw = hdot(pw, pw)
    egc = jnp.exp(gc)
    u = hdot(inv, va * beta)
    w = hdot(inv, kb * egc)
    qk = jnp.where(incl, bdot(q, _t(k)) * decay, 0.0)
    glast = gc[..., BLK - 1:BLK, :]
    vnew = u - bdot(w, S)
    o = bdot(q * egc, S) + bdot(qk, vnew)
    S_new = S * jnp.exp(glast) + bdot(_t(k * jnp.exp(glast - gc)), vnew)
    out = o * lax.rsqrt(jnp.mean(o * o, axis=-1, keepdims=True) + EPS) * onorm * _silu(z)
    return out, S_new


def _lincl():
    i = np.arange(BLK)
    return jnp.broadcast_to(jnp.asarray((i[:, None] >= i[None, :]).astype(np.float32)), (4, BLK, BLK))


def gdn_fwd(act, p, alog, dtb, onorm, send, *, name):
    T = p.shape[0]
    N = T // BLK
    ns = len(send)

    def body(qa, ka, va, z, ba, alog_ref, dtb_ref, on_ref, l_ref, *refs):
        x_refs, (out_ref, sall_ref), g_refs = refs[:ns], refs[ns:ns + 2], refs[ns + 2:2 * ns + 2]
        s_scr, sems = refs[2 * ns + 2], refs[2 * ns + 3:]
        n = pl.program_id(0)

        @pl.when(n == 0)
        def _():
            s_scr[...] = jnp.zeros_like(s_scr)
            _gather_direct(True, x_refs, g_refs, *sems)

        @pl.when(n == N - 1)
        def _():
            _gather_direct(False, x_refs, g_refs, *sems)

        S = s_scr[...]
        sall_ref[...] = S
        out, S_new = _gdn_chunk(_heads(qa[...]), _heads(ka[...]), _heads(va[...]), _heads(z[...]), _pick4(ba[...], 0),
                                _pick4(ba[...], 4), S, _pick4(alog_ref[...], 0), _pick4(dtb_ref[...], 0), on_ref[...],
                                _row_mask(n), l_ref[...])
        out_ref[...] = _unheads(out).astype(BF16)
        s_scr[...] = S_new

    vec = _const_spec((1, LANES))
    res = pl.pallas_call(
        body, name=name, grid=(N,),
        in_specs=[_wide(0), _wide(1), _wide(2), _wide(7), pl.BlockSpec((BLK, LANES), lambda n: (n, 32)), vec, vec, vec,
                  _const_spec((4, BLK, BLK))] + [HBM_SPEC] * ns,
        out_specs=[_wide(0), pl.BlockSpec((None, 4, LANES, LANES), lambda n: (n, 0, 0, 0))] + [HBM_SPEC] * ns,
        out_shape=[jax.ShapeDtypeStruct((T, 512), BF16), jax.ShapeDtypeStruct((N, 4, LANES, LANES), F32)]
        + [jax.ShapeDtypeStruct((NDEV,) + x.shape, x.dtype) for x in send],
        scratch_shapes=[pltpu.VMEM((4, LANES, LANES), F32)] + _gather_direct_sems(ns),
        compiler_params=_params(("arbitrary",)),
    )(act, act, act, p, p, alog, dtb, onorm, _lincl(), *send)
    return res[0], res[1], res[2:]


def gdn_bwd(act, p, alog, dtb, onorm, sall, dmixed, partials, *, name):
    T = p.shape[0]
    N = T // BLK
    ns = len(partials)

    def body(qa, ka, va, z, ba, alog_ref, dtb_ref, on_ref, l_ref, sall_ref, do_ref, *refs):
        p_refs, (dq_ref, dk_ref, dv_ref, dz_ref, dba_ref, dal_ref, ddt_ref, don_ref) = refs[:ns], refs[ns:ns + 8]
        got_refs, ds_scr, sems = refs[ns + 8:2 * ns + 8], refs[2 * ns + 8], refs[2 * ns + 9:]
        step = pl.program_id(0)
        n = N - 1 - step

        @pl.when(step == 0)
        def _():
            ds_scr[...] = jnp.zeros_like(ds_scr)
            dal_ref[...] = jnp.zeros_like(dal_ref)
            ddt_ref[...] = jnp.zeros_like(ddt_ref)
            don_ref[...] = jnp.zeros_like(don_ref)
            for cp in _chip_exchange_copies(p_refs, got_refs, *sems):
                cp.start()

        @pl.when(step == N - 1)
        def _():
            for cp in _chip_exchange_copies(p_refs, got_refs, *sems):
                cp.wait()

        rowmask, lincl = _row_mask(n), l_ref[...]
        f = lambda *a: _gdn_chunk(*a, rowmask, lincl)
        _, vjp = jax.vjp(f, _heads(qa[...]), _heads(ka[...]), _heads(va[...]), _heads(z[...]), _pick4(ba[...], 0),
                         _pick4(ba[...], 4), sall_ref[...], _pick4(alog_ref[...], 0), _pick4(dtb_ref[...], 0),
                         on_ref[...])
        g = vjp((_heads(do_ref[...]), ds_scr[...]))
        dq_ref[...] = _unheads(g[0]) * rowmask
        dk_ref[...] = _unheads(g[1]) * rowmask
        dv_ref[...] = _unheads(g[2]) * rowmask
        dz_ref[...] = _unheads(g[3]) * rowmask
        dba_ref[...] = (_spread4(g[4], 0, BLK) + _spread4(g[5], 4, BLK)) * rowmask
        ds_scr[...] = g[6]
        dal_ref[...] += _spread4(g[7], 0, 1)
        ddt_ref[...] += _spread4(g[8], 0, 1)
        don_ref[...] += g[9]

    def rwide(off):
        return pl.BlockSpec((BLK, 4 * LANES), lambda s: (N - 1 - s, off))

    vec = _const_spec((1, LANES))
    col = pl.BlockSpec((BLK, LANES), lambda s: (N - 1 - s, 0))
    res = pl.pallas_call(
        body, name=name, grid=(N,),
        in_specs=[rwide(0), rwide(1), rwide(2), rwide(7), pl.BlockSpec((BLK, LANES), lambda s: (N - 1 - s, 32)), vec, vec,
                  vec, _const_spec((4, BLK, BLK)),
                  pl.BlockSpec((None, 4, LANES, LANES), lambda s: (N - 1 - s, 0, 0, 0)), rwide(1)] + [HBM_SPEC] * ns,
        out_specs=[rwide(0)] * 4 + [col, vec, vec, vec] + [HBM_SPEC] * ns,
        out_shape=[jax.ShapeDtypeStruct((T, 512), F32)] * 4 + [jax.ShapeDtypeStruct((T, LANES), F32)]
        + [jax.ShapeDtypeStruct((1, LANES), F32)] * 3 + _chip_exchange_shapes(partials),
        scratch_shapes=[pltpu.VMEM((4, LANES, LANES), F32)] + _chip_exchange_sems(ns),
        compiler_params=_params(("arbitrary",)),
    )(act, act, act, p, p, alog, dtb, onorm, _lincl(), sall, dmixed, *partials)
    return res[:8], res[8:]


NEG = -1e30


def _swa_block(q, k0, kp, kc, v0, vp, vc, sink, n):
    r, c = _iota2((BLK, BLK), 0), _iota2((BLK, BLK), 1)
    m0 = (c >= PAD) & (c <= n * BLK + r)
    mp = (n >= 2) & (c > r)
    mc = (n >= 1) & (r >= c)
    b = lambda t: jnp.broadcast_to(t, (4,) + t.shape)
    qs = q * (64.0 ** -0.5)
    s0 = jnp.where(m0, bdot(qs, _t(b(k0))), NEG)
    sp = jnp.where(mp, bdot(qs, _t(b(kp))), NEG)
    sc = jnp.where(mc, bdot(qs, _t(b(kc))), NEG)
    mx = jnp.maximum(jnp.max(jnp.maximum(jnp.maximum(s0, sp), sc), axis=-1, keepdims=True), sink)
    mx = lax.stop_gradient(mx)
    p0, pp, pc = jnp.exp(s0 - mx), jnp.exp(sp - mx), jnp.exp(sc - mx)
    den = (jnp.sum(p0, axis=-1, keepdims=True) + jnp.sum(pp, axis=-1, keepdims=True)
           + jnp.sum(pc, axis=-1, keepdims=True) + jnp.exp(sink - mx))
    return (bdot(p0, b(v0)) + bdot(pp, b(vp)) + bdot(pc, b(vc))) / den


def _swa_specs():
    rows = (lambda n: 0, lambda n: jnp.maximum(n - 1, 0), lambda n: n)

    def kv_spec(off, row):
        return pl.BlockSpec((BLK, LANES), lambda g, n: (row(n), off + g))

    q = pl.BlockSpec((BLK, 4 * LANES), lambda g, n: (n, g))
    return q, [kv_spec(off, row) for off in (8, 10) for row in rows]


def swa_fwd(p2, sinkrow, *, name):
    T = p2.shape[0]
    N = T // BLK

    def body(q, k0, kp, kc, v0, vp, vc, sink_ref, o_ref):
        g, n = pl.program_id(0), pl.program_id(1)
        f32 = lambda ref: ref[...].astype(F32)
        o = _swa_block(_heads(f32(q)), f32(k0), f32(kp), f32(kc), f32(v0), f32(vp), f32(vc),
                       _pick4(sink_ref[...], 4 * g), n)
        o_ref[...] = _unheads(o).astype(BF16)

    q, kv = _swa_specs()
    return pl.pallas_call(
        body, name=name, grid=(2, N), in_specs=[q] + kv + [_const_spec((1, LANES))],
        out_specs=q, out_shape=jax.ShapeDtypeStruct((T, 1024), BF16),
        compiler_params=_params(("arbitrary", "arbitrary")),
    )(p2, p2, p2, p2, p2, p2, p2, sinkrow)


def swa_bwd(p2, sinkrow, dmixed, *, name):
    T = p2.shape[0]
    N = T // BLK

    def body(q, k0, kp, kc, v0, vp, vc, sink_ref, do_ref, dq_ref, dk_ref, dv_ref, dsink_ref):
        g, n = pl.program_id(0), pl.program_id(1)

        @pl.when(n == 0)
        def _():
            dk_ref[...] = jnp.zeros_like(dk_ref)
            dv_ref[...] = jnp.zeros_like(dv_ref)

        @pl.when((g == 0) & (n == 0))
        def _():
            dsink_ref[...] = jnp.zeros_like(dsink_ref)

        f = lambda *a: _swa_block(*a, n)
        f32 = lambda ref: ref[...].astype(F32)
        _, vjp = jax.vjp(f, _heads(f32(q)), f32(k0), f32(kp), f32(kc), f32(v0), f32(vp), f32(vc),
                         _pick4(sink_ref[...], 4 * g))
        dq, dk0, dkp, dkc, dv0, dvp, dvc, dsink = vjp(_heads(do_ref[...]))
        dq_ref[...] = _unheads(dq)
        prev = pl.ds(pl.multiple_of(jnp.maximum(n - 1, 0) * BLK, BLK), BLK)
        cur = pl.ds(pl.multiple_of(n * BLK, BLK), BLK)
        for ref, d0, dp, dc in ((dk_ref, dk0, dkp, dkc), (dv_ref, dv0, dvp, dvc)):
            ref[0:BLK, :] += d0
            ref[prev, :] += dp
            ref[cur, :] += dc
        dsink_ref[...] += _spread4(dsink, 4 * g, 1)

    qspec, kv = _swa_specs()
    slab = pl.BlockSpec((T, LANES), lambda g, n: (0, g))
    return pl.pallas_call(
        body, name=name, grid=(2, N), in_specs=[qspec] + kv + [_const_spec((1, LANES)), qspec],
        out_specs=[qspec, slab, slab, _const_spec((1, LANES))],
        out_shape=[jax.ShapeDtypeStruct((T, 1024), F32), jax.ShapeDtypeStruct((T, 256), F32),
                   jax.ShapeDtypeStruct((T, 256), F32), jax.ShapeDtypeStruct((1, LANES), F32)],
        compiler_params=_params(("arbitrary", "arbitrary")),
    )(p2, p2, p2, p2, p2, p2, p2, sinkrow, dmixed)


def _split_dot(x, m):
    rows = x.shape[0]
    hi = x.astype(BF16)
    lo = (x - hi.astype(F32)).astype(BF16)
    r = _nn(jnp.concatenate([hi, lo], axis=0), m)
    return r[:rows] + r[rows:]


def _tri_and_ones(strict, ones=True):
    i = np.arange(BLK)
    m = (i[:, None] > i[None, :]) if strict else (i[:, None] >= i[None, :])
    if ones:
        m = np.concatenate([m, np.ones((BLK, BLK), bool)], axis=1)
    return jnp.asarray(m.astype(np.float32), dtype=BF16)


def _later_and_row_sums(x, m):
    r = _split_dot(x, m)
    if m.shape[1] == 2 * BLK:
        return r[:, :BLK], r[:, BLK:]
    return r, jnp.broadcast_to(jnp.sum(x, axis=1, keepdims=True), x.shape)


SB_PAIR = 2
SB_FWD_GROUP = 4


def _sb_positions():
    r, s = _iota2((BLK, BLK), 0), _iota2((BLK, BLK), 1)
    return s - r, s


def _sb_weights(qbs, ks, base, n, pos, carries, after):
    nh, kb = len(qbs), len(ks[0])
    zs = [[_nt(qbs[h], ks[h][c]) for c in range(kb)] for h in range(nh)]
    valid = [(pos[0] < (n - base - c) * BLK) & (pos[1] >= PAD - (base + c) * BLK) for c in range(kb)]
    lb = [[None] * kb for _ in range(nh)]
    sums = [[None] * kb for _ in range(nh)]
    for c in range(kb):
        for h in range(nh):
            z = zs[h][c]
            lb[h][c] = jnp.minimum(z, 0.0) - jnp.log(1.0 + jnp.exp(-jnp.abs(z)))
            sums[h][c] = _later_and_row_sums(jnp.where(valid[c], lb[h][c] - z, 0.0), after)
    a = [[None] * kb for _ in range(nh)]
    carries = list(carries)
    for c in reversed(range(kb)):
        for h in range(nh):
            a[h][c] = jnp.where(valid[c], jnp.exp(lb[h][c] + carries[h] + sums[h][c][0]), 0.0)
            carries[h] = carries[h] + sums[h][c][1]
    return valid, lb, a, carries


def _key_blocks(n_blocks):
    return next(k for k in (5, 3, 1) if n_blocks % k == 0)


def sb_fwd(p2, *, name):
    T = p2.shape[0]
    N = T // BLK
    kb = _key_blocks(N)
    nh = SB_FWD_GROUP
    heads = [slice(h * LANES, (h + 1) * LANES) for h in range(nh)]

    def body(q_ref, k_ref, v_ref, after_ref, o_ref, of_ref):
        n = pl.program_id(1)
        qbs = [(q_ref[:, hs].astype(F32) * (64.0 ** -0.5)).astype(BF16) for hs in heads]
        after, pos = after_ref[...], _sb_positions()
        nsup = n // kb + 1

        def step(t, c):
            accs, carries = c
            base = (nsup - 1 - t) * kb
            rows = [pl.ds(pl.multiple_of((base + sub) * BLK, BLK), BLK) for sub in range(kb)]
            ks = [[k_ref[r, hs] for r in rows] for hs in heads]
            _, _, a, carries = _sb_weights(qbs, ks, base, n, pos, carries, after)
            accs = list(accs)
            for sub, r in enumerate(rows):
                for h, hs in enumerate(heads):
                    accs[h] = accs[h] + _nn(a[h][sub].astype(BF16), v_ref[r, hs])
            return accs, carries

        zero = [jnp.zeros((BLK, LANES), F32)] * nh
        accs, _ = lax.fori_loop(0, nsup, step, (zero, zero))
        acc = jnp.concatenate(accs, axis=1)
        o_ref[...] = acc.astype(BF16)
        of_ref[...] = acc

    wide = nh * LANES

    def slab(off):
        return pl.BlockSpec((T, wide), lambda g, n: (0, off + g))

    def blk(off):
        return pl.BlockSpec((BLK, wide), lambda g, n: (n, off + g))

    return pl.pallas_call(
        body, name=name, grid=(8 // nh, N),
        in_specs=[blk(12 // nh), slab(20 // nh), slab(28 // nh), _const_spec((BLK, BLK))],
        out_specs=[blk(0), blk(0)],
        out_shape=[jax.ShapeDtypeStruct((T, 1024), BF16), jax.ShapeDtypeStruct((T, 1024), F32)],
        compiler_params=_params(("arbitrary", "arbitrary")),
    )(p2, p2, p2, _tri_and_ones(True, ones=False))


def sb_bwd(p2, o, dmixed, *, name):
    T = p2.shape[0]
    N = T // BLK
    kb = _key_blocks(N)

    heads = [slice(h * LANES, (h + 1) * LANES) for h in range(SB_PAIR)]
    scale = 64.0 ** -0.5

    def body(q_ref, k_ref, v_ref, after_ref, from_ref, o_ref, do_ref, dq_ref, dk_ref, dv_ref, dkt_scr, dvt_scr):
        n = pl.program_id(1)

        @pl.when(n == 0)
        def _():
            dkt_scr[...] = jnp.zeros_like(dkt_scr)
            dvt_scr[...] = jnp.zeros_like(dvt_scr)

        qbs, qts, dobs, dots, totals = [], [], [], [], []
        for hs in heads:
            qs = q_ref[:, hs].astype(F32) * scale
            do = do_ref[:, hs]
            qbs.append(qs.astype(BF16))
            qts.append(qs.T.astype(BF16))
            dobs.append(do.astype(BF16))
            dots.append(do.T.astype(BF16))
            total = jnp.sum(dobs[-1].astype(F32) * o_ref[:, hs], axis=1, keepdims=True)
            totals.append(jnp.broadcast_to(total, (BLK, LANES)))
        after, frm, pos = after_ref[...], from_ref[...], _sb_positions()
        nsup = n // kb + 1

        def step(t, c):
            dqs, carries, gcarries = c
            base = (nsup - 1 - t) * kb
            rows = [pl.ds(pl.multiple_of((base + sub) * BLK, BLK), BLK) for sub in range(kb)]
            ks = [[k_ref[r, hs] for r in rows] for hs in heads]
            valid, lb, a, carries = _sb_weights(qbs, ks, base, n, pos, carries, after)
            das = [[_nt(dobs[h], v_ref[r, hs]) for r in rows] for h, hs in enumerate(heads)]
            ab = [[a[h][sub].astype(BF16) for sub in range(kb)] for h in range(SB_PAIR)]
            g = [[None] * kb for _ in heads]
            sums = [[None] * kb for _ in heads]
            for sub in range(kb):
                for h in range(SB_PAIR):
                    g[h][sub] = das[h][sub] * ab[h][sub].astype(F32)
                    sums[h][sub] = _later_and_row_sums(g[h][sub], frm)
            dqs, gcarries = list(dqs), list(gcarries)
            for sub in reversed(range(kb)):
                for h in range(SB_PAIR):
                    before = totals[h] - (gcarries[h] + sums[h][sub][0])
                    gcarries[h] = gcarries[h] + sums[h][sub][1]
                    beta = jnp.exp(lb[h][sub])
                    dz = jnp.where(valid[sub], g[h][sub] - beta * (g[h][sub] + before), 0.0).astype(BF16)
                    dqs[h] = dqs[h] + _nn(dz, ks[h][sub])
                    dkt_scr[h * N + base + sub] += _nn(qts[h], dz)
                    dvt_scr[h * N + base + sub] += _nn(dots[h], ab[h][sub])
            return dqs, carries, gcarries

        zero = [jnp.zeros((BLK, LANES), F32)] * SB_PAIR
        dqs, _, _ = lax.fori_loop(0, nsup, step, (zero, zero, zero))
        dq_ref[...] = (jnp.concatenate(dqs, axis=1) * scale).astype(dq_ref.dtype)

        @pl.when(n == N - 1)
        def _():
            def flush(j, _):
                rows = pl.ds(pl.multiple_of(j * BLK, BLK), BLK)
                for h, hs in enumerate(heads):
                    dk_ref[rows, hs] = dkt_scr[h * N + j].T.astype(dk_ref.dtype)
                    dv_ref[rows, hs] = dvt_scr[h * N + j].T.astype(dv_ref.dtype)
                return 0

            lax.fori_loop(0, N, flush, 0)

    wide = SB_PAIR * LANES

    def slab(off):
        return pl.BlockSpec((T, wide), lambda g, n: (0, off + g))

    def blk(off):
        return pl.BlockSpec((BLK, wide), lambda g, n: (n, off + g))

    tri = _const_spec((BLK, BLK))
    return pl.pallas_call(
        body, name=name, grid=(8 // SB_PAIR, N),
        in_specs=[blk(12 // SB_PAIR), slab(20 // SB_PAIR), slab(28 // SB_PAIR), tri, tri, blk(0), blk(8 // SB_PAIR)],
        out_specs=[blk(0), slab(0), slab(0)],
        out_shape=[jax.ShapeDtypeStruct((T, 1024), BF16)] * 3,
        scratch_shapes=[pltpu.VMEM((SB_PAIR * N, LANES, LANES), F32), pltpu.VMEM((SB_PAIR * N, LANES, LANES), F32)],
        compiler_params=_params(("arbitrary", "arbitrary")),
    )(p2, p2, p2, _tri_and_ones(True, ones=False), _tri_and_ones(False, ones=False), o, dmixed)


def ffn_fwd(h, g_pre, g_post, wg, wu, wd, tag):
    u, gate, up, act = norm_mm(h, g_pre, (wg, wu), swiglu=True, wt=True, name=f"ffn_up_{tag}")
    y, h_new = mm_norm_res([act], [wd], h, g_post, 0.5, name=f"ffn_down_{tag}")
    return h_new, (h, u, gate, up, y)


def ffn_bwd(saved, dh, g_pre, g_post, wg, wu, wd, tag):
    h, u, gate, up, y = saved
    dy, dg_post, dgate, dup, act = normbwd_mm_nt(dh, y, g_post, wd, 0.5, (gate, up), name=f"ffn_bwd_down_{tag}")
    dwd = mm_tn(act, dy, name=f"ffn_dwd_{tag}")
    dwg = mm_tn(dgate, u, name=f"ffn_dwg_{tag}")
    dwu = mm_tn(dup, u, name=f"ffn_dwu_{tag}")
    dh_in, dg_pre = mm_nt_normbwd([dgate, dup], [wg, wu], h, g_pre, dh, wt=True, name=f"ffn_bwd_up_{tag}")
    return dh_in, (dg_pre, dg_post), (dwg, dwu, dwd)


def _lane_row(v):
    v = v.reshape(1, -1)
    return jnp.pad(v, ((0, 0), (0, LANES - v.shape[1])))


AB_WIDTHS = (512,) * 8 + (LANES,)


def mixer_ab_fwd(h, g_pre, g_post, w_in, conv_w, a_log, dt_bias, out_norm, w_out, tables, send_conv, send_gdn):
    u, p = norm_mm(h, g_pre, (w_in,), swiglu=False, name="ab_in")
    ret, sall_r = retention_fwd(p, tables, name="retention_fwd")
    act, gathered_conv = conv_silu_fwd(p, conv_w, send_conv, name="conv_fwd")
    gdn, sall_g, gathered_gdn = gdn_fwd(act, p, _lane_row(a_log), _lane_row(dt_bias), out_norm.reshape(1, LANES),
                                        send_gdn, name="gdn_fwd")
    y, h_new = mm_norm_res([ret, gdn], [w_out[:512], w_out[512:]], h, g_post, 1.0, name="ab_out")
    return h_new, (h, u, p, ret, sall_r, act, gdn, sall_g, y), list(gathered_conv) + list(gathered_gdn)


def mixer_ab_bwd(saved, dh, g_pre, g_post, w_in, conv_w, a_log, dt_bias, out_norm, w_out, tables, partials):
    h, u, p, ret, sall_r, act, gdn, sall_g, y = saved
    dy, dg_post, dmixed = normbwd_mm_nt(dh, y, g_post, w_out, 1.0, name="ab_bwd_out")
    dw_out = jnp.concatenate([mm_tn(ret, dy, name="ab_dwout_ret"), mm_tn(gdn, dy, name="ab_dwout_gdn")], axis=0)
    pieces = list(retention_bwd(p, sall_r, dmixed, tables, name="retention_bwd"))
    (dqa, dka, dva, dz, dba, dalog, ddtb, donorm), arrived = gdn_bwd(
        act, p, _lane_row(a_log), _lane_row(dt_bias), out_norm.reshape(1, LANES), sall_g, dmixed, partials,
        name="gdn_bwd")
    dconv = []
    for part, dact in enumerate((dqa, dka, dva)):
        dx, dw = conv_silu_bwd(p, conv_w, dact, part, name=f"conv_bwd_{part}")
        pieces.append(dx)
        dconv.append(dw)
    pieces += [dz, dba]
    offs = np.cumsum((0,) + AB_WIDTHS)
    w_parts = [w_in[:, a:b] for a, b in zip(offs[:-1], offs[1:])]
    dh_in, dg_pre = mm_nt_normbwd(pieces, w_parts, h, g_pre, dh, name="ab_bwd_in")
    dw_in = jnp.concatenate([mm_tn(u, pc, name=f"ab_dwin_{i}") for i, pc in enumerate(pieces)], axis=1)
    small = (jnp.concatenate(dconv, axis=1), dalog[:, :4], ddtb[:, :4], donorm)
    return dh_in, (dg_pre, dg_post), (dw_in, dw_out), small, arrived


CD_WIDTHS = (1024, 256, 256, 1024, 1024, 1024)


def mixer_cd_fwd(h, g_pre, g_post, w_in, sinks, w_out):
    u, p2 = norm_mm(h, g_pre, (w_in,), swiglu=False, out_dtype=BF16, name="cd_in")
    swa = swa_fwd(p2, _lane_row(sinks), name="swa_fwd")
    sb, sb_f32 = sb_fwd(p2, name="sb_fwd")
    y, h_new = mm_norm_res([swa, sb], [w_out[:1024], w_out[1024:]], h, g_post, 1.0, name="cd_out")
    return h_new, (h, u, p2, swa, sb, sb_f32, y)


def mixer_cd_bwd(saved, dh, g_pre, g_post, w_in, sinks, w_out):
    h, u, p2, swa, sb, sb_f32, y = saved
    dy, dg_post, dmixed = normbwd_mm_nt(dh, y, g_post, w_out, 1.0, name="cd_bwd_out")
    dw_out = jnp.concatenate([mm_tn(swa, dy, name="cd_dwout_swa"), mm_tn(sb, dy, name="cd_dwout_sb")], axis=0)
    dq_c, dk_c, dv_c, dsink = swa_bwd(p2, _lane_row(sinks), dmixed, name="swa_bwd")
    pieces = [dq_c, dk_c, dv_c] + list(sb_bwd(p2, sb_f32, dmixed, name="sb_bwd"))
    offs = np.cumsum((0,) + CD_WIDTHS)
    w_parts = [w_in[:, a:b] for a, b in zip(offs[:-1], offs[1:])]
    dh_in, dg_pre = mm_nt_normbwd(pieces, w_parts, h, g_pre, dh, name="cd_bwd_in")
    dw_in = jnp.concatenate([mm_tn(u, pc, name=f"cd_dwin_{i}") for i, pc in enumerate(pieces)], axis=1)
    return dh_in, (dg_pre, dg_post), (dw_in, dw_out), dsink[:, :8]


def _pad_heads(w, axis):
    shape = w.shape
    w = w.reshape(shape[:axis] + (shape[axis] // 64, 64) + shape[axis + 1:])
    pad = [(0, 0)] * w.ndim
    pad[axis + 1] = (0, 64)
    return jnp.pad(w, pad).reshape(shape[:axis] + (2 * shape[axis],) + shape[axis + 1:])


def _unpad_heads(w, axis):
    shape = w.shape
    w = w.reshape(shape[:axis] + (shape[axis] // 128, 128) + shape[axis + 1:])
    w = lax.slice_in_dim(w, 0, 64, axis=axis + 1)
    return w.reshape(shape[:axis] + (shape[axis] // 2,) + shape[axis + 1:])


SMALL_SHARDED = (("meta_tokens", (NMETA, LANES), 1), ("norm_gains", (2, 6, LANES), 2), ("ab_conv_w", (1, 4, 192), 2))
SMALL_REPL = (("ab_a_log", (1, 4)), ("ab_dt_bias", (1, 4)), ("ab_out_norm", (1, LANES)), ("cd_sinks", (1, 8)))


def _stack_shards(g, axis):
    full = jnp.moveaxis(g, 0, axis)
    shape = full.shape
    return full.reshape(shape[:axis] + (shape[axis] * shape[axis + 1],) + shape[axis + 2:])


def _split_shards(full, axis):
    shape = full.shape
    g = full.reshape(shape[:axis] + (NDEV, shape[axis] // NDEV) + shape[axis + 1:])
    return jnp.moveaxis(g, axis, 0)


def _pad_rows8(a):
    rows = []
    for x in a:
        flat = x.reshape(x.shape[0], -1)
        n = -(-flat.shape[1] // LANES) * LANES
        rows.append(jnp.pad(flat, ((0, 0), (0, n - flat.shape[1]))).reshape(x.shape[0], n // LANES, LANES))
    cat = jnp.concatenate(rows, axis=1)
    return jnp.pad(cat, ((0, 0), (0, -cat.shape[1] % 8), (0, 0)))


def _unpad_rows8(packed, shapes):
    out, at = [], 0
    for shape in shapes:
        size = int(np.prod(shape))
        nrow = -(-size // LANES)
        blk = packed[:, at:at + nrow].reshape(packed.shape[0], -1)[:, :size]
        out.append(blk.reshape((packed.shape[0],) + tuple(shape)))
        at += nrow
    return out


def kernel(x, meta_tokens, norm_gains, ffn_w_gate, ffn_w_up, ffn_w_down, ab_w_in, ab_conv_w, ab_a_log, ab_dt_bias, ab_out_norm, ab_w_out, cd_w_in, cd_sinks, cd_w_out, loss_target, m_meta_tokens, m_norm_gains, m_ffn_w_gate, m_ffn_w_up, m_ffn_w_down, m_ab_w_in, m_ab_conv_w, m_ab_a_log, m_ab_dt_bias, m_ab_out_norm, m_ab_w_out, m_cd_w_in, m_cd_sinks, m_cd_w_out, v_meta_tokens, v_norm_gains, v_ffn_w_gate, v_ffn_w_up, v_ffn_w_down, v_ab_w_in, v_ab_conv_w, v_ab_a_log, v_ab_dt_bias, v_ab_out_norm, v_ab_w_out, v_cd_w_in, v_cd_sinks, v_cd_w_out):
    w = dict(meta_tokens=meta_tokens, norm_gains=norm_gains, ffn_w_gate=ffn_w_gate, ffn_w_up=ffn_w_up,
             ffn_w_down=ffn_w_down, ab_w_in=ab_w_in, ab_conv_w=ab_conv_w, ab_a_log=ab_a_log, ab_dt_bias=ab_dt_bias,
             ab_out_norm=ab_out_norm, ab_w_out=ab_w_out, cd_w_in=cd_w_in, cd_sinks=cd_sinks, cd_w_out=cd_w_out)
    m = dict(meta_tokens=m_meta_tokens, norm_gains=m_norm_gains, ffn_w_gate=m_ffn_w_gate, ffn_w_up=m_ffn_w_up,
             ffn_w_down=m_ffn_w_down, ab_w_in=m_ab_w_in, ab_conv_w=m_ab_conv_w, ab_a_log=m_ab_a_log,
             ab_dt_bias=m_ab_dt_bias, ab_out_norm=m_ab_out_norm, ab_w_out=m_ab_w_out, cd_w_in=m_cd_w_in,
             cd_sinks=m_cd_sinks, cd_w_out=m_cd_w_out)
    v = dict(meta_tokens=v_meta_tokens, norm_gains=v_norm_gains, ffn_w_gate=v_ffn_w_gate, ffn_w_up=v_ffn_w_up,
             ffn_w_down=v_ffn_w_down, ab_w_in=v_ab_w_in, ab_conv_w=v_ab_conv_w, ab_a_log=v_ab_a_log,
             ab_dt_bias=v_ab_dt_bias, ab_out_norm=v_ab_out_norm, ab_w_out=v_ab_w_out, cd_w_in=v_cd_w_in,
             cd_sinks=v_cd_sinks, cd_w_out=v_cd_w_out)
    order = list(w)
    S = x.shape[1]
    T = S + BLK

    fs = DFF // NDEV

    def ffn_local(i, j):
        return jnp.concatenate([ffn_w_gate[i, j].T, ffn_w_up[i, j].T, ffn_w_down[i, j]], axis=0).astype(BF16)

    wg, wu, wd = {}, {}, {}

    def ffn_gathered(gathered, ij):
        for kind, full_w in enumerate((wg, wu, wd)):
            full_w[ij] = gathered[:, kind * fs:(kind + 1) * fs].reshape(DFF, D)

    ffn00_all, abin_all, about_all = all_gather_big(
        [ffn_local(0, 0), ab_w_in[0].astype(BF16), ab_w_out[0].astype(BF16)], name="gather_first")
    ffn_gathered(ffn00_all, (0, 0))
    ab_in = jnp.pad(_stack_shards(abin_all, 1), ((0, 0), (0, AB_INP - AB_IN)))
    ab_out = about_all.reshape(D, D)
    under_conv = [ffn_local(0, 1)]
    under_gdn = [ffn_local(1, 0), ffn_local(1, 1), cd_w_in[0].astype(BF16), cd_w_out[0].astype(BF16)]
    small_src = jnp.broadcast_to(_pad_rows8([w[n][None] for n, _, _ in SMALL_SHARDED]), (NDEV, 40, LANES))
    small_all = _unpad_rows8(all_to_all_small(small_src, name="gather_small"), [s for _, s, _ in SMALL_SHARDED])
    full = {n: _stack_shards(g, ax) for (n, _, ax), g in zip(SMALL_SHARDED, small_all)}
    conv_w = full["ab_conv_w"][0]
    gains = full["norm_gains"].reshape(2, 6, 1, D)
    tables = retention_tables(T)

    h = jnp.concatenate([jnp.zeros((PAD, D), F32), full["meta_tokens"], x[0]], axis=0)
    h, s00 = ffn_fwd(h, gains[0, 0], gains[0, 1], wg[0, 0], wu[0, 0], wd[0, 0], "00")
    h, sab, later_part = mixer_ab_fwd(h, gains[0, 2], gains[0, 3], ab_in, conv_w, ab_a_log, ab_dt_bias, ab_out_norm,
                                      ab_out, tables, under_conv, under_gdn)
    ffn01_all, ffn10_all, ffn11_all, cdin_all, cdout_all = gather_forward_to_sibling(later_part, name="gather_finish")
    for ij, gathered in (((0, 1), ffn01_all), ((1, 0), ffn10_all), ((1, 1), ffn11_all)):
        ffn_gathered(gathered, ij)
    cd_in = _pad_heads(_stack_shards(cdin_all, 1), 1)
    cd_out = _pad_heads(cdout_all.reshape(D, D), 0)
    h, s01 = ffn_fwd(h, gains[0, 4], gains[0, 5], wg[0, 1], wu[0, 1], wd[0, 1], "01")
    h, s10 = ffn_fwd(h, gains[1, 0], gains[1, 1], wg[1, 0], wu[1, 0], wd[1, 0], "10")
    h, scd = mixer_cd_fwd(h, gains[1, 2], gains[1, 3], cd_in, cd_sinks, cd_out)
    h, s11 = ffn_fwd(h, gains[1, 4], gains[1, 5], wg[1, 1], wu[1, 1], wd[1, 1], "11")
    loss_tile, dh = loss_and_grad(h, loss_target[0], name="loss")
    loss = lax.psum(loss_tile[0, 0], ("x", "y", "c"))

    dgain = [[None] * 6, [None] * 6]
    dffn = {}
    dh, (dgain[1][4], dgain[1][5]), dffn[1, 1] = ffn_bwd(s11, dh, gains[1, 4], gains[1, 5], wg[1, 1], wu[1, 1], wd[1, 1], "11")
    dh, (dgain[1][2], dgain[1][3]), (dcd_in, dcd_out), dsinks = mixer_cd_bwd(scd, dh, gains[1, 2], gains[1, 3], cd_in, cd_sinks, cd_out)
    dh, (dgain[1][0], dgain[1][1]), dffn[1, 0] = ffn_bwd(s10, dh, gains[1, 0], gains[1, 1], wg[1, 0], wu[1, 0], wd[1, 0], "10")
    dh, (dgain[0][4], dgain[0][5]), dffn[0, 1] = ffn_bwd(s01, dh, gains[0, 4], gains[0, 5], wg[0, 1], wu[0, 1], wd[0, 1], "01")
    ffn_send = lambda ij: jnp.concatenate([t.astype(BF16).reshape(NDEV, fs, D) for t in dffn[ij]], axis=1)
    early = [(0, 1), (1, 0), (1, 1)]
    early_send = [ffn_send(ij) for ij in early] + [_split_shards(_unpad_heads(dcd_in, 1).astype(BF16), 1),
                                                   _unpad_heads(dcd_out, 0).astype(BF16).reshape(NDEV, D // NDEV, D)]
    early_sib = rs_exchange_sibling(early_send, name="rs_sibling_early")
    early_parts = [rs_chip_partials(g, t, name=f"rs_chip_partials_early_{i}")
                   for i, (g, t) in enumerate(zip(early_send, early_sib))]
    dh, (dgain[0][2], dgain[0][3]), (dab_in, dab_out), (dconv, dalog, ddtb, donorm), early_got = mixer_ab_bwd(
        sab, dh, gains[0, 2], gains[0, 3], ab_in, conv_w, ab_a_log, ab_dt_bias, ab_out_norm, ab_out, tables, early_parts)
    dh, (dgain[0][0], dgain[0][1]), dffn[0, 0] = ffn_bwd(s00, dh, gains[0, 0], gains[0, 1], wg[0, 0], wu[0, 0], wd[0, 0], "00")
    grad_x = dh[BLK:][None]

    gfull = dict(meta_tokens=dh[PAD:BLK], norm_gains=jnp.stack([jnp.concatenate(r, axis=0) for r in dgain]),
                 ab_conv_w=dconv[None])
    early_g = [rs_final_sum(p, t, name=f"rs_final_sum_early_{i}") for i, (p, t) in enumerate(zip(early_parts, early_got))]
    ffn00_g, abin_g, about_g = reduce_scatter_big(
        [ffn_send((0, 0)), _split_shards(dab_in[:, :AB_IN].astype(BF16), 1),
         dab_out.astype(BF16).reshape(NDEV, D // NDEV, D)])
    cdin_g, cdout_g = early_g[3:]
    ffn_by = dict(zip(early, early_g[:3]))
    ffn_by[0, 0] = ffn00_g
    ffn_g = jnp.stack([jnp.stack([ffn_by[i, j].reshape(3, fs, D) for j in range(2)]) for i in range(2)])
    ffn_g = jnp.moveaxis(ffn_g, 2, 0)
    grads = dict(ffn_w_gate=jnp.swapaxes(ffn_g[0], 2, 3), ffn_w_up=jnp.swapaxes(ffn_g[1], 2, 3), ffn_w_down=ffn_g[2],
                 ab_w_in=abin_g[None], ab_w_out=about_g[None], cd_w_in=cdin_g[None], cd_w_out=cdout_g[None])
    repl = [jnp.broadcast_to(t[None], (NDEV,) + t.shape) for t in (dalog, ddtb, donorm, dsinks)]
    ssend = _pad_rows8([_split_shards(gfull[n], ax) for n, _, ax in SMALL_SHARDED] + repl)
    ssum = sum_slots(all_to_all_small(ssend, name="exchange_small_grads"), name="sum_small_grads")[None]
    small = _unpad_rows8(ssum, [s for _, s, _ in SMALL_SHARDED] + [s for _, s in SMALL_REPL])
    grads.update({n: g[0] for n, g in zip([n for n, _, _ in SMALL_SHARDED] + [n for n, _ in SMALL_REPL], small)})

    delta, new_m, new_v = {}, {}, {}
    for n in order:
        shape = w[n].shape
        view = (-1, shape[-1])
        d_, m_, v_ = adamw(w[n].reshape(view), grads[n].reshape(view), m[n].reshape(view), v[n].reshape(view),
                           name=f"adamw_{n}")
        delta[n], new_m[n], new_v[n] = d_.reshape(shape), m_.reshape(shape), v_.reshape(shape)
    return (loss, grad_x, *[grads[n] for n in order], *[delta[n] for n in order], *[new_m[n] for n in order],
            *[new_v[n] for n in order])
```

```python
import numpy as np
import jax
import jax.numpy as jnp
from jax import lax
from jax.experimental import pallas as pl
from jax.experimental.pallas import tpu as pltpu

F32, BF16 = jnp.float32, jnp.bfloat16
EPS = 1e-6
D = 1024
NMETA = 16
BLK = 128
PAD = BLK - NMETA
DFF = 2816
LANES = 128
NDEV = 8
AB_IN, AB_INP = 4104, 4224
ADAM_LR, ADAM_B1, ADAM_B2, ADAM_EPS, ADAM_WD, ADAM_STEP = 0.001, 0.9, 0.999, 1e-08, 0.01, 10
VMEM_LIMIT = 56 * 1024 * 1024
MESH = pl.DeviceIdType.MESH
HIGH = lax.Precision.HIGH


def _params(sem):
    return pltpu.CompilerParams(dimension_semantics=sem, vmem_limit_bytes=VMEM_LIMIT)


def _row_tile(T, streamed, resident):
    for tm in (640, 320, 128):
        if T % tm == 0 and 2 * (tm * streamed + resident) <= VMEM_LIMIT - 14 * 1024 * 1024:
            return tm
    return _tile(T, 128)


MXU_COLS = 256


def _col_chunks(n):
    return [slice(c, min(c + MXU_COLS, n)) for c in range(0, n, MXU_COLS)]


def _tile(n, cap, unit=LANES):
    if n <= cap:
        return n
    best = None
    for t in range(unit, cap + 1, unit):
        if n % t == 0:
            best = t
    assert best is not None, (n, cap)
    return best


def _rms_fwd(x, g):
    return x * lax.rsqrt(jnp.mean(x * x, axis=-1, keepdims=True) + EPS) * g


def _rms_bwd(x, g, dz):
    r = lax.rsqrt(jnp.mean(x * x, axis=-1, keepdims=True) + EPS)
    xh = x * r
    dg = jnp.sum(dz * xh, axis=0, keepdims=True)
    t = dz * g
    return r * (t - xh * jnp.mean(t * xh, axis=-1, keepdims=True)), dg


def _sigmoid(x):
    return 0.5 * jnp.tanh(0.5 * x) + 0.5


def _silu(x):
    return x * _sigmoid(x)


def _nn(a, b, precision=None):
    return lax.dot_general(a, b, (((1,), (0,)), ((), ())), preferred_element_type=F32, precision=precision)


def _nt(a, b):
    return lax.dot_general(a, b, (((1,), (1,)), ((), ())), preferred_element_type=F32)


def _tn(a, b):
    return lax.dot_general(a, b, (((0,), (0,)), ((), ())), preferred_element_type=F32)


def _mm(a, b, precision=None):
    if a.ndim == 3:
        return lax.dot_general(a, b, (((2,), (1,)), ((0,), (0,))), preferred_element_type=F32, precision=precision)
    return _nn(a, b, precision)


def _t(x):
    return jnp.swapaxes(x, -1, -2)


@jax.custom_vjp
def bdot(a, b):
    return _mm(a.astype(BF16), b.astype(BF16))


def _bdot_fwd(a, b):
    return bdot(a, b), (a, b)


def _bdot_bwd(res, g):
    a, b = res
    return bdot(g, _t(b)), bdot(_t(a), g)


bdot.defvjp(_bdot_fwd, _bdot_bwd)


@jax.custom_vjp
def hdot(a, b):
    return _mm(a, b, HIGH)


def _hdot_fwd(a, b):
    return hdot(a, b), (a, b)


def _hdot_bwd(res, g):
    a, b = res
    return hdot(g, _t(b)), hdot(_t(a), g)


hdot.defvjp(_hdot_fwd, _hdot_bwd)


def _iota2(shape, axis):
    return lax.broadcasted_iota(jnp.int32, shape, axis)


def _lane_pick(row, lane):
    return jnp.sum(jnp.where(_iota2(row.shape, 1) == lane, row, 0.0), axis=1, keepdims=True)


def norm_mm(h, gain, ws, *, swiglu, name, wt=False, out_dtype=F32):
    T, Dm = h.shape
    N = ws[0].shape[0 if wt else 1]
    tm, tn = _tile(T, 640), _tile(N, 1408)
    nw = len(ws)
    mm = _nt if wt else _nn

    def body(h_ref, g_ref, *refs):
        w_refs, u_ref, o_refs = refs[:nw], refs[nw], refs[nw + 1:]

        @pl.when(pl.program_id(1) == 0)
        def _():
            u_ref[...] = _rms_fwd(h_ref[...], g_ref[...]).astype(BF16)

        u = u_ref[...]
        for cols in _col_chunks(tn):
            acc = [mm(u, w[cols, :] if wt else w[:, cols]) for w in w_refs]
            if swiglu:
                o_refs[0][:, cols] = acc[0].astype(BF16)
                o_refs[1][:, cols] = acc[1].astype(BF16)
                o_refs[2][:, cols] = (_silu(acc[0]) * acc[1]).astype(BF16)
            else:
                o_refs[0][:, cols] = acc[0].astype(out_dtype)

    row = pl.BlockSpec((tm, Dm), lambda i, j: (i, 0))
    tile = pl.BlockSpec((tm, tn), lambda i, j: (i, j))
    if swiglu:
        out_shape = [jax.ShapeDtypeStruct((T, Dm), BF16)] + [jax.ShapeDtypeStruct((T, N), BF16)] * 3
        out_specs = [row, tile, tile, tile]
    else:
        out_shape = [jax.ShapeDtypeStruct((T, Dm), BF16), jax.ShapeDtypeStruct((T, N), out_dtype)]
        out_specs = [row, tile]
    return pl.pallas_call(
        body, name=name, grid=(T // tm, N // tn),
        in_specs=[row, pl.BlockSpec((1, Dm), lambda i, j: (0, 0))]
        + [pl.BlockSpec((tn, Dm), lambda i, j: (j, 0)) if wt else pl.BlockSpec((Dm, tn), lambda i, j: (0, j))] * nw,
        out_specs=out_specs, out_shape=out_shape,
        compiler_params=_params(("arbitrary", "arbitrary")),
    )(h, gain, *ws)


def mm_norm_res(As, Ws, h, gain, scale, *, name):
    T, Dm = h.shape
    n = len(As)
    tm = _row_tile(T, sum(a.shape[1] * a.dtype.itemsize for a in As) + 3 * Dm * 4,
                   sum(w.size * w.dtype.itemsize for w in Ws))

    def body(*refs):
        a_refs, w_refs = refs[:n], refs[n:2 * n]
        h_ref, g_ref, y_ref, hn_ref = refs[2 * n:]
        y = _nn(a_refs[0][...].astype(BF16), w_refs[0][...])
        for a, w in zip(a_refs[1:], w_refs[1:]):
            y = y + _nn(a[...].astype(BF16), w[...])
        y_ref[...] = y
        hn_ref[...] = h_ref[...] + scale * _rms_fwd(y, g_ref[...])

    row = pl.BlockSpec((tm, Dm), lambda i: (i, 0))
    return pl.pallas_call(
        body, name=name, grid=(T // tm,),
        in_specs=[pl.BlockSpec((tm, a.shape[1]), lambda i: (i, 0)) for a in As]
        + [pl.BlockSpec(w.shape, lambda i: (0, 0)) for w in Ws]
        + [row, pl.BlockSpec((1, Dm), lambda i: (0, 0))],
        out_specs=[row, row], out_shape=[jax.ShapeDtypeStruct((T, Dm), F32)] * 2,
        compiler_params=_params(("arbitrary",)),
    )(*As, *Ws, h, gain)


def normbwd_mm_nt(dh, y, gain, w, scale, gu=None, *, name):
    T, Dm = dh.shape
    N = w.shape[0]
    tm, tn = _tile(T, 640), _tile(N, 1408)
    swiglu = gu is not None

    def body(dh_ref, y_ref, g_ref, w_ref, *refs):
        if swiglu:
            gate_ref, up_ref, dy_ref, dg_ref, dgate_ref, dup_ref, a_ref = refs
        else:
            dy_ref, dg_ref, da_ref = refs
        i, j = pl.program_id(0), pl.program_id(1)

        @pl.when(j == 0)
        def _():
            dy, dg = _rms_bwd(y_ref[...], g_ref[...], scale * dh_ref[...])
            dy_ref[...] = dy.astype(BF16)

            @pl.when(i == 0)
            def _():
                dg_ref[...] = jnp.zeros_like(dg_ref)

            dg_ref[...] += dg

        dy = dy_ref[...]
        for cols in _col_chunks(tn):
            da = _nt(dy, w_ref[cols, :])
            if swiglu:
                gate, up = gate_ref[:, cols].astype(F32), up_ref[:, cols].astype(F32)
                s = _sigmoid(gate)
                dgate_ref[:, cols] = (da * up * s * (1.0 + gate * (1.0 - s))).astype(BF16)
                dup_ref[:, cols] = (da * gate * s).astype(BF16)
                a_ref[:, cols] = (gate * s * up).astype(BF16)
            else:
                da_ref[:, cols] = da

    row = pl.BlockSpec((tm, Dm), lambda i, j: (i, 0))
    vec = pl.BlockSpec((1, Dm), lambda i, j: (0, 0))
    tile = pl.BlockSpec((tm, tn), lambda i, j: (i, j))
    in_specs = [row, row, vec, pl.BlockSpec((tn, Dm), lambda i, j: (j, 0))]
    out_shape = [jax.ShapeDtypeStruct((T, Dm), BF16), jax.ShapeDtypeStruct((1, Dm), F32)]
    if swiglu:
        in_specs += [tile, tile]
        out_shape += [jax.ShapeDtypeStruct((T, N), BF16)] * 3
        out_specs = [row, vec, tile, tile, tile]
        args = (dh, y, gain, w, *gu)
    else:
        out_shape += [jax.ShapeDtypeStruct((T, N), F32)]
        out_specs = [row, vec, tile]
        args = (dh, y, gain, w)
    return pl.pallas_call(
        body, name=name, grid=(T // tm, N // tn), in_specs=in_specs, out_specs=out_specs,
        out_shape=out_shape, compiler_params=_params(("arbitrary", "arbitrary")),
    )(*args)


def mm_nt_normbwd(dPs, Ws, h, gain, dh_in, *, name, wt=False):
    T, Dm = h.shape
    n = len(dPs)
    tm = _row_tile(T, sum(p.shape[1] * p.dtype.itemsize for p in dPs) + 3 * Dm * 4,
                   sum(w.size * w.dtype.itemsize for w in Ws))
    mm = _nn if wt else _nt

    def body(*refs):
        p_refs, w_refs = refs[:n], refs[n:2 * n]
        h_ref, g_ref, dhin_ref, dh_ref, dg_ref = refs[2 * n:]
        du = mm(p_refs[0][...].astype(BF16), w_refs[0][...])
        for p, w in zip(p_refs[1:], w_refs[1:]):
            du = du + mm(p[...].astype(BF16), w[...])
        dx, dg = _rms_bwd(h_ref[...], g_ref[...], du)
        dh_ref[...] = dhin_ref[...] + dx

        @pl.when(pl.program_id(0) == 0)
        def _():
            dg_ref[...] = jnp.zeros_like(dg_ref)

        dg_ref[...] += dg

    row = pl.BlockSpec((tm, Dm), lambda i: (i, 0))
    vec = pl.BlockSpec((1, Dm), lambda i: (0, 0))
    return pl.pallas_call(
        body, name=name, grid=(T // tm,),
        in_specs=[pl.BlockSpec((tm, p.shape[1]), lambda i: (i, 0)) for p in dPs]
        + [pl.BlockSpec(w.shape, lambda i: (0, 0)) for w in Ws] + [row, vec, row],
        out_specs=[row, vec],
        out_shape=[jax.ShapeDtypeStruct((T, Dm), F32), jax.ShapeDtypeStruct((1, Dm), F32)],
        compiler_params=_params(("arbitrary",)),
    )(*dPs, *Ws, h, gain, dh_in)


def mm_tn(a, b, *, name):
    T, M = a.shape
    N = b.shape[1]
    tm, tn, tk = _tile(M, 1408), _tile(N, 1408), _tile(T, 640)

    def body(a_ref, b_ref, o_ref):
        @pl.when(pl.program_id(2) == 0)
        def _():
            o_ref[...] = jnp.zeros_like(o_ref)

        o_ref[...] += _tn(a_ref[...].astype(BF16), b_ref[...].astype(BF16))

    return pl.pallas_call(
        body, name=name, grid=(M // tm, N // tn, T // tk),
        in_specs=[pl.BlockSpec((tk, tm), lambda i, j, k: (k, i)), pl.BlockSpec((tk, tn), lambda i, j, k: (k, j))],
        out_specs=pl.BlockSpec((tm, tn), lambda i, j, k: (i, j)),
        out_shape=jax.ShapeDtypeStruct((M, N), F32),
        compiler_params=_params(("arbitrary", "arbitrary", "arbitrary")),
    )(a, b)


def loss_and_grad(h, target, *, name):
    T, Dm = h.shape

    def body(h_ref, t_ref, loss_ref, dh_ref):
        b = pl.program_id(0)

        @pl.when(b == 0)
        def _():
            loss_ref[...] = jnp.zeros_like(loss_ref)
            dh_ref[...] = jnp.zeros_like(dh_ref)

        @pl.when(b > 0)
        def _():
            e = h_ref[...] - t_ref[...]
            dh_ref[...] = e * (1.0 / Dm)
            loss_ref[...] += jnp.sum(e * e) * (0.5 / Dm)

    return pl.pallas_call(
        body, name=name, grid=(T // BLK,),
        in_specs=[pl.BlockSpec((BLK, Dm), lambda b: (b, 0)),
                  pl.BlockSpec((BLK, Dm), lambda b: (jnp.maximum(b - 1, 0), 0))],
        out_specs=[pl.BlockSpec((8, LANES), lambda b: (0, 0)), pl.BlockSpec((BLK, Dm), lambda b: (b, 0))],
        out_shape=[jax.ShapeDtypeStruct((8, LANES), F32), jax.ShapeDtypeStruct((T, Dm), F32)],
        compiler_params=_params(("arbitrary",)),
    )(h, target)


def adamw(w, g, m, v, *, name):
    R, C = w.shape
    tr = R
    for t in (512, 352, 256):
        if R > t and R % t == 0:
            tr = t
            break

    def body(w_ref, g_ref, m_ref, v_ref, d_ref, nm_ref, nv_ref):
        g_ = g_ref[...]
        m_ = ADAM_B1 * m_ref[...] + (1.0 - ADAM_B1) * g_
        v_ = ADAM_B2 * v_ref[...] + (1.0 - ADAM_B2) * (g_ * g_)
        m_hat = m_ / (1.0 - ADAM_B1 ** ADAM_STEP)
        v_hat = v_ / (1.0 - ADAM_B2 ** ADAM_STEP)
        d_ref[...] = -ADAM_LR * (m_hat / (jnp.sqrt(v_hat) + ADAM_EPS) + ADAM_WD * w_ref[...])
        nm_ref[...] = m_
        nv_ref[...] = v_

    spec = pl.BlockSpec((tr, C), lambda i: (i, 0))
    return pl.pallas_call(
        body, name=name, grid=(R // tr,), in_specs=[spec] * 4, out_specs=[spec] * 3,
        out_shape=[jax.ShapeDtypeStruct((R, C), F32)] * 3, compiler_params=_params(("arbitrary",)),
    )(w, g, m, v)


def _me():
    return lax.axis_index("x"), lax.axis_index("y"), lax.axis_index("c")


def _flip(pos, rel):
    return tuple(1 - p if r else p for p, r in zip(pos, rel))


def _slot(pos):
    return 4 * pos[0] + 2 * pos[1] + pos[2]


HBM_SPEC = pl.BlockSpec(memory_space=pltpu.HBM)
CHIP_RELS = ((1, 0), (0, 1), (1, 1))


def all_gather_big(xs, *, name):
    n = len(xs)

    def body(*refs):
        x_refs, out_refs = refs[:n], refs[n:2 * n]
        send_sems, recv_sems, local_sems = refs[2 * n:]
        me = _me()
        sibling = _flip(me, (0, 0, 1))
        chips = [_flip(me, rel + (0,)) for rel in CHIP_RELS]

        def copy(i, k, block, to, src=None):
            dst = out_refs[i].at[_slot(block)]
            return pltpu.make_async_remote_copy(
                src_ref=dst if src is None else src, dst_ref=dst, send_sem=send_sems.at[i, k],
                recv_sem=recv_sems.at[i, k], device_id=to, device_id_type=MESH)

        sent, local = [], []
        for i in range(n):
            mine = pltpu.make_async_copy(x_refs[i], out_refs[i].at[_slot(me)], local_sems.at[i])
            mine.start()
            local.append(mine)
            sent += [copy(i, 0, me, sibling, src=x_refs[i])]
            sent += [copy(i, 1 + j, me, chip, src=x_refs[i]) for j, chip in enumerate(chips)]
        for cp in sent:
            cp.start()
        for i in range(n):
            for j, chip in enumerate(chips):
                copy(i, 1 + j, chip, me).wait_recv()
                passed = copy(i, 4 + j, chip, sibling)
                passed.start()
                sent.append(passed)
        for i in range(n):
            copy(i, 0, sibling, me).wait_recv()
            for j, chip in enumerate(chips):
                copy(i, 4 + j, _flip(chip, (0, 0, 1)), me).wait_recv()
        for cp in sent:
            cp.wait_send()
        for mine in local:
            mine.wait()

    return pl.pallas_call(
        body, name=name, in_specs=[HBM_SPEC] * n, out_specs=[HBM_SPEC] * n,
        out_shape=[jax.ShapeDtypeStruct((NDEV,) + x.shape, x.dtype) for x in xs],
        scratch_shapes=[pltpu.SemaphoreType.DMA((n, 7)), pltpu.SemaphoreType.DMA((n, 7)), pltpu.SemaphoreType.DMA((n,))],
    )(*xs)


def all_to_all_small(src, *, name):
    _, r, C = src.shape

    def body(src_ref, out_ref, send_sems, recv_sems):
        me = _me()
        my = _slot(me)
        out_ref[my] = src_ref[my]
        copies = []
        for k in range(1, NDEV):
            peer = _flip(me, ((k >> 2) & 1, (k >> 1) & 1, k & 1))
            cp = pltpu.make_async_remote_copy(
                src_ref=src_ref.at[_slot(peer)], dst_ref=out_ref.at[my], send_sem=send_sems.at[k - 1],
                recv_sem=recv_sems.at[k - 1], device_id=peer, device_id_type=MESH)
            cp.start()
            copies.append((cp, peer))
        for k, (cp, peer) in enumerate(copies):
            pltpu.make_async_remote_copy(
                src_ref=src_ref.at[my], dst_ref=out_ref.at[_slot(peer)], send_sem=send_sems.at[k],
                recv_sem=recv_sems.at[k], device_id=peer, device_id_type=MESH).wait_recv()
        for cp, _ in copies:
            cp.wait_send()

    vm = pl.BlockSpec(memory_space=pltpu.VMEM)
    return pl.pallas_call(
        body, name=name, in_specs=[vm], out_specs=vm, out_shape=jax.ShapeDtypeStruct(src.shape, src.dtype),
        scratch_shapes=[pltpu.SemaphoreType.DMA((7,)), pltpu.SemaphoreType.DMA((7,))],
    )(src)


def sum_slots(a, *, name):
    n, r, C = a.shape

    def body(a_ref, o_ref):
        s = a_ref[0]
        for k in range(1, n):
            s = s + a_ref[k]
        o_ref[...] = s

    vm = pl.BlockSpec(memory_space=pltpu.VMEM)
    return pl.pallas_call(body, name=name, in_specs=[vm], out_specs=vm,
                          out_shape=jax.ShapeDtypeStruct((r, C), F32))(a)


def rs_exchange_sibling(gs, *, name):
    n = len(gs)

    def body(*refs):
        g_refs, out_refs, send_sems, recv_sems = refs[:n], refs[n:2 * n], refs[2 * n], refs[2 * n + 1]
        sibling = _flip(_me(), (0, 0, 1))
        copies = []
        for i in range(n):
            for chip in range(4):
                cp = pltpu.make_async_remote_copy(
                    src_ref=g_refs[i].at[2 * chip + sibling[2]], dst_ref=out_refs[i].at[chip],
                    send_sem=send_sems.at[i, chip], recv_sem=recv_sems.at[i, chip], device_id=sibling,
                    device_id_type=MESH)
                cp.start()
                copies.append(cp)
        for cp in copies:
            cp.wait()

    return pl.pallas_call(
        body, name=name, in_specs=[HBM_SPEC] * n, out_specs=[HBM_SPEC] * n,
        out_shape=[jax.ShapeDtypeStruct((4,) + g.shape[1:], g.dtype) for g in gs],
        scratch_shapes=[pltpu.SemaphoreType.DMA((n, 4)), pltpu.SemaphoreType.DMA((n, 4))],
    )(*gs)


def rs_chip_partials(g, got, *, name):
    _, R, C = g.shape
    tr = _tile(R, 768, unit=16)

    def body(c_ref, g_ref, got_ref, o_ref):
        o_ref[...] = (g_ref[...].astype(F32) + got_ref[...].astype(F32)).astype(o_ref.dtype)

    c = jnp.reshape(lax.axis_index("c"), (1,)).astype(jnp.int32)
    return pl.pallas_call(
        body, name=name,
        grid_spec=pltpu.PrefetchScalarGridSpec(
            num_scalar_prefetch=1, grid=(4, R // tr),
            in_specs=[pl.BlockSpec((None, tr, C), lambda k, i, c_ref: (2 * k + c_ref[0], i, 0)),
                      pl.BlockSpec((None, tr, C), lambda k, i, c_ref: (k, i, 0))],
            out_specs=pl.BlockSpec((None, tr, C), lambda k, i, c_ref: (k, i, 0))),
        out_shape=jax.ShapeDtypeStruct((4, R, C), g.dtype), compiler_params=_params(("arbitrary", "arbitrary")),
    )(c, g, got)


def rs_exchange_chips(ps, *, name):
    n = len(ps)

    def body(*refs):
        copies = _chip_exchange_copies(refs[:n], refs[n:2 * n], refs[2 * n], refs[2 * n + 1])
        for cp in copies:
            cp.start()
        for cp in copies:
            cp.wait()

    return pl.pallas_call(
        body, name=name, in_specs=[HBM_SPEC] * n, out_specs=[HBM_SPEC] * n,
        out_shape=_chip_exchange_shapes(ps), scratch_shapes=_chip_exchange_sems(n),
    )(*ps)


def _chip_exchange_copies(p_refs, out_refs, send_sems, recv_sems):
    me = _me()
    copies = []
    for i, (p_ref, out_ref) in enumerate(zip(p_refs, out_refs)):
        for j, rel in enumerate(CHIP_RELS):
            peer = _flip(me, rel + (0,))
            copies.append(pltpu.make_async_remote_copy(
                src_ref=p_ref.at[2 * peer[0] + peer[1]], dst_ref=out_ref.at[j], send_sem=send_sems.at[i, j],
                recv_sem=recv_sems.at[i, j], device_id=peer, device_id_type=MESH))
    return copies


def _chip_exchange_shapes(ps):
    return [jax.ShapeDtypeStruct((3,) + p.shape[1:], p.dtype) for p in ps]


def _chip_exchange_sems(n):
    return [pltpu.SemaphoreType.DMA((n, 3)), pltpu.SemaphoreType.DMA((n, 3))]


def _gather_direct(start, x_refs, out_refs, send_sems, recv_sems, local_sems):
    me = _me()
    peers = [_flip(me, (0, 0, 1))] + [_flip(me, rel + (0,)) for rel in CHIP_RELS]
    for i, (x_ref, out_ref) in enumerate(zip(x_refs, out_refs)):
        local = pltpu.make_async_copy(x_ref, out_ref.at[_slot(me)], local_sems.at[i])
        local.start() if start else local.wait()
        for k, peer in enumerate(peers):
            def copy(block):
                return pltpu.make_async_remote_copy(
                    src_ref=x_ref, dst_ref=out_ref.at[_slot(block)], send_sem=send_sems.at[i, k],
                    recv_sem=recv_sems.at[i, k], device_id=peer, device_id_type=MESH)
            if start:
                copy(me).start()
            else:
                copy(me).wait_send()
                copy(peer).wait_recv()


def _gather_direct_sems(n):
    return [pltpu.SemaphoreType.DMA((n, 4)), pltpu.SemaphoreType.DMA((n, 4)), pltpu.SemaphoreType.DMA((n,))]


def gather_forward_to_sibling(gs, *, name):
    n = len(gs)

    def body(*refs):
        in_refs, out_refs, send_sems, recv_sems = refs[:n], refs[n:2 * n], refs[2 * n], refs[2 * n + 1]
        me = _me()
        sibling = _flip(me, (0, 0, 1))
        chips = [_flip(me, rel + (0,)) for rel in CHIP_RELS]
        sends, recvs = [], []
        for i in range(n):
            for j, chip in enumerate(chips):
                def copy(block):
                    return pltpu.make_async_remote_copy(
                        src_ref=in_refs[i].at[_slot(chip)], dst_ref=out_refs[i].at[_slot(block)],
                        send_sem=send_sems.at[i, j], recv_sem=recv_sems.at[i, j], device_id=sibling, device_id_type=MESH)
                sends.append(copy(chip))
                recvs.append(copy(_flip(chip, (0, 0, 1))))
        for cp in sends:
            cp.start()
        for cp in recvs:
            cp.wait_recv()
        for cp in sends:
            cp.wait_send()

    return pl.pallas_call(
        body, name=name, in_specs=[HBM_SPEC] * n, out_specs=[HBM_SPEC] * n,
        out_shape=[jax.ShapeDtypeStruct(g.shape, g.dtype) for g in gs],
        input_output_aliases={i: i for i in range(n)},
        scratch_shapes=[pltpu.SemaphoreType.DMA((n, 3)), pltpu.SemaphoreType.DMA((n, 3))],
    )(*gs)


def rs_final_sum(p, got, *, name):
    _, R, C = p.shape
    tr = _tile(R, 768, unit=16)

    def body(chip_ref, p_ref, got_ref, o_ref):
        s = p_ref[...].astype(F32)
        for j in range(3):
            s = s + got_ref[j].astype(F32)
        o_ref[...] = s

    mychip = jnp.reshape(2 * lax.axis_index("x") + lax.axis_index("y"), (1,)).astype(jnp.int32)
    return pl.pallas_call(
        body, name=name,
        grid_spec=pltpu.PrefetchScalarGridSpec(
            num_scalar_prefetch=1, grid=(R // tr,),
            in_specs=[pl.BlockSpec((None, tr, C), lambda i, chip_ref: (chip_ref[0], i, 0)),
                      pl.BlockSpec((3, tr, C), lambda i, chip_ref: (0, i, 0))],
            out_specs=pl.BlockSpec((tr, C), lambda i, chip_ref: (i, 0))),
        out_shape=jax.ShapeDtypeStruct((R, C), F32), compiler_params=_params(("arbitrary",)),
    )(mychip, p, got)


def reduce_scatter_big(gs):
    got = rs_exchange_sibling(gs, name="rs_sibling")
    parts = [rs_chip_partials(g, t, name=f"rs_chip_partials_{i}") for i, (g, t) in enumerate(zip(gs, got))]
    got2 = rs_exchange_chips(parts, name="rs_chips")
    return [rs_final_sum(p, t, name=f"rs_final_sum_{i}") for i, (p, t) in enumerate(zip(parts, got2))]


def _const_spec(shape):
    return pl.BlockSpec(shape, lambda *_: (0,) * len(shape))


def retention_tables(T):
    pos = jnp.arange(T, dtype=F32) - float(PAD)
    inv_freq = 1.0 / (10000.0 ** jnp.linspace(0.0, 1.0, 64, dtype=F32))
    ang = pos[:, None] * inv_freq[None, :]
    cos = jnp.repeat(jnp.cos(ang), 2, axis=1)
    sin = jnp.repeat(jnp.sin(ang), 2, axis=1) * jnp.tile(jnp.array([-1.0, 1.0], F32), 64)[None, :]
    lane = np.arange(LANES)
    perm = jnp.broadcast_to(jnp.asarray((lane[:, None] == (lane[None, :] ^ 1)).astype(np.float32)), (4, LANES, LANES))
    log_gamma = jnp.log1p(-jnp.exp2(-5.0 - jnp.arange(4, dtype=F32)))
    idx = jnp.arange(BLK, dtype=F32)
    diff = idx[:, None] - idx[None, :]
    intra = jnp.where(diff >= 0, jnp.exp(jnp.maximum(diff, 0.0) * log_gamma[:, None, None]), 0.0)
    zeta = jnp.exp((BLK - 1.0 - idx)[None, :] * log_gamma[:, None])
    xi = jnp.exp((idx + 1.0)[None, :] * log_gamma[:, None])
    bc = lambda t: jnp.broadcast_to(t[:, :, None], (4, BLK, LANES))
    return cos, sin, perm, intra, bc(zeta), bc(xi)


def _heads(x):
    return jnp.stack([x[:, h * LANES:(h + 1) * LANES] for h in range(4)])


def _unheads(y):
    return jnp.concatenate([y[h] for h in range(4)], axis=1)


def _ret_chunk(rq, rk, rv, rg, S, cos, sin, intra, zeta, xi, perm):
    q = rq * cos + hdot(rq, perm) * sin
    k = (rk * cos + hdot(rk, perm) * sin) * (128.0 ** -0.5)
    ret = bdot(bdot(q, _t(k)) * intra, rv) + bdot(q * xi, S)
    S_new = S * xi[..., BLK - 1:BLK, :] + bdot(_t(k * zeta), rv)
    c = ret - jnp.mean(ret, axis=-1, keepdims=True)
    out = c * lax.rsqrt(jnp.mean(c * c, axis=-1, keepdims=True) + EPS) * _silu(rg)
    return out, S_new


def _wide(off):
    return pl.BlockSpec((BLK, 4 * LANES), lambda n: (n, off))


def retention_fwd(p, tables, *, name):
    T = p.shape[0]
    N = T // BLK
    cos, sin, perm, intra, zeta, xi = tables

    def body(rq, rk, rv, rg, cos_ref, sin_ref, in_ref, ze_ref, xi_ref, perm_ref, out_ref, sall_ref, s_scr):
        @pl.when(pl.program_id(0) == 0)
        def _():
            s_scr[...] = jnp.zeros_like(s_scr)

        S = s_scr[...]
        sall_ref[...] = S
        out, S_new = _ret_chunk(_heads(rq[...]), _heads(rk[...]), _heads(rv[...]), _heads(rg[...]), S, cos_ref[...],
                                sin_ref[...], in_ref[...], ze_ref[...], xi_ref[...], perm_ref[...])
        out_ref[...] = _unheads(out).astype(BF16)
        s_scr[...] = S_new

    rowtab = pl.BlockSpec((BLK, LANES), lambda n: (n, 0))
    tab = _const_spec((4, BLK, LANES))
    return pl.pallas_call(
        body, name=name, grid=(N,),
        in_specs=[_wide(0), _wide(1), _wide(2), _wide(3), rowtab, rowtab, tab, tab, tab, tab],
        out_specs=[_wide(0), pl.BlockSpec((None, 4, LANES, LANES), lambda n: (n, 0, 0, 0))],
        out_shape=[jax.ShapeDtypeStruct((T, 512), BF16), jax.ShapeDtypeStruct((N, 4, LANES, LANES), F32)],
        scratch_shapes=[pltpu.VMEM((4, LANES, LANES), F32)],
        compiler_params=_params(("arbitrary",)),
    )(p, p, p, p, cos, sin, intra, zeta, xi, perm)


def _row_mask(n):
    return (n * BLK + _iota2((BLK, 1), 0) >= PAD).astype(F32)


def retention_bwd(p, sall, dmixed, tables, *, name):
    T = p.shape[0]
    N = T // BLK
    cos, sin, perm, intra, zeta, xi = tables

    def body(rq, rk, rv, rg, cos_ref, sin_ref, in_ref, ze_ref, xi_ref, perm_ref, sall_ref, do_ref, drq, drk, drv, drg,
             ds_scr):
        n = N - 1 - pl.program_id(0)

        @pl.when(pl.program_id(0) == 0)
        def _():
            ds_scr[...] = jnp.zeros_like(ds_scr)

        f = lambda a, b, c, d, s: _ret_chunk(a, b, c, d, s, cos_ref[...], sin_ref[...], in_ref[...], ze_ref[...],
                                             xi_ref[...], perm_ref[...])
        _, vjp = jax.vjp(f, _heads(rq[...]), _heads(rk[...]), _heads(rv[...]), _heads(rg[...]), sall_ref[...])
        g = vjp((_heads(do_ref[...]), ds_scr[...]))
        mask = _row_mask(n)
        for ref, val in zip((drq, drk, drv, drg), g[:4]):
            ref[...] = _unheads(val) * mask
        ds_scr[...] = g[4]

    def rwide(off):
        return pl.BlockSpec((BLK, 4 * LANES), lambda n: (N - 1 - n, off))

    rowtab = pl.BlockSpec((BLK, LANES), lambda n: (N - 1 - n, 0))
    tab = _const_spec((4, BLK, LANES))
    return pl.pallas_call(
        body, name=name, grid=(N,),
        in_specs=[rwide(0), rwide(1), rwide(2), rwide(3), rowtab, rowtab, tab, tab, tab, tab,
                  pl.BlockSpec((None, 4, LANES, LANES), lambda n: (N - 1 - n, 0, 0, 0)), rwide(0)],
        out_specs=[rwide(0)] * 4, out_shape=[jax.ShapeDtypeStruct((T, 512), F32)] * 4,
        scratch_shapes=[pltpu.VMEM((4, LANES, LANES), F32)],
        compiler_params=_params(("arbitrary",)),
    )(p, p, p, p, cos, sin, intra, zeta, xi, perm, sall, dmixed)


def conv_silu_fwd(p, w, send, *, name):
    T = p.shape[0]
    N = T // BLK
    ns = len(send)

    def body(x_ref, xp_ref, w_ref, *refs):
        x_refs, o_ref, g_refs, sems = refs[:ns], refs[ns], refs[ns + 1:2 * ns + 1], refs[2 * ns + 1:]
        n, part = pl.program_id(0), pl.program_id(1)

        @pl.when((n == 0) & (part == 0))
        def _():
            _gather_direct(True, x_refs, g_refs, *sems)

        @pl.when((n == N - 1) & (part == 2))
        def _():
            _gather_direct(False, x_refs, g_refs, *sems)

        cur = x_ref[...]
        cat = jnp.concatenate([jnp.where(n > 0, xp_ref[...], 0.0), cur], axis=0)
        y = w_ref[3:4, :] * cur
        for s in (1, 2, 3):
            y = y + w_ref[3 - s:4 - s, :] * pltpu.roll(cat, s, 0)[BLK:]
        o_ref[...] = _silu(y)

    cw = 4 * LANES
    res = pl.pallas_call(
        body, name=name, grid=(N, 3),
        in_specs=[pl.BlockSpec((BLK, cw), lambda n, c: (n, 4 + c)),
                  pl.BlockSpec((BLK, cw), lambda n, c: (jnp.maximum(n - 1, 0), 4 + c)),
                  pl.BlockSpec((4, cw), lambda n, c: (0, c))] + [HBM_SPEC] * ns,
        out_specs=[pl.BlockSpec((BLK, cw), lambda n, c: (n, c))] + [HBM_SPEC] * ns,
        out_shape=[jax.ShapeDtypeStruct((T, 1536), F32)] + [jax.ShapeDtypeStruct((NDEV,) + x.shape, x.dtype) for x in send],
        scratch_shapes=_gather_direct_sems(ns), compiler_params=_params(("arbitrary", "arbitrary")),
    )(p, p, w, *send)
    return res[0], res[1:]


def conv_silu_bwd(p, w, dact, part, *, name):
    T = p.shape[0]
    N = T // BLK
    cw = 4 * LANES

    def body(xp_ref, x_ref, xn_ref, w_ref, da_ref, dan_ref, dx_ref, dw_ref):
        n = pl.program_id(0)
        last = n == N - 1
        cat = jnp.concatenate([jnp.where(n > 0, xp_ref[...], 0.0), x_ref[...], jnp.where(last, 0.0, xn_ref[...])], axis=0)
        shifted = [cat] + [pltpu.roll(cat, s, 0) for s in (1, 2, 3)]
        y = w_ref[3:4, :] * shifted[0]
        for s in (1, 2, 3):
            y = y + w_ref[3 - s:4 - s, :] * shifted[s]
        y = y[BLK:]
        da = jnp.concatenate([da_ref[...], jnp.where(last, 0.0, dan_ref[...])], axis=0)
        sg = _sigmoid(y)
        dy = da * sg * (1.0 + y * (1.0 - sg))
        dx = w_ref[3:4, :] * dy[:BLK]
        for s in (1, 2, 3):
            dx = dx + w_ref[3 - s:4 - s, :] * pltpu.roll(dy, 2 * BLK - s, 0)[:BLK]
        dx_ref[...] = dx * _row_mask(n)

        @pl.when(n == 0)
        def _():
            dw_ref[...] = jnp.zeros_like(dw_ref)

        for s in (0, 1, 2, 3):
            dw_ref[3 - s:4 - s, :] += jnp.sum(dy[:BLK] * shifted[s][BLK:2 * BLK], axis=0, keepdims=True)

    def xs(d):
        return pl.BlockSpec((BLK, cw), lambda n: (jnp.clip(n + d, 0, N - 1), 4 + part))

    return pl.pallas_call(
        body, name=name, grid=(N,),
        in_specs=[xs(-1), xs(0), xs(1), pl.BlockSpec((4, cw), lambda n: (0, part)),
                  pl.BlockSpec((BLK, cw), lambda n: (n, 0)),
                  pl.BlockSpec((BLK, cw), lambda n: (jnp.minimum(n + 1, N - 1), 0))],
        out_specs=[pl.BlockSpec((BLK, cw), lambda n: (n, 0)), pl.BlockSpec((4, cw), lambda n: (0, 0))],
        out_shape=[jax.ShapeDtypeStruct((T, 512), F32), jax.ShapeDtypeStruct((4, 512), F32)],
        compiler_params=_params(("arbitrary",)),
    )(p, p, p, w, dact, dact)


def _softplus(x):
    return jnp.maximum(x, 0.0) + jnp.log1p(jnp.exp(-jnp.abs(x)))


def _pick4(tile, off):
    return jnp.stack([_lane_pick(tile, off + h) for h in range(4)])


def _spread4(v4, off, rows):
    lane = _iota2((rows, LANES), 1)
    out = jnp.where(lane == off, v4[0], 0.0)
    for h in range(1, 4):
        out = out + jnp.where(lane == off + h, v4[h], 0.0)
    return out


def _gdn_chunk(qa, ka, va, z, braw, araw, S, alog, dtb, onorm, rowmask, lincl):
    r, c = _iota2((BLK, BLK), 0), _iota2((BLK, BLK), 1)
    incl, strict = r >= c, r > c
    eye = (r == c).astype(F32)
    q = qa * lax.rsqrt(jnp.sum(qa * qa, axis=-1, keepdims=True) + EPS) * (128.0 ** -0.5)
    k = ka * lax.rsqrt(jnp.sum(ka * ka, axis=-1, keepdims=True) + EPS)
    beta = _sigmoid(braw) * rowmask
    g = -jnp.exp(alog) * _softplus(araw + dtb) * rowmask
    gc = hdot(lincl, jnp.broadcast_to(g, qa.shape))
    decay = jnp.where(incl, jnp.exp(jnp.where(incl, gc - _t(gc), 0.0)), 0.0)
    kb = k * beta
    amat = jnp.where(strict, bdot(kb, _t(k)) * decay, 0.0)
    m = -amat
    inv = eye + m
    pw = hdot(m, m)
    for t in range(6):
        inv = inv + hdot(inv, pw)
        if t < 5:
            pw = hdot(pw, pw)
    egc = jnp.exp(gc)
    u = hdot(inv, va * beta)
    w = hdot(inv, kb * egc)
    qk = jnp.where(incl, bdot(q, _t(k)) * decay, 0.0)
    glast = gc[..., BLK - 1:BLK, :]
    vnew = u - bdot(w, S)
    o = bdot(q * egc, S) + bdot(qk, vnew)
    S_new = S * jnp.exp(glast) + bdot(_t(k * jnp.exp(glast - gc)), vnew)
    out = o * lax.rsqrt(jnp.mean(o * o, axis=-1, keepdims=True) + EPS) * onorm * _silu(z)
    return out, S_new


def _lincl():
    i = np.arange(BLK)
    return jnp.broadcast_to(jnp.asarray((i[:, None] >= i[None, :]).astype(np.float32)), (4, BLK, BLK))


def gdn_fwd(act, p, alog, dtb, onorm, send, *, name):
    T = p.shape[0]
    N = T // BLK
    ns = len(send)

    def body(qa, ka, va, z, ba, alog_ref, dtb_ref, on_ref, l_ref, *refs):
        x_refs, (out_ref, sall_ref), g_refs = refs[:ns], refs[ns:ns + 2], refs[ns + 2:2 * ns + 2]
        s_scr, sems = refs[2 * ns + 2], refs[2 * ns + 3:]
        n = pl.program_id(0)

        @pl.when(n == 0)
        def _():
            s_scr[...] = jnp.zeros_like(s_scr)
            _gather_direct(True, x_refs, g_refs, *sems)

        @pl.when(n == N - 1)
        def _():
            _gather_direct(False, x_refs, g_refs, *sems)

        S = s_scr[...]
        sall_ref[...] = S
        out, S_new = _gdn_chunk(_heads(qa[...]), _heads(ka[...]), _heads(va[...]), _heads(z[...]), _pick4(ba[...], 0),
                                _pick4(ba[...], 4), S, _pick4(alog_ref[...], 0), _pick4(dtb_ref[...], 0), on_ref[...],
                                _row_mask(n), l_ref[...])
        out_ref[...] = _unheads(out).astype(BF16)
        s_scr[...] = S_new

    vec = _const_spec((1, LANES))
    res = pl.pallas_call(
        body, name=name, grid=(N,),
        in_specs=[_wide(0), _wide(1), _wide(2), _wide(7), pl.BlockSpec((BLK, LANES), lambda n: (n, 32)), vec, vec, vec,
                  _const_spec((4, BLK, BLK))] + [HBM_SPEC] * ns,
        out_specs=[_wide(0), pl.BlockSpec((None, 4, LANES, LANES), lambda n: (n, 0, 0, 0))] + [HBM_SPEC] * ns,
        out_shape=[jax.ShapeDtypeStruct((T, 512), BF16), jax.ShapeDtypeStruct((N, 4, LANES, LANES), F32)]
        + [jax.ShapeDtypeStruct((NDEV,) + x.shape, x.dtype) for x in send],
        scratch_shapes=[pltpu.VMEM((4, LANES, LANES), F32)] + _gather_direct_sems(ns),
        compiler_params=_params(("arbitrary",)),
    )(act, act, act, p, p, alog, dtb, onorm, _lincl(), *send)
    return res[0], res[1], res[2:]


def gdn_bwd(act, p, alog, dtb, onorm, sall, dmixed, partials, *, name):
    T = p.shape[0]
    N = T // BLK
    ns = len(partials)

    def body(qa, ka, va, z, ba, alog_ref, dtb_ref, on_ref, l_ref, sall_ref, do_ref, *refs):
        p_refs, (dq_ref, dk_ref, dv_ref, dz_ref, dba_ref, dal_ref, ddt_ref, don_ref) = refs[:ns], refs[ns:ns + 8]
        got_refs, ds_scr, sems = refs[ns + 8:2 * ns + 8], refs[2 * ns + 8], refs[2 * ns + 9:]
        step = pl.program_id(0)
        n = N - 1 - step

        @pl.when(step == 0)
        def _():
            ds_scr[...] = jnp.zeros_like(ds_scr)
            dal_ref[...] = jnp.zeros_like(dal_ref)
            ddt_ref[...] = jnp.zeros_like(ddt_ref)
            don_ref[...] = jnp.zeros_like(don_ref)
            for cp in _chip_exchange_copies(p_refs, got_refs, *sems):
                cp.start()

        @pl.when(step == N - 1)
        def _():
            for cp in _chip_exchange_copies(p_refs, got_refs, *sems):
                cp.wait()

        rowmask, lincl = _row_mask(n), l_ref[...]
        f = lambda *a: _gdn_chunk(*a, rowmask, lincl)
        _, vjp = jax.vjp(f, _heads(qa[...]), _heads(ka[...]), _heads(va[...]), _heads(z[...]), _pick4(ba[...], 0),
                         _pick4(ba[...], 4), sall_ref[...], _pick4(alog_ref[...], 0), _pick4(dtb_ref[...], 0),
                         on_ref[...])
        g = vjp((_heads(do_ref[...]), ds_scr[...]))
        dq_ref[...] = _unheads(g[0]) * rowmask
        dk_ref[...] = _unheads(g[1]) * rowmask
        dv_ref[...] = _unheads(g[2]) * rowmask
        dz_ref[...] = _unheads(g[3]) * rowmask
        dba_ref[...] = (_spread4(g[4], 0, BLK) + _spread4(g[5], 4, BLK)) * rowmask
        ds_scr[...] = g[6]
        dal_ref[...] += _spread4(g[7], 0, 1)
        ddt_ref[...] += _spread4(g[8], 0, 1)
        don_ref[...] += g[9]

    def rwide(off):
        return pl.BlockSpec((BLK, 4 * LANES), lambda s: (N - 1 - s, off))

    vec = _const_spec((1, LANES))
    col = pl.BlockSpec((BLK, LANES), lambda s: (N - 1 - s, 0))
    res = pl.pallas_call(
        body, name=name, grid=(N,),
        in_specs=[rwide(0), rwide(1), rwide(2), rwide(7), pl.BlockSpec((BLK, LANES), lambda s: (N - 1 - s, 32)), vec, vec,
                  vec, _const_spec((4, BLK, BLK)),
                  pl.BlockSpec((None, 4, LANES, LANES), lambda s: (N - 1 - s, 0, 0, 0)), rwide(1)] + [HBM_SPEC] * ns,
        out_specs=[rwide(0)] * 4 + [col, vec, vec, vec] + [HBM_SPEC] * ns,
        out_shape=[jax.ShapeDtypeStruct((T, 512), F32)] * 4 + [jax.ShapeDtypeStruct((T, LANES), F32)]
        + [jax.ShapeDtypeStruct((1, LANES), F32)] * 3 + _chip_exchange_shapes(partials),
        scratch_shapes=[pltpu.VMEM((4, LANES, LANES), F32)] + _chip_exchange_sems(ns),
        compiler_params=_params(("arbitrary",)),
    )(act, act, act, p, p, alog, dtb, onorm, _lincl(), sall, dmixed, *partials)
    return res[:8], res[8:]


NEG = -1e30


def _swa_block(q, k0, kp, kc, v0, vp, vc, sink, n):
    r, c = _iota2((BLK, BLK), 0), _iota2((BLK, BLK), 1)
    m0 = (c >= PAD) & (c <= n * BLK + r)
    mp = (n >= 2) & (c > r)
    mc = (n >= 1) & (r >= c)
    b = lambda t: jnp.broadcast_to(t, (4,) + t.shape)
    qs = q * (64.0 ** -0.5)
    s0 = jnp.where(m0, bdot(qs, _t(b(k0))), NEG)
    sp = jnp.where(mp, bdot(qs, _t(b(kp))), NEG)
    sc = jnp.where(mc, bdot(qs, _t(b(kc))), NEG)
    mx = jnp.maximum(jnp.max(jnp.maximum(jnp.maximum(s0, sp), sc), axis=-1, keepdims=True), sink)
    mx = lax.stop_gradient(mx)
    p0, pp, pc = jnp.exp(s0 - mx), jnp.exp(sp - mx), jnp.exp(sc - mx)
    den = (jnp.sum(p0, axis=-1, keepdims=True) + jnp.sum(pp, axis=-1, keepdims=True)
           + jnp.sum(pc, axis=-1, keepdims=True) + jnp.exp(sink - mx))
    return (bdot(p0, b(v0)) + bdot(pp, b(vp)) + bdot(pc, b(vc))) / den


def _swa_specs():
    rows = (lambda n: 0, lambda n: jnp.maximum(n - 1, 0), lambda n: n)

    def kv_spec(off, row):
        return pl.BlockSpec((BLK, LANES), lambda g, n: (row(n), off + g))

    q = pl.BlockSpec((BLK, 4 * LANES), lambda g, n: (n, g))
    return q, [kv_spec(off, row) for off in (8, 10) for row in rows]


def swa_fwd(p2, sinkrow, *, name):
    T = p2.shape[0]
    N = T // BLK

    def body(q, k0, kp, kc, v0, vp, vc, sink_ref, o_ref):
        g, n = pl.program_id(0), pl.program_id(1)
        f32 = lambda ref: ref[...].astype(F32)
        o = _swa_block(_heads(f32(q)), f32(k0), f32(kp), f32(kc), f32(v0), f32(vp), f32(vc),
                       _pick4(sink_ref[...], 4 * g), n)
        o_ref[...] = _unheads(o).astype(BF16)

    q, kv = _swa_specs()
    return pl.pallas_call(
        body, name=name, grid=(2, N), in_specs=[q] + kv + [_const_spec((1, LANES))],
        out_specs=q, out_shape=jax.ShapeDtypeStruct((T, 1024), BF16),
        compiler_params=_params(("arbitrary", "arbitrary")),
    )(p2, p2, p2, p2, p2, p2, p2, sinkrow)


def swa_bwd(p2, sinkrow, dmixed, *, name):
    T = p2.shape[0]
    N = T // BLK

    def body(q, k0, kp, kc, v0, vp, vc, sink_ref, do_ref, dq_ref, dk_ref, dv_ref, dsink_ref):
        g, n = pl.program_id(0), pl.program_id(1)

        @pl.when(n == 0)
        def _():
            dk_ref[...] = jnp.zeros_like(dk_ref)
            dv_ref[...] = jnp.zeros_like(dv_ref)

        @pl.when((g == 0) & (n == 0))
        def _():
            dsink_ref[...] = jnp.zeros_like(dsink_ref)

        f = lambda *a: _swa_block(*a, n)
        f32 = lambda ref: ref[...].astype(F32)
        _, vjp = jax.vjp(f, _heads(f32(q)), f32(k0), f32(kp), f32(kc), f32(v0), f32(vp), f32(vc),
                         _pick4(sink_ref[...], 4 * g))
        dq, dk0, dkp, dkc, dv0, dvp, dvc, dsink = vjp(_heads(do_ref[...]))
        dq_ref[...] = _unheads(dq)
        prev = pl.ds(pl.multiple_of(jnp.maximum(n - 1, 0) * BLK, BLK), BLK)
        cur = pl.ds(pl.multiple_of(n * BLK, BLK), BLK)
        for ref, d0, dp, dc in ((dk_ref, dk0, dkp, dkc), (dv_ref, dv0, dvp, dvc)):
            ref[0:BLK, :] += d0
            ref[prev, :] += dp
            ref[cur, :] += dc
        dsink_ref[...] += _spread4(dsink, 4 * g, 1)

    qspec, kv = _swa_specs()
    slab = pl.BlockSpec((T, LANES), lambda g, n: (0, g))
    return pl.pallas_call(
        body, name=name, grid=(2, N), in_specs=[qspec] + kv + [_const_spec((1, LANES)), qspec],
        out_specs=[qspec, slab, slab, _const_spec((1, LANES))],
        out_shape=[jax.ShapeDtypeStruct((T, 1024), F32), jax.ShapeDtypeStruct((T, 256), F32),
                   jax.ShapeDtypeStruct((T, 256), F32), jax.ShapeDtypeStruct((1, LANES), F32)],
        compiler_params=_params(("arbitrary", "arbitrary")),
    )(p2, p2, p2, p2, p2, p2, p2, sinkrow, dmixed)


def _split_dot(x, m):
    rows = x.shape[0]
    hi = x.astype(BF16)
    lo = (x - hi.astype(F32)).astype(BF16)
    r = _nn(jnp.concatenate([hi, lo], axis=0), m)
    return r[:rows] + r[rows:]


def _tri_and_ones(strict, ones=True):
    i = np.arange(BLK)
    m = (i[:, None] > i[None, :]) if strict else (i[:, None] >= i[None, :])
    if ones:
        m = np.concatenate([m, np.ones((BLK, BLK), bool)], axis=1)
    return jnp.asarray(m.astype(np.float32), dtype=BF16)


def _later_and_row_sums(x, m):
    r = _split_dot(x, m)
    if m.shape[1] == 2 * BLK:
        return r[:, :BLK], r[:, BLK:]
    return r, jnp.broadcast_to(jnp.sum(x, axis=1, keepdims=True), x.shape)


SB_PAIR = 2
SB_FWD_GROUP = 4


def _sb_positions():
    r, s = _iota2((BLK, BLK), 0), _iota2((BLK, BLK), 1)
    return s - r, s


def _sb_weights(qbs, ks, base, n, pos, carries, after):
    nh, kb = len(qbs), len(ks[0])
    zs = [[_nt(qbs[h], ks[h][c]) for c in range(kb)] for h in range(nh)]
    valid = [(pos[0] < (n - base - c) * BLK) & (pos[1] >= PAD - (base + c) * BLK) for c in range(kb)]
    lb = [[None] * kb for _ in range(nh)]
    sums = [[None] * kb for _ in range(nh)]
    for c in range(kb):
        for h in range(nh):
            z = jnp.where(valid[c], zs[h][c], NEG)
            lb[h][c] = jnp.minimum(z, 0.0) - jnp.log(1.0 + jnp.exp(-jnp.abs(z)))
            sums[h][c] = _later_and_row_sums(lb[h][c] - z, after)
    a = [[None] * kb for _ in range(nh)]
    carries = list(carries)
    for c in reversed(range(kb)):
        for h in range(nh):
            a[h][c] = jnp.exp(lb[h][c] + carries[h] + sums[h][c][0])
            carries[h] = carries[h] + sums[h][c][1]
    return valid, lb, a, carries


def _key_blocks(n_blocks):
    return next(k for k in (5, 3, 1) if n_blocks % k == 0)


def sb_fwd(p2, *, name):
    T = p2.shape[0]
    N = T // BLK
    kb = _key_blocks(N)
    nh = SB_FWD_GROUP
    heads = [slice(h * LANES, (h + 1) * LANES) for h in range(nh)]

    def body(q_ref, k_ref, v_ref, after_ref, o_ref, of_ref):
        n = pl.program_id(1)
        qbs = [(q_ref[:, hs].astype(F32) * (64.0 ** -0.5)).astype(BF16) for hs in heads]
        after, pos = after_ref[...], _sb_positions()
        nsup = n // kb + 1

        def step(t, c):
            accs, carries = c
            base = (nsup - 1 - t) * kb
            rows = [pl.ds(pl.multiple_of((base + sub) * BLK, BLK), BLK) for sub in range(kb)]
            ks = [[k_ref[r, hs] for r in rows] for hs in heads]
            _, _, a, carries = _sb_weights(qbs, ks, base, n, pos, carries, after)
            accs = list(accs)
            for sub, r in enumerate(rows):
                for h, hs in enumerate(heads):
                    accs[h] = accs[h] + _nn(a[h][sub].astype(BF16), v_ref[r, hs])
            return accs, carries

        zero = [jnp.zeros((BLK, LANES), F32)] * nh
        accs, _ = lax.fori_loop(0, nsup, step, (zero, zero))
        acc = jnp.concatenate(accs, axis=1)
        o_ref[...] = acc.astype(BF16)
        of_ref[...] = acc

    wide = nh * LANES

    def slab(off):
        return pl.BlockSpec((T, wide), lambda g, n: (0, off + g))

    def blk(off):
        return pl.BlockSpec((BLK, wide), lambda g, n: (n, off + g))

    return pl.pallas_call(
        body, name=name, grid=(8 // nh, N),
        in_specs=[blk(12 // nh), slab(20 // nh), slab(28 // nh), _const_spec((BLK, BLK))],
        out_specs=[blk(0), blk(0)],
        out_shape=[jax.ShapeDtypeStruct((T, 1024), BF16), jax.ShapeDtypeStruct((T, 1024), F32)],
        compiler_params=_params(("arbitrary", "arbitrary")),
    )(p2, p2, p2, _tri_and_ones(True, ones=False))


def sb_bwd(p2, o, dmixed, *, name):
    T = p2.shape[0]
    N = T // BLK
    kb = _key_blocks(N)

    heads = [slice(h * LANES, (h + 1) * LANES) for h in range(SB_PAIR)]
    scale = 64.0 ** -0.5

    def body(q_ref, k_ref, v_ref, after_ref, from_ref, o_ref, do_ref, dq_ref, dk_ref, dv_ref, dkt_scr, dvt_scr):
        n = pl.program_id(1)

        @pl.when(n == 0)
        def _():
            dkt_scr[...] = jnp.zeros_like(dkt_scr)
            dvt_scr[...] = jnp.zeros_like(dvt_scr)

        qbs, qts, dobs, dots, totals = [], [], [], [], []
        for hs in heads:
            qs = q_ref[:, hs].astype(F32) * scale
            do = do_ref[:, hs]
            qbs.append(qs.astype(BF16))
            qts.append(qs.T.astype(BF16))
            dobs.append(do.astype(BF16))
            dots.append(do.T.astype(BF16))
            total = jnp.sum(dobs[-1].astype(F32) * o_ref[:, hs], axis=1, keepdims=True)
            totals.append(jnp.broadcast_to(total, (BLK, LANES)))
        after, frm, pos = after_ref[...], from_ref[...], _sb_positions()
        nsup = n // kb + 1

        def step(t, c):
            dqs, carries, gcarries = c
            base = (nsup - 1 - t) * kb
            rows = [pl.ds(pl.multiple_of((base + sub) * BLK, BLK), BLK) for sub in range(kb)]
            ks = [[k_ref[r, hs] for r in rows] for hs in heads]
            valid, lb, a, carries = _sb_weights(qbs, ks, base, n, pos, carries, after)
            das = [[_nt(dobs[h], v_ref[r, hs]) for r in rows] for h, hs in enumerate(heads)]
            ab = [[a[h][sub].astype(BF16) for sub in range(kb)] for h in range(SB_PAIR)]
            g = [[None] * kb for _ in heads]
            sums = [[None] * kb for _ in heads]
            for sub in range(kb):
                for h in range(SB_PAIR):
                    g[h][sub] = das[h][sub] * ab[h][sub].astype(F32)
                    sums[h][sub] = _later_and_row_sums(g[h][sub], frm)
            dqs, gcarries = list(dqs), list(gcarries)
            for sub in reversed(range(kb)):
                for h in range(SB_PAIR):
                    before = totals[h] - (gcarries[h] + sums[h][sub][0])
                    gcarries[h] = gcarries[h] + sums[h][sub][1]
                    beta = jnp.exp(lb[h][sub])
                    dz = (g[h][sub] - beta * (g[h][sub] + before)).astype(BF16)
                    dqs[h] = dqs[h] + _nn(dz, ks[h][sub])
                    dkt_scr[h * N + base + sub] += _nn(qts[h], dz)
                    dvt_scr[h * N + base + sub] += _nn(dots[h], ab[h][sub])
            return dqs, carries, gcarries

        zero = [jnp.zeros((BLK, LANES), F32)] * SB_PAIR
        dqs, _, _ = lax.fori_loop(0, nsup, step, (zero, zero, zero))
        dq_ref[...] = (jnp.concatenate(dqs, axis=1) * scale).astype(dq_ref.dtype)

        @pl.when(n == N - 1)
        def _():
            def flush(j, _):
                rows = pl.ds(pl.multiple_of(j * BLK, BLK), BLK)
                for h, hs in enumerate(heads):
                    dk_ref[rows, hs] = dkt_scr[h * N + j].T.astype(dk_ref.dtype)
                    dv_ref[rows, hs] = dvt_scr[h * N + j].T.astype(dv_ref.dtype)
                return 0

            lax.fori_loop(0, N, flush, 0)

    wide = SB_PAIR * LANES

    def slab(off):
        return pl.BlockSpec((T, wide), lambda g, n: (0, off + g))

    def blk(off):
        return pl.BlockSpec((BLK, wide), lambda g, n: (n, off + g))

    tri = _const_spec((BLK, BLK))
    return pl.pallas_call(
        body, name=name, grid=(8 // SB_PAIR, N),
        in_specs=[blk(12 // SB_PAIR), slab(20 // SB_PAIR), slab(28 // SB_PAIR), tri, tri, blk(0), blk(8 // SB_PAIR)],
        out_specs=[blk(0), slab(0), slab(0)],
        out_shape=[jax.ShapeDtypeStruct((T, 1024), BF16)] * 3,
        scratch_shapes=[pltpu.VMEM((SB_PAIR * N, LANES, LANES), F32), pltpu.VMEM((SB_PAIR * N, LANES, LANES), F32)],
        compiler_params=_params(("arbitrary", "arbitrary")),
    )(p2, p2, p2, _tri_and_ones(True, ones=False), _tri_and_ones(False, ones=False), o, dmixed)


def ffn_fwd(h, g_pre, g_post, wg, wu, wd, tag):
    u, gate, up, act = norm_mm(h, g_pre, (wg, wu), swiglu=True, wt=True, name=f"ffn_up_{tag}")
    y, h_new = mm_norm_res([act], [wd], h, g_post, 0.5, name=f"ffn_down_{tag}")
    return h_new, (h, u, gate, up, y)


def ffn_bwd(saved, dh, g_pre, g_post, wg, wu, wd, tag):
    h, u, gate, up, y = saved
    dy, dg_post, dgate, dup, act = normbwd_mm_nt(dh, y, g_post, wd, 0.5, (gate, up), name=f"ffn_bwd_down_{tag}")
    dwd = mm_tn(act, dy, name=f"ffn_dwd_{tag}")
    dwg = mm_tn(dgate, u, name=f"ffn_dwg_{tag}")
    dwu = mm_tn(dup, u, name=f"ffn_dwu_{tag}")
    dh_in, dg_pre = mm_nt_normbwd([dgate, dup], [wg, wu], h, g_pre, dh, wt=True, name=f"ffn_bwd_up_{tag}")
    return dh_in, (dg_pre, dg_post), (dwg, dwu, dwd)


def _lane_row(v):
    v = v.reshape(1, -1)
    return jnp.pad(v, ((0, 0), (0, LANES - v.shape[1])))


AB_WIDTHS = (512,) * 8 + (LANES,)


def mixer_ab_fwd(h, g_pre, g_post, w_in, conv_w, a_log, dt_bias, out_norm, w_out, tables, send_conv, send_gdn):
    u, p = norm_mm(h, g_pre, (w_in,), swiglu=False, name="ab_in")
    ret, sall_r = retention_fwd(p, tables, name="retention_fwd")
    act, gathered_conv = conv_silu_fwd(p, conv_w, send_conv, name="conv_fwd")
    gdn, sall_g, gathered_gdn = gdn_fwd(act, p, _lane_row(a_log), _lane_row(dt_bias), out_norm.reshape(1, LANES),
                                        send_gdn, name="gdn_fwd")
    y, h_new = mm_norm_res([ret, gdn], [w_out[:512], w_out[512:]], h, g_post, 1.0, name="ab_out")
    return h_new, (h, u, p, ret, sall_r, act, gdn, sall_g, y), list(gathered_conv) + list(gathered_gdn)


def mixer_ab_bwd(saved, dh, g_pre, g_post, w_in, conv_w, a_log, dt_bias, out_norm, w_out, tables, partials):
    h, u, p, ret, sall_r, act, gdn, sall_g, y = saved
    dy, dg_post, dmixed = normbwd_mm_nt(dh, y, g_post, w_out, 1.0, name="ab_bwd_out")
    dw_out = jnp.concatenate([mm_tn(ret, dy, name="ab_dwout_ret"), mm_tn(gdn, dy, name="ab_dwout_gdn")], axis=0)
    pieces = list(retention_bwd(p, sall_r, dmixed, tables, name="retention_bwd"))
    (dqa, dka, dva, dz, dba, dalog, ddtb, donorm), arrived = gdn_bwd(
        act, p, _lane_row(a_log), _lane_row(dt_bias), out_norm.reshape(1, LANES), sall_g, dmixed, partials,
        name="gdn_bwd")
    dconv = []
    for part, dact in enumerate((dqa, dka, dva)):
        dx, dw = conv_silu_bwd(p, conv_w, dact, part, name=f"conv_bwd_{part}")
        pieces.append(dx)
        dconv.append(dw)
    pieces += [dz, dba]
    offs = np.cumsum((0,) + AB_WIDTHS)
    w_parts = [w_in[:, a:b] for a, b in zip(offs[:-1], offs[1:])]
    dh_in, dg_pre = mm_nt_normbwd(pieces, w_parts, h, g_pre, dh, name="ab_bwd_in")
    dw_in = jnp.concatenate([mm_tn(u, pc, name=f"ab_dwin_{i}") for i, pc in enumerate(pieces)], axis=1)
    small = (jnp.concatenate(dconv, axis=1), dalog[:, :4], ddtb[:, :4], donorm)
    return dh_in, (dg_pre, dg_post), (dw_in, dw_out), small, arrived


CD_WIDTHS = (1024, 256, 256, 1024, 1024, 1024)


def mixer_cd_fwd(h, g_pre, g_post, w_in, sinks, w_out):
    u, p2 = norm_mm(h, g_pre, (w_in,), swiglu=False, out_dtype=BF16, name="cd_in")
    swa = swa_fwd(p2, _lane_row(sinks), name="swa_fwd")
    sb, sb_f32 = sb_fwd(p2, name="sb_fwd")
    y, h_new = mm_norm_res([swa, sb], [w_out[:1024], w_out[1024:]], h, g_post, 1.0, name="cd_out")
    return h_new, (h, u, p2, swa, sb, sb_f32, y)


def mixer_cd_bwd(saved, dh, g_pre, g_post, w_in, sinks, w_out):
    h, u, p2, swa, sb, sb_f32, y = saved
    dy, dg_post, dmixed = normbwd_mm_nt(dh, y, g_post, w_out, 1.0, name="cd_bwd_out")
    dw_out = jnp.concatenate([mm_tn(swa, dy, name="cd_dwout_swa"), mm_tn(sb, dy, name="cd_dwout_sb")], axis=0)
    dq_c, dk_c, dv_c, dsink = swa_bwd(p2, _lane_row(sinks), dmixed, name="swa_bwd")
    pieces = [dq_c, dk_c, dv_c] + list(sb_bwd(p2, sb_f32, dmixed, name="sb_bwd"))
    offs = np.cumsum((0,) + CD_WIDTHS)
    w_parts = [w_in[:, a:b] for a, b in zip(offs[:-1], offs[1:])]
    dh_in, dg_pre = mm_nt_normbwd(pieces, w_parts, h, g_pre, dh, name="cd_bwd_in")
    dw_in = jnp.concatenate([mm_tn(u, pc, name=f"cd_dwin_{i}") for i, pc in enumerate(pieces)], axis=1)
    return dh_in, (dg_pre, dg_post), (dw_in, dw_out), dsink[:, :8]


def _pad_heads(w, axis):
    shape = w.shape
    w = w.reshape(shape[:axis] + (shape[axis] // 64, 64) + shape[axis + 1:])
    pad = [(0, 0)] * w.ndim
    pad[axis + 1] = (0, 64)
    return jnp.pad(w, pad).reshape(shape[:axis] + (2 * shape[axis],) + shape[axis + 1:])


def _unpad_heads(w, axis):
    shape = w.shape
    w = w.reshape(shape[:axis] + (shape[axis] // 128, 128) + shape[axis + 1:])
    w = lax.slice_in_dim(w, 0, 64, axis=axis + 1)
    return w.reshape(shape[:axis] + (shape[axis] // 2,) + shape[axis + 1:])


SMALL_SHARDED = (("meta_tokens", (NMETA, LANES), 1), ("norm_gains", (2, 6, LANES), 2), ("ab_conv_w", (1, 4, 192), 2))
SMALL_REPL = (("ab_a_log", (1, 4)), ("ab_dt_bias", (1, 4)), ("ab_out_norm", (1, LANES)), ("cd_sinks", (1, 8)))


def _stack_shards(g, axis):
    full = jnp.moveaxis(g, 0, axis)
    shape = full.shape
    return full.reshape(shape[:axis] + (shape[axis] * shape[axis + 1],) + shape[axis + 2:])


def _split_shards(full, axis):
    shape = full.shape
    g = full.reshape(shape[:axis] + (NDEV, shape[axis] // NDEV) + shape[axis + 1:])
    return jnp.moveaxis(g, axis, 0)


def _pad_rows8(a):
    rows = []
    for x in a:
        flat = x.reshape(x.shape[0], -1)
        n = -(-flat.shape[1] // LANES) * LANES
        rows.append(jnp.pad(flat, ((0, 0), (0, n - flat.shape[1]))).reshape(x.shape[0], n // LANES, LANES))
    cat = jnp.concatenate(rows, axis=1)
    return jnp.pad(cat, ((0, 0), (0, -cat.shape[1] % 8), (0, 0)))


def _unpad_rows8(packed, shapes):
    out, at = [], 0
    for shape in shapes:
        size = int(np.prod(shape))
        nrow = -(-size // LANES)
        blk = packed[:, at:at + nrow].reshape(packed.shape[0], -1)[:, :size]
        out.append(blk.reshape((packed.shape[0],) + tuple(shape)))
        at += nrow
    return out


def kernel(x, meta_tokens, norm_gains, ffn_w_gate, ffn_w_up, ffn_w_down, ab_w_in, ab_conv_w, ab_a_log, ab_dt_bias, ab_out_norm, ab_w_out, cd_w_in, cd_sinks, cd_w_out, loss_target, m_meta_tokens, m_norm_gains, m_ffn_w_gate, m_ffn_w_up, m_ffn_w_down, m_ab_w_in, m_ab_conv_w, m_ab_a_log, m_ab_dt_bias, m_ab_out_norm, m_ab_w_out, m_cd_w_in, m_cd_sinks, m_cd_w_out, v_meta_tokens, v_norm_gains, v_ffn_w_gate, v_ffn_w_up, v_ffn_w_down, v_ab_w_in, v_ab_conv_w, v_ab_a_log, v_ab_dt_bias, v_ab_out_norm, v_ab_w_out, v_cd_w_in, v_cd_sinks, v_cd_w_out):
    w = dict(meta_tokens=meta_tokens, norm_gains=norm_gains, ffn_w_gate=ffn_w_gate, ffn_w_up=ffn_w_up,
             ffn_w_down=ffn_w_down, ab_w_in=ab_w_in, ab_conv_w=ab_conv_w, ab_a_log=ab_a_log, ab_dt_bias=ab_dt_bias,
             ab_out_norm=ab_out_norm, ab_w_out=ab_w_out, cd_w_in=cd_w_in, cd_sinks=cd_sinks, cd_w_out=cd_w_out)
    m = dict(meta_tokens=m_meta_tokens, norm_gains=m_norm_gains, ffn_w_gate=m_ffn_w_gate, ffn_w_up=m_ffn_w_up,
             ffn_w_down=m_ffn_w_down, ab_w_in=m_ab_w_in, ab_conv_w=m_ab_conv_w, ab_a_log=m_ab_a_log,
             ab_dt_bias=m_ab_dt_bias, ab_out_norm=m_ab_out_norm, ab_w_out=m_ab_w_out, cd_w_in=m_cd_w_in,
             cd_sinks=m_cd_sinks, cd_w_out=m_cd_w_out)
    v = dict(meta_tokens=v_meta_tokens, norm_gains=v_norm_gains, ffn_w_gate=v_ffn_w_gate, ffn_w_up=v_ffn_w_up,
             ffn_w_down=v_ffn_w_down, ab_w_in=v_ab_w_in, ab_conv_w=v_ab_conv_w, ab_a_log=v_ab_a_log,
             ab_dt_bias=v_ab_dt_bias, ab_out_norm=v_ab_out_norm, ab_w_out=v_ab_w_out, cd_w_in=v_cd_w_in,
             cd_sinks=v_cd_sinks, cd_w_out=v_cd_w_out)
    order = list(w)
    S = x.shape[1]
    T = S + BLK

    fs = DFF // NDEV

    def ffn_local(i, j):
        return jnp.concatenate([ffn_w_gate[i, j].T, ffn_w_up[i, j].T, ffn_w_down[i, j]], axis=0).astype(BF16)

    wg, wu, wd = {}, {}, {}

    def ffn_gathered(gathered, ij):
        for kind, full_w in enumerate((wg, wu, wd)):
            full_w[ij] = gathered[:, kind * fs:(kind + 1) * fs].reshape(DFF, D)

    ffn00_all, abin_all, about_all = all_gather_big(
        [ffn_local(0, 0), ab_w_in[0].astype(BF16), ab_w_out[0].astype(BF16)], name="gather_first")
    ffn_gathered(ffn00_all, (0, 0))
    ab_in = jnp.pad(_stack_shards(abin_all, 1), ((0, 0), (0, AB_INP - AB_IN)))
    ab_out = about_all.reshape(D, D)
    under_conv = [ffn_local(0, 1)]
    under_gdn = [ffn_local(1, 0), ffn_local(1, 1), cd_w_in[0].astype(BF16), cd_w_out[0].astype(BF16)]
    small_src = jnp.broadcast_to(_pad_rows8([w[n][None] for n, _, _ in SMALL_SHARDED]), (NDEV, 40, LANES))
    small_all = _unpad_rows8(all_to_all_small(small_src, name="gather_small"), [s for _, s, _ in SMALL_SHARDED])
    full = {n: _stack_shards(g, ax) for (n, _, ax), g in zip(SMALL_SHARDED, small_all)}
    conv_w = full["ab_conv_w"][0]
    gains = full["norm_gains"].reshape(2, 6, 1, D)
    tables = retention_tables(T)

    h = jnp.concatenate([jnp.zeros((PAD, D), F32), full["meta_tokens"], x[0]], axis=0)
    h, s00 = ffn_fwd(h, gains[0, 0], gains[0, 1], wg[0, 0], wu[0, 0], wd[0, 0], "00")
    h, sab, later_part = mixer_ab_fwd(h, gains[0, 2], gains[0, 3], ab_in, conv_w, ab_a_log, ab_dt_bias, ab_out_norm,
                                      ab_out, tables, under_conv, under_gdn)
    ffn01_all, ffn10_all, ffn11_all, cdin_all, cdout_all = gather_forward_to_sibling(later_part, name="gather_finish")
    for ij, gathered in (((0, 1), ffn01_all), ((1, 0), ffn10_all), ((1, 1), ffn11_all)):
        ffn_gathered(gathered, ij)
    cd_in = _pad_heads(_stack_shards(cdin_all, 1), 1)
    cd_out = _pad_heads(cdout_all.reshape(D, D), 0)
    h, s01 = ffn_fwd(h, gains[0, 4], gains[0, 5], wg[0, 1], wu[0, 1], wd[0, 1], "01")
    h, s10 = ffn_fwd(h, gains[1, 0], gains[1, 1], wg[1, 0], wu[1, 0], wd[1, 0], "10")
    h, scd = mixer_cd_fwd(h, gains[1, 2], gains[1, 3], cd_in, cd_sinks, cd_out)
    h, s11 = ffn_fwd(h, gains[1, 4], gains[1, 5], wg[1, 1], wu[1, 1], wd[1, 1], "11")
    loss_tile, dh = loss_and_grad(h, loss_target[0], name="loss")
    loss = lax.psum(loss_tile[0, 0], ("x", "y", "c"))

    dgain = [[None] * 6, [None] * 6]
    dffn = {}
    dh, (dgain[1][4], dgain[1][5]), dffn[1, 1] = ffn_bwd(s11, dh, gains[1, 4], gains[1, 5], wg[1, 1], wu[1, 1], wd[1, 1], "11")
    dh, (dgain[1][2], dgain[1][3]), (dcd_in, dcd_out), dsinks = mixer_cd_bwd(scd, dh, gains[1, 2], gains[1, 3], cd_in, cd_sinks, cd_out)
    dh, (dgain[1][0], dgain[1][1]), dffn[1, 0] = ffn_bwd(s10, dh, gains[1, 0], gains[1, 1], wg[1, 0], wu[1, 0], wd[1, 0], "10")
    dh, (dgain[0][4], dgain[0][5]), dffn[0, 1] = ffn_bwd(s01, dh, gains[0, 4], gains[0, 5], wg[0, 1], wu[0, 1], wd[0, 1], "01")
    ffn_send = lambda ij: jnp.concatenate([t.astype(BF16).reshape(NDEV, fs, D) for t in dffn[ij]], axis=1)
    early = [(0, 1), (1, 0), (1, 1)]
    early_send = [ffn_send(ij) for ij in early] + [_split_shards(_unpad_heads(dcd_in, 1).astype(BF16), 1),
                                                   _unpad_heads(dcd_out, 0).astype(BF16).reshape(NDEV, D // NDEV, D)]
    early_sib = rs_exchange_sibling(early_send, name="rs_sibling_early")
    early_parts = [rs_chip_partials(g, t, name=f"rs_chip_partials_early_{i}")
                   for i, (g, t) in enumerate(zip(early_send, early_sib))]
    dh, (dgain[0][2], dgain[0][3]), (dab_in, dab_out), (dconv, dalog, ddtb, donorm), early_got = mixer_ab_bwd(
        sab, dh, gains[0, 2], gains[0, 3], ab_in, conv_w, ab_a_log, ab_dt_bias, ab_out_norm, ab_out, tables, early_parts)
    dh, (dgain[0][0], dgain[0][1]), dffn[0, 0] = ffn_bwd(s00, dh, gains[0, 0], gains[0, 1], wg[0, 0], wu[0, 0], wd[0, 0], "00")
    grad_x = dh[BLK:][None]

    gfull = dict(meta_tokens=dh[PAD:BLK], norm_gains=jnp.stack([jnp.concatenate(r, axis=0) for r in dgain]),
                 ab_conv_w=dconv[None])
    early_g = [rs_final_sum(p, t, name=f"rs_final_sum_early_{i}") for i, (p, t) in enumerate(zip(early_parts, early_got))]
    ffn00_g, abin_g, about_g = reduce_scatter_big(
        [ffn_send((0, 0)), _split_shards(dab_in[:, :AB_IN].astype(BF16), 1),
         dab_out.astype(BF16).reshape(NDEV, D // NDEV, D)])
    cdin_g, cdout_g = early_g[3:]
    ffn_by = dict(zip(early, early_g[:3]))
    ffn_by[0, 0] = ffn00_g
    ffn_g = jnp.stack([jnp.stack([ffn_by[i, j].reshape(3, fs, D) for j in range(2)]) for i in range(2)])
    ffn_g = jnp.moveaxis(ffn_g, 2, 0)
    grads = dict(ffn_w_gate=jnp.swapaxes(ffn_g[0], 2, 3), ffn_w_up=jnp.swapaxes(ffn_g[1], 2, 3), ffn_w_down=ffn_g[2],
                 ab_w_in=abin_g[None], ab_w_out=about_g[None], cd_w_in=cdin_g[None], cd_w_out=cdout_g[None])
    repl = [jnp.broadcast_to(t[None], (NDEV,) + t.shape) for t in (dalog, ddtb, donorm, dsinks)]
    ssend = _pad_rows8([_split_shards(gfull[n], ax) for n, _, ax in SMALL_SHARDED] + repl)
    ssum = sum_slots(all_to_all_small(ssend, name="exchange_small_grads"), name="sum_small_grads")[None]
    small = _unpad_rows8(ssum, [s for _, s, _ in SMALL_SHARDED] + [s for _, s in SMALL_REPL])
    grads.update({n: g[0] for n, g in zip([n for n, _, _ in SMALL_SHARDED] + [n for n, _ in SMALL_REPL], small)})

    delta, new_m, new_v = {}, {}, {}
    for n in order:
        shape = w[n].shape
        view = (-1, shape[-1])
        d_, m_, v_ = adamw(w[n].reshape(view), grads[n].reshape(view), m[n].reshape(view), v[n].reshape(view),
                           name=f"adamw_{n}")
        delta[n], new_m[n], new_v[n] = d_.reshape(shape), m_.reshape(shape), v_.reshape(shape)
    return (loss, grad_x, *[grads[n] for n in order], *[delta[n] for n in order], *[new_m[n] for n in order],
            *[new_v[n] for n in order])
```

```python
import numpy as np
import jax
import jax.numpy as jnp
from jax import lax
from jax.experimental import pallas as pl
from jax.experimental.pallas import tpu as pltpu

F32, BF16 = jnp.float32, jnp.bfloat16
EPS = 1e-6
D = 1024
NMETA = 16
BLK = 128
PAD = BLK - NMETA
DFF = 2816
LANES = 128
NDEV = 8
AB_IN, AB_INP = 4104, 4224
ADAM_LR, ADAM_B1, ADAM_B2, ADAM_EPS, ADAM_WD, ADAM_STEP = 0.001, 0.9, 0.999, 1e-08, 0.01, 10
VMEM_LIMIT = 56 * 1024 * 1024
MESH = pl.DeviceIdType.MESH
HIGH = lax.Precision.HIGH


def _params(sem):
    return pltpu.CompilerParams(dimension_semantics=sem, vmem_limit_bytes=VMEM_LIMIT)


def _row_tile(T, streamed, resident):
    for tm in (640, 320, 128):
        if T % tm == 0 and 2 * (tm * streamed + resident) <= VMEM_LIMIT - 14 * 1024 * 1024:
            return tm
    return _tile(T, 128)


MXU_COLS = 256


def _col_chunks(n):
    return [slice(c, min(c + MXU_COLS, n)) for c in range(0, n, MXU_COLS)]


def _tile(n, cap, unit=LANES):
    if n <= cap:
        return n
    best = None
    for t in range(unit, cap + 1, unit):
        if n % t == 0:
            best = t
    assert best is not None, (n, cap)
    return best


def _rms_fwd(x, g):
    return x * lax.rsqrt(jnp.mean(x * x, axis=-1, keepdims=True) + EPS) * g


def _rms_bwd(x, g, dz):
    r = lax.rsqrt(jnp.mean(x * x, axis=-1, keepdims=True) + EPS)
    xh = x * r
    dg = jnp.sum(dz * xh, axis=0, keepdims=True)
    t = dz * g
    return r * (t - xh * jnp.mean(t * xh, axis=-1, keepdims=True)), dg


def _sigmoid(x):
    return 0.5 * jnp.tanh(0.5 * x) + 0.5


def _silu(x):
    return x * _sigmoid(x)


def _nn(a, b, precision=None):
    return lax.dot_general(a, b, (((1,), (0,)), ((), ())), preferred_element_type=F32, precision=precision)


def _nt(a, b):
    return lax.dot_general(a, b, (((1,), (1,)), ((), ())), preferred_element_type=F32)


def _tn(a, b):
    return lax.dot_general(a, b, (((0,), (0,)), ((), ())), preferred_element_type=F32)


def _mm(a, b, precision=None):
    if a.ndim == 3:
        return lax.dot_general(a, b, (((2,), (1,)), ((0,), (0,))), preferred_element_type=F32, precision=precision)
    return _nn(a, b, precision)


def _t(x):
    return jnp.swapaxes(x, -1, -2)


@jax.custom_vjp
def bdot(a, b):
    return _mm(a.astype(BF16), b.astype(BF16))


def _bdot_fwd(a, b):
    return bdot(a, b), (a, b)


def _bdot_bwd(res, g):
    a, b = res
    return bdot(g, _t(b)), bdot(_t(a), g)


bdot.defvjp(_bdot_fwd, _bdot_bwd)


@jax.custom_vjp
def hdot(a, b):
    return _mm(a, b, HIGH)


def _hdot_fwd(a, b):
    return hdot(a, b), (a, b)


def _hdot_bwd(res, g):
    a, b = res
    return hdot(g, _t(b)), hdot(_t(a), g)


hdot.defvjp(_hdot_fwd, _hdot_bwd)


def _iota2(shape, axis):
    return lax.broadcasted_iota(jnp.int32, shape, axis)


def _lane_pick(row, lane):
    return jnp.sum(jnp.where(_iota2(row.shape, 1) == lane, row, 0.0), axis=1, keepdims=True)


def norm_mm(h, gain, ws, *, swiglu, name, wt=False, out_dtype=F32):
    T, Dm = h.shape
    N = ws[0].shape[0 if wt else 1]
    tm, tn = _tile(T, 640), _tile(N, 1408)
    nw = len(ws)
    mm = _nt if wt else _nn

    def body(h_ref, g_ref, *refs):
        w_refs, u_ref, o_refs = refs[:nw], refs[nw], refs[nw + 1:]

        @pl.when(pl.program_id(1) == 0)
        def _():
            u_ref[...] = _rms_fwd(h_ref[...], g_ref[...]).astype(BF16)

        u = u_ref[...]
        for cols in _col_chunks(tn):
            acc = [mm(u, w[cols, :] if wt else w[:, cols]) for w in w_refs]
            if swiglu:
                o_refs[0][:, cols] = acc[0].astype(BF16)
                o_refs[1][:, cols] = acc[1].astype(BF16)
                o_refs[2][:, cols] = (_silu(acc[0]) * acc[1]).astype(BF16)
            else:
                o_refs[0][:, cols] = acc[0].astype(out_dtype)

    row = pl.BlockSpec((tm, Dm), lambda i, j: (i, 0))
    tile = pl.BlockSpec((tm, tn), lambda i, j: (i, j))
    if swiglu:
        out_shape = [jax.ShapeDtypeStruct((T, Dm), BF16)] + [jax.ShapeDtypeStruct((T, N), BF16)] * 3
        out_specs = [row, tile, tile, tile]
    else:
        out_shape = [jax.ShapeDtypeStruct((T, Dm), BF16), jax.ShapeDtypeStruct((T, N), out_dtype)]
        out_specs = [row, tile]
    return pl.pallas_call(
        body, name=name, grid=(T // tm, N // tn),
        in_specs=[row, pl.BlockSpec((1, Dm), lambda i, j: (0, 0))]
        + [pl.BlockSpec((tn, Dm), lambda i, j: (j, 0)) if wt else pl.BlockSpec((Dm, tn), lambda i, j: (0, j))] * nw,
        out_specs=out_specs, out_shape=out_shape,
        compiler_params=_params(("arbitrary", "arbitrary")),
    )(h, gain, *ws)


def mm_norm_res(As, Ws, h, gain, scale, *, name):
    T, Dm = h.shape
    n = len(As)
    tm = _row_tile(T, sum(a.shape[1] * a.dtype.itemsize for a in As) + 3 * Dm * 4,
                   sum(w.size * w.dtype.itemsize for w in Ws))

    def body(*refs):
        a_refs, w_refs = refs[:n], refs[n:2 * n]
        h_ref, g_ref, y_ref, hn_ref = refs[2 * n:]
        y = _nn(a_refs[0][...].astype(BF16), w_refs[0][...])
        for a, w in zip(a_refs[1:], w_refs[1:]):
            y = y + _nn(a[...].astype(BF16), w[...])
        y_ref[...] = y
        hn_ref[...] = h_ref[...] + scale * _rms_fwd(y, g_ref[...])

    row = pl.BlockSpec((tm, Dm), lambda i: (i, 0))
    return pl.pallas_call(
        body, name=name, grid=(T // tm,),
        in_specs=[pl.BlockSpec((tm, a.shape[1]), lambda i: (i, 0)) for a in As]
        + [pl.BlockSpec(w.shape, lambda i: (0, 0)) for w in Ws]
        + [row, pl.BlockSpec((1, Dm), lambda i: (0, 0))],
        out_specs=[row, row], out_shape=[jax.ShapeDtypeStruct((T, Dm), F32)] * 2,
        compiler_params=_params(("arbitrary",)),
    )(*As, *Ws, h, gain)


def normbwd_mm_nt(dh, y, gain, w, scale, gu=None, *, name):
    T, Dm = dh.shape
    N = w.shape[0]
    tm, tn = _tile(T, 640), _tile(N, 1408)
    swiglu = gu is not None

    def body(dh_ref, y_ref, g_ref, w_ref, *refs):
        if swiglu:
            gate_ref, up_ref, dy_ref, dg_ref, dgate_ref, dup_ref, a_ref = refs
        else:
            dy_ref, dg_ref, da_ref = refs
        i, j = pl.program_id(0), pl.program_id(1)

        @pl.when(j == 0)
        def _():
            dy, dg = _rms_bwd(y_ref[...], g_ref[...], scale * dh_ref[...])
            dy_ref[...] = dy.astype(BF16)

            @pl.when(i == 0)
            def _():
                dg_ref[...] = jnp.zeros_like(dg_ref)

            dg_ref[...] += dg

        dy = dy_ref[...]
        for cols in _col_chunks(tn):
            da = _nt(dy, w_ref[cols, :])
            if swiglu:
                gate, up = gate_ref[:, cols].astype(F32), up_ref[:, cols].astype(F32)
                s = _sigmoid(gate)
                dgate_ref[:, cols] = (da * up * s * (1.0 + gate * (1.0 - s))).astype(BF16)
                dup_ref[:, cols] = (da * gate * s).astype(BF16)
                a_ref[:, cols] = (gate * s * up).astype(BF16)
            else:
                da_ref[:, cols] = da

    row = pl.BlockSpec((tm, Dm), lambda i, j: (i, 0))
    vec = pl.BlockSpec((1, Dm), lambda i, j: (0, 0))
    tile = pl.BlockSpec((tm, tn), lambda i, j: (i, j))
    in_specs = [row, row, vec, pl.BlockSpec((tn, Dm), lambda i, j: (j, 0))]
    out_shape = [jax.ShapeDtypeStruct((T, Dm), BF16), jax.ShapeDtypeStruct((1, Dm), F32)]
    if swiglu:
        in_specs += [tile, tile]
        out_shape += [jax.ShapeDtypeStruct((T, N), BF16)] * 3
        out_specs = [row, vec, tile, tile, tile]
        args = (dh, y, gain, w, *gu)
    else:
        out_shape += [jax.ShapeDtypeStruct((T, N), F32)]
        out_specs = [row, vec, tile]
        args = (dh, y, gain, w)
    return pl.pallas_call(
        body, name=name, grid=(T // tm, N // tn), in_specs=in_specs, out_specs=out_specs,
        out_shape=out_shape, compiler_params=_params(("arbitrary", "arbitrary")),
    )(*args)


def mm_nt_normbwd(dPs, Ws, h, gain, dh_in, *, name, wt=False):
    T, Dm = h.shape
    n = len(dPs)
    tm = _row_tile(T, sum(p.shape[1] * p.dtype.itemsize for p in dPs) + 3 * Dm * 4,
                   sum(w.size * w.dtype.itemsize for w in Ws))
    mm = _nn if wt else _nt

    def body(*refs):
        p_refs, w_refs = refs[:n], refs[n:2 * n]
        h_ref, g_ref, dhin_ref, dh_ref, dg_ref = refs[2 * n:]
        du = mm(p_refs[0][...].astype(BF16), w_refs[0][...])
        for p, w in zip(p_refs[1:], w_refs[1:]):
            du = du + mm(p[...].astype(BF16), w[...])
        dx, dg = _rms_bwd(h_ref[...], g_ref[...], du)
        dh_ref[...] = dhin_ref[...] + dx

        @pl.when(pl.program_id(0) == 0)
        def _():
            dg_ref[...] = jnp.zeros_like(dg_ref)

        dg_ref[...] += dg

    row = pl.BlockSpec((tm, Dm), lambda i: (i, 0))
    vec = pl.BlockSpec((1, Dm), lambda i: (0, 0))
    return pl.pallas_call(
        body, name=name, grid=(T // tm,),
        in_specs=[pl.BlockSpec((tm, p.shape[1]), lambda i: (i, 0)) for p in dPs]
        + [pl.BlockSpec(w.shape, lambda i: (0, 0)) for w in Ws] + [row, vec, row],
        out_specs=[row, vec],
        out_shape=[jax.ShapeDtypeStruct((T, Dm), F32), jax.ShapeDtypeStruct((1, Dm), F32)],
        compiler_params=_params(("arbitrary",)),
    )(*dPs, *Ws, h, gain, dh_in)


def mm_tn(a, b, *, name):
    T, M = a.shape
    N = b.shape[1]
    tm, tn, tk = _tile(M, 1408), _tile(N, 1408), _tile(T, 1664)

    def body(a_ref, b_ref, o_ref):
        @pl.when(pl.program_id(2) == 0)
        def _():
            o_ref[...] = jnp.zeros_like(o_ref)

        o_ref[...] += _tn(a_ref[...].astype(BF16), b_ref[...].astype(BF16))

    return pl.pallas_call(
        body, name=name, grid=(M // tm, N // tn, T // tk),
        in_specs=[pl.BlockSpec((tk, tm), lambda i, j, k: (k, i)), pl.BlockSpec((tk, tn), lambda i, j, k: (k, j))],
        out_specs=pl.BlockSpec((tm, tn), lambda i, j, k: (i, j)),
        out_shape=jax.ShapeDtypeStruct((M, N), F32),
        compiler_params=_params(("arbitrary", "arbitrary", "arbitrary")),
    )(a, b)


def loss_and_grad(h, target, *, name):
    T, Dm = h.shape

    def body(h_ref, t_ref, loss_ref, dh_ref):
        b = pl.program_id(0)

        @pl.when(b == 0)
        def _():
            loss_ref[...] = jnp.zeros_like(loss_ref)
            dh_ref[...] = jnp.zeros_like(dh_ref)

        @pl.when(b > 0)
        def _():
            e = h_ref[...] - t_ref[...]
            dh_ref[...] = e * (1.0 / Dm)
            loss_ref[...] += jnp.sum(e * e) * (0.5 / Dm)

    return pl.pallas_call(
        body, name=name, grid=(T // BLK,),
        in_specs=[pl.BlockSpec((BLK, Dm), lambda b: (b, 0)),
                  pl.BlockSpec((BLK, Dm), lambda b: (jnp.maximum(b - 1, 0), 0))],
        out_specs=[pl.BlockSpec((8, LANES), lambda b: (0, 0)), pl.BlockSpec((BLK, Dm), lambda b: (b, 0))],
        out_shape=[jax.ShapeDtypeStruct((8, LANES), F32), jax.ShapeDtypeStruct((T, Dm), F32)],
        compiler_params=_params(("arbitrary",)),
    )(h, target)


def adamw(w, g, m, v, *, name):
    R, C = w.shape
    tr = R
    for t in (512, 352, 256):
        if R > t and R % t == 0:
            tr = t
            break

    def body(w_ref, g_ref, m_ref, v_ref, d_ref, nm_ref, nv_ref):
        g_ = g_ref[...]
        m_ = ADAM_B1 * m_ref[...] + (1.0 - ADAM_B1) * g_
        v_ = ADAM_B2 * v_ref[...] + (1.0 - ADAM_B2) * (g_ * g_)
        m_hat = m_ / (1.0 - ADAM_B1 ** ADAM_STEP)
        v_hat = v_ / (1.0 - ADAM_B2 ** ADAM_STEP)
        d_ref[...] = -ADAM_LR * (m_hat / (jnp.sqrt(v_hat) + ADAM_EPS) + ADAM_WD * w_ref[...])
        nm_ref[...] = m_
        nv_ref[...] = v_

    spec = pl.BlockSpec((tr, C), lambda i: (i, 0))
    return pl.pallas_call(
        body, name=name, grid=(R // tr,), in_specs=[spec] * 4, out_specs=[spec] * 3,
        out_shape=[jax.ShapeDtypeStruct((R, C), F32)] * 3, compiler_params=_params(("arbitrary",)),
    )(w, g, m, v)


def _me():
    return lax.axis_index("x"), lax.axis_index("y"), lax.axis_index("c")


def _flip(pos, rel):
    return tuple(1 - p if r else p for p, r in zip(pos, rel))


def _slot(pos):
    return 4 * pos[0] + 2 * pos[1] + pos[2]


HBM_SPEC = pl.BlockSpec(memory_space=pltpu.HBM)
CHIP_RELS = ((1, 0), (0, 1), (1, 1))


def all_gather_big(xs, *, name):
    n = len(xs)

    def body(*refs):
        x_refs, out_refs = refs[:n], refs[n:2 * n]
        send_sems, recv_sems, local_sems = refs[2 * n:]
        me = _me()
        sibling = _flip(me, (0, 0, 1))
        chips = [_flip(me, rel + (0,)) for rel in CHIP_RELS]

        def copy(i, k, block, to, src=None):
            dst = out_refs[i].at[_slot(block)]
            return pltpu.make_async_remote_copy(
                src_ref=dst if src is None else src, dst_ref=dst, send_sem=send_sems.at[i, k],
                recv_sem=recv_sems.at[i, k], device_id=to, device_id_type=MESH)

        sent, local = [], []
        for i in range(n):
            mine = pltpu.make_async_copy(x_refs[i], out_refs[i].at[_slot(me)], local_sems.at[i])
            mine.start()
            local.append(mine)
            sent += [copy(i, 0, me, sibling, src=x_refs[i])]
            sent += [copy(i, 1 + j, me, chip, src=x_refs[i]) for j, chip in enumerate(chips)]
        for cp in sent:
            cp.start()
        for i in range(n):
            for j, chip in enumerate(chips):
                copy(i, 1 + j, chip, me).wait_recv()
                passed = copy(i, 4 + j, chip, sibling)
                passed.start()
                sent.append(passed)
        for i in range(n):
            copy(i, 0, sibling, me).wait_recv()
            for j, chip in enumerate(chips):
                copy(i, 4 + j, _flip(chip, (0, 0, 1)), me).wait_recv()
        for cp in sent:
            cp.wait_send()
        for mine in local:
            mine.wait()

    return pl.pallas_call(
        body, name=name, in_specs=[HBM_SPEC] * n, out_specs=[HBM_SPEC] * n,
        out_shape=[jax.ShapeDtypeStruct((NDEV,) + x.shape, x.dtype) for x in xs],
        scratch_shapes=[pltpu.SemaphoreType.DMA((n, 7)), pltpu.SemaphoreType.DMA((n, 7)), pltpu.SemaphoreType.DMA((n,))],
    )(*xs)


def all_to_all_small(src, *, name):
    _, r, C = src.shape

    def body(src_ref, out_ref, send_sems, recv_sems):
        me = _me()
        my = _slot(me)
        out_ref[my] = src_ref[my]
        copies = []
        for k in range(1, NDEV):
            peer = _flip(me, ((k >> 2) & 1, (k >> 1) & 1, k & 1))
            cp = pltpu.make_async_remote_copy(
                src_ref=src_ref.at[_slot(peer)], dst_ref=out_ref.at[my], send_sem=send_sems.at[k - 1],
                recv_sem=recv_sems.at[k - 1], device_id=peer, device_id_type=MESH)
            cp.start()
            copies.append((cp, peer))
        for k, (cp, peer) in enumerate(copies):
            pltpu.make_async_remote_copy(
                src_ref=src_ref.at[my], dst_ref=out_ref.at[_slot(peer)], send_sem=send_sems.at[k],
                recv_sem=recv_sems.at[k], device_id=peer, device_id_type=MESH).wait_recv()
        for cp, _ in copies:
            cp.wait_send()

    vm = pl.BlockSpec(memory_space=pltpu.VMEM)
    return pl.pallas_call(
        body, name=name, in_specs=[vm], out_specs=vm, out_shape=jax.ShapeDtypeStruct(src.shape, src.dtype),
        scratch_shapes=[pltpu.SemaphoreType.DMA((7,)), pltpu.SemaphoreType.DMA((7,))],
    )(src)


def sum_slots(a, *, name):
    n, r, C = a.shape

    def body(a_ref, o_ref):
        s = a_ref[0]
        for k in range(1, n):
            s = s + a_ref[k]
        o_ref[...] = s

    vm = pl.BlockSpec(memory_space=pltpu.VMEM)
    return pl.pallas_call(body, name=name, in_specs=[vm], out_specs=vm,
                          out_shape=jax.ShapeDtypeStruct((r, C), F32))(a)


def rs_exchange_sibling(gs, *, name):
    n = len(gs)

    def body(*refs):
        g_refs, out_refs, send_sems, recv_sems = refs[:n], refs[n:2 * n], refs[2 * n], refs[2 * n + 1]
        sibling = _flip(_me(), (0, 0, 1))
        copies = []
        for i in range(n):
            for chip in range(4):
                cp = pltpu.make_async_remote_copy(
                    src_ref=g_refs[i].at[2 * chip + sibling[2]], dst_ref=out_refs[i].at[chip],
                    send_sem=send_sems.at[i, chip], recv_sem=recv_sems.at[i, chip], device_id=sibling,
                    device_id_type=MESH)
                cp.start()
                copies.append(cp)
        for cp in copies:
            cp.wait()

    return pl.pallas_call(
        body, name=name, in_specs=[HBM_SPEC] * n, out_specs=[HBM_SPEC] * n,
        out_shape=[jax.ShapeDtypeStruct((4,) + g.shape[1:], g.dtype) for g in gs],
        scratch_shapes=[pltpu.SemaphoreType.DMA((n, 4)), pltpu.SemaphoreType.DMA((n, 4))],
    )(*gs)


def rs_chip_partials(g, got, *, name):
    _, R, C = g.shape
    tr = _tile(R, 768, unit=16)

    def body(c_ref, g_ref, got_ref, o_ref):
        o_ref[...] = (g_ref[...].astype(F32) + got_ref[...].astype(F32)).astype(o_ref.dtype)

    c = jnp.reshape(lax.axis_index("c"), (1,)).astype(jnp.int32)
    return pl.pallas_call(
        body, name=name,
        grid_spec=pltpu.PrefetchScalarGridSpec(
            num_scalar_prefetch=1, grid=(4, R // tr),
            in_specs=[pl.BlockSpec((None, tr, C), lambda k, i, c_ref: (2 * k + c_ref[0], i, 0)),
                      pl.BlockSpec((None, tr, C), lambda k, i, c_ref: (k, i, 0))],
            out_specs=pl.BlockSpec((None, tr, C), lambda k, i, c_ref: (k, i, 0))),
        out_shape=jax.ShapeDtypeStruct((4, R, C), g.dtype), compiler_params=_params(("arbitrary", "arbitrary")),
    )(c, g, got)


def rs_exchange_chips(ps, *, name):
    n = len(ps)

    def body(*refs):
        copies = _chip_exchange_copies(refs[:n], refs[n:2 * n], refs[2 * n], refs[2 * n + 1])
        for cp in copies:
            cp.start()
        for cp in copies:
            cp.wait()

    return pl.pallas_call(
        body, name=name, in_specs=[HBM_SPEC] * n, out_specs=[HBM_SPEC] * n,
        out_shape=_chip_exchange_shapes(ps), scratch_shapes=_chip_exchange_sems(n),
    )(*ps)


def _chip_exchange_copies(p_refs, out_refs, send_sems, recv_sems):
    me = _me()
    copies = []
    for i, (p_ref, out_ref) in enumerate(zip(p_refs, out_refs)):
        for j, rel in enumerate(CHIP_RELS):
            peer = _flip(me, rel + (0,))
            copies.append(pltpu.make_async_remote_copy(
                src_ref=p_ref.at[2 * peer[0] + peer[1]], dst_ref=out_ref.at[j], send_sem=send_sems.at[i, j],
                recv_sem=recv_sems.at[i, j], device_id=peer, device_id_type=MESH))
    return copies


def _chip_exchange_shapes(ps):
    return [jax.ShapeDtypeStruct((3,) + p.shape[1:], p.dtype) for p in ps]


def _chip_exchange_sems(n):
    return [pltpu.SemaphoreType.DMA((n, 3)), pltpu.SemaphoreType.DMA((n, 3))]


def _gather_direct(start, x_refs, out_refs, send_sems, recv_sems, local_sems):
    me = _me()
    peers = [_flip(me, (0, 0, 1))] + [_flip(me, rel + (0,)) for rel in CHIP_RELS]
    for i, (x_ref, out_ref) in enumerate(zip(x_refs, out_refs)):
        local = pltpu.make_async_copy(x_ref, out_ref.at[_slot(me)], local_sems.at[i])
        local.start() if start else local.wait()
        for k, peer in enumerate(peers):
            def copy(block):
                return pltpu.make_async_remote_copy(
                    src_ref=x_ref, dst_ref=out_ref.at[_slot(block)], send_sem=send_sems.at[i, k],
                    recv_sem=recv_sems.at[i, k], device_id=peer, device_id_type=MESH)
            if start:
                copy(me).start()
            else:
                copy(me).wait_send()
                copy(peer).wait_recv()


def _gather_direct_sems(n):
    return [pltpu.SemaphoreType.DMA((n, 4)), pltpu.SemaphoreType.DMA((n, 4)), pltpu.SemaphoreType.DMA((n,))]


def gather_forward_to_sibling(gs, *, name):
    n = len(gs)

    def body(*refs):
        in_refs, out_refs, send_sems, recv_sems = refs[:n], refs[n:2 * n], refs[2 * n], refs[2 * n + 1]
        me = _me()
        sibling = _flip(me, (0, 0, 1))
        chips = [_flip(me, rel + (0,)) for rel in CHIP_RELS]
        sends, recvs = [], []
        for i in range(n):
            for j, chip in enumerate(chips):
                def copy(block):
                    return pltpu.make_async_remote_copy(
                        src_ref=in_refs[i].at[_slot(chip)], dst_ref=out_refs[i].at[_slot(block)],
                        send_sem=send_sems.at[i, j], recv_sem=recv_sems.at[i, j], device_id=sibling, device_id_type=MESH)
                sends.append(copy(chip))
                recvs.append(copy(_flip(chip, (0, 0, 1))))
        for cp in sends:
            cp.start()
        for cp in recvs:
            cp.wait_recv()
        for cp in sends:
            cp.wait_send()

    return pl.pallas_call(
        body, name=name, in_specs=[HBM_SPEC] * n, out_specs=[HBM_SPEC] * n,
        out_shape=[jax.ShapeDtypeStruct(g.shape, g.dtype) for g in gs],
        input_output_aliases={i: i for i in range(n)},
        scratch_shapes=[pltpu.SemaphoreType.DMA((n, 3)), pltpu.SemaphoreType.DMA((n, 3))],
    )(*gs)


def rs_final_sum(p, got, *, name):
    _, R, C = p.shape
    tr = _tile(R, 768, unit=16)

    def body(chip_ref, p_ref, got_ref, o_ref):
        s = p_ref[...].astype(F32)
        for j in range(3):
            s = s + got_ref[j].astype(F32)
        o_ref[...] = s

    mychip = jnp.reshape(2 * lax.axis_index("x") + lax.axis_index("y"), (1,)).astype(jnp.int32)
    return pl.pallas_call(
        body, name=name,
        grid_spec=pltpu.PrefetchScalarGridSpec(
            num_scalar_prefetch=1, grid=(R // tr,),
            in_specs=[pl.BlockSpec((None, tr, C), lambda i, chip_ref: (chip_ref[0], i, 0)),
                      pl.BlockSpec((3, tr, C), lambda i, chip_ref: (0, i, 0))],
            out_specs=pl.BlockSpec((tr, C), lambda i, chip_ref: (i, 0))),
        out_shape=jax.ShapeDtypeStruct((R, C), F32), compiler_params=_params(("arbitrary",)),
    )(mychip, p, got)


def reduce_scatter_big(gs):
    got = rs_exchange_sibling(gs, name="rs_sibling")
    parts = [rs_chip_partials(g, t, name=f"rs_chip_partials_{i}") for i, (g, t) in enumerate(zip(gs, got))]
    got2 = rs_exchange_chips(parts, name="rs_chips")
    return [rs_final_sum(p, t, name=f"rs_final_sum_{i}") for i, (p, t) in enumerate(zip(parts, got2))]


def _const_spec(shape):
    return pl.BlockSpec(shape, lambda *_: (0,) * len(shape))


def retention_tables(T):
    pos = jnp.arange(T, dtype=F32) - float(PAD)
    inv_freq = 1.0 / (10000.0 ** jnp.linspace(0.0, 1.0, 64, dtype=F32))
    ang = pos[:, None] * inv_freq[None, :]
    cos = jnp.repeat(jnp.cos(ang), 2, axis=1)
    sin = jnp.repeat(jnp.sin(ang), 2, axis=1) * jnp.tile(jnp.array([-1.0, 1.0], F32), 64)[None, :]
    lane = np.arange(LANES)
    perm = jnp.broadcast_to(jnp.asarray((lane[:, None] == (lane[None, :] ^ 1)).astype(np.float32)), (4, LANES, LANES))
    log_gamma = jnp.log1p(-jnp.exp2(-5.0 - jnp.arange(4, dtype=F32)))
    idx = jnp.arange(BLK, dtype=F32)
    diff = idx[:, None] - idx[None, :]
    intra = jnp.where(diff >= 0, jnp.exp(jnp.maximum(diff, 0.0) * log_gamma[:, None, None]), 0.0)
    zeta = jnp.exp((BLK - 1.0 - idx)[None, :] * log_gamma[:, None])
    xi = jnp.exp((idx + 1.0)[None, :] * log_gamma[:, None])
    bc = lambda t: jnp.broadcast_to(t[:, :, None], (4, BLK, LANES))
    return cos, sin, perm, intra, bc(zeta), bc(xi)


def _heads(x):
    return jnp.stack([x[:, h * LANES:(h + 1) * LANES] for h in range(4)])


def _unheads(y):
    return jnp.concatenate([y[h] for h in range(4)], axis=1)


def _ret_chunk(rq, rk, rv, rg, S, cos, sin, intra, zeta, xi, perm):
    q = rq * cos + hdot(rq, perm) * sin
    k = (rk * cos + hdot(rk, perm) * sin) * (128.0 ** -0.5)
    ret = bdot(bdot(q, _t(k)) * intra, rv) + bdot(q * xi, S)
    S_new = S * xi[..., BLK - 1:BLK, :] + bdot(_t(k * zeta), rv)
    c = ret - jnp.mean(ret, axis=-1, keepdims=True)
    out = c * lax.rsqrt(jnp.mean(c * c, axis=-1, keepdims=True) + EPS) * _silu(rg)
    return out, S_new


def _wide(off):
    return pl.BlockSpec((BLK, 4 * LANES), lambda n: (n, off))


def retention_fwd(p, tables, *, name):
    T = p.shape[0]
    N = T // BLK
    cos, sin, perm, intra, zeta, xi = tables

    def body(rq, rk, rv, rg, cos_ref, sin_ref, in_ref, ze_ref, xi_ref, perm_ref, out_ref, sall_ref, s_scr):
        @pl.when(pl.program_id(0) == 0)
        def _():
            s_scr[...] = jnp.zeros_like(s_scr)

        S = s_scr[...]
        sall_ref[...] = S
        out, S_new = _ret_chunk(_heads(rq[...]), _heads(rk[...]), _heads(rv[...]), _heads(rg[...]), S, cos_ref[...],
                                sin_ref[...], in_ref[...], ze_ref[...], xi_ref[...], perm_ref[...])
        out_ref[...] = _unheads(out).astype(BF16)
        s_scr[...] = S_new

    rowtab = pl.BlockSpec((BLK, LANES), lambda n: (n, 0))
    tab = _const_spec((4, BLK, LANES))
    return pl.pallas_call(
        body, name=name, grid=(N,),
        in_specs=[_wide(0), _wide(1), _wide(2), _wide(3), rowtab, rowtab, tab, tab, tab, tab],
        out_specs=[_wide(0), pl.BlockSpec((None, 4, LANES, LANES), lambda n: (n, 0, 0, 0))],
        out_shape=[jax.ShapeDtypeStruct((T, 512), BF16), jax.ShapeDtypeStruct((N, 4, LANES, LANES), F32)],
        scratch_shapes=[pltpu.VMEM((4, LANES, LANES), F32)],
        compiler_params=_params(("arbitrary",)),
    )(p, p, p, p, cos, sin, intra, zeta, xi, perm)


def _row_mask(n):
    return (n * BLK + _iota2((BLK, 1), 0) >= PAD).astype(F32)


def retention_bwd(p, sall, dmixed, tables, *, name):
    T = p.shape[0]
    N = T // BLK
    cos, sin, perm, intra, zeta, xi = tables

    def body(rq, rk, rv, rg, cos_ref, sin_ref, in_ref, ze_ref, xi_ref, perm_ref, sall_ref, do_ref, drq, drk, drv, drg,
             ds_scr):
        n = N - 1 - pl.program_id(0)

        @pl.when(pl.program_id(0) == 0)
        def _():
            ds_scr[...] = jnp.zeros_like(ds_scr)

        f = lambda a, b, c, d, s: _ret_chunk(a, b, c, d, s, cos_ref[...], sin_ref[...], in_ref[...], ze_ref[...],
                                             xi_ref[...], perm_ref[...])
        _, vjp = jax.vjp(f, _heads(rq[...]), _heads(rk[...]), _heads(rv[...]), _heads(rg[...]), sall_ref[...])
        g = vjp((_heads(do_ref[...]), ds_scr[...]))
        mask = _row_mask(n)
        for ref, val in zip((drq, drk, drv, drg), g[:4]):
            ref[...] = _unheads(val) * mask
        ds_scr[...] = g[4]

    def rwide(off):
        return pl.BlockSpec((BLK, 4 * LANES), lambda n: (N - 1 - n, off))

    rowtab = pl.BlockSpec((BLK, LANES), lambda n: (N - 1 - n, 0))
    tab = _const_spec((4, BLK, LANES))
    return pl.pallas_call(
        body, name=name, grid=(N,),
        in_specs=[rwide(0), rwide(1), rwide(2), rwide(3), rowtab, rowtab, tab, tab, tab, tab,
                  pl.BlockSpec((None, 4, LANES, LANES), lambda n: (N - 1 - n, 0, 0, 0)), rwide(0)],
        out_specs=[rwide(0)] * 4, out_shape=[jax.ShapeDtypeStruct((T, 512), F32)] * 4,
        scratch_shapes=[pltpu.VMEM((4, LANES, LANES), F32)],
        compiler_params=_params(("arbitrary",)),
    )(p, p, p, p, cos, sin, intra, zeta, xi, perm, sall, dmixed)


def conv_silu_fwd(p, w, send, *, name):
    T = p.shape[0]
    N = T // BLK
    ns = len(send)

    def body(x_ref, xp_ref, w_ref, *refs):
        x_refs, o_ref, g_refs, sems = refs[:ns], refs[ns], refs[ns + 1:2 * ns + 1], refs[2 * ns + 1:]
        n, part = pl.program_id(0), pl.program_id(1)

        @pl.when((n == 0) & (part == 0))
        def _():
            _gather_direct(True, x_refs, g_refs, *sems)

        @pl.when((n == N - 1) & (part == 2))
        def _():
            _gather_direct(False, x_refs, g_refs, *sems)

        cur = x_ref[...]
        cat = jnp.concatenate([jnp.where(n > 0, xp_ref[...], 0.0), cur], axis=0)
        y = w_ref[3:4, :] * cur
        for s in (1, 2, 3):
            y = y + w_ref[3 - s:4 - s, :] * pltpu.roll(cat, s, 0)[BLK:]
        o_ref[...] = _silu(y)

    cw = 4 * LANES
    res = pl.pallas_call(
        body, name=name, grid=(N, 3),
        in_specs=[pl.BlockSpec((BLK, cw), lambda n, c: (n, 4 + c)),
                  pl.BlockSpec((BLK, cw), lambda n, c: (jnp.maximum(n - 1, 0), 4 + c)),
                  pl.BlockSpec((4, cw), lambda n, c: (0, c))] + [HBM_SPEC] * ns,
        out_specs=[pl.BlockSpec((BLK, cw), lambda n, c: (n, c))] + [HBM_SPEC] * ns,
        out_shape=[jax.ShapeDtypeStruct((T, 1536), F32)] + [jax.ShapeDtypeStruct((NDEV,) + x.shape, x.dtype) for x in send],
        scratch_shapes=_gather_direct_sems(ns), compiler_params=_params(("arbitrary", "arbitrary")),
    )(p, p, w, *send)
    return res[0], res[1:]


def conv_silu_bwd(p, w, dact, part, *, name):
    T = p.shape[0]
    N = T // BLK
    cw = 4 * LANES

    def body(xp_ref, x_ref, xn_ref, w_ref, da_ref, dan_ref, dx_ref, dw_ref):
        n = pl.program_id(0)
        last = n == N - 1
        cat = jnp.concatenate([jnp.where(n > 0, xp_ref[...], 0.0), x_ref[...], jnp.where(last, 0.0, xn_ref[...])], axis=0)
        shifted = [cat] + [pltpu.roll(cat, s, 0) for s in (1, 2, 3)]
        y = w_ref[3:4, :] * shifted[0]
        for s in (1, 2, 3):
            y = y + w_ref[3 - s:4 - s, :] * shifted[s]
        y = y[BLK:]
        da = jnp.concatenate([da_ref[...], jnp.where(last, 0.0, dan_ref[...])], axis=0)
        sg = _sigmoid(y)
        dy = da * sg * (1.0 + y * (1.0 - sg))
        dx = w_ref[3:4, :] * dy[:BLK]
        for s in (1, 2, 3):
            dx = dx + w_ref[3 - s:4 - s, :] * pltpu.roll(dy, 2 * BLK - s, 0)[:BLK]
        dx_ref[...] = dx * _row_mask(n)

        @pl.when(n == 0)
        def _():
            dw_ref[...] = jnp.zeros_like(dw_ref)

        for s in (0, 1, 2, 3):
            dw_ref[3 - s:4 - s, :] += jnp.sum(dy[:BLK] * shifted[s][BLK:2 * BLK], axis=0, keepdims=True)

    def xs(d):
        return pl.BlockSpec((BLK, cw), lambda n: (jnp.clip(n + d, 0, N - 1), 4 + part))

    return pl.pallas_call(
        body, name=name, grid=(N,),
        in_specs=[xs(-1), xs(0), xs(1), pl.BlockSpec((4, cw), lambda n: (0, part)),
                  pl.BlockSpec((BLK, cw), lambda n: (n, 0)),
                  pl.BlockSpec((BLK, cw), lambda n: (jnp.minimum(n + 1, N - 1), 0))],
        out_specs=[pl.BlockSpec((BLK, cw), lambda n: (n, 0)), pl.BlockSpec((4, cw), lambda n: (0, 0))],
        out_shape=[jax.ShapeDtypeStruct((T, 512), F32), jax.ShapeDtypeStruct((4, 512), F32)],
        compiler_params=_params(("arbitrary",)),
    )(p, p, p, w, dact, dact)


def _softplus(x):
    return jnp.maximum(x, 0.0) + jnp.log1p(jnp.exp(-jnp.abs(x)))


def _pick4(tile, off):
    return jnp.stack([_lane_pick(tile, off + h) for h in range(4)])


def _spread4(v4, off, rows):
    lane = _iota2((rows, LANES), 1)
    out = jnp.where(lane == off, v4[0], 0.0)
    for h in range(1, 4):
        out = out + jnp.where(lane == off + h, v4[h], 0.0)
    return out


def _gdn_chunk(qa, ka, va, z, braw, araw, S, alog, dtb, onorm, rowmask, lincl):
    r, c = _iota2((BLK, BLK), 0), _iota2((BLK, BLK), 1)
    incl, strict = r >= c, r > c
    eye = (r == c).astype(F32)
    q = qa * lax.rsqrt(jnp.sum(qa * qa, axis=-1, keepdims=True) + EPS) * (128.0 ** -0.5)
    k = ka * lax.rsqrt(jnp.sum(ka * ka, axis=-1, keepdims=True) + EPS)
    beta = _sigmoid(braw) * rowmask
    g = -jnp.exp(alog) * _softplus(araw + dtb) * rowmask
    gc = hdot(lincl, jnp.broadcast_to(g, qa.shape))
    decay = jnp.where(incl, jnp.exp(jnp.where(incl, gc - _t(gc), 0.0)), 0.0)
    kb = k * beta
    amat = jnp.where(strict, bdot(kb, _t(k)) * decay, 0.0)
    m = -amat
    inv = eye + m
    pw = hdot(m, m)
    for t in range(6):
        inv = inv + hdot(inv, pw)
        if t < 5:
            pw = hdot(pw, pw)
    egc = jnp.exp(gc)
    u = hdot(inv, va * beta)
    w = hdot(inv, kb * egc)
    qk = jnp.where(incl, bdot(q, _t(k)) * decay, 0.0)
    glast = gc[..., BLK - 1:BLK, :]
    vnew = u - bdot(w, S)
    o = bdot(q * egc, S) + bdot(qk, vnew)
    S_new = S * jnp.exp(glast) + bdot(_t(k * jnp.exp(glast - gc)), vnew)
    out = o * lax.rsqrt(jnp.mean(o * o, axis=-1, keepdims=True) + EPS) * onorm * _silu(z)
    return out, S_new


def _lincl():
    i = np.arange(BLK)
    return jnp.broadcast_to(jnp.asarray((i[:, None] >= i[None, :]).astype(np.float32)), (4, BLK, BLK))


def gdn_fwd(act, p, alog, dtb, onorm, send, *, name):
    T = p.shape[0]
    N = T // BLK
    ns = len(send)

    def body(qa, ka, va, z, ba, alog_ref, dtb_ref, on_ref, l_ref, *refs):
        x_refs, (out_ref, sall_ref), g_refs = refs[:ns], refs[ns:ns + 2], refs[ns + 2:2 * ns + 2]
        s_scr, sems = refs[2 * ns + 2], refs[2 * ns + 3:]
        n = pl.program_id(0)

        @pl.when(n == 0)
        def _():
            s_scr[...] = jnp.zeros_like(s_scr)
            _gather_direct(True, x_refs, g_refs, *sems)

        @pl.when(n == N - 1)
        def _():
            _gather_direct(False, x_refs, g_refs, *sems)

        S = s_scr[...]
        sall_ref[...] = S
        out, S_new = _gdn_chunk(_heads(qa[...]), _heads(ka[...]), _heads(va[...]), _heads(z[...]), _pick4(ba[...], 0),
                                _pick4(ba[...], 4), S, _pick4(alog_ref[...], 0), _pick4(dtb_ref[...], 0), on_ref[...],
                                _row_mask(n), l_ref[...])
        out_ref[...] = _unheads(out).astype(BF16)
        s_scr[...] = S_new

    vec = _const_spec((1, LANES))
    res = pl.pallas_call(
        body, name=name, grid=(N,),
        in_specs=[_wide(0), _wide(1), _wide(2), _wide(7), pl.BlockSpec((BLK, LANES), lambda n: (n, 32)), vec, vec, vec,
                  _const_spec((4, BLK, BLK))] + [HBM_SPEC] * ns,
        out_specs=[_wide(0), pl.BlockSpec((None, 4, LANES, LANES), lambda n: (n, 0, 0, 0))] + [HBM_SPEC] * ns,
        out_shape=[jax.ShapeDtypeStruct((T, 512), BF16), jax.ShapeDtypeStruct((N, 4, LANES, LANES), F32)]
        + [jax.ShapeDtypeStruct((NDEV,) + x.shape, x.dtype) for x in send],
        scratch_shapes=[pltpu.VMEM((4, LANES, LANES), F32)] + _gather_direct_sems(ns),
        compiler_params=_params(("arbitrary",)),
    )(act, act, act, p, p, alog, dtb, onorm, _lincl(), *send)
    return res[0], res[1], res[2:]


def gdn_bwd(act, p, alog, dtb, onorm, sall, dmixed, partials, *, name):
    T = p.shape[0]
    N = T // BLK
    ns = len(partials)

    def body(qa, ka, va, z, ba, alog_ref, dtb_ref, on_ref, l_ref, sall_ref, do_ref, *refs):
        p_refs, (dq_ref, dk_ref, dv_ref, dz_ref, dba_ref, dal_ref, ddt_ref, don_ref) = refs[:ns], refs[ns:ns + 8]
        got_refs, ds_scr, sems = refs[ns + 8:2 * ns + 8], refs[2 * ns + 8], refs[2 * ns + 9:]
        step = pl.program_id(0)
        n = N - 1 - step

        @pl.when(step == 0)
        def _():
            ds_scr[...] = jnp.zeros_like(ds_scr)
            dal_ref[...] = jnp.zeros_like(dal_ref)
            ddt_ref[...] = jnp.zeros_like(ddt_ref)
            don_ref[...] = jnp.zeros_like(don_ref)
            for cp in _chip_exchange_copies(p_refs, got_refs, *sems):
                cp.start()

        @pl.when(step == N - 1)
        def _():
            for cp in _chip_exchange_copies(p_refs, got_refs, *sems):
                cp.wait()

        rowmask, lincl = _row_mask(n), l_ref[...]
        f = lambda *a: _gdn_chunk(*a, rowmask, lincl)
        _, vjp = jax.vjp(f, _heads(qa[...]), _heads(ka[...]), _heads(va[...]), _heads(z[...]), _pick4(ba[...], 0),
                         _pick4(ba[...], 4), sall_ref[...], _pick4(alog_ref[...], 0), _pick4(dtb_ref[...], 0),
                         on_ref[...])
        g = vjp((_heads(do_ref[...]), ds_scr[...]))
        dq_ref[...] = _unheads(g[0]) * rowmask
        dk_ref[...] = _unheads(g[1]) * rowmask
        dv_ref[...] = _unheads(g[2]) * rowmask
        dz_ref[...] = _unheads(g[3]) * rowmask
        dba_ref[...] = (_spread4(g[4], 0, BLK) + _spread4(g[5], 4, BLK)) * rowmask
        ds_scr[...] = g[6]
        dal_ref[...] += _spread4(g[7], 0, 1)
        ddt_ref[...] += _spread4(g[8], 0, 1)
        don_ref[...] += g[9]

    def rwide(off):
        return pl.BlockSpec((BLK, 4 * LANES), lambda s: (N - 1 - s, off))

    vec = _const_spec((1, LANES))
    col = pl.BlockSpec((BLK, LANES), lambda s: (N - 1 - s, 0))
    res = pl.pallas_call(
        body, name=name, grid=(N,),
        in_specs=[rwide(0), rwide(1), rwide(2), rwide(7), pl.BlockSpec((BLK, LANES), lambda s: (N - 1 - s, 32)), vec, vec,
                  vec, _const_spec((4, BLK, BLK)),
                  pl.BlockSpec((None, 4, LANES, LANES), lambda s: (N - 1 - s, 0, 0, 0)), rwide(1)] + [HBM_SPEC] * ns,
        out_specs=[rwide(0)] * 4 + [col, vec, vec, vec] + [HBM_SPEC] * ns,
        out_shape=[jax.ShapeDtypeStruct((T, 512), F32)] * 4 + [jax.ShapeDtypeStruct((T, LANES), F32)]
        + [jax.ShapeDtypeStruct((1, LANES), F32)] * 3 + _chip_exchange_shapes(partials),
        scratch_shapes=[pltpu.VMEM((4, LANES, LANES), F32)] + _chip_exchange_sems(ns),
        compiler_params=_params(("arbitrary",)),
    )(act, act, act, p, p, alog, dtb, onorm, _lincl(), sall, dmixed, *partials)
    return res[:8], res[8:]


NEG = -1e30


def _swa_block(q, k0, kp, kc, v0, vp, vc, sink, n):
    r, c = _iota2((BLK, BLK), 0), _iota2((BLK, BLK), 1)
    m0 = (c >= PAD) & (c <= n * BLK + r)
    mp = (n >= 2) & (c > r)
    mc = (n >= 1) & (r >= c)
    b = lambda t: jnp.broadcast_to(t, (4,) + t.shape)
    qs = q * (64.0 ** -0.5)
    s0 = jnp.where(m0, bdot(qs, _t(b(k0))), NEG)
    sp = jnp.where(mp, bdot(qs, _t(b(kp))), NEG)
    sc = jnp.where(mc, bdot(qs, _t(b(kc))), NEG)
    mx = jnp.maximum(jnp.max(jnp.maximum(jnp.maximum(s0, sp), sc), axis=-1, keepdims=True), sink)
    mx = lax.stop_gradient(mx)
    p0, pp, pc = jnp.exp(s0 - mx), jnp.exp(sp - mx), jnp.exp(sc - mx)
    den = (jnp.sum(p0, axis=-1, keepdims=True) + jnp.sum(pp, axis=-1, keepdims=True)
           + jnp.sum(pc, axis=-1, keepdims=True) + jnp.exp(sink - mx))
    return (bdot(p0, b(v0)) + bdot(pp, b(vp)) + bdot(pc, b(vc))) / den


def _swa_specs():
    rows = (lambda n: 0, lambda n: jnp.maximum(n - 1, 0), lambda n: n)

    def kv_spec(off, row):
        return pl.BlockSpec((BLK, LANES), lambda g, n: (row(n), off + g))

    q = pl.BlockSpec((BLK, 4 * LANES), lambda g, n: (n, g))
    return q, [kv_spec(off, row) for off in (8, 10) for row in rows]


def swa_fwd(p2, sinkrow, *, name):
    T = p2.shape[0]
    N = T // BLK

    def body(q, k0, kp, kc, v0, vp, vc, sink_ref, o_ref):
        g, n = pl.program_id(0), pl.program_id(1)
        f32 = lambda ref: ref[...].astype(F32)
        o = _swa_block(_heads(f32(q)), f32(k0), f32(kp), f32(kc), f32(v0), f32(vp), f32(vc),
                       _pick4(sink_ref[...], 4 * g), n)
        o_ref[...] = _unheads(o).astype(BF16)

    q, kv = _swa_specs()
    return pl.pallas_call(
        body, name=name, grid=(2, N), in_specs=[q] + kv + [_const_spec((1, LANES))],
        out_specs=q, out_shape=jax.ShapeDtypeStruct((T, 1024), BF16),
        compiler_params=_params(("arbitrary", "arbitrary")),
    )(p2, p2, p2, p2, p2, p2, p2, sinkrow)


def swa_bwd(p2, sinkrow, dmixed, *, name):
    T = p2.shape[0]
    N = T // BLK

    def body(q, k0, kp, kc, v0, vp, vc, sink_ref, do_ref, dq_ref, dk_ref, dv_ref, dsink_ref):
        g, n = pl.program_id(0), pl.program_id(1)

        @pl.when(n == 0)
        def _():
            dk_ref[...] = jnp.zeros_like(dk_ref)
            dv_ref[...] = jnp.zeros_like(dv_ref)

        @pl.when((g == 0) & (n == 0))
        def _():
            dsink_ref[...] = jnp.zeros_like(dsink_ref)

        f = lambda *a: _swa_block(*a, n)
        f32 = lambda ref: ref[...].astype(F32)
        _, vjp = jax.vjp(f, _heads(f32(q)), f32(k0), f32(kp), f32(kc), f32(v0), f32(vp), f32(vc),
                         _pick4(sink_ref[...], 4 * g))
        dq, dk0, dkp, dkc, dv0, dvp, dvc, dsink = vjp(_heads(do_ref[...]))
        dq_ref[...] = _unheads(dq)
        prev = pl.ds(pl.multiple_of(jnp.maximum(n - 1, 0) * BLK, BLK), BLK)
        cur = pl.ds(pl.multiple_of(n * BLK, BLK), BLK)
        for ref, d0, dp, dc in ((dk_ref, dk0, dkp, dkc), (dv_ref, dv0, dvp, dvc)):
            ref[0:BLK, :] += d0
            ref[prev, :] += dp
            ref[cur, :] += dc
        dsink_ref[...] += _spread4(dsink, 4 * g, 1)

    qspec, kv = _swa_specs()
    slab = pl.BlockSpec((T, LANES), lambda g, n: (0, g))
    return pl.pallas_call(
        body, name=name, grid=(2, N), in_specs=[qspec] + kv + [_const_spec((1, LANES)), qspec],
        out_specs=[qspec, slab, slab, _const_spec((1, LANES))],
        out_shape=[jax.ShapeDtypeStruct((T, 1024), F32), jax.ShapeDtypeStruct((T, 256), F32),
                   jax.ShapeDtypeStruct((T, 256), F32), jax.ShapeDtypeStruct((1, LANES), F32)],
        compiler_params=_params(("arbitrary", "arbitrary")),
    )(p2, p2, p2, p2, p2, p2, p2, sinkrow, dmixed)


def _split_dot(x, m):
    rows = x.shape[0]
    hi = x.astype(BF16)
    lo = (x - hi.astype(F32)).astype(BF16)
    r = _nn(jnp.concatenate([hi, lo], axis=0), m)
    return r[:rows] + r[rows:]


def _tri_and_ones(strict, ones=True):
    i = np.arange(BLK)
    m = (i[:, None] > i[None, :]) if strict else (i[:, None] >= i[None, :])
    if ones:
        m = np.concatenate([m, np.ones((BLK, BLK), bool)], axis=1)
    return jnp.asarray(m.astype(np.float32), dtype=BF16)


def _later_and_row_sums(x, m):
    r = _split_dot(x, m)
    if m.shape[1] == 2 * BLK:
        return r[:, :BLK], r[:, BLK:]
    return r, jnp.broadcast_to(jnp.sum(x, axis=1, keepdims=True), x.shape)


SB_PAIR = 2
SB_FWD_GROUP = 4


def _sb_positions():
    r, s = _iota2((BLK, BLK), 0), _iota2((BLK, BLK), 1)
    return s - r, s


def _sb_weights(qbs, ks, base, n, pos, carries, after):
    nh, kb = len(qbs), len(ks[0])
    zs = [[_nt(qbs[h], ks[h][c]) for c in range(kb)] for h in range(nh)]
    valid = [(pos[0] < (n - base - c) * BLK) & (pos[1] >= PAD - (base + c) * BLK) for c in range(kb)]
    lb = [[None] * kb for _ in range(nh)]
    sums = [[None] * kb for _ in range(nh)]
    for c in range(kb):
        for h in range(nh):
            z = jnp.where(valid[c], zs[h][c], NEG)
            lb[h][c] = jnp.minimum(z, 0.0) - jnp.log(1.0 + jnp.exp(-jnp.abs(z)))
            sums[h][c] = _later_and_row_sums(lb[h][c] - z, after)
    a = [[None] * kb for _ in range(nh)]
    carries = list(carries)
    for c in reversed(range(kb)):
        for h in range(nh):
            a[h][c] = jnp.exp(lb[h][c] + carries[h] + sums[h][c][0])
            carries[h] = carries[h] + sums[h][c][1]
    return valid, lb, a, carries


def _key_blocks(n_blocks):
    return next(k for k in (5, 3, 1) if n_blocks % k == 0)


def sb_fwd(p2, *, name):
    T = p2.shape[0]
    N = T // BLK
    kb = _key_blocks(N)
    nh = SB_FWD_GROUP
    heads = [slice(h * LANES, (h + 1) * LANES) for h in range(nh)]

    def body(q_ref, k_ref, v_ref, after_ref, o_ref, of_ref):
        n = pl.program_id(1)
        qbs = [(q_ref[:, hs].astype(F32) * (64.0 ** -0.5)).astype(BF16) for hs in heads]
        after, pos = after_ref[...], _sb_positions()
        nsup = n // kb + 1

        def step(t, c):
            accs, carries = c
            base = (nsup - 1 - t) * kb
            rows = [pl.ds(pl.multiple_of((base + sub) * BLK, BLK), BLK) for sub in range(kb)]
            ks = [[k_ref[r, hs] for r in rows] for hs in heads]
            _, _, a, carries = _sb_weights(qbs, ks, base, n, pos, carries, after)
            accs = list(accs)
            for sub, r in enumerate(rows):
                for h, hs in enumerate(heads):
                    accs[h] = accs[h] + _nn(a[h][sub].astype(BF16), v_ref[r, hs])
            return accs, carries

        zero = [jnp.zeros((BLK, LANES), F32)] * nh
        accs, _ = lax.fori_loop(0, nsup, step, (zero, zero))
        acc = jnp.concatenate(accs, axis=1)
        o_ref[...] = acc.astype(BF16)
        of_ref[...] = acc

    wide = nh * LANES

    def slab(off):
        return pl.BlockSpec((T, wide), lambda g, n: (0, off + g))

    def blk(off):
        return pl.BlockSpec((BLK, wide), lambda g, n: (n, off + g))

    return pl.pallas_call(
        body, name=name, grid=(8 // nh, N),
        in_specs=[blk(12 // nh), slab(20 // nh), slab(28 // nh), _const_spec((BLK, BLK))],
        out_specs=[blk(0), blk(0)],
        out_shape=[jax.ShapeDtypeStruct((T, 1024), BF16), jax.ShapeDtypeStruct((T, 1024), F32)],
        compiler_params=_params(("arbitrary", "arbitrary")),
    )(p2, p2, p2, _tri_and_ones(True, ones=False))


def sb_bwd(p2, o, dmixed, *, name):
    T = p2.shape[0]
    N = T // BLK
    kb = _key_blocks(N)

    heads = [slice(h * LANES, (h + 1) * LANES) for h in range(SB_PAIR)]
    scale = 64.0 ** -0.5

    def body(q_ref, k_ref, v_ref, after_ref, from_ref, o_ref, do_ref, dq_ref, dk_ref, dv_ref, dkt_scr, dvt_scr):
        n = pl.program_id(1)

        @pl.when(n == 0)
        def _():
            dkt_scr[...] = jnp.zeros_like(dkt_scr)
            dvt_scr[...] = jnp.zeros_like(dvt_scr)

        qbs, qts, dobs, dots, totals = [], [], [], [], []
        for hs in heads:
            qs = q_ref[:, hs].astype(F32) * scale
            do = do_ref[:, hs]
            qbs.append(qs.astype(BF16))
            qts.append(qs.T.astype(BF16))
            dobs.append(do.astype(BF16))
            dots.append(do.T.astype(BF16))
            total = jnp.sum(dobs[-1].astype(F32) * o_ref[:, hs], axis=1, keepdims=True)
            totals.append(jnp.broadcast_to(total, (BLK, LANES)))
        after, frm, pos = after_ref[...], from_ref[...], _sb_positions()
        nsup = n // kb + 1

        def step(t, c):
            dqs, carries, gcarries = c
            base = (nsup - 1 - t) * kb
            rows = [pl.ds(pl.multiple_of((base + sub) * BLK, BLK), BLK) for sub in range(kb)]
            ks = [[k_ref[r, hs] for r in rows] for hs in heads]
            valid, lb, a, carries = _sb_weights(qbs, ks, base, n, pos, carries, after)
            das = [[_nt(dobs[h], v_ref[r, hs]) for r in rows] for h, hs in enumerate(heads)]
            ab = [[a[h][sub].astype(BF16) for sub in range(kb)] for h in range(SB_PAIR)]
            g = [[None] * kb for _ in heads]
            sums = [[None] * kb for _ in heads]
            for sub in range(kb):
                for h in range(SB_PAIR):
                    g[h][sub] = das[h][sub] * ab[h][sub].astype(F32)
                    sums[h][sub] = _later_and_row_sums(g[h][sub], frm)
            dqs, gcarries = list(dqs), list(gcarries)
            for sub in reversed(range(kb)):
                for h in range(SB_PAIR):
                    before = totals[h] - (gcarries[h] + sums[h][sub][0])
                    gcarries[h] = gcarries[h] + sums[h][sub][1]
                    beta = jnp.exp(lb[h][sub])
                    dz = (g[h][sub] - beta * (g[h][sub] + before)).astype(BF16)
                    dqs[h] = dqs[h] + _nn(dz, ks[h][sub])
                    dkt_scr[h * N + base + sub] += _nn(qts[h], dz)
                    dvt_scr[h * N + base + sub] += _nn(dots[h], ab[h][sub])
            return dqs, carries, gcarries

        zero = [jnp.zeros((BLK, LANES), F32)] * SB_PAIR
        dqs, _, _ = lax.fori_loop(0, nsup, step, (zero, zero, zero))
        dq_ref[...] = (jnp.concatenate(dqs, axis=1) * scale).astype(dq_ref.dtype)

        @pl.when(n == N - 1)
        def _():
            def flush(j, _):
                rows = pl.ds(pl.multiple_of(j * BLK, BLK), BLK)
                for h, hs in enumerate(heads):
                    dk_ref[rows, hs] = dkt_scr[h * N + j].T.astype(dk_ref.dtype)
                    dv_ref[rows, hs] = dvt_scr[h * N + j].T.astype(dv_ref.dtype)
                return 0

            lax.fori_loop(0, N, flush, 0)

    wide = SB_PAIR * LANES

    def slab(off):
        return pl.BlockSpec((T, wide), lambda g, n: (0, off + g))

    def blk(off):
        return pl.BlockSpec((BLK, wide), lambda g, n: (n, off + g))

    tri = _const_spec((BLK, BLK))
    return pl.pallas_call(
        body, name=name, grid=(8 // SB_PAIR, N),
        in_specs=[blk(12 // SB_PAIR), slab(20 // SB_PAIR), slab(28 // SB_PAIR), tri, tri, blk(0), blk(8 // SB_PAIR)],
        out_specs=[blk(0), slab(0), slab(0)],
        out_shape=[jax.ShapeDtypeStruct((T, 1024), BF16)] * 3,
        scratch_shapes=[pltpu.VMEM((SB_PAIR * N, LANES, LANES), F32), pltpu.VMEM((SB_PAIR * N, LANES, LANES), F32)],
        compiler_params=_params(("arbitrary", "arbitrary")),
    )(p2, p2, p2, _tri_and_ones(True, ones=False), _tri_and_ones(False, ones=False), o, dmixed)


def ffn_fwd(h, g_pre, g_post, wg, wu, wd, tag):
    u, gate, up, act = norm_mm(h, g_pre, (wg, wu), swiglu=True, wt=True, name=f"ffn_up_{tag}")
    y, h_new = mm_norm_res([act], [wd], h, g_post, 0.5, name=f"ffn_down_{tag}")
    return h_new, (h, u, gate, up, y)


def ffn_bwd(saved, dh, g_pre, g_post, wg, wu, wd, tag):
    h, u, gate, up, y = saved
    dy, dg_post, dgate, dup, act = normbwd_mm_nt(dh, y, g_post, wd, 0.5, (gate, up), name=f"ffn_bwd_down_{tag}")
    dwd = mm_tn(act, dy, name=f"ffn_dwd_{tag}")
    dwg = mm_tn(dgate, u, name=f"ffn_dwg_{tag}")
    dwu = mm_tn(dup, u, name=f"ffn_dwu_{tag}")
    dh_in, dg_pre = mm_nt_normbwd([dgate, dup], [wg, wu], h, g_pre, dh, wt=True, name=f"ffn_bwd_up_{tag}")
    return dh_in, (dg_pre, dg_post), (dwg, dwu, dwd)


def _lane_row(v):
    v = v.reshape(1, -1)
    return jnp.pad(v, ((0, 0), (0, LANES - v.shape[1])))


AB_WIDTHS = (512,) * 8 + (LANES,)


def mixer_ab_fwd(h, g_pre, g_post, w_in, conv_w, a_log, dt_bias, out_norm, w_out, tables, send_conv, send_gdn):
    u, p = norm_mm(h, g_pre, (w_in,), swiglu=False, name="ab_in")
    ret, sall_r = retention_fwd(p, tables, name="retention_fwd")
    act, gathered_conv = conv_silu_fwd(p, conv_w, send_conv, name="conv_fwd")
    gdn, sall_g, gathered_gdn = gdn_fwd(act, p, _lane_row(a_log), _lane_row(dt_bias), out_norm.reshape(1, LANES),
                                        send_gdn, name="gdn_fwd")
    y, h_new = mm_norm_res([ret, gdn], [w_out[:512], w_out[512:]], h, g_post, 1.0, name="ab_out")
    return h_new, (h, u, p, ret, sall_r, act, gdn, sall_g, y), list(gathered_conv) + list(gathered_gdn)


def mixer_ab_bwd(saved, dh, g_pre, g_post, w_in, conv_w, a_log, dt_bias, out_norm, w_out, tables, partials):
    h, u, p, ret, sall_r, act, gdn, sall_g, y = saved
    dy, dg_post, dmixed = normbwd_mm_nt(dh, y, g_post, w_out, 1.0, name="ab_bwd_out")
    dw_out = jnp.concatenate([mm_tn(ret, dy, name="ab_dwout_ret"), mm_tn(gdn, dy, name="ab_dwout_gdn")], axis=0)
    pieces = list(retention_bwd(p, sall_r, dmixed, tables, name="retention_bwd"))
    (dqa, dka, dva, dz, dba, dalog, ddtb, donorm), arrived = gdn_bwd(
        act, p, _lane_row(a_log), _lane_row(dt_bias), out_norm.reshape(1, LANES), sall_g, dmixed, partials,
        name="gdn_bwd")
    dconv = []
    for part, dact in enumerate((dqa, dka, dva)):
        dx, dw = conv_silu_bwd(p, conv_w, dact, part, name=f"conv_bwd_{part}")
        pieces.append(dx)
        dconv.append(dw)
    pieces += [dz, dba]
    offs = np.cumsum((0,) + AB_WIDTHS)
    w_parts = [w_in[:, a:b] for a, b in zip(offs[:-1], offs[1:])]
    dh_in, dg_pre = mm_nt_normbwd(pieces, w_parts, h, g_pre, dh, name="ab_bwd_in")
    dw_in = jnp.concatenate([mm_tn(u, pc, name=f"ab_dwin_{i}") for i, pc in enumerate(pieces)], axis=1)
    small = (jnp.concatenate(dconv, axis=1), dalog[:, :4], ddtb[:, :4], donorm)
    return dh_in, (dg_pre, dg_post), (dw_in, dw_out), small, arrived


CD_WIDTHS = (1024, 256, 256, 1024, 1024, 1024)


def mixer_cd_fwd(h, g_pre, g_post, w_in, sinks, w_out):
    u, p2 = norm_mm(h, g_pre, (w_in,), swiglu=False, out_dtype=BF16, name="cd_in")
    swa = swa_fwd(p2, _lane_row(sinks), name="swa_fwd")
    sb, sb_f32 = sb_fwd(p2, name="sb_fwd")
    y, h_new = mm_norm_res([swa, sb], [w_out[:1024], w_out[1024:]], h, g_post, 1.0, name="cd_out")
    return h_new, (h, u, p2, swa, sb, sb_f32, y)


def mixer_cd_bwd(saved, dh, g_pre, g_post, w_in, sinks, w_out):
    h, u, p2, swa, sb, sb_f32, y = saved
    dy, dg_post, dmixed = normbwd_mm_nt(dh, y, g_post, w_out, 1.0, name="cd_bwd_out")
    dw_out = jnp.concatenate([mm_tn(swa, dy, name="cd_dwout_swa"), mm_tn(sb, dy, name="cd_dwout_sb")], axis=0)
    dq_c, dk_c, dv_c, dsink = swa_bwd(p2, _lane_row(sinks), dmixed, name="swa_bwd")
    pieces = [dq_c, dk_c, dv_c] + list(sb_bwd(p2, sb_f32, dmixed, name="sb_bwd"))
    offs = np.cumsum((0,) + CD_WIDTHS)
    w_parts = [w_in[:, a:b] for a, b in zip(offs[:-1], offs[1:])]
    dh_in, dg_pre = mm_nt_normbwd(pieces, w_parts, h, g_pre, dh, name="cd_bwd_in")
    dw_in = jnp.concatenate([mm_tn(u, pc, name=f"cd_dwin_{i}") for i, pc in enumerate(pieces)], axis=1)
    return dh_in, (dg_pre, dg_post), (dw_in, dw_out), dsink[:, :8]


def _pad_heads(w, axis):
    shape = w.shape
    w = w.reshape(shape[:axis] + (shape[axis] // 64, 64) + shape[axis + 1:])
    pad = [(0, 0)] * w.ndim
    pad[axis + 1] = (0, 64)
    return jnp.pad(w, pad).reshape(shape[:axis] + (2 * shape[axis],) + shape[axis + 1:])


def _unpad_heads(w, axis):
    shape = w.shape
    w = w.reshape(shape[:axis] + (shape[axis] // 128, 128) + shape[axis + 1:])
    w = lax.slice_in_dim(w, 0, 64, axis=axis + 1)
    return w.reshape(shape[:axis] + (shape[axis] // 2,) + shape[axis + 1:])


SMALL_SHARDED = (("meta_tokens", (NMETA, LANES), 1), ("norm_gains", (2, 6, LANES), 2), ("ab_conv_w", (1, 4, 192), 2))
SMALL_REPL = (("ab_a_log", (1, 4)), ("ab_dt_bias", (1, 4)), ("ab_out_norm", (1, LANES)), ("cd_sinks", (1, 8)))


def _stack_shards(g, axis):
    full = jnp.moveaxis(g, 0, axis)
    shape = full.shape
    return full.reshape(shape[:axis] + (shape[axis] * shape[axis + 1],) + shape[axis + 2:])


def _split_shards(full, axis):
    shape = full.shape
    g = full.reshape(shape[:axis] + (NDEV, shape[axis] // NDEV) + shape[axis + 1:])
    return jnp.moveaxis(g, axis, 0)


def _pad_rows8(a):
    rows = []
    for x in a:
        flat = x.reshape(x.shape[0], -1)
        n = -(-flat.shape[1] // LANES) * LANES
        rows.append(jnp.pad(flat, ((0, 0), (0, n - flat.shape[1]))).reshape(x.shape[0], n // LANES, LANES))
    cat = jnp.concatenate(rows, axis=1)
    return jnp.pad(cat, ((0, 0), (0, -cat.shape[1] % 8), (0, 0)))


def _unpad_rows8(packed, shapes):
    out, at = [], 0
    for shape in shapes:
        size = int(np.prod(shape))
        nrow = -(-size // LANES)
        blk = packed[:, at:at + nrow].reshape(packed.shape[0], -1)[:, :size]
        out.append(blk.reshape((packed.shape[0],) + tuple(shape)))
        at += nrow
    return out


def kernel(x, meta_tokens, norm_gains, ffn_w_gate, ffn_w_up, ffn_w_down, ab_w_in, ab_conv_w, ab_a_log, ab_dt_bias, ab_out_norm, ab_w_out, cd_w_in, cd_sinks, cd_w_out, loss_target, m_meta_tokens, m_norm_gains, m_ffn_w_gate, m_ffn_w_up, m_ffn_w_down, m_ab_w_in, m_ab_conv_w, m_ab_a_log, m_ab_dt_bias, m_ab_out_norm, m_ab_w_out, m_cd_w_in, m_cd_sinks, m_cd_w_out, v_meta_tokens, v_norm_gains, v_ffn_w_gate, v_ffn_w_up, v_ffn_w_down, v_ab_w_in, v_ab_conv_w, v_ab_a_log, v_ab_dt_bias, v_ab_out_norm, v_ab_w_out, v_cd_w_in, v_cd_sinks, v_cd_w_out):
    w = dict(meta_tokens=meta_tokens, norm_gains=norm_gains, ffn_w_gate=ffn_w_gate, ffn_w_up=ffn_w_up,
             ffn_w_down=ffn_w_down, ab_w_in=ab_w_in, ab_conv_w=ab_conv_w, ab_a_log=ab_a_log, ab_dt_bias=ab_dt_bias,
             ab_out_norm=ab_out_norm, ab_w_out=ab_w_out, cd_w_in=cd_w_in, cd_sinks=cd_sinks, cd_w_out=cd_w_out)
    m = dict(meta_tokens=m_meta_tokens, norm_gains=m_norm_gains, ffn_w_gate=m_ffn_w_gate, ffn_w_up=m_ffn_w_up,
             ffn_w_down=m_ffn_w_down, ab_w_in=m_ab_w_in, ab_conv_w=m_ab_conv_w, ab_a_log=m_ab_a_log,
             ab_dt_bias=m_ab_dt_bias, ab_out_norm=m_ab_out_norm, ab_w_out=m_ab_w_out, cd_w_in=m_cd_w_in,
             cd_sinks=m_cd_sinks, cd_w_out=m_cd_w_out)
    v = dict(meta_tokens=v_meta_tokens, norm_gains=v_norm_gains, ffn_w_gate=v_ffn_w_gate, ffn_w_up=v_ffn_w_up,
             ffn_w_down=v_ffn_w_down, ab_w_in=v_ab_w_in, ab_conv_w=v_ab_conv_w, ab_a_log=v_ab_a_log,
             ab_dt_bias=v_ab_dt_bias, ab_out_norm=v_ab_out_norm, ab_w_out=v_ab_w_out, cd_w_in=v_cd_w_in,
             cd_sinks=v_cd_sinks, cd_w_out=v_cd_w_out)
    order = list(w)
    S = x.shape[1]
    T = S + BLK

    fs = DFF // NDEV

    def ffn_local(i, j):
        return jnp.concatenate([ffn_w_gate[i, j].T, ffn_w_up[i, j].T, ffn_w_down[i, j]], axis=0).astype(BF16)

    wg, wu, wd = {}, {}, {}

    def ffn_gathered(gathered, ij):
        for kind, full_w in enumerate((wg, wu, wd)):
            full_w[ij] = gathered[:, kind * fs:(kind + 1) * fs].reshape(DFF, D)

    ffn00_all, abin_all, about_all = all_gather_big(
        [ffn_local(0, 0), ab_w_in[0].astype(BF16), ab_w_out[0].astype(BF16)], name="gather_first")
    ffn_gathered(ffn00_all, (0, 0))
    ab_in = jnp.pad(_stack_shards(abin_all, 1), ((0, 0), (0, AB_INP - AB_IN)))
    ab_out = about_all.reshape(D, D)
    under_conv = [ffn_local(0, 1)]
    under_gdn = [ffn_local(1, 0), ffn_local(1, 1), cd_w_in[0].astype(BF16), cd_w_out[0].astype(BF16)]
    small_src = jnp.broadcast_to(_pad_rows8([w[n][None] for n, _, _ in SMALL_SHARDED]), (NDEV, 40, LANES))
    small_all = _unpad_rows8(all_to_all_small(small_src, name="gather_small"), [s for _, s, _ in SMALL_SHARDED])
    full = {n: _stack_shards(g, ax) for (n, _, ax), g in zip(SMALL_SHARDED, small_all)}
    conv_w = full["ab_conv_w"][0]
    gains = full["norm_gains"].reshape(2, 6, 1, D)
    tables = retention_tables(T)

    h = jnp.concatenate([jnp.zeros((PAD, D), F32), full["meta_tokens"], x[0]], axis=0)
    h, s00 = ffn_fwd(h, gains[0, 0], gains[0, 1], wg[0, 0], wu[0, 0], wd[0, 0], "00")
    h, sab, later_part = mixer_ab_fwd(h, gains[0, 2], gains[0, 3], ab_in, conv_w, ab_a_log, ab_dt_bias, ab_out_norm,
                                      ab_out, tables, under_conv, under_gdn)
    ffn01_all, ffn10_all, ffn11_all, cdin_all, cdout_all = gather_forward_to_sibling(later_part, name="gather_finish")
    for ij, gathered in (((0, 1), ffn01_all), ((1, 0), ffn10_all), ((1, 1), ffn11_all)):
        ffn_gathered(gathered, ij)
    cd_in = _pad_heads(_stack_shards(cdin_all, 1), 1)
    cd_out = _pad_heads(cdout_all.reshape(D, D), 0)
    h, s01 = ffn_fwd(h, gains[0, 4], gains[0, 5], wg[0, 1], wu[0, 1], wd[0, 1], "01")
    h, s10 = ffn_fwd(h, gains[1, 0], gains[1, 1], wg[1, 0], wu[1, 0], wd[1, 0], "10")
    h, scd = mixer_cd_fwd(h, gains[1, 2], gains[1, 3], cd_in, cd_sinks, cd_out)
    h, s11 = ffn_fwd(h, gains[1, 4], gains[1, 5], wg[1, 1], wu[1, 1], wd[1, 1], "11")
    loss_tile, dh = loss_and_grad(h, loss_target[0], name="loss")
    loss = lax.psum(loss_tile[0, 0], ("x", "y", "c"))

    dgain = [[None] * 6, [None] * 6]
    dffn = {}
    dh, (dgain[1][4], dgain[1][5]), dffn[1, 1] = ffn_bwd(s11, dh, gains[1, 4], gains[1, 5], wg[1, 1], wu[1, 1], wd[1, 1], "11")
    dh, (dgain[1][2], dgain[1][3]), (dcd_in, dcd_out), dsinks = mixer_cd_bwd(scd, dh, gains[1, 2], gains[1, 3], cd_in, cd_sinks, cd_out)
    dh, (dgain[1][0], dgain[1][1]), dffn[1, 0] = ffn_bwd(s10, dh, gains[1, 0], gains[1, 1], wg[1, 0], wu[1, 0], wd[1, 0], "10")
    dh, (dgain[0][4], dgain[0][5]), dffn[0, 1] = ffn_bwd(s01, dh, gains[0, 4], gains[0, 5], wg[0, 1], wu[0, 1], wd[0, 1], "01")
    ffn_send = lambda ij: jnp.concatenate([t.astype(BF16).reshape(NDEV, fs, D) for t in dffn[ij]], axis=1)
    early = [(0, 1), (1, 0), (1, 1)]
    early_send = [ffn_send(ij) for ij in early] + [_split_shards(_unpad_heads(dcd_in, 1).astype(BF16), 1),
                                                   _unpad_heads(dcd_out, 0).astype(BF16).reshape(NDEV, D // NDEV, D)]
    early_sib = rs_exchange_sibling(early_send, name="rs_sibling_early")
    early_parts = [rs_chip_partials(g, t, name=f"rs_chip_partials_early_{i}")
                   for i, (g, t) in enumerate(zip(early_send, early_sib))]
    dh, (dgain[0][2], dgain[0][3]), (dab_in, dab_out), (dconv, dalog, ddtb, donorm), early_got = mixer_ab_bwd(
        sab, dh, gains[0, 2], gains[0, 3], ab_in, conv_w, ab_a_log, ab_dt_bias, ab_out_norm, ab_out, tables, early_parts)
    dh, (dgain[0][0], dgain[0][1]), dffn[0, 0] = ffn_bwd(s00, dh, gains[0, 0], gains[0, 1], wg[0, 0], wu[0, 0], wd[0, 0], "00")
    grad_x = dh[BLK:][None]

    gfull = dict(meta_tokens=dh[PAD:BLK], norm_gains=jnp.stack([jnp.concatenate(r, axis=0) for r in dgain]),
                 ab_conv_w=dconv[None])
    early_g = [rs_final_sum(p, t, name=f"rs_final_sum_early_{i}") for i, (p, t) in enumerate(zip(early_parts, early_got))]
    ffn00_g, abin_g, about_g = reduce_scatter_big(
        [ffn_send((0, 0)), _split_shards(dab_in[:, :AB_IN].astype(BF16), 1),
         dab_out.astype(BF16).reshape(NDEV, D // NDEV, D)])
    cdin_g, cdout_g = early_g[3:]
    ffn_by = dict(zip(early, early_g[:3]))
    ffn_by[0, 0] = ffn00_g
    ffn_g = jnp.stack([jnp.stack([ffn_by[i, j].reshape(3, fs, D) for j in range(2)]) for i in range(2)])
    ffn_g = jnp.moveaxis(ffn_g, 2, 0)
    grads = dict(ffn_w_gate=jnp.swapaxes(ffn_g[0], 2, 3), ffn_w_up=jnp.swapaxes(ffn_g[1], 2, 3), ffn_w_down=ffn_g[2],
                 ab_w_in=abin_g[None], ab_w_out=about_g[None], cd_w_in=cdin_g[None], cd_w_out=cdout_g[None])
    repl = [jnp.broadcast_to(t[None], (NDEV,) + t.shape) for t in (dalog, ddtb, donorm, dsinks)]
    ssend = _pad_rows8([_split_shards(gfull[n], ax) for n, _, ax in SMALL_SHARDED] + repl)
    ssum = sum_slots(all_to_all_small(ssend, name="exchange_small_grads"), name="sum_small_grads")[None]
    small = _unpad_rows8(ssum, [s for _, s, _ in SMALL_SHARDED] + [s for _, s in SMALL_REPL])
    grads.update({n: g[0] for n, g in zip([n for n, _, _ in SMALL_SHARDED] + [n for n, _ in SMALL_REPL], small)})

    delta, new_m, new_v = {}, {}, {}
    for n in order:
        shape = w[n].shape
        view = (-1, shape[-1])
        d_, m_, v_ = adamw(w[n].reshape(view), grads[n].reshape(view), m[n].reshape(view), v[n].reshape(view),
                           name=f"adamw_{n}")
        delta[n], new_m[n], new_v[n] = d_.reshape(shape), m_.reshape(shape), v_.reshape(shape)
    return (loss, grad_x, *[grads[n] for n in order], *[delta[n] for n in order], *[new_m[n] for n in order],
            *[new_v[n] for n in order])
```

```python
import numpy as np
import jax
import jax.numpy as jnp
from jax import lax
from jax.experimental import pallas as pl
from jax.experimental.pallas import tpu as pltpu

F32, BF16 = jnp.float32, jnp.bfloat16
EPS = 1e-6
D = 1024
NMETA = 16
BLK = 128
PAD = BLK - NMETA
DFF = 2816
LANES = 128
NDEV = 8
AB_IN, AB_INP = 4104, 4224
ADAM_LR, ADAM_B1, ADAM_B2, ADAM_EPS, ADAM_WD, ADAM_STEP = 0.001, 0.9, 0.999, 1e-08, 0.01, 10
VMEM_LIMIT = 56 * 1024 * 1024
MESH = pl.DeviceIdType.MESH
HIGH = lax.Precision.HIGH


def _params(sem):
    return pltpu.CompilerParams(dimension_semantics=sem, vmem_limit_bytes=VMEM_LIMIT)


def _row_tile(T, streamed, resident):
    for tm in (640, 320, 128):
        if T % tm == 0 and 2 * (tm * streamed + resident) <= VMEM_LIMIT - 14 * 1024 * 1024:
            return tm
    return _tile(T, 128)


MXU_COLS = 256


def _col_chunks(n):
    return [slice(c, min(c + MXU_COLS, n)) for c in range(0, n, MXU_COLS)]


def _tile(n, cap, unit=LANES):
    if n <= cap:
        return n
    best = None
    for t in range(unit, cap + 1, unit):
        if n % t == 0:
            best = t
    assert best is not None, (n, cap)
    return best


def _rms_fwd(x, g):
    return x * lax.rsqrt(jnp.mean(x * x, axis=-1, keepdims=True) + EPS) * g


def _rms_bwd(x, g, dz):
    r = lax.rsqrt(jnp.mean(x * x, axis=-1, keepdims=True) + EPS)
    xh = x * r
    dg = jnp.sum(dz * xh, axis=0, keepdims=True)
    t = dz * g
    return r * (t - xh * jnp.mean(t * xh, axis=-1, keepdims=True)), dg


def _sigmoid(x):
    return 0.5 * jnp.tanh(0.5 * x) + 0.5


def _silu(x):
    return x * _sigmoid(x)


def _nn(a, b, precision=None):
    return lax.dot_general(a, b, (((1,), (0,)), ((), ())), preferred_element_type=F32, precision=precision)


def _nt(a, b):
    return lax.dot_general(a, b, (((1,), (1,)), ((), ())), preferred_element_type=F32)


def _tn(a, b):
    return lax.dot_general(a, b, (((0,), (0,)), ((), ())), preferred_element_type=F32)


def _mm(a, b, precision=None):
    if a.ndim == 3:
        return lax.dot_general(a, b, (((2,), (1,)), ((0,), (0,))), preferred_element_type=F32, precision=precision)
    return _nn(a, b, precision)


def _t(x):
    return jnp.swapaxes(x, -1, -2)


@jax.custom_vjp
def bdot(a, b):
    return _mm(a.astype(BF16), b.astype(BF16))


def _bdot_fwd(a, b):
    return bdot(a, b), (a, b)


def _bdot_bwd(res, g):
    a, b = res
    return bdot(g, _t(b)), bdot(_t(a), g)


bdot.defvjp(_bdot_fwd, _bdot_bwd)


@jax.custom_vjp
def hdot(a, b):
    return _mm(a, b, HIGH)


def _hdot_fwd(a, b):
    return hdot(a, b), (a, b)


def _hdot_bwd(res, g):
    a, b = res
    return hdot(g, _t(b)), hdot(_t(a), g)


hdot.defvjp(_hdot_fwd, _hdot_bwd)


def _iota2(shape, axis):
    return lax.broadcasted_iota(jnp.int32, shape, axis)


def _lane_pick(row, lane):
    return jnp.sum(jnp.where(_iota2(row.shape, 1) == lane, row, 0.0), axis=1, keepdims=True)


def norm_mm(h, gain, ws, *, swiglu, name, wt=False, out_dtype=F32):
    T, Dm = h.shape
    N = ws[0].shape[0 if wt else 1]
    tm, tn = _tile(T, 640), _tile(N, 1408)
    nw = len(ws)
    mm = _nt if wt else _nn

    def body(h_ref, g_ref, *refs):
        w_refs, u_ref, o_refs = refs[:nw], refs[nw], refs[nw + 1:]

        @pl.when(pl.program_id(1) == 0)
        def _():
            u_ref[...] = _rms_fwd(h_ref[...], g_ref[...]).astype(BF16)

        u = u_ref[...]
        for cols in _col_chunks(tn):
            acc = [mm(u, w[cols, :] if wt else w[:, cols]) for w in w_refs]
            if swiglu:
                o_refs[0][:, cols] = acc[0].astype(BF16)
                o_refs[1][:, cols] = acc[1].astype(BF16)
                o_refs[2][:, cols] = (_silu(acc[0]) * acc[1]).astype(BF16)
            else:
                o_refs[0][:, cols] = acc[0].astype(out_dtype)

    row = pl.BlockSpec((tm, Dm), lambda i, j: (i, 0))
    tile = pl.BlockSpec((tm, tn), lambda i, j: (i, j))
    if swiglu:
        out_shape = [jax.ShapeDtypeStruct((T, Dm), BF16)] + [jax.ShapeDtypeStruct((T, N), BF16)] * 3
        out_specs = [row, tile, tile, tile]
    else:
        out_shape = [jax.ShapeDtypeStruct((T, Dm), BF16), jax.ShapeDtypeStruct((T, N), out_dtype)]
        out_specs = [row, tile]
    return pl.pallas_call(
        body, name=name, grid=(T // tm, N // tn),
        in_specs=[row, pl.BlockSpec((1, Dm), lambda i, j: (0, 0))]
        + [pl.BlockSpec((tn, Dm), lambda i, j: (j, 0)) if wt else pl.BlockSpec((Dm, tn), lambda i, j: (0, j))] * nw,
        out_specs=out_specs, out_shape=out_shape,
        compiler_params=_params(("arbitrary", "arbitrary")),
    )(h, gain, *ws)


def mm_norm_res(As, Ws, h, gain, scale, *, name):
    T, Dm = h.shape
    n = len(As)
    tm = _row_tile(T, sum(a.shape[1] * a.dtype.itemsize for a in As) + 3 * Dm * 4,
                   sum(w.size * w.dtype.itemsize for w in Ws))

    def body(*refs):
        a_refs, w_refs = refs[:n], refs[n:2 * n]
        h_ref, g_ref, y_ref, hn_ref = refs[2 * n:]
        y = _nn(a_refs[0][...].astype(BF16), w_refs[0][...])
        for a, w in zip(a_refs[1:], w_refs[1:]):
            y = y + _nn(a[...].astype(BF16), w[...])
        y_ref[...] = y
        hn_ref[...] = h_ref[...] + scale * _rms_fwd(y, g_ref[...])

    row = pl.BlockSpec((tm, Dm), lambda i: (i, 0))
    return pl.pallas_call(
        body, name=name, grid=(T // tm,),
        in_specs=[pl.BlockSpec((tm, a.shape[1]), lambda i: (i, 0)) for a in As]
        + [pl.BlockSpec(w.shape, lambda i: (0, 0)) for w in Ws]
        + [row, pl.BlockSpec((1, Dm), lambda i: (0, 0))],
        out_specs=[row, row], out_shape=[jax.ShapeDtypeStruct((T, Dm), F32)] * 2,
        compiler_params=_params(("arbitrary",)),
    )(*As, *Ws, h, gain)


def normbwd_mm_nt(dh, y, gain, w, scale, gu=None, *, name):
    T, Dm = dh.shape
    N = w.shape[0]
    tm, tn = _tile(T, 640), _tile(N, 1408)
    swiglu = gu is not None

    def body(dh_ref, y_ref, g_ref, w_ref, *refs):
        if swiglu:
            gate_ref, up_ref, dy_ref, dg_ref, dgate_ref, dup_ref, a_ref = refs
        else:
            dy_ref, dg_ref, da_ref = refs
        i, j = pl.program_id(0), pl.program_id(1)

        @pl.when(j == 0)
        def _():
            dy, dg = _rms_bwd(y_ref[...], g_ref[...], scale * dh_ref[...])
            dy_ref[...] = dy.astype(BF16)

            @pl.when(i == 0)
            def _():
                dg_ref[...] = jnp.zeros_like(dg_ref)

            dg_ref[...] += dg

        dy = dy_ref[...]
        for cols in _col_chunks(tn):
            da = _nt(dy, w_ref[cols, :])
            if swiglu:
                for r0 in range(0, tm, BLK):
                    rows = slice(r0, min(r0 + BLK, tm))
                    d = da[rows]
                    gate, up = gate_ref[rows, cols].astype(F32), up_ref[rows, cols].astype(F32)
                    s = _sigmoid(gate)
                    gs = gate * s
                    dgate_ref[rows, cols] = (d * up * (s + gs * (1.0 - s))).astype(BF16)
                    dup_ref[rows, cols] = (d * gs).astype(BF16)
                    a_ref[rows, cols] = (gs * up).astype(BF16)
            else:
                da_ref[:, cols] = da

    row = pl.BlockSpec((tm, Dm), lambda i, j: (i, 0))
    vec = pl.BlockSpec((1, Dm), lambda i, j: (0, 0))
    tile = pl.BlockSpec((tm, tn), lambda i, j: (i, j))
    in_specs = [row, row, vec, pl.BlockSpec((tn, Dm), lambda i, j: (j, 0))]
    out_shape = [jax.ShapeDtypeStruct((T, Dm), BF16), jax.ShapeDtypeStruct((1, Dm), F32)]
    if swiglu:
        in_specs += [tile, tile]
        out_shape += [jax.ShapeDtypeStruct((T, N), BF16)] * 3
        out_specs = [row, vec, tile, tile, tile]
        args = (dh, y, gain, w, *gu)
    else:
        out_shape += [jax.ShapeDtypeStruct((T, N), F32)]
        out_specs = [row, vec, tile]
        args = (dh, y, gain, w)
    return pl.pallas_call(
        body, name=name, grid=(T // tm, N // tn), in_specs=in_specs, out_specs=out_specs,
        out_shape=out_shape, compiler_params=_params(("arbitrary", "arbitrary")),
    )(*args)


def mm_nt_normbwd(dPs, Ws, h, gain, dh_in, *, name, wt=False):
    T, Dm = h.shape
    n = len(dPs)
    tm = _row_tile(T, sum(p.shape[1] * p.dtype.itemsize for p in dPs) + 3 * Dm * 4,
                   sum(w.size * w.dtype.itemsize for w in Ws))
    mm = _nn if wt else _nt

    def body(*refs):
        p_refs, w_refs = refs[:n], refs[n:2 * n]
        h_ref, g_ref, dhin_ref, dh_ref, dg_ref = refs[2 * n:]
        du = mm(p_refs[0][...].astype(BF16), w_refs[0][...])
        for p, w in zip(p_refs[1:], w_refs[1:]):
            du = du + mm(p[...].astype(BF16), w[...])
        dx, dg = _rms_bwd(h_ref[...], g_ref[...], du)
        dh_ref[...] = dhin_ref[...] + dx

        @pl.when(pl.program_id(0) == 0)
        def _():
            dg_ref[...] = jnp.zeros_like(dg_ref)

        dg_ref[...] += dg

    row = pl.BlockSpec((tm, Dm), lambda i: (i, 0))
    vec = pl.BlockSpec((1, Dm), lambda i: (0, 0))
    return pl.pallas_call(
        body, name=name, grid=(T // tm,),
        in_specs=[pl.BlockSpec((tm, p.shape[1]), lambda i: (i, 0)) for p in dPs]
        + [pl.BlockSpec(w.shape, lambda i: (0, 0)) for w in Ws] + [row, vec, row],
        out_specs=[row, vec],
        out_shape=[jax.ShapeDtypeStruct((T, Dm), F32), jax.ShapeDtypeStruct((1, Dm), F32)],
        compiler_params=_params(("arbitrary",)),
    )(*dPs, *Ws, h, gain, dh_in)


def mm_tn(a, b, *, name):
    T, M = a.shape
    N = b.shape[1]
    tm, tn, tk = _tile(M, 1408), _tile(N, 1408), _tile(T, 1664)

    def body(a_ref, b_ref, o_ref):
        @pl.when(pl.program_id(2) == 0)
        def _():
            o_ref[...] = jnp.zeros_like(o_ref)

        o_ref[...] += _tn(a_ref[...].astype(BF16), b_ref[...].astype(BF16))

    return pl.pallas_call(
        body, name=name, grid=(M // tm, N // tn, T // tk),
        in_specs=[pl.BlockSpec((tk, tm), lambda i, j, k: (k, i)), pl.BlockSpec((tk, tn), lambda i, j, k: (k, j))],
        out_specs=pl.BlockSpec((tm, tn), lambda i, j, k: (i, j)),
        out_shape=jax.ShapeDtypeStruct((M, N), F32),
        compiler_params=_params(("arbitrary", "arbitrary", "arbitrary")),
    )(a, b)


def loss_and_grad(h, target, *, name):
    T, Dm = h.shape

    def body(h_ref, t_ref, loss_ref, dh_ref):
        b = pl.program_id(0)

        @pl.when(b == 0)
        def _():
            loss_ref[...] = jnp.zeros_like(loss_ref)
            dh_ref[...] = jnp.zeros_like(dh_ref)

        @pl.when(b > 0)
        def _():
            e = h_ref[...] - t_ref[...]
            dh_ref[...] = e * (1.0 / Dm)
            loss_ref[...] += jnp.sum(e * e) * (0.5 / Dm)

    return pl.pallas_call(
        body, name=name, grid=(T // BLK,),
        in_specs=[pl.BlockSpec((BLK, Dm), lambda b: (b, 0)),
                  pl.BlockSpec((BLK, Dm), lambda b: (jnp.maximum(b - 1, 0), 0))],
        out_specs=[pl.BlockSpec((8, LANES), lambda b: (0, 0)), pl.BlockSpec((BLK, Dm), lambda b: (b, 0))],
        out_shape=[jax.ShapeDtypeStruct((8, LANES), F32), jax.ShapeDtypeStruct((T, Dm), F32)],
        compiler_params=_params(("arbitrary",)),
    )(h, target)


def adamw(w, g, m, v, *, name):
    R, C = w.shape
    tr = R
    for t in (512, 352, 256):
        if R > t and R % t == 0:
            tr = t
            break

    def body(w_ref, g_ref, m_ref, v_ref, d_ref, nm_ref, nv_ref):
        g_ = g_ref[...]
        m_ = ADAM_B1 * m_ref[...] + (1.0 - ADAM_B1) * g_
        v_ = ADAM_B2 * v_ref[...] + (1.0 - ADAM_B2) * (g_ * g_)
        m_hat = m_ / (1.0 - ADAM_B1 ** ADAM_STEP)
        v_hat = v_ / (1.0 - ADAM_B2 ** ADAM_STEP)
        d_ref[...] = -ADAM_LR * (m_hat / (jnp.sqrt(v_hat) + ADAM_EPS) + ADAM_WD * w_ref[...])
        nm_ref[...] = m_
        nv_ref[...] = v_

    spec = pl.BlockSpec((tr, C), lambda i: (i, 0))
    return pl.pallas_call(
        body, name=name, grid=(R // tr,), in_specs=[spec] * 4, out_specs=[spec] * 3,
        out_shape=[jax.ShapeDtypeStruct((R, C), F32)] * 3, compiler_params=_params(("arbitrary",)),
    )(w, g, m, v)


def _me():
    return lax.axis_index("x"), lax.axis_index("y"), lax.axis_index("c")


def _flip(pos, rel):
    return tuple(1 - p if r else p for p, r in zip(pos, rel))


def _slot(pos):
    return 4 * pos[0] + 2 * pos[1] + pos[2]


HBM_SPEC = pl.BlockSpec(memory_space=pltpu.HBM)
CHIP_RELS = ((1, 0), (0, 1), (1, 1))


def all_gather_big(xs, *, name):
    n = len(xs)

    def body(*refs):
        x_refs, out_refs = refs[:n], refs[n:2 * n]
        send_sems, recv_sems, local_sems = refs[2 * n:]
        me = _me()
        sibling = _flip(me, (0, 0, 1))
        chips = [_flip(me, rel + (0,)) for rel in CHIP_RELS]

        def copy(i, k, block, to, src=None):
            dst = out_refs[i].at[_slot(block)]
            return pltpu.make_async_remote_copy(
                src_ref=dst if src is None else src, dst_ref=dst, send_sem=send_sems.at[i, k],
                recv_sem=recv_sems.at[i, k], device_id=to, device_id_type=MESH)

        sent, local = [], []
        for i in range(n):
            mine = pltpu.make_async_copy(x_refs[i], out_refs[i].at[_slot(me)], local_sems.at[i])
            mine.start()
            local.append(mine)
            sent += [copy(i, 0, me, sibling, src=x_refs[i])]
            sent += [copy(i, 1 + j, me, chip, src=x_refs[i]) for j, chip in enumerate(chips)]
        for cp in sent:
            cp.start()
        for i in range(n):
            for j, chip in enumerate(chips):
                copy(i, 1 + j, chip, me).wait_recv()
                passed = copy(i, 4 + j, chip, sibling)
                passed.start()
                sent.append(passed)
        for i in range(n):
            copy(i, 0, sibling, me).wait_recv()
            for j, chip in enumerate(chips):
                copy(i, 4 + j, _flip(chip, (0, 0, 1)), me).wait_recv()
        for cp in sent:
            cp.wait_send()
        for mine in local:
            mine.wait()

    return pl.pallas_call(
        body, name=name, in_specs=[HBM_SPEC] * n, out_specs=[HBM_SPEC] * n,
        out_shape=[jax.ShapeDtypeStruct((NDEV,) + x.shape, x.dtype) for x in xs],
        scratch_shapes=[pltpu.SemaphoreType.DMA((n, 7)), pltpu.SemaphoreType.DMA((n, 7)), pltpu.SemaphoreType.DMA((n,))],
    )(*xs)


def all_to_all_small(src, *, name):
    _, r, C = src.shape

    def body(src_ref, out_ref, send_sems, recv_sems):
        me = _me()
        my = _slot(me)
        out_ref[my] = src_ref[my]
        copies = []
        for k in range(1, NDEV):
            peer = _flip(me, ((k >> 2) & 1, (k >> 1) & 1, k & 1))
            cp = pltpu.make_async_remote_copy(
                src_ref=src_ref.at[_slot(peer)], dst_ref=out_ref.at[my], send_sem=send_sems.at[k - 1],
                recv_sem=recv_sems.at[k - 1], device_id=peer, device_id_type=MESH)
            cp.start()
            copies.append((cp, peer))
        for k, (cp, peer) in enumerate(copies):
            pltpu.make_async_remote_copy(
                src_ref=src_ref.at[my], dst_ref=out_ref.at[_slot(peer)], send_sem=send_sems.at[k],
                recv_sem=recv_sems.at[k], device_id=peer, device_id_type=MESH).wait_recv()
        for cp, _ in copies:
            cp.wait_send()

    vm = pl.BlockSpec(memory_space=pltpu.VMEM)
    return pl.pallas_call(
        body, name=name, in_specs=[vm], out_specs=vm, out_shape=jax.ShapeDtypeStruct(src.shape, src.dtype),
        scratch_shapes=[pltpu.SemaphoreType.DMA((7,)), pltpu.SemaphoreType.DMA((7,))],
    )(src)


def sum_slots(a, *, name):
    n, r, C = a.shape

    def body(a_ref, o_ref):
        s = a_ref[0]
        for k in range(1, n):
            s = s + a_ref[k]
        o_ref[...] = s

    vm = pl.BlockSpec(memory_space=pltpu.VMEM)
    return pl.pallas_call(body, name=name, in_specs=[vm], out_specs=vm,
                          out_shape=jax.ShapeDtypeStruct((r, C), F32))(a)


def rs_exchange_sibling(gs, *, name):
    n = len(gs)

    def body(*refs):
        g_refs, out_refs, send_sems, recv_sems = refs[:n], refs[n:2 * n], refs[2 * n], refs[2 * n + 1]
        sibling = _flip(_me(), (0, 0, 1))
        copies = []
        for i in range(n):
            for chip in range(4):
                cp = pltpu.make_async_remote_copy(
                    src_ref=g_refs[i].at[2 * chip + sibling[2]], dst_ref=out_refs[i].at[chip],
                    send_sem=send_sems.at[i, chip], recv_sem=recv_sems.at[i, chip], device_id=sibling,
                    device_id_type=MESH)
                cp.start()
                copies.append(cp)
        for cp in copies:
            cp.wait()

    return pl.pallas_call(
        body, name=name, in_specs=[HBM_SPEC] * n, out_specs=[HBM_SPEC] * n,
        out_shape=[jax.ShapeDtypeStruct((4,) + g.shape[1:], g.dtype) for g in gs],
        scratch_shapes=[pltpu.SemaphoreType.DMA((n, 4)), pltpu.SemaphoreType.DMA((n, 4))],
    )(*gs)


def rs_chip_partials(g, got, *, name):
    _, R, C = g.shape
    tr = _tile(R, 768, unit=16)

    def body(c_ref, g_ref, got_ref, o_ref):
        o_ref[...] = (g_ref[...].astype(F32) + got_ref[...].astype(F32)).astype(o_ref.dtype)

    c = jnp.reshape(lax.axis_index("c"), (1,)).astype(jnp.int32)
    return pl.pallas_call(
        body, name=name,
        grid_spec=pltpu.PrefetchScalarGridSpec(
            num_scalar_prefetch=1, grid=(4, R // tr),
            in_specs=[pl.BlockSpec((None, tr, C), lambda k, i, c_ref: (2 * k + c_ref[0], i, 0)),
                      pl.BlockSpec((None, tr, C), lambda k, i, c_ref: (k, i, 0))],
            out_specs=pl.BlockSpec((None, tr, C), lambda k, i, c_ref: (k, i, 0))),
        out_shape=jax.ShapeDtypeStruct((4, R, C), g.dtype), compiler_params=_params(("arbitrary", "arbitrary")),
    )(c, g, got)


def rs_exchange_chips(ps, *, name):
    n = len(ps)

    def body(*refs):
        copies = _chip_exchange_copies(refs[:n], refs[n:2 * n], refs[2 * n], refs[2 * n + 1])
        for cp in copies:
            cp.start()
        for cp in copies:
            cp.wait()

    return pl.pallas_call(
        body, name=name, in_specs=[HBM_SPEC] * n, out_specs=[HBM_SPEC] * n,
        out_shape=_chip_exchange_shapes(ps), scratch_shapes=_chip_exchange_sems(n),
    )(*ps)


def _chip_exchange_copies(p_refs, out_refs, send_sems, recv_sems):
    me = _me()
    copies = []
    for i, (p_ref, out_ref) in enumerate(zip(p_refs, out_refs)):
        for j, rel in enumerate(CHIP_RELS):
            peer = _flip(me, rel + (0,))
            copies.append(pltpu.make_async_remote_copy(
                src_ref=p_ref.at[2 * peer[0] + peer[1]], dst_ref=out_ref.at[j], send_sem=send_sems.at[i, j],
                recv_sem=recv_sems.at[i, j], device_id=peer, device_id_type=MESH))
    return copies


def _chip_exchange_shapes(ps):
    return [jax.ShapeDtypeStruct((3,) + p.shape[1:], p.dtype) for p in ps]


def _chip_exchange_sems(n):
    return [pltpu.SemaphoreType.DMA((n, 3)), pltpu.SemaphoreType.DMA((n, 3))]


def _gather_direct(start, x_refs, out_refs, send_sems, recv_sems, local_sems):
    me = _me()
    peers = [_flip(me, (0, 0, 1))] + [_flip(me, rel + (0,)) for rel in CHIP_RELS]
    for i, (x_ref, out_ref) in enumerate(zip(x_refs, out_refs)):
        local = pltpu.make_async_copy(x_ref, out_ref.at[_slot(me)], local_sems.at[i])
        local.start() if start else local.wait()
        for k, peer in enumerate(peers):
            def copy(block):
                return pltpu.make_async_remote_copy(
                    src_ref=x_ref, dst_ref=out_ref.at[_slot(block)], send_sem=send_sems.at[i, k],
                    recv_sem=recv_sems.at[i, k], device_id=peer, device_id_type=MESH)
            if start:
                copy(me).start()
            else:
                copy(me).wait_send()
                copy(peer).wait_recv()


def _gather_direct_sems(n):
    return [pltpu.SemaphoreType.DMA((n, 4)), pltpu.SemaphoreType.DMA((n, 4)), pltpu.SemaphoreType.DMA((n,))]


def gather_forward_to_sibling(gs, *, name):
    n = len(gs)

    def body(*refs):
        in_refs, out_refs, send_sems, recv_sems = refs[:n], refs[n:2 * n], refs[2 * n], refs[2 * n + 1]
        me = _me()
        sibling = _flip(me, (0, 0, 1))
        chips = [_flip(me, rel + (0,)) for rel in CHIP_RELS]
        sends, recvs = [], []
        for i in range(n):
            for j, chip in enumerate(chips):
                def copy(block):
                    return pltpu.make_async_remote_copy(
                        src_ref=in_refs[i].at[_slot(chip)], dst_ref=out_refs[i].at[_slot(block)],
                        send_sem=send_sems.at[i, j], recv_sem=recv_sems.at[i, j], device_id=sibling, device_id_type=MESH)
                sends.append(copy(chip))
                recvs.append(copy(_flip(chip, (0, 0, 1))))
        for cp in sends:
            cp.start()
        for cp in recvs:
            cp.wait_recv()
        for cp in sends:
            cp.wait_send()

    return pl.pallas_call(
        body, name=name, in_specs=[HBM_SPEC] * n, out_specs=[HBM_SPEC] * n,
        out_shape=[jax.ShapeDtypeStruct(g.shape, g.dtype) for g in gs],
        input_output_aliases={i: i for i in range(n)},
        scratch_shapes=[pltpu.SemaphoreType.DMA((n, 3)), pltpu.SemaphoreType.DMA((n, 3))],
    )(*gs)


def rs_final_sum(p, got, *, name):
    _, R, C = p.shape
    tr = _tile(R, 768, unit=16)

    def body(chip_ref, p_ref, got_ref, o_ref):
        s = p_ref[...].astype(F32)
        for j in range(3):
            s = s + got_ref[j].astype(F32)
        o_ref[...] = s

    mychip = jnp.reshape(2 * lax.axis_index("x") + lax.axis_index("y"), (1,)).astype(jnp.int32)
    return pl.pallas_call(
        body, name=name,
        grid_spec=pltpu.PrefetchScalarGridSpec(
            num_scalar_prefetch=1, grid=(R // tr,),
            in_specs=[pl.BlockSpec((None, tr, C), lambda i, chip_ref: (chip_ref[0], i, 0)),
                      pl.BlockSpec((3, tr, C), lambda i, chip_ref: (0, i, 0))],
            out_specs=pl.BlockSpec((tr, C), lambda i, chip_ref: (i, 0))),
        out_shape=jax.ShapeDtypeStruct((R, C), F32), compiler_params=_params(("arbitrary",)),
    )(mychip, p, got)


def reduce_scatter_big(gs):
    got = rs_exchange_sibling(gs, name="rs_sibling")
    parts = [rs_chip_partials(g, t, name=f"rs_chip_partials_{i}") for i, (g, t) in enumerate(zip(gs, got))]
    got2 = rs_exchange_chips(parts, name="rs_chips")
    return [rs_final_sum(p, t, name=f"rs_final_sum_{i}") for i, (p, t) in enumerate(zip(parts, got2))]


def _const_spec(shape):
    return pl.BlockSpec(shape, lambda *_: (0,) * len(shape))


def retention_tables(T):
    pos = jnp.arange(T, dtype=F32) - float(PAD)
    inv_freq = 1.0 / (10000.0 ** jnp.linspace(0.0, 1.0, 64, dtype=F32))
    ang = pos[:, None] * inv_freq[None, :]
    cos = jnp.repeat(jnp.cos(ang), 2, axis=1)
    sin = jnp.repeat(jnp.sin(ang), 2, axis=1) * jnp.tile(jnp.array([-1.0, 1.0], F32), 64)[None, :]
    lane = np.arange(LANES)
    perm = jnp.broadcast_to(jnp.asarray((lane[:, None] == (lane[None, :] ^ 1)).astype(np.float32)), (4, LANES, LANES))
    log_gamma = jnp.log1p(-jnp.exp2(-5.0 - jnp.arange(4, dtype=F32)))
    idx = jnp.arange(BLK, dtype=F32)
    diff = idx[:, None] - idx[None, :]
    intra = jnp.where(diff >= 0, jnp.exp(jnp.maximum(diff, 0.0) * log_gamma[:, None, None]), 0.0)
    zeta = jnp.exp((BLK - 1.0 - idx)[None, :] * log_gamma[:, None])
    xi = jnp.exp((idx + 1.0)[None, :] * log_gamma[:, None])
    bc = lambda t: jnp.broadcast_to(t[:, :, None], (4, BLK, LANES))
    return cos, sin, perm, intra, bc(zeta), bc(xi)


def _heads(x):
    return jnp.stack([x[:, h * LANES:(h + 1) * LANES] for h in range(4)])


def _unheads(y):
    return jnp.concatenate([y[h] for h in range(4)], axis=1)


def _ret_chunk(rq, rk, rv, rg, S, cos, sin, intra, zeta, xi, perm):
    q = rq * cos + hdot(rq, perm) * sin
    k = (rk * cos + hdot(rk, perm) * sin) * (128.0 ** -0.5)
    ret = bdot(bdot(q, _t(k)) * intra, rv) + bdot(q * xi, S)
    S_new = S * xi[..., BLK - 1:BLK, :] + bdot(_t(k * zeta), rv)
    c = ret - jnp.mean(ret, axis=-1, keepdims=True)
    out = c * lax.rsqrt(jnp.mean(c * c, axis=-1, keepdims=True) + EPS) * _silu(rg)
    return out, S_new


def _wide(off):
    return pl.BlockSpec((BLK, 4 * LANES), lambda n: (n, off))


def retention_fwd(p, tables, *, name):
    T = p.shape[0]
    N = T // BLK
    cos, sin, perm, intra, zeta, xi = tables

    def body(rq, rk, rv, rg, cos_ref, sin_ref, in_ref, ze_ref, xi_ref, perm_ref, out_ref, sall_ref, s_scr):
        @pl.when(pl.program_id(0) == 0)
        def _():
            s_scr[...] = jnp.zeros_like(s_scr)

        S = s_scr[...]
        sall_ref[...] = S
        out, S_new = _ret_chunk(_heads(rq[...]), _heads(rk[...]), _heads(rv[...]), _heads(rg[...]), S, cos_ref[...],
                                sin_ref[...], in_ref[...], ze_ref[...], xi_ref[...], perm_ref[...])
        out_ref[...] = _unheads(out).astype(BF16)
        s_scr[...] = S_new

    rowtab = pl.BlockSpec((BLK, LANES), lambda n: (n, 0))
    tab = _const_spec((4, BLK, LANES))
    return pl.pallas_call(
        body, name=name, grid=(N,),
        in_specs=[_wide(0), _wide(1), _wide(2), _wide(3), rowtab, rowtab, tab, tab, tab, tab],
        out_specs=[_wide(0), pl.BlockSpec((None, 4, LANES, LANES), lambda n: (n, 0, 0, 0))],
        out_shape=[jax.ShapeDtypeStruct((T, 512), BF16), jax.ShapeDtypeStruct((N, 4, LANES, LANES), F32)],
        scratch_shapes=[pltpu.VMEM((4, LANES, LANES), F32)],
        compiler_params=_params(("arbitrary",)),
    )(p, p, p, p, cos, sin, intra, zeta, xi, perm)


def _row_mask(n):
    return (n * BLK + _iota2((BLK, 1), 0) >= PAD).astype(F32)


def retention_bwd(p, sall, dmixed, tables, *, name):
    T = p.shape[0]
    N = T // BLK
    cos, sin, perm, intra, zeta, xi = tables

    def body(rq, rk, rv, rg, cos_ref, sin_ref, in_ref, ze_ref, xi_ref, perm_ref, sall_ref, do_ref, drq, drk, drv, drg,
             ds_scr):
        n = N - 1 - pl.program_id(0)

        @pl.when(pl.program_id(0) == 0)
        def _():
            ds_scr[...] = jnp.zeros_like(ds_scr)

        f = lambda a, b, c, d, s: _ret_chunk(a, b, c, d, s, cos_ref[...], sin_ref[...], in_ref[...], ze_ref[...],
                                             xi_ref[...], perm_ref[...])
        _, vjp = jax.vjp(f, _heads(rq[...]), _heads(rk[...]), _heads(rv[...]), _heads(rg[...]), sall_ref[...])
        g = vjp((_heads(do_ref[...]), ds_scr[...]))
        mask = _row_mask(n)
        for ref, val in zip((drq, drk, drv, drg), g[:4]):
            ref[...] = _unheads(val) * mask
        ds_scr[...] = g[4]

    def rwide(off):
        return pl.BlockSpec((BLK, 4 * LANES), lambda n: (N - 1 - n, off))

    rowtab = pl.BlockSpec((BLK, LANES), lambda n: (N - 1 - n, 0))
    tab = _const_spec((4, BLK, LANES))
    return pl.pallas_call(
        body, name=name, grid=(N,),
        in_specs=[rwide(0), rwide(1), rwide(2), rwide(3), rowtab, rowtab, tab, tab, tab, tab,
                  pl.BlockSpec((None, 4, LANES, LANES), lambda n: (N - 1 - n, 0, 0, 0)), rwide(0)],
        out_specs=[rwide(0)] * 4, out_shape=[jax.ShapeDtypeStruct((T, 512), F32)] * 4,
        scratch_shapes=[pltpu.VMEM((4, LANES, LANES), F32)],
        compiler_params=_params(("arbitrary",)),
    )(p, p, p, p, cos, sin, intra, zeta, xi, perm, sall, dmixed)


def conv_silu_fwd(p, w, send, *, name):
    T = p.shape[0]
    N = T // BLK
    ns = len(send)

    def body(x_ref, xp_ref, w_ref, *refs):
        x_refs, o_ref, g_refs, sems = refs[:ns], refs[ns], refs[ns + 1:2 * ns + 1], refs[2 * ns + 1:]
        n, part = pl.program_id(0), pl.program_id(1)

        @pl.when((n == 0) & (part == 0))
        def _():
            _gather_direct(True, x_refs, g_refs, *sems)

        @pl.when((n == N - 1) & (part == 2))
        def _():
            _gather_direct(False, x_refs, g_refs, *sems)

        cur = x_ref[...]
        cat = jnp.concatenate([jnp.where(n > 0, xp_ref[...], 0.0), cur], axis=0)
        y = w_ref[3:4, :] * cur
        for s in (1, 2, 3):
            y = y + w_ref[3 - s:4 - s, :] * pltpu.roll(cat, s, 0)[BLK:]
        o_ref[...] = _silu(y)

    cw = 4 * LANES
    res = pl.pallas_call(
        body, name=name, grid=(N, 3),
        in_specs=[pl.BlockSpec((BLK, cw), lambda n, c: (n, 4 + c)),
                  pl.BlockSpec((BLK, cw), lambda n, c: (jnp.maximum(n - 1, 0), 4 + c)),
                  pl.BlockSpec((4, cw), lambda n, c: (0, c))] + [HBM_SPEC] * ns,
        out_specs=[pl.BlockSpec((BLK, cw), lambda n, c: (n, c))] + [HBM_SPEC] * ns,
        out_shape=[jax.ShapeDtypeStruct((T, 1536), F32)] + [jax.ShapeDtypeStruct((NDEV,) + x.shape, x.dtype) for x in send],
        scratch_shapes=_gather_direct_sems(ns), compiler_params=_params(("arbitrary", "arbitrary")),
    )(p, p, w, *send)
    return res[0], res[1:]


def conv_silu_bwd(p, w, dact, part, *, name):
    T = p.shape[0]
    N = T // BLK
    cw = 4 * LANES

    def body(xp_ref, x_ref, xn_ref, w_ref, da_ref, dan_ref, dx_ref, dw_ref):
        n = pl.program_id(0)
        last = n == N - 1
        cat = jnp.concatenate([jnp.where(n > 0, xp_ref[...], 0.0), x_ref[...], jnp.where(last, 0.0, xn_ref[...])], axis=0)
        shifted = [cat] + [pltpu.roll(cat, s, 0) for s in (1, 2, 3)]
        y = w_ref[3:4, :] * shifted[0]
        for s in (1, 2, 3):
            y = y + w_ref[3 - s:4 - s, :] * shifted[s]
        y = y[BLK:]
        da = jnp.concatenate([da_ref[...], jnp.where(last, 0.0, dan_ref[...])], axis=0)
        sg = _sigmoid(y)
        dy = da * sg * (1.0 + y * (1.0 - sg))
        dx = w_ref[3:4, :] * dy[:BLK]
        for s in (1, 2, 3):
            dx = dx + w_ref[3 - s:4 - s, :] * pltpu.roll(dy, 2 * BLK - s, 0)[:BLK]
        dx_ref[...] = dx * _row_mask(n)

        @pl.when(n == 0)
        def _():
            dw_ref[...] = jnp.zeros_like(dw_ref)

        for s in (0, 1, 2, 3):
            dw_ref[3 - s:4 - s, :] += jnp.sum(dy[:BLK] * shifted[s][BLK:2 * BLK], axis=0, keepdims=True)

    def xs(d):
        return pl.BlockSpec((BLK, cw), lambda n: (jnp.clip(n + d, 0, N - 1), 4 + part))

    return pl.pallas_call(
        body, name=name, grid=(N,),
        in_specs=[xs(-1), xs(0), xs(1), pl.BlockSpec((4, cw), lambda n: (0, part)),
                  pl.BlockSpec((BLK, cw), lambda n: (n, 0)),
                  pl.BlockSpec((BLK, cw), lambda n: (jnp.minimum(n + 1, N - 1), 0))],
        out_specs=[pl.BlockSpec((BLK, cw), lambda n: (n, 0)), pl.BlockSpec((4, cw), lambda n: (0, 0))],
        out_shape=[jax.ShapeDtypeStruct((T, 512), F32), jax.ShapeDtypeStruct((4, 512), F32)],
        compiler_params=_params(("arbitrary",)),
    )(p, p, p, w, dact, dact)


def _softplus(x):
    return jnp.maximum(x, 0.0) + jnp.log1p(jnp.exp(-jnp.abs(x)))


def _pick4(tile, off):
    return jnp.stack([_lane_pick(tile, off + h) for h in range(4)])


def _spread4(v4, off, rows):
    lane = _iota2((rows, LANES), 1)
    out = jnp.where(lane == off, v4[0], 0.0)
    for h in range(1, 4):
        out = out + jnp.where(lane == off + h, v4[h], 0.0)
    return out


def _gdn_chunk(qa, ka, va, z, braw, araw, S, alog, dtb, onorm, rowmask, lincl):
    r, c = _iota2((BLK, BLK), 0), _iota2((BLK, BLK), 1)
    incl, strict = r >= c, r > c
    eye = (r == c).astype(F32)
    q = qa * lax.rsqrt(jnp.sum(qa * qa, axis=-1, keepdims=True) + EPS) * (128.0 ** -0.5)
    k = ka * lax.rsqrt(jnp.sum(ka * ka, axis=-1, keepdims=True) + EPS)
    beta = _sigmoid(braw) * rowmask
    g = -jnp.exp(alog) * _softplus(araw + dtb) * rowmask
    gc = hdot(lincl, jnp.broadcast_to(g, qa.shape))
    decay = jnp.where(incl, jnp.exp(jnp.where(incl, gc - _t(gc), 0.0)), 0.0)
    kb = k * beta
    amat = jnp.where(strict, bdot(kb, _t(k)) * decay, 0.0)
    m = -amat
    inv = eye + m
    pw = hdot(m, m)
    for t in range(6):
        inv = inv + hdot(inv, pw)
        if t < 5:
            pw = hdot(pw, pw)
    egc = jnp.exp(gc)
    u = hdot(inv, va * beta)
    w = hdot(inv, kb * egc)
    qk = jnp.where(incl, bdot(q, _t(k)) * decay, 0.0)
    glast = gc[..., BLK - 1:BLK, :]
    vnew = u - bdot(w, S)
    o = bdot(q * egc, S) + bdot(qk, vnew)
    S_new = S * jnp.exp(glast) + bdot(_t(k * jnp.exp(glast - gc)), vnew)
    out = o * lax.rsqrt(jnp.mean(o * o, axis=-1, keepdims=True) + EPS) * onorm * _silu(z)
    return out, S_new


def _lincl():
    i = np.arange(BLK)
    return jnp.broadcast_to(jnp.asarray((i[:, None] >= i[None, :]).astype(np.float32)), (4, BLK, BLK))


def gdn_fwd(act, p, alog, dtb, onorm, send, *, name):
    T = p.shape[0]
    N = T // BLK
    ns = len(send)

    def body(qa, ka, va, z, ba, alog_ref, dtb_ref, on_ref, l_ref, *refs):
        x_refs, (out_ref, sall_ref), g_refs = refs[:ns], refs[ns:ns + 2], refs[ns + 2:2 * ns + 2]
        s_scr, sems = refs[2 * ns + 2], refs[2 * ns + 3:]
        n = pl.program_id(0)

        @pl.when(n == 0)
        def _():
            s_scr[...] = jnp.zeros_like(s_scr)
            _gather_direct(True, x_refs, g_refs, *sems)

        @pl.when(n == N - 1)
        def _():
            _gather_direct(False, x_refs, g_refs, *sems)

        S = s_scr[...]
        sall_ref[...] = S
        out, S_new = _gdn_chunk(_heads(qa[...]), _heads(ka[...]), _heads(va[...]), _heads(z[...]), _pick4(ba[...], 0),
                                _pick4(ba[...], 4), S, _pick4(alog_ref[...], 0), _pick4(dtb_ref[...], 0), on_ref[...],
                                _row_mask(n), l_ref[...])
        out_ref[...] = _unheads(out).astype(BF16)
        s_scr[...] = S_new

    vec = _const_spec((1, LANES))
    res = pl.pallas_call(
        body, name=name, grid=(N,),
        in_specs=[_wide(0), _wide(1), _wide(2), _wide(7), pl.BlockSpec((BLK, LANES), lambda n: (n, 32)), vec, vec, vec,
                  _const_spec((4, BLK, BLK))] + [HBM_SPEC] * ns,
        out_specs=[_wide(0), pl.BlockSpec((None, 4, LANES, LANES), lambda n: (n, 0, 0, 0))] + [HBM_SPEC] * ns,
        out_shape=[jax.ShapeDtypeStruct((T, 512), BF16), jax.ShapeDtypeStruct((N, 4, LANES, LANES), F32)]
        + [jax.ShapeDtypeStruct((NDEV,) + x.shape, x.dtype) for x in send],
        scratch_shapes=[pltpu.VMEM((4, LANES, LANES), F32)] + _gather_direct_sems(ns),
        compiler_params=_params(("arbitrary",)),
    )(act, act, act, p, p, alog, dtb, onorm, _lincl(), *send)
    return res[0], res[1], res[2:]


def gdn_bwd(act, p, alog, dtb, onorm, sall, dmixed, partials, *, name):
    T = p.shape[0]
    N = T // BLK
    ns = len(partials)

    def body(qa, ka, va, z, ba, alog_ref, dtb_ref, on_ref, l_ref, sall_ref, do_ref, *refs):
        p_refs, (dq_ref, dk_ref, dv_ref, dz_ref, dba_ref, dal_ref, ddt_ref, don_ref) = refs[:ns], refs[ns:ns + 8]
        got_refs, ds_scr, sems = refs[ns + 8:2 * ns + 8], refs[2 * ns + 8], refs[2 * ns + 9:]
        step = pl.program_id(0)
        n = N - 1 - step

        @pl.when(step == 0)
        def _():
            ds_scr[...] = jnp.zeros_like(ds_scr)
            dal_ref[...] = jnp.zeros_like(dal_ref)
            ddt_ref[...] = jnp.zeros_like(ddt_ref)
            don_ref[...] = jnp.zeros_like(don_ref)
            for cp in _chip_exchange_copies(p_refs, got_refs, *sems):
                cp.start()

        @pl.when(step == N - 1)
        def _():
            for cp in _chip_exchange_copies(p_refs, got_refs, *sems):
                cp.wait()

        rowmask, lincl = _row_mask(n), l_ref[...]
        f = lambda *a: _gdn_chunk(*a, rowmask, lincl)
        _, vjp = jax.vjp(f, _heads(qa[...]), _heads(ka[...]), _heads(va[...]), _heads(z[...]), _pick4(ba[...], 0),
                         _pick4(ba[...], 4), sall_ref[...], _pick4(alog_ref[...], 0), _pick4(dtb_ref[...], 0),
                         on_ref[...])
        g = vjp((_heads(do_ref[...]), ds_scr[...]))
        dq_ref[...] = _unheads(g[0]) * rowmask
        dk_ref[...] = _unheads(g[1]) * rowmask
        dv_ref[...] = _unheads(g[2]) * rowmask
        dz_ref[...] = _unheads(g[3]) * rowmask
        dba_ref[...] = (_spread4(g[4], 0, BLK) + _spread4(g[5], 4, BLK)) * rowmask
        ds_scr[...] = g[6]
        dal_ref[...] += _spread4(g[7], 0, 1)
        ddt_ref[...] += _spread4(g[8], 0, 1)
        don_ref[...] += g[9]

    def rwide(off):
        return pl.BlockSpec((BLK, 4 * LANES), lambda s: (N - 1 - s, off))

    vec = _const_spec((1, LANES))
    col = pl.BlockSpec((BLK, LANES), lambda s: (N - 1 - s, 0))
    res = pl.pallas_call(
        body, name=name, grid=(N,),
        in_specs=[rwide(0), rwide(1), rwide(2), rwide(7), pl.BlockSpec((BLK, LANES), lambda s: (N - 1 - s, 32)), vec, vec,
                  vec, _const_spec((4, BLK, BLK)),
                  pl.BlockSpec((None, 4, LANES, LANES), lambda s: (N - 1 - s, 0, 0, 0)), rwide(1)] + [HBM_SPEC] * ns,
        out_specs=[rwide(0)] * 4 + [col, vec, vec, vec] + [HBM_SPEC] * ns,
        out_shape=[jax.ShapeDtypeStruct((T, 512), F32)] * 4 + [jax.ShapeDtypeStruct((T, LANES), F32)]
        + [jax.ShapeDtypeStruct((1, LANES), F32)] * 3 + _chip_exchange_shapes(partials),
        scratch_shapes=[pltpu.VMEM((4, LANES, LANES), F32)] + _chip_exchange_sems(ns),
        compiler_params=_params(("arbitrary",)),
    )(act, act, act, p, p, alog, dtb, onorm, _lincl(), sall, dmixed, *partials)
    return res[:8], res[8:]


NEG = -1e30


def _swa_block(q, k0, kp, kc, v0, vp, vc, sink, n):
    r, c = _iota2((BLK, BLK), 0), _iota2((BLK, BLK), 1)
    m0 = (c >= PAD) & (c <= n * BLK + r)
    mp = (n >= 2) & (c > r)
    mc = (n >= 1) & (r >= c)
    b = lambda t: jnp.broadcast_to(t, (4,) + t.shape)
    qs = q * (64.0 ** -0.5)
    s0 = jnp.where(m0, bdot(qs, _t(b(k0))), NEG)
    sp = jnp.where(mp, bdot(qs, _t(b(kp))), NEG)
    sc = jnp.where(mc, bdot(qs, _t(b(kc))), NEG)
    mx = jnp.maximum(jnp.max(jnp.maximum(jnp.maximum(s0, sp), sc), axis=-1, keepdims=True), sink)
    mx = lax.stop_gradient(mx)
    p0, pp, pc = jnp.exp(s0 - mx), jnp.exp(sp - mx), jnp.exp(sc - mx)
    den = (jnp.sum(p0, axis=-1, keepdims=True) + jnp.sum(pp, axis=-1, keepdims=True)
           + jnp.sum(pc, axis=-1, keepdims=True) + jnp.exp(sink - mx))
    return (bdot(p0, b(v0)) + bdot(pp, b(vp)) + bdot(pc, b(vc))) / den


def _swa_specs():
    rows = (lambda n: 0, lambda n: jnp.maximum(n - 1, 0), lambda n: n)

    def kv_spec(off, row):
        return pl.BlockSpec((BLK, LANES), lambda g, n: (row(n), off + g))

    q = pl.BlockSpec((BLK, 4 * LANES), lambda g, n: (n, g))
    return q, [kv_spec(off, row) for off in (8, 10) for row in rows]


def swa_fwd(p2, sinkrow, *, name):
    T = p2.shape[0]
    N = T // BLK

    def body(q, k0, kp, kc, v0, vp, vc, sink_ref, o_ref):
        g, n = pl.program_id(0), pl.program_id(1)
        f32 = lambda ref: ref[...].astype(F32)
        o = _swa_block(_heads(f32(q)), f32(k0), f32(kp), f32(kc), f32(v0), f32(vp), f32(vc),
                       _pick4(sink_ref[...], 4 * g), n)
        o_ref[...] = _unheads(o).astype(BF16)

    q, kv = _swa_specs()
    return pl.pallas_call(
        body, name=name, grid=(2, N), in_specs=[q] + kv + [_const_spec((1, LANES))],
        out_specs=q, out_shape=jax.ShapeDtypeStruct((T, 1024), BF16),
        compiler_params=_params(("arbitrary", "arbitrary")),
    )(p2, p2, p2, p2, p2, p2, p2, sinkrow)


def swa_bwd(p2, sinkrow, dmixed, *, name):
    T = p2.shape[0]
    N = T // BLK

    def body(q, k0, kp, kc, v0, vp, vc, sink_ref, do_ref, dq_ref, dk_ref, dv_ref, dsink_ref):
        g, n = pl.program_id(0), pl.program_id(1)

        @pl.when(n == 0)
        def _():
            dk_ref[...] = jnp.zeros_like(dk_ref)
            dv_ref[...] = jnp.zeros_like(dv_ref)

        @pl.when((g == 0) & (n == 0))
        def _():
            dsink_ref[...] = jnp.zeros_like(dsink_ref)

        f = lambda *a: _swa_block(*a, n)
        f32 = lambda ref: ref[...].astype(F32)
        _, vjp = jax.vjp(f, _heads(f32(q)), f32(k0), f32(kp), f32(kc), f32(v0), f32(vp), f32(vc),
                         _pick4(sink_ref[...], 4 * g))
        dq, dk0, dkp, dkc, dv0, dvp, dvc, dsink = vjp(_heads(do_ref[...]))
        dq_ref[...] = _unheads(dq)
        prev = pl.ds(pl.multiple_of(jnp.maximum(n - 1, 0) * BLK, BLK), BLK)
        cur = pl.ds(pl.multiple_of(n * BLK, BLK), BLK)
        for ref, d0, dp, dc in ((dk_ref, dk0, dkp, dkc), (dv_ref, dv0, dvp, dvc)):
            ref[0:BLK, :] += d0
            ref[prev, :] += dp
            ref[cur, :] += dc
        dsink_ref[...] += _spread4(dsink, 4 * g, 1)

    qspec, kv = _swa_specs()
    slab = pl.BlockSpec((T, LANES), lambda g, n: (0, g))
    return pl.pallas_call(
        body, name=name, grid=(2, N), in_specs=[qspec] + kv + [_const_spec((1, LANES)), qspec],
        out_specs=[qspec, slab, slab, _const_spec((1, LANES))],
        out_shape=[jax.ShapeDtypeStruct((T, 1024), F32), jax.ShapeDtypeStruct((T, 256), F32),
                   jax.ShapeDtypeStruct((T, 256), F32), jax.ShapeDtypeStruct((1, LANES), F32)],
        compiler_params=_params(("arbitrary", "arbitrary")),
    )(p2, p2, p2, p2, p2, p2, p2, sinkrow, dmixed)


def _split_dot(x, m):
    rows = x.shape[0]
    hi = x.astype(BF16)
    lo = (x - hi.astype(F32)).astype(BF16)
    r = _nn(jnp.concatenate([hi, lo], axis=0), m)
    return r[:rows] + r[rows:]


def _tri(strict):
    i = np.arange(BLK)
    m = (i[:, None] > i[None, :]) if strict else (i[:, None] >= i[None, :])
    return jnp.asarray(m.astype(np.float32), dtype=BF16)


def _later_and_row_sums(x, m):
    return _split_dot(x, m), jnp.broadcast_to(jnp.sum(x, axis=1, keepdims=True), x.shape)


SB_PAIR = 2
SB_FWD_GROUP = 4


def _sb_positions():
    r, s = _iota2((BLK, BLK), 0), _iota2((BLK, BLK), 1)
    return s - r, s


def _sb_weights(qbs, ks, base, n, pos, carries, after):
    nh, kb = len(qbs), len(ks[0])
    zs = [[_nt(qbs[h], ks[h][c]) for c in range(kb)] for h in range(nh)]
    valid = [(pos[0] < (n - base - c) * BLK) & (pos[1] >= PAD - (base + c) * BLK) for c in range(kb)]
    lb = [[None] * kb for _ in range(nh)]
    sums = [[None] * kb for _ in range(nh)]
    for c in range(kb):
        for h in range(nh):
            z = jnp.where(valid[c], zs[h][c], NEG)
            lb[h][c] = jnp.minimum(z, 0.0) - jnp.log(1.0 + jnp.exp(-jnp.abs(z)))
            sums[h][c] = _later_and_row_sums(lb[h][c] - z, after)
    a = [[None] * kb for _ in range(nh)]
    carries = list(carries)
    for c in reversed(range(kb)):
        for h in range(nh):
            a[h][c] = jnp.exp(lb[h][c] + carries[h] + sums[h][c][0])
            carries[h] = carries[h] + sums[h][c][1]
    return valid, lb, a, carries


def _key_blocks(n_blocks):
    return next(k for k in (5, 3, 1) if n_blocks % k == 0)


def sb_fwd(p2, *, name):
    T = p2.shape[0]
    N = T // BLK
    kb = _key_blocks(N)
    nh = SB_FWD_GROUP
    heads = [slice(h * LANES, (h + 1) * LANES) for h in range(nh)]

    def body(q_ref, k_ref, v_ref, after_ref, o_ref, of_ref):
        n = pl.program_id(1)
        qbs = [(q_ref[:, hs].astype(F32) * (64.0 ** -0.5)).astype(BF16) for hs in heads]
        after, pos = after_ref[...], _sb_positions()
        nsup = n // kb + 1

        def step(t, c):
            accs, carries = c
            base = (nsup - 1 - t) * kb
            rows = [pl.ds(pl.multiple_of((base + sub) * BLK, BLK), BLK) for sub in range(kb)]
            ks = [[k_ref[r, hs] for r in rows] for hs in heads]
            _, _, a, carries = _sb_weights(qbs, ks, base, n, pos, carries, after)
            accs = list(accs)
            for sub, r in enumerate(rows):
                for h, hs in enumerate(heads):
                    accs[h] = accs[h] + _nn(a[h][sub].astype(BF16), v_ref[r, hs])
            return accs, carries

        zero = [jnp.zeros((BLK, LANES), F32)] * nh
        accs, _ = lax.fori_loop(0, nsup, step, (zero, zero))
        acc = jnp.concatenate(accs, axis=1)
        o_ref[...] = acc.astype(BF16)
        of_ref[...] = acc

    wide = nh * LANES

    def slab(off):
        return pl.BlockSpec((T, wide), lambda g, n: (0, off + g))

    def blk(off):
        return pl.BlockSpec((BLK, wide), lambda g, n: (n, off + g))

    return pl.pallas_call(
        body, name=name, grid=(8 // nh, N),
        in_specs=[blk(12 // nh), slab(20 // nh), slab(28 // nh), _const_spec((BLK, BLK))],
        out_specs=[blk(0), blk(0)],
        out_shape=[jax.ShapeDtypeStruct((T, 1024), BF16), jax.ShapeDtypeStruct((T, 1024), F32)],
        compiler_params=_params(("arbitrary", "arbitrary")),
    )(p2, p2, p2, _tri(True))


def sb_bwd(p2, o, dmixed, *, name):
    T = p2.shape[0]
    N = T // BLK
    kb = _key_blocks(N)

    heads = [slice(h * LANES, (h + 1) * LANES) for h in range(SB_PAIR)]
    scale = 64.0 ** -0.5

    def body(q_ref, k_ref, v_ref, after_ref, from_ref, o_ref, do_ref, dq_ref, dk_ref, dv_ref, dkt_scr, dvt_scr):
        n = pl.program_id(1)

        @pl.when(n == 0)
        def _():
            dkt_scr[...] = jnp.zeros_like(dkt_scr)
            dvt_scr[...] = jnp.zeros_like(dvt_scr)

        qbs, qts, dobs, dots, totals = [], [], [], [], []
        for hs in heads:
            qs = q_ref[:, hs].astype(F32) * scale
            do = do_ref[:, hs]
            qbs.append(qs.astype(BF16))
            qts.append(qs.T.astype(BF16))
            dobs.append(do.astype(BF16))
            dots.append(do.T.astype(BF16))
            total = jnp.sum(dobs[-1].astype(F32) * o_ref[:, hs], axis=1, keepdims=True)
            totals.append(jnp.broadcast_to(total, (BLK, LANES)))
        after, frm, pos = after_ref[...], from_ref[...], _sb_positions()
        nsup = n // kb + 1

        def step(t, c):
            dqs, carries, gcarries = c
            base = (nsup - 1 - t) * kb
            rows = [pl.ds(pl.multiple_of((base + sub) * BLK, BLK), BLK) for sub in range(kb)]
            ks = [[k_ref[r, hs] for r in rows] for hs in heads]
            valid, lb, a, carries = _sb_weights(qbs, ks, base, n, pos, carries, after)
            das = [[_nt(dobs[h], v_ref[r, hs]) for r in rows] for h, hs in enumerate(heads)]
            ab = [[a[h][sub].astype(BF16) for sub in range(kb)] for h in range(SB_PAIR)]
            g = [[None] * kb for _ in heads]
            sums = [[None] * kb for _ in heads]
            for sub in range(kb):
                for h in range(SB_PAIR):
                    g[h][sub] = das[h][sub] * ab[h][sub].astype(F32)
                    sums[h][sub] = _later_and_row_sums(g[h][sub], frm)
            dqs, gcarries = list(dqs), list(gcarries)
            for sub in reversed(range(kb)):
                for h in range(SB_PAIR):
                    before = totals[h] - (gcarries[h] + sums[h][sub][0])
                    gcarries[h] = gcarries[h] + sums[h][sub][1]
                    beta = jnp.exp(lb[h][sub])
                    dz = (g[h][sub] - beta * (g[h][sub] + before)).astype(BF16)
                    dqs[h] = dqs[h] + _nn(dz, ks[h][sub])
                    dkt_scr[h * N + base + sub] += _nn(qts[h], dz)
                    dvt_scr[h * N + base + sub] += _nn(dots[h], ab[h][sub])
            return dqs, carries, gcarries

        zero = [jnp.zeros((BLK, LANES), F32)] * SB_PAIR
        dqs, _, _ = lax.fori_loop(0, nsup, step, (zero, zero, zero))
        dq_ref[...] = (jnp.concatenate(dqs, axis=1) * scale).astype(dq_ref.dtype)

        @pl.when(n == N - 1)
        def _():
            def flush(j, _):
                rows = pl.ds(pl.multiple_of(j * BLK, BLK), BLK)
                for h, hs in enumerate(heads):
                    dk_ref[rows, hs] = dkt_scr[h * N + j].T.astype(dk_ref.dtype)
                    dv_ref[rows, hs] = dvt_scr[h * N + j].T.astype(dv_ref.dtype)
                return 0

            lax.fori_loop(0, N, flush, 0)

    wide = SB_PAIR * LANES

    def slab(off):
        return pl.BlockSpec((T, wide), lambda g, n: (0, off + g))

    def blk(off):
        return pl.BlockSpec((BLK, wide), lambda g, n: (n, off + g))

    tri = _const_spec((BLK, BLK))
    return pl.pallas_call(
        body, name=name, grid=(8 // SB_PAIR, N),
        in_specs=[blk(12 // SB_PAIR), slab(20 // SB_PAIR), slab(28 // SB_PAIR), tri, tri, blk(0), blk(8 // SB_PAIR)],
        out_specs=[blk(0), slab(0), slab(0)],
        out_shape=[jax.ShapeDtypeStruct((T, 1024), BF16)] * 3,
        scratch_shapes=[pltpu.VMEM((SB_PAIR * N, LANES, LANES), F32), pltpu.VMEM((SB_PAIR * N, LANES, LANES), F32)],
        compiler_params=_params(("arbitrary", "arbitrary")),
    )(p2, p2, p2, _tri(True), _tri(False), o, dmixed)


def ffn_fwd(h, g_pre, g_post, wg, wu, wd, tag):
    u, gate, up, act = norm_mm(h, g_pre, (wg, wu), swiglu=True, wt=True, name=f"ffn_up_{tag}")
    y, h_new = mm_norm_res([act], [wd], h, g_post, 0.5, name=f"ffn_down_{tag}")
    return h_new, (h, u, gate, up, y)


def ffn_bwd(saved, dh, g_pre, g_post, wg, wu, wd, tag):
    h, u, gate, up, y = saved
    dy, dg_post, dgate, dup, act = normbwd_mm_nt(dh, y, g_post, wd, 0.5, (gate, up), name=f"ffn_bwd_down_{tag}")
    dwd = mm_tn(act, dy, name=f"ffn_dwd_{tag}")
    dwg = mm_tn(dgate, u, name=f"ffn_dwg_{tag}")
    dwu = mm_tn(dup, u, name=f"ffn_dwu_{tag}")
    dh_in, dg_pre = mm_nt_normbwd([dgate, dup], [wg, wu], h, g_pre, dh, wt=True, name=f"ffn_bwd_up_{tag}")
    return dh_in, (dg_pre, dg_post), (dwg, dwu, dwd)


def _lane_row(v):
    v = v.reshape(1, -1)
    return jnp.pad(v, ((0, 0), (0, LANES - v.shape[1])))


AB_WIDTHS = (512,) * 8 + (LANES,)


def mixer_ab_fwd(h, g_pre, g_post, w_in, conv_w, a_log, dt_bias, out_norm, w_out, tables, send_conv, send_gdn):
    u, p = norm_mm(h, g_pre, (w_in,), swiglu=False, name="ab_in")
    ret, sall_r = retention_fwd(p, tables, name="retention_fwd")
    act, gathered_conv = conv_silu_fwd(p, conv_w, send_conv, name="conv_fwd")
    gdn, sall_g, gathered_gdn = gdn_fwd(act, p, _lane_row(a_log), _lane_row(dt_bias), out_norm.reshape(1, LANES),
                                        send_gdn, name="gdn_fwd")
    y, h_new = mm_norm_res([ret, gdn], [w_out[:512], w_out[512:]], h, g_post, 1.0, name="ab_out")
    return h_new, (h, u, p, ret, sall_r, act, gdn, sall_g, y), list(gathered_conv) + list(gathered_gdn)


def mixer_ab_bwd(saved, dh, g_pre, g_post, w_in, conv_w, a_log, dt_bias, out_norm, w_out, tables, partials):
    h, u, p, ret, sall_r, act, gdn, sall_g, y = saved
    dy, dg_post, dmixed = normbwd_mm_nt(dh, y, g_post, w_out, 1.0, name="ab_bwd_out")
    dw_out = jnp.concatenate([mm_tn(ret, dy, name="ab_dwout_ret"), mm_tn(gdn, dy, name="ab_dwout_gdn")], axis=0)
    pieces = list(retention_bwd(p, sall_r, dmixed, tables, name="retention_bwd"))
    (dqa, dka, dva, dz, dba, dalog, ddtb, donorm), arrived = gdn_bwd(
        act, p, _lane_row(a_log), _lane_row(dt_bias), out_norm.reshape(1, LANES), sall_g, dmixed, partials,
        name="gdn_bwd")
    dconv = []
    for part, dact in enumerate((dqa, dka, dva)):
        dx, dw = conv_silu_bwd(p, conv_w, dact, part, name=f"conv_bwd_{part}")
        pieces.append(dx)
        dconv.append(dw)
    pieces += [dz, dba]
    offs = np.cumsum((0,) + AB_WIDTHS)
    w_parts = [w_in[:, a:b] for a, b in zip(offs[:-1], offs[1:])]
    dh_in, dg_pre = mm_nt_normbwd(pieces, w_parts, h, g_pre, dh, name="ab_bwd_in")
    dw_in = jnp.concatenate([mm_tn(u, pc, name=f"ab_dwin_{i}") for i, pc in enumerate(pieces)], axis=1)
    small = (jnp.concatenate(dconv, axis=1), dalog[:, :4], ddtb[:, :4], donorm)
    return dh_in, (dg_pre, dg_post), (dw_in, dw_out), small, arrived


CD_WIDTHS = (1024, 256, 256, 1024, 1024, 1024)


def mixer_cd_fwd(h, g_pre, g_post, w_in, sinks, w_out):
    u, p2 = norm_mm(h, g_pre, (w_in,), swiglu=False, out_dtype=BF16, name="cd_in")
    swa = swa_fwd(p2, _lane_row(sinks), name="swa_fwd")
    sb, sb_f32 = sb_fwd(p2, name="sb_fwd")
    y, h_new = mm_norm_res([swa, sb], [w_out[:1024], w_out[1024:]], h, g_post, 1.0, name="cd_out")
    return h_new, (h, u, p2, swa, sb, sb_f32, y)


def mixer_cd_bwd(saved, dh, g_pre, g_post, w_in, sinks, w_out):
    h, u, p2, swa, sb, sb_f32, y = saved
    dy, dg_post, dmixed = normbwd_mm_nt(dh, y, g_post, w_out, 1.0, name="cd_bwd_out")
    dw_out = jnp.concatenate([mm_tn(swa, dy, name="cd_dwout_swa"), mm_tn(sb, dy, name="cd_dwout_sb")], axis=0)
    dq_c, dk_c, dv_c, dsink = swa_bwd(p2, _lane_row(sinks), dmixed, name="swa_bwd")
    pieces = [dq_c, dk_c, dv_c] + list(sb_bwd(p2, sb_f32, dmixed, name="sb_bwd"))
    offs = np.cumsum((0,) + CD_WIDTHS)
    w_parts = [w_in[:, a:b] for a, b in zip(offs[:-1], offs[1:])]
    dh_in, dg_pre = mm_nt_normbwd(pieces, w_parts, h, g_pre, dh, name="cd_bwd_in")
    dw_in = jnp.concatenate([mm_tn(u, pc, name=f"cd_dwin_{i}") for i, pc in enumerate(pieces)], axis=1)
    return dh_in, (dg_pre, dg_post), (dw_in, dw_out), dsink[:, :8]


def _pad_heads(w, axis):
    shape = w.shape
    w = w.reshape(shape[:axis] + (shape[axis] // 64, 64) + shape[axis + 1:])
    pad = [(0, 0)] * w.ndim
    pad[axis + 1] = (0, 64)
    return jnp.pad(w, pad).reshape(shape[:axis] + (2 * shape[axis],) + shape[axis + 1:])


def _unpad_heads(w, axis):
    shape = w.shape
    w = w.reshape(shape[:axis] + (shape[axis] // 128, 128) + shape[axis + 1:])
    w = lax.slice_in_dim(w, 0, 64, axis=axis + 1)
    return w.reshape(shape[:axis] + (shape[axis] // 2,) + shape[axis + 1:])


SMALL_SHARDED = (("meta_tokens", (NMETA, LANES), 1), ("norm_gains", (2, 6, LANES), 2), ("ab_conv_w", (1, 4, 192), 2))
SMALL_REPL = (("ab_a_log", (1, 4)), ("ab_dt_bias", (1, 4)), ("ab_out_norm", (1, LANES)), ("cd_sinks", (1, 8)))


def _stack_shards(g, axis):
    full = jnp.moveaxis(g, 0, axis)
    shape = full.shape
    return full.reshape(shape[:axis] + (shape[axis] * shape[axis + 1],) + shape[axis + 2:])


def _split_shards(full, axis):
    shape = full.shape
    g = full.reshape(shape[:axis] + (NDEV, shape[axis] // NDEV) + shape[axis + 1:])
    return jnp.moveaxis(g, axis, 0)


def _pad_rows8(a):
    rows = []
    for x in a:
        flat = x.reshape(x.shape[0], -1)
        n = -(-flat.shape[1] // LANES) * LANES
        rows.append(jnp.pad(flat, ((0, 0), (0, n - flat.shape[1]))).reshape(x.shape[0], n // LANES, LANES))
    cat = jnp.concatenate(rows, axis=1)
    return jnp.pad(cat, ((0, 0), (0, -cat.shape[1] % 8), (0, 0)))


def _unpad_rows8(packed, shapes):
    out, at = [], 0
    for shape in shapes:
        size = int(np.prod(shape))
        nrow = -(-size // LANES)
        blk = packed[:, at:at + nrow].reshape(packed.shape[0], -1)[:, :size]
        out.append(blk.reshape((packed.shape[0],) + tuple(shape)))
        at += nrow
    return out


def kernel(x, meta_tokens, norm_gains, ffn_w_gate, ffn_w_up, ffn_w_down, ab_w_in, ab_conv_w, ab_a_log, ab_dt_bias, ab_out_norm, ab_w_out, cd_w_in, cd_sinks, cd_w_out, loss_target, m_meta_tokens, m_norm_gains, m_ffn_w_gate, m_ffn_w_up, m_ffn_w_down, m_ab_w_in, m_ab_conv_w, m_ab_a_log, m_ab_dt_bias, m_ab_out_norm, m_ab_w_out, m_cd_w_in, m_cd_sinks, m_cd_w_out, v_meta_tokens, v_norm_gains, v_ffn_w_gate, v_ffn_w_up, v_ffn_w_down, v_ab_w_in, v_ab_conv_w, v_ab_a_log, v_ab_dt_bias, v_ab_out_norm, v_ab_w_out, v_cd_w_in, v_cd_sinks, v_cd_w_out):
    w = dict(meta_tokens=meta_tokens, norm_gains=norm_gains, ffn_w_gate=ffn_w_gate, ffn_w_up=ffn_w_up,
             ffn_w_down=ffn_w_down, ab_w_in=ab_w_in, ab_conv_w=ab_conv_w, ab_a_log=ab_a_log, ab_dt_bias=ab_dt_bias,
             ab_out_norm=ab_out_norm, ab_w_out=ab_w_out, cd_w_in=cd_w_in, cd_sinks=cd_sinks, cd_w_out=cd_w_out)
    m = dict(meta_tokens=m_meta_tokens, norm_gains=m_norm_gains, ffn_w_gate=m_ffn_w_gate, ffn_w_up=m_ffn_w_up,
             ffn_w_down=m_ffn_w_down, ab_w_in=m_ab_w_in, ab_conv_w=m_ab_conv_w, ab_a_log=m_ab_a_log,
             ab_dt_bias=m_ab_dt_bias, ab_out_norm=m_ab_out_norm, ab_w_out=m_ab_w_out, cd_w_in=m_cd_w_in,
             cd_sinks=m_cd_sinks, cd_w_out=m_cd_w_out)
    v = dict(meta_tokens=v_meta_tokens, norm_gains=v_norm_gains, ffn_w_gate=v_ffn_w_gate, ffn_w_up=v_ffn_w_up,
             ffn_w_down=v_ffn_w_down, ab_w_in=v_ab_w_in, ab_conv_w=v_ab_conv_w, ab_a_log=v_ab_a_log,
             ab_dt_bias=v_ab_dt_bias, ab_out_norm=v_ab_out_norm, ab_w_out=v_ab_w_out, cd_w_in=v_cd_w_in,
             cd_sinks=v_cd_sinks, cd_w_out=v_cd_w_out)
    order = list(w)
    S = x.shape[1]
    T = S + BLK

    fs = DFF // NDEV

    def ffn_local(i, j):
        return jnp.concatenate([ffn_w_gate[i, j].T, ffn_w_up[i, j].T, ffn_w_down[i, j]], axis=0).astype(BF16)

    wg, wu, wd = {}, {}, {}

    def ffn_gathered(gathered, ij):
        for kind, full_w in enumerate((wg, wu, wd)):
            full_w[ij] = gathered[:, kind * fs:(kind + 1) * fs].reshape(DFF, D)

    ffn00_all, abin_all, about_all = all_gather_big(
        [ffn_local(0, 0), ab_w_in[0].astype(BF16), ab_w_out[0].astype(BF16)], name="gather_first")
    ffn_gathered(ffn00_all, (0, 0))
    ab_in = jnp.pad(_stack_shards(abin_all, 1), ((0, 0), (0, AB_INP - AB_IN)))
    ab_out = about_all.reshape(D, D)
    under_conv = [ffn_local(0, 1)]
    under_gdn = [ffn_local(1, 0), ffn_local(1, 1), cd_w_in[0].astype(BF16), cd_w_out[0].astype(BF16)]
    small_src = jnp.broadcast_to(_pad_rows8([w[n][None] for n, _, _ in SMALL_SHARDED]), (NDEV, 40, LANES))
    small_all = _unpad_rows8(all_to_all_small(small_src, name="gather_small"), [s for _, s, _ in SMALL_SHARDED])
    full = {n: _stack_shards(g, ax) for (n, _, ax), g in zip(SMALL_SHARDED, small_all)}
    conv_w = full["ab_conv_w"][0]
    gains = full["norm_gains"].reshape(2, 6, 1, D)
    tables = retention_tables(T)

    h = jnp.concatenate([jnp.zeros((PAD, D), F32), full["meta_tokens"], x[0]], axis=0)
    h, s00 = ffn_fwd(h, gains[0, 0], gains[0, 1], wg[0, 0], wu[0, 0], wd[0, 0], "00")
    h, sab, later_part = mixer_ab_fwd(h, gains[0, 2], gains[0, 3], ab_in, conv_w, ab_a_log, ab_dt_bias, ab_out_norm,
                                      ab_out, tables, under_conv, under_gdn)
    ffn01_all, ffn10_all, ffn11_all, cdin_all, cdout_all = gather_forward_to_sibling(later_part, name="gather_finish")
    for ij, gathered in (((0, 1), ffn01_all), ((1, 0), ffn10_all), ((1, 1), ffn11_all)):
        ffn_gathered(gathered, ij)
    cd_in = _pad_heads(_stack_shards(cdin_all, 1), 1)
    cd_out = _pad_heads(cdout_all.reshape(D, D), 0)
    h, s01 = ffn_fwd(h, gains[0, 4], gains[0, 5], wg[0, 1], wu[0, 1], wd[0, 1], "01")
    h, s10 = ffn_fwd(h, gains[1, 0], gains[1, 1], wg[1, 0], wu[1, 0], wd[1, 0], "10")
    h, scd = mixer_cd_fwd(h, gains[1, 2], gains[1, 3], cd_in, cd_sinks, cd_out)
    h, s11 = ffn_fwd(h, gains[1, 4], gains[1, 5], wg[1, 1], wu[1, 1], wd[1, 1], "11")
    loss_tile, dh = loss_and_grad(h, loss_target[0], name="loss")
    loss = lax.psum(loss_tile[0, 0], ("x", "y", "c"))

    dgain = [[None] * 6, [None] * 6]
    dffn = {}
    dh, (dgain[1][4], dgain[1][5]), dffn[1, 1] = ffn_bwd(s11, dh, gains[1, 4], gains[1, 5], wg[1, 1], wu[1, 1], wd[1, 1], "11")
    dh, (dgain[1][2], dgain[1][3]), (dcd_in, dcd_out), dsinks = mixer_cd_bwd(scd, dh, gains[1, 2], gains[1, 3], cd_in, cd_sinks, cd_out)
    dh, (dgain[1][0], dgain[1][1]), dffn[1, 0] = ffn_bwd(s10, dh, gains[1, 0], gains[1, 1], wg[1, 0], wu[1, 0], wd[1, 0], "10")
    dh, (dgain[0][4], dgain[0][5]), dffn[0, 1] = ffn_bwd(s01, dh, gains[0, 4], gains[0, 5], wg[0, 1], wu[0, 1], wd[0, 1], "01")
    ffn_send = lambda ij: jnp.concatenate([t.astype(BF16).reshape(NDEV, fs, D) for t in dffn[ij]], axis=1)
    early = [(0, 1), (1, 0), (1, 1)]
    early_send = [ffn_send(ij) for ij in early] + [_split_shards(_unpad_heads(dcd_in, 1).astype(BF16), 1),
                                                   _unpad_heads(dcd_out, 0).astype(BF16).reshape(NDEV, D // NDEV, D)]
    early_sib = rs_exchange_sibling(early_send, name="rs_sibling_early")
    early_parts = [rs_chip_partials(g, t, name=f"rs_chip_partials_early_{i}")
                   for i, (g, t) in enumerate(zip(early_send, early_sib))]
    dh, (dgain[0][2], dgain[0][3]), (dab_in, dab_out), (dconv, dalog, ddtb, donorm), early_got = mixer_ab_bwd(
        sab, dh, gains[0, 2], gains[0, 3], ab_in, conv_w, ab_a_log, ab_dt_bias, ab_out_norm, ab_out, tables, early_parts)
    dh, (dgain[0][0], dgain[0][1]), dffn[0, 0] = ffn_bwd(s00, dh, gains[0, 0], gains[0, 1], wg[0, 0], wu[0, 0], wd[0, 0], "00")
    grad_x = dh[BLK:][None]

    gfull = dict(meta_tokens=dh[PAD:BLK], norm_gains=jnp.stack([jnp.concatenate(r, axis=0) for r in dgain]),
                 ab_conv_w=dconv[None])
    early_g = [rs_final_sum(p, t, name=f"rs_final_sum_early_{i}") for i, (p, t) in enumerate(zip(early_parts, early_got))]
    ffn00_g, abin_g, about_g = reduce_scatter_big(
        [ffn_send((0, 0)), _split_shards(dab_in[:, :AB_IN].astype(BF16), 1),
         dab_out.astype(BF16).reshape(NDEV, D // NDEV, D)])
    cdin_g, cdout_g = early_g[3:]
    ffn_by = dict(zip(early, early_g[:3]))
    ffn_by[0, 0] = ffn00_g
    ffn_g = jnp.stack([jnp.stack([ffn_by[i, j].reshape(3, fs, D) for j in range(2)]) for i in range(2)])
    ffn_g = jnp.moveaxis(ffn_g, 2, 0)
    grads = dict(ffn_w_gate=jnp.swapaxes(ffn_g[0], 2, 3), ffn_w_up=jnp.swapaxes(ffn_g[1], 2, 3), ffn_w_down=ffn_g[2],
                 ab_w_in=abin_g[None], ab_w_out=about_g[None], cd_w_in=cdin_g[None], cd_w_out=cdout_g[None])
    repl = [jnp.broadcast_to(t[None], (NDEV,) + t.shape) for t in (dalog, ddtb, donorm, dsinks)]
    ssend = _pad_rows8([_split_shards(gfull[n], ax) for n, _, ax in SMALL_SHARDED] + repl)
    ssum = sum_slots(all_to_all_small(ssend, name="exchange_small_grads"), name="sum_small_grads")[None]
    small = _unpad_rows8(ssum, [s for _, s, _ in SMALL_SHARDED] + [s for _, s in SMALL_REPL])
    grads.update({n: g[0] for n, g in zip([n for n, _, _ in SMALL_SHARDED] + [n for n, _ in SMALL_REPL], small)})

    delta, new_m, new_v = {}, {}, {}
    for n in order:
        shape = w[n].shape
        view = (-1, shape[-1])
        d_, m_, v_ = adamw(w[n].reshape(view), grads[n].reshape(view), m[n].reshape(view), v[n].reshape(view),
                           name=f"adamw_{n}")
        delta[n], new_m[n], new_v[n] = d_.reshape(shape), m_.reshape(shape), v_.reshape(shape)
    return (loss, grad_x, *[grads[n] for n in order], *[delta[n] for n in order], *[new_m[n] for n in order],
            *[new_v[n] for n in order])
```
